```python
import jax, jax.numpy as jnp
from jax import lax
import numpy as np

D_MODEL = 1024
BATCH = 8
SEQ = 4096
DEPTH = 1

FFN_HIDDEN = 2816
FFN_RESIDUAL_WEIGHT = 0.5

MLA_HEADS = 8
MLA_NOPE_DIM = 64
MLA_ROPE_DIM = 32
MLA_V_DIM = 64
Q_LORA_RANK = 192
KV_LORA_RANK = 128
ROPE_THETA = 10000.0
MAX_POS_OFFSET = 1024

FOX_HEADS = 8
FOX_HEAD_DIM = 64
FOX_WIDTH = FOX_HEADS * FOX_HEAD_DIM
FOX_BF_MIN = 1.0
FOX_BF_MAX = 6.0

MLA_WIDTH = MLA_HEADS * MLA_V_DIM
Q_BLOCK = 128
RMS_EPS = 1e-6

IN_SPLIT_SIZES = (Q_LORA_RANK, KV_LORA_RANK, MLA_ROPE_DIM,
                  FOX_WIDTH, FOX_WIDTH, FOX_WIDTH, FOX_HEADS,
                  D_MODEL, D_MODEL)
IN_WIDTH = sum(IN_SPLIT_SIZES)

kernel_name = "hybrid_mla_fox_macaron_gated"


def rms_norm(x, gain):
    xf = x.astype(jnp.float32)
    y = xf * lax.rsqrt(jnp.mean(xf * xf, axis=-1, keepdims=True) + RMS_EPS)
    return (y * gain.astype(jnp.float32)).astype(x.dtype)


def swiglu(x, w_gate, w_up, w_down):
    return (jax.nn.silu(x @ w_gate) * (x @ w_up)) @ w_down


def rope_tables(positions):
    half = MLA_ROPE_DIM // 2
    inv_freq = ROPE_THETA ** (-jnp.arange(half, dtype=jnp.float32) / half)
    ang = positions.astype(jnp.float32)[..., None] * inv_freq
    return jnp.cos(ang), jnp.sin(ang)


def apply_rotary(x, cos, sin):
    half = x.shape[-1] // 2
    xf = x.astype(jnp.float32)
    x1, x2 = xf[..., :half], xf[..., half:]
    return jnp.concatenate([x1 * cos - x2 * sin, x2 * cos + x1 * sin], axis=-1).astype(x.dtype)


def to_heads(t):
    return t.transpose(0, 2, 1, 3)


def from_heads(t):
    b, h, s, d = t.shape
    return t.transpose(0, 2, 1, 3).reshape(b, s, h * d)


def blocked_causal_attention(q, k, v, log_decay_cum=None):
    b, h, s, dk = q.shape
    dv = v.shape[-1]
    n_blocks = s // Q_BLOCK
    scale = dk ** -0.5
    k_pos = jnp.arange(s)

    def one_block(i):
        start = i * Q_BLOCK
        q_blk = lax.dynamic_slice_in_dim(q, start, Q_BLOCK, axis=2)
        logits = jnp.einsum('bhqd,bhkd->bhqk', q_blk, k,
                            preferred_element_type=jnp.float32) * scale
        if log_decay_cum is not None:
            c_q = lax.dynamic_slice_in_dim(log_decay_cum, start, Q_BLOCK, axis=2)
            logits = logits + (c_q[..., :, None] - log_decay_cum[..., None, :])
        q_pos = start + jnp.arange(Q_BLOCK)
        logits = jnp.where(k_pos[None, :] <= q_pos[:, None], logits, -jnp.inf)
        probs = jax.nn.softmax(logits, axis=-1)
        return jnp.einsum('bhqk,bhkd->bhqd', probs.astype(v.dtype), v)

    out = lax.map(one_block, jnp.arange(n_blocks))
    return out.transpose(1, 2, 0, 3, 4).reshape(b, h, s, dv)


def mla_mixer(q_lat, kv_lat, k_rope, cos, sin, q_lat_norm, w_qb, kv_lat_norm, w_kvb,
              q_nope_gain, q_rope_gain, k_nope_gain, k_rope_gain):
    b, s, _ = q_lat.shape
    q = (rms_norm(q_lat, q_lat_norm) @ w_qb).reshape(b, s, MLA_HEADS, MLA_NOPE_DIM + MLA_ROPE_DIM)
    q_nope = rms_norm(q[..., :MLA_NOPE_DIM], q_nope_gain)
    q_rope = apply_rotary(rms_norm(q[..., MLA_NOPE_DIM:], q_rope_gain),
                          cos[:, :, None, :], sin[:, :, None, :])
    kv = (rms_norm(kv_lat, kv_lat_norm) @ w_kvb).reshape(b, s, MLA_HEADS, MLA_NOPE_DIM + MLA_V_DIM)
    k_nope = rms_norm(kv[..., :MLA_NOPE_DIM], k_nope_gain)
    v = kv[..., MLA_NOPE_DIM:]
    k_r = apply_rotary(rms_norm(k_rope, k_rope_gain), cos, sin)
    k_r = jnp.broadcast_to(k_r[:, :, None, :], (b, s, MLA_HEADS, MLA_ROPE_DIM))
    q_full = jnp.concatenate([q_nope, q_rope], axis=-1)
    k_full = jnp.concatenate([k_nope, k_r], axis=-1)
    out = blocked_causal_attention(to_heads(q_full), to_heads(k_full), to_heads(v))
    return from_heads(out)


def fox_mixer(fq, fk, fv, f_logit, q_gain, k_gain, b_f):
    b, s, _ = fq.shape
    q = rms_norm(fq.reshape(b, s, FOX_HEADS, FOX_HEAD_DIM), q_gain)
    k = rms_norm(fk.reshape(b, s, FOX_HEADS, FOX_HEAD_DIM), k_gain)
    v = fv.reshape(b, s, FOX_HEADS, FOX_HEAD_DIM)
    log_f = jax.nn.log_sigmoid((f_logit + b_f).astype(jnp.float32))
    c = jnp.cumsum(log_f, axis=1).transpose(0, 2, 1)
    out = blocked_causal_attention(to_heads(q), to_heads(k), to_heads(v), c)
    return from_heads(out)


def _fwd_setup_inputs(seed: int = 0) -> dict:
    key = jax.random.key(seed)
    ks = iter(jax.random.split(key, 40))
    f32 = jnp.float32

    def w(shape, fan_in):
        return jax.random.normal(next(ks), (DEPTH,) + shape, f32) * (fan_in ** -0.5)

    def gain(*shape):
        return 1.0 + 0.02 * jax.random.normal(next(ks), (DEPTH,) + shape, f32)

    x = jax.random.normal(next(ks), (BATCH, SEQ, D_MODEL), f32)
    offsets = jax.random.randint(next(ks), (BATCH, 1), 0, MAX_POS_OFFSET, dtype=jnp.int32)
    positions = (offsets + jnp.arange(SEQ, dtype=jnp.int32)[None, :]).astype(jnp.int32)
    return {
        "x": x,
        "positions": positions,
        "ffn1_norm": gain(D_MODEL),
        "ffn1_w_gate": w((D_MODEL, FFN_HIDDEN), D_MODEL),
        "ffn1_w_up": w((D_MODEL, FFN_HIDDEN), D_MODEL),
        "ffn1_w_down": w((FFN_HIDDEN, D_MODEL), FFN_HIDDEN),
        "mix_norm": gain(D_MODEL),
        "w_in": w((D_MODEL, IN_WIDTH), D_MODEL),
        "mla_q_lat_norm": gain(Q_LORA_RANK),
        "mla_w_qb": w((Q_LORA_RANK, MLA_HEADS * (MLA_NOPE_DIM + MLA_ROPE_DIM)), Q_LORA_RANK),
        "mla_kv_lat_norm": gain(KV_LORA_RANK),
        "mla_w_kvb": w((KV_LORA_RANK, MLA_HEADS * (MLA_NOPE_DIM + MLA_V_DIM)), KV_LORA_RANK),
        "mla_q_nope_gain": gain(MLA_NOPE_DIM),
        "mla_q_rope_gain": gain(MLA_ROPE_DIM),
        "mla_k_nope_gain": gain(MLA_NOPE_DIM),
        "mla_k_rope_gain": gain(MLA_ROPE_DIM),
        "fox_q_gain": gain(FOX_HEAD_DIM),
        "fox_k_gain": gain(FOX_HEAD_DIM),
        "fox_b_f": jax.random.uniform(next(ks), (DEPTH, FOX_HEADS), f32, FOX_BF_MIN, FOX_BF_MAX),
        "w_branch_mla": w((MLA_WIDTH, D_MODEL), MLA_WIDTH),
        "w_branch_fox": w((FOX_WIDTH, D_MODEL), FOX_WIDTH),
        "b_gate": 0.02 * jax.random.normal(next(ks), (DEPTH, 2, D_MODEL), f32),
        "w_o": w((D_MODEL, D_MODEL), D_MODEL),
        "ffn2_norm": gain(D_MODEL),
        "ffn2_w_gate": w((D_MODEL, FFN_HIDDEN), D_MODEL),
        "ffn2_w_up": w((D_MODEL, FFN_HIDDEN), D_MODEL),
        "ffn2_w_down": w((FFN_HIDDEN, D_MODEL), FFN_HIDDEN),
    }


def _fwd_reference(x, positions, ffn1_norm, ffn1_w_gate, ffn1_w_up, ffn1_w_down, mix_norm, w_in,
              mla_q_lat_norm, mla_w_qb, mla_kv_lat_norm, mla_w_kvb, mla_q_nope_gain,
              mla_q_rope_gain, mla_k_nope_gain, mla_k_rope_gain, fox_q_gain, fox_k_gain,
              fox_b_f, w_branch_mla, w_branch_fox, b_gate, w_o, ffn2_norm, ffn2_w_gate,
              ffn2_w_up, ffn2_w_down):
    split_points = [int(p) for p in np.cumsum(IN_SPLIT_SIZES)[:-1]]
    cos, sin = rope_tables(positions)
    for l in range(DEPTH):
        x = x + FFN_RESIDUAL_WEIGHT * swiglu(rms_norm(x, ffn1_norm[l]),
                                             ffn1_w_gate[l], ffn1_w_up[l], ffn1_w_down[l])
        h = rms_norm(x, mix_norm[l])
        proj = h @ w_in[l]
        (q_lat, kv_lat, k_rope, fq, fk, fv, f_logit,
         g_mla, g_fox) = jnp.split(proj, split_points, axis=-1)
        y_mla = mla_mixer(q_lat, kv_lat, k_rope, cos, sin, mla_q_lat_norm[l], mla_w_qb[l],
                          mla_kv_lat_norm[l], mla_w_kvb[l], mla_q_nope_gain[l],
                          mla_q_rope_gain[l], mla_k_nope_gain[l], mla_k_rope_gain[l])
        y_fox = fox_mixer(fq, fk, fv, f_logit, fox_q_gain[l], fox_k_gain[l], fox_b_f[l])
        mixed = (jax.nn.sigmoid(g_mla + b_gate[l, 0]) * (y_mla @ w_branch_mla[l])
                 + jax.nn.sigmoid(g_fox + b_gate[l, 1]) * (y_fox @ w_branch_fox[l]))
        x = x + mixed @ w_o[l]
        x = x + FFN_RESIDUAL_WEIGHT * swiglu(rms_norm(x, ffn2_norm[l]),
                                             ffn2_w_gate[l], ffn2_w_up[l], ffn2_w_down[l])
    return x


import jax as _jax
import jax.numpy as _jnp

TWIN_FORMAT = 'train_step'
FWD_PARAMS = ['x', 'positions', 'ffn1_norm', 'ffn1_w_gate', 'ffn1_w_up', 'ffn1_w_down', 'mix_norm', 'w_in', 'mla_q_lat_norm', 'mla_w_qb', 'mla_kv_lat_norm', 'mla_w_kvb', 'mla_q_nope_gain', 'mla_q_rope_gain', 'mla_k_nope_gain', 'mla_k_rope_gain', 'fox_q_gain', 'fox_k_gain', 'fox_b_f', 'w_branch_mla', 'w_branch_fox', 'b_gate', 'w_o', 'ffn2_norm', 'ffn2_w_gate', 'ffn2_w_up', 'ffn2_w_down']
TWIN_WEIGHTS = ['ffn1_norm', 'ffn1_w_gate', 'ffn1_w_up', 'ffn1_w_down', 'mix_norm', 'w_in', 'mla_q_lat_norm', 'mla_w_qb', 'mla_kv_lat_norm', 'mla_w_kvb', 'mla_q_nope_gain', 'mla_q_rope_gain', 'mla_k_nope_gain', 'mla_k_rope_gain', 'fox_q_gain', 'fox_k_gain', 'fox_b_f', 'w_branch_mla', 'w_branch_fox', 'b_gate', 'w_o', 'ffn2_norm', 'ffn2_w_gate', 'ffn2_w_up', 'ffn2_w_down']
TWIN_DIFF_INPUT = 'x'
TWIN_INPUTS = ['x', 'positions', 'ffn1_norm', 'ffn1_w_gate', 'ffn1_w_up', 'ffn1_w_down', 'mix_norm', 'w_in', 'mla_q_lat_norm', 'mla_w_qb', 'mla_kv_lat_norm', 'mla_w_kvb', 'mla_q_nope_gain', 'mla_q_rope_gain', 'mla_k_nope_gain', 'mla_k_rope_gain', 'fox_q_gain', 'fox_k_gain', 'fox_b_f', 'w_branch_mla', 'w_branch_fox', 'b_gate', 'w_o', 'ffn2_norm', 'ffn2_w_gate', 'ffn2_w_up', 'ffn2_w_down', 'loss_target', 'm_ffn1_norm', 'm_ffn1_w_gate', 'm_ffn1_w_up', 'm_ffn1_w_down', 'm_mix_norm', 'm_w_in', 'm_mla_q_lat_norm', 'm_mla_w_qb', 'm_mla_kv_lat_norm', 'm_mla_w_kvb', 'm_mla_q_nope_gain', 'm_mla_q_rope_gain', 'm_mla_k_nope_gain', 'm_mla_k_rope_gain', 'm_fox_q_gain', 'm_fox_k_gain', 'm_fox_b_f', 'm_w_branch_mla', 'm_w_branch_fox', 'm_b_gate', 'm_w_o', 'm_ffn2_norm', 'm_ffn2_w_gate', 'm_ffn2_w_up', 'm_ffn2_w_down', 'v_ffn1_norm', 'v_ffn1_w_gate', 'v_ffn1_w_up', 'v_ffn1_w_down', 'v_mix_norm', 'v_w_in', 'v_mla_q_lat_norm', 'v_mla_w_qb', 'v_mla_kv_lat_norm', 'v_mla_w_kvb', 'v_mla_q_nope_gain', 'v_mla_q_rope_gain', 'v_mla_k_nope_gain', 'v_mla_k_rope_gain', 'v_fox_q_gain', 'v_fox_k_gain', 'v_fox_b_f', 'v_w_branch_mla', 'v_w_branch_fox', 'v_b_gate', 'v_w_o', 'v_ffn2_norm', 'v_ffn2_w_gate', 'v_ffn2_w_up', 'v_ffn2_w_down']
TWIN_OUTPUTS = ['loss', 'grad_x', 'grad_ffn1_norm', 'grad_ffn1_w_gate', 'grad_ffn1_w_up', 'grad_ffn1_w_down', 'grad_mix_norm', 'grad_w_in', 'grad_mla_q_lat_norm', 'grad_mla_w_qb', 'grad_mla_kv_lat_norm', 'grad_mla_w_kvb', 'grad_mla_q_nope_gain', 'grad_mla_q_rope_gain', 'grad_mla_k_nope_gain', 'grad_mla_k_rope_gain', 'grad_fox_q_gain', 'grad_fox_k_gain', 'grad_fox_b_f', 'grad_w_branch_mla', 'grad_w_branch_fox', 'grad_b_gate', 'grad_w_o', 'grad_ffn2_norm', 'grad_ffn2_w_gate', 'grad_ffn2_w_up', 'grad_ffn2_w_down', 'delta_ffn1_norm', 'delta_ffn1_w_gate', 'delta_ffn1_w_up', 'delta_ffn1_w_down', 'delta_mix_norm', 'delta_w_in', 'delta_mla_q_lat_norm', 'delta_mla_w_qb', 'delta_mla_kv_lat_norm', 'delta_mla_w_kvb', 'delta_mla_q_nope_gain', 'delta_mla_q_rope_gain', 'delta_mla_k_nope_gain', 'delta_mla_k_rope_gain', 'delta_fox_q_gain', 'delta_fox_k_gain', 'delta_fox_b_f', 'delta_w_branch_mla', 'delta_w_branch_fox', 'delta_b_gate', 'delta_w_o', 'delta_ffn2_norm', 'delta_ffn2_w_gate', 'delta_ffn2_w_up', 'delta_ffn2_w_down', 'new_m_ffn1_norm', 'new_m_ffn1_w_gate', 'new_m_ffn1_w_up', 'new_m_ffn1_w_down', 'new_m_mix_norm', 'new_m_w_in', 'new_m_mla_q_lat_norm', 'new_m_mla_w_qb', 'new_m_mla_kv_lat_norm', 'new_m_mla_w_kvb', 'new_m_mla_q_nope_gain', 'new_m_mla_q_rope_gain', 'new_m_mla_k_nope_gain', 'new_m_mla_k_rope_gain', 'new_m_fox_q_gain', 'new_m_fox_k_gain', 'new_m_fox_b_f', 'new_m_w_branch_mla', 'new_m_w_branch_fox', 'new_m_b_gate', 'new_m_w_o', 'new_m_ffn2_norm', 'new_m_ffn2_w_gate', 'new_m_ffn2_w_up', 'new_m_ffn2_w_down', 'new_v_ffn1_norm', 'new_v_ffn1_w_gate', 'new_v_ffn1_w_up', 'new_v_ffn1_w_down', 'new_v_mix_norm', 'new_v_w_in', 'new_v_mla_q_lat_norm', 'new_v_mla_w_qb', 'new_v_mla_kv_lat_norm', 'new_v_mla_w_kvb', 'new_v_mla_q_nope_gain', 'new_v_mla_q_rope_gain', 'new_v_mla_k_nope_gain', 'new_v_mla_k_rope_gain', 'new_v_fox_q_gain', 'new_v_fox_k_gain', 'new_v_fox_b_f', 'new_v_w_branch_mla', 'new_v_w_branch_fox', 'new_v_b_gate', 'new_v_w_o', 'new_v_ffn2_norm', 'new_v_ffn2_w_gate', 'new_v_ffn2_w_up', 'new_v_ffn2_w_down']
TWIN_LEAF_KINDS = {'loss': 'loss', 'grad_x': 'grad_x', 'grad_ffn1_norm': 'grad_w', 'grad_ffn1_w_gate': 'grad_w', 'grad_ffn1_w_up': 'grad_w', 'grad_ffn1_w_down': 'grad_w', 'grad_mix_norm': 'grad_w', 'grad_w_in': 'grad_w', 'grad_mla_q_lat_norm': 'grad_w', 'grad_mla_w_qb': 'grad_w', 'grad_mla_kv_lat_norm': 'grad_w', 'grad_mla_w_kvb': 'grad_w', 'grad_mla_q_nope_gain': 'grad_w', 'grad_mla_q_rope_gain': 'grad_w', 'grad_mla_k_nope_gain': 'grad_w', 'grad_mla_k_rope_gain': 'grad_w', 'grad_fox_q_gain': 'grad_w', 'grad_fox_k_gain': 'grad_w', 'grad_fox_b_f': 'grad_w', 'grad_w_branch_mla': 'grad_w', 'grad_w_branch_fox': 'grad_w', 'grad_b_gate': 'grad_w', 'grad_w_o': 'grad_w', 'grad_ffn2_norm': 'grad_w', 'grad_ffn2_w_gate': 'grad_w', 'grad_ffn2_w_up': 'grad_w', 'grad_ffn2_w_down': 'grad_w', 'delta_ffn1_norm': 'delta_w', 'delta_ffn1_w_gate': 'delta_w', 'delta_ffn1_w_up': 'delta_w', 'delta_ffn1_w_down': 'delta_w', 'delta_mix_norm': 'delta_w', 'delta_w_in': 'delta_w', 'delta_mla_q_lat_norm': 'delta_w', 'delta_mla_w_qb': 'delta_w', 'delta_mla_kv_lat_norm': 'delta_w', 'delta_mla_w_kvb': 'delta_w', 'delta_mla_q_nope_gain': 'delta_w', 'delta_mla_q_rope_gain': 'delta_w', 'delta_mla_k_nope_gain': 'delta_w', 'delta_mla_k_rope_gain': 'delta_w', 'delta_fox_q_gain': 'delta_w', 'delta_fox_k_gain': 'delta_w', 'delta_fox_b_f': 'delta_w', 'delta_w_branch_mla': 'delta_w', 'delta_w_branch_fox': 'delta_w', 'delta_b_gate': 'delta_w', 'delta_w_o': 'delta_w', 'delta_ffn2_norm': 'delta_w', 'delta_ffn2_w_gate': 'delta_w', 'delta_ffn2_w_up': 'delta_w', 'delta_ffn2_w_down': 'delta_w', 'new_m_ffn1_norm': 'new_m', 'new_m_ffn1_w_gate': 'new_m', 'new_m_ffn1_w_up': 'new_m', 'new_m_ffn1_w_down': 'new_m', 'new_m_mix_norm': 'new_m', 'new_m_w_in': 'new_m', 'new_m_mla_q_lat_norm': 'new_m', 'new_m_mla_w_qb': 'new_m', 'new_m_mla_kv_lat_norm': 'new_m', 'new_m_mla_w_kvb': 'new_m', 'new_m_mla_q_nope_gain': 'new_m', 'new_m_mla_q_rope_gain': 'new_m', 'new_m_mla_k_nope_gain': 'new_m', 'new_m_mla_k_rope_gain': 'new_m', 'new_m_fox_q_gain': 'new_m', 'new_m_fox_k_gain': 'new_m', 'new_m_fox_b_f': 'new_m', 'new_m_w_branch_mla': 'new_m', 'new_m_w_branch_fox': 'new_m', 'new_m_b_gate': 'new_m', 'new_m_w_o': 'new_m', 'new_m_ffn2_norm': 'new_m', 'new_m_ffn2_w_gate': 'new_m', 'new_m_ffn2_w_up': 'new_m', 'new_m_ffn2_w_down': 'new_m', 'new_v_ffn1_norm': 'new_v', 'new_v_ffn1_w_gate': 'new_v', 'new_v_ffn1_w_up': 'new_v', 'new_v_ffn1_w_down': 'new_v', 'new_v_mix_norm': 'new_v', 'new_v_w_in': 'new_v', 'new_v_mla_q_lat_norm': 'new_v', 'new_v_mla_w_qb': 'new_v', 'new_v_mla_kv_lat_norm': 'new_v', 'new_v_mla_w_kvb': 'new_v', 'new_v_mla_q_nope_gain': 'new_v', 'new_v_mla_q_rope_gain': 'new_v', 'new_v_mla_k_nope_gain': 'new_v', 'new_v_mla_k_rope_gain': 'new_v', 'new_v_fox_q_gain': 'new_v', 'new_v_fox_k_gain': 'new_v', 'new_v_fox_b_f': 'new_v', 'new_v_w_branch_mla': 'new_v', 'new_v_w_branch_fox': 'new_v', 'new_v_b_gate': 'new_v', 'new_v_w_o': 'new_v', 'new_v_ffn2_norm': 'new_v', 'new_v_ffn2_w_gate': 'new_v', 'new_v_ffn2_w_up': 'new_v', 'new_v_ffn2_w_down': 'new_v'}


def _forward(args):
    return _fwd_reference(*[args[k] for k in FWD_PARAMS])


def _output_shape():
    def fwd():
        inp = _fwd_setup_inputs(0)
        return _fwd_reference(*[inp[k] for k in FWD_PARAMS])
    out = _jax.eval_shape(fwd)
    return out.shape, out.dtype

N_MICROBATCH = 1
ADAM_LR = 0.001
ADAM_B1 = 0.9
ADAM_B2 = 0.999
ADAM_EPS = 1e-08
ADAM_WD = 0.01
ADAM_STEP = 10
PER_EXAMPLE_BATCH_AXIS = {'x': 0, 'positions': 0, 'loss_target': 0}
SHARED_INPUTS = []
_WEIGHT_DTYPES = {'ffn1_norm': _jnp.float32, 'ffn1_w_gate': _jnp.float32, 'ffn1_w_up': _jnp.float32, 'ffn1_w_down': _jnp.float32, 'mix_norm': _jnp.float32, 'w_in': _jnp.float32, 'mla_q_lat_norm': _jnp.float32, 'mla_w_qb': _jnp.float32, 'mla_kv_lat_norm': _jnp.float32, 'mla_w_kvb': _jnp.float32, 'mla_q_nope_gain': _jnp.float32, 'mla_q_rope_gain': _jnp.float32, 'mla_k_nope_gain': _jnp.float32, 'mla_k_rope_gain': _jnp.float32, 'fox_q_gain': _jnp.float32, 'fox_k_gain': _jnp.float32, 'fox_b_f': _jnp.float32, 'w_branch_mla': _jnp.float32, 'w_branch_fox': _jnp.float32, 'b_gate': _jnp.float32, 'w_o': _jnp.float32, 'ffn2_norm': _jnp.float32, 'ffn2_w_gate': _jnp.float32, 'ffn2_w_up': _jnp.float32, 'ffn2_w_down': _jnp.float32}
MOMENT_SCALE = {'ffn1_norm': 6.146145e+00, 'ffn1_w_gate': 6.280445e-02, 'ffn1_w_up': 7.427828e-02, 'ffn1_w_down': 1.212001e-01, 'mix_norm': 7.525596e-01, 'w_in': 5.825408e-02, 'mla_q_lat_norm': 7.593623e-02, 'mla_w_qb': 3.987364e-02, 'mla_kv_lat_norm': 9.700925e-01, 'mla_w_kvb': 5.688005e-02, 'mla_q_nope_gain': 9.715167e-01, 'mla_q_rope_gain': 6.422280e-01, 'mla_k_nope_gain': 9.724830e-01, 'mla_k_rope_gain': 6.432580e-01, 'fox_q_gain': 4.001461e+00, 'fox_k_gain': 4.000393e+00, 'fox_b_f': 3.279925e+01, 'w_branch_mla': 4.273303e-02, 'w_branch_fox': 6.673957e-02, 'b_gate': 1.274021e-01, 'w_o': 6.608971e-02, 'ffn2_norm': 6.169401e+00, 'ffn2_w_gate': 5.800662e-02, 'ffn2_w_up': 7.255310e-02, 'ffn2_w_down': 1.176373e-01}


def _to_microbatches(a, axis):
    t = _jnp.moveaxis(a, axis, 0)
    t = t.reshape((N_MICROBATCH, t.shape[0] // N_MICROBATCH) + t.shape[1:])
    return _jnp.moveaxis(t, 1, axis + 1)


def setup_inputs(seed: int = 0) -> dict:
    inp = _fwd_setup_inputs(seed)
    key = _jax.random.fold_in(_jax.random.key(seed), 7919)
    shape, _ = _output_shape()
    out = dict(inp)
    out["loss_target"] = _jax.random.normal(_jax.random.fold_in(key, 0), shape, _jnp.float32)
    for i, name in enumerate(TWIN_WEIGHTS):
        w = inp[name].astype(_jnp.float32)
        if MOMENT_SCALE is None:
            s = _jnp.sqrt(_jnp.mean(_jnp.square(w)) + 1e-30)
        else:
            s = MOMENT_SCALE[name]
        km, kv = _jax.random.split(_jax.random.fold_in(key, i + 1))
        out[name] = w
        out["m_" + name] = s * _jax.random.normal(km, w.shape, _jnp.float32)
        out["v_" + name] = (s * s) * _jax.random.uniform(kv, w.shape, _jnp.float32, 0.5, 1.5)
    if N_MICROBATCH > 1:
        for name, axis in PER_EXAMPLE_BATCH_AXIS.items():
            out[name] = _to_microbatches(out[name], axis)
    return {'x': out['x'], 'positions': out['positions'], 'ffn1_norm': out['ffn1_norm'], 'ffn1_w_gate': out['ffn1_w_gate'], 'ffn1_w_up': out['ffn1_w_up'], 'ffn1_w_down': out['ffn1_w_down'], 'mix_norm': out['mix_norm'], 'w_in': out['w_in'], 'mla_q_lat_norm': out['mla_q_lat_norm'], 'mla_w_qb': out['mla_w_qb'], 'mla_kv_lat_norm': out['mla_kv_lat_norm'], 'mla_w_kvb': out['mla_w_kvb'], 'mla_q_nope_gain': out['mla_q_nope_gain'], 'mla_q_rope_gain': out['mla_q_rope_gain'], 'mla_k_nope_gain': out['mla_k_nope_gain'], 'mla_k_rope_gain': out['mla_k_rope_gain'], 'fox_q_gain': out['fox_q_gain'], 'fox_k_gain': out['fox_k_gain'], 'fox_b_f': out['fox_b_f'], 'w_branch_mla': out['w_branch_mla'], 'w_branch_fox': out['w_branch_fox'], 'b_gate': out['b_gate'], 'w_o': out['w_o'], 'ffn2_norm': out['ffn2_norm'], 'ffn2_w_gate': out['ffn2_w_gate'], 'ffn2_w_up': out['ffn2_w_up'], 'ffn2_w_down': out['ffn2_w_down'], 'loss_target': out['loss_target'], 'm_ffn1_norm': out['m_ffn1_norm'], 'm_ffn1_w_gate': out['m_ffn1_w_gate'], 'm_ffn1_w_up': out['m_ffn1_w_up'], 'm_ffn1_w_down': out['m_ffn1_w_down'], 'm_mix_norm': out['m_mix_norm'], 'm_w_in': out['m_w_in'], 'm_mla_q_lat_norm': out['m_mla_q_lat_norm'], 'm_mla_w_qb': out['m_mla_w_qb'], 'm_mla_kv_lat_norm': out['m_mla_kv_lat_norm'], 'm_mla_w_kvb': out['m_mla_w_kvb'], 'm_mla_q_nope_gain': out['m_mla_q_nope_gain'], 'm_mla_q_rope_gain': out['m_mla_q_rope_gain'], 'm_mla_k_nope_gain': out['m_mla_k_nope_gain'], 'm_mla_k_rope_gain': out['m_mla_k_rope_gain'], 'm_fox_q_gain': out['m_fox_q_gain'], 'm_fox_k_gain': out['m_fox_k_gain'], 'm_fox_b_f': out['m_fox_b_f'], 'm_w_branch_mla': out['m_w_branch_mla'], 'm_w_branch_fox': out['m_w_branch_fox'], 'm_b_gate': out['m_b_gate'], 'm_w_o': out['m_w_o'], 'm_ffn2_norm': out['m_ffn2_norm'], 'm_ffn2_w_gate': out['m_ffn2_w_gate'], 'm_ffn2_w_up': out['m_ffn2_w_up'], 'm_ffn2_w_down': out['m_ffn2_w_down'], 'v_ffn1_norm': out['v_ffn1_norm'], 'v_ffn1_w_gate': out['v_ffn1_w_gate'], 'v_ffn1_w_up': out['v_ffn1_w_up'], 'v_ffn1_w_down': out['v_ffn1_w_down'], 'v_mix_norm': out['v_mix_norm'], 'v_w_in': out['v_w_in'], 'v_mla_q_lat_norm': out['v_mla_q_lat_norm'], 'v_mla_w_qb': out['v_mla_w_qb'], 'v_mla_kv_lat_norm': out['v_mla_kv_lat_norm'], 'v_mla_w_kvb': out['v_mla_w_kvb'], 'v_mla_q_nope_gain': out['v_mla_q_nope_gain'], 'v_mla_q_rope_gain': out['v_mla_q_rope_gain'], 'v_mla_k_nope_gain': out['v_mla_k_nope_gain'], 'v_mla_k_rope_gain': out['v_mla_k_rope_gain'], 'v_fox_q_gain': out['v_fox_q_gain'], 'v_fox_k_gain': out['v_fox_k_gain'], 'v_fox_b_f': out['v_fox_b_f'], 'v_w_branch_mla': out['v_w_branch_mla'], 'v_w_branch_fox': out['v_w_branch_fox'], 'v_b_gate': out['v_b_gate'], 'v_w_o': out['v_w_o'], 'v_ffn2_norm': out['v_ffn2_norm'], 'v_ffn2_w_gate': out['v_ffn2_w_gate'], 'v_ffn2_w_up': out['v_ffn2_w_up'], 'v_ffn2_w_down': out['v_ffn2_w_down']}


def _loss(weights, diff, rest, loss_target):
    with _jax.named_scope("forward"):
        args = {**rest, TWIN_DIFF_INPUT: diff, **{k: w.astype(_WEIGHT_DTYPES[k]) for k, w in weights.items()}}
        y = _forward(args)
    with _jax.named_scope("loss_head"):
        err = _jnp.square(y.astype(_jnp.float32) - loss_target)
        return 0.5 * _jnp.sum(_jnp.mean(err, axis=-1)) if err.ndim else 0.5 * err


def _adamw(w, g, m, v):
    m = ADAM_B1 * m + (1.0 - ADAM_B1) * g
    v = ADAM_B2 * v + (1.0 - ADAM_B2) * _jnp.square(g)
    m_hat = m / (1.0 - ADAM_B1 ** ADAM_STEP)
    v_hat = v / (1.0 - ADAM_B2 ** ADAM_STEP)
    delta = -ADAM_LR * (m_hat / (_jnp.sqrt(v_hat) + ADAM_EPS) + ADAM_WD * w)
    return delta, m, v


def reference(x, positions, ffn1_norm, ffn1_w_gate, ffn1_w_up, ffn1_w_down, mix_norm, w_in, mla_q_lat_norm, mla_w_qb, mla_kv_lat_norm, mla_w_kvb, mla_q_nope_gain, mla_q_rope_gain, mla_k_nope_gain, mla_k_rope_gain, fox_q_gain, fox_k_gain, fox_b_f, w_branch_mla, w_branch_fox, b_gate, w_o, ffn2_norm, ffn2_w_gate, ffn2_w_up, ffn2_w_down, loss_target, m_ffn1_norm, m_ffn1_w_gate, m_ffn1_w_up, m_ffn1_w_down, m_mix_norm, m_w_in, m_mla_q_lat_norm, m_mla_w_qb, m_mla_kv_lat_norm, m_mla_w_kvb, m_mla_q_nope_gain, m_mla_q_rope_gain, m_mla_k_nope_gain, m_mla_k_rope_gain, m_fox_q_gain, m_fox_k_gain, m_fox_b_f, m_w_branch_mla, m_w_branch_fox, m_b_gate, m_w_o, m_ffn2_norm, m_ffn2_w_gate, m_ffn2_w_up, m_ffn2_w_down, v_ffn1_norm, v_ffn1_w_gate, v_ffn1_w_up, v_ffn1_w_down, v_mix_norm, v_w_in, v_mla_q_lat_norm, v_mla_w_qb, v_mla_kv_lat_norm, v_mla_w_kvb, v_mla_q_nope_gain, v_mla_q_rope_gain, v_mla_k_nope_gain, v_mla_k_rope_gain, v_fox_q_gain, v_fox_k_gain, v_fox_b_f, v_w_branch_mla, v_w_branch_fox, v_b_gate, v_w_o, v_ffn2_norm, v_ffn2_w_gate, v_ffn2_w_up, v_ffn2_w_down):
    given = dict(x=x, positions=positions, ffn1_norm=ffn1_norm, ffn1_w_gate=ffn1_w_gate, ffn1_w_up=ffn1_w_up, ffn1_w_down=ffn1_w_down, mix_norm=mix_norm, w_in=w_in, mla_q_lat_norm=mla_q_lat_norm, mla_w_qb=mla_w_qb, mla_kv_lat_norm=mla_kv_lat_norm, mla_w_kvb=mla_w_kvb, mla_q_nope_gain=mla_q_nope_gain, mla_q_rope_gain=mla_q_rope_gain, mla_k_nope_gain=mla_k_nope_gain, mla_k_rope_gain=mla_k_rope_gain, fox_q_gain=fox_q_gain, fox_k_gain=fox_k_gain, fox_b_f=fox_b_f, w_branch_mla=w_branch_mla, w_branch_fox=w_branch_fox, b_gate=b_gate, w_o=w_o, ffn2_norm=ffn2_norm, ffn2_w_gate=ffn2_w_gate, ffn2_w_up=ffn2_w_up, ffn2_w_down=ffn2_w_down, loss_target=loss_target, m_ffn1_norm=m_ffn1_norm, m_ffn1_w_gate=m_ffn1_w_gate, m_ffn1_w_up=m_ffn1_w_up, m_ffn1_w_down=m_ffn1_w_down, m_mix_norm=m_mix_norm, m_w_in=m_w_in, m_mla_q_lat_norm=m_mla_q_lat_norm, m_mla_w_qb=m_mla_w_qb, m_mla_kv_lat_norm=m_mla_kv_lat_norm, m_mla_w_kvb=m_mla_w_kvb, m_mla_q_nope_gain=m_mla_q_nope_gain, m_mla_q_rope_gain=m_mla_q_rope_gain, m_mla_k_nope_gain=m_mla_k_nope_gain, m_mla_k_rope_gain=m_mla_k_rope_gain, m_fox_q_gain=m_fox_q_gain, m_fox_k_gain=m_fox_k_gain, m_fox_b_f=m_fox_b_f, m_w_branch_mla=m_w_branch_mla, m_w_branch_fox=m_w_branch_fox, m_b_gate=m_b_gate, m_w_o=m_w_o, m_ffn2_norm=m_ffn2_norm, m_ffn2_w_gate=m_ffn2_w_gate, m_ffn2_w_up=m_ffn2_w_up, m_ffn2_w_down=m_ffn2_w_down, v_ffn1_norm=v_ffn1_norm, v_ffn1_w_gate=v_ffn1_w_gate, v_ffn1_w_up=v_ffn1_w_up, v_ffn1_w_down=v_ffn1_w_down, v_mix_norm=v_mix_norm, v_w_in=v_w_in, v_mla_q_lat_norm=v_mla_q_lat_norm, v_mla_w_qb=v_mla_w_qb, v_mla_kv_lat_norm=v_mla_kv_lat_norm, v_mla_w_kvb=v_mla_w_kvb, v_mla_q_nope_gain=v_mla_q_nope_gain, v_mla_q_rope_gain=v_mla_q_rope_gain, v_mla_k_nope_gain=v_mla_k_nope_gain, v_mla_k_rope_gain=v_mla_k_rope_gain, v_fox_q_gain=v_fox_q_gain, v_fox_k_gain=v_fox_k_gain, v_fox_b_f=v_fox_b_f, v_w_branch_mla=v_w_branch_mla, v_w_branch_fox=v_w_branch_fox, v_b_gate=v_b_gate, v_w_o=v_w_o, v_ffn2_norm=v_ffn2_norm, v_ffn2_w_gate=v_ffn2_w_gate, v_ffn2_w_up=v_ffn2_w_up, v_ffn2_w_down=v_ffn2_w_down)
    weights = {n: given[n] for n in TWIN_WEIGHTS}
    shared = {n: given[n] for n in SHARED_INPUTS}
    per_example = {n: given[n] for n in ['x', 'positions']}
    grad_fn = _jax.value_and_grad(_loss, argnums=(0, 1))

    def one_microbatch(ex, loss_target):
        ex = dict(ex)
        diff = ex.pop(TWIN_DIFF_INPUT)
        return grad_fn(weights, diff, {**shared, **ex}, loss_target)

    if N_MICROBATCH == 1:
        loss, (grad_w, grad_x) = one_microbatch(per_example, given["loss_target"])
    else:
        def body(carry, xs):
            loss_sum, grad_sum = carry
            l_k, (gw_k, gx_k) = one_microbatch(xs[0], xs[1])
            with _jax.named_scope("update"):
                return (loss_sum + l_k, _jax.tree.map(_jnp.add, grad_sum, gw_k)), gx_k

        init = (_jnp.zeros((), _jnp.float32), _jax.tree.map(_jnp.zeros_like, weights))
        (loss, grad_w), grad_x = _jax.lax.scan(body, init, (per_example, given["loss_target"]))
    with _jax.named_scope("update"):
        delta_w, new_m, new_v = {}, {}, {}
        for n in TWIN_WEIGHTS:
            delta_w[n], new_m[n], new_v[n] = _adamw(weights[n], grad_w[n], given["m_" + n], given["v_" + n])
    return (loss, grad_x, *[grad_w[n] for n in TWIN_WEIGHTS], *[delta_w[n] for n in TWIN_WEIGHTS],
            *[new_m[n] for n in TWIN_WEIGHTS], *[new_v[n] for n in TWIN_WEIGHTS])
```

```python
import functools

import jax
import jax.numpy as jnp
from jax import lax
from jax.experimental import pallas as pl
from jax.experimental.pallas import tpu as pltpu

F32 = jnp.float32
BF16 = jnp.bfloat16

D_MODEL = 1024
FFN_HIDDEN = 2816
N_HEADS = 8
HEAD_DIM = 64
ROPE_DIM = 32
ROPE_HALF = 16
Q_LORA = 192
KV_LORA = 128
ROPE_THETA = 10000.0
RMS_EPS = 1e-6
MLA_SCALE = (HEAD_DIM + ROPE_DIM) ** -0.5
FOX_SCALE = HEAD_DIM ** -0.5
MLA_QK_LANES = 128
ADAM_LR, ADAM_B1, ADAM_B2, ADAM_EPS, ADAM_WD, ADAM_STEP = 0.001, 0.9, 0.999, 1e-08, 0.01, 10
N_DEV = 8
W_IN_PIECE = 493
W_IN_PIECE_PAD = 496
LAT_W = 512
LAT_Q, LAT_KV, LAT_KR, LAT_F = 0, 256, 384, 416
MASK_VALUE = -1e30

TOK_TILE = 512
PROJ_TILE = 256
ATT_TILE = 512
FFN_HID_TILE = 1408

NT = (((1,), (1,)), ((), ()))
TN = (((0,), (0,)), ((), ()))
NN = (((1,), (0,)), ((), ()))
MESH = pl.DeviceIdType.MESH


def _dot(a, b, dims=NN):
    return lax.dot_general(a, b, dims, preferred_element_type=F32)


def _sds(shape, dtype):
    return jax.ShapeDtypeStruct(shape, dtype)


def _rms_fwd(x, g):
    r = lax.rsqrt(jnp.mean(x * x, axis=-1, keepdims=True) + RMS_EPS)
    return x * r * g, r


def _rms_bwd(dy, x, g, r):
    xn = x * r
    dyg = dy * g
    dx = r * (dyg - xn * jnp.mean(dyg * xn, axis=-1, keepdims=True))
    return dx, dy * xn


def _colsum(x):
    return jnp.sum(x, axis=0, keepdims=True)


def _full(shape):
    return pl.BlockSpec(shape, lambda *_: (0,) * len(shape))


def _tok(tm, n):
    return pl.BlockSpec((tm, n), lambda i, *_: (i, 0))


def _ffn_fwd(x, gain, wg_t, wu_t, wd, target=None):
    t = x.shape[0]
    tm = min(TOK_TILE, t)
    tf = FFN_HID_TILE
    n_t, n_f = t // tm, FFN_HIDDEN // tf
    with_loss = target is not None

    def body(*refs):
        if with_loss:
            x_ref, g_ref, wg_ref, wu_ref, wd_ref, t_ref, out_ref, a_ref, b_ref, lvec_ref, n_scr, acc = refs
        else:
            x_ref, g_ref, wg_ref, wu_ref, wd_ref, out_ref, a_ref, b_ref, n_scr, acc = refs
        i, j = pl.program_id(0), pl.program_id(1)

        @pl.when(j == 0)
        def _():
            xn, _ = _rms_fwd(x_ref[...], g_ref[...])
            n_scr[...] = xn.astype(BF16)
            acc[...] = jnp.zeros_like(acc)

        n = n_scr[...]
        a = _dot(n, wg_ref[...], NT)
        b = _dot(n, wu_ref[...], NT)
        a_ref[...] = a.astype(BF16)
        b_ref[...] = b.astype(BF16)
        h = (a * jax.nn.sigmoid(a)) * b
        acc[...] += _dot(h.astype(BF16), wd_ref[...])

        @pl.when(j == n_f - 1)
        def _():
            y = x_ref[...] + 0.5 * acc[...]
            if with_loss:
                diff = y - t_ref[...]
                out_ref[...] = diff * (1.0 / D_MODEL)
                sq = _colsum(diff * diff)

                @pl.when(i == 0)
                def _():
                    lvec_ref[...] = sq

                @pl.when(i > 0)
                def _():
                    lvec_ref[...] += sq
            else:
                out_ref[...] = y

    wspec = pl.BlockSpec((tf, D_MODEL), lambda i, j: (j, 0))
    hspec = pl.BlockSpec((tm, tf), lambda i, j: (i, j))
    in_specs = [_tok(tm, D_MODEL), _full((1, D_MODEL)), wspec, wspec, wspec]
    out_specs = [_tok(tm, D_MODEL), hspec, hspec]
    out_shape = [_sds((t, D_MODEL), F32), _sds((t, FFN_HIDDEN), BF16), _sds((t, FFN_HIDDEN), BF16)]
    args = [x, gain, wg_t, wu_t, wd]
    if with_loss:
        in_specs.append(_tok(tm, D_MODEL))
        out_specs.append(_full((1, D_MODEL)))
        out_shape.append(_sds((1, D_MODEL), F32))
        args.append(target)
    return pl.pallas_call(
        body, name="ffn_fwd_loss" if with_loss else "ffn_fwd", grid=(n_t, n_f), in_specs=in_specs, out_specs=out_specs,
        out_shape=out_shape,
        scratch_shapes=[pltpu.VMEM((tm, D_MODEL), BF16), pltpu.VMEM((tm, D_MODEL), F32)],
        compiler_params=pltpu.CompilerParams(dimension_semantics=("arbitrary", "arbitrary")),
    )(*args)


def _ffn_bwd(dy, x, gain, a, b, wg_t, wu_t, wd, name):
    t = x.shape[0]
    tm = min(PROJ_TILE, t)
    tf = FFN_HID_TILE
    n_t, n_f = t // tm, FFN_HIDDEN // tf

    def body(dy_ref, x_ref, g_ref, a_ref, b_ref, wg_ref, wu_ref, wd_ref,
             dx_ref, dg_ref, da_ref, db_ref, h_ref, n_ref, dyh_ref, acc):
        i, j = pl.program_id(0), pl.program_id(1)

        @pl.when(j == 0)
        def _():
            xn, _ = _rms_fwd(x_ref[...], g_ref[...])
            n_ref[...] = xn.astype(BF16)
            dyh_ref[...] = (0.5 * dy_ref[...]).astype(BF16)
            acc[...] = jnp.zeros_like(acc)

        dh = _dot(dyh_ref[...], wd_ref[...], NT)
        av = a_ref[...].astype(F32)
        bv = b_ref[...].astype(F32)
        s = jax.nn.sigmoid(av)
        silu = av * s
        da = (dh * bv * (s * (1.0 + av * (1.0 - s)))).astype(BF16)
        db = (dh * silu).astype(BF16)
        da_ref[...] = da
        db_ref[...] = db
        h_ref[...] = (silu * bv).astype(BF16)
        acc[...] += _dot(da, wg_ref[...]) + _dot(db, wu_ref[...])

        @pl.when(j == n_f - 1)
        def _():
            xv, g = x_ref[...], g_ref[...]
            r = lax.rsqrt(jnp.mean(xv * xv, axis=-1, keepdims=True) + RMS_EPS)
            dx, dg_rows = _rms_bwd(acc[...], xv, g, r)
            dx_ref[...] = dy_ref[...] + dx
            dg = _colsum(dg_rows)

            @pl.when(i == 0)
            def _():
                dg_ref[...] = dg

            @pl.when(i > 0)
            def _():
                dg_ref[...] += dg

    wspec = pl.BlockSpec((tf, D_MODEL), lambda i, j: (j, 0))
    hspec = pl.BlockSpec((tm, tf), lambda i, j: (i, j))
    tok = _tok(tm, D_MODEL)
    hid = _sds((t, FFN_HIDDEN), BF16)
    return pl.pallas_call(
        body, name=name, grid=(n_t, n_f),
        in_specs=[tok, tok, _full((1, D_MODEL)), hspec, hspec, wspec, wspec, wspec],
        out_specs=[tok, _full((1, D_MODEL)), hspec, hspec, hspec, tok, tok],
        out_shape=[_sds((t, D_MODEL), F32), _sds((1, D_MODEL), F32), hid, hid, hid,
                   _sds((t, D_MODEL), BF16), _sds((t, D_MODEL), BF16)],
        scratch_shapes=[pltpu.VMEM((tm, D_MODEL), F32)],
        compiler_params=pltpu.CompilerParams(dimension_semantics=("arbitrary", "arbitrary")),
    )(dy, x, gain, a, b, wg_t, wu_t, wd)


def _tn_matmul(a, b, name):
    t, m = a.shape
    n = b.shape[1]
    tk = min(TOK_TILE, t)
    n_k = t // tk

    def body(a_ref, b_ref, o_ref, acc):
        k = pl.program_id(0)
        p = _dot(a_ref[...], b_ref[...], TN)

        @pl.when(k == 0)
        def _():
            acc[...] = p

        @pl.when(k > 0)
        def _():
            acc[...] += p

        @pl.when(k == n_k - 1)
        def _():
            o_ref[...] = acc[...].astype(BF16)

    return pl.pallas_call(
        body, name=name, grid=(n_k,), in_specs=[_tok(tk, m), _tok(tk, n)], out_specs=_full((m, n)),
        out_shape=_sds((m, n), BF16), scratch_shapes=[pltpu.VMEM((m, n), F32)],
        compiler_params=pltpu.CompilerParams(dimension_semantics=("arbitrary",)),
    )(a, b)


def _proj_fwd(x, gain, wgate_t, wfox_t, wlat_t):
    t = x.shape[0]
    tm = min(PROJ_TILE, t)

    def body(x_ref, g_ref, wg_ref, wf_ref, wl_ref, og_ref, of_ref, ol_ref):
        xn, _ = _rms_fwd(x_ref[...], g_ref[...])
        n = xn.astype(BF16)
        og_ref[...] = _dot(n, wg_ref[...], NT)
        of_ref[...] = _dot(n, wf_ref[...], NT)
        ol_ref[...] = _dot(n, wl_ref[...], NT)

    return pl.pallas_call(
        body, name="proj_fwd", grid=(t // tm,),
        in_specs=[_tok(tm, D_MODEL), _full((1, D_MODEL)), _full(wgate_t.shape), _full(wfox_t.shape), _full(wlat_t.shape)],
        out_specs=[_tok(tm, 2 * D_MODEL), _tok(tm, 3 * 512), _tok(tm, LAT_W)],
        out_shape=[_sds((t, 2 * D_MODEL), F32), _sds((t, 3 * 512), F32), _sds((t, LAT_W), F32)],
        compiler_params=pltpu.CompilerParams(dimension_semantics=("arbitrary",)),
    )(x, gain, wgate_t, wfox_t, wlat_t)


def _proj_bwd(dgates, dfox, dlat, wgate_t, wfox_t, wlat_t, x, gain, dres):
    t = x.shape[0]
    tm = min(PROJ_TILE, t)

    def body(dg_ref, df_ref, dl_ref, wg_ref, wf_ref, wl_ref, x_ref, g_ref, dres_ref, dx_ref, dgain_ref, n_ref):
        i = pl.program_id(0)
        dn = _dot(dg_ref[...], wg_ref[...]) + _dot(df_ref[...], wf_ref[...]) + _dot(dl_ref[...], wl_ref[...])
        xv, g = x_ref[...], g_ref[...]
        xn, r = _rms_fwd(xv, g)
        n_ref[...] = xn.astype(BF16)
        dx, dg_rows = _rms_bwd(dn, xv, g, r)
        dx_ref[...] = dres_ref[...] + dx
        dgn = _colsum(dg_rows)

        @pl.when(i == 0)
        def _():
            dgain_ref[...] = dgn

        @pl.when(i > 0)
        def _():
            dgain_ref[...] += dgn

    tok = _tok(tm, D_MODEL)
    return pl.pallas_call(
        body, name="proj_bwd", grid=(t // tm,),
        in_specs=[_tok(tm, 2 * D_MODEL), _tok(tm, 3 * 512), _tok(tm, LAT_W), _full(wgate_t.shape), _full(wfox_t.shape),
                  _full(wlat_t.shape), tok, _full((1, D_MODEL)), tok],
        out_specs=[tok, _full((1, D_MODEL)), tok],
        out_shape=[_sds((t, D_MODEL), F32), _sds((1, D_MODEL), F32), _sds((t, D_MODEL), BF16)],
        compiler_params=pltpu.CompilerParams(dimension_semantics=("arbitrary",)),
    )(dgates, dfox, dlat, wgate_t, wfox_t, wlat_t, x, gain, dres)


def _tri(n, lower):
    r = lax.broadcasted_iota(jnp.int32, (n, n), 0)
    c = lax.broadcasted_iota(jnp.int32, (n, n), 1)
    return ((c <= r) if lower else (c >= r)).astype(F32)


def _log_sigmoid(z):
    return jnp.minimum(z, 0.0) - jnp.log1p(jnp.exp(-jnp.abs(z)))


def _rope_pair_fwd(p1, p2, g1, g2, cos, sin):
    ss = jnp.sum(p1 * p1, axis=-1, keepdims=True) + jnp.sum(p2 * p2, axis=-1, keepdims=True)
    r = lax.rsqrt(ss * (1.0 / ROPE_DIM) + RMS_EPS)
    n1, n2 = p1 * r, p2 * r
    y1, y2 = n1 * g1, n2 * g2
    return y1 * cos - y2 * sin, y2 * cos + y1 * sin, n1, n2, r


def _rope_pair_bwd(do1, do2, n1, n2, r, g1, g2, cos, sin):
    dy1 = do1 * cos + do2 * sin
    dy2 = do2 * cos - do1 * sin
    dyg1, dyg2 = dy1 * g1, dy2 * g2
    mu = (jnp.sum(dyg1 * n1, axis=-1, keepdims=True) + jnp.sum(dyg2 * n2, axis=-1, keepdims=True)) * (1.0 / ROPE_DIM)
    return r * (dyg1 - n1 * mu), r * (dyg2 - n2 * mu), dy1 * n1, dy2 * n2


def _prep_fwd(fox, lat, cos, sin, wqb_t, wkvb_t, g_qlat, g_kvlat, g_qn, g_qr, g_kn, g_kr, g_fq, g_fk, b_f):
    t = fox.shape[0]
    tm = min(PROJ_TILE, t)
    hd, lanes = HEAD_DIM, MLA_QK_LANES

    def body(fox_ref, lat_ref, cos_ref, sin_ref, wqb_ref, wkvb_ref, gql_ref, gkvl_ref, gqn_ref, gqr_ref, gkn_ref,
             gkr_ref, gfq_ref, gfk_ref, bf_ref, fq_ref, fk_ref, fv_ref, c_ref, mq_ref, mk_ref, mv_ref,
             carry, qscr, kscr):
        i = pl.program_id(0)
        cos_v, sin_v = cos_ref[...], sin_ref[...]

        for h in range(N_HEADS):
            w = slice(h * hd, (h + 1) * hd)
            qn, _ = _rms_fwd(fox_ref[:, h * hd:(h + 1) * hd], gfq_ref[...])
            fq_ref[:, w] = (qn * FOX_SCALE).astype(BF16)
            kn, _ = _rms_fwd(fox_ref[:, 512 + h * hd:512 + (h + 1) * hd], gfk_ref[...])
            fk_ref[:, w] = kn.astype(BF16)
        fv_ref[...] = fox_ref[:, 1024:1536].astype(BF16)

        @pl.when(i == 0)
        def _():
            carry[...] = jnp.zeros_like(carry)

        logf = _log_sigmoid(lat_ref[:, LAT_F:LAT_F + N_HEADS] + bf_ref[...])
        c_ref[...] = jnp.dot(_tri(tm, True), logf, precision=lax.Precision.HIGHEST, preferred_element_type=F32) + carry[...]
        carry[...] += _colsum(logf)

        qlat_n, _ = _rms_fwd(lat_ref[:, LAT_Q:LAT_Q + Q_LORA], gql_ref[...])
        qlat_b = qlat_n.astype(BF16)
        g1, g2 = gqr_ref[:, 0:ROPE_HALF], gqr_ref[:, ROPE_HALF:ROPE_DIM]
        qscr[...] = jnp.zeros_like(qscr)
        for h in range(N_HEADS):
            base = h * (hd + ROPE_DIM)
            pn = _dot(qlat_b, wqb_ref[base:base + hd, :], NT)
            p1 = _dot(qlat_b, wqb_ref[base + hd:base + hd + ROPE_HALF, :], NT)
            p2 = _dot(qlat_b, wqb_ref[base + hd + ROPE_HALF:base + hd + ROPE_DIM, :], NT)
            nope, _ = _rms_fwd(pn, gqn_ref[...])
            o1, o2, _, _, _ = _rope_pair_fwd(p1, p2, g1, g2, cos_v, sin_v)
            qscr[:, h * lanes:h * lanes + hd] = nope
            qscr[:, h * lanes + hd:h * lanes + hd + ROPE_HALF] = o1
            qscr[:, h * lanes + hd + ROPE_HALF:h * lanes + hd + ROPE_DIM] = o2
        mq_ref[...] = qscr[...].astype(BF16)

        kv_n, _ = _rms_fwd(lat_ref[:, LAT_KV:LAT_KV + KV_LORA], gkvl_ref[...])
        kv_b = kv_n.astype(BF16)
        kr1, kr2, _, _, _ = _rope_pair_fwd(lat_ref[:, LAT_KR:LAT_KR + ROPE_HALF], lat_ref[:, LAT_KR + ROPE_HALF:LAT_KR + ROPE_DIM],
                                           gkr_ref[:, 0:ROPE_HALF], gkr_ref[:, ROPE_HALF:ROPE_DIM], cos_v, sin_v)
        kscr[...] = jnp.zeros_like(kscr)
        for h in range(N_HEADS):
            pk = _dot(kv_b, wkvb_ref[h * 2 * hd:h * 2 * hd + hd, :], NT)
            kn, _ = _rms_fwd(pk, gkn_ref[...])
            kscr[:, h * lanes:h * lanes + hd] = kn
            kscr[:, h * lanes + hd:h * lanes + hd + ROPE_HALF] = kr1
            kscr[:, h * lanes + hd + ROPE_HALF:h * lanes + hd + ROPE_DIM] = kr2
            mv_ref[:, h * hd:(h + 1) * hd] = _dot(kv_b, wkvb_ref[h * 2 * hd + hd:(h + 1) * 2 * hd, :], NT).astype(BF16)
        mk_ref[...] = kscr[...].astype(BF16)

    small = [g_qlat, g_kvlat, g_qn, g_qr, g_kn, g_kr, g_fq, g_fk, b_f]
    t512 = _tok(tm, 512)
    return pl.pallas_call(
        body, name="prep_fwd", grid=(t // tm,),
        in_specs=[_tok(tm, 1536), _tok(tm, LAT_W), _tok(tm, ROPE_HALF), _tok(tm, ROPE_HALF), _full(wqb_t.shape),
                  _full(wkvb_t.shape)] + [_full(s.shape) for s in small],
        out_specs=[t512, t512, t512, _tok(tm, N_HEADS), _tok(tm, N_HEADS * lanes), _tok(tm, N_HEADS * lanes), t512],
        out_shape=[_sds((t, 512), BF16), _sds((t, 512), BF16), _sds((t, 512), BF16), _sds((t, N_HEADS), F32),
                   _sds((t, N_HEADS * lanes), BF16), _sds((t, N_HEADS * lanes), BF16), _sds((t, 512), BF16)],
        scratch_shapes=[pltpu.VMEM((1, N_HEADS), F32), pltpu.VMEM((tm, N_HEADS * lanes), F32),
                        pltpu.VMEM((tm, N_HEADS * lanes), F32)],
        compiler_params=pltpu.CompilerParams(dimension_semantics=("arbitrary",)),
    )(fox, lat, cos, sin, wqb_t, wkvb_t, *small)


def _prep_bwd(fox, lat, cos, sin, wqb_t, wkvb_t, g_qlat, g_kvlat, g_qn, g_qr, g_kn, g_kr, g_fq, g_fk, b_f,
              dfq, dfk, dfv, dc, dmq, dmk, dmv):
    t = fox.shape[0]
    tm = min(PROJ_TILE, t)
    n_t = t // tm
    hd, lanes = HEAD_DIM, MLA_QK_LANES

    def body(fox_ref, lat_ref, cos_ref, sin_ref, wqb_ref, wkvb_ref, gql_ref, gkvl_ref, gqn_ref, gqr_ref, gkn_ref,
             gkr_ref, gfq_ref, gfk_ref, bf_ref, dfq_ref, dfk_ref, dfv_ref, dc_ref, dmq_ref, dmk_ref, dmv_ref,
             dfox_ref, dlat_ref, dwqb_ref, dwkvb_ref, o_gql, o_gkvl, o_gqn, o_gqr, o_gkn, o_gkr, o_gfq, o_gfk, o_bf,
             carry, lscr, wq_acc, wkv_acc):
        i = pl.program_id(0)
        cos_v, sin_v = cos_ref[...], sin_ref[...]
        small_out = [o_gql, o_gkvl, o_gqn, o_gqr, o_gkn, o_gkr, o_gfq, o_gfk, o_bf]

        @pl.when(i == 0)
        def _():
            carry[...] = jnp.zeros_like(carry)
            wq_acc[...] = jnp.zeros_like(wq_acc)
            wkv_acc[...] = jnp.zeros_like(wkv_acc)
            for o in small_out:
                o[...] = jnp.zeros_like(o)

        d_gfq = jnp.zeros((1, hd), F32)
        d_gfk = jnp.zeros((1, hd), F32)
        for h in range(N_HEADS):
            w = slice(h * hd, (h + 1) * hd)
            xq = fox_ref[:, h * hd:(h + 1) * hd]
            rq = lax.rsqrt(jnp.mean(xq * xq, axis=-1, keepdims=True) + RMS_EPS)
            dxq, gq_rows = _rms_bwd(dfq_ref[:, w] * FOX_SCALE, xq, gfq_ref[...], rq)
            dfox_ref[:, w] = dxq.astype(BF16)
            d_gfq += _colsum(gq_rows)
            xk = fox_ref[:, 512 + h * hd:512 + (h + 1) * hd]
            rk = lax.rsqrt(jnp.mean(xk * xk, axis=-1, keepdims=True) + RMS_EPS)
            dxk, gk_rows = _rms_bwd(dfk_ref[:, w], xk, gfk_ref[...], rk)
            dfox_ref[:, 512 + h * hd:512 + (h + 1) * hd] = dxk.astype(BF16)
            d_gfk += _colsum(gk_rows)
        dfox_ref[:, 1024:1536] = dfv_ref[...].astype(BF16)
        o_gfq[...] += d_gfq
        o_gfk[...] += d_gfk

        lscr[...] = jnp.zeros_like(lscr)

        dcv = dc_ref[...]
        dlogf = jnp.dot(_tri(tm, False), dcv, precision=lax.Precision.HIGHEST, preferred_element_type=F32) + carry[...]
        carry[...] += _colsum(dcv)
        z = lat_ref[:, LAT_F:LAT_F + N_HEADS] + bf_ref[...]
        dz = dlogf * jax.nn.sigmoid(-z)
        lscr[:, LAT_F:LAT_F + N_HEADS] = dz
        o_bf[...] += _colsum(dz)

        xql = lat_ref[:, LAT_Q:LAT_Q + Q_LORA]
        qlat_n, r_ql = _rms_fwd(xql, gql_ref[...])
        qlat_b = qlat_n.astype(BF16)
        g1, g2 = gqr_ref[:, 0:ROPE_HALF], gqr_ref[:, ROPE_HALF:ROPE_DIM]
        dqlat_n = jnp.zeros((tm, Q_LORA), F32)
        d_gqn = jnp.zeros((1, hd), F32)
        d_g1 = jnp.zeros((1, ROPE_HALF), F32)
        d_g2 = jnp.zeros((1, ROPE_HALF), F32)
        for h in range(N_HEADS):
            base = h * (hd + ROPE_DIM)
            rows_n = slice(base, base + hd)
            rows_1 = slice(base + hd, base + hd + ROPE_HALF)
            rows_2 = slice(base + hd + ROPE_HALF, base + hd + ROPE_DIM)
            pn = _dot(qlat_b, wqb_ref[rows_n, :], NT)
            p1 = _dot(qlat_b, wqb_ref[rows_1, :], NT)
            p2 = _dot(qlat_b, wqb_ref[rows_2, :], NT)
            r_n = lax.rsqrt(jnp.mean(pn * pn, axis=-1, keepdims=True) + RMS_EPS)
            _, _, n1, n2, r_r = _rope_pair_fwd(p1, p2, g1, g2, cos_v, sin_v)
            dpn, gn_rows = _rms_bwd(dmq_ref[:, h * lanes:h * lanes + hd], pn, gqn_ref[...], r_n)
            dp1, dp2, g1_rows, g2_rows = _rope_pair_bwd(
                dmq_ref[:, h * lanes + hd:h * lanes + hd + ROPE_HALF],
                dmq_ref[:, h * lanes + hd + ROPE_HALF:h * lanes + hd + ROPE_DIM], n1, n2, r_r, g1, g2, cos_v, sin_v)
            d_gqn += _colsum(gn_rows)
            d_g1 += _colsum(g1_rows)
            d_g2 += _colsum(g2_rows)
            dpn_b, dp1_b, dp2_b = dpn.astype(BF16), dp1.astype(BF16), dp2.astype(BF16)
            dqlat_n += _dot(dpn_b, wqb_ref[rows_n, :]) + _dot(dp1_b, wqb_ref[rows_1, :]) + _dot(dp2_b, wqb_ref[rows_2, :])
            wq_acc[rows_n, :] += _dot(dpn_b, qlat_b, TN)
            wq_acc[rows_1, :] += _dot(dp1_b, qlat_b, TN)
            wq_acc[rows_2, :] += _dot(dp2_b, qlat_b, TN)
        dxql, gql_rows = _rms_bwd(dqlat_n, xql, gql_ref[...], r_ql)
        lscr[:, LAT_Q:LAT_Q + Q_LORA] = dxql
        o_gql[...] += _colsum(gql_rows)
        o_gqn[...] += d_gqn
        o_gqr[:, 0:ROPE_HALF] += d_g1
        o_gqr[:, ROPE_HALF:ROPE_DIM] += d_g2

        xkv = lat_ref[:, LAT_KV:LAT_KV + KV_LORA]
        kv_n, r_kv = _rms_fwd(xkv, gkvl_ref[...])
        kv_b = kv_n.astype(BF16)
        dkv_n = jnp.zeros((tm, KV_LORA), F32)
        d_gkn = jnp.zeros((1, hd), F32)
        dkr1 = jnp.zeros((tm, ROPE_HALF), F32)
        dkr2 = jnp.zeros((tm, ROPE_HALF), F32)
        for h in range(N_HEADS):
            rows_k = slice(h * 2 * hd, h * 2 * hd + hd)
            rows_v = slice(h * 2 * hd + hd, (h + 1) * 2 * hd)
            pk = _dot(kv_b, wkvb_ref[rows_k, :], NT)
            r_k = lax.rsqrt(jnp.mean(pk * pk, axis=-1, keepdims=True) + RMS_EPS)
            dpk, gk_rows = _rms_bwd(dmk_ref[:, h * lanes:h * lanes + hd], pk, gkn_ref[...], r_k)
            d_gkn += _colsum(gk_rows)
            dkr1 += dmk_ref[:, h * lanes + hd:h * lanes + hd + ROPE_HALF]
            dkr2 += dmk_ref[:, h * lanes + hd + ROPE_HALF:h * lanes + hd + ROPE_DIM]
            dpk_b = dpk.astype(BF16)
            dv_b = dmv_ref[:, h * hd:(h + 1) * hd].astype(BF16)
            dkv_n += _dot(dpk_b, wkvb_ref[rows_k, :]) + _dot(dv_b, wkvb_ref[rows_v, :])
            wkv_acc[rows_k, :] += _dot(dpk_b, kv_b, TN)
            wkv_acc[rows_v, :] += _dot(dv_b, kv_b, TN)
        dxkv, gkvl_rows = _rms_bwd(dkv_n, xkv, gkvl_ref[...], r_kv)
        lscr[:, LAT_KV:LAT_KV + KV_LORA] = dxkv
        o_gkvl[...] += _colsum(gkvl_rows)
        o_gkn[...] += d_gkn

        gk1, gk2 = gkr_ref[:, 0:ROPE_HALF], gkr_ref[:, ROPE_HALF:ROPE_DIM]
        _, _, kn1, kn2, r_kr = _rope_pair_fwd(lat_ref[:, LAT_KR:LAT_KR + ROPE_HALF],
                                              lat_ref[:, LAT_KR + ROPE_HALF:LAT_KR + ROPE_DIM], gk1, gk2, cos_v, sin_v)
        dk1, dk2, gk1_rows, gk2_rows = _rope_pair_bwd(dkr1, dkr2, kn1, kn2, r_kr, gk1, gk2, cos_v, sin_v)
        lscr[:, LAT_KR:LAT_KR + ROPE_HALF] = dk1
        lscr[:, LAT_KR + ROPE_HALF:LAT_KR + ROPE_DIM] = dk2
        o_gkr[:, 0:ROPE_HALF] += _colsum(gk1_rows)
        o_gkr[:, ROPE_HALF:ROPE_DIM] += _colsum(gk2_rows)

        dlat_ref[...] = lscr[...].astype(BF16)

        @pl.when(i == n_t - 1)
        def _():
            dwqb_ref[...] = wq_acc[...].astype(BF16)
            dwkvb_ref[...] = wkv_acc[...].astype(BF16)

    small = [g_qlat, g_kvlat, g_qn, g_qr, g_kn, g_kr, g_fq, g_fk, b_f]

    def rtok(n):
        return pl.BlockSpec((tm, n), lambda i: (n_t - 1 - i, 0))

    return pl.pallas_call(
        body, name="prep_bwd", grid=(n_t,),
        in_specs=[rtok(1536), rtok(LAT_W), rtok(ROPE_HALF), rtok(ROPE_HALF), _full(wqb_t.shape), _full(wkvb_t.shape)]
        + [_full(s.shape) for s in small]
        + [rtok(512), rtok(512), rtok(512), rtok(N_HEADS), rtok(N_HEADS * lanes), rtok(N_HEADS * lanes), rtok(512)],
        out_specs=[rtok(1536), rtok(LAT_W), _full(wqb_t.shape), _full(wkvb_t.shape)] + [_full(s.shape) for s in small],
        out_shape=[_sds((t, 1536), BF16), _sds((t, LAT_W), BF16), _sds(wqb_t.shape, BF16), _sds(wkvb_t.shape, BF16)]
        + [_sds(s.shape, F32) for s in small],
        scratch_shapes=[pltpu.VMEM((1, N_HEADS), F32), pltpu.VMEM((tm, LAT_W), F32), pltpu.VMEM(wqb_t.shape, F32),
                        pltpu.VMEM(wkvb_t.shape, F32)],
        compiler_params=pltpu.CompilerParams(dimension_semantics=("arbitrary",)),
    )(fox, lat, cos, sin, wqb_t, wkvb_t, *small, dfq, dfk, dfv, dc, dmq, dmk, dmv)


def _attn_fwd(q, k, v, c, c_t, *, lanes, scale, name):
    t = q.shape[0]
    tq = min(ATT_TILE, t)
    n_q = t // tq
    hd = HEAD_DIM
    decay = c is not None

    def body(*refs):
        if decay:
            q_ref, k_ref, v_ref, c_ref, ct_ref, o_ref, o32_ref, lse_ref, m_scr, l_scr, acc = refs
        else:
            q_ref, k_ref, v_ref, o_ref, o32_ref, lse_ref, m_scr, l_scr, acc = refs
        i, j = pl.program_id(0), pl.program_id(1)

        @pl.when(j == 0)
        def _():
            m_scr[...] = jnp.full_like(m_scr, MASK_VALUE)
            l_scr[...] = jnp.zeros_like(l_scr)
            acc[...] = jnp.zeros_like(acc)

        def step(diagonal):
            if diagonal:
                keep = lax.broadcasted_iota(jnp.int32, (tq, tq), 0) >= lax.broadcasted_iota(jnp.int32, (tq, tq), 1)
            for h in range(N_HEADS):
                s = _dot(q_ref[:, h * lanes:(h + 1) * lanes], k_ref[:, h * lanes:(h + 1) * lanes], NT)
                if scale != 1.0:
                    s = s * scale
                if decay:
                    s = s + (c_ref[:, h:h + 1] - ct_ref[h:h + 1, :])
                if diagonal:
                    s = jnp.where(keep, s, MASK_VALUE)
                m_prev = m_scr[h]
                m_new = jnp.maximum(m_prev, jnp.max(s, axis=1, keepdims=True))
                alpha = jnp.exp(m_prev - m_new)
                p = jnp.exp(s - m_new)
                l_scr[h] = alpha * l_scr[h] + jnp.sum(p, axis=1, keepdims=True)
                w = slice(h * hd, (h + 1) * hd)
                p_b = p.astype(BF16)
                acc[0, :, w] = alpha * acc[0, :, w] + _dot(p_b, v_ref[:, w])
                if decay:
                    p_lo = (p - p_b.astype(F32)).astype(BF16)
                    acc[1, :, w] = alpha * acc[1, :, w] + _dot(p_lo, v_ref[:, w])
                m_scr[h] = m_new

        @pl.when(j < i)
        def _():
            step(False)

        @pl.when(j == i)
        def _():
            step(True)
            for h in range(N_HEADS):
                w = slice(h * hd, (h + 1) * hd)
                l = l_scr[h]
                o_ref[:, w] = (acc[0, :, w] / l).astype(BF16)
                o32_ref[:, w] = ((acc[0, :, w] + acc[1, :, w]) if decay else acc[0, :, w]) / l
                lse_ref[:, h:h + 1] = m_scr[h] + jnp.log(l)

    qspec = lambda n: pl.BlockSpec((tq, n), lambda i, j: (i, 0))
    kspec = lambda n: pl.BlockSpec((tq, n), lambda i, j: (jnp.minimum(i, j), 0))
    in_specs = [qspec(N_HEADS * lanes), kspec(N_HEADS * lanes), kspec(512)]
    args = [q, k, v]
    if decay:
        in_specs += [qspec(N_HEADS), pl.BlockSpec((N_HEADS, tq), lambda i, j: (0, jnp.minimum(i, j)))]
        args += [c, c_t]
    return pl.pallas_call(
        body, name=name, grid=(n_q, n_q), in_specs=in_specs, out_specs=[qspec(512), qspec(512), qspec(N_HEADS)],
        out_shape=[_sds((t, 512), BF16), _sds((t, 512), F32), _sds((t, N_HEADS), F32)],
        scratch_shapes=[pltpu.VMEM((N_HEADS, tq, 1), F32), pltpu.VMEM((N_HEADS, tq, 1), F32), pltpu.VMEM((2, tq, 512), F32)],
        compiler_params=pltpu.CompilerParams(dimension_semantics=("arbitrary", "arbitrary")),
    )(*args)


def _attn_bwd(q, k, v, do, delta, lse, c, c_t, *, lanes, scale, name):
    t = q.shape[0]
    tq = min(ATT_TILE, t)
    n_q = t // tq
    hd = HEAD_DIM
    decay = c is not None

    def body(*refs):
        if decay:
            q_ref, k_ref, v_ref, do_ref, delta_ref, lse_ref, c_ref, ct_ref, dq_hbm, dk_ref, dv_ref, dct_ref, dq_ref = refs
        else:
            q_ref, k_ref, v_ref, do_ref, delta_ref, lse_ref, dq_hbm, dk_ref, dv_ref, dq_ref = refs
        j, i = pl.program_id(0), pl.program_id(1)

        @pl.when((j == 0) & (i == 0))
        def _():
            dq_ref[...] = jnp.zeros_like(dq_ref)

        @pl.when(i == j)
        def _():
            dk_ref[...] = jnp.zeros_like(dk_ref)
            dv_ref[...] = jnp.zeros_like(dv_ref)
            if decay:
                dct_ref[...] = jnp.zeros_like(dct_ref)

        def step(diagonal):
            if diagonal:
                keep = lax.broadcasted_iota(jnp.int32, (tq, tq), 0) >= lax.broadcasted_iota(jnp.int32, (tq, tq), 1)
            rows = pl.ds(pl.multiple_of(i * tq, tq), tq)
            for h in range(N_HEADS):
                wl = slice(h * lanes, (h + 1) * lanes)
                w = slice(h * hd, (h + 1) * hd)
                qh, kh = q_ref[:, wl], k_ref[:, wl]
                s = _dot(qh, kh, NT)
                if scale != 1.0:
                    s = s * scale
                if decay:
                    s = s + (c_ref[:, h:h + 1] - ct_ref[h:h + 1, :])
                if diagonal:
                    s = jnp.where(keep, s, MASK_VALUE)
                p = jnp.exp(s - lse_ref[:, h:h + 1])
                doh = do_ref[:, w]
                dv_ref[:, w] += _dot(p.astype(BF16), doh, TN)
                dp = _dot(doh, v_ref[:, w], NT)
                ds = p * (dp - delta_ref[:, h:h + 1])
                if decay:
                    dct_ref[h:h + 1, :] -= _colsum(ds)
                if scale != 1.0:
                    ds = ds * scale
                ds_b = ds.astype(BF16)
                dk_ref[:, wl] += _dot(ds_b, qh, TN)
                dq_ref[rows, wl] += _dot(ds_b, kh)

        @pl.when(i > j)
        def _():
            step(False)

        @pl.when(i == j)
        def _():
            step(True)

        @pl.when((j == n_q - 1) & (i == n_q - 1))
        def _():
            pltpu.sync_copy(dq_ref, dq_hbm)

    qspec = lambda n: pl.BlockSpec((tq, n), lambda j, i: (jnp.maximum(i, j), 0))
    kspec = lambda n: pl.BlockSpec((tq, n), lambda j, i: (j, 0))
    in_specs = [qspec(N_HEADS * lanes), kspec(N_HEADS * lanes), kspec(512), qspec(512), qspec(N_HEADS), qspec(N_HEADS)]
    out_specs = [pl.BlockSpec(memory_space=pl.ANY), kspec(N_HEADS * lanes), kspec(512)]
    out_shape = [_sds((t, N_HEADS * lanes), F32), _sds((t, N_HEADS * lanes), F32), _sds((t, 512), F32)]
    args = [q, k, v, do, delta, lse]
    if decay:
        ctspec = pl.BlockSpec((N_HEADS, tq), lambda j, i: (0, j))
        in_specs += [qspec(N_HEADS), ctspec]
        out_specs.append(ctspec)
        out_shape.append(_sds((N_HEADS, t), F32))
        args += [c, c_t]
    return pl.pallas_call(
        body, name=name, grid=(n_q, n_q), in_specs=in_specs, out_specs=out_specs, out_shape=out_shape,
        scratch_shapes=[pltpu.VMEM((t, N_HEADS * lanes), F32)],
        compiler_params=pltpu.CompilerParams(dimension_semantics=("arbitrary", "arbitrary")),
    )(*args)


def _mix_fwd(x, y_mla, y_fox, gates, b_gate, wbm_t, wbf_t, wo):
    t = x.shape[0]
    tm = min(PROJ_TILE, t)

    def body(x_ref, ym_ref, yf_ref, gt_ref, bg_ref, wbm_ref, wbf_ref, wo_ref, out_ref):
        um = _dot(ym_ref[...], wbm_ref[...], NT)
        uf = _dot(yf_ref[...], wbf_ref[...], NT)
        sm = jax.nn.sigmoid(gt_ref[:, 0:D_MODEL] + bg_ref[0:1, :])
        sf = jax.nn.sigmoid(gt_ref[:, D_MODEL:2 * D_MODEL] + bg_ref[1:2, :])
        mixed = sm * um + sf * uf
        out_ref[...] = x_ref[...] + _dot(mixed.astype(BF16), wo_ref[...])

    tok = _tok(tm, D_MODEL)
    return pl.pallas_call(
        body, name="mix_fwd", grid=(t // tm,),
        in_specs=[tok, _tok(tm, 512), _tok(tm, 512), _tok(tm, 2 * D_MODEL), _full((2, D_MODEL)), _full(wbm_t.shape),
                  _full(wbf_t.shape), _full(wo.shape)],
        out_specs=tok, out_shape=_sds((t, D_MODEL), F32),
        compiler_params=pltpu.CompilerParams(dimension_semantics=("arbitrary",)),
    )(x, y_mla, y_fox, gates, b_gate, wbm_t, wbf_t, wo)


def _mix_bwd(dx, y_mla, y_fox, y_mla32, y_fox32, gates, b_gate, wbm_t, wbf_t, wo):
    t = dx.shape[0]
    tm = min(PROJ_TILE, t)

    def body(dx_ref, ym_ref, yf_ref, ym32_ref, yf32_ref, gt_ref, bg_ref, wbm_ref, wbf_ref, wo_ref,
             dym_ref, dyf_ref, dlm_ref, dlf_ref, dgt_ref, mixed_ref, dum_ref, duf_ref, dxb_ref, dbg_ref, prod):
        i = pl.program_id(0)
        dxb = dx_ref[...].astype(BF16)
        dxb_ref[...] = dxb
        dmixed = _dot(dxb, wo_ref[...], NT)
        um = _dot(ym_ref[...], wbm_ref[...], NT)
        uf = _dot(yf_ref[...], wbf_ref[...], NT)
        sm = jax.nn.sigmoid(gt_ref[:, 0:D_MODEL] + bg_ref[0:1, :])
        sf = jax.nn.sigmoid(gt_ref[:, D_MODEL:2 * D_MODEL] + bg_ref[1:2, :])
        mixed_ref[...] = (sm * um + sf * uf).astype(BF16)
        dum = (dmixed * sm).astype(BF16)
        duf = (dmixed * sf).astype(BF16)
        dum_ref[...] = dum
        duf_ref[...] = duf
        dgm = dmixed * um * (sm * (1.0 - sm))
        dgf = dmixed * uf * (sf * (1.0 - sf))
        dgt_ref[:, 0:D_MODEL] = dgm.astype(BF16)
        dgt_ref[:, D_MODEL:2 * D_MODEL] = dgf.astype(BF16)
        for du, wb_ref, y32_ref, dy_ref, dl_ref in ((dum, wbm_ref, ym32_ref, dym_ref, dlm_ref),
                                                    (duf, wbf_ref, yf32_ref, dyf_ref, dlf_ref)):
            dy = _dot(du, wb_ref[...])
            dy_ref[...] = dy.astype(BF16)
            prod[...] = dy.astype(BF16).astype(F32) * y32_ref[...]
            for h in range(N_HEADS):
                dl_ref[:, h:h + 1] = jnp.sum(prod[:, h * HEAD_DIM:(h + 1) * HEAD_DIM], axis=1, keepdims=True)

        @pl.when(i == 0)
        def _():
            dbg_ref[...] = jnp.zeros_like(dbg_ref)

        dbg_ref[0:1, :] += _colsum(dgm)
        dbg_ref[1:2, :] += _colsum(dgf)

    tok = _tok(tm, D_MODEL)
    tokb = _sds((t, D_MODEL), BF16)
    t512, t8 = _tok(tm, 512), _tok(tm, N_HEADS)
    return pl.pallas_call(
        body, name="mix_bwd", grid=(t // tm,),
        in_specs=[tok, t512, t512, t512, t512, _tok(tm, 2 * D_MODEL), _full((2, D_MODEL)), _full(wbm_t.shape),
                  _full(wbf_t.shape), _full(wo.shape)],
        out_specs=[t512, t512, t8, t8, _tok(tm, 2 * D_MODEL), tok, tok, tok, tok, _full((2, D_MODEL))],
        out_shape=[_sds((t, 512), BF16), _sds((t, 512), BF16), _sds((t, N_HEADS), F32), _sds((t, N_HEADS), F32),
                   _sds((t, 2 * D_MODEL), BF16), tokb, tokb, tokb, tokb, _sds((2, D_MODEL), F32)],
        scratch_shapes=[pltpu.VMEM((tm, 512), F32)],
        compiler_params=pltpu.CompilerParams(dimension_semantics=("arbitrary",)),
    )(dx, y_mla, y_fox, y_mla32, y_fox32, gates, b_gate, wbm_t, wbf_t, wo)


def _my_position():
    x, y, c = lax.axis_index("x"), lax.axis_index("y"), lax.axis_index("c")
    return x, y, c, 4 * x + 2 * y + c


def _peer(x, y, c, mask):
    px = 1 - x if mask & 4 else x
    py = 1 - y if mask & 2 else y
    pc = 1 - c if mask & 1 else c
    return (px, py, pc), 4 * px + 2 * py + pc


def _all_gather(pieces):
    n_arr = len(pieces)

    def body(*refs):
        srcs, dsts = refs[:n_arr], refs[n_arr:2 * n_arr]
        send_sems, recv_sems, local_sems = refs[2 * n_arr:]
        x, y, c, me = _my_position()
        copies = []
        for a in range(n_arr):
            r = srcs[a].shape[0]
            local = pltpu.make_async_copy(srcs[a], dsts[a].at[pl.ds(me * r, r)], local_sems.at[a])
            local.start()
            copies.append(local)
            for mask in range(1, N_DEV):
                peer, _ = _peer(x, y, c, mask)
                cp = pltpu.make_async_remote_copy(
                    src_ref=srcs[a], dst_ref=dsts[a].at[pl.ds(me * r, r)], send_sem=send_sems.at[a, mask],
                    recv_sem=recv_sems.at[a, mask], device_id=peer, device_id_type=MESH)
                cp.start()
                copies.append(cp)
        for cp in copies:
            cp.wait()

    hbm = pl.BlockSpec(memory_space=pl.ANY)
    return pl.pallas_call(
        body, name="all_gather", in_specs=[hbm] * n_arr, out_specs=[hbm] * n_arr,
        out_shape=[_sds((N_DEV * p.shape[0],) + p.shape[1:], p.dtype) for p in pieces],
        scratch_shapes=[pltpu.SemaphoreType.DMA((n_arr, N_DEV)), pltpu.SemaphoreType.DMA((n_arr, N_DEV)),
                        pltpu.SemaphoreType.DMA((n_arr,))],
    )(*pieces)


def _exchange_pieces(grads):
    n_arr = len(grads)

    def body(*refs):
        srcs, dsts = refs[:n_arr], refs[n_arr:2 * n_arr]
        send_sems, recv_sems, local_sems = refs[2 * n_arr:]
        x, y, c, me = _my_position()
        copies = []
        for a in range(n_arr):
            r = srcs[a].shape[0] // N_DEV
            local = pltpu.make_async_copy(srcs[a].at[pl.ds(me * r, r)], dsts[a].at[me], local_sems.at[a])
            local.start()
            copies.append(local)
            for mask in range(1, N_DEV):
                peer, peer_id = _peer(x, y, c, mask)
                cp = pltpu.make_async_remote_copy(
                    src_ref=srcs[a].at[pl.ds(peer_id * r, r)], dst_ref=dsts[a].at[me], send_sem=send_sems.at[a, mask],
                    recv_sem=recv_sems.at[a, mask], device_id=peer, device_id_type=MESH)
                cp.start()
                copies.append(cp)
        for cp in copies:
            cp.wait()

    hbm = pl.BlockSpec(memory_space=pl.ANY)
    return pl.pallas_call(
        body, name="grad_exchange", in_specs=[hbm] * n_arr, out_specs=[hbm] * n_arr,
        out_shape=[_sds((N_DEV, g.shape[0] // N_DEV) + g.shape[1:], g.dtype) for g in grads],
        scratch_shapes=[pltpu.SemaphoreType.DMA((n_arr, N_DEV)), pltpu.SemaphoreType.DMA((n_arr, N_DEV)),
                        pltpu.SemaphoreType.DMA((n_arr,))],
    )(*grads)


def _all_reduce_small(vec):
    r = vec.shape[0]

    def body(v_ref, o_ref, buf, send_sems, recv_sems):
        x, y, c, me = _my_position()
        buf[me] = v_ref[...]
        copies = []
        for mask in range(1, N_DEV):
            peer, _ = _peer(x, y, c, mask)
            cp = pltpu.make_async_remote_copy(src_ref=v_ref, dst_ref=buf.at[me], send_sem=send_sems.at[mask],
                                              recv_sem=recv_sems.at[mask], device_id=peer, device_id_type=MESH)
            cp.start()
            copies.append(cp)
        for cp in copies:
            cp.wait()
        total = buf[0]
        for s in range(1, N_DEV):
            total = total + buf[s]
        o_ref[...] = total

    vm = pl.BlockSpec(memory_space=pltpu.VMEM)
    return pl.pallas_call(
        body, name="all_reduce_small", in_specs=[vm], out_specs=vm, out_shape=_sds(vec.shape, F32),
        scratch_shapes=[pltpu.VMEM((N_DEV, r, 128), F32), pltpu.SemaphoreType.DMA((N_DEV,)), pltpu.SemaphoreType.DMA((N_DEV,))],
    )(vec)


def _adamw_math(w, g, m, v):
    m = ADAM_B1 * m + (1.0 - ADAM_B1) * g
    v = ADAM_B2 * v + (1.0 - ADAM_B2) * (g * g)
    m_hat = m / (1.0 - ADAM_B1 ** ADAM_STEP)
    v_hat = v / (1.0 - ADAM_B2 ** ADAM_STEP)
    delta = -ADAM_LR * (m_hat / (jnp.sqrt(v_hat) + ADAM_EPS) + ADAM_WD * w)
    return delta, m, v


def _reduce_adamw(slots, w, m, v, *, transpose, name):
    r, n = slots.shape[1:]

    def body(s_ref, w_ref, m_ref, v_ref, g_ref, d_ref, nm_ref, nv_ref, *scr):
        g = s_ref[0].astype(F32)
        for s in range(1, N_DEV):
            g = g + s_ref[s].astype(F32)
        if transpose:
            scr[0][...] = g.T
            g = scr[0][:, 0:w_ref.shape[1]]
        g_ref[...] = g
        d_ref[...], nm_ref[...], nv_ref[...] = _adamw_math(w_ref[...], g, m_ref[...], v_ref[...])

    out = _sds(w.shape, F32)
    return pl.pallas_call(
        body, name=name, out_shape=[out, out, out, out],
        scratch_shapes=[pltpu.VMEM((n, r), F32)] if transpose else [],
    )(slots, w, m, v)


def _adamw(g, w, m, v, name):
    def body(g_ref, w_ref, m_ref, v_ref, d_ref, nm_ref, nv_ref):
        d_ref[...], nm_ref[...], nv_ref[...] = _adamw_math(w_ref[...], g_ref[...], m_ref[...], v_ref[...])

    out = _sds(w.shape, F32)
    return pl.pallas_call(body, name=name, out_shape=[out, out, out])(g, w, m, v)


_SMALL = ["ffn1_norm", "mix_norm", "ffn2_norm", "mla_q_lat_norm", "mla_kv_lat_norm", "mla_q_nope_gain", "mla_q_rope_gain",
          "mla_k_nope_gain", "mla_k_rope_gain", "fox_q_gain", "fox_k_gain", "fox_b_f"]
_WEIGHTS = ["ffn1_norm", "ffn1_w_gate", "ffn1_w_up", "ffn1_w_down", "mix_norm", "w_in", "mla_q_lat_norm", "mla_w_qb",
            "mla_kv_lat_norm", "mla_w_kvb", "mla_q_nope_gain", "mla_q_rope_gain", "mla_k_nope_gain", "mla_k_rope_gain",
            "fox_q_gain", "fox_k_gain", "fox_b_f", "w_branch_mla", "w_branch_fox", "b_gate", "w_o", "ffn2_norm",
            "ffn2_w_gate", "ffn2_w_up", "ffn2_w_down"]
_IN_Q, _IN_KV, _IN_KR, _IN_FOX, _IN_F, _IN_GATES = (0, 192), (192, 128), (320, 32), (352, 1536), (1888, 8), (1896, 2048)


def _rows(a, seg):
    return a[seg[0]:seg[0] + seg[1]]


def _split_w_in(win_t):
    z = lambda n: jnp.zeros((n, D_MODEL), win_t.dtype)
    lat = jnp.concatenate([_rows(win_t, _IN_Q), z(LAT_KV - Q_LORA), _rows(win_t, _IN_KV), _rows(win_t, _IN_KR),
                           _rows(win_t, _IN_F), z(LAT_W - LAT_F - N_HEADS)], axis=0)
    return _rows(win_t, _IN_GATES), _rows(win_t, _IN_FOX), lat


def _join_w_in(d_gates, d_fox, d_lat):
    return jnp.concatenate([d_lat[LAT_Q:LAT_Q + Q_LORA], d_lat[LAT_KV:LAT_KV + KV_LORA], d_lat[LAT_KR:LAT_KR + ROPE_DIM],
                            d_fox, d_lat[LAT_F:LAT_F + N_HEADS], d_gates], axis=0)


def kernel(x, positions, ffn1_norm, ffn1_w_gate, ffn1_w_up, ffn1_w_down, mix_norm, w_in, mla_q_lat_norm, mla_w_qb, mla_kv_lat_norm, mla_w_kvb, mla_q_nope_gain, mla_q_rope_gain, mla_k_nope_gain, mla_k_rope_gain, fox_q_gain, fox_k_gain, fox_b_f, w_branch_mla, w_branch_fox, b_gate, w_o, ffn2_norm, ffn2_w_gate, ffn2_w_up, ffn2_w_down, loss_target, m_ffn1_norm, m_ffn1_w_gate, m_ffn1_w_up, m_ffn1_w_down, m_mix_norm, m_w_in, m_mla_q_lat_norm, m_mla_w_qb, m_mla_kv_lat_norm, m_mla_w_kvb, m_mla_q_nope_gain, m_mla_q_rope_gain, m_mla_k_nope_gain, m_mla_k_rope_gain, m_fox_q_gain, m_fox_k_gain, m_fox_b_f, m_w_branch_mla, m_w_branch_fox, m_b_gate, m_w_o, m_ffn2_norm, m_ffn2_w_gate, m_ffn2_w_up, m_ffn2_w_down, v_ffn1_norm, v_ffn1_w_gate, v_ffn1_w_up, v_ffn1_w_down, v_mix_norm, v_w_in, v_mla_q_lat_norm, v_mla_w_qb, v_mla_kv_lat_norm, v_mla_w_kvb, v_mla_q_nope_gain, v_mla_q_rope_gain, v_mla_k_nope_gain, v_mla_k_rope_gain, v_fox_q_gain, v_fox_k_gain, v_fox_b_f, v_w_branch_mla, v_w_branch_fox, v_b_gate, v_w_o, v_ffn2_norm, v_ffn2_w_gate, v_ffn2_w_up, v_ffn2_w_down):
    env = dict(locals())
    strip = lambda n, a: a if n in _SMALL else a[0]
    W = {n: strip(n, env[n]) for n in _WEIGHTS}
    M = {n: strip(n, env["m_" + n]) for n in _WEIGHTS}
    V = {n: strip(n, env["v_" + n]) for n in _WEIGHTS}
    xs = x[0]
    t = xs.shape[0]

    col_split = ["ffn1_w_gate", "ffn1_w_up", "ffn2_w_gate", "ffn2_w_up", "mla_w_qb", "mla_w_kvb", "w_branch_mla", "w_branch_fox"]
    row_split = ["ffn1_w_down", "ffn2_w_down", "w_o"]
    pieces = {n: W[n].T.astype(BF16) for n in col_split}
    pieces.update({n: W[n].astype(BF16) for n in row_split})
    pieces["w_in"] = jnp.pad(W["w_in"].T.astype(BF16), ((0, W_IN_PIECE_PAD - W_IN_PIECE), (0, 0)))
    pieces["b_gate"] = W["b_gate"].T
    order = col_split + row_split + ["w_in", "b_gate"]
    G = dict(zip(order, _all_gather([pieces[n] for n in order])))
    win_t = G["w_in"].reshape(N_DEV, W_IN_PIECE_PAD, D_MODEL)[:, :W_IN_PIECE].reshape(N_DEV * W_IN_PIECE, D_MODEL)
    wgate_t, wfox_t, wlat_t = _split_w_in(win_t)
    bg = G["b_gate"].T

    inv_freq = ROPE_THETA ** (-jnp.arange(ROPE_HALF, dtype=F32) / ROPE_HALF)
    ang = positions[0].astype(F32)[:, None] * inv_freq
    cos, sin = jnp.cos(ang), jnp.sin(ang)
    prep_small = [W[n] for n in ["mla_q_lat_norm", "mla_kv_lat_norm", "mla_q_nope_gain", "mla_q_rope_gain", "mla_k_nope_gain",
                                 "mla_k_rope_gain", "fox_q_gain", "fox_k_gain", "fox_b_f"]]

    x1, a1, b1 = _ffn_fwd(xs, W["ffn1_norm"], G["ffn1_w_gate"], G["ffn1_w_up"], G["ffn1_w_down"])
    gates, fox, lat = _proj_fwd(x1, W["mix_norm"], wgate_t, wfox_t, wlat_t)
    fq, fk, fv, c, mq, mk, mv = _prep_fwd(fox, lat, cos, sin, G["mla_w_qb"], G["mla_w_kvb"], *prep_small)
    c_t = c.T
    y_fox, y_fox32, lse_fox = _attn_fwd(fq, fk, fv, c, c_t, lanes=HEAD_DIM, scale=1.0, name="fox_fwd")
    y_mla, y_mla32, lse_mla = _attn_fwd(mq, mk, mv, None, None, lanes=MLA_QK_LANES, scale=MLA_SCALE, name="mla_fwd")
    x2 = _mix_fwd(x1, y_mla, y_fox, gates, bg, G["w_branch_mla"], G["w_branch_fox"], G["w_o"])
    dx3, a2, b2, loss_vec = _ffn_fwd(x2, W["ffn2_norm"], G["ffn2_w_gate"], G["ffn2_w_up"], G["ffn2_w_down"], target=loss_target[0])

    dx2, dg_ffn2, da2, db2, h2, n2, dyh2 = _ffn_bwd(dx3, x2, W["ffn2_norm"], a2, b2, G["ffn2_w_gate"], G["ffn2_w_up"],
                                                   G["ffn2_w_down"], "ffn2_bwd")
    grads = {"ffn2_w_gate": _tn_matmul(da2, n2, "ffn2_dgate"), "ffn2_w_up": _tn_matmul(db2, n2, "ffn2_dup"),
             "ffn2_w_down": _tn_matmul(h2, dyh2, "ffn2_ddown")}
    dy_mla, dy_fox, delta_mla, delta_fox, dgates, mixed, dum, duf, dx2b, dbg = _mix_bwd(
        dx2, y_mla, y_fox, y_mla32, y_fox32, gates, bg, G["w_branch_mla"], G["w_branch_fox"], G["w_o"])
    grads["w_o"] = _tn_matmul(mixed, dx2b, "d_w_o")
    grads["w_branch_mla"] = _tn_matmul(dum, y_mla, "d_w_branch_mla")
    grads["w_branch_fox"] = _tn_matmul(duf, y_fox, "d_w_branch_fox")
    dfq, dfk, dfv, dc_t = _attn_bwd(fq, fk, fv, dy_fox, delta_fox, lse_fox, c, c_t, lanes=HEAD_DIM, scale=1.0, name="fox_bwd")
    dmq, dmk, dmv = _attn_bwd(mq, mk, mv, dy_mla, delta_mla, lse_mla, None, None, lanes=MLA_QK_LANES, scale=MLA_SCALE, name="mla_bwd")
    prep_out = _prep_bwd(fox, lat, cos, sin, G["mla_w_qb"], G["mla_w_kvb"], *prep_small, dfq, dfk, dfv, dc_t.T, dmq, dmk, dmv)
    dfox, dlat, grads["mla_w_qb"], grads["mla_w_kvb"] = prep_out[:4]
    d_prep_small = prep_out[4:]
    dx1, dg_mix, nmix = _proj_bwd(dgates, dfox, dlat, wgate_t, wfox_t, wlat_t, x1, W["mix_norm"], dx2)
    dwin_t = _join_w_in(_tn_matmul(dgates, nmix, "d_w_in_gates"), _tn_matmul(dfox, nmix, "d_w_in_fox"),
                        _tn_matmul(dlat, nmix, "d_w_in_lat"))
    grads["w_in"] = jnp.pad(dwin_t.reshape(N_DEV, W_IN_PIECE, D_MODEL), ((0, 0), (0, W_IN_PIECE_PAD - W_IN_PIECE), (0, 0))
                            ).reshape(N_DEV * W_IN_PIECE_PAD, D_MODEL)
    dx0, dg_ffn1, da1, db1, h1, n1, dyh1 = _ffn_bwd(dx1, xs, W["ffn1_norm"], a1, b1, G["ffn1_w_gate"], G["ffn1_w_up"],
                                                   G["ffn1_w_down"], "ffn1_bwd")
    grads["ffn1_w_gate"] = _tn_matmul(da1, n1, "ffn1_dgate")
    grads["ffn1_w_up"] = _tn_matmul(db1, n1, "ffn1_dup")
    grads["ffn1_w_down"] = _tn_matmul(h1, dyh1, "ffn1_ddown")

    big = col_split + row_split + ["w_in"]
    slots = dict(zip(big, _exchange_pieces([grads[n] for n in big])))
    small_parts = [dg_ffn1, dg_mix, dg_ffn2] + list(d_prep_small) + [dbg.reshape(1, 2 * D_MODEL), loss_vec]
    flat = jnp.concatenate([p.reshape(-1) for p in small_parts])
    n_flat = flat.shape[0]
    rows = -(-n_flat // (8 * 128)) * 8
    total = _all_reduce_small(jnp.pad(flat, (0, rows * 128 - n_flat)).reshape(rows, 128)).reshape(-1)
    offs, small_g = 0, {}
    for n in _SMALL:
        small_g[n] = total[offs:offs + W[n].shape[1]].reshape(W[n].shape)
        offs += W[n].shape[1]
    bg_full = total[offs:offs + 2 * D_MODEL].reshape(2, D_MODEL)
    offs += 2 * D_MODEL
    loss = (0.5 / D_MODEL) * jnp.sum(total[offs:offs + D_MODEL])
    _, _, _, me = _my_position()
    small_g["b_gate"] = lax.dynamic_slice_in_dim(bg_full, me * (D_MODEL // N_DEV), D_MODEL // N_DEV, axis=1)

    res = {}
    for n in _WEIGHTS:
        if n in small_g:
            res[n] = (small_g[n],) + tuple(_adamw(small_g[n], W[n], M[n], V[n], "adamw_" + n))
        else:
            res[n] = tuple(_reduce_adamw(slots[n], W[n], M[n], V[n], transpose=n in col_split or n == "w_in", name="adamw_" + n))
    outs = [loss, dx0[None]]
    for k in range(4):
        outs += [res[n][k] if n in _SMALL else res[n][k][None] for n in _WEIGHTS]
    return tuple(outs)
```

```python
import functools

import jax
import jax.numpy as jnp
from jax import lax
from jax.experimental import pallas as pl
from jax.experimental.pallas import tpu as pltpu

F32 = jnp.float32
BF16 = jnp.bfloat16

D_MODEL = 1024
FFN_HIDDEN = 2816
N_HEADS = 8
HEAD_DIM = 64
ROPE_DIM = 32
ROPE_HALF = 16
Q_LORA = 192
KV_LORA = 128
ROPE_THETA = 10000.0
RMS_EPS = 1e-6
MLA_SCALE = (HEAD_DIM + ROPE_DIM) ** -0.5
FOX_SCALE = HEAD_DIM ** -0.5
MLA_QK_LANES = 128
ADAM_LR, ADAM_B1, ADAM_B2, ADAM_EPS, ADAM_WD, ADAM_STEP = 0.001, 0.9, 0.999, 1e-08, 0.01, 10
N_DEV = 8
N_CHIP = 4
W_IN_PIECE = 493
W_IN_PIECE_PAD = 496
LAT_W = 512
LAT_Q, LAT_KV, LAT_KR, LAT_F = 0, 256, 384, 416
MASK_VALUE = -1e30

TOK_TILE = 512
PROJ_TILE = 256
ATT_TILE = 512
FFN_HID_TILE = 1408

NT = (((1,), (1,)), ((), ()))
TN = (((0,), (0,)), ((), ()))
NN = (((1,), (0,)), ((), ()))
MESH = pl.DeviceIdType.MESH


def _dot(a, b, dims=NN):
    return lax.dot_general(a, b, dims, preferred_element_type=F32)


def _sds(shape, dtype):
    return jax.ShapeDtypeStruct(shape, dtype)


def _rms_fwd(x, g):
    r = lax.rsqrt(jnp.mean(x * x, axis=-1, keepdims=True) + RMS_EPS)
    return x * r * g, r


def _rms_bwd(dy, x, g, r):
    xn = x * r
    dyg = dy * g
    dx = r * (dyg - xn * jnp.mean(dyg * xn, axis=-1, keepdims=True))
    return dx, dy * xn


def _colsum(x):
    return jnp.sum(x, axis=0, keepdims=True)


def _full(shape):
    return pl.BlockSpec(shape, lambda *_: (0,) * len(shape))


def _tok(tm, n):
    return pl.BlockSpec((tm, n), lambda i, *_: (i, 0))


def _ffn_fwd(x, gain, wg_t, wu_t, wd, target=None):
    t = x.shape[0]
    tm = min(TOK_TILE, t)
    tf = FFN_HID_TILE
    n_t, n_f = t // tm, FFN_HIDDEN // tf
    with_loss = target is not None

    def body(*refs):
        if with_loss:
            x_ref, g_ref, wg_ref, wu_ref, wd_ref, t_ref, out_ref, a_ref, b_ref, lvec_ref, n_scr, acc = refs
        else:
            x_ref, g_ref, wg_ref, wu_ref, wd_ref, out_ref, a_ref, b_ref, n_scr, acc = refs
        i, j = pl.program_id(0), pl.program_id(1)

        @pl.when(j == 0)
        def _():
            xn, _ = _rms_fwd(x_ref[...], g_ref[...])
            n_scr[...] = xn.astype(BF16)
            acc[...] = jnp.zeros_like(acc)

        n = n_scr[...]
        a = _dot(n, wg_ref[...], NT)
        b = _dot(n, wu_ref[...], NT)
        a_ref[...] = a.astype(BF16)
        b_ref[...] = b.astype(BF16)
        h = (a * jax.nn.sigmoid(a)) * b
        acc[...] += _dot(h.astype(BF16), wd_ref[...])

        @pl.when(j == n_f - 1)
        def _():
            y = x_ref[...] + 0.5 * acc[...]
            if with_loss:
                diff = y - t_ref[...]
                out_ref[...] = diff * (1.0 / D_MODEL)
                sq = _colsum(diff * diff)

                @pl.when(i == 0)
                def _():
                    lvec_ref[...] = sq

                @pl.when(i > 0)
                def _():
                    lvec_ref[...] += sq
            else:
                out_ref[...] = y

    wspec = pl.BlockSpec((tf, D_MODEL), lambda i, j: (j, 0))
    hspec = pl.BlockSpec((tm, tf), lambda i, j: (i, j))
    in_specs = [_tok(tm, D_MODEL), _full((1, D_MODEL)), wspec, wspec, wspec]
    out_specs = [_tok(tm, D_MODEL), hspec, hspec]
    out_shape = [_sds((t, D_MODEL), F32), _sds((t, FFN_HIDDEN), BF16), _sds((t, FFN_HIDDEN), BF16)]
    args = [x, gain, wg_t, wu_t, wd]
    if with_loss:
        in_specs.append(_tok(tm, D_MODEL))
        out_specs.append(_full((1, D_MODEL)))
        out_shape.append(_sds((1, D_MODEL), F32))
        args.append(target)
    return pl.pallas_call(
        body, name="ffn_fwd_loss" if with_loss else "ffn_fwd", grid=(n_t, n_f), in_specs=in_specs, out_specs=out_specs,
        out_shape=out_shape,
        scratch_shapes=[pltpu.VMEM((tm, D_MODEL), BF16), pltpu.VMEM((tm, D_MODEL), F32)],
        compiler_params=pltpu.CompilerParams(dimension_semantics=("arbitrary", "arbitrary")),
    )(*args)


def _ffn_bwd(dy, x, gain, a, b, wg_t, wu_t, wd, name):
    t = x.shape[0]
    tm = min(PROJ_TILE, t)
    tf = FFN_HID_TILE
    n_t, n_f = t // tm, FFN_HIDDEN // tf

    def body(dy_ref, x_ref, g_ref, a_ref, b_ref, wg_ref, wu_ref, wd_ref,
             dx_ref, dg_ref, da_ref, db_ref, h_ref, n_ref, dyh_ref, acc):
        i, j = pl.program_id(0), pl.program_id(1)

        @pl.when(j == 0)
        def _():
            xn, _ = _rms_fwd(x_ref[...], g_ref[...])
            n_ref[...] = xn.astype(BF16)
            dyh_ref[...] = (0.5 * dy_ref[...]).astype(BF16)
            acc[...] = jnp.zeros_like(acc)

        dh = _dot(dyh_ref[...], wd_ref[...], NT)
        av = a_ref[...].astype(F32)
        bv = b_ref[...].astype(F32)
        s = jax.nn.sigmoid(av)
        silu = av * s
        da = (dh * bv * (s * (1.0 + av * (1.0 - s)))).astype(BF16)
        db = (dh * silu).astype(BF16)
        da_ref[...] = da
        db_ref[...] = db
        h_ref[...] = (silu * bv).astype(BF16)
        acc[...] += _dot(da, wg_ref[...]) + _dot(db, wu_ref[...])

        @pl.when(j == n_f - 1)
        def _():
            xv, g = x_ref[...], g_ref[...]
            r = lax.rsqrt(jnp.mean(xv * xv, axis=-1, keepdims=True) + RMS_EPS)
            dx, dg_rows = _rms_bwd(acc[...], xv, g, r)
            dx_ref[...] = dy_ref[...] + dx
            dg = _colsum(dg_rows)

            @pl.when(i == 0)
            def _():
                dg_ref[...] = dg

            @pl.when(i > 0)
            def _():
                dg_ref[...] += dg

    wspec = pl.BlockSpec((tf, D_MODEL), lambda i, j: (j, 0))
    hspec = pl.BlockSpec((tm, tf), lambda i, j: (i, j))
    tok = _tok(tm, D_MODEL)
    hid = _sds((t, FFN_HIDDEN), BF16)
    return pl.pallas_call(
        body, name=name, grid=(n_t, n_f),
        in_specs=[tok, tok, _full((1, D_MODEL)), hspec, hspec, wspec, wspec, wspec],
        out_specs=[tok, _full((1, D_MODEL)), hspec, hspec, hspec, tok, tok],
        out_shape=[_sds((t, D_MODEL), F32), _sds((1, D_MODEL), F32), hid, hid, hid,
                   _sds((t, D_MODEL), BF16), _sds((t, D_MODEL), BF16)],
        scratch_shapes=[pltpu.VMEM((tm, D_MODEL), F32)],
        compiler_params=pltpu.CompilerParams(dimension_semantics=("arbitrary", "arbitrary")),
    )(dy, x, gain, a, b, wg_t, wu_t, wd)


def _tn_matmul(a, b, name):
    t, m = a.shape
    n = b.shape[1]
    tk = min(TOK_TILE, t)
    n_k = t // tk

    def body(a_ref, b_ref, o_ref, acc):
        k = pl.program_id(0)
        p = _dot(a_ref[...], b_ref[...], TN)

        @pl.when(k == 0)
        def _():
            acc[...] = p

        @pl.when(k > 0)
        def _():
            acc[...] += p

        @pl.when(k == n_k - 1)
        def _():
            o_ref[...] = acc[...].astype(BF16)

    return pl.pallas_call(
        body, name=name, grid=(n_k,), in_specs=[_tok(tk, m), _tok(tk, n)], out_specs=_full((m, n)),
        out_shape=_sds((m, n), BF16), scratch_shapes=[pltpu.VMEM((m, n), F32)],
        compiler_params=pltpu.CompilerParams(dimension_semantics=("arbitrary",)),
    )(a, b)


def _proj_fwd(x, gain, wgate_t, wfox_t, wlat_t):
    t = x.shape[0]
    tm = min(PROJ_TILE, t)

    def body(x_ref, g_ref, wg_ref, wf_ref, wl_ref, og_ref, of_ref, ol_ref):
        xn, _ = _rms_fwd(x_ref[...], g_ref[...])
        n = xn.astype(BF16)
        og_ref[...] = _dot(n, wg_ref[...], NT)
        of_ref[...] = _dot(n, wf_ref[...], NT)
        ol_ref[...] = _dot(n, wl_ref[...], NT)

    return pl.pallas_call(
        body, name="proj_fwd", grid=(t // tm,),
        in_specs=[_tok(tm, D_MODEL), _full((1, D_MODEL)), _full(wgate_t.shape), _full(wfox_t.shape), _full(wlat_t.shape)],
        out_specs=[_tok(tm, 2 * D_MODEL), _tok(tm, 3 * 512), _tok(tm, LAT_W)],
        out_shape=[_sds((t, 2 * D_MODEL), F32), _sds((t, 3 * 512), F32), _sds((t, LAT_W), F32)],
        compiler_params=pltpu.CompilerParams(dimension_semantics=("arbitrary",)),
    )(x, gain, wgate_t, wfox_t, wlat_t)


def _proj_bwd(dgates, dfox, dlat, wgate_t, wfox_t, wlat_t, x, gain, dres):
    t = x.shape[0]
    tm = min(PROJ_TILE, t)

    def body(dg_ref, df_ref, dl_ref, wg_ref, wf_ref, wl_ref, x_ref, g_ref, dres_ref, dx_ref, dgain_ref, n_ref):
        i = pl.program_id(0)
        dn = _dot(dg_ref[...], wg_ref[...]) + _dot(df_ref[...], wf_ref[...]) + _dot(dl_ref[...], wl_ref[...])
        xv, g = x_ref[...], g_ref[...]
        xn, r = _rms_fwd(xv, g)
        n_ref[...] = xn.astype(BF16)
        dx, dg_rows = _rms_bwd(dn, xv, g, r)
        dx_ref[...] = dres_ref[...] + dx
        dgn = _colsum(dg_rows)

        @pl.when(i == 0)
        def _():
            dgain_ref[...] = dgn

        @pl.when(i > 0)
        def _():
            dgain_ref[...] += dgn

    tok = _tok(tm, D_MODEL)
    return pl.pallas_call(
        body, name="proj_bwd", grid=(t // tm,),
        in_specs=[_tok(tm, 2 * D_MODEL), _tok(tm, 3 * 512), _tok(tm, LAT_W), _full(wgate_t.shape), _full(wfox_t.shape),
                  _full(wlat_t.shape), tok, _full((1, D_MODEL)), tok],
        out_specs=[tok, _full((1, D_MODEL)), tok],
        out_shape=[_sds((t, D_MODEL), F32), _sds((1, D_MODEL), F32), _sds((t, D_MODEL), BF16)],
        compiler_params=pltpu.CompilerParams(dimension_semantics=("arbitrary",)),
    )(dgates, dfox, dlat, wgate_t, wfox_t, wlat_t, x, gain, dres)


def _tri(n, lower):
    r = lax.broadcasted_iota(jnp.int32, (n, n), 0)
    c = lax.broadcasted_iota(jnp.int32, (n, n), 1)
    return ((c <= r) if lower else (c >= r)).astype(F32)


def _log_sigmoid(z):
    return jnp.minimum(z, 0.0) - jnp.log1p(jnp.exp(-jnp.abs(z)))


def _rope_pair_fwd(p1, p2, g1, g2, cos, sin):
    ss = jnp.sum(p1 * p1, axis=-1, keepdims=True) + jnp.sum(p2 * p2, axis=-1, keepdims=True)
    r = lax.rsqrt(ss * (1.0 / ROPE_DIM) + RMS_EPS)
    n1, n2 = p1 * r, p2 * r
    y1, y2 = n1 * g1, n2 * g2
    return y1 * cos - y2 * sin, y2 * cos + y1 * sin, n1, n2, r


def _rope_pair_bwd(do1, do2, n1, n2, r, g1, g2, cos, sin):
    dy1 = do1 * cos + do2 * sin
    dy2 = do2 * cos - do1 * sin
    dyg1, dyg2 = dy1 * g1, dy2 * g2
    mu = (jnp.sum(dyg1 * n1, axis=-1, keepdims=True) + jnp.sum(dyg2 * n2, axis=-1, keepdims=True)) * (1.0 / ROPE_DIM)
    return r * (dyg1 - n1 * mu), r * (dyg2 - n2 * mu), dy1 * n1, dy2 * n2


def _prep_fwd(fox, lat, cos, sin, wqb_t, wkvb_t, g_qlat, g_kvlat, g_qn, g_qr, g_kn, g_kr, g_fq, g_fk, b_f):
    t = fox.shape[0]
    tm = min(PROJ_TILE, t)
    hd, lanes = HEAD_DIM, MLA_QK_LANES

    def body(fox_ref, lat_ref, cos_ref, sin_ref, wqb_ref, wkvb_ref, gql_ref, gkvl_ref, gqn_ref, gqr_ref, gkn_ref,
             gkr_ref, gfq_ref, gfk_ref, bf_ref, fq_ref, fk_ref, fv_ref, c_ref, mq_ref, mk_ref, mv_ref,
             carry, qscr, kscr):
        i = pl.program_id(0)
        cos_v, sin_v = cos_ref[...], sin_ref[...]

        for h in range(N_HEADS):
            w = slice(h * hd, (h + 1) * hd)
            qn, _ = _rms_fwd(fox_ref[:, h * hd:(h + 1) * hd], gfq_ref[...])
            fq_ref[:, w] = (qn * FOX_SCALE).astype(BF16)
            kn, _ = _rms_fwd(fox_ref[:, 512 + h * hd:512 + (h + 1) * hd], gfk_ref[...])
            fk_ref[:, w] = kn.astype(BF16)
        fv_ref[...] = fox_ref[:, 1024:1536].astype(BF16)

        @pl.when(i == 0)
        def _():
            carry[...] = jnp.zeros_like(carry)

        logf = _log_sigmoid(lat_ref[:, LAT_F:LAT_F + N_HEADS] + bf_ref[...])
        c_ref[...] = jnp.dot(_tri(tm, True), logf, precision=lax.Precision.HIGHEST, preferred_element_type=F32) + carry[...]
        carry[...] += _colsum(logf)

        qlat_n, _ = _rms_fwd(lat_ref[:, LAT_Q:LAT_Q + Q_LORA], gql_ref[...])
        qlat_b = qlat_n.astype(BF16)
        g1, g2 = gqr_ref[:, 0:ROPE_HALF], gqr_ref[:, ROPE_HALF:ROPE_DIM]
        qscr[...] = jnp.zeros_like(qscr)
        for h in range(N_HEADS):
            base = h * (hd + ROPE_DIM)
            pn = _dot(qlat_b, wqb_ref[base:base + hd, :], NT)
            p1 = _dot(qlat_b, wqb_ref[base + hd:base + hd + ROPE_HALF, :], NT)
            p2 = _dot(qlat_b, wqb_ref[base + hd + ROPE_HALF:base + hd + ROPE_DIM, :], NT)
            nope, _ = _rms_fwd(pn, gqn_ref[...])
            o1, o2, _, _, _ = _rope_pair_fwd(p1, p2, g1, g2, cos_v, sin_v)
            qscr[:, h * lanes:h * lanes + hd] = nope
            qscr[:, h * lanes + hd:h * lanes + hd + ROPE_HALF] = o1
            qscr[:, h * lanes + hd + ROPE_HALF:h * lanes + hd + ROPE_DIM] = o2
        mq_ref[...] = qscr[...].astype(BF16)

        kv_n, _ = _rms_fwd(lat_ref[:, LAT_KV:LAT_KV + KV_LORA], gkvl_ref[...])
        kv_b = kv_n.astype(BF16)
        kr1, kr2, _, _, _ = _rope_pair_fwd(lat_ref[:, LAT_KR:LAT_KR + ROPE_HALF], lat_ref[:, LAT_KR + ROPE_HALF:LAT_KR + ROPE_DIM],
                                           gkr_ref[:, 0:ROPE_HALF], gkr_ref[:, ROPE_HALF:ROPE_DIM], cos_v, sin_v)
        kscr[...] = jnp.zeros_like(kscr)
        for h in range(N_HEADS):
            pk = _dot(kv_b, wkvb_ref[h * 2 * hd:h * 2 * hd + hd, :], NT)
            kn, _ = _rms_fwd(pk, gkn_ref[...])
            kscr[:, h * lanes:h * lanes + hd] = kn
            kscr[:, h * lanes + hd:h * lanes + hd + ROPE_HALF] = kr1
            kscr[:, h * lanes + hd + ROPE_HALF:h * lanes + hd + ROPE_DIM] = kr2
            mv_ref[:, h * hd:(h + 1) * hd] = _dot(kv_b, wkvb_ref[h * 2 * hd + hd:(h + 1) * 2 * hd, :], NT).astype(BF16)
        mk_ref[...] = kscr[...].astype(BF16)

    small = [g_qlat, g_kvlat, g_qn, g_qr, g_kn, g_kr, g_fq, g_fk, b_f]
    t512 = _tok(tm, 512)
    return pl.pallas_call(
        body, name="prep_fwd", grid=(t // tm,),
        in_specs=[_tok(tm, 1536), _tok(tm, LAT_W), _tok(tm, ROPE_HALF), _tok(tm, ROPE_HALF), _full(wqb_t.shape),
                  _full(wkvb_t.shape)] + [_full(s.shape) for s in small],
        out_specs=[t512, t512, t512, _tok(tm, N_HEADS), _tok(tm, N_HEADS * lanes), _tok(tm, N_HEADS * lanes), t512],
        out_shape=[_sds((t, 512), BF16), _sds((t, 512), BF16), _sds((t, 512), BF16), _sds((t, N_HEADS), F32),
                   _sds((t, N_HEADS * lanes), BF16), _sds((t, N_HEADS * lanes), BF16), _sds((t, 512), BF16)],
        scratch_shapes=[pltpu.VMEM((1, N_HEADS), F32), pltpu.VMEM((tm, N_HEADS * lanes), F32),
                        pltpu.VMEM((tm, N_HEADS * lanes), F32)],
        compiler_params=pltpu.CompilerParams(dimension_semantics=("arbitrary",)),
    )(fox, lat, cos, sin, wqb_t, wkvb_t, *small)


def _prep_bwd(fox, lat, cos, sin, wqb_t, wkvb_t, g_qlat, g_kvlat, g_qn, g_qr, g_kn, g_kr, g_fq, g_fk, b_f,
              dfq, dfk, dfv, dc, dmq, dmk, dmv):
    t = fox.shape[0]
    tm = min(PROJ_TILE, t)
    n_t = t // tm
    hd, lanes = HEAD_DIM, MLA_QK_LANES

    def body(fox_ref, lat_ref, cos_ref, sin_ref, wqb_ref, wkvb_ref, gql_ref, gkvl_ref, gqn_ref, gqr_ref, gkn_ref,
             gkr_ref, gfq_ref, gfk_ref, bf_ref, dfq_ref, dfk_ref, dfv_ref, dc_ref, dmq_ref, dmk_ref, dmv_ref,
             dfox_ref, dlat_ref, dwqb_ref, dwkvb_ref, o_gql, o_gkvl, o_gqn, o_gqr, o_gkn, o_gkr, o_gfq, o_gfk, o_bf,
             carry, lscr, wq_acc, wkv_acc):
        i = pl.program_id(0)
        cos_v, sin_v = cos_ref[...], sin_ref[...]
        small_out = [o_gql, o_gkvl, o_gqn, o_gqr, o_gkn, o_gkr, o_gfq, o_gfk, o_bf]

        @pl.when(i == 0)
        def _():
            carry[...] = jnp.zeros_like(carry)
            wq_acc[...] = jnp.zeros_like(wq_acc)
            wkv_acc[...] = jnp.zeros_like(wkv_acc)
            for o in small_out:
                o[...] = jnp.zeros_like(o)

        d_gfq = jnp.zeros((1, hd), F32)
        d_gfk = jnp.zeros((1, hd), F32)
        for h in range(N_HEADS):
            w = slice(h * hd, (h + 1) * hd)
            xq = fox_ref[:, h * hd:(h + 1) * hd]
            rq = lax.rsqrt(jnp.mean(xq * xq, axis=-1, keepdims=True) + RMS_EPS)
            dxq, gq_rows = _rms_bwd(dfq_ref[:, w] * FOX_SCALE, xq, gfq_ref[...], rq)
            dfox_ref[:, w] = dxq.astype(BF16)
            d_gfq += _colsum(gq_rows)
            xk = fox_ref[:, 512 + h * hd:512 + (h + 1) * hd]
            rk = lax.rsqrt(jnp.mean(xk * xk, axis=-1, keepdims=True) + RMS_EPS)
            dxk, gk_rows = _rms_bwd(dfk_ref[:, w], xk, gfk_ref[...], rk)
            dfox_ref[:, 512 + h * hd:512 + (h + 1) * hd] = dxk.astype(BF16)
            d_gfk += _colsum(gk_rows)
        dfox_ref[:, 1024:1536] = dfv_ref[...].astype(BF16)
        o_gfq[...] += d_gfq
        o_gfk[...] += d_gfk

        lscr[...] = jnp.zeros_like(lscr)

        dcv = dc_ref[...]
        dlogf = jnp.dot(_tri(tm, False), dcv, precision=lax.Precision.HIGHEST, preferred_element_type=F32) + carry[...]
        carry[...] += _colsum(dcv)
        z = lat_ref[:, LAT_F:LAT_F + N_HEADS] + bf_ref[...]
        dz = dlogf * jax.nn.sigmoid(-z)
        lscr[:, LAT_F:LAT_F + N_HEADS] = dz
        o_bf[...] += _colsum(dz)

        xql = lat_ref[:, LAT_Q:LAT_Q + Q_LORA]
        qlat_n, r_ql = _rms_fwd(xql, gql_ref[...])
        qlat_b = qlat_n.astype(BF16)
        g1, g2 = gqr_ref[:, 0:ROPE_HALF], gqr_ref[:, ROPE_HALF:ROPE_DIM]
        dqlat_n = jnp.zeros((tm, Q_LORA), F32)
        d_gqn = jnp.zeros((1, hd), F32)
        d_g1 = jnp.zeros((1, ROPE_HALF), F32)
        d_g2 = jnp.zeros((1, ROPE_HALF), F32)
        for h in range(N_HEADS):
            base = h * (hd + ROPE_DIM)
            rows_n = slice(base, base + hd)
            rows_1 = slice(base + hd, base + hd + ROPE_HALF)
            rows_2 = slice(base + hd + ROPE_HALF, base + hd + ROPE_DIM)
            pn = _dot(qlat_b, wqb_ref[rows_n, :], NT)
            p1 = _dot(qlat_b, wqb_ref[rows_1, :], NT)
            p2 = _dot(qlat_b, wqb_ref[rows_2, :], NT)
            r_n = lax.rsqrt(jnp.mean(pn * pn, axis=-1, keepdims=True) + RMS_EPS)
            _, _, n1, n2, r_r = _rope_pair_fwd(p1, p2, g1, g2, cos_v, sin_v)
            dpn, gn_rows = _rms_bwd(dmq_ref[:, h * lanes:h * lanes + hd], pn, gqn_ref[...], r_n)
            dp1, dp2, g1_rows, g2_rows = _rope_pair_bwd(
                dmq_ref[:, h * lanes + hd:h * lanes + hd + ROPE_HALF],
                dmq_ref[:, h * lanes + hd + ROPE_HALF:h * lanes + hd + ROPE_DIM], n1, n2, r_r, g1, g2, cos_v, sin_v)
            d_gqn += _colsum(gn_rows)
            d_g1 += _colsum(g1_rows)
            d_g2 += _colsum(g2_rows)
            dpn_b, dp1_b, dp2_b = dpn.astype(BF16), dp1.astype(BF16), dp2.astype(BF16)
            dqlat_n += _dot(dpn_b, wqb_ref[rows_n, :]) + _dot(dp1_b, wqb_ref[rows_1, :]) + _dot(dp2_b, wqb_ref[rows_2, :])
            wq_acc[rows_n, :] += _dot(dpn_b, qlat_b, TN)
            wq_acc[rows_1, :] += _dot(dp1_b, qlat_b, TN)
            wq_acc[rows_2, :] += _dot(dp2_b, qlat_b, TN)
        dxql, gql_rows = _rms_bwd(dqlat_n, xql, gql_ref[...], r_ql)
        lscr[:, LAT_Q:LAT_Q + Q_LORA] = dxql
        o_gql[...] += _colsum(gql_rows)
        o_gqn[...] += d_gqn
        o_gqr[:, 0:ROPE_HALF] += d_g1
        o_gqr[:, ROPE_HALF:ROPE_DIM] += d_g2

        xkv = lat_ref[:, LAT_KV:LAT_KV + KV_LORA]
        kv_n, r_kv = _rms_fwd(xkv, gkvl_ref[...])
        kv_b = kv_n.astype(BF16)
        dkv_n = jnp.zeros((tm, KV_LORA), F32)
        d_gkn = jnp.zeros((1, hd), F32)
        dkr1 = jnp.zeros((tm, ROPE_HALF), F32)
        dkr2 = jnp.zeros((tm, ROPE_HALF), F32)
        for h in range(N_HEADS):
            rows_k = slice(h * 2 * hd, h * 2 * hd + hd)
            rows_v = slice(h * 2 * hd + hd, (h + 1) * 2 * hd)
            pk = _dot(kv_b, wkvb_ref[rows_k, :], NT)
            r_k = lax.rsqrt(jnp.mean(pk * pk, axis=-1, keepdims=True) + RMS_EPS)
            dpk, gk_rows = _rms_bwd(dmk_ref[:, h * lanes:h * lanes + hd], pk, gkn_ref[...], r_k)
            d_gkn += _colsum(gk_rows)
            dkr1 += dmk_ref[:, h * lanes + hd:h * lanes + hd + ROPE_HALF]
            dkr2 += dmk_ref[:, h * lanes + hd + ROPE_HALF:h * lanes + hd + ROPE_DIM]
            dpk_b = dpk.astype(BF16)
            dv_b = dmv_ref[:, h * hd:(h + 1) * hd].astype(BF16)
            dkv_n += _dot(dpk_b, wkvb_ref[rows_k, :]) + _dot(dv_b, wkvb_ref[rows_v, :])
            wkv_acc[rows_k, :] += _dot(dpk_b, kv_b, TN)
            wkv_acc[rows_v, :] += _dot(dv_b, kv_b, TN)
        dxkv, gkvl_rows = _rms_bwd(dkv_n, xkv, gkvl_ref[...], r_kv)
        lscr[:, LAT_KV:LAT_KV + KV_LORA] = dxkv
        o_gkvl[...] += _colsum(gkvl_rows)
        o_gkn[...] += d_gkn

        gk1, gk2 = gkr_ref[:, 0:ROPE_HALF], gkr_ref[:, ROPE_HALF:ROPE_DIM]
        _, _, kn1, kn2, r_kr = _rope_pair_fwd(lat_ref[:, LAT_KR:LAT_KR + ROPE_HALF],
                                              lat_ref[:, LAT_KR + ROPE_HALF:LAT_KR + ROPE_DIM], gk1, gk2, cos_v, sin_v)
        dk1, dk2, gk1_rows, gk2_rows = _rope_pair_bwd(dkr1, dkr2, kn1, kn2, r_kr, gk1, gk2, cos_v, sin_v)
        lscr[:, LAT_KR:LAT_KR + ROPE_HALF] = dk1
        lscr[:, LAT_KR + ROPE_HALF:LAT_KR + ROPE_DIM] = dk2
        o_gkr[:, 0:ROPE_HALF] += _colsum(gk1_rows)
        o_gkr[:, ROPE_HALF:ROPE_DIM] += _colsum(gk2_rows)

        dlat_ref[...] = lscr[...].astype(BF16)

        @pl.when(i == n_t - 1)
        def _():
            dwqb_ref[...] = wq_acc[...].astype(BF16)
            dwkvb_ref[...] = wkv_acc[...].astype(BF16)

    small = [g_qlat, g_kvlat, g_qn, g_qr, g_kn, g_kr, g_fq, g_fk, b_f]

    def rtok(n):
        return pl.BlockSpec((tm, n), lambda i: (n_t - 1 - i, 0))

    return pl.pallas_call(
        body, name="prep_bwd", grid=(n_t,),
        in_specs=[rtok(1536), rtok(LAT_W), rtok(ROPE_HALF), rtok(ROPE_HALF), _full(wqb_t.shape), _full(wkvb_t.shape)]
        + [_full(s.shape) for s in small]
        + [rtok(512), rtok(512), rtok(512), rtok(N_HEADS), rtok(N_HEADS * lanes), rtok(N_HEADS * lanes), rtok(512)],
        out_specs=[rtok(1536), rtok(LAT_W), _full(wqb_t.shape), _full(wkvb_t.shape)] + [_full(s.shape) for s in small],
        out_shape=[_sds((t, 1536), BF16), _sds((t, LAT_W), BF16), _sds(wqb_t.shape, BF16), _sds(wkvb_t.shape, BF16)]
        + [_sds(s.shape, F32) for s in small],
        scratch_shapes=[pltpu.VMEM((1, N_HEADS), F32), pltpu.VMEM((tm, LAT_W), F32), pltpu.VMEM(wqb_t.shape, F32),
                        pltpu.VMEM(wkvb_t.shape, F32)],
        compiler_params=pltpu.CompilerParams(dimension_semantics=("arbitrary",)),
    )(fox, lat, cos, sin, wqb_t, wkvb_t, *small, dfq, dfk, dfv, dc, dmq, dmk, dmv)


def _attn_fwd(q, k, v, c, c_t, *, lanes, scale, name):
    t = q.shape[0]
    tq = min(ATT_TILE, t)
    n_q = t // tq
    hd = HEAD_DIM
    decay = c is not None

    def body(*refs):
        if decay:
            q_ref, k_ref, v_ref, c_ref, ct_ref, o_ref, o32_ref, lse_ref, m_scr, l_scr, acc = refs
        else:
            q_ref, k_ref, v_ref, o_ref, o32_ref, lse_ref, m_scr, l_scr, acc = refs
        i, j = pl.program_id(0), pl.program_id(1)

        @pl.when(j == 0)
        def _():
            m_scr[...] = jnp.full_like(m_scr, MASK_VALUE)
            l_scr[...] = jnp.zeros_like(l_scr)
            acc[...] = jnp.zeros_like(acc)

        def step(diagonal):
            if diagonal:
                keep = lax.broadcasted_iota(jnp.int32, (tq, tq), 0) >= lax.broadcasted_iota(jnp.int32, (tq, tq), 1)
            for h in range(N_HEADS):
                s = _dot(q_ref[:, h * lanes:(h + 1) * lanes], k_ref[:, h * lanes:(h + 1) * lanes], NT)
                if scale != 1.0:
                    s = s * scale
                if decay:
                    s = s + (c_ref[:, h:h + 1] - ct_ref[h:h + 1, :])
                if diagonal:
                    s = jnp.where(keep, s, MASK_VALUE)
                m_prev = m_scr[h]
                m_new = jnp.maximum(m_prev, jnp.max(s, axis=1, keepdims=True))
                alpha = jnp.exp(m_prev - m_new)
                p = jnp.exp(s - m_new)
                l_scr[h] = alpha * l_scr[h] + jnp.sum(p, axis=1, keepdims=True)
                w = slice(h * hd, (h + 1) * hd)
                p_b = p.astype(BF16)
                acc[0, :, w] = alpha * acc[0, :, w] + _dot(p_b, v_ref[:, w])
                if decay:
                    p_lo = (p - p_b.astype(F32)).astype(BF16)
                    acc[1, :, w] = alpha * acc[1, :, w] + _dot(p_lo, v_ref[:, w])
                m_scr[h] = m_new

        @pl.when(j < i)
        def _():
            step(False)

        @pl.when(j == i)
        def _():
            step(True)
            for h in range(N_HEADS):
                w = slice(h * hd, (h + 1) * hd)
                l = l_scr[h]
                o_ref[:, w] = (acc[0, :, w] / l).astype(BF16)
                o32_ref[:, w] = ((acc[0, :, w] + acc[1, :, w]) if decay else acc[0, :, w]) / l
                lse_ref[:, h:h + 1] = m_scr[h] + jnp.log(l)

    qspec = lambda n: pl.BlockSpec((tq, n), lambda i, j: (i, 0))
    kspec = lambda n: pl.BlockSpec((tq, n), lambda i, j: (jnp.minimum(i, j), 0))
    in_specs = [qspec(N_HEADS * lanes), kspec(N_HEADS * lanes), kspec(512)]
    args = [q, k, v]
    if decay:
        in_specs += [qspec(N_HEADS), pl.BlockSpec((N_HEADS, tq), lambda i, j: (0, jnp.minimum(i, j)))]
        args += [c, c_t]
    return pl.pallas_call(
        body, name=name, grid=(n_q, n_q), in_specs=in_specs, out_specs=[qspec(512), qspec(512), qspec(N_HEADS)],
        out_shape=[_sds((t, 512), BF16), _sds((t, 512), F32), _sds((t, N_HEADS), F32)],
        scratch_shapes=[pltpu.VMEM((N_HEADS, tq, 1), F32), pltpu.VMEM((N_HEADS, tq, 1), F32), pltpu.VMEM((2, tq, 512), F32)],
        compiler_params=pltpu.CompilerParams(dimension_semantics=("arbitrary", "arbitrary")),
    )(*args)


def _attn_bwd(q, k, v, do, delta, lse, c, c_t, *, lanes, scale, name):
    t = q.shape[0]
    tq = min(ATT_TILE, t)
    n_q = t // tq
    hd = HEAD_DIM
    decay = c is not None

    def body(*refs):
        if decay:
            q_ref, k_ref, v_ref, do_ref, delta_ref, lse_ref, c_ref, ct_ref, dq_hbm, dk_ref, dv_ref, dct_ref, dq_ref = refs
        else:
            q_ref, k_ref, v_ref, do_ref, delta_ref, lse_ref, dq_hbm, dk_ref, dv_ref, dq_ref = refs
        j, i = pl.program_id(0), pl.program_id(1)

        @pl.when((j == 0) & (i == 0))
        def _():
            dq_ref[...] = jnp.zeros_like(dq_ref)

        @pl.when(i == j)
        def _():
            dk_ref[...] = jnp.zeros_like(dk_ref)
            dv_ref[...] = jnp.zeros_like(dv_ref)
            if decay:
                dct_ref[...] = jnp.zeros_like(dct_ref)

        def step(diagonal):
            if diagonal:
                keep = lax.broadcasted_iota(jnp.int32, (tq, tq), 0) >= lax.broadcasted_iota(jnp.int32, (tq, tq), 1)
            rows = pl.ds(pl.multiple_of(i * tq, tq), tq)
            for h in range(N_HEADS):
                wl = slice(h * lanes, (h + 1) * lanes)
                w = slice(h * hd, (h + 1) * hd)
                qh, kh = q_ref[:, wl], k_ref[:, wl]
                s = _dot(qh, kh, NT)
                if scale != 1.0:
                    s = s * scale
                if decay:
                    s = s + (c_ref[:, h:h + 1] - ct_ref[h:h + 1, :])
                if diagonal:
                    s = jnp.where(keep, s, MASK_VALUE)
                p = jnp.exp(s - lse_ref[:, h:h + 1])
                doh = do_ref[:, w]
                dv_ref[:, w] += _dot(p.astype(BF16), doh, TN)
                dp = _dot(doh, v_ref[:, w], NT)
                ds = p * (dp - delta_ref[:, h:h + 1])
                if decay:
                    dct_ref[h:h + 1, :] -= _colsum(ds)
                if scale != 1.0:
                    ds = ds * scale
                ds_b = ds.astype(BF16)
                dk_ref[:, wl] += _dot(ds_b, qh, TN)
                dq_ref[rows, wl] += _dot(ds_b, kh)

        @pl.when(i > j)
        def _():
            step(False)

        @pl.when(i == j)
        def _():
            step(True)

        @pl.when((j == n_q - 1) & (i == n_q - 1))
        def _():
            pltpu.sync_copy(dq_ref, dq_hbm)

    qspec = lambda n: pl.BlockSpec((tq, n), lambda j, i: (jnp.maximum(i, j), 0))
    kspec = lambda n: pl.BlockSpec((tq, n), lambda j, i: (j, 0))
    in_specs = [qspec(N_HEADS * lanes), kspec(N_HEADS * lanes), kspec(512), qspec(512), qspec(N_HEADS), qspec(N_HEADS)]
    out_specs = [pl.BlockSpec(memory_space=pl.ANY), kspec(N_HEADS * lanes), kspec(512)]
    out_shape = [_sds((t, N_HEADS * lanes), F32), _sds((t, N_HEADS * lanes), F32), _sds((t, 512), F32)]
    args = [q, k, v, do, delta, lse]
    if decay:
        ctspec = pl.BlockSpec((N_HEADS, tq), lambda j, i: (0, j))
        in_specs += [qspec(N_HEADS), ctspec]
        out_specs.append(ctspec)
        out_shape.append(_sds((N_HEADS, t), F32))
        args += [c, c_t]
    return pl.pallas_call(
        body, name=name, grid=(n_q, n_q), in_specs=in_specs, out_specs=out_specs, out_shape=out_shape,
        scratch_shapes=[pltpu.VMEM((t, N_HEADS * lanes), F32)],
        compiler_params=pltpu.CompilerParams(dimension_semantics=("arbitrary", "arbitrary")),
    )(*args)


def _mix_fwd(x, y_mla, y_fox, gates, b_gate, wbm_t, wbf_t, wo):
    t = x.shape[0]
    tm = min(PROJ_TILE, t)

    def body(x_ref, ym_ref, yf_ref, gt_ref, bg_ref, wbm_ref, wbf_ref, wo_ref, out_ref):
        um = _dot(ym_ref[...], wbm_ref[...], NT)
        uf = _dot(yf_ref[...], wbf_ref[...], NT)
        sm = jax.nn.sigmoid(gt_ref[:, 0:D_MODEL] + bg_ref[0:1, :])
        sf = jax.nn.sigmoid(gt_ref[:, D_MODEL:2 * D_MODEL] + bg_ref[1:2, :])
        mixed = sm * um + sf * uf
        out_ref[...] = x_ref[...] + _dot(mixed.astype(BF16), wo_ref[...])

    tok = _tok(tm, D_MODEL)
    return pl.pallas_call(
        body, name="mix_fwd", grid=(t // tm,),
        in_specs=[tok, _tok(tm, 512), _tok(tm, 512), _tok(tm, 2 * D_MODEL), _full((2, D_MODEL)), _full(wbm_t.shape),
                  _full(wbf_t.shape), _full(wo.shape)],
        out_specs=tok, out_shape=_sds((t, D_MODEL), F32),
        compiler_params=pltpu.CompilerParams(dimension_semantics=("arbitrary",)),
    )(x, y_mla, y_fox, gates, b_gate, wbm_t, wbf_t, wo)


def _mix_bwd(dx, y_mla, y_fox, y_mla32, y_fox32, gates, b_gate, wbm_t, wbf_t, wo):
    t = dx.shape[0]
    tm = min(PROJ_TILE, t)

    def body(dx_ref, ym_ref, yf_ref, ym32_ref, yf32_ref, gt_ref, bg_ref, wbm_ref, wbf_ref, wo_ref,
             dym_ref, dyf_ref, dlm_ref, dlf_ref, dgt_ref, mixed_ref, dum_ref, duf_ref, dxb_ref, dbg_ref, prod):
        i = pl.program_id(0)
        dxb = dx_ref[...].astype(BF16)
        dxb_ref[...] = dxb
        dmixed = _dot(dxb, wo_ref[...], NT)
        um = _dot(ym_ref[...], wbm_ref[...], NT)
        uf = _dot(yf_ref[...], wbf_ref[...], NT)
        sm = jax.nn.sigmoid(gt_ref[:, 0:D_MODEL] + bg_ref[0:1, :])
        sf = jax.nn.sigmoid(gt_ref[:, D_MODEL:2 * D_MODEL] + bg_ref[1:2, :])
        mixed_ref[...] = (sm * um + sf * uf).astype(BF16)
        dum = (dmixed * sm).astype(BF16)
        duf = (dmixed * sf).astype(BF16)
        dum_ref[...] = dum
        duf_ref[...] = duf
        dgm = dmixed * um * (sm * (1.0 - sm))
        dgf = dmixed * uf * (sf * (1.0 - sf))
        dgt_ref[:, 0:D_MODEL] = dgm.astype(BF16)
        dgt_ref[:, D_MODEL:2 * D_MODEL] = dgf.astype(BF16)
        for du, wb_ref, y32_ref, dy_ref, dl_ref in ((dum, wbm_ref, ym32_ref, dym_ref, dlm_ref),
                                                    (duf, wbf_ref, yf32_ref, dyf_ref, dlf_ref)):
            dy = _dot(du, wb_ref[...])
            dy_ref[...] = dy.astype(BF16)
            prod[...] = dy.astype(BF16).astype(F32) * y32_ref[...]
            for h in range(N_HEADS):
                dl_ref[:, h:h + 1] = jnp.sum(prod[:, h * HEAD_DIM:(h + 1) * HEAD_DIM], axis=1, keepdims=True)

        @pl.when(i == 0)
        def _():
            dbg_ref[...] = jnp.zeros_like(dbg_ref)

        dbg_ref[0:1, :] += _colsum(dgm)
        dbg_ref[1:2, :] += _colsum(dgf)

    tok = _tok(tm, D_MODEL)
    tokb = _sds((t, D_MODEL), BF16)
    t512, t8 = _tok(tm, 512), _tok(tm, N_HEADS)
    return pl.pallas_call(
        body, name="mix_bwd", grid=(t // tm,),
        in_specs=[tok, t512, t512, t512, t512, _tok(tm, 2 * D_MODEL), _full((2, D_MODEL)), _full(wbm_t.shape),
                  _full(wbf_t.shape), _full(wo.shape)],
        out_specs=[t512, t512, t8, t8, _tok(tm, 2 * D_MODEL), tok, tok, tok, tok, _full((2, D_MODEL))],
        out_shape=[_sds((t, 512), BF16), _sds((t, 512), BF16), _sds((t, N_HEADS), F32), _sds((t, N_HEADS), F32),
                   _sds((t, 2 * D_MODEL), BF16), tokb, tokb, tokb, tokb, _sds((2, D_MODEL), F32)],
        scratch_shapes=[pltpu.VMEM((tm, 512), F32)],
        compiler_params=pltpu.CompilerParams(dimension_semantics=("arbitrary",)),
    )(dx, y_mla, y_fox, y_mla32, y_fox32, gates, b_gate, wbm_t, wbf_t, wo)


def _my_position():
    x, y, c = lax.axis_index("x"), lax.axis_index("y"), lax.axis_index("c")
    return x, y, c, 4 * x + 2 * y + c


def _peer(x, y, c, mask):
    px = 1 - x if mask & 4 else x
    py = 1 - y if mask & 2 else y
    pc = 1 - c if mask & 1 else c
    return (px, py, pc), 4 * px + 2 * py + pc


def _chip_peer(x, y, km):
    px = 1 - x if km & 2 else x
    py = 1 - y if km & 1 else y
    return px, py, 2 * px + py


_HBM = pl.BlockSpec(memory_space=pl.ANY)


def _wait_all(copies):
    for cp in copies:
        cp.wait()


def _gather_over_chips(pieces, name):
    n_arr = len(pieces)

    def body(*refs):
        srcs, dsts = refs[:n_arr], refs[n_arr:2 * n_arr]
        send_sems, recv_sems, local_sems = refs[2 * n_arr:]
        x, y, c, me = _my_position()
        copies = []
        for a in range(n_arr):
            r = srcs[a].shape[0]
            mine = dsts[a].at[pl.ds(me * r, r)]
            copies.append(pltpu.make_async_copy(srcs[a], mine, local_sems.at[a]))
            for km in range(1, N_CHIP):
                px, py, _ = _chip_peer(x, y, km)
                copies.append(pltpu.make_async_remote_copy(
                    src_ref=srcs[a], dst_ref=mine, send_sem=send_sems.at[a, km], recv_sem=recv_sems.at[a, km],
                    device_id=(px, py, c), device_id_type=MESH))
        for cp in copies:
            cp.start()
        _wait_all(copies)

    return pl.pallas_call(
        body, name=name, in_specs=[_HBM] * n_arr, out_specs=[_HBM] * n_arr,
        out_shape=[_sds((N_DEV * p.shape[0],) + p.shape[1:], p.dtype) for p in pieces],
        scratch_shapes=[pltpu.SemaphoreType.DMA((n_arr, N_CHIP)), pltpu.SemaphoreType.DMA((n_arr, N_CHIP)),
                        pltpu.SemaphoreType.DMA((n_arr,))],
    )(*pieces)


def _gather_over_cores(arrays, name):
    n_arr = len(arrays)

    def body(*refs):
        srcs, dsts = refs[:n_arr], refs[n_arr:2 * n_arr]
        send_sems, recv_sems = refs[2 * n_arr:]
        x, y, c, _ = _my_position()
        copies = []
        for a in range(n_arr):
            r = srcs[a].shape[0] // N_DEV
            for q in range(N_CHIP):
                rows = pl.ds((2 * q + c) * r, r)
                copies.append(pltpu.make_async_remote_copy(
                    src_ref=srcs[a].at[rows], dst_ref=dsts[a].at[rows], send_sem=send_sems.at[a, q],
                    recv_sem=recv_sems.at[a, q], device_id=(x, y, 1 - c), device_id_type=MESH))
        for cp in copies:
            cp.start()
        _wait_all(copies)

    return pl.pallas_call(
        body, name=name, in_specs=[_HBM] * n_arr, out_specs=[_HBM] * n_arr,
        out_shape=[_sds(a.shape, a.dtype) for a in arrays], input_output_aliases={a: a for a in range(n_arr)},
        scratch_shapes=[pltpu.SemaphoreType.DMA((n_arr, N_CHIP)), pltpu.SemaphoreType.DMA((n_arr, N_CHIP))],
    )(*arrays)


def _grads_to_sibling(grads, name):
    n_arr = len(grads)

    def body(*refs):
        srcs, dsts = refs[:n_arr], refs[n_arr:2 * n_arr]
        send_sems, recv_sems = refs[2 * n_arr:]
        x, y, c, _ = _my_position()
        copies = []
        for a in range(n_arr):
            r = srcs[a].shape[0] // N_DEV
            for q in range(N_CHIP):
                copies.append(pltpu.make_async_remote_copy(
                    src_ref=srcs[a].at[pl.ds((2 * q + 1 - c) * r, r)], dst_ref=dsts[a].at[q], send_sem=send_sems.at[a, q],
                    recv_sem=recv_sems.at[a, q], device_id=(x, y, 1 - c), device_id_type=MESH))
        for cp in copies:
            cp.start()
        _wait_all(copies)

    return pl.pallas_call(
        body, name=name, in_specs=[_HBM] * n_arr, out_specs=[_HBM] * n_arr,
        out_shape=[_sds((N_CHIP, g.shape[0] // N_DEV) + g.shape[1:], g.dtype) for g in grads],
        scratch_shapes=[pltpu.SemaphoreType.DMA((n_arr, N_CHIP)), pltpu.SemaphoreType.DMA((n_arr, N_CHIP))],
    )(*grads)


def _pair_sum(grad, from_sibling, name):
    r, n = from_sibling.shape[1:]

    def body(g_ref, s_ref, o_ref):
        c = lax.axis_index("c")
        o_ref[...] = (g_ref[c].astype(F32) + s_ref[...].astype(F32)).astype(BF16)

    return pl.pallas_call(
        body, name=name, grid=(N_CHIP,),
        in_specs=[pl.BlockSpec((None, 2, r, n), lambda q: (q, 0, 0, 0)), pl.BlockSpec((None, r, n), lambda q: (q, 0, 0))],
        out_specs=pl.BlockSpec((None, r, n), lambda q: (q, 0, 0)), out_shape=_sds((N_CHIP, r, n), BF16),
    )(grad.reshape(N_CHIP, 2, r, n), from_sibling)


def _grads_over_chips(sums, name):
    n_arr = len(sums)

    def body(*refs):
        srcs, dsts = refs[:n_arr], refs[n_arr:2 * n_arr]
        send_sems, recv_sems, local_sems = refs[2 * n_arr:]
        x, y, c, _ = _my_position()
        q_me = 2 * x + y
        copies = []
        for a in range(n_arr):
            copies.append(pltpu.make_async_copy(srcs[a].at[q_me], dsts[a].at[q_me], local_sems.at[a]))
            for km in range(1, N_CHIP):
                px, py, q_peer = _chip_peer(x, y, km)
                copies.append(pltpu.make_async_remote_copy(
                    src_ref=srcs[a].at[q_peer], dst_ref=dsts[a].at[q_me], send_sem=send_sems.at[a, km],
                    recv_sem=recv_sems.at[a, km], device_id=(px, py, c), device_id_type=MESH))
        for cp in copies:
            cp.start()
        _wait_all(copies)

    return pl.pallas_call(
        body, name=name, in_specs=[_HBM] * n_arr, out_specs=[_HBM] * n_arr,
        out_shape=[_sds(s.shape, s.dtype) for s in sums],
        scratch_shapes=[pltpu.SemaphoreType.DMA((n_arr, N_CHIP)), pltpu.SemaphoreType.DMA((n_arr, N_CHIP)),
                        pltpu.SemaphoreType.DMA((n_arr,))],
    )(*sums)


def _all_reduce_small(vec):
    r = vec.shape[0]

    def body(v_ref, o_ref, buf, send_sems, recv_sems):
        x, y, c, me = _my_position()
        buf[me] = v_ref[...]
        copies = []
        for mask in range(1, N_DEV):
            peer, _ = _peer(x, y, c, mask)
            cp = pltpu.make_async_remote_copy(src_ref=v_ref, dst_ref=buf.at[me], send_sem=send_sems.at[mask],
                                              recv_sem=recv_sems.at[mask], device_id=peer, device_id_type=MESH)
            cp.start()
            copies.append(cp)
        for cp in copies:
            cp.wait()
        total = buf[0]
        for s in range(1, N_DEV):
            total = total + buf[s]
        o_ref[...] = total

    vm = pl.BlockSpec(memory_space=pltpu.VMEM)
    return pl.pallas_call(
        body, name="all_reduce_small", in_specs=[vm], out_specs=vm, out_shape=_sds(vec.shape, F32),
        scratch_shapes=[pltpu.VMEM((N_DEV, r, 128), F32), pltpu.SemaphoreType.DMA((N_DEV,)), pltpu.SemaphoreType.DMA((N_DEV,))],
    )(vec)


def _adamw_math(w, g, m, v):
    m = ADAM_B1 * m + (1.0 - ADAM_B1) * g
    v = ADAM_B2 * v + (1.0 - ADAM_B2) * (g * g)
    m_hat = m / (1.0 - ADAM_B1 ** ADAM_STEP)
    v_hat = v / (1.0 - ADAM_B2 ** ADAM_STEP)
    delta = -ADAM_LR * (m_hat / (jnp.sqrt(v_hat) + ADAM_EPS) + ADAM_WD * w)
    return delta, m, v


def _reduce_adamw(slots, w, m, v, *, transpose, name):
    r, n = slots.shape[1:]

    def body(s_ref, w_ref, m_ref, v_ref, g_ref, d_ref, nm_ref, nv_ref, *scr):
        g = s_ref[0].astype(F32)
        for s in range(1, slots.shape[0]):
            g = g + s_ref[s].astype(F32)
        if transpose:
            scr[0][...] = g.T
            g = scr[0][:, 0:w_ref.shape[1]]
        g_ref[...] = g
        d_ref[...], nm_ref[...], nv_ref[...] = _adamw_math(w_ref[...], g, m_ref[...], v_ref[...])

    out = _sds(w.shape, F32)
    return pl.pallas_call(
        body, name=name, out_shape=[out, out, out, out],
        scratch_shapes=[pltpu.VMEM((n, r), F32)] if transpose else [],
    )(slots, w, m, v)


def _adamw(g, w, m, v, name):
    def body(g_ref, w_ref, m_ref, v_ref, d_ref, nm_ref, nv_ref):
        d_ref[...], nm_ref[...], nv_ref[...] = _adamw_math(w_ref[...], g_ref[...], m_ref[...], v_ref[...])

    out = _sds(w.shape, F32)
    return pl.pallas_call(body, name=name, out_shape=[out, out, out])(g, w, m, v)


_SMALL = ["ffn1_norm", "mix_norm", "ffn2_norm", "mla_q_lat_norm", "mla_kv_lat_norm", "mla_q_nope_gain", "mla_q_rope_gain",
          "mla_k_nope_gain", "mla_k_rope_gain", "fox_q_gain", "fox_k_gain", "fox_b_f"]
_WEIGHTS = ["ffn1_norm", "ffn1_w_gate", "ffn1_w_up", "ffn1_w_down", "mix_norm", "w_in", "mla_q_lat_norm", "mla_w_qb",
            "mla_kv_lat_norm", "mla_w_kvb", "mla_q_nope_gain", "mla_q_rope_gain", "mla_k_nope_gain", "mla_k_rope_gain",
            "fox_q_gain", "fox_k_gain", "fox_b_f", "w_branch_mla", "w_branch_fox", "b_gate", "w_o", "ffn2_norm",
            "ffn2_w_gate", "ffn2_w_up", "ffn2_w_down"]
_IN_Q, _IN_KV, _IN_KR, _IN_FOX, _IN_F, _IN_GATES = (0, 192), (192, 128), (320, 32), (352, 1536), (1888, 8), (1896, 2048)


def _rows(a, seg):
    return a[seg[0]:seg[0] + seg[1]]


def _split_w_in(win_t):
    z = lambda n: jnp.zeros((n, D_MODEL), win_t.dtype)
    lat = jnp.concatenate([_rows(win_t, _IN_Q), z(LAT_KV - Q_LORA), _rows(win_t, _IN_KV), _rows(win_t, _IN_KR),
                           _rows(win_t, _IN_F), z(LAT_W - LAT_F - N_HEADS)], axis=0)
    return _rows(win_t, _IN_GATES), _rows(win_t, _IN_FOX), lat


def _join_w_in(d_gates, d_fox, d_lat):
    return jnp.concatenate([d_lat[LAT_Q:LAT_Q + Q_LORA], d_lat[LAT_KV:LAT_KV + KV_LORA], d_lat[LAT_KR:LAT_KR + ROPE_DIM],
                            d_fox, d_lat[LAT_F:LAT_F + N_HEADS], d_gates], axis=0)


def kernel(x, positions, ffn1_norm, ffn1_w_gate, ffn1_w_up, ffn1_w_down, mix_norm, w_in, mla_q_lat_norm, mla_w_qb, mla_kv_lat_norm, mla_w_kvb, mla_q_nope_gain, mla_q_rope_gain, mla_k_nope_gain, mla_k_rope_gain, fox_q_gain, fox_k_gain, fox_b_f, w_branch_mla, w_branch_fox, b_gate, w_o, ffn2_norm, ffn2_w_gate, ffn2_w_up, ffn2_w_down, loss_target, m_ffn1_norm, m_ffn1_w_gate, m_ffn1_w_up, m_ffn1_w_down, m_mix_norm, m_w_in, m_mla_q_lat_norm, m_mla_w_qb, m_mla_kv_lat_norm, m_mla_w_kvb, m_mla_q_nope_gain, m_mla_q_rope_gain, m_mla_k_nope_gain, m_mla_k_rope_gain, m_fox_q_gain, m_fox_k_gain, m_fox_b_f, m_w_branch_mla, m_w_branch_fox, m_b_gate, m_w_o, m_ffn2_norm, m_ffn2_w_gate, m_ffn2_w_up, m_ffn2_w_down, v_ffn1_norm, v_ffn1_w_gate, v_ffn1_w_up, v_ffn1_w_down, v_mix_norm, v_w_in, v_mla_q_lat_norm, v_mla_w_qb, v_mla_kv_lat_norm, v_mla_w_kvb, v_mla_q_nope_gain, v_mla_q_rope_gain, v_mla_k_nope_gain, v_mla_k_rope_gain, v_fox_q_gain, v_fox_k_gain, v_fox_b_f, v_w_branch_mla, v_w_branch_fox, v_b_gate, v_w_o, v_ffn2_norm, v_ffn2_w_gate, v_ffn2_w_up, v_ffn2_w_down):
    env = dict(locals())
    strip = lambda n, a: a if n in _SMALL else a[0]
    W = {n: strip(n, env[n]) for n in _WEIGHTS}
    M = {n: strip(n, env["m_" + n]) for n in _WEIGHTS}
    V = {n: strip(n, env["v_" + n]) for n in _WEIGHTS}
    xs = x[0]
    t = xs.shape[0]

    col_split = ["ffn1_w_gate", "ffn1_w_up", "ffn2_w_gate", "ffn2_w_up", "mla_w_qb", "mla_w_kvb", "w_branch_mla", "w_branch_fox"]
    row_split = ["ffn1_w_down", "ffn2_w_down", "w_o"]
    pieces = {n: W[n].T.astype(BF16) for n in col_split}
    pieces.update({n: W[n].astype(BF16) for n in row_split})
    pieces["w_in"] = jnp.pad(W["w_in"].T.astype(BF16), ((0, W_IN_PIECE_PAD - W_IN_PIECE), (0, 0)))
    pieces["b_gate"] = W["b_gate"].T
    order = col_split + row_split + ["w_in", "b_gate"]
    G = dict(zip(order, _gather_over_cores(_gather_over_chips([pieces[n] for n in order], "gather_ici"), "gather_d2d")))
    win_t = G["w_in"].reshape(N_DEV, W_IN_PIECE_PAD, D_MODEL)[:, :W_IN_PIECE].reshape(N_DEV * W_IN_PIECE, D_MODEL)
    wgate_t, wfox_t, wlat_t = _split_w_in(win_t)
    bg = G["b_gate"].T

    inv_freq = ROPE_THETA ** (-jnp.arange(ROPE_HALF, dtype=F32) / ROPE_HALF)
    ang = positions[0].astype(F32)[:, None] * inv_freq
    cos, sin = jnp.cos(ang), jnp.sin(ang)
    prep_small = [W[n] for n in ["mla_q_lat_norm", "mla_kv_lat_norm", "mla_q_nope_gain", "mla_q_rope_gain", "mla_k_nope_gain",
                                 "mla_k_rope_gain", "fox_q_gain", "fox_k_gain", "fox_b_f"]]

    x1, a1, b1 = _ffn_fwd(xs, W["ffn1_norm"], G["ffn1_w_gate"], G["ffn1_w_up"], G["ffn1_w_down"])
    gates, fox, lat = _proj_fwd(x1, W["mix_norm"], wgate_t, wfox_t, wlat_t)
    fq, fk, fv, c, mq, mk, mv = _prep_fwd(fox, lat, cos, sin, G["mla_w_qb"], G["mla_w_kvb"], *prep_small)
    c_t = c.T
    y_fox, y_fox32, lse_fox = _attn_fwd(fq, fk, fv, c, c_t, lanes=HEAD_DIM, scale=1.0, name="fox_fwd")
    y_mla, y_mla32, lse_mla = _attn_fwd(mq, mk, mv, None, None, lanes=MLA_QK_LANES, scale=MLA_SCALE, name="mla_fwd")
    x2 = _mix_fwd(x1, y_mla, y_fox, gates, bg, G["w_branch_mla"], G["w_branch_fox"], G["w_o"])
    dx3, a2, b2, loss_vec = _ffn_fwd(x2, W["ffn2_norm"], G["ffn2_w_gate"], G["ffn2_w_up"], G["ffn2_w_down"], target=loss_target[0])

    dx2, dg_ffn2, da2, db2, h2, n2, dyh2 = _ffn_bwd(dx3, x2, W["ffn2_norm"], a2, b2, G["ffn2_w_gate"], G["ffn2_w_up"],
                                                   G["ffn2_w_down"], "ffn2_bwd")
    grads = {"ffn2_w_gate": _tn_matmul(da2, n2, "ffn2_dgate"), "ffn2_w_up": _tn_matmul(db2, n2, "ffn2_dup"),
             "ffn2_w_down": _tn_matmul(h2, dyh2, "ffn2_ddown")}
    dy_mla, dy_fox, delta_mla, delta_fox, dgates, mixed, dum, duf, dx2b, dbg = _mix_bwd(
        dx2, y_mla, y_fox, y_mla32, y_fox32, gates, bg, G["w_branch_mla"], G["w_branch_fox"], G["w_o"])
    grads["w_o"] = _tn_matmul(mixed, dx2b, "d_w_o")
    grads["w_branch_mla"] = _tn_matmul(dum, y_mla, "d_w_branch_mla")
    grads["w_branch_fox"] = _tn_matmul(duf, y_fox, "d_w_branch_fox")
    dfq, dfk, dfv, dc_t = _attn_bwd(fq, fk, fv, dy_fox, delta_fox, lse_fox, c, c_t, lanes=HEAD_DIM, scale=1.0, name="fox_bwd")
    dmq, dmk, dmv = _attn_bwd(mq, mk, mv, dy_mla, delta_mla, lse_mla, None, None, lanes=MLA_QK_LANES, scale=MLA_SCALE, name="mla_bwd")
    prep_out = _prep_bwd(fox, lat, cos, sin, G["mla_w_qb"], G["mla_w_kvb"], *prep_small, dfq, dfk, dfv, dc_t.T, dmq, dmk, dmv)
    dfox, dlat, grads["mla_w_qb"], grads["mla_w_kvb"] = prep_out[:4]
    d_prep_small = prep_out[4:]
    dx1, dg_mix, nmix = _proj_bwd(dgates, dfox, dlat, wgate_t, wfox_t, wlat_t, x1, W["mix_norm"], dx2)
    dwin_t = _join_w_in(_tn_matmul(dgates, nmix, "d_w_in_gates"), _tn_matmul(dfox, nmix, "d_w_in_fox"),
                        _tn_matmul(dlat, nmix, "d_w_in_lat"))
    grads["w_in"] = jnp.pad(dwin_t.reshape(N_DEV, W_IN_PIECE, D_MODEL), ((0, 0), (0, W_IN_PIECE_PAD - W_IN_PIECE), (0, 0))
                            ).reshape(N_DEV * W_IN_PIECE_PAD, D_MODEL)
    dx0, dg_ffn1, da1, db1, h1, n1, dyh1 = _ffn_bwd(dx1, xs, W["ffn1_norm"], a1, b1, G["ffn1_w_gate"], G["ffn1_w_up"],
                                                   G["ffn1_w_down"], "ffn1_bwd")
    grads["ffn1_w_gate"] = _tn_matmul(da1, n1, "ffn1_dgate")
    grads["ffn1_w_up"] = _tn_matmul(db1, n1, "ffn1_dup")
    grads["ffn1_w_down"] = _tn_matmul(h1, dyh1, "ffn1_ddown")

    big = col_split + row_split + ["w_in"]
    from_sibling = _grads_to_sibling([grads[n] for n in big], "grads_d2d")
    sums = [_pair_sum(grads[n], s, "pair_sum_" + n) for n, s in zip(big, from_sibling)]
    slots = dict(zip(big, _grads_over_chips(sums, "grads_ici")))
    small_parts = [dg_ffn1, dg_mix, dg_ffn2] + list(d_prep_small) + [dbg.reshape(1, 2 * D_MODEL), loss_vec]
    flat = jnp.concatenate([p.reshape(-1) for p in small_parts])
    n_flat = flat.shape[0]
    rows = -(-n_flat // (8 * 128)) * 8
    total = _all_reduce_small(jnp.pad(flat, (0, rows * 128 - n_flat)).reshape(rows, 128)).reshape(-1)
    offs, small_g = 0, {}
    for n in _SMALL:
        small_g[n] = total[offs:offs + W[n].shape[1]].reshape(W[n].shape)
        offs += W[n].shape[1]
    bg_full = total[offs:offs + 2 * D_MODEL].reshape(2, D_MODEL)
    offs += 2 * D_MODEL
    loss = (0.5 / D_MODEL) * jnp.sum(total[offs:offs + D_MODEL])
    _, _, _, me = _my_position()
    small_g["b_gate"] = lax.dynamic_slice_in_dim(bg_full, me * (D_MODEL // N_DEV), D_MODEL // N_DEV, axis=1)

    res = {}
    for n in _WEIGHTS:
        if n in small_g:
            res[n] = (small_g[n],) + tuple(_adamw(small_g[n], W[n], M[n], V[n], "adamw_" + n))
        else:
            res[n] = tuple(_reduce_adamw(slots[n], W[n], M[n], V[n], transpose=n in col_split or n == "w_in", name="adamw_" + n))
    outs = [loss, dx0[None]]
    for k in range(4):
        outs += [res[n][k] if n in _SMALL else res[n][k][None] for n in _WEIGHTS]
    return tuple(outs)
```

```python
import functools

import jax
import jax.numpy as jnp
from jax import lax
from jax.experimental import pallas as pl
from jax.experimental.pallas import tpu as pltpu

F32 = jnp.float32
BF16 = jnp.bfloat16

D_MODEL = 1024
FFN_HIDDEN = 2816
N_HEADS = 8
HEAD_DIM = 64
ROPE_DIM = 32
ROPE_HALF = 16
Q_LORA = 192
KV_LORA = 128
ROPE_THETA = 10000.0
RMS_EPS = 1e-6
MLA_SCALE = (HEAD_DIM + ROPE_DIM) ** -0.5
FOX_SCALE = HEAD_DIM ** -0.5
MLA_QK_LANES = 128
ADAM_LR, ADAM_B1, ADAM_B2, ADAM_EPS, ADAM_WD, ADAM_STEP = 0.001, 0.9, 0.999, 1e-08, 0.01, 10
N_DEV = 8
N_CHIP = 4
W_IN_PIECE = 493
W_IN_PIECE_PAD = 496
LAT_W = 512
LAT_Q, LAT_KV, LAT_KR, LAT_F = 0, 256, 384, 416
MASK_VALUE = -1e30
FIXED_SHIFT_MAX_BOUND = 30.0

TOK_TILE = 512
PROJ_TILE = 256
ATT_TILE = 512
ATT_COL_CHUNK = 256
FFN_HID_TILE = 1408

NT = (((1,), (1,)), ((), ()))
TN = (((0,), (0,)), ((), ()))
NN = (((1,), (0,)), ((), ()))
MESH = pl.DeviceIdType.MESH


def _dot(a, b, dims=NN):
    return lax.dot_general(a, b, dims, preferred_element_type=F32)


def _sds(shape, dtype):
    return jax.ShapeDtypeStruct(shape, dtype)


def _rms_fwd(x, g):
    r = lax.rsqrt(jnp.mean(x * x, axis=-1, keepdims=True) + RMS_EPS)
    return x * r * g, r


def _rms_bwd(dy, x, g, r):
    xn = x * r
    dyg = dy * g
    dx = r * (dyg - xn * jnp.mean(dyg * xn, axis=-1, keepdims=True))
    return dx, dy * xn


def _colsum(x):
    return jnp.sum(x, axis=0, keepdims=True)


def _full(shape):
    return pl.BlockSpec(shape, lambda *_: (0,) * len(shape))


def _tok(tm, n):
    return pl.BlockSpec((tm, n), lambda i, *_: (i, 0))


def _ffn_fwd(x, gain, wg_t, wu_t, wd, target=None):
    t = x.shape[0]
    tm = min(TOK_TILE, t)
    tf = FFN_HID_TILE
    n_t, n_f = t // tm, FFN_HIDDEN // tf
    with_loss = target is not None

    def body(*refs):
        if with_loss:
            x_ref, g_ref, wg_ref, wu_ref, wd_ref, t_ref, out_ref, a_ref, b_ref, lvec_ref, n_scr, acc = refs
        else:
            x_ref, g_ref, wg_ref, wu_ref, wd_ref, out_ref, a_ref, b_ref, n_scr, acc = refs
        i, j = pl.program_id(0), pl.program_id(1)

        @pl.when(j == 0)
        def _():
            xn, _ = _rms_fwd(x_ref[...], g_ref[...])
            n_scr[...] = xn.astype(BF16)
            acc[...] = jnp.zeros_like(acc)

        n = n_scr[...]
        a = _dot(n, wg_ref[...], NT)
        b = _dot(n, wu_ref[...], NT)
        a_ref[...] = a.astype(BF16)
        b_ref[...] = b.astype(BF16)
        h = (a * jax.nn.sigmoid(a)) * b
        acc[...] += _dot(h.astype(BF16), wd_ref[...])

        @pl.when(j == n_f - 1)
        def _():
            y = x_ref[...] + 0.5 * acc[...]
            if with_loss:
                diff = y - t_ref[...]
                out_ref[...] = diff * (1.0 / D_MODEL)
                sq = _colsum(diff * diff)

                @pl.when(i == 0)
                def _():
                    lvec_ref[...] = sq

                @pl.when(i > 0)
                def _():
                    lvec_ref[...] += sq
            else:
                out_ref[...] = y

    wspec = pl.BlockSpec((tf, D_MODEL), lambda i, j: (j, 0))
    hspec = pl.BlockSpec((tm, tf), lambda i, j: (i, j))
    in_specs = [_tok(tm, D_MODEL), _full((1, D_MODEL)), wspec, wspec, wspec]
    out_specs = [_tok(tm, D_MODEL), hspec, hspec]
    out_shape = [_sds((t, D_MODEL), F32), _sds((t, FFN_HIDDEN), BF16), _sds((t, FFN_HIDDEN), BF16)]
    args = [x, gain, wg_t, wu_t, wd]
    if with_loss:
        in_specs.append(_tok(tm, D_MODEL))
        out_specs.append(_full((1, D_MODEL)))
        out_shape.append(_sds((1, D_MODEL), F32))
        args.append(target)
    return pl.pallas_call(
        body, name="ffn_fwd_loss" if with_loss else "ffn_fwd", grid=(n_t, n_f), in_specs=in_specs, out_specs=out_specs,
        out_shape=out_shape,
        scratch_shapes=[pltpu.VMEM((tm, D_MODEL), BF16), pltpu.VMEM((tm, D_MODEL), F32)],
        compiler_params=pltpu.CompilerParams(dimension_semantics=("arbitrary", "arbitrary")),
    )(*args)


def _ffn_bwd(dy, x, gain, a, b, wg_t, wu_t, wd, name):
    t = x.shape[0]
    tm = min(PROJ_TILE, t)
    tf = FFN_HID_TILE
    n_t, n_f = t // tm, FFN_HIDDEN // tf

    def body(dy_ref, x_ref, g_ref, a_ref, b_ref, wg_ref, wu_ref, wd_ref,
             dx_ref, dg_ref, da_ref, db_ref, h_ref, n_ref, dyh_ref, acc):
        i, j = pl.program_id(0), pl.program_id(1)

        @pl.when(j == 0)
        def _():
            xn, _ = _rms_fwd(x_ref[...], g_ref[...])
            n_ref[...] = xn.astype(BF16)
            dyh_ref[...] = (0.5 * dy_ref[...]).astype(BF16)
            acc[...] = jnp.zeros_like(acc)

        dh = _dot(dyh_ref[...], wd_ref[...], NT)
        av = a_ref[...].astype(F32)
        bv = b_ref[...].astype(F32)
        s = jax.nn.sigmoid(av)
        silu = av * s
        da = (dh * bv * (s * (1.0 + av * (1.0 - s)))).astype(BF16)
        db = (dh * silu).astype(BF16)
        da_ref[...] = da
        db_ref[...] = db
        h_ref[...] = (silu * bv).astype(BF16)
        acc[...] += _dot(da, wg_ref[...]) + _dot(db, wu_ref[...])

        @pl.when(j == n_f - 1)
        def _():
            xv, g = x_ref[...], g_ref[...]
            r = lax.rsqrt(jnp.mean(xv * xv, axis=-1, keepdims=True) + RMS_EPS)
            dx, dg_rows = _rms_bwd(acc[...], xv, g, r)
            dx_ref[...] = dy_ref[...] + dx
            dg = _colsum(dg_rows)

            @pl.when(i == 0)
            def _():
                dg_ref[...] = dg

            @pl.when(i > 0)
            def _():
                dg_ref[...] += dg

    wspec = pl.BlockSpec((tf, D_MODEL), lambda i, j: (j, 0))
    hspec = pl.BlockSpec((tm, tf), lambda i, j: (i, j))
    tok = _tok(tm, D_MODEL)
    hid = _sds((t, FFN_HIDDEN), BF16)
    return pl.pallas_call(
        body, name=name, grid=(n_t, n_f),
        in_specs=[tok, tok, _full((1, D_MODEL)), hspec, hspec, wspec, wspec, wspec],
        out_specs=[tok, _full((1, D_MODEL)), hspec, hspec, hspec, tok, tok],
        out_shape=[_sds((t, D_MODEL), F32), _sds((1, D_MODEL), F32), hid, hid, hid,
                   _sds((t, D_MODEL), BF16), _sds((t, D_MODEL), BF16)],
        scratch_shapes=[pltpu.VMEM((tm, D_MODEL), F32)],
        compiler_params=pltpu.CompilerParams(dimension_semantics=("arbitrary", "arbitrary")),
    )(dy, x, gain, a, b, wg_t, wu_t, wd)


def _tn_matmul(a, b, name):
    t, m = a.shape
    n = b.shape[1]
    tk = min(TOK_TILE, t)
    n_k = t // tk

    def body(a_ref, b_ref, o_ref, acc):
        k = pl.program_id(0)
        p = _dot(a_ref[...], b_ref[...], TN)

        @pl.when(k == 0)
        def _():
            acc[...] = p

        @pl.when(k > 0)
        def _():
            acc[...] += p

        @pl.when(k == n_k - 1)
        def _():
            o_ref[...] = acc[...].astype(BF16)

    return pl.pallas_call(
        body, name=name, grid=(n_k,), in_specs=[_tok(tk, m), _tok(tk, n)], out_specs=_full((m, n)),
        out_shape=_sds((m, n), BF16), scratch_shapes=[pltpu.VMEM((m, n), F32)],
        compiler_params=pltpu.CompilerParams(dimension_semantics=("arbitrary",)),
    )(a, b)


def _proj_fwd(x, gain, wgate_t, wfox_t, wlat_t):
    t = x.shape[0]
    tm = min(PROJ_TILE, t)

    def body(x_ref, g_ref, wg_ref, wf_ref, wl_ref, og_ref, of_ref, ol_ref):
        xn, _ = _rms_fwd(x_ref[...], g_ref[...])
        n = xn.astype(BF16)
        og_ref[...] = _dot(n, wg_ref[...], NT)
        of_ref[...] = _dot(n, wf_ref[...], NT)
        ol_ref[...] = _dot(n, wl_ref[...], NT)

    return pl.pallas_call(
        body, name="proj_fwd", grid=(t // tm,),
        in_specs=[_tok(tm, D_MODEL), _full((1, D_MODEL)), _full(wgate_t.shape), _full(wfox_t.shape), _full(wlat_t.shape)],
        out_specs=[_tok(tm, 2 * D_MODEL), _tok(tm, 3 * 512), _tok(tm, LAT_W)],
        out_shape=[_sds((t, 2 * D_MODEL), F32), _sds((t, 3 * 512), F32), _sds((t, LAT_W), F32)],
        compiler_params=pltpu.CompilerParams(dimension_semantics=("arbitrary",)),
    )(x, gain, wgate_t, wfox_t, wlat_t)


def _proj_bwd(dgates, dfox, dlat, wgate_t, wfox_t, wlat_t, x, gain, dres):
    t = x.shape[0]
    tm = min(PROJ_TILE, t)

    def body(dg_ref, df_ref, dl_ref, wg_ref, wf_ref, wl_ref, x_ref, g_ref, dres_ref, dx_ref, dgain_ref, n_ref):
        i = pl.program_id(0)
        dn = _dot(dg_ref[...], wg_ref[...]) + _dot(df_ref[...], wf_ref[...]) + _dot(dl_ref[...], wl_ref[...])
        xv, g = x_ref[...], g_ref[...]
        xn, r = _rms_fwd(xv, g)
        n_ref[...] = xn.astype(BF16)
        dx, dg_rows = _rms_bwd(dn, xv, g, r)
        dx_ref[...] = dres_ref[...] + dx
        dgn = _colsum(dg_rows)

        @pl.when(i == 0)
        def _():
            dgain_ref[...] = dgn

        @pl.when(i > 0)
        def _():
            dgain_ref[...] += dgn

    tok = _tok(tm, D_MODEL)
    return pl.pallas_call(
        body, name="proj_bwd", grid=(t // tm,),
        in_specs=[_tok(tm, 2 * D_MODEL), _tok(tm, 3 * 512), _tok(tm, LAT_W), _full(wgate_t.shape), _full(wfox_t.shape),
                  _full(wlat_t.shape), tok, _full((1, D_MODEL)), tok],
        out_specs=[tok, _full((1, D_MODEL)), tok],
        out_shape=[_sds((t, D_MODEL), F32), _sds((1, D_MODEL), F32), _sds((t, D_MODEL), BF16)],
        compiler_params=pltpu.CompilerParams(dimension_semantics=("arbitrary",)),
    )(dgates, dfox, dlat, wgate_t, wfox_t, wlat_t, x, gain, dres)


def _tri(n, lower):
    r = lax.broadcasted_iota(jnp.int32, (n, n), 0)
    c = lax.broadcasted_iota(jnp.int32, (n, n), 1)
    return ((c <= r) if lower else (c >= r)).astype(F32)


def _log_sigmoid(z):
    return jnp.minimum(z, 0.0) - jnp.log1p(jnp.exp(-jnp.abs(z)))


def _rope_pair_fwd(p1, p2, g1, g2, cos, sin):
    ss = jnp.sum(p1 * p1, axis=-1, keepdims=True) + jnp.sum(p2 * p2, axis=-1, keepdims=True)
    r = lax.rsqrt(ss * (1.0 / ROPE_DIM) + RMS_EPS)
    n1, n2 = p1 * r, p2 * r
    y1, y2 = n1 * g1, n2 * g2
    return y1 * cos - y2 * sin, y2 * cos + y1 * sin, n1, n2, r


def _rope_pair_bwd(do1, do2, n1, n2, r, g1, g2, cos, sin):
    dy1 = do1 * cos + do2 * sin
    dy2 = do2 * cos - do1 * sin
    dyg1, dyg2 = dy1 * g1, dy2 * g2
    mu = (jnp.sum(dyg1 * n1, axis=-1, keepdims=True) + jnp.sum(dyg2 * n2, axis=-1, keepdims=True)) * (1.0 / ROPE_DIM)
    return r * (dyg1 - n1 * mu), r * (dyg2 - n2 * mu), dy1 * n1, dy2 * n2


def _prep_fwd(fox, lat, cos, sin, wqb_t, wkvb_t, g_qlat, g_kvlat, g_qn, g_qr, g_kn, g_kr, g_fq, g_fk, b_f):
    t = fox.shape[0]
    tm = min(PROJ_TILE, t)
    hd, lanes = HEAD_DIM, MLA_QK_LANES

    def body(fox_ref, lat_ref, cos_ref, sin_ref, wqb_ref, wkvb_ref, gql_ref, gkvl_ref, gqn_ref, gqr_ref, gkn_ref,
             gkr_ref, gfq_ref, gfk_ref, bf_ref, fq_ref, fk_ref, fv_ref, c_ref, mq_ref, mk_ref, mv_ref,
             carry, qscr, kscr):
        i = pl.program_id(0)
        cos_v, sin_v = cos_ref[...], sin_ref[...]

        for h in range(N_HEADS):
            w = slice(h * hd, (h + 1) * hd)
            qn, _ = _rms_fwd(fox_ref[:, h * hd:(h + 1) * hd], gfq_ref[...])
            fq_ref[:, w] = (qn * FOX_SCALE).astype(BF16)
            kn, _ = _rms_fwd(fox_ref[:, 512 + h * hd:512 + (h + 1) * hd], gfk_ref[...])
            fk_ref[:, w] = kn.astype(BF16)
        fv_ref[...] = fox_ref[:, 1024:1536].astype(BF16)

        @pl.when(i == 0)
        def _():
            carry[...] = jnp.zeros_like(carry)

        logf = _log_sigmoid(lat_ref[:, LAT_F:LAT_F + N_HEADS] + bf_ref[...])
        c_ref[...] = jnp.dot(_tri(tm, True), logf, precision=lax.Precision.HIGHEST, preferred_element_type=F32) + carry[...]
        carry[...] += _colsum(logf)

        qlat_n, _ = _rms_fwd(lat_ref[:, LAT_Q:LAT_Q + Q_LORA], gql_ref[...])
        qlat_b = qlat_n.astype(BF16)
        g1, g2 = gqr_ref[:, 0:ROPE_HALF], gqr_ref[:, ROPE_HALF:ROPE_DIM]
        qscr[...] = jnp.zeros_like(qscr)
        for h in range(N_HEADS):
            base = h * (hd + ROPE_DIM)
            pn = _dot(qlat_b, wqb_ref[base:base + hd, :], NT)
            p1 = _dot(qlat_b, wqb_ref[base + hd:base + hd + ROPE_HALF, :], NT)
            p2 = _dot(qlat_b, wqb_ref[base + hd + ROPE_HALF:base + hd + ROPE_DIM, :], NT)
            nope, _ = _rms_fwd(pn, gqn_ref[...])
            o1, o2, _, _, _ = _rope_pair_fwd(p1, p2, g1, g2, cos_v, sin_v)
            qscr[:, h * lanes:h * lanes + hd] = nope
            qscr[:, h * lanes + hd:h * lanes + hd + ROPE_HALF] = o1
            qscr[:, h * lanes + hd + ROPE_HALF:h * lanes + hd + ROPE_DIM] = o2
        mq_ref[...] = qscr[...].astype(BF16)

        kv_n, _ = _rms_fwd(lat_ref[:, LAT_KV:LAT_KV + KV_LORA], gkvl_ref[...])
        kv_b = kv_n.astype(BF16)
        kr1, kr2, _, _, _ = _rope_pair_fwd(lat_ref[:, LAT_KR:LAT_KR + ROPE_HALF], lat_ref[:, LAT_KR + ROPE_HALF:LAT_KR + ROPE_DIM],
                                           gkr_ref[:, 0:ROPE_HALF], gkr_ref[:, ROPE_HALF:ROPE_DIM], cos_v, sin_v)
        kscr[...] = jnp.zeros_like(kscr)
        for h in range(N_HEADS):
            pk = _dot(kv_b, wkvb_ref[h * 2 * hd:h * 2 * hd + hd, :], NT)
            kn, _ = _rms_fwd(pk, gkn_ref[...])
            kscr[:, h * lanes:h * lanes + hd] = kn
            kscr[:, h * lanes + hd:h * lanes + hd + ROPE_HALF] = kr1
            kscr[:, h * lanes + hd + ROPE_HALF:h * lanes + hd + ROPE_DIM] = kr2
            mv_ref[:, h * hd:(h + 1) * hd] = _dot(kv_b, wkvb_ref[h * 2 * hd + hd:(h + 1) * 2 * hd, :], NT).astype(BF16)
        mk_ref[...] = kscr[...].astype(BF16)

    small = [g_qlat, g_kvlat, g_qn, g_qr, g_kn, g_kr, g_fq, g_fk, b_f]
    t512 = _tok(tm, 512)
    return pl.pallas_call(
        body, name="prep_fwd", grid=(t // tm,),
        in_specs=[_tok(tm, 1536), _tok(tm, LAT_W), _tok(tm, ROPE_HALF), _tok(tm, ROPE_HALF), _full(wqb_t.shape),
                  _full(wkvb_t.shape)] + [_full(s.shape) for s in small],
        out_specs=[t512, t512, t512, _tok(tm, N_HEADS), _tok(tm, N_HEADS * lanes), _tok(tm, N_HEADS * lanes), t512],
        out_shape=[_sds((t, 512), BF16), _sds((t, 512), BF16), _sds((t, 512), BF16), _sds((t, N_HEADS), F32),
                   _sds((t, N_HEADS * lanes), BF16), _sds((t, N_HEADS * lanes), BF16), _sds((t, 512), BF16)],
        scratch_shapes=[pltpu.VMEM((1, N_HEADS), F32), pltpu.VMEM((tm, N_HEADS * lanes), F32),
                        pltpu.VMEM((tm, N_HEADS * lanes), F32)],
        compiler_params=pltpu.CompilerParams(dimension_semantics=("arbitrary",)),
    )(fox, lat, cos, sin, wqb_t, wkvb_t, *small)


def _prep_bwd(fox, lat, cos, sin, wqb_t, wkvb_t, g_qlat, g_kvlat, g_qn, g_qr, g_kn, g_kr, g_fq, g_fk, b_f,
              dfq, dfk, dfv, dc, dmq, dmk, dmv):
    t = fox.shape[0]
    tm = min(PROJ_TILE, t)
    n_t = t // tm
    hd, lanes = HEAD_DIM, MLA_QK_LANES

    def body(fox_ref, lat_ref, cos_ref, sin_ref, wqb_ref, wkvb_ref, gql_ref, gkvl_ref, gqn_ref, gqr_ref, gkn_ref,
             gkr_ref, gfq_ref, gfk_ref, bf_ref, dfq_ref, dfk_ref, dfv_ref, dc_ref, dmq_ref, dmk_ref, dmv_ref,
             dfox_ref, dlat_ref, dwqb_ref, dwkvb_ref, o_gql, o_gkvl, o_gqn, o_gqr, o_gkn, o_gkr, o_gfq, o_gfk, o_bf,
             carry, lscr, wq_acc, wkv_acc):
        i = pl.program_id(0)
        cos_v, sin_v = cos_ref[...], sin_ref[...]
        small_out = [o_gql, o_gkvl, o_gqn, o_gqr, o_gkn, o_gkr, o_gfq, o_gfk, o_bf]

        @pl.when(i == 0)
        def _():
            carry[...] = jnp.zeros_like(carry)
            wq_acc[...] = jnp.zeros_like(wq_acc)
            wkv_acc[...] = jnp.zeros_like(wkv_acc)
            for o in small_out:
                o[...] = jnp.zeros_like(o)

        d_gfq = jnp.zeros((1, hd), F32)
        d_gfk = jnp.zeros((1, hd), F32)
        for h in range(N_HEADS):
            w = slice(h * hd, (h + 1) * hd)
            xq = fox_ref[:, h * hd:(h + 1) * hd]
            rq = lax.rsqrt(jnp.mean(xq * xq, axis=-1, keepdims=True) + RMS_EPS)
            dxq, gq_rows = _rms_bwd(dfq_ref[:, w] * FOX_SCALE, xq, gfq_ref[...], rq)
            dfox_ref[:, w] = dxq.astype(BF16)
            d_gfq += _colsum(gq_rows)
            xk = fox_ref[:, 512 + h * hd:512 + (h + 1) * hd]
            rk = lax.rsqrt(jnp.mean(xk * xk, axis=-1, keepdims=True) + RMS_EPS)
            dxk, gk_rows = _rms_bwd(dfk_ref[:, w], xk, gfk_ref[...], rk)
            dfox_ref[:, 512 + h * hd:512 + (h + 1) * hd] = dxk.astype(BF16)
            d_gfk += _colsum(gk_rows)
        dfox_ref[:, 1024:1536] = dfv_ref[...].astype(BF16)
        o_gfq[...] += d_gfq
        o_gfk[...] += d_gfk

        lscr[...] = jnp.zeros_like(lscr)

        dcv = dc_ref[...]
        dlogf = jnp.dot(_tri(tm, False), dcv, precision=lax.Precision.HIGHEST, preferred_element_type=F32) + carry[...]
        carry[...] += _colsum(dcv)
        z = lat_ref[:, LAT_F:LAT_F + N_HEADS] + bf_ref[...]
        dz = dlogf * jax.nn.sigmoid(-z)
        lscr[:, LAT_F:LAT_F + N_HEADS] = dz
        o_bf[...] += _colsum(dz)

        xql = lat_ref[:, LAT_Q:LAT_Q + Q_LORA]
        qlat_n, r_ql = _rms_fwd(xql, gql_ref[...])
        qlat_b = qlat_n.astype(BF16)
        g1, g2 = gqr_ref[:, 0:ROPE_HALF], gqr_ref[:, ROPE_HALF:ROPE_DIM]
        dqlat_n = jnp.zeros((tm, Q_LORA), F32)
        d_gqn = jnp.zeros((1, hd), F32)
        d_g1 = jnp.zeros((1, ROPE_HALF), F32)
        d_g2 = jnp.zeros((1, ROPE_HALF), F32)
        for h in range(N_HEADS):
            base = h * (hd + ROPE_DIM)
            rows_n = slice(base, base + hd)
            rows_1 = slice(base + hd, base + hd + ROPE_HALF)
            rows_2 = slice(base + hd + ROPE_HALF, base + hd + ROPE_DIM)
            pn = _dot(qlat_b, wqb_ref[rows_n, :], NT)
            p1 = _dot(qlat_b, wqb_ref[rows_1, :], NT)
            p2 = _dot(qlat_b, wqb_ref[rows_2, :], NT)
            r_n = lax.rsqrt(jnp.mean(pn * pn, axis=-1, keepdims=True) + RMS_EPS)
            _, _, n1, n2, r_r = _rope_pair_fwd(p1, p2, g1, g2, cos_v, sin_v)
            dpn, gn_rows = _rms_bwd(dmq_ref[:, h * lanes:h * lanes + hd], pn, gqn_ref[...], r_n)
            dp1, dp2, g1_rows, g2_rows = _rope_pair_bwd(
                dmq_ref[:, h * lanes + hd:h * lanes + hd + ROPE_HALF],
                dmq_ref[:, h * lanes + hd + ROPE_HALF:h * lanes + hd + ROPE_DIM], n1, n2, r_r, g1, g2, cos_v, sin_v)
            d_gqn += _colsum(gn_rows)
            d_g1 += _colsum(g1_rows)
            d_g2 += _colsum(g2_rows)
            dpn_b, dp1_b, dp2_b = dpn.astype(BF16), dp1.astype(BF16), dp2.astype(BF16)
            dqlat_n += _dot(dpn_b, wqb_ref[rows_n, :]) + _dot(dp1_b, wqb_ref[rows_1, :]) + _dot(dp2_b, wqb_ref[rows_2, :])
            wq_acc[rows_n, :] += _dot(dpn_b, qlat_b, TN)
            wq_acc[rows_1, :] += _dot(dp1_b, qlat_b, TN)
            wq_acc[rows_2, :] += _dot(dp2_b, qlat_b, TN)
        dxql, gql_rows = _rms_bwd(dqlat_n, xql, gql_ref[...], r_ql)
        lscr[:, LAT_Q:LAT_Q + Q_LORA] = dxql
        o_gql[...] += _colsum(gql_rows)
        o_gqn[...] += d_gqn
        o_gqr[:, 0:ROPE_HALF] += d_g1
        o_gqr[:, ROPE_HALF:ROPE_DIM] += d_g2

        xkv = lat_ref[:, LAT_KV:LAT_KV + KV_LORA]
        kv_n, r_kv = _rms_fwd(xkv, gkvl_ref[...])
        kv_b = kv_n.astype(BF16)
        dkv_n = jnp.zeros((tm, KV_LORA), F32)
        d_gkn = jnp.zeros((1, hd), F32)
        dkr1 = jnp.zeros((tm, ROPE_HALF), F32)
        dkr2 = jnp.zeros((tm, ROPE_HALF), F32)
        for h in range(N_HEADS):
            rows_k = slice(h * 2 * hd, h * 2 * hd + hd)
            rows_v = slice(h * 2 * hd + hd, (h + 1) * 2 * hd)
            pk = _dot(kv_b, wkvb_ref[rows_k, :], NT)
            r_k = lax.rsqrt(jnp.mean(pk * pk, axis=-1, keepdims=True) + RMS_EPS)
            dpk, gk_rows = _rms_bwd(dmk_ref[:, h * lanes:h * lanes + hd], pk, gkn_ref[...], r_k)
            d_gkn += _colsum(gk_rows)
            dkr1 += dmk_ref[:, h * lanes + hd:h * lanes + hd + ROPE_HALF]
            dkr2 += dmk_ref[:, h * lanes + hd + ROPE_HALF:h * lanes + hd + ROPE_DIM]
            dpk_b = dpk.astype(BF16)
            dv_b = dmv_ref[:, h * hd:(h + 1) * hd].astype(BF16)
            dkv_n += _dot(dpk_b, wkvb_ref[rows_k, :]) + _dot(dv_b, wkvb_ref[rows_v, :])
            wkv_acc[rows_k, :] += _dot(dpk_b, kv_b, TN)
            wkv_acc[rows_v, :] += _dot(dv_b, kv_b, TN)
        dxkv, gkvl_rows = _rms_bwd(dkv_n, xkv, gkvl_ref[...], r_kv)
        lscr[:, LAT_KV:LAT_KV + KV_LORA] = dxkv
        o_gkvl[...] += _colsum(gkvl_rows)
        o_gkn[...] += d_gkn

        gk1, gk2 = gkr_ref[:, 0:ROPE_HALF], gkr_ref[:, ROPE_HALF:ROPE_DIM]
        _, _, kn1, kn2, r_kr = _rope_pair_fwd(lat_ref[:, LAT_KR:LAT_KR + ROPE_HALF],
                                              lat_ref[:, LAT_KR + ROPE_HALF:LAT_KR + ROPE_DIM], gk1, gk2, cos_v, sin_v)
        dk1, dk2, gk1_rows, gk2_rows = _rope_pair_bwd(dkr1, dkr2, kn1, kn2, r_kr, gk1, gk2, cos_v, sin_v)
        lscr[:, LAT_KR:LAT_KR + ROPE_HALF] = dk1
        lscr[:, LAT_KR + ROPE_HALF:LAT_KR + ROPE_DIM] = dk2
        o_gkr[:, 0:ROPE_HALF] += _colsum(gk1_rows)
        o_gkr[:, ROPE_HALF:ROPE_DIM] += _colsum(gk2_rows)

        dlat_ref[...] = lscr[...].astype(BF16)

        @pl.when(i == n_t - 1)
        def _():
            dwqb_ref[...] = wq_acc[...].astype(BF16)
            dwkvb_ref[...] = wkv_acc[...].astype(BF16)

    small = [g_qlat, g_kvlat, g_qn, g_qr, g_kn, g_kr, g_fq, g_fk, b_f]

    def rtok(n):
        return pl.BlockSpec((tm, n), lambda i: (n_t - 1 - i, 0))

    return pl.pallas_call(
        body, name="prep_bwd", grid=(n_t,),
        in_specs=[rtok(1536), rtok(LAT_W), rtok(ROPE_HALF), rtok(ROPE_HALF), _full(wqb_t.shape), _full(wkvb_t.shape)]
        + [_full(s.shape) for s in small]
        + [rtok(512), rtok(512), rtok(512), rtok(N_HEADS), rtok(N_HEADS * lanes), rtok(N_HEADS * lanes), rtok(512)],
        out_specs=[rtok(1536), rtok(LAT_W), _full(wqb_t.shape), _full(wkvb_t.shape)] + [_full(s.shape) for s in small],
        out_shape=[_sds((t, 1536), BF16), _sds((t, LAT_W), BF16), _sds(wqb_t.shape, BF16), _sds(wkvb_t.shape, BF16)]
        + [_sds(s.shape, F32) for s in small],
        scratch_shapes=[pltpu.VMEM((1, N_HEADS), F32), pltpu.VMEM((tm, LAT_W), F32), pltpu.VMEM(wqb_t.shape, F32),
                        pltpu.VMEM(wkvb_t.shape, F32)],
        compiler_params=pltpu.CompilerParams(dimension_semantics=("arbitrary",)),
    )(fox, lat, cos, sin, wqb_t, wkvb_t, *small, dfq, dfk, dfv, dc, dmq, dmk, dmv)


def _qk_norms(q, k, *, lanes, name):
    t = q.shape[0]
    tm = min(TOK_TILE, t)

    def body(q_ref, k_ref, qn_ref, kmax_ref):
        i = pl.program_id(0)

        @pl.when(i == 0)
        def _():
            kmax_ref[...] = jnp.zeros_like(kmax_ref)

        for h in range(N_HEADS):
            qh = q_ref[:, h * lanes:(h + 1) * lanes].astype(F32)
            kh = k_ref[:, h * lanes:(h + 1) * lanes].astype(F32)
            qn_ref[:, h:h + 1] = jnp.sum(qh * qh, axis=1, keepdims=True)
            kmax = jnp.max(jnp.sum(kh * kh, axis=1, keepdims=True), axis=0, keepdims=True)
            kmax_ref[:, h:h + 1] = jnp.maximum(kmax_ref[:, h:h + 1], kmax)

    return pl.pallas_call(
        body, name=name, grid=(t // tm,), in_specs=[_tok(tm, N_HEADS * lanes), _tok(tm, N_HEADS * lanes)],
        out_specs=[_tok(tm, N_HEADS), _full((1, N_HEADS))], out_shape=[_sds((t, N_HEADS), F32), _sds((1, N_HEADS), F32)],
        compiler_params=pltpu.CompilerParams(dimension_semantics=("arbitrary",)),
    )(q, k)


def _logit_bound(q, k, *, lanes, scale, name):
    qn, kmax = _qk_norms(q, k, lanes=lanes, name=name)
    bound = jnp.sqrt(qn * kmax) * (scale * (1.0 + 2.0 ** -10)) + 2.0 ** -10
    flag = (jnp.max(bound) <= FIXED_SHIFT_MAX_BOUND).astype(F32).reshape(1, 1)
    return bound, flag


def _attn_fwd(q, k, v, bound, fixed_ok, c, c_t, *, lanes, scale, name):
    t = q.shape[0]
    tq = min(ATT_TILE, t)
    n_q = t // tq
    hd = HEAD_DIM
    ch = min(ATT_COL_CHUNK, tq)
    decay = c is not None

    def body(*refs):
        if decay:
            q_ref, k_ref, v_ref, b_ref, ok_ref, c_ref, ct_ref, o_ref, o32_ref, lse_ref, m_scr, l_scr, acc = refs
        else:
            q_ref, k_ref, v_ref, b_ref, ok_ref, o_ref, o32_ref, lse_ref, m_scr, l_scr, acc = refs
        i, j = pl.program_id(0), pl.program_id(1)
        fixed = ok_ref[0, 0] > 0.5

        @pl.when(j == 0)
        def _():
            m_scr[...] = jnp.full_like(m_scr, MASK_VALUE)
            l_scr[...] = jnp.zeros_like(l_scr)
            acc[...] = jnp.zeros_like(acc)

        def fixed_step(diagonal):
            for h in range(N_HEADS):
                wl = slice(h * lanes, (h + 1) * lanes)
                w = slice(h * hd, (h + 1) * hd)
                qh = q_ref[:, wl]
                row = (c_ref[:, h:h + 1] - b_ref[:, h:h + 1]) if decay else -b_ref[:, h:h + 1]
                l_new = jnp.zeros((tq, 1), F32)
                o_hi = jnp.zeros((tq, hd), F32)
                o_lo = jnp.zeros((tq, hd), F32)
                for cc in range(tq // ch):
                    cols = slice(cc * ch, (cc + 1) * ch)
                    s = _dot(qh, k_ref[cols, wl], NT)
                    if scale != 1.0:
                        s = s * scale
                    s = s + ((row - ct_ref[h:h + 1, cols]) if decay else row)
                    if diagonal:
                        keep = (lax.broadcasted_iota(jnp.int32, (tq, ch), 0)
                                >= lax.broadcasted_iota(jnp.int32, (tq, ch), 1) + cc * ch)
                        s = jnp.where(keep, s, MASK_VALUE)
                    p = jnp.exp(s)
                    l_new = l_new + jnp.sum(p, axis=1, keepdims=True)
                    p_b = p.astype(BF16)
                    o_hi = o_hi + _dot(p_b, v_ref[cols, w])
                    if decay:
                        o_lo = o_lo + _dot((p - p_b.astype(F32)).astype(BF16), v_ref[cols, w])
                l_scr[h] += l_new
                acc[0, :, w] += o_hi
                if decay:
                    acc[1, :, w] += o_lo

        def step(diagonal):
            if diagonal:
                keep = lax.broadcasted_iota(jnp.int32, (tq, tq), 0) >= lax.broadcasted_iota(jnp.int32, (tq, tq), 1)
            for h in range(N_HEADS):
                s = _dot(q_ref[:, h * lanes:(h + 1) * lanes], k_ref[:, h * lanes:(h + 1) * lanes], NT)
                if scale != 1.0:
                    s = s * scale
                if decay:
                    s = s + (c_ref[:, h:h + 1] - ct_ref[h:h + 1, :])
                if diagonal:
                    s = jnp.where(keep, s, MASK_VALUE)
                m_prev = m_scr[h]
                m_new = jnp.maximum(m_prev, jnp.max(s, axis=1, keepdims=True))
                alpha = jnp.exp(m_prev - m_new)
                p = jnp.exp(s - m_new)
                l_scr[h] = alpha * l_scr[h] + jnp.sum(p, axis=1, keepdims=True)
                w = slice(h * hd, (h + 1) * hd)
                p_b = p.astype(BF16)
                acc[0, :, w] = alpha * acc[0, :, w] + _dot(p_b, v_ref[:, w])
                if decay:
                    p_lo = (p - p_b.astype(F32)).astype(BF16)
                    acc[1, :, w] = alpha * acc[1, :, w] + _dot(p_lo, v_ref[:, w])
                m_scr[h] = m_new

        for diagonal, here in ((False, j < i), (True, j == i)):
            @pl.when(here & fixed)
            def _():
                fixed_step(diagonal)

            @pl.when(here & jnp.logical_not(fixed))
            def _():
                step(diagonal)

        @pl.when(j == i)
        def _():
            for h in range(N_HEADS):
                w = slice(h * hd, (h + 1) * hd)
                l = l_scr[h]
                o_ref[:, w] = (acc[0, :, w] / l).astype(BF16)
                o32_ref[:, w] = ((acc[0, :, w] + acc[1, :, w]) if decay else acc[0, :, w]) / l
                lse_ref[:, h:h + 1] = jnp.where(fixed, b_ref[:, h:h + 1], m_scr[h]) + jnp.log(l)

    qspec = lambda n: pl.BlockSpec((tq, n), lambda i, j: (i, 0))
    kspec = lambda n: pl.BlockSpec((tq, n), lambda i, j: (jnp.minimum(i, j), 0))
    in_specs = [qspec(N_HEADS * lanes), kspec(N_HEADS * lanes), kspec(512), qspec(N_HEADS),
                pl.BlockSpec(memory_space=pltpu.SMEM)]
    args = [q, k, v, bound, fixed_ok]
    if decay:
        in_specs += [qspec(N_HEADS), pl.BlockSpec((N_HEADS, tq), lambda i, j: (0, jnp.minimum(i, j)))]
        args += [c, c_t]
    return pl.pallas_call(
        body, name=name, grid=(n_q, n_q), in_specs=in_specs, out_specs=[qspec(512), qspec(512), qspec(N_HEADS)],
        out_shape=[_sds((t, 512), BF16), _sds((t, 512), F32), _sds((t, N_HEADS), F32)],
        scratch_shapes=[pltpu.VMEM((N_HEADS, tq, 1), F32), pltpu.VMEM((N_HEADS, tq, 1), F32), pltpu.VMEM((2, tq, 512), F32)],
        compiler_params=pltpu.CompilerParams(dimension_semantics=("arbitrary", "arbitrary")),
    )(*args)


def _attn_bwd(q, k, v, do, delta, lse, c, c_t, *, lanes, scale, name):
    t = q.shape[0]
    tq = min(ATT_TILE, t)
    n_q = t // tq
    hd = HEAD_DIM
    decay = c is not None

    def body(*refs):
        if decay:
            q_ref, k_ref, v_ref, do_ref, delta_ref, lse_ref, c_ref, ct_ref, dq_hbm, dk_ref, dv_ref, dct_ref, dq_ref = refs
        else:
            q_ref, k_ref, v_ref, do_ref, delta_ref, lse_ref, dq_hbm, dk_ref, dv_ref, dq_ref = refs
        j, i = pl.program_id(0), pl.program_id(1)

        @pl.when((j == 0) & (i == 0))
        def _():
            dq_ref[...] = jnp.zeros_like(dq_ref)

        @pl.when(i == j)
        def _():
            dk_ref[...] = jnp.zeros_like(dk_ref)
            dv_ref[...] = jnp.zeros_like(dv_ref)
            if decay:
                dct_ref[...] = jnp.zeros_like(dct_ref)

        def step(diagonal):
            if diagonal:
                keep = lax.broadcasted_iota(jnp.int32, (tq, tq), 0) >= lax.broadcasted_iota(jnp.int32, (tq, tq), 1)
            rows = pl.ds(pl.multiple_of(i * tq, tq), tq)
            for h in range(N_HEADS):
                wl = slice(h * lanes, (h + 1) * lanes)
                w = slice(h * hd, (h + 1) * hd)
                qh, kh = q_ref[:, wl], k_ref[:, wl]
                s = _dot(qh, kh, NT)
                if scale != 1.0:
                    s = s * scale
                if decay:
                    s = s + (c_ref[:, h:h + 1] - ct_ref[h:h + 1, :])
                if diagonal:
                    s = jnp.where(keep, s, MASK_VALUE)
                p = jnp.exp(s - lse_ref[:, h:h + 1])
                doh = do_ref[:, w]
                dv_ref[:, w] += _dot(p.astype(BF16), doh, TN)
                dp = _dot(doh, v_ref[:, w], NT)
                ds = p * (dp - delta_ref[:, h:h + 1])
                if decay:
                    dct_ref[h:h + 1, :] -= _colsum(ds)
                if scale != 1.0:
                    ds = ds * scale
                ds_b = ds.astype(BF16)
                dk_ref[:, wl] += _dot(ds_b, qh, TN)
                dq_ref[rows, wl] += _dot(ds_b, kh)

        @pl.when(i > j)
        def _():
            step(False)

        @pl.when(i == j)
        def _():
            step(True)

        @pl.when((j == n_q - 1) & (i == n_q - 1))
        def _():
            pltpu.sync_copy(dq_ref, dq_hbm)

    qspec = lambda n: pl.BlockSpec((tq, n), lambda j, i: (jnp.maximum(i, j), 0))
    kspec = lambda n: pl.BlockSpec((tq, n), lambda j, i: (j, 0))
    in_specs = [qspec(N_HEADS * lanes), kspec(N_HEADS * lanes), kspec(512), qspec(512), qspec(N_HEADS), qspec(N_HEADS)]
    out_specs = [pl.BlockSpec(memory_space=pl.ANY), kspec(N_HEADS * lanes), kspec(512)]
    out_shape = [_sds((t, N_HEADS * lanes), F32), _sds((t, N_HEADS * lanes), F32), _sds((t, 512), F32)]
    args = [q, k, v, do, delta, lse]
    if decay:
        ctspec = pl.BlockSpec((N_HEADS, tq), lambda j, i: (0, j))
        in_specs += [qspec(N_HEADS), ctspec]
        out_specs.append(ctspec)
        out_shape.append(_sds((N_HEADS, t), F32))
        args += [c, c_t]
    return pl.pallas_call(
        body, name=name, grid=(n_q, n_q), in_specs=in_specs, out_specs=out_specs, out_shape=out_shape,
        scratch_shapes=[pltpu.VMEM((t, N_HEADS * lanes), F32)],
        compiler_params=pltpu.CompilerParams(dimension_semantics=("arbitrary", "arbitrary")),
    )(*args)


def _mix_fwd(x, y_mla, y_fox, gates, b_gate, wbm_t, wbf_t, wo):
    t = x.shape[0]
    tm = min(PROJ_TILE, t)

    def body(x_ref, ym_ref, yf_ref, gt_ref, bg_ref, wbm_ref, wbf_ref, wo_ref, out_ref):
        um = _dot(ym_ref[...], wbm_ref[...], NT)
        uf = _dot(yf_ref[...], wbf_ref[...], NT)
        sm = jax.nn.sigmoid(gt_ref[:, 0:D_MODEL] + bg_ref[0:1, :])
        sf = jax.nn.sigmoid(gt_ref[:, D_MODEL:2 * D_MODEL] + bg_ref[1:2, :])
        mixed = sm * um + sf * uf
        out_ref[...] = x_ref[...] + _dot(mixed.astype(BF16), wo_ref[...])

    tok = _tok(tm, D_MODEL)
    return pl.pallas_call(
        body, name="mix_fwd", grid=(t // tm,),
        in_specs=[tok, _tok(tm, 512), _tok(tm, 512), _tok(tm, 2 * D_MODEL), _full((2, D_MODEL)), _full(wbm_t.shape),
                  _full(wbf_t.shape), _full(wo.shape)],
        out_specs=tok, out_shape=_sds((t, D_MODEL), F32),
        compiler_params=pltpu.CompilerParams(dimension_semantics=("arbitrary",)),
    )(x, y_mla, y_fox, gates, b_gate, wbm_t, wbf_t, wo)


def _mix_bwd(dx, y_mla, y_fox, y_mla32, y_fox32, gates, b_gate, wbm_t, wbf_t, wo):
    t = dx.shape[0]
    tm = min(PROJ_TILE, t)

    def body(dx_ref, ym_ref, yf_ref, ym32_ref, yf32_ref, gt_ref, bg_ref, wbm_ref, wbf_ref, wo_ref,
             dym_ref, dyf_ref, dlm_ref, dlf_ref, dgt_ref, mixed_ref, dum_ref, duf_ref, dxb_ref, dbg_ref, prod):
        i = pl.program_id(0)
        dxb = dx_ref[...].astype(BF16)
        dxb_ref[...] = dxb
        dmixed = _dot(dxb, wo_ref[...], NT)
        um = _dot(ym_ref[...], wbm_ref[...], NT)
        uf = _dot(yf_ref[...], wbf_ref[...], NT)
        sm = jax.nn.sigmoid(gt_ref[:, 0:D_MODEL] + bg_ref[0:1, :])
        sf = jax.nn.sigmoid(gt_ref[:, D_MODEL:2 * D_MODEL] + bg_ref[1:2, :])
        mixed_ref[...] = (sm * um + sf * uf).astype(BF16)
        dum = (dmixed * sm).astype(BF16)
        duf = (dmixed * sf).astype(BF16)
        dum_ref[...] = dum
        duf_ref[...] = duf
        dgm = dmixed * um * (sm * (1.0 - sm))
        dgf = dmixed * uf * (sf * (1.0 - sf))
        dgt_ref[:, 0:D_MODEL] = dgm.astype(BF16)
        dgt_ref[:, D_MODEL:2 * D_MODEL] = dgf.astype(BF16)
        for du, wb_ref, y32_ref, dy_ref, dl_ref in ((dum, wbm_ref, ym32_ref, dym_ref, dlm_ref),
                                                    (duf, wbf_ref, yf32_ref, dyf_ref, dlf_ref)):
            dy = _dot(du, wb_ref[...])
            dy_ref[...] = dy.astype(BF16)
            prod[...] = dy.astype(BF16).astype(F32) * y32_ref[...]
            for h in range(N_HEADS):
                dl_ref[:, h:h + 1] = jnp.sum(prod[:, h * HEAD_DIM:(h + 1) * HEAD_DIM], axis=1, keepdims=True)

        @pl.when(i == 0)
        def _():
            dbg_ref[...] = jnp.zeros_like(dbg_ref)

        dbg_ref[0:1, :] += _colsum(dgm)
        dbg_ref[1:2, :] += _colsum(dgf)

    tok = _tok(tm, D_MODEL)
    tokb = _sds((t, D_MODEL), BF16)
    t512, t8 = _tok(tm, 512), _tok(tm, N_HEADS)
    return pl.pallas_call(
        body, name="mix_bwd", grid=(t // tm,),
        in_specs=[tok, t512, t512, t512, t512, _tok(tm, 2 * D_MODEL), _full((2, D_MODEL)), _full(wbm_t.shape),
                  _full(wbf_t.shape), _full(wo.shape)],
        out_specs=[t512, t512, t8, t8, _tok(tm, 2 * D_MODEL), tok, tok, tok, tok, _full((2, D_MODEL))],
        out_shape=[_sds((t, 512), BF16), _sds((t, 512), BF16), _sds((t, N_HEADS), F32), _sds((t, N_HEADS), F32),
                   _sds((t, 2 * D_MODEL), BF16), tokb, tokb, tokb, tokb, _sds((2, D_MODEL), F32)],
        scratch_shapes=[pltpu.VMEM((tm, 512), F32)],
        compiler_params=pltpu.CompilerParams(dimension_semantics=("arbitrary",)),
    )(dx, y_mla, y_fox, y_mla32, y_fox32, gates, b_gate, wbm_t, wbf_t, wo)


def _my_position():
    x, y, c = lax.axis_index("x"), lax.axis_index("y"), lax.axis_index("c")
    return x, y, c, 4 * x + 2 * y + c


def _peer(x, y, c, mask):
    px = 1 - x if mask & 4 else x
    py = 1 - y if mask & 2 else y
    pc = 1 - c if mask & 1 else c
    return (px, py, pc), 4 * px + 2 * py + pc


def _chip_peer(x, y, km):
    px = 1 - x if km & 2 else x
    py = 1 - y if km & 1 else y
    return px, py, 2 * px + py


_HBM = pl.BlockSpec(memory_space=pl.ANY)


def _wait_all(copies):
    for cp in copies:
        cp.wait()


def _gather_over_chips(pieces, name):
    n_arr = len(pieces)

    def body(*refs):
        srcs, dsts = refs[:n_arr], refs[n_arr:2 * n_arr]
        send_sems, recv_sems, local_sems = refs[2 * n_arr:]
        x, y, c, me = _my_position()
        copies = []
        for a in range(n_arr):
            r = srcs[a].shape[0]
            mine = dsts[a].at[pl.ds(me * r, r)]
            copies.append(pltpu.make_async_copy(srcs[a], mine, local_sems.at[a]))
            for km in range(1, N_CHIP):
                px, py, _ = _chip_peer(x, y, km)
                copies.append(pltpu.make_async_remote_copy(
                    src_ref=srcs[a], dst_ref=mine, send_sem=send_sems.at[a, km], recv_sem=recv_sems.at[a, km],
                    device_id=(px, py, c), device_id_type=MESH))
        for cp in copies:
            cp.start()
        _wait_all(copies)

    return pl.pallas_call(
        body, name=name, in_specs=[_HBM] * n_arr, out_specs=[_HBM] * n_arr,
        out_shape=[_sds((N_DEV * p.shape[0],) + p.shape[1:], p.dtype) for p in pieces],
        scratch_shapes=[pltpu.SemaphoreType.DMA((n_arr, N_CHIP)), pltpu.SemaphoreType.DMA((n_arr, N_CHIP)),
                        pltpu.SemaphoreType.DMA((n_arr,))],
    )(*pieces)


def _gather_over_cores(arrays, name):
    n_arr = len(arrays)

    def body(*refs):
        srcs, dsts = refs[:n_arr], refs[n_arr:2 * n_arr]
        send_sems, recv_sems = refs[2 * n_arr:]
        x, y, c, _ = _my_position()
        copies = []
        for a in range(n_arr):
            r = srcs[a].shape[0] // N_DEV
            for q in range(N_CHIP):
                rows = pl.ds((2 * q + c) * r, r)
                copies.append(pltpu.make_async_remote_copy(
                    src_ref=srcs[a].at[rows], dst_ref=dsts[a].at[rows], send_sem=send_sems.at[a, q],
                    recv_sem=recv_sems.at[a, q], device_id=(x, y, 1 - c), device_id_type=MESH))
        for cp in copies:
            cp.start()
        _wait_all(copies)

    return pl.pallas_call(
        body, name=name, in_specs=[_HBM] * n_arr, out_specs=[_HBM] * n_arr,
        out_shape=[_sds(a.shape, a.dtype) for a in arrays], input_output_aliases={a: a for a in range(n_arr)},
        scratch_shapes=[pltpu.SemaphoreType.DMA((n_arr, N_CHIP)), pltpu.SemaphoreType.DMA((n_arr, N_CHIP))],
    )(*arrays)


def _grads_to_sibling(grads, name):
    n_arr = len(grads)

    def body(*refs):
        srcs, dsts = refs[:n_arr], refs[n_arr:2 * n_arr]
        send_sems, recv_sems = refs[2 * n_arr:]
        x, y, c, _ = _my_position()
        copies = []
        for a in range(n_arr):
            r = srcs[a].shape[0] // N_DEV
            for q in range(N_CHIP):
                copies.append(pltpu.make_async_remote_copy(
                    src_ref=srcs[a].at[pl.ds((2 * q + 1 - c) * r, r)], dst_ref=dsts[a].at[q], send_sem=send_sems.at[a, q],
                    recv_sem=recv_sems.at[a, q], device_id=(x, y, 1 - c), device_id_type=MESH))
        for cp in copies:
            cp.start()
        _wait_all(copies)

    return pl.pallas_call(
        body, name=name, in_specs=[_HBM] * n_arr, out_specs=[_HBM] * n_arr,
        out_shape=[_sds((N_CHIP, g.shape[0] // N_DEV) + g.shape[1:], g.dtype) for g in grads],
        scratch_shapes=[pltpu.SemaphoreType.DMA((n_arr, N_CHIP)), pltpu.SemaphoreType.DMA((n_arr, N_CHIP))],
    )(*grads)


def _pair_sum(grad, from_sibling, name):
    r, n = from_sibling.shape[1:]

    def body(g_ref, s_ref, o_ref):
        c = lax.axis_index("c")
        o_ref[...] = (g_ref[c].astype(F32) + s_ref[...].astype(F32)).astype(BF16)

    return pl.pallas_call(
        body, name=name, grid=(N_CHIP,),
        in_specs=[pl.BlockSpec((None, 2, r, n), lambda q: (q, 0, 0, 0)), pl.BlockSpec((None, r, n), lambda q: (q, 0, 0))],
        out_specs=pl.BlockSpec((None, r, n), lambda q: (q, 0, 0)), out_shape=_sds((N_CHIP, r, n), BF16),
    )(grad.reshape(N_CHIP, 2, r, n), from_sibling)


def _grads_over_chips(sums, name):
    n_arr = len(sums)

    def body(*refs):
        srcs, dsts = refs[:n_arr], refs[n_arr:2 * n_arr]
        send_sems, recv_sems, local_sems = refs[2 * n_arr:]
        x, y, c, _ = _my_position()
        q_me = 2 * x + y
        copies = []
        for a in range(n_arr):
            copies.append(pltpu.make_async_copy(srcs[a].at[q_me], dsts[a].at[q_me], local_sems.at[a]))
            for km in range(1, N_CHIP):
                px, py, q_peer = _chip_peer(x, y, km)
                copies.append(pltpu.make_async_remote_copy(
                    src_ref=srcs[a].at[q_peer], dst_ref=dsts[a].at[q_me], send_sem=send_sems.at[a, km],
                    recv_sem=recv_sems.at[a, km], device_id=(px, py, c), device_id_type=MESH))
        for cp in copies:
            cp.start()
        _wait_all(copies)

    return pl.pallas_call(
        body, name=name, in_specs=[_HBM] * n_arr, out_specs=[_HBM] * n_arr,
        out_shape=[_sds(s.shape, s.dtype) for s in sums],
        scratch_shapes=[pltpu.SemaphoreType.DMA((n_arr, N_CHIP)), pltpu.SemaphoreType.DMA((n_arr, N_CHIP)),
                        pltpu.SemaphoreType.DMA((n_arr,))],
    )(*sums)


def _all_reduce_small(vec):
    r = vec.shape[0]

    def body(v_ref, o_ref, buf, send_sems, recv_sems):
        x, y, c, me = _my_position()
        buf[me] = v_ref[...]
        copies = []
        for mask in range(1, N_DEV):
            peer, _ = _peer(x, y, c, mask)
            cp = pltpu.make_async_remote_copy(src_ref=v_ref, dst_ref=buf.at[me], send_sem=send_sems.at[mask],
                                              recv_sem=recv_sems.at[mask], device_id=peer, device_id_type=MESH)
            cp.start()
            copies.append(cp)
        for cp in copies:
            cp.wait()
        total = buf[0]
        for s in range(1, N_DEV):
            total = total + buf[s]
        o_ref[...] = total

    vm = pl.BlockSpec(memory_space=pltpu.VMEM)
    return pl.pallas_call(
        body, name="all_reduce_small", in_specs=[vm], out_specs=vm, out_shape=_sds(vec.shape, F32),
        scratch_shapes=[pltpu.VMEM((N_DEV, r, 128), F32), pltpu.SemaphoreType.DMA((N_DEV,)), pltpu.SemaphoreType.DMA((N_DEV,))],
    )(vec)


def _adamw_math(w, g, m, v):
    m = ADAM_B1 * m + (1.0 - ADAM_B1) * g
    v = ADAM_B2 * v + (1.0 - ADAM_B2) * (g * g)
    m_hat = m / (1.0 - ADAM_B1 ** ADAM_STEP)
    v_hat = v / (1.0 - ADAM_B2 ** ADAM_STEP)
    delta = -ADAM_LR * (m_hat / (jnp.sqrt(v_hat) + ADAM_EPS) + ADAM_WD * w)
    return delta, m, v


def _reduce_adamw(slots, w, m, v, *, transpose, name):
    r, n = slots.shape[1:]

    def body(s_ref, w_ref, m_ref, v_ref, g_ref, d_ref, nm_ref, nv_ref, *scr):
        g = s_ref[0].astype(F32)
        for s in range(1, slots.shape[0]):
            g = g + s_ref[s].astype(F32)
        if transpose:
            scr[0][...] = g.T
            g = scr[0][:, 0:w_ref.shape[1]]
        g_ref[...] = g
        d_ref[...], nm_ref[...], nv_ref[...] = _adamw_math(w_ref[...], g, m_ref[...], v_ref[...])

    out = _sds(w.shape, F32)
    return pl.pallas_call(
        body, name=name, out_shape=[out, out, out, out],
        scratch_shapes=[pltpu.VMEM((n, r), F32)] if transpose else [],
    )(slots, w, m, v)


def _adamw(g, w, m, v, name):
    def body(g_ref, w_ref, m_ref, v_ref, d_ref, nm_ref, nv_ref):
        d_ref[...], nm_ref[...], nv_ref[...] = _adamw_math(w_ref[...], g_ref[...], m_ref[...], v_ref[...])

    out = _sds(w.shape, F32)
    return pl.pallas_call(body, name=name, out_shape=[out, out, out])(g, w, m, v)


_SMALL = ["ffn1_norm", "mix_norm", "ffn2_norm", "mla_q_lat_norm", "mla_kv_lat_norm", "mla_q_nope_gain", "mla_q_rope_gain",
          "mla_k_nope_gain", "mla_k_rope_gain", "fox_q_gain", "fox_k_gain", "fox_b_f"]
_WEIGHTS = ["ffn1_norm", "ffn1_w_gate", "ffn1_w_up", "ffn1_w_down", "mix_norm", "w_in", "mla_q_lat_norm", "mla_w_qb",
            "mla_kv_lat_norm", "mla_w_kvb", "mla_q_nope_gain", "mla_q_rope_gain", "mla_k_nope_gain", "mla_k_rope_gain",
            "fox_q_gain", "fox_k_gain", "fox_b_f", "w_branch_mla", "w_branch_fox", "b_gate", "w_o", "ffn2_norm",
            "ffn2_w_gate", "ffn2_w_up", "ffn2_w_down"]
_IN_Q, _IN_KV, _IN_KR, _IN_FOX, _IN_F, _IN_GATES = (0, 192), (192, 128), (320, 32), (352, 1536), (1888, 8), (1896, 2048)


def _rows(a, seg):
    return a[seg[0]:seg[0] + seg[1]]


def _split_w_in(win_t):
    z = lambda n: jnp.zeros((n, D_MODEL), win_t.dtype)
    lat = jnp.concatenate([_rows(win_t, _IN_Q), z(LAT_KV - Q_LORA), _rows(win_t, _IN_KV), _rows(win_t, _IN_KR),
                           _rows(win_t, _IN_F), z(LAT_W - LAT_F - N_HEADS)], axis=0)
    return _rows(win_t, _IN_GATES), _rows(win_t, _IN_FOX), lat


def _join_w_in(d_gates, d_fox, d_lat):
    return jnp.concatenate([d_lat[LAT_Q:LAT_Q + Q_LORA], d_lat[LAT_KV:LAT_KV + KV_LORA], d_lat[LAT_KR:LAT_KR + ROPE_DIM],
                            d_fox, d_lat[LAT_F:LAT_F + N_HEADS], d_gates], axis=0)


def kernel(x, positions, ffn1_norm, ffn1_w_gate, ffn1_w_up, ffn1_w_down, mix_norm, w_in, mla_q_lat_norm, mla_w_qb, mla_kv_lat_norm, mla_w_kvb, mla_q_nope_gain, mla_q_rope_gain, mla_k_nope_gain, mla_k_rope_gain, fox_q_gain, fox_k_gain, fox_b_f, w_branch_mla, w_branch_fox, b_gate, w_o, ffn2_norm, ffn2_w_gate, ffn2_w_up, ffn2_w_down, loss_target, m_ffn1_norm, m_ffn1_w_gate, m_ffn1_w_up, m_ffn1_w_down, m_mix_norm, m_w_in, m_mla_q_lat_norm, m_mla_w_qb, m_mla_kv_lat_norm, m_mla_w_kvb, m_mla_q_nope_gain, m_mla_q_rope_gain, m_mla_k_nope_gain, m_mla_k_rope_gain, m_fox_q_gain, m_fox_k_gain, m_fox_b_f, m_w_branch_mla, m_w_branch_fox, m_b_gate, m_w_o, m_ffn2_norm, m_ffn2_w_gate, m_ffn2_w_up, m_ffn2_w_down, v_ffn1_norm, v_ffn1_w_gate, v_ffn1_w_up, v_ffn1_w_down, v_mix_norm, v_w_in, v_mla_q_lat_norm, v_mla_w_qb, v_mla_kv_lat_norm, v_mla_w_kvb, v_mla_q_nope_gain, v_mla_q_rope_gain, v_mla_k_nope_gain, v_mla_k_rope_gain, v_fox_q_gain, v_fox_k_gain, v_fox_b_f, v_w_branch_mla, v_w_branch_fox, v_b_gate, v_w_o, v_ffn2_norm, v_ffn2_w_gate, v_ffn2_w_up, v_ffn2_w_down):
    env = dict(locals())
    strip = lambda n, a: a if n in _SMALL else a[0]
    W = {n: strip(n, env[n]) for n in _WEIGHTS}
    M = {n: strip(n, env["m_" + n]) for n in _WEIGHTS}
    V = {n: strip(n, env["v_" + n]) for n in _WEIGHTS}
    xs = x[0]
    t = xs.shape[0]

    col_split = ["ffn1_w_gate", "ffn1_w_up", "ffn2_w_gate", "ffn2_w_up", "mla_w_qb", "mla_w_kvb", "w_branch_mla", "w_branch_fox"]
    row_split = ["ffn1_w_down", "ffn2_w_down", "w_o"]
    pieces = {n: W[n].T.astype(BF16) for n in col_split}
    pieces.update({n: W[n].astype(BF16) for n in row_split})
    pieces["w_in"] = jnp.pad(W["w_in"].T.astype(BF16), ((0, W_IN_PIECE_PAD - W_IN_PIECE), (0, 0)))
    pieces["b_gate"] = W["b_gate"].T
    order = col_split + row_split + ["w_in", "b_gate"]
    G = dict(zip(order, _gather_over_cores(_gather_over_chips([pieces[n] for n in order], "gather_ici"), "gather_d2d")))
    win_t = G["w_in"].reshape(N_DEV, W_IN_PIECE_PAD, D_MODEL)[:, :W_IN_PIECE].reshape(N_DEV * W_IN_PIECE, D_MODEL)
    wgate_t, wfox_t, wlat_t = _split_w_in(win_t)
    bg = G["b_gate"].T

    inv_freq = ROPE_THETA ** (-jnp.arange(ROPE_HALF, dtype=F32) / ROPE_HALF)
    ang = positions[0].astype(F32)[:, None] * inv_freq
    cos, sin = jnp.cos(ang), jnp.sin(ang)
    prep_small = [W[n] for n in ["mla_q_lat_norm", "mla_kv_lat_norm", "mla_q_nope_gain", "mla_q_rope_gain", "mla_k_nope_gain",
                                 "mla_k_rope_gain", "fox_q_gain", "fox_k_gain", "fox_b_f"]]

    x1, a1, b1 = _ffn_fwd(xs, W["ffn1_norm"], G["ffn1_w_gate"], G["ffn1_w_up"], G["ffn1_w_down"])
    gates, fox, lat = _proj_fwd(x1, W["mix_norm"], wgate_t, wfox_t, wlat_t)
    fq, fk, fv, c, mq, mk, mv = _prep_fwd(fox, lat, cos, sin, G["mla_w_qb"], G["mla_w_kvb"], *prep_small)
    c_t = c.T
    b_fox, ok_fox = _logit_bound(fq, fk, lanes=HEAD_DIM, scale=1.0, name="fox_norms")
    b_mla, ok_mla = _logit_bound(mq, mk, lanes=MLA_QK_LANES, scale=MLA_SCALE, name="mla_norms")
    y_fox, y_fox32, lse_fox = _attn_fwd(fq, fk, fv, b_fox, ok_fox, c, c_t, lanes=HEAD_DIM, scale=1.0, name="fox_fwd")
    y_mla, y_mla32, lse_mla = _attn_fwd(mq, mk, mv, b_mla, ok_mla, None, None, lanes=MLA_QK_LANES, scale=MLA_SCALE, name="mla_fwd")
    x2 = _mix_fwd(x1, y_mla, y_fox, gates, bg, G["w_branch_mla"], G["w_branch_fox"], G["w_o"])
    dx3, a2, b2, loss_vec = _ffn_fwd(x2, W["ffn2_norm"], G["ffn2_w_gate"], G["ffn2_w_up"], G["ffn2_w_down"], target=loss_target[0])

    dx2, dg_ffn2, da2, db2, h2, n2, dyh2 = _ffn_bwd(dx3, x2, W["ffn2_norm"], a2, b2, G["ffn2_w_gate"], G["ffn2_w_up"],
                                                   G["ffn2_w_down"], "ffn2_bwd")
    grads = {"ffn2_w_gate": _tn_matmul(da2, n2, "ffn2_dgate"), "ffn2_w_up": _tn_matmul(db2, n2, "ffn2_dup"),
             "ffn2_w_down": _tn_matmul(h2, dyh2, "ffn2_ddown")}
    dy_mla, dy_fox, delta_mla, delta_fox, dgates, mixed, dum, duf, dx2b, dbg = _mix_bwd(
        dx2, y_mla, y_fox, y_mla32, y_fox32, gates, bg, G["w_branch_mla"], G["w_branch_fox"], G["w_o"])
    grads["w_o"] = _tn_matmul(mixed, dx2b, "d_w_o")
    grads["w_branch_mla"] = _tn_matmul(dum, y_mla, "d_w_branch_mla")
    grads["w_branch_fox"] = _tn_matmul(duf, y_fox, "d_w_branch_fox")
    dfq, dfk, dfv, dc_t = _attn_bwd(fq, fk, fv, dy_fox, delta_fox, lse_fox, c, c_t, lanes=HEAD_DIM, scale=1.0, name="fox_bwd")
    dmq, dmk, dmv = _attn_bwd(mq, mk, mv, dy_mla, delta_mla, lse_mla, None, None, lanes=MLA_QK_LANES, scale=MLA_SCALE, name="mla_bwd")
    prep_out = _prep_bwd(fox, lat, cos, sin, G["mla_w_qb"], G["mla_w_kvb"], *prep_small, dfq, dfk, dfv, dc_t.T, dmq, dmk, dmv)
    dfox, dlat, grads["mla_w_qb"], grads["mla_w_kvb"] = prep_out[:4]
    d_prep_small = prep_out[4:]
    dx1, dg_mix, nmix = _proj_bwd(dgates, dfox, dlat, wgate_t, wfox_t, wlat_t, x1, W["mix_norm"], dx2)
    dwin_t = _join_w_in(_tn_matmul(dgates, nmix, "d_w_in_gates"), _tn_matmul(dfox, nmix, "d_w_in_fox"),
                        _tn_matmul(dlat, nmix, "d_w_in_lat"))
    grads["w_in"] = jnp.pad(dwin_t.reshape(N_DEV, W_IN_PIECE, D_MODEL), ((0, 0), (0, W_IN_PIECE_PAD - W_IN_PIECE), (0, 0))
                            ).reshape(N_DEV * W_IN_PIECE_PAD, D_MODEL)
    dx0, dg_ffn1, da1, db1, h1, n1, dyh1 = _ffn_bwd(dx1, xs, W["ffn1_norm"], a1, b1, G["ffn1_w_gate"], G["ffn1_w_up"],
                                                   G["ffn1_w_down"], "ffn1_bwd")
    grads["ffn1_w_gate"] = _tn_matmul(da1, n1, "ffn1_dgate")
    grads["ffn1_w_up"] = _tn_matmul(db1, n1, "ffn1_dup")
    grads["ffn1_w_down"] = _tn_matmul(h1, dyh1, "ffn1_ddown")

    big = col_split + row_split + ["w_in"]
    from_sibling = _grads_to_sibling([grads[n] for n in big], "grads_d2d")
    sums = [_pair_sum(grads[n], s, "pair_sum_" + n) for n, s in zip(big, from_sibling)]
    slots = dict(zip(big, _grads_over_chips(sums, "grads_ici")))
    small_parts = [dg_ffn1, dg_mix, dg_ffn2] + list(d_prep_small) + [dbg.reshape(1, 2 * D_MODEL), loss_vec]
    flat = jnp.concatenate([p.reshape(-1) for p in small_parts])
    n_flat = flat.shape[0]
    rows = -(-n_flat // (8 * 128)) * 8
    total = _all_reduce_small(jnp.pad(flat, (0, rows * 128 - n_flat)).reshape(rows, 128)).reshape(-1)
    offs, small_g = 0, {}
    for n in _SMALL:
        small_g[n] = total[offs:offs + W[n].shape[1]].reshape(W[n].shape)
        offs += W[n].shape[1]
    bg_full = total[offs:offs + 2 * D_MODEL].reshape(2, D_MODEL)
    offs += 2 * D_MODEL
    loss = (0.5 / D_MODEL) * jnp.sum(total[offs:offs + D_MODEL])
    _, _, _, me = _my_position()
    small_g["b_gate"] = lax.dynamic_slice_in_dim(bg_full, me * (D_MODEL // N_DEV), D_MODEL // N_DEV, axis=1)

    res = {}
    for n in _WEIGHTS:
        if n in small_g:
            res[n] = (small_g[n],) + tuple(_adamw(small_g[n], W[n], M[n], V[n], "adamw_" + n))
        else:
            res[n] = tuple(_reduce_adamw(slots[n], W[n], M[n], V[n], transpose=n in col_split or n == "w_in", name="adamw_" + n))
    outs = [loss, dx0[None]]
    for k in range(4):
        outs += [res[n][k] if n in _SMALL else res[n][k][None] for n in _WEIGHTS]
    return tuple(outs)
```

```python
import functools

import jax
import jax.numpy as jnp
from jax import lax
from jax.experimental import pallas as pl
from jax.experimental.pallas import tpu as pltpu

F32 = jnp.float32
BF16 = jnp.bfloat16

D_MODEL = 1024
FFN_HIDDEN = 2816
N_HEADS = 8
HEAD_DIM = 64
ROPE_DIM = 32
ROPE_HALF = 16
Q_LORA = 192
KV_LORA = 128
ROPE_THETA = 10000.0
RMS_EPS = 1e-6
MLA_SCALE = (HEAD_DIM + ROPE_DIM) ** -0.5
FOX_SCALE = HEAD_DIM ** -0.5
MLA_QK_LANES = 128
ADAM_LR, ADAM_B1, ADAM_B2, ADAM_EPS, ADAM_WD, ADAM_STEP = 0.001, 0.9, 0.999, 1e-08, 0.01, 10
N_DEV = 8
N_CHIP = 4
W_IN_PIECE = 493
W_IN_PIECE_PAD = 496
LAT_W = 512
LAT_Q, LAT_KV, LAT_KR, LAT_F = 0, 256, 384, 416
MASK_VALUE = -1e30
FIXED_SHIFT_MAX_BOUND = 30.0

TOK_TILE = 512
PROJ_TILE = 256
ATT_TILE = 512
ATT_COL_CHUNK = 256
FFN_HID_TILE = 1408

NT = (((1,), (1,)), ((), ()))
TN = (((0,), (0,)), ((), ()))
NN = (((1,), (0,)), ((), ()))
MESH = pl.DeviceIdType.MESH


def _dot(a, b, dims=NN):
    return lax.dot_general(a, b, dims, preferred_element_type=F32)


def _sds(shape, dtype):
    return jax.ShapeDtypeStruct(shape, dtype)


def _rms_fwd(x, g):
    r = lax.rsqrt(jnp.mean(x * x, axis=-1, keepdims=True) + RMS_EPS)
    return x * r * g, r


def _rms_bwd(dy, x, g, r):
    xn = x * r
    dyg = dy * g
    dx = r * (dyg - xn * jnp.mean(dyg * xn, axis=-1, keepdims=True))
    return dx, dy * xn


def _colsum(x):
    return jnp.sum(x, axis=0, keepdims=True)


def _full(shape):
    return pl.BlockSpec(shape, lambda *_: (0,) * len(shape))


def _tok(tm, n):
    return pl.BlockSpec((tm, n), lambda i, *_: (i, 0))


def _ffn_fwd(x, gain, wg_t, wu_t, wd, target=None, exchange=None):
    t = x.shape[0]
    tm = min(TOK_TILE, t)
    tf = FFN_HID_TILE
    n_t, n_f = t // tm, FFN_HIDDEN // tf
    with_loss = target is not None

    def body(*refs):
        if with_loss:
            x_ref, g_ref, wg_ref, wu_ref, wd_ref, t_ref, out_ref, a_ref, b_ref, lvec_ref, n_scr, acc = refs
        else:
            x_ref, g_ref, wg_ref, wu_ref, wd_ref, out_ref, a_ref, b_ref, n_scr, acc = refs
        i, j = pl.program_id(0), pl.program_id(1)

        @pl.when(j == 0)
        def _():
            xn, _ = _rms_fwd(x_ref[...], g_ref[...])
            n_scr[...] = xn.astype(BF16)
            acc[...] = jnp.zeros_like(acc)

        n = n_scr[...]
        a = _dot(n, wg_ref[...], NT)
        b = _dot(n, wu_ref[...], NT)
        a_ref[...] = a.astype(BF16)
        b_ref[...] = b.astype(BF16)
        h = (a * jax.nn.sigmoid(a)) * b
        acc[...] += _dot(h.astype(BF16), wd_ref[...])

        @pl.when(j == n_f - 1)
        def _():
            y = x_ref[...] + 0.5 * acc[...]
            if with_loss:
                diff = y - t_ref[...]
                out_ref[...] = diff * (1.0 / D_MODEL)
                sq = _colsum(diff * diff)

                @pl.when(i == 0)
                def _():
                    lvec_ref[...] = sq

                @pl.when(i > 0)
                def _():
                    lvec_ref[...] += sq
            else:
                out_ref[...] = y

    wspec = pl.BlockSpec((tf, D_MODEL), lambda i, j: (j, 0))
    hspec = pl.BlockSpec((tm, tf), lambda i, j: (i, j))
    in_specs = [_tok(tm, D_MODEL), _full((1, D_MODEL)), wspec, wspec, wspec]
    out_specs = [_tok(tm, D_MODEL), hspec, hspec]
    out_shape = [_sds((t, D_MODEL), F32), _sds((t, FFN_HIDDEN), BF16), _sds((t, FFN_HIDDEN), BF16)]
    args = [x, gain, wg_t, wu_t, wd]
    if with_loss:
        in_specs.append(_tok(tm, D_MODEL))
        out_specs.append(_full((1, D_MODEL)))
        out_shape.append(_sds((1, D_MODEL), F32))
        args.append(target)
    return _gridded_call(
        body, name="ffn_fwd_loss" if with_loss else "ffn_fwd", grid=(n_t, n_f), in_specs=in_specs, out_specs=out_specs,
        out_shape=out_shape, scratch_shapes=[pltpu.VMEM((tm, D_MODEL), BF16), pltpu.VMEM((tm, D_MODEL), F32)],
        args=args, exchange=exchange)


def _ffn_bwd(dy, x, gain, a, b, wg_t, wu_t, wd, name, exchange=None):
    t = x.shape[0]
    tm = min(PROJ_TILE, t)
    tf = FFN_HID_TILE
    n_t, n_f = t // tm, FFN_HIDDEN // tf

    def body(dy_ref, x_ref, g_ref, a_ref, b_ref, wg_ref, wu_ref, wd_ref,
             dx_ref, dg_ref, da_ref, db_ref, h_ref, n_ref, dyh_ref, acc):
        i, j = pl.program_id(0), pl.program_id(1)

        @pl.when(j == 0)
        def _():
            xn, _ = _rms_fwd(x_ref[...], g_ref[...])
            n_ref[...] = xn.astype(BF16)
            dyh_ref[...] = (0.5 * dy_ref[...]).astype(BF16)
            acc[...] = jnp.zeros_like(acc)

        dh = _dot(dyh_ref[...], wd_ref[...], NT)
        av = a_ref[...].astype(F32)
        bv = b_ref[...].astype(F32)
        s = jax.nn.sigmoid(av)
        silu = av * s
        da = (dh * bv * (s * (1.0 + av * (1.0 - s)))).astype(BF16)
        db = (dh * silu).astype(BF16)
        da_ref[...] = da
        db_ref[...] = db
        h_ref[...] = (silu * bv).astype(BF16)
        acc[...] += _dot(da, wg_ref[...]) + _dot(db, wu_ref[...])

        @pl.when(j == n_f - 1)
        def _():
            xv, g = x_ref[...], g_ref[...]
            r = lax.rsqrt(jnp.mean(xv * xv, axis=-1, keepdims=True) + RMS_EPS)
            dx, dg_rows = _rms_bwd(acc[...], xv, g, r)
            dx_ref[...] = dy_ref[...] + dx
            dg = _colsum(dg_rows)

            @pl.when(i == 0)
            def _():
                dg_ref[...] = dg

            @pl.when(i > 0)
            def _():
                dg_ref[...] += dg

    wspec = pl.BlockSpec((tf, D_MODEL), lambda i, j: (j, 0))
    hspec = pl.BlockSpec((tm, tf), lambda i, j: (i, j))
    tok = _tok(tm, D_MODEL)
    hid = _sds((t, FFN_HIDDEN), BF16)
    return _gridded_call(
        body, name=name, grid=(n_t, n_f),
        in_specs=[tok, tok, _full((1, D_MODEL)), hspec, hspec, wspec, wspec, wspec],
        out_specs=[tok, _full((1, D_MODEL)), hspec, hspec, hspec, tok, tok],
        out_shape=[_sds((t, D_MODEL), F32), _sds((1, D_MODEL), F32), hid, hid, hid,
                   _sds((t, D_MODEL), BF16), _sds((t, D_MODEL), BF16)],
        scratch_shapes=[pltpu.VMEM((tm, D_MODEL), F32)], args=[dy, x, gain, a, b, wg_t, wu_t, wd], exchange=exchange)


def _tn_matmul(a, b, name):
    t, m = a.shape
    n = b.shape[1]
    tk = min(TOK_TILE, t)
    n_k = t // tk

    def body(a_ref, b_ref, o_ref, acc):
        k = pl.program_id(0)
        p = _dot(a_ref[...], b_ref[...], TN)

        @pl.when(k == 0)
        def _():
            acc[...] = p

        @pl.when(k > 0)
        def _():
            acc[...] += p

        @pl.when(k == n_k - 1)
        def _():
            o_ref[...] = acc[...].astype(BF16)

    return pl.pallas_call(
        body, name=name, grid=(n_k,), in_specs=[_tok(tk, m), _tok(tk, n)], out_specs=_full((m, n)),
        out_shape=_sds((m, n), BF16), scratch_shapes=[pltpu.VMEM((m, n), F32)],
        compiler_params=pltpu.CompilerParams(dimension_semantics=("arbitrary",)),
    )(a, b)


def _proj_fwd(x, gain, wgate_t, wfox_t, wlat_t):
    t = x.shape[0]
    tm = min(PROJ_TILE, t)

    def body(x_ref, g_ref, wg_ref, wf_ref, wl_ref, og_ref, of_ref, ol_ref):
        xn, _ = _rms_fwd(x_ref[...], g_ref[...])
        n = xn.astype(BF16)
        og_ref[...] = _dot(n, wg_ref[...], NT)
        of_ref[...] = _dot(n, wf_ref[...], NT)
        ol_ref[...] = _dot(n, wl_ref[...], NT)

    return pl.pallas_call(
        body, name="proj_fwd", grid=(t // tm,),
        in_specs=[_tok(tm, D_MODEL), _full((1, D_MODEL)), _full(wgate_t.shape), _full(wfox_t.shape), _full(wlat_t.shape)],
        out_specs=[_tok(tm, 2 * D_MODEL), _tok(tm, 3 * 512), _tok(tm, LAT_W)],
        out_shape=[_sds((t, 2 * D_MODEL), F32), _sds((t, 3 * 512), F32), _sds((t, LAT_W), F32)],
        compiler_params=pltpu.CompilerParams(dimension_semantics=("arbitrary",)),
    )(x, gain, wgate_t, wfox_t, wlat_t)


def _proj_bwd(dgates, dfox, dlat, wgate_t, wfox_t, wlat_t, x, gain, dres):
    t = x.shape[0]
    tm = min(PROJ_TILE, t)

    def body(dg_ref, df_ref, dl_ref, wg_ref, wf_ref, wl_ref, x_ref, g_ref, dres_ref, dx_ref, dgain_ref, n_ref):
        i = pl.program_id(0)
        dn = _dot(dg_ref[...], wg_ref[...]) + _dot(df_ref[...], wf_ref[...]) + _dot(dl_ref[...], wl_ref[...])
        xv, g = x_ref[...], g_ref[...]
        xn, r = _rms_fwd(xv, g)
        n_ref[...] = xn.astype(BF16)
        dx, dg_rows = _rms_bwd(dn, xv, g, r)
        dx_ref[...] = dres_ref[...] + dx
        dgn = _colsum(dg_rows)

        @pl.when(i == 0)
        def _():
            dgain_ref[...] = dgn

        @pl.when(i > 0)
        def _():
            dgain_ref[...] += dgn

    tok = _tok(tm, D_MODEL)
    return pl.pallas_call(
        body, name="proj_bwd", grid=(t // tm,),
        in_specs=[_tok(tm, 2 * D_MODEL), _tok(tm, 3 * 512), _tok(tm, LAT_W), _full(wgate_t.shape), _full(wfox_t.shape),
                  _full(wlat_t.shape), tok, _full((1, D_MODEL)), tok],
        out_specs=[tok, _full((1, D_MODEL)), tok],
        out_shape=[_sds((t, D_MODEL), F32), _sds((1, D_MODEL), F32), _sds((t, D_MODEL), BF16)],
        compiler_params=pltpu.CompilerParams(dimension_semantics=("arbitrary",)),
    )(dgates, dfox, dlat, wgate_t, wfox_t, wlat_t, x, gain, dres)


def _tri(n, lower):
    r = lax.broadcasted_iota(jnp.int32, (n, n), 0)
    c = lax.broadcasted_iota(jnp.int32, (n, n), 1)
    return ((c <= r) if lower else (c >= r)).astype(F32)


def _log_sigmoid(z):
    return jnp.minimum(z, 0.0) - jnp.log1p(jnp.exp(-jnp.abs(z)))


def _rope_pair_fwd(p1, p2, g1, g2, cos, sin):
    ss = jnp.sum(p1 * p1, axis=-1, keepdims=True) + jnp.sum(p2 * p2, axis=-1, keepdims=True)
    r = lax.rsqrt(ss * (1.0 / ROPE_DIM) + RMS_EPS)
    n1, n2 = p1 * r, p2 * r
    y1, y2 = n1 * g1, n2 * g2
    return y1 * cos - y2 * sin, y2 * cos + y1 * sin, n1, n2, r


def _rope_pair_bwd(do1, do2, n1, n2, r, g1, g2, cos, sin):
    dy1 = do1 * cos + do2 * sin
    dy2 = do2 * cos - do1 * sin
    dyg1, dyg2 = dy1 * g1, dy2 * g2
    mu = (jnp.sum(dyg1 * n1, axis=-1, keepdims=True) + jnp.sum(dyg2 * n2, axis=-1, keepdims=True)) * (1.0 / ROPE_DIM)
    return r * (dyg1 - n1 * mu), r * (dyg2 - n2 * mu), dy1 * n1, dy2 * n2


def _prep_fwd(fox, lat, cos, sin, wqb_t, wkvb_t, g_qlat, g_kvlat, g_qn, g_qr, g_kn, g_kr, g_fq, g_fk, b_f):
    t = fox.shape[0]
    tm = min(PROJ_TILE, t)
    hd, lanes = HEAD_DIM, MLA_QK_LANES

    def body(fox_ref, lat_ref, cos_ref, sin_ref, wqb_ref, wkvb_ref, gql_ref, gkvl_ref, gqn_ref, gqr_ref, gkn_ref,
             gkr_ref, gfq_ref, gfk_ref, bf_ref, fq_ref, fk_ref, fv_ref, c_ref, mq_ref, mk_ref, mv_ref,
             carry, qscr, kscr):
        i = pl.program_id(0)
        cos_v, sin_v = cos_ref[...], sin_ref[...]

        for h in range(N_HEADS):
            w = slice(h * hd, (h + 1) * hd)
            qn, _ = _rms_fwd(fox_ref[:, h * hd:(h + 1) * hd], gfq_ref[...])
            fq_ref[:, w] = (qn * FOX_SCALE).astype(BF16)
            kn, _ = _rms_fwd(fox_ref[:, 512 + h * hd:512 + (h + 1) * hd], gfk_ref[...])
            fk_ref[:, w] = kn.astype(BF16)
        fv_ref[...] = fox_ref[:, 1024:1536].astype(BF16)

        @pl.when(i == 0)
        def _():
            carry[...] = jnp.zeros_like(carry)

        logf = _log_sigmoid(lat_ref[:, LAT_F:LAT_F + N_HEADS] + bf_ref[...])
        c_ref[...] = jnp.dot(_tri(tm, True), logf, precision=lax.Precision.HIGHEST, preferred_element_type=F32) + carry[...]
        carry[...] += _colsum(logf)

        qlat_n, _ = _rms_fwd(lat_ref[:, LAT_Q:LAT_Q + Q_LORA], gql_ref[...])
        qlat_b = qlat_n.astype(BF16)
        g1, g2 = gqr_ref[:, 0:ROPE_HALF], gqr_ref[:, ROPE_HALF:ROPE_DIM]
        qscr[...] = jnp.zeros_like(qscr)
        for h in range(N_HEADS):
            base = h * (hd + ROPE_DIM)
            pn = _dot(qlat_b, wqb_ref[base:base + hd, :], NT)
            p1 = _dot(qlat_b, wqb_ref[base + hd:base + hd + ROPE_HALF, :], NT)
            p2 = _dot(qlat_b, wqb_ref[base + hd + ROPE_HALF:base + hd + ROPE_DIM, :], NT)
            nope, _ = _rms_fwd(pn, gqn_ref[...])
            o1, o2, _, _, _ = _rope_pair_fwd(p1, p2, g1, g2, cos_v, sin_v)
            qscr[:, h * lanes:h * lanes + hd] = nope
            qscr[:, h * lanes + hd:h * lanes + hd + ROPE_HALF] = o1
            qscr[:, h * lanes + hd + ROPE_HALF:h * lanes + hd + ROPE_DIM] = o2
        mq_ref[...] = qscr[...].astype(BF16)

        kv_n, _ = _rms_fwd(lat_ref[:, LAT_KV:LAT_KV + KV_LORA], gkvl_ref[...])
        kv_b = kv_n.astype(BF16)
        kr1, kr2, _, _, _ = _rope_pair_fwd(lat_ref[:, LAT_KR:LAT_KR + ROPE_HALF], lat_ref[:, LAT_KR + ROPE_HALF:LAT_KR + ROPE_DIM],
                                           gkr_ref[:, 0:ROPE_HALF], gkr_ref[:, ROPE_HALF:ROPE_DIM], cos_v, sin_v)
        kscr[...] = jnp.zeros_like(kscr)
        for h in range(N_HEADS):
            pk = _dot(kv_b, wkvb_ref[h * 2 * hd:h * 2 * hd + hd, :], NT)
            kn, _ = _rms_fwd(pk, gkn_ref[...])
            kscr[:, h * lanes:h * lanes + hd] = kn
            kscr[:, h * lanes + hd:h * lanes + hd + ROPE_HALF] = kr1
            kscr[:, h * lanes + hd + ROPE_HALF:h * lanes + hd + ROPE_DIM] = kr2
            mv_ref[:, h * hd:(h + 1) * hd] = _dot(kv_b, wkvb_ref[h * 2 * hd + hd:(h + 1) * 2 * hd, :], NT).astype(BF16)
        mk_ref[...] = kscr[...].astype(BF16)

    small = [g_qlat, g_kvlat, g_qn, g_qr, g_kn, g_kr, g_fq, g_fk, b_f]
    t512 = _tok(tm, 512)
    return pl.pallas_call(
        body, name="prep_fwd", grid=(t // tm,),
        in_specs=[_tok(tm, 1536), _tok(tm, LAT_W), _tok(tm, ROPE_HALF), _tok(tm, ROPE_HALF), _full(wqb_t.shape),
                  _full(wkvb_t.shape)] + [_full(s.shape) for s in small],
        out_specs=[t512, t512, t512, _tok(tm, N_HEADS), _tok(tm, N_HEADS * lanes), _tok(tm, N_HEADS * lanes), t512],
        out_shape=[_sds((t, 512), BF16), _sds((t, 512), BF16), _sds((t, 512), BF16), _sds((t, N_HEADS), F32),
                   _sds((t, N_HEADS * lanes), BF16), _sds((t, N_HEADS * lanes), BF16), _sds((t, 512), BF16)],
        scratch_shapes=[pltpu.VMEM((1, N_HEADS), F32), pltpu.VMEM((tm, N_HEADS * lanes), F32),
                        pltpu.VMEM((tm, N_HEADS * lanes), F32)],
        compiler_params=pltpu.CompilerParams(dimension_semantics=("arbitrary",)),
    )(fox, lat, cos, sin, wqb_t, wkvb_t, *small)


def _prep_bwd(fox, lat, cos, sin, wqb_t, wkvb_t, g_qlat, g_kvlat, g_qn, g_qr, g_kn, g_kr, g_fq, g_fk, b_f,
              dfq, dfk, dfv, dc, dmq, dmk, dmv):
    t = fox.shape[0]
    tm = min(PROJ_TILE, t)
    n_t = t // tm
    hd, lanes = HEAD_DIM, MLA_QK_LANES

    def body(fox_ref, lat_ref, cos_ref, sin_ref, wqb_ref, wkvb_ref, gql_ref, gkvl_ref, gqn_ref, gqr_ref, gkn_ref,
             gkr_ref, gfq_ref, gfk_ref, bf_ref, dfq_ref, dfk_ref, dfv_ref, dc_ref, dmq_ref, dmk_ref, dmv_ref,
             dfox_ref, dlat_ref, dwqb_ref, dwkvb_ref, o_gql, o_gkvl, o_gqn, o_gqr, o_gkn, o_gkr, o_gfq, o_gfk, o_bf,
             carry, lscr, wq_acc, wkv_acc):
        i = pl.program_id(0)
        cos_v, sin_v = cos_ref[...], sin_ref[...]
        small_out = [o_gql, o_gkvl, o_gqn, o_gqr, o_gkn, o_gkr, o_gfq, o_gfk, o_bf]

        @pl.when(i == 0)
        def _():
            carry[...] = jnp.zeros_like(carry)
            wq_acc[...] = jnp.zeros_like(wq_acc)
            wkv_acc[...] = jnp.zeros_like(wkv_acc)
            for o in small_out:
                o[...] = jnp.zeros_like(o)

        d_gfq = jnp.zeros((1, hd), F32)
        d_gfk = jnp.zeros((1, hd), F32)
        for h in range(N_HEADS):
            w = slice(h * hd, (h + 1) * hd)
            xq = fox_ref[:, h * hd:(h + 1) * hd]
            rq = lax.rsqrt(jnp.mean(xq * xq, axis=-1, keepdims=True) + RMS_EPS)
            dxq, gq_rows = _rms_bwd(dfq_ref[:, w] * FOX_SCALE, xq, gfq_ref[...], rq)
            dfox_ref[:, w] = dxq.astype(BF16)
            d_gfq += _colsum(gq_rows)
            xk = fox_ref[:, 512 + h * hd:512 + (h + 1) * hd]
            rk = lax.rsqrt(jnp.mean(xk * xk, axis=-1, keepdims=True) + RMS_EPS)
            dxk, gk_rows = _rms_bwd(dfk_ref[:, w], xk, gfk_ref[...], rk)
            dfox_ref[:, 512 + h * hd:512 + (h + 1) * hd] = dxk.astype(BF16)
            d_gfk += _colsum(gk_rows)
        dfox_ref[:, 1024:1536] = dfv_ref[...].astype(BF16)
        o_gfq[...] += d_gfq
        o_gfk[...] += d_gfk

        lscr[...] = jnp.zeros_like(lscr)

        dcv = dc_ref[...]
        dlogf = jnp.dot(_tri(tm, False), dcv, precision=lax.Precision.HIGHEST, preferred_element_type=F32) + carry[...]
        carry[...] += _colsum(dcv)
        z = lat_ref[:, LAT_F:LAT_F + N_HEADS] + bf_ref[...]
        dz = dlogf * jax.nn.sigmoid(-z)
        lscr[:, LAT_F:LAT_F + N_HEADS] = dz
        o_bf[...] += _colsum(dz)

        xql = lat_ref[:, LAT_Q:LAT_Q + Q_LORA]
        qlat_n, r_ql = _rms_fwd(xql, gql_ref[...])
        qlat_b = qlat_n.astype(BF16)
        g1, g2 = gqr_ref[:, 0:ROPE_HALF], gqr_ref[:, ROPE_HALF:ROPE_DIM]
        dqlat_n = jnp.zeros((tm, Q_LORA), F32)
        d_gqn = jnp.zeros((1, hd), F32)
        d_g1 = jnp.zeros((1, ROPE_HALF), F32)
        d_g2 = jnp.zeros((1, ROPE_HALF), F32)
        for h in range(N_HEADS):
            base = h * (hd + ROPE_DIM)
            rows_n = slice(base, base + hd)
            rows_1 = slice(base + hd, base + hd + ROPE_HALF)
            rows_2 = slice(base + hd + ROPE_HALF, base + hd + ROPE_DIM)
            pn = _dot(qlat_b, wqb_ref[rows_n, :], NT)
            p1 = _dot(qlat_b, wqb_ref[rows_1, :], NT)
            p2 = _dot(qlat_b, wqb_ref[rows_2, :], NT)
            r_n = lax.rsqrt(jnp.mean(pn * pn, axis=-1, keepdims=True) + RMS_EPS)
            _, _, n1, n2, r_r = _rope_pair_fwd(p1, p2, g1, g2, cos_v, sin_v)
            dpn, gn_rows = _rms_bwd(dmq_ref[:, h * lanes:h * lanes + hd], pn, gqn_ref[...], r_n)
            dp1, dp2, g1_rows, g2_rows = _rope_pair_bwd(
                dmq_ref[:, h * lanes + hd:h * lanes + hd + ROPE_HALF],
                dmq_ref[:, h * lanes + hd + ROPE_HALF:h * lanes + hd + ROPE_DIM], n1, n2, r_r, g1, g2, cos_v, sin_v)
            d_gqn += _colsum(gn_rows)
            d_g1 += _colsum(g1_rows)
            d_g2 += _colsum(g2_rows)
            dpn_b, dp1_b, dp2_b = dpn.astype(BF16), dp1.astype(BF16), dp2.astype(BF16)
            dqlat_n += _dot(dpn_b, wqb_ref[rows_n, :]) + _dot(dp1_b, wqb_ref[rows_1, :]) + _dot(dp2_b, wqb_ref[rows_2, :])
            wq_acc[rows_n, :] += _dot(dpn_b, qlat_b, TN)
            wq_acc[rows_1, :] += _dot(dp1_b, qlat_b, TN)
            wq_acc[rows_2, :] += _dot(dp2_b, qlat_b, TN)
        dxql, gql_rows = _rms_bwd(dqlat_n, xql, gql_ref[...], r_ql)
        lscr[:, LAT_Q:LAT_Q + Q_LORA] = dxql
        o_gql[...] += _colsum(gql_rows)
        o_gqn[...] += d_gqn
        o_gqr[:, 0:ROPE_HALF] += d_g1
        o_gqr[:, ROPE_HALF:ROPE_DIM] += d_g2

        xkv = lat_ref[:, LAT_KV:LAT_KV + KV_LORA]
        kv_n, r_kv = _rms_fwd(xkv, gkvl_ref[...])
        kv_b = kv_n.astype(BF16)
        dkv_n = jnp.zeros((tm, KV_LORA), F32)
        d_gkn = jnp.zeros((1, hd), F32)
        dkr1 = jnp.zeros((tm, ROPE_HALF), F32)
        dkr2 = jnp.zeros((tm, ROPE_HALF), F32)
        for h in range(N_HEADS):
            rows_k = slice(h * 2 * hd, h * 2 * hd + hd)
            rows_v = slice(h * 2 * hd + hd, (h + 1) * 2 * hd)
            pk = _dot(kv_b, wkvb_ref[rows_k, :], NT)
            r_k = lax.rsqrt(jnp.mean(pk * pk, axis=-1, keepdims=True) + RMS_EPS)
            dpk, gk_rows = _rms_bwd(dmk_ref[:, h * lanes:h * lanes + hd], pk, gkn_ref[...], r_k)
            d_gkn += _colsum(gk_rows)
            dkr1 += dmk_ref[:, h * lanes + hd:h * lanes + hd + ROPE_HALF]
            dkr2 += dmk_ref[:, h * lanes + hd + ROPE_HALF:h * lanes + hd + ROPE_DIM]
            dpk_b = dpk.astype(BF16)
            dv_b = dmv_ref[:, h * hd:(h + 1) * hd].astype(BF16)
            dkv_n += _dot(dpk_b, wkvb_ref[rows_k, :]) + _dot(dv_b, wkvb_ref[rows_v, :])
            wkv_acc[rows_k, :] += _dot(dpk_b, kv_b, TN)
            wkv_acc[rows_v, :] += _dot(dv_b, kv_b, TN)
        dxkv, gkvl_rows = _rms_bwd(dkv_n, xkv, gkvl_ref[...], r_kv)
        lscr[:, LAT_KV:LAT_KV + KV_LORA] = dxkv
        o_gkvl[...] += _colsum(gkvl_rows)
        o_gkn[...] += d_gkn

        gk1, gk2 = gkr_ref[:, 0:ROPE_HALF], gkr_ref[:, ROPE_HALF:ROPE_DIM]
        _, _, kn1, kn2, r_kr = _rope_pair_fwd(lat_ref[:, LAT_KR:LAT_KR + ROPE_HALF],
                                              lat_ref[:, LAT_KR + ROPE_HALF:LAT_KR + ROPE_DIM], gk1, gk2, cos_v, sin_v)
        dk1, dk2, gk1_rows, gk2_rows = _rope_pair_bwd(dkr1, dkr2, kn1, kn2, r_kr, gk1, gk2, cos_v, sin_v)
        lscr[:, LAT_KR:LAT_KR + ROPE_HALF] = dk1
        lscr[:, LAT_KR + ROPE_HALF:LAT_KR + ROPE_DIM] = dk2
        o_gkr[:, 0:ROPE_HALF] += _colsum(gk1_rows)
        o_gkr[:, ROPE_HALF:ROPE_DIM] += _colsum(gk2_rows)

        dlat_ref[...] = lscr[...].astype(BF16)

        @pl.when(i == n_t - 1)
        def _():
            dwqb_ref[...] = wq_acc[...].astype(BF16)
            dwkvb_ref[...] = wkv_acc[...].astype(BF16)

    small = [g_qlat, g_kvlat, g_qn, g_qr, g_kn, g_kr, g_fq, g_fk, b_f]

    def rtok(n):
        return pl.BlockSpec((tm, n), lambda i: (n_t - 1 - i, 0))

    return pl.pallas_call(
        body, name="prep_bwd", grid=(n_t,),
        in_specs=[rtok(1536), rtok(LAT_W), rtok(ROPE_HALF), rtok(ROPE_HALF), _full(wqb_t.shape), _full(wkvb_t.shape)]
        + [_full(s.shape) for s in small]
        + [rtok(512), rtok(512), rtok(512), rtok(N_HEADS), rtok(N_HEADS * lanes), rtok(N_HEADS * lanes), rtok(512)],
        out_specs=[rtok(1536), rtok(LAT_W), _full(wqb_t.shape), _full(wkvb_t.shape)] + [_full(s.shape) for s in small],
        out_shape=[_sds((t, 1536), BF16), _sds((t, LAT_W), BF16), _sds(wqb_t.shape, BF16), _sds(wkvb_t.shape, BF16)]
        + [_sds(s.shape, F32) for s in small],
        scratch_shapes=[pltpu.VMEM((1, N_HEADS), F32), pltpu.VMEM((tm, LAT_W), F32), pltpu.VMEM(wqb_t.shape, F32),
                        pltpu.VMEM(wkvb_t.shape, F32)],
        compiler_params=pltpu.CompilerParams(dimension_semantics=("arbitrary",)),
    )(fox, lat, cos, sin, wqb_t, wkvb_t, *small, dfq, dfk, dfv, dc, dmq, dmk, dmv)


def _qk_norms(q, k, *, lanes, name):
    t = q.shape[0]
    tm = min(TOK_TILE, t)

    def body(q_ref, k_ref, qn_ref, kmax_ref):
        i = pl.program_id(0)

        @pl.when(i == 0)
        def _():
            kmax_ref[...] = jnp.zeros_like(kmax_ref)

        for h in range(N_HEADS):
            qh = q_ref[:, h * lanes:(h + 1) * lanes].astype(F32)
            kh = k_ref[:, h * lanes:(h + 1) * lanes].astype(F32)
            qn_ref[:, h:h + 1] = jnp.sum(qh * qh, axis=1, keepdims=True)
            kmax = jnp.max(jnp.sum(kh * kh, axis=1, keepdims=True), axis=0, keepdims=True)
            kmax_ref[:, h:h + 1] = jnp.maximum(kmax_ref[:, h:h + 1], kmax)

    return pl.pallas_call(
        body, name=name, grid=(t // tm,), in_specs=[_tok(tm, N_HEADS * lanes), _tok(tm, N_HEADS * lanes)],
        out_specs=[_tok(tm, N_HEADS), _full((1, N_HEADS))], out_shape=[_sds((t, N_HEADS), F32), _sds((1, N_HEADS), F32)],
        compiler_params=pltpu.CompilerParams(dimension_semantics=("arbitrary",)),
    )(q, k)


def _logit_bound(q, k, *, lanes, scale, name):
    qn, kmax = _qk_norms(q, k, lanes=lanes, name=name)
    bound = jnp.sqrt(qn * kmax) * (scale * (1.0 + 2.0 ** -10)) + 2.0 ** -10
    flag = (jnp.max(bound) <= FIXED_SHIFT_MAX_BOUND).astype(F32).reshape(1, 1)
    return bound, flag


def _attn_fwd(q, k, v, bound, fixed_ok, c, c_t, *, lanes, scale, name, exchange=None):
    t = q.shape[0]
    tq = min(ATT_TILE, t)
    n_q = t // tq
    hd = HEAD_DIM
    ch = min(ATT_COL_CHUNK, tq)
    decay = c is not None

    def body(*refs):
        if decay:
            q_ref, k_ref, v_ref, b_ref, ok_ref, c_ref, ct_ref, o_ref, o32_ref, lse_ref, m_scr, l_scr, acc = refs
        else:
            q_ref, k_ref, v_ref, b_ref, ok_ref, o_ref, o32_ref, lse_ref, m_scr, l_scr, acc = refs
        i, j = pl.program_id(0), pl.program_id(1)
        fixed = ok_ref[0, 0] > 0.5

        @pl.when(j == 0)
        def _():
            m_scr[...] = jnp.full_like(m_scr, MASK_VALUE)
            l_scr[...] = jnp.zeros_like(l_scr)
            acc[...] = jnp.zeros_like(acc)

        def fixed_step(diagonal):
            for h in range(N_HEADS):
                wl = slice(h * lanes, (h + 1) * lanes)
                w = slice(h * hd, (h + 1) * hd)
                qh = q_ref[:, wl]
                row = (c_ref[:, h:h + 1] - b_ref[:, h:h + 1]) if decay else -b_ref[:, h:h + 1]
                l_new = jnp.zeros((tq, 1), F32)
                o_hi = jnp.zeros((tq, hd), F32)
                o_lo = jnp.zeros((tq, hd), F32)
                for cc in range(tq // ch):
                    cols = slice(cc * ch, (cc + 1) * ch)
                    s = _dot(qh, k_ref[cols, wl], NT)
                    if scale != 1.0:
                        s = s * scale
                    s = s + ((row - ct_ref[h:h + 1, cols]) if decay else row)
                    if diagonal:
                        keep = (lax.broadcasted_iota(jnp.int32, (tq, ch), 0)
                                >= lax.broadcasted_iota(jnp.int32, (tq, ch), 1) + cc * ch)
                        s = jnp.where(keep, s, MASK_VALUE)
                    p = jnp.exp(s)
                    l_new = l_new + jnp.sum(p, axis=1, keepdims=True)
                    p_b = p.astype(BF16)
                    o_hi = o_hi + _dot(p_b, v_ref[cols, w])
                    if decay:
                        o_lo = o_lo + _dot((p - p_b.astype(F32)).astype(BF16), v_ref[cols, w])
                l_scr[h] += l_new
                acc[0, :, w] += o_hi
                if decay:
                    acc[1, :, w] += o_lo

        def step(diagonal):
            if diagonal:
                keep = lax.broadcasted_iota(jnp.int32, (tq, tq), 0) >= lax.broadcasted_iota(jnp.int32, (tq, tq), 1)
            for h in range(N_HEADS):
                s = _dot(q_ref[:, h * lanes:(h + 1) * lanes], k_ref[:, h * lanes:(h + 1) * lanes], NT)
                if scale != 1.0:
                    s = s * scale
                if decay:
                    s = s + (c_ref[:, h:h + 1] - ct_ref[h:h + 1, :])
                if diagonal:
                    s = jnp.where(keep, s, MASK_VALUE)
                m_prev = m_scr[h]
                m_new = jnp.maximum(m_prev, jnp.max(s, axis=1, keepdims=True))
                alpha = jnp.exp(m_prev - m_new)
                p = jnp.exp(s - m_new)
                l_scr[h] = alpha * l_scr[h] + jnp.sum(p, axis=1, keepdims=True)
                w = slice(h * hd, (h + 1) * hd)
                p_b = p.astype(BF16)
                acc[0, :, w] = alpha * acc[0, :, w] + _dot(p_b, v_ref[:, w])
                if decay:
                    p_lo = (p - p_b.astype(F32)).astype(BF16)
                    acc[1, :, w] = alpha * acc[1, :, w] + _dot(p_lo, v_ref[:, w])
                m_scr[h] = m_new

        for diagonal, here in ((False, j < i), (True, j == i)):
            @pl.when(here & fixed)
            def _():
                fixed_step(diagonal)

            @pl.when(here & jnp.logical_not(fixed))
            def _():
                step(diagonal)

        @pl.when(j == i)
        def _():
            for h in range(N_HEADS):
                w = slice(h * hd, (h + 1) * hd)
                l = l_scr[h]
                o_ref[:, w] = (acc[0, :, w] / l).astype(BF16)
                o32_ref[:, w] = ((acc[0, :, w] + acc[1, :, w]) if decay else acc[0, :, w]) / l
                lse_ref[:, h:h + 1] = jnp.where(fixed, b_ref[:, h:h + 1], m_scr[h]) + jnp.log(l)

    qspec = lambda n: pl.BlockSpec((tq, n), lambda i, j: (i, 0))
    kspec = lambda n: pl.BlockSpec((tq, n), lambda i, j: (jnp.minimum(i, j), 0))
    in_specs = [qspec(N_HEADS * lanes), kspec(N_HEADS * lanes), kspec(512), qspec(N_HEADS),
                pl.BlockSpec(memory_space=pltpu.SMEM)]
    args = [q, k, v, bound, fixed_ok]
    if decay:
        in_specs += [qspec(N_HEADS), pl.BlockSpec((N_HEADS, tq), lambda i, j: (0, jnp.minimum(i, j)))]
        args += [c, c_t]
    return _gridded_call(
        body, name=name, grid=(n_q, n_q), in_specs=in_specs, out_specs=[qspec(512), qspec(512), qspec(N_HEADS)],
        out_shape=[_sds((t, 512), BF16), _sds((t, 512), F32), _sds((t, N_HEADS), F32)],
        scratch_shapes=[pltpu.VMEM((N_HEADS, tq, 1), F32), pltpu.VMEM((N_HEADS, tq, 1), F32), pltpu.VMEM((2, tq, 512), F32)],
        args=args, exchange=exchange)


def _attn_bwd(q, k, v, do, delta, lse, c, c_t, *, lanes, scale, name, exchange=None):
    t = q.shape[0]
    tq = min(ATT_TILE, t)
    n_q = t // tq
    hd = HEAD_DIM
    decay = c is not None

    def body(*refs):
        if decay:
            q_ref, k_ref, v_ref, do_ref, delta_ref, lse_ref, c_ref, ct_ref, dq_hbm, dk_ref, dv_ref, dct_ref, dq_ref = refs
        else:
            q_ref, k_ref, v_ref, do_ref, delta_ref, lse_ref, dq_hbm, dk_ref, dv_ref, dq_ref = refs
        j, i = pl.program_id(0), pl.program_id(1)

        @pl.when((j == 0) & (i == 0))
        def _():
            dq_ref[...] = jnp.zeros_like(dq_ref)

        @pl.when(i == j)
        def _():
            dk_ref[...] = jnp.zeros_like(dk_ref)
            dv_ref[...] = jnp.zeros_like(dv_ref)
            if decay:
                dct_ref[...] = jnp.zeros_like(dct_ref)

        def step(diagonal):
            if diagonal:
                keep = lax.broadcasted_iota(jnp.int32, (tq, tq), 0) >= lax.broadcasted_iota(jnp.int32, (tq, tq), 1)
            rows = pl.ds(pl.multiple_of(i * tq, tq), tq)
            for h in range(N_HEADS):
                wl = slice(h * lanes, (h + 1) * lanes)
                w = slice(h * hd, (h + 1) * hd)
                qh, kh = q_ref[:, wl], k_ref[:, wl]
                s = _dot(qh, kh, NT)
                if scale != 1.0:
                    s = s * scale
                if decay:
                    s = s + (c_ref[:, h:h + 1] - ct_ref[h:h + 1, :])
                if diagonal:
                    s = jnp.where(keep, s, MASK_VALUE)
                p = jnp.exp(s - lse_ref[:, h:h + 1])
                doh = do_ref[:, w]
                dv_ref[:, w] += _dot(p.astype(BF16), doh, TN)
                dp = _dot(doh, v_ref[:, w], NT)
                ds = p * (dp - delta_ref[:, h:h + 1])
                if decay:
                    dct_ref[h:h + 1, :] -= _colsum(ds)
                if scale != 1.0:
                    ds = ds * scale
                ds_b = ds.astype(BF16)
                dk_ref[:, wl] += _dot(ds_b, qh, TN)
                dq_ref[rows, wl] += _dot(ds_b, kh)

        @pl.when(i > j)
        def _():
            step(False)

        @pl.when(i == j)
        def _():
            step(True)

        @pl.when((j == n_q - 1) & (i == n_q - 1))
        def _():
            pltpu.sync_copy(dq_ref, dq_hbm)

    qspec = lambda n: pl.BlockSpec((tq, n), lambda j, i: (jnp.maximum(i, j), 0))
    kspec = lambda n: pl.BlockSpec((tq, n), lambda j, i: (j, 0))
    in_specs = [qspec(N_HEADS * lanes), kspec(N_HEADS * lanes), kspec(512), qspec(512), qspec(N_HEADS), qspec(N_HEADS)]
    out_specs = [pl.BlockSpec(memory_space=pl.ANY), kspec(N_HEADS * lanes), kspec(512)]
    out_shape = [_sds((t, N_HEADS * lanes), F32), _sds((t, N_HEADS * lanes), F32), _sds((t, 512), F32)]
    args = [q, k, v, do, delta, lse]
    if decay:
        ctspec = pl.BlockSpec((N_HEADS, tq), lambda j, i: (0, j))
        in_specs += [qspec(N_HEADS), ctspec]
        out_specs.append(ctspec)
        out_shape.append(_sds((N_HEADS, t), F32))
        args += [c, c_t]
    return _gridded_call(body, name=name, grid=(n_q, n_q), in_specs=in_specs, out_specs=out_specs, out_shape=out_shape,
                         scratch_shapes=[pltpu.VMEM((t, N_HEADS * lanes), F32)], args=args, exchange=exchange)


def _mix_fwd(x, y_mla, y_fox, gates, b_gate, wbm_t, wbf_t, wo):
    t = x.shape[0]
    tm = min(PROJ_TILE, t)

    def body(x_ref, ym_ref, yf_ref, gt_ref, bg_ref, wbm_ref, wbf_ref, wo_ref, out_ref):
        um = _dot(ym_ref[...], wbm_ref[...], NT)
        uf = _dot(yf_ref[...], wbf_ref[...], NT)
        sm = jax.nn.sigmoid(gt_ref[:, 0:D_MODEL] + bg_ref[0:1, :])
        sf = jax.nn.sigmoid(gt_ref[:, D_MODEL:2 * D_MODEL] + bg_ref[1:2, :])
        mixed = sm * um + sf * uf
        out_ref[...] = x_ref[...] + _dot(mixed.astype(BF16), wo_ref[...])

    tok = _tok(tm, D_MODEL)
    return pl.pallas_call(
        body, name="mix_fwd", grid=(t // tm,),
        in_specs=[tok, _tok(tm, 512), _tok(tm, 512), _tok(tm, 2 * D_MODEL), _full((2, D_MODEL)), _full(wbm_t.shape),
                  _full(wbf_t.shape), _full(wo.shape)],
        out_specs=tok, out_shape=_sds((t, D_MODEL), F32),
        compiler_params=pltpu.CompilerParams(dimension_semantics=("arbitrary",)),
    )(x, y_mla, y_fox, gates, b_gate, wbm_t, wbf_t, wo)


def _mix_bwd(dx, y_mla, y_fox, y_mla32, y_fox32, gates, b_gate, wbm_t, wbf_t, wo):
    t = dx.shape[0]
    tm = min(PROJ_TILE, t)

    def body(dx_ref, ym_ref, yf_ref, ym32_ref, yf32_ref, gt_ref, bg_ref, wbm_ref, wbf_ref, wo_ref,
             dym_ref, dyf_ref, dlm_ref, dlf_ref, dgt_ref, mixed_ref, dum_ref, duf_ref, dxb_ref, dbg_ref, prod):
        i = pl.program_id(0)
        dxb = dx_ref[...].astype(BF16)
        dxb_ref[...] = dxb
        dmixed = _dot(dxb, wo_ref[...], NT)
        um = _dot(ym_ref[...], wbm_ref[...], NT)
        uf = _dot(yf_ref[...], wbf_ref[...], NT)
        sm = jax.nn.sigmoid(gt_ref[:, 0:D_MODEL] + bg_ref[0:1, :])
        sf = jax.nn.sigmoid(gt_ref[:, D_MODEL:2 * D_MODEL] + bg_ref[1:2, :])
        mixed_ref[...] = (sm * um + sf * uf).astype(BF16)
        dum = (dmixed * sm).astype(BF16)
        duf = (dmixed * sf).astype(BF16)
        dum_ref[...] = dum
        duf_ref[...] = duf
        dgm = dmixed * um * (sm * (1.0 - sm))
        dgf = dmixed * uf * (sf * (1.0 - sf))
        dgt_ref[:, 0:D_MODEL] = dgm.astype(BF16)
        dgt_ref[:, D_MODEL:2 * D_MODEL] = dgf.astype(BF16)
        for du, wb_ref, y32_ref, dy_ref, dl_ref in ((dum, wbm_ref, ym32_ref, dym_ref, dlm_ref),
                                                    (duf, wbf_ref, yf32_ref, dyf_ref, dlf_ref)):
            dy = _dot(du, wb_ref[...])
            dy_ref[...] = dy.astype(BF16)
            prod[...] = dy.astype(BF16).astype(F32) * y32_ref[...]
            for h in range(N_HEADS):
                dl_ref[:, h:h + 1] = jnp.sum(prod[:, h * HEAD_DIM:(h + 1) * HEAD_DIM], axis=1, keepdims=True)

        @pl.when(i == 0)
        def _():
            dbg_ref[...] = jnp.zeros_like(dbg_ref)

        dbg_ref[0:1, :] += _colsum(dgm)
        dbg_ref[1:2, :] += _colsum(dgf)

    tok = _tok(tm, D_MODEL)
    tokb = _sds((t, D_MODEL), BF16)
    t512, t8 = _tok(tm, 512), _tok(tm, N_HEADS)
    return pl.pallas_call(
        body, name="mix_bwd", grid=(t // tm,),
        in_specs=[tok, t512, t512, t512, t512, _tok(tm, 2 * D_MODEL), _full((2, D_MODEL)), _full(wbm_t.shape),
                  _full(wbf_t.shape), _full(wo.shape)],
        out_specs=[t512, t512, t8, t8, _tok(tm, 2 * D_MODEL), tok, tok, tok, tok, _full((2, D_MODEL))],
        out_shape=[_sds((t, 512), BF16), _sds((t, 512), BF16), _sds((t, N_HEADS), F32), _sds((t, N_HEADS), F32),
                   _sds((t, 2 * D_MODEL), BF16), tokb, tokb, tokb, tokb, _sds((2, D_MODEL), F32)],
        scratch_shapes=[pltpu.VMEM((tm, 512), F32)],
        compiler_params=pltpu.CompilerParams(dimension_semantics=("arbitrary",)),
    )(dx, y_mla, y_fox, y_mla32, y_fox32, gates, b_gate, wbm_t, wbf_t, wo)


def _my_position():
    x, y, c = lax.axis_index("x"), lax.axis_index("y"), lax.axis_index("c")
    return x, y, c, 4 * x + 2 * y + c


def _peer(x, y, c, mask):
    px = 1 - x if mask & 4 else x
    py = 1 - y if mask & 2 else y
    pc = 1 - c if mask & 1 else c
    return (px, py, pc), 4 * px + 2 * py + pc


def _chip_peer(x, y, km):
    px = 1 - x if km & 2 else x
    py = 1 - y if km & 1 else y
    return px, py, 2 * px + py


_HBM = pl.BlockSpec(memory_space=pl.ANY)


def _wait_all(copies):
    for cp in copies:
        cp.wait()


class _ChipExchange:
    def __init__(self, gather, arrays):
        self.gather, self.arrays = gather, list(arrays)
        n = len(self.arrays)
        self.out_shape = [_sds((N_DEV * a.shape[0],) + a.shape[1:], a.dtype) if gather else _sds(a.shape, a.dtype)
                          for a in self.arrays]
        self.scratch_shapes = [pltpu.SemaphoreType.DMA((n, N_CHIP)), pltpu.SemaphoreType.DMA((n, N_CHIP)),
                               pltpu.SemaphoreType.DMA((n,))]

    def copies(self, srcs, dsts, send_sems, recv_sems, local_sems):
        x, y, c, me = _my_position()
        q_me = 2 * x + y
        out = []
        for a in range(len(self.arrays)):
            if self.gather:
                r = srcs[a].shape[0]
                local_src, dst = srcs[a], dsts[a].at[pl.ds(me * r, r)]
            else:
                local_src, dst = srcs[a].at[q_me], dsts[a].at[q_me]
            out.append(pltpu.make_async_copy(local_src, dst, local_sems.at[a]))
            for km in range(1, N_CHIP):
                px, py, q_peer = _chip_peer(x, y, km)
                out.append(pltpu.make_async_remote_copy(
                    src_ref=srcs[a] if self.gather else srcs[a].at[q_peer], dst_ref=dst, send_sem=send_sems.at[a, km],
                    recv_sem=recv_sems.at[a, km], device_id=(px, py, c), device_id_type=MESH))
        return out

    def standalone(self, name):
        n = len(self.arrays)

        def body(*refs):
            copies = self.copies(refs[:n], refs[n:2 * n], *refs[2 * n:])
            for cp in copies:
                cp.start()
            _wait_all(copies)

        return pl.pallas_call(body, name=name, in_specs=[_HBM] * n, out_specs=[_HBM] * n, out_shape=self.out_shape,
                              scratch_shapes=self.scratch_shapes)(*self.arrays)


def _gridded_call(body, *, name, grid, in_specs, out_specs, out_shape, scratch_shapes, args, exchange=None):
    params = pltpu.CompilerParams(dimension_semantics=("arbitrary",) * len(grid))
    if exchange is None:
        return pl.pallas_call(body, name=name, grid=grid, in_specs=in_specs, out_specs=out_specs, out_shape=out_shape,
                              scratch_shapes=scratch_shapes, compiler_params=params)(*args), None
    n_in, n_out, n_scr, n_x = len(in_specs), len(out_specs), len(scratch_shapes), len(exchange.arrays)

    def carrier(*refs):
        ins, x_src, refs = refs[:n_in], refs[n_in:n_in + n_x], refs[n_in + n_x:]
        outs, x_dst, refs = refs[:n_out], refs[n_out:n_out + n_x], refs[n_out + n_x:]
        copies = exchange.copies(x_src, x_dst, *refs[n_scr:])
        pids = [pl.program_id(d) for d in range(len(grid))]
        first = functools.reduce(jnp.logical_and, [p == 0 for p in pids])
        last = functools.reduce(jnp.logical_and, [p == g - 1 for p, g in zip(pids, grid)])

        @pl.when(first)
        def _():
            for cp in copies:
                cp.start()

        body(*ins, *outs, *refs[:n_scr])

        @pl.when(last)
        def _():
            _wait_all(copies)

    res = pl.pallas_call(
        carrier, name=name, grid=grid, in_specs=list(in_specs) + [_HBM] * n_x, out_specs=list(out_specs) + [_HBM] * n_x,
        out_shape=list(out_shape) + exchange.out_shape, scratch_shapes=list(scratch_shapes) + exchange.scratch_shapes,
        compiler_params=params)(*args, *exchange.arrays)
    return res[:n_out], res[n_out:]


def _gather_over_cores(arrays, name):
    n_arr = len(arrays)

    def body(*refs):
        srcs, dsts = refs[:n_arr], refs[n_arr:2 * n_arr]
        send_sems, recv_sems = refs[2 * n_arr:]
        x, y, c, _ = _my_position()
        copies = []
        for a in range(n_arr):
            r = srcs[a].shape[0] // N_DEV
            for q in range(N_CHIP):
                rows = pl.ds((2 * q + c) * r, r)
                copies.append(pltpu.make_async_remote_copy(
                    src_ref=srcs[a].at[rows], dst_ref=dsts[a].at[rows], send_sem=send_sems.at[a, q],
                    recv_sem=recv_sems.at[a, q], device_id=(x, y, 1 - c), device_id_type=MESH))
        for cp in copies:
            cp.start()
        _wait_all(copies)

    return pl.pallas_call(
        body, name=name, in_specs=[_HBM] * n_arr, out_specs=[_HBM] * n_arr,
        out_shape=[_sds(a.shape, a.dtype) for a in arrays], input_output_aliases={a: a for a in range(n_arr)},
        scratch_shapes=[pltpu.SemaphoreType.DMA((n_arr, N_CHIP)), pltpu.SemaphoreType.DMA((n_arr, N_CHIP))],
    )(*arrays)


def _grads_to_sibling(grads, name):
    n_arr = len(grads)

    def body(*refs):
        srcs, dsts = refs[:n_arr], refs[n_arr:2 * n_arr]
        send_sems, recv_sems = refs[2 * n_arr:]
        x, y, c, _ = _my_position()
        copies = []
        for a in range(n_arr):
            r = srcs[a].shape[0] // N_DEV
            for q in range(N_CHIP):
                copies.append(pltpu.make_async_remote_copy(
                    src_ref=srcs[a].at[pl.ds((2 * q + 1 - c) * r, r)], dst_ref=dsts[a].at[q], send_sem=send_sems.at[a, q],
                    recv_sem=recv_sems.at[a, q], device_id=(x, y, 1 - c), device_id_type=MESH))
        for cp in copies:
            cp.start()
        _wait_all(copies)

    return pl.pallas_call(
        body, name=name, in_specs=[_HBM] * n_arr, out_specs=[_HBM] * n_arr,
        out_shape=[_sds((N_CHIP, g.shape[0] // N_DEV) + g.shape[1:], g.dtype) for g in grads],
        scratch_shapes=[pltpu.SemaphoreType.DMA((n_arr, N_CHIP)), pltpu.SemaphoreType.DMA((n_arr, N_CHIP))],
    )(*grads)


def _pair_sum(grad, from_sibling, name):
    r, n = from_sibling.shape[1:]

    def body(g_ref, s_ref, o_ref):
        c = lax.axis_index("c")
        o_ref[...] = (g_ref[c].astype(F32) + s_ref[...].astype(F32)).astype(BF16)

    return pl.pallas_call(
        body, name=name, grid=(N_CHIP,),
        in_specs=[pl.BlockSpec((None, 2, r, n), lambda q: (q, 0, 0, 0)), pl.BlockSpec((None, r, n), lambda q: (q, 0, 0))],
        out_specs=pl.BlockSpec((None, r, n), lambda q: (q, 0, 0)), out_shape=_sds((N_CHIP, r, n), BF16),
    )(grad.reshape(N_CHIP, 2, r, n), from_sibling)


def _all_reduce_small(vec):
    r = vec.shape[0]

    def body(v_ref, o_ref, buf, send_sems, recv_sems):
        x, y, c, me = _my_position()
        buf[me] = v_ref[...]
        copies = []
        for mask in range(1, N_DEV):
            peer, _ = _peer(x, y, c, mask)
            cp = pltpu.make_async_remote_copy(src_ref=v_ref, dst_ref=buf.at[me], send_sem=send_sems.at[mask],
                                              recv_sem=recv_sems.at[mask], device_id=peer, device_id_type=MESH)
            cp.start()
            copies.append(cp)
        for cp in copies:
            cp.wait()
        total = buf[0]
        for s in range(1, N_DEV):
            total = total + buf[s]
        o_ref[...] = total

    vm = pl.BlockSpec(memory_space=pltpu.VMEM)
    return pl.pallas_call(
        body, name="all_reduce_small", in_specs=[vm], out_specs=vm, out_shape=_sds(vec.shape, F32),
        scratch_shapes=[pltpu.VMEM((N_DEV, r, 128), F32), pltpu.SemaphoreType.DMA((N_DEV,)), pltpu.SemaphoreType.DMA((N_DEV,))],
    )(vec)


def _adamw_math(w, g, m, v):
    m = ADAM_B1 * m + (1.0 - ADAM_B1) * g
    v = ADAM_B2 * v + (1.0 - ADAM_B2) * (g * g)
    m_hat = m / (1.0 - ADAM_B1 ** ADAM_STEP)
    v_hat = v / (1.0 - ADAM_B2 ** ADAM_STEP)
    delta = -ADAM_LR * (m_hat / (jnp.sqrt(v_hat) + ADAM_EPS) + ADAM_WD * w)
    return delta, m, v


def _reduce_adamw(slots, w, m, v, *, transpose, name):
    r, n = slots.shape[1:]

    def body(s_ref, w_ref, m_ref, v_ref, g_ref, d_ref, nm_ref, nv_ref, *scr):
        g = s_ref[0].astype(F32)
        for s in range(1, slots.shape[0]):
            g = g + s_ref[s].astype(F32)
        if transpose:
            scr[0][...] = g.T
            g = scr[0][:, 0:w_ref.shape[1]]
        g_ref[...] = g
        d_ref[...], nm_ref[...], nv_ref[...] = _adamw_math(w_ref[...], g, m_ref[...], v_ref[...])

    out = _sds(w.shape, F32)
    return pl.pallas_call(
        body, name=name, out_shape=[out, out, out, out],
        scratch_shapes=[pltpu.VMEM((n, r), F32)] if transpose else [],
    )(slots, w, m, v)


def _adamw(g, w, m, v, name):
    def body(g_ref, w_ref, m_ref, v_ref, d_ref, nm_ref, nv_ref):
        d_ref[...], nm_ref[...], nv_ref[...] = _adamw_math(w_ref[...], g_ref[...], m_ref[...], v_ref[...])

    out = _sds(w.shape, F32)
    return pl.pallas_call(body, name=name, out_shape=[out, out, out])(g, w, m, v)


_SMALL = ["ffn1_norm", "mix_norm", "ffn2_norm", "mla_q_lat_norm", "mla_kv_lat_norm", "mla_q_nope_gain", "mla_q_rope_gain",
          "mla_k_nope_gain", "mla_k_rope_gain", "fox_q_gain", "fox_k_gain", "fox_b_f"]
_WEIGHTS = ["ffn1_norm", "ffn1_w_gate", "ffn1_w_up", "ffn1_w_down", "mix_norm", "w_in", "mla_q_lat_norm", "mla_w_qb",
            "mla_kv_lat_norm", "mla_w_kvb", "mla_q_nope_gain", "mla_q_rope_gain", "mla_k_nope_gain", "mla_k_rope_gain",
            "fox_q_gain", "fox_k_gain", "fox_b_f", "w_branch_mla", "w_branch_fox", "b_gate", "w_o", "ffn2_norm",
            "ffn2_w_gate", "ffn2_w_up", "ffn2_w_down"]
_IN_Q, _IN_KV, _IN_KR, _IN_FOX, _IN_F, _IN_GATES = (0, 192), (192, 128), (320, 32), (352, 1536), (1888, 8), (1896, 2048)


def _rows(a, seg):
    return a[seg[0]:seg[0] + seg[1]]


def _split_w_in(win_t):
    z = lambda n: jnp.zeros((n, D_MODEL), win_t.dtype)
    lat = jnp.concatenate([_rows(win_t, _IN_Q), z(LAT_KV - Q_LORA), _rows(win_t, _IN_KV), _rows(win_t, _IN_KR),
                           _rows(win_t, _IN_F), z(LAT_W - LAT_F - N_HEADS)], axis=0)
    return _rows(win_t, _IN_GATES), _rows(win_t, _IN_FOX), lat


def _join_w_in(d_gates, d_fox, d_lat):
    return jnp.concatenate([d_lat[LAT_Q:LAT_Q + Q_LORA], d_lat[LAT_KV:LAT_KV + KV_LORA], d_lat[LAT_KR:LAT_KR + ROPE_DIM],
                            d_fox, d_lat[LAT_F:LAT_F + N_HEADS], d_gates], axis=0)


def kernel(x, positions, ffn1_norm, ffn1_w_gate, ffn1_w_up, ffn1_w_down, mix_norm, w_in, mla_q_lat_norm, mla_w_qb, mla_kv_lat_norm, mla_w_kvb, mla_q_nope_gain, mla_q_rope_gain, mla_k_nope_gain, mla_k_rope_gain, fox_q_gain, fox_k_gain, fox_b_f, w_branch_mla, w_branch_fox, b_gate, w_o, ffn2_norm, ffn2_w_gate, ffn2_w_up, ffn2_w_down, loss_target, m_ffn1_norm, m_ffn1_w_gate, m_ffn1_w_up, m_ffn1_w_down, m_mix_norm, m_w_in, m_mla_q_lat_norm, m_mla_w_qb, m_mla_kv_lat_norm, m_mla_w_kvb, m_mla_q_nope_gain, m_mla_q_rope_gain, m_mla_k_nope_gain, m_mla_k_rope_gain, m_fox_q_gain, m_fox_k_gain, m_fox_b_f, m_w_branch_mla, m_w_branch_fox, m_b_gate, m_w_o, m_ffn2_norm, m_ffn2_w_gate, m_ffn2_w_up, m_ffn2_w_down, v_ffn1_norm, v_ffn1_w_gate, v_ffn1_w_up, v_ffn1_w_down, v_mix_norm, v_w_in, v_mla_q_lat_norm, v_mla_w_qb, v_mla_kv_lat_norm, v_mla_w_kvb, v_mla_q_nope_gain, v_mla_q_rope_gain, v_mla_k_nope_gain, v_mla_k_rope_gain, v_fox_q_gain, v_fox_k_gain, v_fox_b_f, v_w_branch_mla, v_w_branch_fox, v_b_gate, v_w_o, v_ffn2_norm, v_ffn2_w_gate, v_ffn2_w_up, v_ffn2_w_down):
    env = dict(locals())
    strip = lambda n, a: a if n in _SMALL else a[0]
    W = {n: strip(n, env[n]) for n in _WEIGHTS}
    M = {n: strip(n, env["m_" + n]) for n in _WEIGHTS}
    V = {n: strip(n, env["v_" + n]) for n in _WEIGHTS}
    xs = x[0]
    t = xs.shape[0]

    col_split = ["ffn1_w_gate", "ffn1_w_up", "ffn2_w_gate", "ffn2_w_up", "mla_w_qb", "mla_w_kvb", "w_branch_mla", "w_branch_fox"]
    row_split = ["ffn1_w_down", "ffn2_w_down", "w_o"]
    pieces = {n: W[n].T.astype(BF16) for n in col_split}
    pieces.update({n: W[n].astype(BF16) for n in row_split})
    pieces["w_in"] = jnp.pad(W["w_in"].T.astype(BF16), ((0, W_IN_PIECE_PAD - W_IN_PIECE), (0, 0)))
    pieces["b_gate"] = W["b_gate"].T
    group_a = ["ffn1_w_gate", "ffn1_w_up", "ffn1_w_down"]
    group_b = ["w_in", "mla_w_qb", "mla_w_kvb", "w_branch_mla", "w_branch_fox", "w_o", "b_gate"]
    group_c = ["ffn2_w_gate", "ffn2_w_up", "ffn2_w_down"]
    gather = lambda group: _ChipExchange(True, [pieces[n] for n in group])
    G = dict(zip(group_a, _gather_over_cores(gather(group_a).standalone("gather_ici_a"), "gather_d2d_a")))

    inv_freq = ROPE_THETA ** (-jnp.arange(ROPE_HALF, dtype=F32) / ROPE_HALF)
    ang = positions[0].astype(F32)[:, None] * inv_freq
    cos, sin = jnp.cos(ang), jnp.sin(ang)
    prep_small = [W[n] for n in ["mla_q_lat_norm", "mla_kv_lat_norm", "mla_q_nope_gain", "mla_q_rope_gain", "mla_k_nope_gain",
                                 "mla_k_rope_gain", "fox_q_gain", "fox_k_gain", "fox_b_f"]]

    (x1, a1, b1), got_b = _ffn_fwd(xs, W["ffn1_norm"], G["ffn1_w_gate"], G["ffn1_w_up"], G["ffn1_w_down"],
                                   exchange=gather(group_b))
    G.update(zip(group_b, _gather_over_cores(got_b, "gather_d2d_b")))
    win_t = G["w_in"].reshape(N_DEV, W_IN_PIECE_PAD, D_MODEL)[:, :W_IN_PIECE].reshape(N_DEV * W_IN_PIECE, D_MODEL)
    wgate_t, wfox_t, wlat_t = _split_w_in(win_t)
    bg = G["b_gate"].T
    gates, fox, lat = _proj_fwd(x1, W["mix_norm"], wgate_t, wfox_t, wlat_t)
    fq, fk, fv, c, mq, mk, mv = _prep_fwd(fox, lat, cos, sin, G["mla_w_qb"], G["mla_w_kvb"], *prep_small)
    c_t = c.T
    b_fox, ok_fox = _logit_bound(fq, fk, lanes=HEAD_DIM, scale=1.0, name="fox_norms")
    b_mla, ok_mla = _logit_bound(mq, mk, lanes=MLA_QK_LANES, scale=MLA_SCALE, name="mla_norms")
    (y_fox, y_fox32, lse_fox), got_c = _attn_fwd(fq, fk, fv, b_fox, ok_fox, c, c_t, lanes=HEAD_DIM, scale=1.0, name="fox_fwd",
                                                 exchange=gather(group_c))
    G.update(zip(group_c, _gather_over_cores(got_c, "gather_d2d_c")))
    (y_mla, y_mla32, lse_mla), _ = _attn_fwd(mq, mk, mv, b_mla, ok_mla, None, None, lanes=MLA_QK_LANES, scale=MLA_SCALE,
                                             name="mla_fwd")
    x2 = _mix_fwd(x1, y_mla, y_fox, gates, bg, G["w_branch_mla"], G["w_branch_fox"], G["w_o"])
    (dx3, a2, b2, loss_vec), _ = _ffn_fwd(x2, W["ffn2_norm"], G["ffn2_w_gate"], G["ffn2_w_up"], G["ffn2_w_down"],
                                          target=loss_target[0])

    def chip_sums(group, tag):
        from_sibling = _grads_to_sibling([grads[n] for n in group], "grads_d2d_" + tag)
        return _ChipExchange(False, [_pair_sum(grads[n], s, "pair_sum_" + n) for n, s in zip(group, from_sibling)])

    (dx2, dg_ffn2, da2, db2, h2, n2, dyh2), _ = _ffn_bwd(dx3, x2, W["ffn2_norm"], a2, b2, G["ffn2_w_gate"], G["ffn2_w_up"],
                                                        G["ffn2_w_down"], "ffn2_bwd")
    grads = {"ffn2_w_gate": _tn_matmul(da2, n2, "ffn2_dgate"), "ffn2_w_up": _tn_matmul(db2, n2, "ffn2_dup"),
             "ffn2_w_down": _tn_matmul(h2, dyh2, "ffn2_ddown")}
    dy_mla, dy_fox, delta_mla, delta_fox, dgates, mixed, dum, duf, dx2b, dbg = _mix_bwd(
        dx2, y_mla, y_fox, y_mla32, y_fox32, gates, bg, G["w_branch_mla"], G["w_branch_fox"], G["w_o"])
    grads["w_o"] = _tn_matmul(mixed, dx2b, "d_w_o")
    grads["w_branch_mla"] = _tn_matmul(dum, y_mla, "d_w_branch_mla")
    grads["w_branch_fox"] = _tn_matmul(duf, y_fox, "d_w_branch_fox")
    (dfq, dfk, dfv, dc_t), slots_c = _attn_bwd(fq, fk, fv, dy_fox, delta_fox, lse_fox, c, c_t, lanes=HEAD_DIM, scale=1.0,
                                               name="fox_bwd", exchange=chip_sums(group_c, "c"))
    slots = dict(zip(group_c, slots_c))
    (dmq, dmk, dmv), _ = _attn_bwd(mq, mk, mv, dy_mla, delta_mla, lse_mla, None, None, lanes=MLA_QK_LANES, scale=MLA_SCALE,
                                   name="mla_bwd")
    prep_out = _prep_bwd(fox, lat, cos, sin, G["mla_w_qb"], G["mla_w_kvb"], *prep_small, dfq, dfk, dfv, dc_t.T, dmq, dmk, dmv)
    dfox, dlat, grads["mla_w_qb"], grads["mla_w_kvb"] = prep_out[:4]
    d_prep_small = prep_out[4:]
    dx1, dg_mix, nmix = _proj_bwd(dgates, dfox, dlat, wgate_t, wfox_t, wlat_t, x1, W["mix_norm"], dx2)
    dwin_t = _join_w_in(_tn_matmul(dgates, nmix, "d_w_in_gates"), _tn_matmul(dfox, nmix, "d_w_in_fox"),
                        _tn_matmul(dlat, nmix, "d_w_in_lat"))
    grads["w_in"] = jnp.pad(dwin_t.reshape(N_DEV, W_IN_PIECE, D_MODEL), ((0, 0), (0, W_IN_PIECE_PAD - W_IN_PIECE), (0, 0))
                            ).reshape(N_DEV * W_IN_PIECE_PAD, D_MODEL)
    grad_group_b = [n for n in group_b if n != "b_gate"]
    (dx0, dg_ffn1, da1, db1, h1, n1, dyh1), slots_b = _ffn_bwd(dx1, xs, W["ffn1_norm"], a1, b1, G["ffn1_w_gate"], G["ffn1_w_up"],
                                                              G["ffn1_w_down"], "ffn1_bwd", exchange=chip_sums(grad_group_b, "b"))
    slots.update(zip(grad_group_b, slots_b))
    grads["ffn1_w_gate"] = _tn_matmul(da1, n1, "ffn1_dgate")
    grads["ffn1_w_up"] = _tn_matmul(db1, n1, "ffn1_dup")
    grads["ffn1_w_down"] = _tn_matmul(h1, dyh1, "ffn1_ddown")
    slots.update(zip(group_a, chip_sums(group_a, "a").standalone("grads_ici_a")))

    small_parts = [dg_ffn1, dg_mix, dg_ffn2] + list(d_prep_small) + [dbg.reshape(1, 2 * D_MODEL), loss_vec]
    flat = jnp.concatenate([p.reshape(-1) for p in small_parts])
    n_flat = flat.shape[0]
    rows = -(-n_flat // (8 * 128)) * 8
    total = _all_reduce_small(jnp.pad(flat, (0, rows * 128 - n_flat)).reshape(rows, 128)).reshape(-1)
    offs, small_g = 0, {}
    for n in _SMALL:
        small_g[n] = total[offs:offs + W[n].shape[1]].reshape(W[n].shape)
        offs += W[n].shape[1]
    bg_full = total[offs:offs + 2 * D_MODEL].reshape(2, D_MODEL)
    offs += 2 * D_MODEL
    loss = (0.5 / D_MODEL) * jnp.sum(total[offs:offs + D_MODEL])
    _, _, _, me = _my_position()
    small_g["b_gate"] = lax.dynamic_slice_in_dim(bg_full, me * (D_MODEL // N_DEV), D_MODEL // N_DEV, axis=1)

    res = {}
    for n in _WEIGHTS:
        if n in small_g:
            res[n] = (small_g[n],) + tuple(_adamw(small_g[n], W[n], M[n], V[n], "adamw_" + n))
        else:
            res[n] = tuple(_reduce_adamw(slots[n], W[n], M[n], V[n], transpose=n in col_split or n == "w_in", name="adamw_" + n))
    outs = [loss, dx0[None]]
    for k in range(4):
        outs += [res[n][k] if n in _SMALL else res[n][k][None] for n in _WEIGHTS]
    return tuple(outs)
```

```python
import functools

import jax
import jax.numpy as jnp
from jax import lax
from jax.experimental import pallas as pl
from jax.experimental.pallas import tpu as pltpu

F32 = jnp.float32
BF16 = jnp.bfloat16

D_MODEL = 1024
FFN_HIDDEN = 2816
N_HEADS = 8
HEAD_DIM = 64
ROPE_DIM = 32
ROPE_HALF = 16
Q_LORA = 192
KV_LORA = 128
ROPE_THETA = 10000.0
RMS_EPS = 1e-6
MLA_SCALE = (HEAD_DIM + ROPE_DIM) ** -0.5
FOX_SCALE = HEAD_DIM ** -0.5
MLA_QK_LANES = 128
ADAM_LR, ADAM_B1, ADAM_B2, ADAM_EPS, ADAM_WD, ADAM_STEP = 0.001, 0.9, 0.999, 1e-08, 0.01, 10
N_DEV = 8
N_CHIP = 4
W_IN_PIECE = 493
W_IN_PIECE_PAD = 496
LAT_W = 512
LAT_Q, LAT_KV, LAT_KR, LAT_F = 0, 256, 384, 416
MASK_VALUE = -1e30
FIXED_SHIFT_MAX_BOUND = 30.0

TOK_TILE = 512
PROJ_TILE = 256
ATT_TILE = 512
ATT_COL_CHUNK = 256
FFN_HID_TILE = 1408
FFN_HID_SPLIT = ((0, 768), (768, 1408))

NT = (((1,), (1,)), ((), ()))
TN = (((0,), (0,)), ((), ()))
NN = (((1,), (0,)), ((), ()))
MESH = pl.DeviceIdType.MESH


def _dot(a, b, dims=NN):
    return lax.dot_general(a, b, dims, preferred_element_type=F32)


def _sds(shape, dtype):
    return jax.ShapeDtypeStruct(shape, dtype)


def _rms_fwd(x, g):
    r = lax.rsqrt(jnp.mean(x * x, axis=-1, keepdims=True) + RMS_EPS)
    return x * r * g, r


def _rms_bwd(dy, x, g, r):
    xn = x * r
    dyg = dy * g
    dx = r * (dyg - xn * jnp.mean(dyg * xn, axis=-1, keepdims=True))
    return dx, dy * xn


def _colsum(x):
    return jnp.sum(x, axis=0, keepdims=True)


def _full(shape):
    return pl.BlockSpec(shape, lambda *_: (0,) * len(shape))


def _tok(tm, n):
    return pl.BlockSpec((tm, n), lambda i, *_: (i, 0))


def _ffn_fwd(x, gain, wg_t, wu_t, wd, target=None, exchange=None):
    t = x.shape[0]
    tm = min(TOK_TILE, t)
    tf = FFN_HID_TILE
    n_t, n_f = t // tm, FFN_HIDDEN // tf
    with_loss = target is not None

    def body(*refs):
        if with_loss:
            x_ref, g_ref, wg_ref, wu_ref, wd_ref, t_ref, out_ref, a_ref, b_ref, lvec_ref, n_scr, acc = refs
        else:
            x_ref, g_ref, wg_ref, wu_ref, wd_ref, out_ref, a_ref, b_ref, n_scr, acc = refs
        i, j = pl.program_id(0), pl.program_id(1)

        @pl.when(j == 0)
        def _():
            xn, _ = _rms_fwd(x_ref[...], g_ref[...])
            n_scr[...] = xn.astype(BF16)
            acc[...] = jnp.zeros_like(acc)

        n = n_scr[...]
        a = _dot(n, wg_ref[...], NT)
        b = _dot(n, wu_ref[...], NT)
        a_ref[...] = a.astype(BF16)
        b_ref[...] = b.astype(BF16)
        h = (a * jax.nn.sigmoid(a)) * b
        acc[...] += _dot(h.astype(BF16), wd_ref[...])

        @pl.when(j == n_f - 1)
        def _():
            y = x_ref[...] + 0.5 * acc[...]
            if with_loss:
                diff = y - t_ref[...]
                out_ref[...] = diff * (1.0 / D_MODEL)
                sq = _colsum(diff * diff)

                @pl.when(i == 0)
                def _():
                    lvec_ref[...] = sq

                @pl.when(i > 0)
                def _():
                    lvec_ref[...] += sq
            else:
                out_ref[...] = y

    wspec = pl.BlockSpec((tf, D_MODEL), lambda i, j: (j, 0))
    hspec = pl.BlockSpec((tm, tf), lambda i, j: (i, j))
    in_specs = [_tok(tm, D_MODEL), _full((1, D_MODEL)), wspec, wspec, wspec]
    out_specs = [_tok(tm, D_MODEL), hspec, hspec]
    out_shape = [_sds((t, D_MODEL), F32), _sds((t, FFN_HIDDEN), BF16), _sds((t, FFN_HIDDEN), BF16)]
    args = [x, gain, wg_t, wu_t, wd]
    if with_loss:
        in_specs.append(_tok(tm, D_MODEL))
        out_specs.append(_full((1, D_MODEL)))
        out_shape.append(_sds((1, D_MODEL), F32))
        args.append(target)
    return _gridded_call(
        body, name="ffn_fwd_loss" if with_loss else "ffn_fwd", grid=(n_t, n_f), in_specs=in_specs, out_specs=out_specs,
        out_shape=out_shape, scratch_shapes=[pltpu.VMEM((tm, D_MODEL), BF16), pltpu.VMEM((tm, D_MODEL), F32)],
        args=args, exchange=exchange)


def _ffn_bwd(dy, x, gain, a, b, wg_t, wu_t, wd, name, exchange=None):
    t = x.shape[0]
    tm = min(TOK_TILE, t)
    tf = FFN_HID_TILE
    n_t, n_f = t // tm, FFN_HIDDEN // tf

    def body(dy_ref, x_ref, g_ref, a_ref, b_ref, wg_ref, wu_ref, wd_ref,
             dx_ref, dg_ref, da_ref, db_ref, h_ref, n_ref, dyh_ref, acc):
        i, j = pl.program_id(0), pl.program_id(1)

        @pl.when(j == 0)
        def _():
            xn, _ = _rms_fwd(x_ref[...], g_ref[...])
            n_ref[...] = xn.astype(BF16)
            dyh_ref[...] = (0.5 * dy_ref[...]).astype(BF16)
            acc[...] = jnp.zeros_like(acc)

        dyh = dyh_ref[...]
        for lo, hi in FFN_HID_SPLIT:
            dh = _dot(dyh, wd_ref[lo:hi, :], NT)
            av = a_ref[:, lo:hi].astype(F32)
            bv = b_ref[:, lo:hi].astype(F32)
            s = jax.nn.sigmoid(av)
            silu = av * s
            da = (dh * bv * (s * (1.0 + av * (1.0 - s)))).astype(BF16)
            db = (dh * silu).astype(BF16)
            da_ref[:, lo:hi] = da
            db_ref[:, lo:hi] = db
            h_ref[:, lo:hi] = (silu * bv).astype(BF16)
            acc[...] += _dot(da, wg_ref[lo:hi, :]) + _dot(db, wu_ref[lo:hi, :])

        @pl.when(j == n_f - 1)
        def _():
            xv, g = x_ref[...], g_ref[...]
            r = lax.rsqrt(jnp.mean(xv * xv, axis=-1, keepdims=True) + RMS_EPS)
            dx, dg_rows = _rms_bwd(acc[...], xv, g, r)
            dx_ref[...] = dy_ref[...] + dx
            dg = _colsum(dg_rows)

            @pl.when(i == 0)
            def _():
                dg_ref[...] = dg

            @pl.when(i > 0)
            def _():
                dg_ref[...] += dg

    wspec = pl.BlockSpec((tf, D_MODEL), lambda i, j: (j, 0))
    hspec = pl.BlockSpec((tm, tf), lambda i, j: (i, j))
    tok = _tok(tm, D_MODEL)
    hid = _sds((t, FFN_HIDDEN), BF16)
    return _gridded_call(
        body, name=name, grid=(n_t, n_f),
        in_specs=[tok, tok, _full((1, D_MODEL)), hspec, hspec, wspec, wspec, wspec],
        out_specs=[tok, _full((1, D_MODEL)), hspec, hspec, hspec, tok, tok],
        out_shape=[_sds((t, D_MODEL), F32), _sds((1, D_MODEL), F32), hid, hid, hid,
                   _sds((t, D_MODEL), BF16), _sds((t, D_MODEL), BF16)],
        scratch_shapes=[pltpu.VMEM((tm, D_MODEL), F32)], args=[dy, x, gain, a, b, wg_t, wu_t, wd], exchange=exchange)


def _tn_matmul(a, b, name):
    t, m = a.shape
    n = b.shape[1]
    tk = min(TOK_TILE, t)
    n_k = t // tk

    def body(a_ref, b_ref, o_ref, acc):
        k = pl.program_id(0)
        p = _dot(a_ref[...], b_ref[...], TN)

        @pl.when(k == 0)
        def _():
            acc[...] = p

        @pl.when(k > 0)
        def _():
            acc[...] += p

        @pl.when(k == n_k - 1)
        def _():
            o_ref[...] = acc[...].astype(BF16)

    return pl.pallas_call(
        body, name=name, grid=(n_k,), in_specs=[_tok(tk, m), _tok(tk, n)], out_specs=_full((m, n)),
        out_shape=_sds((m, n), BF16), scratch_shapes=[pltpu.VMEM((m, n), F32)],
        compiler_params=pltpu.CompilerParams(dimension_semantics=("arbitrary",)),
    )(a, b)


def _proj_fwd(x, gain, wgate_t, wfox_t, wlat_t):
    t = x.shape[0]
    tm = min(PROJ_TILE, t)

    def body(x_ref, g_ref, wg_ref, wf_ref, wl_ref, og_ref, of_ref, ol_ref):
        xn, _ = _rms_fwd(x_ref[...], g_ref[...])
        n = xn.astype(BF16)
        og_ref[...] = _dot(n, wg_ref[...], NT)
        of_ref[...] = _dot(n, wf_ref[...], NT)
        ol_ref[...] = _dot(n, wl_ref[...], NT)

    return pl.pallas_call(
        body, name="proj_fwd", grid=(t // tm,),
        in_specs=[_tok(tm, D_MODEL), _full((1, D_MODEL)), _full(wgate_t.shape), _full(wfox_t.shape), _full(wlat_t.shape)],
        out_specs=[_tok(tm, 2 * D_MODEL), _tok(tm, 3 * 512), _tok(tm, LAT_W)],
        out_shape=[_sds((t, 2 * D_MODEL), F32), _sds((t, 3 * 512), F32), _sds((t, LAT_W), F32)],
        compiler_params=pltpu.CompilerParams(dimension_semantics=("arbitrary",)),
    )(x, gain, wgate_t, wfox_t, wlat_t)


def _proj_bwd(dgates, dfox, dlat, wgate_t, wfox_t, wlat_t, x, gain, dres):
    t = x.shape[0]
    tm = min(PROJ_TILE, t)

    def body(dg_ref, df_ref, dl_ref, wg_ref, wf_ref, wl_ref, x_ref, g_ref, dres_ref, dx_ref, dgain_ref, n_ref):
        i = pl.program_id(0)
        dn = _dot(dg_ref[...], wg_ref[...]) + _dot(df_ref[...], wf_ref[...]) + _dot(dl_ref[...], wl_ref[...])
        xv, g = x_ref[...], g_ref[...]
        xn, r = _rms_fwd(xv, g)
        n_ref[...] = xn.astype(BF16)
        dx, dg_rows = _rms_bwd(dn, xv, g, r)
        dx_ref[...] = dres_ref[...] + dx
        dgn = _colsum(dg_rows)

        @pl.when(i == 0)
        def _():
            dgain_ref[...] = dgn

        @pl.when(i > 0)
        def _():
            dgain_ref[...] += dgn

    tok = _tok(tm, D_MODEL)
    return pl.pallas_call(
        body, name="proj_bwd", grid=(t // tm,),
        in_specs=[_tok(tm, 2 * D_MODEL), _tok(tm, 3 * 512), _tok(tm, LAT_W), _full(wgate_t.shape), _full(wfox_t.shape),
                  _full(wlat_t.shape), tok, _full((1, D_MODEL)), tok],
        out_specs=[tok, _full((1, D_MODEL)), tok],
        out_shape=[_sds((t, D_MODEL), F32), _sds((1, D_MODEL), F32), _sds((t, D_MODEL), BF16)],
        compiler_params=pltpu.CompilerParams(dimension_semantics=("arbitrary",)),
    )(dgates, dfox, dlat, wgate_t, wfox_t, wlat_t, x, gain, dres)


def _tri(n, lower):
    r = lax.broadcasted_iota(jnp.int32, (n, n), 0)
    c = lax.broadcasted_iota(jnp.int32, (n, n), 1)
    return ((c <= r) if lower else (c >= r)).astype(F32)


def _log_sigmoid(z):
    return jnp.minimum(z, 0.0) - jnp.log1p(jnp.exp(-jnp.abs(z)))


def _rope_pair_fwd(p1, p2, g1, g2, cos, sin):
    ss = jnp.sum(p1 * p1, axis=-1, keepdims=True) + jnp.sum(p2 * p2, axis=-1, keepdims=True)
    r = lax.rsqrt(ss * (1.0 / ROPE_DIM) + RMS_EPS)
    n1, n2 = p1 * r, p2 * r
    y1, y2 = n1 * g1, n2 * g2
    return y1 * cos - y2 * sin, y2 * cos + y1 * sin, n1, n2, r


def _rope_pair_bwd(do1, do2, n1, n2, r, g1, g2, cos, sin):
    dy1 = do1 * cos + do2 * sin
    dy2 = do2 * cos - do1 * sin
    dyg1, dyg2 = dy1 * g1, dy2 * g2
    mu = (jnp.sum(dyg1 * n1, axis=-1, keepdims=True) + jnp.sum(dyg2 * n2, axis=-1, keepdims=True)) * (1.0 / ROPE_DIM)
    return r * (dyg1 - n1 * mu), r * (dyg2 - n2 * mu), dy1 * n1, dy2 * n2


def _prep_fwd(fox, lat, cos, sin, wqb_t, wkvb_t, g_qlat, g_kvlat, g_qn, g_qr, g_kn, g_kr, g_fq, g_fk, b_f):
    t = fox.shape[0]
    tm = min(PROJ_TILE, t)
    hd, lanes = HEAD_DIM, MLA_QK_LANES

    def body(fox_ref, lat_ref, cos_ref, sin_ref, wqb_ref, wkvb_ref, gql_ref, gkvl_ref, gqn_ref, gqr_ref, gkn_ref,
             gkr_ref, gfq_ref, gfk_ref, bf_ref, fq_ref, fk_ref, fv_ref, c_ref, mq_ref, mk_ref, mv_ref,
             carry, qscr, kscr, pscr):
        i = pl.program_id(0)
        cos_v, sin_v = cos_ref[...], sin_ref[...]

        for h in range(N_HEADS):
            w = slice(h * hd, (h + 1) * hd)
            qn, _ = _rms_fwd(fox_ref[:, h * hd:(h + 1) * hd], gfq_ref[...])
            fq_ref[:, w] = (qn * FOX_SCALE).astype(BF16)
            kn, _ = _rms_fwd(fox_ref[:, 512 + h * hd:512 + (h + 1) * hd], gfk_ref[...])
            fk_ref[:, w] = kn.astype(BF16)
        fv_ref[...] = fox_ref[:, 1024:1536].astype(BF16)

        @pl.when(i == 0)
        def _():
            carry[...] = jnp.zeros_like(carry)

        logf = _log_sigmoid(lat_ref[:, LAT_F:LAT_F + N_HEADS] + bf_ref[...])
        c_ref[...] = jnp.dot(_tri(tm, True), logf, precision=lax.Precision.HIGHEST, preferred_element_type=F32) + carry[...]
        carry[...] += _colsum(logf)

        qlat_n, _ = _rms_fwd(lat_ref[:, LAT_Q:LAT_Q + Q_LORA], gql_ref[...])
        qlat_b = qlat_n.astype(BF16)
        g1, g2 = gqr_ref[:, 0:ROPE_HALF], gqr_ref[:, ROPE_HALF:ROPE_DIM]
        qscr[...] = jnp.zeros_like(qscr)
        pscr[:, 0:N_HEADS * (hd + ROPE_DIM)] = _dot(qlat_b, wqb_ref[...], NT)
        for h in range(N_HEADS):
            base = h * (hd + ROPE_DIM)
            pn = pscr[:, base:base + hd]
            p1 = pscr[:, base + hd:base + hd + ROPE_HALF]
            p2 = pscr[:, base + hd + ROPE_HALF:base + hd + ROPE_DIM]
            nope, _ = _rms_fwd(pn, gqn_ref[...])
            o1, o2, _, _, _ = _rope_pair_fwd(p1, p2, g1, g2, cos_v, sin_v)
            qscr[:, h * lanes:h * lanes + hd] = nope
            qscr[:, h * lanes + hd:h * lanes + hd + ROPE_HALF] = o1
            qscr[:, h * lanes + hd + ROPE_HALF:h * lanes + hd + ROPE_DIM] = o2
        mq_ref[...] = qscr[...].astype(BF16)

        kv_n, _ = _rms_fwd(lat_ref[:, LAT_KV:LAT_KV + KV_LORA], gkvl_ref[...])
        kv_b = kv_n.astype(BF16)
        kr1, kr2, _, _, _ = _rope_pair_fwd(lat_ref[:, LAT_KR:LAT_KR + ROPE_HALF], lat_ref[:, LAT_KR + ROPE_HALF:LAT_KR + ROPE_DIM],
                                           gkr_ref[:, 0:ROPE_HALF], gkr_ref[:, ROPE_HALF:ROPE_DIM], cos_v, sin_v)
        kscr[...] = jnp.zeros_like(kscr)
        pscr[...] = _dot(kv_b, wkvb_ref[...], NT)
        for h in range(N_HEADS):
            kn, _ = _rms_fwd(pscr[:, h * 2 * hd:h * 2 * hd + hd], gkn_ref[...])
            kscr[:, h * lanes:h * lanes + hd] = kn
            kscr[:, h * lanes + hd:h * lanes + hd + ROPE_HALF] = kr1
            kscr[:, h * lanes + hd + ROPE_HALF:h * lanes + hd + ROPE_DIM] = kr2
            mv_ref[:, h * hd:(h + 1) * hd] = pscr[:, h * 2 * hd + hd:(h + 1) * 2 * hd].astype(BF16)
        mk_ref[...] = kscr[...].astype(BF16)

    small = [g_qlat, g_kvlat, g_qn, g_qr, g_kn, g_kr, g_fq, g_fk, b_f]
    t512 = _tok(tm, 512)
    return pl.pallas_call(
        body, name="prep_fwd", grid=(t // tm,),
        in_specs=[_tok(tm, 1536), _tok(tm, LAT_W), _tok(tm, ROPE_HALF), _tok(tm, ROPE_HALF), _full(wqb_t.shape),
                  _full(wkvb_t.shape)] + [_full(s.shape) for s in small],
        out_specs=[t512, t512, t512, _tok(tm, N_HEADS), _tok(tm, N_HEADS * lanes), _tok(tm, N_HEADS * lanes), t512],
        out_shape=[_sds((t, 512), BF16), _sds((t, 512), BF16), _sds((t, 512), BF16), _sds((t, N_HEADS), F32),
                   _sds((t, N_HEADS * lanes), BF16), _sds((t, N_HEADS * lanes), BF16), _sds((t, 512), BF16)],
        scratch_shapes=[pltpu.VMEM((1, N_HEADS), F32), pltpu.VMEM((tm, N_HEADS * lanes), F32),
                        pltpu.VMEM((tm, N_HEADS * lanes), F32), pltpu.VMEM((tm, N_HEADS * 2 * hd), F32)],
        compiler_params=pltpu.CompilerParams(dimension_semantics=("arbitrary",)),
    )(fox, lat, cos, sin, wqb_t, wkvb_t, *small)


def _prep_bwd(fox, lat, cos, sin, wqb_t, wkvb_t, g_qlat, g_kvlat, g_qn, g_qr, g_kn, g_kr, g_fq, g_fk, b_f,
              dfq, dfk, dfv, dc, dmq, dmk, dmv):
    t = fox.shape[0]
    tm = min(PROJ_TILE, t)
    n_t = t // tm
    hd, lanes = HEAD_DIM, MLA_QK_LANES

    def body(fox_ref, lat_ref, cos_ref, sin_ref, wqb_ref, wkvb_ref, gql_ref, gkvl_ref, gqn_ref, gqr_ref, gkn_ref,
             gkr_ref, gfq_ref, gfk_ref, bf_ref, dfq_ref, dfk_ref, dfv_ref, dc_ref, dmq_ref, dmk_ref, dmv_ref,
             dfox_ref, dlat_ref, dwqb_ref, dwkvb_ref, o_gql, o_gkvl, o_gqn, o_gqr, o_gkn, o_gkr, o_gfq, o_gfk, o_bf,
             carry, lscr, wq_acc, wkv_acc, pscr, dscr):
        i = pl.program_id(0)
        cos_v, sin_v = cos_ref[...], sin_ref[...]
        small_out = [o_gql, o_gkvl, o_gqn, o_gqr, o_gkn, o_gkr, o_gfq, o_gfk, o_bf]

        @pl.when(i == 0)
        def _():
            carry[...] = jnp.zeros_like(carry)
            wq_acc[...] = jnp.zeros_like(wq_acc)
            wkv_acc[...] = jnp.zeros_like(wkv_acc)
            for o in small_out:
                o[...] = jnp.zeros_like(o)

        d_gfq = jnp.zeros((1, hd), F32)
        d_gfk = jnp.zeros((1, hd), F32)
        for h in range(N_HEADS):
            w = slice(h * hd, (h + 1) * hd)
            xq = fox_ref[:, h * hd:(h + 1) * hd]
            rq = lax.rsqrt(jnp.mean(xq * xq, axis=-1, keepdims=True) + RMS_EPS)
            dxq, gq_rows = _rms_bwd(dfq_ref[:, w] * FOX_SCALE, xq, gfq_ref[...], rq)
            dfox_ref[:, w] = dxq.astype(BF16)
            d_gfq += _colsum(gq_rows)
            xk = fox_ref[:, 512 + h * hd:512 + (h + 1) * hd]
            rk = lax.rsqrt(jnp.mean(xk * xk, axis=-1, keepdims=True) + RMS_EPS)
            dxk, gk_rows = _rms_bwd(dfk_ref[:, w], xk, gfk_ref[...], rk)
            dfox_ref[:, 512 + h * hd:512 + (h + 1) * hd] = dxk.astype(BF16)
            d_gfk += _colsum(gk_rows)
        dfox_ref[:, 1024:1536] = dfv_ref[...].astype(BF16)
        o_gfq[...] += d_gfq
        o_gfk[...] += d_gfk

        lscr[...] = jnp.zeros_like(lscr)

        dcv = dc_ref[...]
        dlogf = jnp.dot(_tri(tm, False), dcv, precision=lax.Precision.HIGHEST, preferred_element_type=F32) + carry[...]
        carry[...] += _colsum(dcv)
        z = lat_ref[:, LAT_F:LAT_F + N_HEADS] + bf_ref[...]
        dz = dlogf * jax.nn.sigmoid(-z)
        lscr[:, LAT_F:LAT_F + N_HEADS] = dz
        o_bf[...] += _colsum(dz)

        xql = lat_ref[:, LAT_Q:LAT_Q + Q_LORA]
        qlat_n, r_ql = _rms_fwd(xql, gql_ref[...])
        qlat_b = qlat_n.astype(BF16)
        g1, g2 = gqr_ref[:, 0:ROPE_HALF], gqr_ref[:, ROPE_HALF:ROPE_DIM]
        d_gqn = jnp.zeros((1, hd), F32)
        d_g1 = jnp.zeros((1, ROPE_HALF), F32)
        d_g2 = jnp.zeros((1, ROPE_HALF), F32)
        n_q = N_HEADS * (hd + ROPE_DIM)
        pscr[:, 0:n_q] = _dot(qlat_b, wqb_ref[...], NT)
        for h in range(N_HEADS):
            base = h * (hd + ROPE_DIM)
            rows_n = slice(base, base + hd)
            rows_1 = slice(base + hd, base + hd + ROPE_HALF)
            rows_2 = slice(base + hd + ROPE_HALF, base + hd + ROPE_DIM)
            pn, p1, p2 = pscr[:, rows_n], pscr[:, rows_1], pscr[:, rows_2]
            r_n = lax.rsqrt(jnp.mean(pn * pn, axis=-1, keepdims=True) + RMS_EPS)
            _, _, n1, n2, r_r = _rope_pair_fwd(p1, p2, g1, g2, cos_v, sin_v)
            dpn, gn_rows = _rms_bwd(dmq_ref[:, h * lanes:h * lanes + hd], pn, gqn_ref[...], r_n)
            dp1, dp2, g1_rows, g2_rows = _rope_pair_bwd(
                dmq_ref[:, h * lanes + hd:h * lanes + hd + ROPE_HALF],
                dmq_ref[:, h * lanes + hd + ROPE_HALF:h * lanes + hd + ROPE_DIM], n1, n2, r_r, g1, g2, cos_v, sin_v)
            d_gqn += _colsum(gn_rows)
            d_g1 += _colsum(g1_rows)
            d_g2 += _colsum(g2_rows)
            dscr[:, rows_n], dscr[:, rows_1], dscr[:, rows_2] = dpn, dp1, dp2
        dp_b = dscr[:, 0:n_q].astype(BF16)
        wq_acc[...] += _dot(dp_b, qlat_b, TN)
        dxql, gql_rows = _rms_bwd(_dot(dp_b, wqb_ref[...]), xql, gql_ref[...], r_ql)
        lscr[:, LAT_Q:LAT_Q + Q_LORA] = dxql
        o_gql[...] += _colsum(gql_rows)
        o_gqn[...] += d_gqn
        o_gqr[:, 0:ROPE_HALF] += d_g1
        o_gqr[:, ROPE_HALF:ROPE_DIM] += d_g2

        xkv = lat_ref[:, LAT_KV:LAT_KV + KV_LORA]
        kv_n, r_kv = _rms_fwd(xkv, gkvl_ref[...])
        kv_b = kv_n.astype(BF16)
        d_gkn = jnp.zeros((1, hd), F32)
        dkr1 = jnp.zeros((tm, ROPE_HALF), F32)
        dkr2 = jnp.zeros((tm, ROPE_HALF), F32)
        pscr[...] = _dot(kv_b, wkvb_ref[...], NT)
        for h in range(N_HEADS):
            rows_k = slice(h * 2 * hd, h * 2 * hd + hd)
            rows_v = slice(h * 2 * hd + hd, (h + 1) * 2 * hd)
            pk = pscr[:, rows_k]
            r_k = lax.rsqrt(jnp.mean(pk * pk, axis=-1, keepdims=True) + RMS_EPS)
            dpk, gk_rows = _rms_bwd(dmk_ref[:, h * lanes:h * lanes + hd], pk, gkn_ref[...], r_k)
            d_gkn += _colsum(gk_rows)
            dkr1 += dmk_ref[:, h * lanes + hd:h * lanes + hd + ROPE_HALF]
            dkr2 += dmk_ref[:, h * lanes + hd + ROPE_HALF:h * lanes + hd + ROPE_DIM]
            dscr[:, rows_k] = dpk
            dscr[:, rows_v] = dmv_ref[:, h * hd:(h + 1) * hd]
        dkv_b = dscr[...].astype(BF16)
        wkv_acc[...] += _dot(dkv_b, kv_b, TN)
        dxkv, gkvl_rows = _rms_bwd(_dot(dkv_b, wkvb_ref[...]), xkv, gkvl_ref[...], r_kv)
        lscr[:, LAT_KV:LAT_KV + KV_LORA] = dxkv
        o_gkvl[...] += _colsum(gkvl_rows)
        o_gkn[...] += d_gkn

        gk1, gk2 = gkr_ref[:, 0:ROPE_HALF], gkr_ref[:, ROPE_HALF:ROPE_DIM]
        _, _, kn1, kn2, r_kr = _rope_pair_fwd(lat_ref[:, LAT_KR:LAT_KR + ROPE_HALF],
                                              lat_ref[:, LAT_KR + ROPE_HALF:LAT_KR + ROPE_DIM], gk1, gk2, cos_v, sin_v)
        dk1, dk2, gk1_rows, gk2_rows = _rope_pair_bwd(dkr1, dkr2, kn1, kn2, r_kr, gk1, gk2, cos_v, sin_v)
        lscr[:, LAT_KR:LAT_KR + ROPE_HALF] = dk1
        lscr[:, LAT_KR + ROPE_HALF:LAT_KR + ROPE_DIM] = dk2
        o_gkr[:, 0:ROPE_HALF] += _colsum(gk1_rows)
        o_gkr[:, ROPE_HALF:ROPE_DIM] += _colsum(gk2_rows)

        dlat_ref[...] = lscr[...].astype(BF16)

        @pl.when(i == n_t - 1)
        def _():
            dwqb_ref[...] = wq_acc[...].astype(BF16)
            dwkvb_ref[...] = wkv_acc[...].astype(BF16)

    small = [g_qlat, g_kvlat, g_qn, g_qr, g_kn, g_kr, g_fq, g_fk, b_f]

    def rtok(n):
        return pl.BlockSpec((tm, n), lambda i: (n_t - 1 - i, 0))

    return pl.pallas_call(
        body, name="prep_bwd", grid=(n_t,),
        in_specs=[rtok(1536), rtok(LAT_W), rtok(ROPE_HALF), rtok(ROPE_HALF), _full(wqb_t.shape), _full(wkvb_t.shape)]
        + [_full(s.shape) for s in small]
        + [rtok(512), rtok(512), rtok(512), rtok(N_HEADS), rtok(N_HEADS * lanes), rtok(N_HEADS * lanes), rtok(512)],
        out_specs=[rtok(1536), rtok(LAT_W), _full(wqb_t.shape), _full(wkvb_t.shape)] + [_full(s.shape) for s in small],
        out_shape=[_sds((t, 1536), BF16), _sds((t, LAT_W), BF16), _sds(wqb_t.shape, BF16), _sds(wkvb_t.shape, BF16)]
        + [_sds(s.shape, F32) for s in small],
        scratch_shapes=[pltpu.VMEM((1, N_HEADS), F32), pltpu.VMEM((tm, LAT_W), F32), pltpu.VMEM(wqb_t.shape, F32),
                        pltpu.VMEM(wkvb_t.shape, F32), pltpu.VMEM((tm, N_HEADS * 2 * hd), F32),
                        pltpu.VMEM((tm, N_HEADS * 2 * hd), F32)],
        compiler_params=pltpu.CompilerParams(dimension_semantics=("arbitrary",)),
    )(fox, lat, cos, sin, wqb_t, wkvb_t, *small, dfq, dfk, dfv, dc, dmq, dmk, dmv)


def _qk_norms(q, k, *, lanes, name):
    t = q.shape[0]
    tm = min(TOK_TILE, t)

    def body(q_ref, k_ref, qn_ref, kmax_ref):
        i = pl.program_id(0)

        @pl.when(i == 0)
        def _():
            kmax_ref[...] = jnp.zeros_like(kmax_ref)

        for h in range(N_HEADS):
            qh = q_ref[:, h * lanes:(h + 1) * lanes].astype(F32)
            kh = k_ref[:, h * lanes:(h + 1) * lanes].astype(F32)
            qn_ref[:, h:h + 1] = jnp.sum(qh * qh, axis=1, keepdims=True)
            kmax = jnp.max(jnp.sum(kh * kh, axis=1, keepdims=True), axis=0, keepdims=True)
            kmax_ref[:, h:h + 1] = jnp.maximum(kmax_ref[:, h:h + 1], kmax)

    return pl.pallas_call(
        body, name=name, grid=(t // tm,), in_specs=[_tok(tm, N_HEADS * lanes), _tok(tm, N_HEADS * lanes)],
        out_specs=[_tok(tm, N_HEADS), _full((1, N_HEADS))], out_shape=[_sds((t, N_HEADS), F32), _sds((1, N_HEADS), F32)],
        compiler_params=pltpu.CompilerParams(dimension_semantics=("arbitrary",)),
    )(q, k)


def _logit_bound(q, k, *, lanes, scale, name):
    qn, kmax = _qk_norms(q, k, lanes=lanes, name=name)
    bound = jnp.sqrt(qn * kmax) * (scale * (1.0 + 2.0 ** -10)) + 2.0 ** -10
    flag = (jnp.max(bound) <= FIXED_SHIFT_MAX_BOUND).astype(F32).reshape(1, 1)
    return bound, flag


def _attn_fwd(q, k, v, bound, fixed_ok, c, c_t, *, lanes, scale, name, exchange=None):
    t = q.shape[0]
    tq = min(ATT_TILE, t)
    n_q = t // tq
    hd = HEAD_DIM
    ch = min(ATT_COL_CHUNK, tq)
    decay = c is not None

    def body(*refs):
        if decay:
            q_ref, k_ref, v_ref, b_ref, ok_ref, c_ref, ct_ref, o_ref, o32_ref, lse_ref, m_scr, l_scr, acc = refs
        else:
            q_ref, k_ref, v_ref, b_ref, ok_ref, o_ref, o32_ref, lse_ref, m_scr, l_scr, acc = refs
        i, j = pl.program_id(0), pl.program_id(1)
        fixed = ok_ref[0, 0] > 0.5

        @pl.when(j == 0)
        def _():
            m_scr[...] = jnp.full_like(m_scr, MASK_VALUE)
            l_scr[...] = jnp.zeros_like(l_scr)
            acc[...] = jnp.zeros_like(acc)

        def fixed_step(diagonal):
            for h in range(N_HEADS):
                wl = slice(h * lanes, (h + 1) * lanes)
                w = slice(h * hd, (h + 1) * hd)
                qh = q_ref[:, wl]
                row = (c_ref[:, h:h + 1] - b_ref[:, h:h + 1]) if decay else -b_ref[:, h:h + 1]
                l_new = jnp.zeros((tq, 1), F32)
                o_hi = jnp.zeros((tq, hd), F32)
                o_lo = jnp.zeros((tq, hd), F32)
                for cc in range(tq // ch):
                    cols = slice(cc * ch, (cc + 1) * ch)
                    s = _dot(qh, k_ref[cols, wl], NT)
                    if scale != 1.0:
                        s = s * scale
                    s = s + ((row - ct_ref[h:h + 1, cols]) if decay else row)
                    if diagonal:
                        keep = (lax.broadcasted_iota(jnp.int32, (tq, ch), 0)
                                >= lax.broadcasted_iota(jnp.int32, (tq, ch), 1) + cc * ch)
                        s = jnp.where(keep, s, MASK_VALUE)
                    p = jnp.exp(s)
                    l_new = l_new + jnp.sum(p, axis=1, keepdims=True)
                    p_b = p.astype(BF16)
                    o_hi = o_hi + _dot(p_b, v_ref[cols, w])
                    if decay:
                        o_lo = o_lo + _dot((p - p_b.astype(F32)).astype(BF16), v_ref[cols, w])
                l_scr[h] += l_new
                acc[0, :, w] += o_hi
                if decay:
                    acc[1, :, w] += o_lo

        def step(diagonal):
            if diagonal:
                keep = lax.broadcasted_iota(jnp.int32, (tq, tq), 0) >= lax.broadcasted_iota(jnp.int32, (tq, tq), 1)
            for h in range(N_HEADS):
                s = _dot(q_ref[:, h * lanes:(h + 1) * lanes], k_ref[:, h * lanes:(h + 1) * lanes], NT)
                if scale != 1.0:
                    s = s * scale
                if decay:
                    s = s + (c_ref[:, h:h + 1] - ct_ref[h:h + 1, :])
                if diagonal:
                    s = jnp.where(keep, s, MASK_VALUE)
                m_prev = m_scr[h]
                m_new = jnp.maximum(m_prev, jnp.max(s, axis=1, keepdims=True))
                alpha = jnp.exp(m_prev - m_new)
                p = jnp.exp(s - m_new)
                l_scr[h] = alpha * l_scr[h] + jnp.sum(p, axis=1, keepdims=True)
                w = slice(h * hd, (h + 1) * hd)
                p_b = p.astype(BF16)
                acc[0, :, w] = alpha * acc[0, :, w] + _dot(p_b, v_ref[:, w])
                if decay:
                    p_lo = (p - p_b.astype(F32)).astype(BF16)
                    acc[1, :, w] = alpha * acc[1, :, w] + _dot(p_lo, v_ref[:, w])
                m_scr[h] = m_new

        for diagonal, here in ((False, j < i), (True, j == i)):
            @pl.when(here & fixed)
            def _():
                fixed_step(diagonal)

            @pl.when(here & jnp.logical_not(fixed))
            def _():
                step(diagonal)

        @pl.when(j == i)
        def _():
            for h in range(N_HEADS):
                w = slice(h * hd, (h + 1) * hd)
                l = l_scr[h]
                o_ref[:, w] = (acc[0, :, w] / l).astype(BF16)
                o32_ref[:, w] = ((acc[0, :, w] + acc[1, :, w]) if decay else acc[0, :, w]) / l
                lse_ref[:, h:h + 1] = jnp.where(fixed, b_ref[:, h:h + 1], m_scr[h]) + jnp.log(l)

    qspec = lambda n: pl.BlockSpec((tq, n), lambda i, j: (i, 0))
    kspec = lambda n: pl.BlockSpec((tq, n), lambda i, j: (jnp.minimum(i, j), 0))
    in_specs = [qspec(N_HEADS * lanes), kspec(N_HEADS * lanes), kspec(512), qspec(N_HEADS),
                pl.BlockSpec(memory_space=pltpu.SMEM)]
    args = [q, k, v, bound, fixed_ok]
    if decay:
        in_specs += [qspec(N_HEADS), pl.BlockSpec((N_HEADS, tq), lambda i, j: (0, jnp.minimum(i, j)))]
        args += [c, c_t]
    return _gridded_call(
        body, name=name, grid=(n_q, n_q), in_specs=in_specs, out_specs=[qspec(512), qspec(512), qspec(N_HEADS)],
        out_shape=[_sds((t, 512), BF16), _sds((t, 512), F32), _sds((t, N_HEADS), F32)],
        scratch_shapes=[pltpu.VMEM((N_HEADS, tq, 1), F32), pltpu.VMEM((N_HEADS, tq, 1), F32), pltpu.VMEM((2, tq, 512), F32)],
        args=args, exchange=exchange)


def _attn_bwd(q, k, v, do, delta, lse, c, c_t, *, lanes, scale, name, exchange=None):
    t = q.shape[0]
    tq = min(ATT_TILE, t)
    n_q = t // tq
    hd = HEAD_DIM
    decay = c is not None

    def body(*refs):
        if decay:
            q_ref, k_ref, v_ref, do_ref, delta_ref, lse_ref, c_ref, ct_ref, dq_hbm, dk_ref, dv_ref, dct_ref, dq_ref = refs
        else:
            q_ref, k_ref, v_ref, do_ref, delta_ref, lse_ref, dq_hbm, dk_ref, dv_ref, dq_ref = refs
        j, i = pl.program_id(0), pl.program_id(1)

        @pl.when((j == 0) & (i == 0))
        def _():
            dq_ref[...] = jnp.zeros_like(dq_ref)

        @pl.when(i == j)
        def _():
            dk_ref[...] = jnp.zeros_like(dk_ref)
            dv_ref[...] = jnp.zeros_like(dv_ref)
            if decay:
                dct_ref[...] = jnp.zeros_like(dct_ref)

        def step(diagonal):
            if diagonal:
                keep = lax.broadcasted_iota(jnp.int32, (tq, tq), 0) >= lax.broadcasted_iota(jnp.int32, (tq, tq), 1)
            rows = pl.ds(pl.multiple_of(i * tq, tq), tq)
            for h in range(N_HEADS):
                wl = slice(h * lanes, (h + 1) * lanes)
                w = slice(h * hd, (h + 1) * hd)
                qh, kh = q_ref[:, wl], k_ref[:, wl]
                s = _dot(qh, kh, NT)
                if scale != 1.0:
                    s = s * scale
                if decay:
                    s = s + (c_ref[:, h:h + 1] - ct_ref[h:h + 1, :])
                if diagonal:
                    s = jnp.where(keep, s, MASK_VALUE)
                p = jnp.exp(s - lse_ref[:, h:h + 1])
                doh = do_ref[:, w]
                dv_ref[:, w] += _dot(p.astype(BF16), doh, TN)
                dp = _dot(doh, v_ref[:, w], NT)
                ds = p * (dp - delta_ref[:, h:h + 1])
                if decay:
                    dct_ref[h:h + 1, :] -= _colsum(ds)
                if scale != 1.0:
                    ds = ds * scale
                ds_b = ds.astype(BF16)
                dk_ref[:, wl] += _dot(ds_b, qh, TN)
                dq_ref[rows, wl] += _dot(ds_b, kh)

        @pl.when(i > j)
        def _():
            step(False)

        @pl.when(i == j)
        def _():
            step(True)

        @pl.when((j == n_q - 1) & (i == n_q - 1))
        def _():
            pltpu.sync_copy(dq_ref, dq_hbm)

    qspec = lambda n: pl.BlockSpec((tq, n), lambda j, i: (jnp.maximum(i, j), 0))
    kspec = lambda n: pl.BlockSpec((tq, n), lambda j, i: (j, 0))
    in_specs = [qspec(N_HEADS * lanes), kspec(N_HEADS * lanes), kspec(512), qspec(512), qspec(N_HEADS), qspec(N_HEADS)]
    out_specs = [pl.BlockSpec(memory_space=pl.ANY), kspec(N_HEADS * lanes), kspec(512)]
    out_shape = [_sds((t, N_HEADS * lanes), F32), _sds((t, N_HEADS * lanes), F32), _sds((t, 512), F32)]
    args = [q, k, v, do, delta, lse]
    if decay:
        ctspec = pl.BlockSpec((N_HEADS, tq), lambda j, i: (0, j))
        in_specs += [qspec(N_HEADS), ctspec]
        out_specs.append(ctspec)
        out_shape.append(_sds((N_HEADS, t), F32))
        args += [c, c_t]
    return _gridded_call(body, name=name, grid=(n_q, n_q), in_specs=in_specs, out_specs=out_specs, out_shape=out_shape,
                         scratch_shapes=[pltpu.VMEM((t, N_HEADS * lanes), F32)], args=args, exchange=exchange)


def _mix_fwd(x, y_mla, y_fox, gates, b_gate, wbm_t, wbf_t, wo):
    t = x.shape[0]
    tm = min(PROJ_TILE, t)

    def body(x_ref, ym_ref, yf_ref, gt_ref, bg_ref, wbm_ref, wbf_ref, wo_ref, out_ref):
        um = _dot(ym_ref[...], wbm_ref[...], NT)
        uf = _dot(yf_ref[...], wbf_ref[...], NT)
        sm = jax.nn.sigmoid(gt_ref[:, 0:D_MODEL] + bg_ref[0:1, :])
        sf = jax.nn.sigmoid(gt_ref[:, D_MODEL:2 * D_MODEL] + bg_ref[1:2, :])
        mixed = sm * um + sf * uf
        out_ref[...] = x_ref[...] + _dot(mixed.astype(BF16), wo_ref[...])

    tok = _tok(tm, D_MODEL)
    return pl.pallas_call(
        body, name="mix_fwd", grid=(t // tm,),
        in_specs=[tok, _tok(tm, 512), _tok(tm, 512), _tok(tm, 2 * D_MODEL), _full((2, D_MODEL)), _full(wbm_t.shape),
                  _full(wbf_t.shape), _full(wo.shape)],
        out_specs=tok, out_shape=_sds((t, D_MODEL), F32),
        compiler_params=pltpu.CompilerParams(dimension_semantics=("arbitrary",)),
    )(x, y_mla, y_fox, gates, b_gate, wbm_t, wbf_t, wo)


def _mix_bwd(dx, y_mla, y_fox, y_mla32, y_fox32, gates, b_gate, wbm_t, wbf_t, wo):
    t = dx.shape[0]
    tm = min(PROJ_TILE, t)

    def body(dx_ref, ym_ref, yf_ref, ym32_ref, yf32_ref, gt_ref, bg_ref, wbm_ref, wbf_ref, wo_ref,
             dym_ref, dyf_ref, dlm_ref, dlf_ref, dgt_ref, mixed_ref, dum_ref, duf_ref, dxb_ref, dbg_ref, prod):
        i = pl.program_id(0)
        dxb = dx_ref[...].astype(BF16)
        dxb_ref[...] = dxb
        dmixed = _dot(dxb, wo_ref[...], NT)
        um = _dot(ym_ref[...], wbm_ref[...], NT)
        uf = _dot(yf_ref[...], wbf_ref[...], NT)
        sm = jax.nn.sigmoid(gt_ref[:, 0:D_MODEL] + bg_ref[0:1, :])
        sf = jax.nn.sigmoid(gt_ref[:, D_MODEL:2 * D_MODEL] + bg_ref[1:2, :])
        mixed_ref[...] = (sm * um + sf * uf).astype(BF16)
        dum = (dmixed * sm).astype(BF16)
        duf = (dmixed * sf).astype(BF16)
        dum_ref[...] = dum
        duf_ref[...] = duf
        dgm = dmixed * um * (sm * (1.0 - sm))
        dgf = dmixed * uf * (sf * (1.0 - sf))
        dgt_ref[:, 0:D_MODEL] = dgm.astype(BF16)
        dgt_ref[:, D_MODEL:2 * D_MODEL] = dgf.astype(BF16)
        for du, wb_ref, y32_ref, dy_ref, dl_ref in ((dum, wbm_ref, ym32_ref, dym_ref, dlm_ref),
                                                    (duf, wbf_ref, yf32_ref, dyf_ref, dlf_ref)):
            dy = _dot(du, wb_ref[...])
            dy_ref[...] = dy.astype(BF16)
            prod[...] = dy.astype(BF16).astype(F32) * y32_ref[...]
            for h in range(N_HEADS):
                dl_ref[:, h:h + 1] = jnp.sum(prod[:, h * HEAD_DIM:(h + 1) * HEAD_DIM], axis=1, keepdims=True)

        @pl.when(i == 0)
        def _():
            dbg_ref[...] = jnp.zeros_like(dbg_ref)

        dbg_ref[0:1, :] += _colsum(dgm)
        dbg_ref[1:2, :] += _colsum(dgf)

    tok = _tok(tm, D_MODEL)
    tokb = _sds((t, D_MODEL), BF16)
    t512, t8 = _tok(tm, 512), _tok(tm, N_HEADS)
    return pl.pallas_call(
        body, name="mix_bwd", grid=(t // tm,),
        in_specs=[tok, t512, t512, t512, t512, _tok(tm, 2 * D_MODEL), _full((2, D_MODEL)), _full(wbm_t.shape),
                  _full(wbf_t.shape), _full(wo.shape)],
        out_specs=[t512, t512, t8, t8, _tok(tm, 2 * D_MODEL), tok, tok, tok, tok, _full((2, D_MODEL))],
        out_shape=[_sds((t, 512), BF16), _sds((t, 512), BF16), _sds((t, N_HEADS), F32), _sds((t, N_HEADS), F32),
                   _sds((t, 2 * D_MODEL), BF16), tokb, tokb, tokb, tokb, _sds((2, D_MODEL), F32)],
        scratch_shapes=[pltpu.VMEM((tm, 512), F32)],
        compiler_params=pltpu.CompilerParams(dimension_semantics=("arbitrary",)),
    )(dx, y_mla, y_fox, y_mla32, y_fox32, gates, b_gate, wbm_t, wbf_t, wo)


def _my_position():
    x, y, c = lax.axis_index("x"), lax.axis_index("y"), lax.axis_index("c")
    return x, y, c, 4 * x + 2 * y + c


def _peer(x, y, c, mask):
    px = 1 - x if mask & 4 else x
    py = 1 - y if mask & 2 else y
    pc = 1 - c if mask & 1 else c
    return (px, py, pc), 4 * px + 2 * py + pc


def _chip_peer(x, y, km):
    px = 1 - x if km & 2 else x
    py = 1 - y if km & 1 else y
    return px, py, 2 * px + py


_HBM = pl.BlockSpec(memory_space=pl.ANY)


def _wait_all(copies):
    for cp in copies:
        cp.wait()


class _ChipExchange:
    def __init__(self, gather, arrays):
        self.gather, self.arrays = gather, list(arrays)
        n = len(self.arrays)
        self.out_shape = [_sds((N_DEV * a.shape[0],) + a.shape[1:], a.dtype) if gather else _sds(a.shape, a.dtype)
                          for a in self.arrays]
        self.scratch_shapes = [pltpu.SemaphoreType.DMA((n, N_CHIP)), pltpu.SemaphoreType.DMA((n, N_CHIP)),
                               pltpu.SemaphoreType.DMA((n,))]

    def copies(self, srcs, dsts, send_sems, recv_sems, local_sems):
        x, y, c, me = _my_position()
        q_me = 2 * x + y
        out = []
        for a in range(len(self.arrays)):
            if self.gather:
                r = srcs[a].shape[0]
                local_src, dst = srcs[a], dsts[a].at[pl.ds(me * r, r)]
            else:
                local_src, dst = srcs[a].at[q_me], dsts[a].at[q_me]
            out.append(pltpu.make_async_copy(local_src, dst, local_sems.at[a]))
            for km in range(1, N_CHIP):
                px, py, q_peer = _chip_peer(x, y, km)
                out.append(pltpu.make_async_remote_copy(
                    src_ref=srcs[a] if self.gather else srcs[a].at[q_peer], dst_ref=dst, send_sem=send_sems.at[a, km],
                    recv_sem=recv_sems.at[a, km], device_id=(px, py, c), device_id_type=MESH))
        return out

    def standalone(self, name):
        n = len(self.arrays)

        def body(*refs):
            copies = self.copies(refs[:n], refs[n:2 * n], *refs[2 * n:])
            for cp in copies:
                cp.start()
            _wait_all(copies)

        return pl.pallas_call(body, name=name, in_specs=[_HBM] * n, out_specs=[_HBM] * n, out_shape=self.out_shape,
                              scratch_shapes=self.scratch_shapes)(*self.arrays)


def _gridded_call(body, *, name, grid, in_specs, out_specs, out_shape, scratch_shapes, args, exchange=None):
    params = pltpu.CompilerParams(dimension_semantics=("arbitrary",) * len(grid))
    if exchange is None:
        return pl.pallas_call(body, name=name, grid=grid, in_specs=in_specs, out_specs=out_specs, out_shape=out_shape,
                              scratch_shapes=scratch_shapes, compiler_params=params)(*args), None
    n_in, n_out, n_scr, n_x = len(in_specs), len(out_specs), len(scratch_shapes), len(exchange.arrays)

    def carrier(*refs):
        ins, x_src, refs = refs[:n_in], refs[n_in:n_in + n_x], refs[n_in + n_x:]
        outs, x_dst, refs = refs[:n_out], refs[n_out:n_out + n_x], refs[n_out + n_x:]
        copies = exchange.copies(x_src, x_dst, *refs[n_scr:])
        pids = [pl.program_id(d) for d in range(len(grid))]
        first = functools.reduce(jnp.logical_and, [p == 0 for p in pids])
        last = functools.reduce(jnp.logical_and, [p == g - 1 for p, g in zip(pids, grid)])

        @pl.when(first)
        def _():
            for cp in copies:
                cp.start()

        body(*ins, *outs, *refs[:n_scr])

        @pl.when(last)
        def _():
            _wait_all(copies)

    res = pl.pallas_call(
        carrier, name=name, grid=grid, in_specs=list(in_specs) + [_HBM] * n_x, out_specs=list(out_specs) + [_HBM] * n_x,
        out_shape=list(out_shape) + exchange.out_shape, scratch_shapes=list(scratch_shapes) + exchange.scratch_shapes,
        compiler_params=params)(*args, *exchange.arrays)
    return res[:n_out], res[n_out:]


def _gather_over_cores(arrays, name):
    n_arr = len(arrays)

    def body(*refs):
        srcs, dsts = refs[:n_arr], refs[n_arr:2 * n_arr]
        send_sems, recv_sems = refs[2 * n_arr:]
        x, y, c, _ = _my_position()
        copies = []
        for a in range(n_arr):
            r = srcs[a].shape[0] // N_DEV
            for q in range(N_CHIP):
                rows = pl.ds((2 * q + c) * r, r)
                copies.append(pltpu.make_async_remote_copy(
                    src_ref=srcs[a].at[rows], dst_ref=dsts[a].at[rows], send_sem=send_sems.at[a, q],
                    recv_sem=recv_sems.at[a, q], device_id=(x, y, 1 - c), device_id_type=MESH))
        for cp in copies:
            cp.start()
        _wait_all(copies)

    return pl.pallas_call(
        body, name=name, in_specs=[_HBM] * n_arr, out_specs=[_HBM] * n_arr,
        out_shape=[_sds(a.shape, a.dtype) for a in arrays], input_output_aliases={a: a for a in range(n_arr)},
        scratch_shapes=[pltpu.SemaphoreType.DMA((n_arr, N_CHIP)), pltpu.SemaphoreType.DMA((n_arr, N_CHIP))],
    )(*arrays)


def _grads_to_sibling(grads, name):
    n_arr = len(grads)

    def body(*refs):
        srcs, dsts = refs[:n_arr], refs[n_arr:2 * n_arr]
        send_sems, recv_sems = refs[2 * n_arr:]
        x, y, c, _ = _my_position()
        copies = []
        for a in range(n_arr):
            r = srcs[a].shape[0] // N_DEV
            for q in range(N_CHIP):
                copies.append(pltpu.make_async_remote_copy(
                    src_ref=srcs[a].at[pl.ds((2 * q + 1 - c) * r, r)], dst_ref=dsts[a].at[q], send_sem=send_sems.at[a, q],
                    recv_sem=recv_sems.at[a, q], device_id=(x, y, 1 - c), device_id_type=MESH))
        for cp in copies:
            cp.start()
        _wait_all(copies)

    return pl.pallas_call(
        body, name=name, in_specs=[_HBM] * n_arr, out_specs=[_HBM] * n_arr,
        out_shape=[_sds((N_CHIP, g.shape[0] // N_DEV) + g.shape[1:], g.dtype) for g in grads],
        scratch_shapes=[pltpu.SemaphoreType.DMA((n_arr, N_CHIP)), pltpu.SemaphoreType.DMA((n_arr, N_CHIP))],
    )(*grads)


def _pair_sum(grad, from_sibling, name):
    r, n = from_sibling.shape[1:]

    def body(g_ref, s_ref, o_ref):
        c = lax.axis_index("c")
        o_ref[...] = (g_ref[c].astype(F32) + s_ref[...].astype(F32)).astype(BF16)

    return pl.pallas_call(
        body, name=name, grid=(N_CHIP,),
        in_specs=[pl.BlockSpec((None, 2, r, n), lambda q: (q, 0, 0, 0)), pl.BlockSpec((None, r, n), lambda q: (q, 0, 0))],
        out_specs=pl.BlockSpec((None, r, n), lambda q: (q, 0, 0)), out_shape=_sds((N_CHIP, r, n), BF16),
    )(grad.reshape(N_CHIP, 2, r, n), from_sibling)


def _all_reduce_small(vec):
    r = vec.shape[0]

    def body(v_ref, o_ref, buf, send_sems, recv_sems):
        x, y, c, me = _my_position()
        buf[me] = v_ref[...]
        copies = []
        for mask in range(1, N_DEV):
            peer, _ = _peer(x, y, c, mask)
            cp = pltpu.make_async_remote_copy(src_ref=v_ref, dst_ref=buf.at[me], send_sem=send_sems.at[mask],
                                              recv_sem=recv_sems.at[mask], device_id=peer, device_id_type=MESH)
            cp.start()
            copies.append(cp)
        for cp in copies:
            cp.wait()
        total = buf[0]
        for s in range(1, N_DEV):
            total = total + buf[s]
        o_ref[...] = total

    vm = pl.BlockSpec(memory_space=pltpu.VMEM)
    return pl.pallas_call(
        body, name="all_reduce_small", in_specs=[vm], out_specs=vm, out_shape=_sds(vec.shape, F32),
        scratch_shapes=[pltpu.VMEM((N_DEV, r, 128), F32), pltpu.SemaphoreType.DMA((N_DEV,)), pltpu.SemaphoreType.DMA((N_DEV,))],
    )(vec)


def _adamw_math(w, g, m, v):
    m = ADAM_B1 * m + (1.0 - ADAM_B1) * g
    v = ADAM_B2 * v + (1.0 - ADAM_B2) * (g * g)
    m_hat = m / (1.0 - ADAM_B1 ** ADAM_STEP)
    v_hat = v / (1.0 - ADAM_B2 ** ADAM_STEP)
    delta = -ADAM_LR * (m_hat / (jnp.sqrt(v_hat) + ADAM_EPS) + ADAM_WD * w)
    return delta, m, v


def _reduce_adamw(slots, w, m, v, *, transpose, name):
    r, n = slots.shape[1:]

    def body(s_ref, w_ref, m_ref, v_ref, g_ref, d_ref, nm_ref, nv_ref, *scr):
        g = s_ref[0].astype(F32)
        for s in range(1, slots.shape[0]):
            g = g + s_ref[s].astype(F32)
        if transpose:
            scr[0][...] = g.T
            g = scr[0][:, 0:w_ref.shape[1]]
        g_ref[...] = g
        d_ref[...], nm_ref[...], nv_ref[...] = _adamw_math(w_ref[...], g, m_ref[...], v_ref[...])

    out = _sds(w.shape, F32)
    return pl.pallas_call(
        body, name=name, out_shape=[out, out, out, out],
        scratch_shapes=[pltpu.VMEM((n, r), F32)] if transpose else [],
    )(slots, w, m, v)


def _adamw(g, w, m, v, name):
    def body(g_ref, w_ref, m_ref, v_ref, d_ref, nm_ref, nv_ref):
        d_ref[...], nm_ref[...], nv_ref[...] = _adamw_math(w_ref[...], g_ref[...], m_ref[...], v_ref[...])

    out = _sds(w.shape, F32)
    return pl.pallas_call(body, name=name, out_shape=[out, out, out])(g, w, m, v)


_SMALL = ["ffn1_norm", "mix_norm", "ffn2_norm", "mla_q_lat_norm", "mla_kv_lat_norm", "mla_q_nope_gain", "mla_q_rope_gain",
          "mla_k_nope_gain", "mla_k_rope_gain", "fox_q_gain", "fox_k_gain", "fox_b_f"]
_WEIGHTS = ["ffn1_norm", "ffn1_w_gate", "ffn1_w_up", "ffn1_w_down", "mix_norm", "w_in", "mla_q_lat_norm", "mla_w_qb",
            "mla_kv_lat_norm", "mla_w_kvb", "mla_q_nope_gain", "mla_q_rope_gain", "mla_k_nope_gain", "mla_k_rope_gain",
            "fox_q_gain", "fox_k_gain", "fox_b_f", "w_branch_mla", "w_branch_fox", "b_gate", "w_o", "ffn2_norm",
            "ffn2_w_gate", "ffn2_w_up", "ffn2_w_down"]
_IN_Q, _IN_KV, _IN_KR, _IN_FOX, _IN_F, _IN_GATES = (0, 192), (192, 128), (320, 32), (352, 1536), (1888, 8), (1896, 2048)


def _rows(a, seg):
    return a[seg[0]:seg[0] + seg[1]]


def _split_w_in(win_t):
    z = lambda n: jnp.zeros((n, D_MODEL), win_t.dtype)
    lat = jnp.concatenate([_rows(win_t, _IN_Q), z(LAT_KV - Q_LORA), _rows(win_t, _IN_KV), _rows(win_t, _IN_KR),
                           _rows(win_t, _IN_F), z(LAT_W - LAT_F - N_HEADS)], axis=0)
    return _rows(win_t, _IN_GATES), _rows(win_t, _IN_FOX), lat


def _join_w_in(d_gates, d_fox, d_lat):
    return jnp.concatenate([d_lat[LAT_Q:LAT_Q + Q_LORA], d_lat[LAT_KV:LAT_KV + KV_LORA], d_lat[LAT_KR:LAT_KR + ROPE_DIM],
                            d_fox, d_lat[LAT_F:LAT_F + N_HEADS], d_gates], axis=0)


def kernel(x, positions, ffn1_norm, ffn1_w_gate, ffn1_w_up, ffn1_w_down, mix_norm, w_in, mla_q_lat_norm, mla_w_qb, mla_kv_lat_norm, mla_w_kvb, mla_q_nope_gain, mla_q_rope_gain, mla_k_nope_gain, mla_k_rope_gain, fox_q_gain, fox_k_gain, fox_b_f, w_branch_mla, w_branch_fox, b_gate, w_o, ffn2_norm, ffn2_w_gate, ffn2_w_up, ffn2_w_down, loss_target, m_ffn1_norm, m_ffn1_w_gate, m_ffn1_w_up, m_ffn1_w_down, m_mix_norm, m_w_in, m_mla_q_lat_norm, m_mla_w_qb, m_mla_kv_lat_norm, m_mla_w_kvb, m_mla_q_nope_gain, m_mla_q_rope_gain, m_mla_k_nope_gain, m_mla_k_rope_gain, m_fox_q_gain, m_fox_k_gain, m_fox_b_f, m_w_branch_mla, m_w_branch_fox, m_b_gate, m_w_o, m_ffn2_norm, m_ffn2_w_gate, m_ffn2_w_up, m_ffn2_w_down, v_ffn1_norm, v_ffn1_w_gate, v_ffn1_w_up, v_ffn1_w_down, v_mix_norm, v_w_in, v_mla_q_lat_norm, v_mla_w_qb, v_mla_kv_lat_norm, v_mla_w_kvb, v_mla_q_nope_gain, v_mla_q_rope_gain, v_mla_k_nope_gain, v_mla_k_rope_gain, v_fox_q_gain, v_fox_k_gain, v_fox_b_f, v_w_branch_mla, v_w_branch_fox, v_b_gate, v_w_o, v_ffn2_norm, v_ffn2_w_gate, v_ffn2_w_up, v_ffn2_w_down):
    env = dict(locals())
    strip = lambda n, a: a if n in _SMALL else a[0]
    W = {n: strip(n, env[n]) for n in _WEIGHTS}
    M = {n: strip(n, env["m_" + n]) for n in _WEIGHTS}
    V = {n: strip(n, env["v_" + n]) for n in _WEIGHTS}
    xs = x[0]
    t = xs.shape[0]

    col_split = ["ffn1_w_gate", "ffn1_w_up", "ffn2_w_gate", "ffn2_w_up", "mla_w_qb", "mla_w_kvb", "w_branch_mla", "w_branch_fox"]
    row_split = ["ffn1_w_down", "ffn2_w_down", "w_o"]
    pieces = {n: W[n].T.astype(BF16) for n in col_split}
    pieces.update({n: W[n].astype(BF16) for n in row_split})
    pieces["w_in"] = jnp.pad(W["w_in"].T.astype(BF16), ((0, W_IN_PIECE_PAD - W_IN_PIECE), (0, 0)))
    pieces["b_gate"] = W["b_gate"].T
    group_a = ["ffn1_w_gate", "ffn1_w_up", "ffn1_w_down"]
    group_b = ["w_in", "mla_w_qb", "mla_w_kvb", "w_branch_mla", "w_branch_fox", "w_o", "b_gate"]
    group_c = ["ffn2_w_gate", "ffn2_w_up", "ffn2_w_down"]
    gather = lambda group: _ChipExchange(True, [pieces[n] for n in group])
    G = dict(zip(group_a, _gather_over_cores(gather(group_a).standalone("gather_ici_a"), "gather_d2d_a")))

    inv_freq = ROPE_THETA ** (-jnp.arange(ROPE_HALF, dtype=F32) / ROPE_HALF)
    ang = positions[0].astype(F32)[:, None] * inv_freq
    cos, sin = jnp.cos(ang), jnp.sin(ang)
    prep_small = [W[n] for n in ["mla_q_lat_norm", "mla_kv_lat_norm", "mla_q_nope_gain", "mla_q_rope_gain", "mla_k_nope_gain",
                                 "mla_k_rope_gain", "fox_q_gain", "fox_k_gain", "fox_b_f"]]

    (x1, a1, b1), got_b = _ffn_fwd(xs, W["ffn1_norm"], G["ffn1_w_gate"], G["ffn1_w_up"], G["ffn1_w_down"],
                                   exchange=gather(group_b))
    G.update(zip(group_b, _gather_over_cores(got_b, "gather_d2d_b")))
    win_t = G["w_in"].reshape(N_DEV, W_IN_PIECE_PAD, D_MODEL)[:, :W_IN_PIECE].reshape(N_DEV * W_IN_PIECE, D_MODEL)
    wgate_t, wfox_t, wlat_t = _split_w_in(win_t)
    bg = G["b_gate"].T
    gates, fox, lat = _proj_fwd(x1, W["mix_norm"], wgate_t, wfox_t, wlat_t)
    fq, fk, fv, c, mq, mk, mv = _prep_fwd(fox, lat, cos, sin, G["mla_w_qb"], G["mla_w_kvb"], *prep_small)
    c_t = c.T
    b_fox, ok_fox = _logit_bound(fq, fk, lanes=HEAD_DIM, scale=1.0, name="fox_norms")
    b_mla, ok_mla = _logit_bound(mq, mk, lanes=MLA_QK_LANES, scale=MLA_SCALE, name="mla_norms")
    (y_fox, y_fox32, lse_fox), got_c = _attn_fwd(fq, fk, fv, b_fox, ok_fox, c, c_t, lanes=HEAD_DIM, scale=1.0, name="fox_fwd",
                                                 exchange=gather(group_c))
    G.update(zip(group_c, _gather_over_cores(got_c, "gather_d2d_c")))
    (y_mla, y_mla32, lse_mla), _ = _attn_fwd(mq, mk, mv, b_mla, ok_mla, None, None, lanes=MLA_QK_LANES, scale=MLA_SCALE,
                                             name="mla_fwd")
    x2 = _mix_fwd(x1, y_mla, y_fox, gates, bg, G["w_branch_mla"], G["w_branch_fox"], G["w_o"])
    (dx3, a2, b2, loss_vec), _ = _ffn_fwd(x2, W["ffn2_norm"], G["ffn2_w_gate"], G["ffn2_w_up"], G["ffn2_w_down"],
                                          target=loss_target[0])

    def chip_sums(group, tag):
        from_sibling = _grads_to_sibling([grads[n] for n in group], "grads_d2d_" + tag)
        return _ChipExchange(False, [_pair_sum(grads[n], s, "pair_sum_" + n) for n, s in zip(group, from_sibling)])

    (dx2, dg_ffn2, da2, db2, h2, n2, dyh2), _ = _ffn_bwd(dx3, x2, W["ffn2_norm"], a2, b2, G["ffn2_w_gate"], G["ffn2_w_up"],
                                                        G["ffn2_w_down"], "ffn2_bwd")
    grads = {"ffn2_w_gate": _tn_matmul(da2, n2, "ffn2_dgate"), "ffn2_w_up": _tn_matmul(db2, n2, "ffn2_dup"),
             "ffn2_w_down": _tn_matmul(h2, dyh2, "ffn2_ddown")}
    dy_mla, dy_fox, delta_mla, delta_fox, dgates, mixed, dum, duf, dx2b, dbg = _mix_bwd(
        dx2, y_mla, y_fox, y_mla32, y_fox32, gates, bg, G["w_branch_mla"], G["w_branch_fox"], G["w_o"])
    grads["w_o"] = _tn_matmul(mixed, dx2b, "d_w_o")
    grads["w_branch_mla"] = _tn_matmul(dum, y_mla, "d_w_branch_mla")
    grads["w_branch_fox"] = _tn_matmul(duf, y_fox, "d_w_branch_fox")
    (dfq, dfk, dfv, dc_t), slots_c = _attn_bwd(fq, fk, fv, dy_fox, delta_fox, lse_fox, c, c_t, lanes=HEAD_DIM, scale=1.0,
                                               name="fox_bwd", exchange=chip_sums(group_c, "c"))
    slots = dict(zip(group_c, slots_c))
    (dmq, dmk, dmv), _ = _attn_bwd(mq, mk, mv, dy_mla, delta_mla, lse_mla, None, None, lanes=MLA_QK_LANES, scale=MLA_SCALE,
                                   name="mla_bwd")
    prep_out = _prep_bwd(fox, lat, cos, sin, G["mla_w_qb"], G["mla_w_kvb"], *prep_small, dfq, dfk, dfv, dc_t.T, dmq, dmk, dmv)
    dfox, dlat, grads["mla_w_qb"], grads["mla_w_kvb"] = prep_out[:4]
    d_prep_small = prep_out[4:]
    dx1, dg_mix, nmix = _proj_bwd(dgates, dfox, dlat, wgate_t, wfox_t, wlat_t, x1, W["mix_norm"], dx2)
    dwin_t = _join_w_in(_tn_matmul(dgates, nmix, "d_w_in_gates"), _tn_matmul(dfox, nmix, "d_w_in_fox"),
                        _tn_matmul(dlat, nmix, "d_w_in_lat"))
    grads["w_in"] = jnp.pad(dwin_t.reshape(N_DEV, W_IN_PIECE, D_MODEL), ((0, 0), (0, W_IN_PIECE_PAD - W_IN_PIECE), (0, 0))
                            ).reshape(N_DEV * W_IN_PIECE_PAD, D_MODEL)
    grad_group_b = [n for n in group_b if n != "b_gate"]
    (dx0, dg_ffn1, da1, db1, h1, n1, dyh1), slots_b = _ffn_bwd(dx1, xs, W["ffn1_norm"], a1, b1, G["ffn1_w_gate"], G["ffn1_w_up"],
                                                              G["ffn1_w_down"], "ffn1_bwd", exchange=chip_sums(grad_group_b, "b"))
    slots.update(zip(grad_group_b, slots_b))
    grads["ffn1_w_gate"] = _tn_matmul(da1, n1, "ffn1_dgate")
    grads["ffn1_w_up"] = _tn_matmul(db1, n1, "ffn1_dup")
    grads["ffn1_w_down"] = _tn_matmul(h1, dyh1, "ffn1_ddown")
    slots.update(zip(group_a, chip_sums(group_a, "a").standalone("grads_ici_a")))

    small_parts = [dg_ffn1, dg_mix, dg_ffn2] + list(d_prep_small) + [dbg.reshape(1, 2 * D_MODEL), loss_vec]
    flat = jnp.concatenate([p.reshape(-1) for p in small_parts])
    n_flat = flat.shape[0]
    rows = -(-n_flat // (8 * 128)) * 8
    total = _all_reduce_small(jnp.pad(flat, (0, rows * 128 - n_flat)).reshape(rows, 128)).reshape(-1)
    offs, small_g = 0, {}
    for n in _SMALL:
        small_g[n] = total[offs:offs + W[n].shape[1]].reshape(W[n].shape)
        offs += W[n].shape[1]
    bg_full = total[offs:offs + 2 * D_MODEL].reshape(2, D_MODEL)
    offs += 2 * D_MODEL
    loss = (0.5 / D_MODEL) * jnp.sum(total[offs:offs + D_MODEL])
    _, _, _, me = _my_position()
    small_g["b_gate"] = lax.dynamic_slice_in_dim(bg_full, me * (D_MODEL // N_DEV), D_MODEL // N_DEV, axis=1)

    res = {}
    for n in _WEIGHTS:
        if n in small_g:
            res[n] = (small_g[n],) + tuple(_adamw(small_g[n], W[n], M[n], V[n], "adamw_" + n))
        else:
            res[n] = tuple(_reduce_adamw(slots[n], W[n], M[n], V[n], transpose=n in col_split or n == "w_in", name="adamw_" + n))
    outs = [loss, dx0[None]]
    for k in range(4):
        outs += [res[n][k] if n in _SMALL else res[n][k][None] for n in _WEIGHTS]
    return tuple(outs)
```

```python
import functools

import jax
import jax.numpy as jnp
from jax import lax
from jax.experimental import pallas as pl
from jax.experimental.pallas import tpu as pltpu

F32 = jnp.float32
BF16 = jnp.bfloat16

D_MODEL = 1024
FFN_HIDDEN = 2816
N_HEADS = 8
HEAD_DIM = 64
ROPE_DIM = 32
ROPE_HALF = 16
Q_LORA = 192
KV_LORA = 128
ROPE_THETA = 10000.0
RMS_EPS = 1e-6
MLA_SCALE = (HEAD_DIM + ROPE_DIM) ** -0.5
FOX_SCALE = HEAD_DIM ** -0.5
MLA_QK_LANES = 128
ADAM_LR, ADAM_B1, ADAM_B2, ADAM_EPS, ADAM_WD, ADAM_STEP = 0.001, 0.9, 0.999, 1e-08, 0.01, 10
N_DEV = 8
N_CHIP = 4
W_IN_PIECE = 493
W_IN_PIECE_PAD = 496
LAT_W = 512
LAT_Q, LAT_KV, LAT_KR, LAT_F = 0, 256, 384, 416
MASK_VALUE = -1e30
FIXED_SHIFT_MAX_BOUND = 30.0

TOK_TILE = 512
PROJ_TILE = 256
ATT_TILE = 512
ATT_COL_CHUNK = 256
FFN_HID_TILE = 1408
FFN_HID_SPLIT = ((0, 768), (768, 1408))

NT = (((1,), (1,)), ((), ()))
TN = (((0,), (0,)), ((), ()))
NN = (((1,), (0,)), ((), ()))
MESH = pl.DeviceIdType.MESH


def _dot(a, b, dims=NN):
    return lax.dot_general(a, b, dims, preferred_element_type=F32)


def _sds(shape, dtype):
    return jax.ShapeDtypeStruct(shape, dtype)


def _rms_fwd(x, g):
    r = lax.rsqrt(jnp.mean(x * x, axis=-1, keepdims=True) + RMS_EPS)
    return x * r * g, r


def _rms_bwd(dy, x, g, r):
    xn = x * r
    dyg = dy * g
    dx = r * (dyg - xn * jnp.mean(dyg * xn, axis=-1, keepdims=True))
    return dx, dy * xn


def _colsum(x):
    return jnp.sum(x, axis=0, keepdims=True)


def _full(shape):
    return pl.BlockSpec(shape, lambda *_: (0,) * len(shape))


def _tok(tm, n):
    return pl.BlockSpec((tm, n), lambda i, *_: (i, 0))


def _ffn_fwd(x, gain, wg_t, wu_t, wd, target=None, exchange=None):
    t = x.shape[0]
    tm = min(TOK_TILE, t)
    tf = FFN_HID_TILE
    n_t, n_f = t // tm, FFN_HIDDEN // tf
    with_loss = target is not None

    def body(*refs):
        if with_loss:
            x_ref, g_ref, wg_ref, wu_ref, wd_ref, t_ref, out_ref, a_ref, b_ref, lvec_ref, n_scr, acc = refs
        else:
            x_ref, g_ref, wg_ref, wu_ref, wd_ref, out_ref, a_ref, b_ref, n_scr, acc = refs
        i, j = pl.program_id(0), pl.program_id(1)

        @pl.when(j == 0)
        def _():
            xn, _ = _rms_fwd(x_ref[...], g_ref[...])
            n_scr[...] = xn.astype(BF16)
            acc[...] = jnp.zeros_like(acc)

        n = n_scr[...]
        a = _dot(n, wg_ref[...], NT)
        b = _dot(n, wu_ref[...], NT)
        a_ref[...] = a.astype(BF16)
        b_ref[...] = b.astype(BF16)
        h = (a * jax.nn.sigmoid(a)) * b
        acc[...] += _dot(h.astype(BF16), wd_ref[...])

        @pl.when(j == n_f - 1)
        def _():
            y = x_ref[...] + 0.5 * acc[...]
            if with_loss:
                diff = y - t_ref[...]
                out_ref[...] = diff * (1.0 / D_MODEL)
                sq = _colsum(diff * diff)

                @pl.when(i == 0)
                def _():
                    lvec_ref[...] = sq

                @pl.when(i > 0)
                def _():
                    lvec_ref[...] += sq
            else:
                out_ref[...] = y

    wspec = pl.BlockSpec((tf, D_MODEL), lambda i, j: (j, 0))
    hspec = pl.BlockSpec((tm, tf), lambda i, j: (i, j))
    in_specs = [_tok(tm, D_MODEL), _full((1, D_MODEL)), wspec, wspec, wspec]
    out_specs = [_tok(tm, D_MODEL), hspec, hspec]
    out_shape = [_sds((t, D_MODEL), F32), _sds((t, FFN_HIDDEN), BF16), _sds((t, FFN_HIDDEN), BF16)]
    args = [x, gain, wg_t, wu_t, wd]
    if with_loss:
        in_specs.append(_tok(tm, D_MODEL))
        out_specs.append(_full((1, D_MODEL)))
        out_shape.append(_sds((1, D_MODEL), F32))
        args.append(target)
    return _gridded_call(
        body, name="ffn_fwd_loss" if with_loss else "ffn_fwd", grid=(n_t, n_f), in_specs=in_specs, out_specs=out_specs,
        out_shape=out_shape, scratch_shapes=[pltpu.VMEM((tm, D_MODEL), BF16), pltpu.VMEM((tm, D_MODEL), F32)],
        args=args, exchange=exchange)


def _ffn_bwd(dy, x, gain, a, b, wg_t, wu_t, wd, name, exchange=None):
    t = x.shape[0]
    tm = min(TOK_TILE, t)
    tf = FFN_HID_TILE
    n_t, n_f = t // tm, FFN_HIDDEN // tf

    def body(dy_ref, x_ref, g_ref, a_ref, b_ref, wg_ref, wu_ref, wd_ref,
             dx_ref, dg_ref, da_ref, db_ref, h_ref, n_ref, dyh_ref, acc):
        i, j = pl.program_id(0), pl.program_id(1)

        @pl.when(j == 0)
        def _():
            xn, _ = _rms_fwd(x_ref[...], g_ref[...])
            n_ref[...] = xn.astype(BF16)
            dyh_ref[...] = (0.5 * dy_ref[...]).astype(BF16)
            acc[...] = jnp.zeros_like(acc)

        dyh = dyh_ref[...]
        for lo, hi in FFN_HID_SPLIT:
            dh = _dot(dyh, wd_ref[lo:hi, :], NT)
            av = a_ref[:, lo:hi].astype(F32)
            bv = b_ref[:, lo:hi].astype(F32)
            s = jax.nn.sigmoid(av)
            silu = av * s
            da = (dh * bv * (s * (1.0 + av * (1.0 - s)))).astype(BF16)
            db = (dh * silu).astype(BF16)
            da_ref[:, lo:hi] = da
            db_ref[:, lo:hi] = db
            h_ref[:, lo:hi] = (silu * bv).astype(BF16)
            acc[...] += _dot(da, wg_ref[lo:hi, :]) + _dot(db, wu_ref[lo:hi, :])

        @pl.when(j == n_f - 1)
        def _():
            xv, g = x_ref[...], g_ref[...]
            r = lax.rsqrt(jnp.mean(xv * xv, axis=-1, keepdims=True) + RMS_EPS)
            dx, dg_rows = _rms_bwd(acc[...], xv, g, r)
            dx_ref[...] = dy_ref[...] + dx
            dg = _colsum(dg_rows)

            @pl.when(i == 0)
            def _():
                dg_ref[...] = dg

            @pl.when(i > 0)
            def _():
                dg_ref[...] += dg

    wspec = pl.BlockSpec((tf, D_MODEL), lambda i, j: (j, 0))
    hspec = pl.BlockSpec((tm, tf), lambda i, j: (i, j))
    tok = _tok(tm, D_MODEL)
    hid = _sds((t, FFN_HIDDEN), BF16)
    return _gridded_call(
        body, name=name, grid=(n_t, n_f),
        in_specs=[tok, tok, _full((1, D_MODEL)), hspec, hspec, wspec, wspec, wspec],
        out_specs=[tok, _full((1, D_MODEL)), hspec, hspec, hspec, tok, tok],
        out_shape=[_sds((t, D_MODEL), F32), _sds((1, D_MODEL), F32), hid, hid, hid,
                   _sds((t, D_MODEL), BF16), _sds((t, D_MODEL), BF16)],
        scratch_shapes=[pltpu.VMEM((tm, D_MODEL), F32)], args=[dy, x, gain, a, b, wg_t, wu_t, wd], exchange=exchange)


def _tn_matmul(a, b, name, exchange=None):
    t, m = a.shape
    n = b.shape[1]
    tk = min(TOK_TILE, t)
    n_k = t // tk

    def body(a_ref, b_ref, o_ref, acc):
        k = pl.program_id(0)
        p = _dot(a_ref[...], b_ref[...], TN)

        @pl.when(k == 0)
        def _():
            acc[...] = p

        @pl.when(k > 0)
        def _():
            acc[...] += p

        @pl.when(k == n_k - 1)
        def _():
            o_ref[...] = acc[...].astype(BF16)

    (out,), got = _gridded_call(
        body, name=name, grid=(n_k,), in_specs=[_tok(tk, m), _tok(tk, n)], out_specs=[_full((m, n))],
        out_shape=[_sds((m, n), BF16)], scratch_shapes=[pltpu.VMEM((m, n), F32)], args=[a, b], exchange=exchange)
    return out if exchange is None else (out, got)


def _proj_fwd(x, gain, wgate_t, wfox_t, wlat_t):
    t = x.shape[0]
    tm = min(PROJ_TILE, t)

    def body(x_ref, g_ref, wg_ref, wf_ref, wl_ref, og_ref, of_ref, ol_ref):
        xn, _ = _rms_fwd(x_ref[...], g_ref[...])
        n = xn.astype(BF16)
        og_ref[...] = _dot(n, wg_ref[...], NT)
        of_ref[...] = _dot(n, wf_ref[...], NT)
        ol_ref[...] = _dot(n, wl_ref[...], NT)

    return pl.pallas_call(
        body, name="proj_fwd", grid=(t // tm,),
        in_specs=[_tok(tm, D_MODEL), _full((1, D_MODEL)), _full(wgate_t.shape), _full(wfox_t.shape), _full(wlat_t.shape)],
        out_specs=[_tok(tm, 2 * D_MODEL), _tok(tm, 3 * 512), _tok(tm, LAT_W)],
        out_shape=[_sds((t, 2 * D_MODEL), F32), _sds((t, 3 * 512), F32), _sds((t, LAT_W), F32)],
        compiler_params=pltpu.CompilerParams(dimension_semantics=("arbitrary",)),
    )(x, gain, wgate_t, wfox_t, wlat_t)


def _proj_bwd(dgates, dfox, dlat, wgate_t, wfox_t, wlat_t, x, gain, dres):
    t = x.shape[0]
    tm = min(PROJ_TILE, t)

    def body(dg_ref, df_ref, dl_ref, wg_ref, wf_ref, wl_ref, x_ref, g_ref, dres_ref, dx_ref, dgain_ref, n_ref):
        i = pl.program_id(0)
        dn = _dot(dg_ref[...], wg_ref[...]) + _dot(df_ref[...], wf_ref[...]) + _dot(dl_ref[...], wl_ref[...])
        xv, g = x_ref[...], g_ref[...]
        xn, r = _rms_fwd(xv, g)
        n_ref[...] = xn.astype(BF16)
        dx, dg_rows = _rms_bwd(dn, xv, g, r)
        dx_ref[...] = dres_ref[...] + dx
        dgn = _colsum(dg_rows)

        @pl.when(i == 0)
        def _():
            dgain_ref[...] = dgn

        @pl.when(i > 0)
        def _():
            dgain_ref[...] += dgn

    tok = _tok(tm, D_MODEL)
    return pl.pallas_call(
        body, name="proj_bwd", grid=(t // tm,),
        in_specs=[_tok(tm, 2 * D_MODEL), _tok(tm, 3 * 512), _tok(tm, LAT_W), _full(wgate_t.shape), _full(wfox_t.shape),
                  _full(wlat_t.shape), tok, _full((1, D_MODEL)), tok],
        out_specs=[tok, _full((1, D_MODEL)), tok],
        out_shape=[_sds((t, D_MODEL), F32), _sds((1, D_MODEL), F32), _sds((t, D_MODEL), BF16)],
        compiler_params=pltpu.CompilerParams(dimension_semantics=("arbitrary",)),
    )(dgates, dfox, dlat, wgate_t, wfox_t, wlat_t, x, gain, dres)


def _tri(n, lower):
    r = lax.broadcasted_iota(jnp.int32, (n, n), 0)
    c = lax.broadcasted_iota(jnp.int32, (n, n), 1)
    return ((c <= r) if lower else (c >= r)).astype(F32)


def _log_sigmoid(z):
    return jnp.minimum(z, 0.0) - jnp.log1p(jnp.exp(-jnp.abs(z)))


def _rope_pair_fwd(p1, p2, g1, g2, cos, sin):
    ss = jnp.sum(p1 * p1, axis=-1, keepdims=True) + jnp.sum(p2 * p2, axis=-1, keepdims=True)
    r = lax.rsqrt(ss * (1.0 / ROPE_DIM) + RMS_EPS)
    n1, n2 = p1 * r, p2 * r
    y1, y2 = n1 * g1, n2 * g2
    return y1 * cos - y2 * sin, y2 * cos + y1 * sin, n1, n2, r


def _rope_pair_bwd(do1, do2, n1, n2, r, g1, g2, cos, sin):
    dy1 = do1 * cos + do2 * sin
    dy2 = do2 * cos - do1 * sin
    dyg1, dyg2 = dy1 * g1, dy2 * g2
    mu = (jnp.sum(dyg1 * n1, axis=-1, keepdims=True) + jnp.sum(dyg2 * n2, axis=-1, keepdims=True)) * (1.0 / ROPE_DIM)
    return r * (dyg1 - n1 * mu), r * (dyg2 - n2 * mu), dy1 * n1, dy2 * n2


def _prep_fwd(fox, lat, cos, sin, wqb_t, wkvb_t, g_qlat, g_kvlat, g_qn, g_qr, g_kn, g_kr, g_fq, g_fk, b_f):
    t = fox.shape[0]
    tm = min(PROJ_TILE, t)
    hd, lanes = HEAD_DIM, MLA_QK_LANES

    def body(fox_ref, lat_ref, cos_ref, sin_ref, wqb_ref, wkvb_ref, gql_ref, gkvl_ref, gqn_ref, gqr_ref, gkn_ref,
             gkr_ref, gfq_ref, gfk_ref, bf_ref, fq_ref, fk_ref, fv_ref, c_ref, mq_ref, mk_ref, mv_ref,
             carry, qscr, kscr):
        i = pl.program_id(0)
        cos_v, sin_v = cos_ref[...], sin_ref[...]

        for h in range(N_HEADS):
            w = slice(h * hd, (h + 1) * hd)
            qn, _ = _rms_fwd(fox_ref[:, h * hd:(h + 1) * hd], gfq_ref[...])
            fq_ref[:, w] = (qn * FOX_SCALE).astype(BF16)
            kn, _ = _rms_fwd(fox_ref[:, 512 + h * hd:512 + (h + 1) * hd], gfk_ref[...])
            fk_ref[:, w] = kn.astype(BF16)
        fv_ref[...] = fox_ref[:, 1024:1536].astype(BF16)

        @pl.when(i == 0)
        def _():
            carry[...] = jnp.zeros_like(carry)

        logf = _log_sigmoid(lat_ref[:, LAT_F:LAT_F + N_HEADS] + bf_ref[...])
        c_ref[...] = jnp.dot(_tri(tm, True), logf, precision=lax.Precision.HIGHEST, preferred_element_type=F32) + carry[...]
        carry[...] += _colsum(logf)

        qlat_n, _ = _rms_fwd(lat_ref[:, LAT_Q:LAT_Q + Q_LORA], gql_ref[...])
        qlat_b = qlat_n.astype(BF16)
        g1, g2 = gqr_ref[:, 0:ROPE_HALF], gqr_ref[:, ROPE_HALF:ROPE_DIM]
        qscr[...] = jnp.zeros_like(qscr)
        for h in range(N_HEADS):
            base = h * (hd + ROPE_DIM)
            pn = _dot(qlat_b, wqb_ref[base:base + hd, :], NT)
            p1 = _dot(qlat_b, wqb_ref[base + hd:base + hd + ROPE_HALF, :], NT)
            p2 = _dot(qlat_b, wqb_ref[base + hd + ROPE_HALF:base + hd + ROPE_DIM, :], NT)
            nope, _ = _rms_fwd(pn, gqn_ref[...])
            o1, o2, _, _, _ = _rope_pair_fwd(p1, p2, g1, g2, cos_v, sin_v)
            qscr[:, h * lanes:h * lanes + hd] = nope
            qscr[:, h * lanes + hd:h * lanes + hd + ROPE_HALF] = o1
            qscr[:, h * lanes + hd + ROPE_HALF:h * lanes + hd + ROPE_DIM] = o2
        mq_ref[...] = qscr[...].astype(BF16)

        kv_n, _ = _rms_fwd(lat_ref[:, LAT_KV:LAT_KV + KV_LORA], gkvl_ref[...])
        kv_b = kv_n.astype(BF16)
        kr1, kr2, _, _, _ = _rope_pair_fwd(lat_ref[:, LAT_KR:LAT_KR + ROPE_HALF], lat_ref[:, LAT_KR + ROPE_HALF:LAT_KR + ROPE_DIM],
                                           gkr_ref[:, 0:ROPE_HALF], gkr_ref[:, ROPE_HALF:ROPE_DIM], cos_v, sin_v)
        kscr[...] = jnp.zeros_like(kscr)
        for h in range(N_HEADS):
            pk = _dot(kv_b, wkvb_ref[h * 2 * hd:h * 2 * hd + hd, :], NT)
            kn, _ = _rms_fwd(pk, gkn_ref[...])
            kscr[:, h * lanes:h * lanes + hd] = kn
            kscr[:, h * lanes + hd:h * lanes + hd + ROPE_HALF] = kr1
            kscr[:, h * lanes + hd + ROPE_HALF:h * lanes + hd + ROPE_DIM] = kr2
            mv_ref[:, h * hd:(h + 1) * hd] = _dot(kv_b, wkvb_ref[h * 2 * hd + hd:(h + 1) * 2 * hd, :], NT).astype(BF16)
        mk_ref[...] = kscr[...].astype(BF16)

    small = [g_qlat, g_kvlat, g_qn, g_qr, g_kn, g_kr, g_fq, g_fk, b_f]
    t512 = _tok(tm, 512)
    return pl.pallas_call(
        body, name="prep_fwd", grid=(t // tm,),
        in_specs=[_tok(tm, 1536), _tok(tm, LAT_W), _tok(tm, ROPE_HALF), _tok(tm, ROPE_HALF), _full(wqb_t.shape),
                  _full(wkvb_t.shape)] + [_full(s.shape) for s in small],
        out_specs=[t512, t512, t512, _tok(tm, N_HEADS), _tok(tm, N_HEADS * lanes), _tok(tm, N_HEADS * lanes), t512],
        out_shape=[_sds((t, 512), BF16), _sds((t, 512), BF16), _sds((t, 512), BF16), _sds((t, N_HEADS), F32),
                   _sds((t, N_HEADS * lanes), BF16), _sds((t, N_HEADS * lanes), BF16), _sds((t, 512), BF16)],
        scratch_shapes=[pltpu.VMEM((1, N_HEADS), F32), pltpu.VMEM((tm, N_HEADS * lanes), F32),
                        pltpu.VMEM((tm, N_HEADS * lanes), F32)],
        compiler_params=pltpu.CompilerParams(dimension_semantics=("arbitrary",)),
    )(fox, lat, cos, sin, wqb_t, wkvb_t, *small)


def _prep_bwd(fox, lat, cos, sin, wqb_t, wkvb_t, g_qlat, g_kvlat, g_qn, g_qr, g_kn, g_kr, g_fq, g_fk, b_f,
              dfq, dfk, dfv, dc, dmq, dmk, dmv):
    t = fox.shape[0]
    tm = min(PROJ_TILE, t)
    n_t = t // tm
    hd, lanes = HEAD_DIM, MLA_QK_LANES

    def body(fox_ref, lat_ref, cos_ref, sin_ref, wqb_ref, wkvb_ref, gql_ref, gkvl_ref, gqn_ref, gqr_ref, gkn_ref,
             gkr_ref, gfq_ref, gfk_ref, bf_ref, dfq_ref, dfk_ref, dfv_ref, dc_ref, dmq_ref, dmk_ref, dmv_ref,
             dfox_ref, dlat_ref, dwqb_ref, dwkvb_ref, o_gql, o_gkvl, o_gqn, o_gqr, o_gkn, o_gkr, o_gfq, o_gfk, o_bf,
             carry, lscr, wq_acc, wkv_acc, pscr, dscr):
        i = pl.program_id(0)
        cos_v, sin_v = cos_ref[...], sin_ref[...]
        small_out = [o_gql, o_gkvl, o_gqn, o_gqr, o_gkn, o_gkr, o_gfq, o_gfk, o_bf]

        @pl.when(i == 0)
        def _():
            carry[...] = jnp.zeros_like(carry)
            wq_acc[...] = jnp.zeros_like(wq_acc)
            wkv_acc[...] = jnp.zeros_like(wkv_acc)
            for o in small_out:
                o[...] = jnp.zeros_like(o)

        d_gfq = jnp.zeros((1, hd), F32)
        d_gfk = jnp.zeros((1, hd), F32)
        for h in range(N_HEADS):
            w = slice(h * hd, (h + 1) * hd)
            xq = fox_ref[:, h * hd:(h + 1) * hd]
            rq = lax.rsqrt(jnp.mean(xq * xq, axis=-1, keepdims=True) + RMS_EPS)
            dxq, gq_rows = _rms_bwd(dfq_ref[:, w] * FOX_SCALE, xq, gfq_ref[...], rq)
            dfox_ref[:, w] = dxq.astype(BF16)
            d_gfq += _colsum(gq_rows)
            xk = fox_ref[:, 512 + h * hd:512 + (h + 1) * hd]
            rk = lax.rsqrt(jnp.mean(xk * xk, axis=-1, keepdims=True) + RMS_EPS)
            dxk, gk_rows = _rms_bwd(dfk_ref[:, w], xk, gfk_ref[...], rk)
            dfox_ref[:, 512 + h * hd:512 + (h + 1) * hd] = dxk.astype(BF16)
            d_gfk += _colsum(gk_rows)
        dfox_ref[:, 1024:1536] = dfv_ref[...].astype(BF16)
        o_gfq[...] += d_gfq
        o_gfk[...] += d_gfk

        lscr[...] = jnp.zeros_like(lscr)

        dcv = dc_ref[...]
        dlogf = jnp.dot(_tri(tm, False), dcv, precision=lax.Precision.HIGHEST, preferred_element_type=F32) + carry[...]
        carry[...] += _colsum(dcv)
        z = lat_ref[:, LAT_F:LAT_F + N_HEADS] + bf_ref[...]
        dz = dlogf * jax.nn.sigmoid(-z)
        lscr[:, LAT_F:LAT_F + N_HEADS] = dz
        o_bf[...] += _colsum(dz)

        xql = lat_ref[:, LAT_Q:LAT_Q + Q_LORA]
        qlat_n, r_ql = _rms_fwd(xql, gql_ref[...])
        qlat_b = qlat_n.astype(BF16)
        g1, g2 = gqr_ref[:, 0:ROPE_HALF], gqr_ref[:, ROPE_HALF:ROPE_DIM]
        d_gqn = jnp.zeros((1, hd), F32)
        d_g1 = jnp.zeros((1, ROPE_HALF), F32)
        d_g2 = jnp.zeros((1, ROPE_HALF), F32)
        n_q = N_HEADS * (hd + ROPE_DIM)
        pscr[:, 0:n_q] = _dot(qlat_b, wqb_ref[...], NT)
        for h in range(N_HEADS):
            base = h * (hd + ROPE_DIM)
            rows_n = slice(base, base + hd)
            rows_1 = slice(base + hd, base + hd + ROPE_HALF)
            rows_2 = slice(base + hd + ROPE_HALF, base + hd + ROPE_DIM)
            pn, p1, p2 = pscr[:, rows_n], pscr[:, rows_1], pscr[:, rows_2]
            r_n = lax.rsqrt(jnp.mean(pn * pn, axis=-1, keepdims=True) + RMS_EPS)
            _, _, n1, n2, r_r = _rope_pair_fwd(p1, p2, g1, g2, cos_v, sin_v)
            dpn, gn_rows = _rms_bwd(dmq_ref[:, h * lanes:h * lanes + hd], pn, gqn_ref[...], r_n)
            dp1, dp2, g1_rows, g2_rows = _rope_pair_bwd(
                dmq_ref[:, h * lanes + hd:h * lanes + hd + ROPE_HALF],
                dmq_ref[:, h * lanes + hd + ROPE_HALF:h * lanes + hd + ROPE_DIM], n1, n2, r_r, g1, g2, cos_v, sin_v)
            d_gqn += _colsum(gn_rows)
            d_g1 += _colsum(g1_rows)
            d_g2 += _colsum(g2_rows)
            dscr[:, rows_n], dscr[:, rows_1], dscr[:, rows_2] = dpn, dp1, dp2
        dp_b = dscr[:, 0:n_q].astype(BF16)
        wq_acc[...] += _dot(dp_b, qlat_b, TN)
        dxql, gql_rows = _rms_bwd(_dot(dp_b, wqb_ref[...]), xql, gql_ref[...], r_ql)
        lscr[:, LAT_Q:LAT_Q + Q_LORA] = dxql
        o_gql[...] += _colsum(gql_rows)
        o_gqn[...] += d_gqn
        o_gqr[:, 0:ROPE_HALF] += d_g1
        o_gqr[:, ROPE_HALF:ROPE_DIM] += d_g2

        xkv = lat_ref[:, LAT_KV:LAT_KV + KV_LORA]
        kv_n, r_kv = _rms_fwd(xkv, gkvl_ref[...])
        kv_b = kv_n.astype(BF16)
        d_gkn = jnp.zeros((1, hd), F32)
        dkr1 = jnp.zeros((tm, ROPE_HALF), F32)
        dkr2 = jnp.zeros((tm, ROPE_HALF), F32)
        pscr[...] = _dot(kv_b, wkvb_ref[...], NT)
        for h in range(N_HEADS):
            rows_k = slice(h * 2 * hd, h * 2 * hd + hd)
            rows_v = slice(h * 2 * hd + hd, (h + 1) * 2 * hd)
            pk = pscr[:, rows_k]
            r_k = lax.rsqrt(jnp.mean(pk * pk, axis=-1, keepdims=True) + RMS_EPS)
            dpk, gk_rows = _rms_bwd(dmk_ref[:, h * lanes:h * lanes + hd], pk, gkn_ref[...], r_k)
            d_gkn += _colsum(gk_rows)
            dkr1 += dmk_ref[:, h * lanes + hd:h * lanes + hd + ROPE_HALF]
            dkr2 += dmk_ref[:, h * lanes + hd + ROPE_HALF:h * lanes + hd + ROPE_DIM]
            dscr[:, rows_k] = dpk
            dscr[:, rows_v] = dmv_ref[:, h * hd:(h + 1) * hd]
        dkv_b = dscr[...].astype(BF16)
        wkv_acc[...] += _dot(dkv_b, kv_b, TN)
        dxkv, gkvl_rows = _rms_bwd(_dot(dkv_b, wkvb_ref[...]), xkv, gkvl_ref[...], r_kv)
        lscr[:, LAT_KV:LAT_KV + KV_LORA] = dxkv
        o_gkvl[...] += _colsum(gkvl_rows)
        o_gkn[...] += d_gkn

        gk1, gk2 = gkr_ref[:, 0:ROPE_HALF], gkr_ref[:, ROPE_HALF:ROPE_DIM]
        _, _, kn1, kn2, r_kr = _rope_pair_fwd(lat_ref[:, LAT_KR:LAT_KR + ROPE_HALF],
                                              lat_ref[:, LAT_KR + ROPE_HALF:LAT_KR + ROPE_DIM], gk1, gk2, cos_v, sin_v)
        dk1, dk2, gk1_rows, gk2_rows = _rope_pair_bwd(dkr1, dkr2, kn1, kn2, r_kr, gk1, gk2, cos_v, sin_v)
        lscr[:, LAT_KR:LAT_KR + ROPE_HALF] = dk1
        lscr[:, LAT_KR + ROPE_HALF:LAT_KR + ROPE_DIM] = dk2
        o_gkr[:, 0:ROPE_HALF] += _colsum(gk1_rows)
        o_gkr[:, ROPE_HALF:ROPE_DIM] += _colsum(gk2_rows)

        dlat_ref[...] = lscr[...].astype(BF16)

        @pl.when(i == n_t - 1)
        def _():
            dwqb_ref[...] = wq_acc[...].astype(BF16)
            dwkvb_ref[...] = wkv_acc[...].astype(BF16)

    small = [g_qlat, g_kvlat, g_qn, g_qr, g_kn, g_kr, g_fq, g_fk, b_f]

    def rtok(n):
        return pl.BlockSpec((tm, n), lambda i: (n_t - 1 - i, 0))

    return pl.pallas_call(
        body, name="prep_bwd", grid=(n_t,),
        in_specs=[rtok(1536), rtok(LAT_W), rtok(ROPE_HALF), rtok(ROPE_HALF), _full(wqb_t.shape), _full(wkvb_t.shape)]
        + [_full(s.shape) for s in small]
        + [rtok(512), rtok(512), rtok(512), rtok(N_HEADS), rtok(N_HEADS * lanes), rtok(N_HEADS * lanes), rtok(512)],
        out_specs=[rtok(1536), rtok(LAT_W), _full(wqb_t.shape), _full(wkvb_t.shape)] + [_full(s.shape) for s in small],
        out_shape=[_sds((t, 1536), BF16), _sds((t, LAT_W), BF16), _sds(wqb_t.shape, BF16), _sds(wkvb_t.shape, BF16)]
        + [_sds(s.shape, F32) for s in small],
        scratch_shapes=[pltpu.VMEM((1, N_HEADS), F32), pltpu.VMEM((tm, LAT_W), F32), pltpu.VMEM(wqb_t.shape, F32),
                        pltpu.VMEM(wkvb_t.shape, F32), pltpu.VMEM((tm, N_HEADS * 2 * hd), F32),
                        pltpu.VMEM((tm, N_HEADS * 2 * hd), F32)],
        compiler_params=pltpu.CompilerParams(dimension_semantics=("arbitrary",)),
    )(fox, lat, cos, sin, wqb_t, wkvb_t, *small, dfq, dfk, dfv, dc, dmq, dmk, dmv)


def _qk_norms(q, k, *, lanes, name):
    t = q.shape[0]
    tm = min(TOK_TILE, t)

    def body(q_ref, k_ref, qn_ref, kmax_ref):
        i = pl.program_id(0)

        @pl.when(i == 0)
        def _():
            kmax_ref[...] = jnp.zeros_like(kmax_ref)

        for h in range(N_HEADS):
            qh = q_ref[:, h * lanes:(h + 1) * lanes].astype(F32)
            kh = k_ref[:, h * lanes:(h + 1) * lanes].astype(F32)
            qn_ref[:, h:h + 1] = jnp.sum(qh * qh, axis=1, keepdims=True)
            kmax = jnp.max(jnp.sum(kh * kh, axis=1, keepdims=True), axis=0, keepdims=True)
            kmax_ref[:, h:h + 1] = jnp.maximum(kmax_ref[:, h:h + 1], kmax)

    return pl.pallas_call(
        body, name=name, grid=(t // tm,), in_specs=[_tok(tm, N_HEADS * lanes), _tok(tm, N_HEADS * lanes)],
        out_specs=[_tok(tm, N_HEADS), _full((1, N_HEADS))], out_shape=[_sds((t, N_HEADS), F32), _sds((1, N_HEADS), F32)],
        compiler_params=pltpu.CompilerParams(dimension_semantics=("arbitrary",)),
    )(q, k)


def _logit_bound(q, k, *, lanes, scale, name):
    qn, kmax = _qk_norms(q, k, lanes=lanes, name=name)
    bound = jnp.sqrt(qn * kmax) * (scale * (1.0 + 2.0 ** -10)) + 2.0 ** -10
    flag = (jnp.max(bound) <= FIXED_SHIFT_MAX_BOUND).astype(F32).reshape(1, 1)
    return bound, flag


def _attn_fwd(q, k, v, bound, fixed_ok, c, c_t, *, lanes, scale, name, exchange=None):
    t = q.shape[0]
    tq = min(ATT_TILE, t)
    n_q = t // tq
    hd = HEAD_DIM
    ch = min(ATT_COL_CHUNK, tq)
    decay = c is not None

    def body(*refs):
        if decay:
            q_ref, k_ref, v_ref, b_ref, ok_ref, c_ref, ct_ref, o_ref, o32_ref, lse_ref, m_scr, l_scr, acc = refs
        else:
            q_ref, k_ref, v_ref, b_ref, ok_ref, o_ref, o32_ref, lse_ref, m_scr, l_scr, acc = refs
        i, j = pl.program_id(0), pl.program_id(1)
        fixed = ok_ref[0, 0] > 0.5

        @pl.when(j == 0)
        def _():
            m_scr[...] = jnp.full_like(m_scr, MASK_VALUE)
            l_scr[...] = jnp.zeros_like(l_scr)
            acc[...] = jnp.zeros_like(acc)

        def fixed_step(diagonal):
            for h in range(N_HEADS):
                wl = slice(h * lanes, (h + 1) * lanes)
                w = slice(h * hd, (h + 1) * hd)
                qh = q_ref[:, wl]
                row = (c_ref[:, h:h + 1] - b_ref[:, h:h + 1]) if decay else -b_ref[:, h:h + 1]
                l_new = jnp.zeros((tq, 1), F32)
                o_hi = jnp.zeros((tq, hd), F32)
                o_lo = jnp.zeros((tq, hd), F32)
                for cc in range(tq // ch):
                    cols = slice(cc * ch, (cc + 1) * ch)
                    s = _dot(qh, k_ref[cols, wl], NT)
                    if scale != 1.0:
                        s = s * scale
                    s = s + ((row - ct_ref[h:h + 1, cols]) if decay else row)
                    if diagonal:
                        keep = (lax.broadcasted_iota(jnp.int32, (tq, ch), 0)
                                >= lax.broadcasted_iota(jnp.int32, (tq, ch), 1) + cc * ch)
                        s = jnp.where(keep, s, MASK_VALUE)
                    p = jnp.exp(s)
                    l_new = l_new + jnp.sum(p, axis=1, keepdims=True)
                    p_b = p.astype(BF16)
                    o_hi = o_hi + _dot(p_b, v_ref[cols, w])
                    if decay:
                        o_lo = o_lo + _dot((p - p_b.astype(F32)).astype(BF16), v_ref[cols, w])
                l_scr[h] += l_new
                acc[0, :, w] += o_hi
                if decay:
                    acc[1, :, w] += o_lo

        def step(diagonal):
            if diagonal:
                keep = lax.broadcasted_iota(jnp.int32, (tq, tq), 0) >= lax.broadcasted_iota(jnp.int32, (tq, tq), 1)
            for h in range(N_HEADS):
                s = _dot(q_ref[:, h * lanes:(h + 1) * lanes], k_ref[:, h * lanes:(h + 1) * lanes], NT)
                if scale != 1.0:
                    s = s * scale
                if decay:
                    s = s + (c_ref[:, h:h + 1] - ct_ref[h:h + 1, :])
                if diagonal:
                    s = jnp.where(keep, s, MASK_VALUE)
                m_prev = m_scr[h]
                m_new = jnp.maximum(m_prev, jnp.max(s, axis=1, keepdims=True))
                alpha = jnp.exp(m_prev - m_new)
                p = jnp.exp(s - m_new)
                l_scr[h] = alpha * l_scr[h] + jnp.sum(p, axis=1, keepdims=True)
                w = slice(h * hd, (h + 1) * hd)
                p_b = p.astype(BF16)
                acc[0, :, w] = alpha * acc[0, :, w] + _dot(p_b, v_ref[:, w])
                if decay:
                    p_lo = (p - p_b.astype(F32)).astype(BF16)
                    acc[1, :, w] = alpha * acc[1, :, w] + _dot(p_lo, v_ref[:, w])
                m_scr[h] = m_new

        for diagonal, here in ((False, j < i), (True, j == i)):
            @pl.when(here & fixed)
            def _():
                fixed_step(diagonal)

            @pl.when(here & jnp.logical_not(fixed))
            def _():
                step(diagonal)

        @pl.when(j == i)
        def _():
            for h in range(N_HEADS):
                w = slice(h * hd, (h + 1) * hd)
                l = l_scr[h]
                o_ref[:, w] = (acc[0, :, w] / l).astype(BF16)
                o32_ref[:, w] = ((acc[0, :, w] + acc[1, :, w]) if decay else acc[0, :, w]) / l
                lse_ref[:, h:h + 1] = jnp.where(fixed, b_ref[:, h:h + 1], m_scr[h]) + jnp.log(l)

    qspec = lambda n: pl.BlockSpec((tq, n), lambda i, j: (i, 0))
    kspec = lambda n: pl.BlockSpec((tq, n), lambda i, j: (jnp.minimum(i, j), 0))
    in_specs = [qspec(N_HEADS * lanes), kspec(N_HEADS * lanes), kspec(512), qspec(N_HEADS),
                pl.BlockSpec(memory_space=pltpu.SMEM)]
    args = [q, k, v, bound, fixed_ok]
    if decay:
        in_specs += [qspec(N_HEADS), pl.BlockSpec((N_HEADS, tq), lambda i, j: (0, jnp.minimum(i, j)))]
        args += [c, c_t]
    return _gridded_call(
        body, name=name, grid=(n_q, n_q), in_specs=in_specs, out_specs=[qspec(512), qspec(512), qspec(N_HEADS)],
        out_shape=[_sds((t, 512), BF16), _sds((t, 512), F32), _sds((t, N_HEADS), F32)],
        scratch_shapes=[pltpu.VMEM((N_HEADS, tq, 1), F32), pltpu.VMEM((N_HEADS, tq, 1), F32), pltpu.VMEM((2, tq, 512), F32)],
        args=args, exchange=exchange)


def _attn_bwd(q, k, v, do, delta, lse, c, c_t, *, lanes, scale, name, exchange=None):
    t = q.shape[0]
    tq = min(ATT_TILE, t)
    n_q = t // tq
    hd = HEAD_DIM
    decay = c is not None

    def body(*refs):
        if decay:
            q_ref, k_ref, v_ref, do_ref, delta_ref, lse_ref, c_ref, ct_ref, dq_hbm, dk_ref, dv_ref, dct_ref, dq_ref = refs
        else:
            q_ref, k_ref, v_ref, do_ref, delta_ref, lse_ref, dq_hbm, dk_ref, dv_ref, dq_ref = refs
        j, i = pl.program_id(0), pl.program_id(1)

        @pl.when((j == 0) & (i == 0))
        def _():
            dq_ref[...] = jnp.zeros_like(dq_ref)

        @pl.when(i == j)
        def _():
            dk_ref[...] = jnp.zeros_like(dk_ref)
            dv_ref[...] = jnp.zeros_like(dv_ref)
            if decay:
                dct_ref[...] = jnp.zeros_like(dct_ref)

        def step(diagonal):
            if diagonal:
                keep = lax.broadcasted_iota(jnp.int32, (tq, tq), 0) >= lax.broadcasted_iota(jnp.int32, (tq, tq), 1)
            rows = pl.ds(pl.multiple_of(i * tq, tq), tq)
            for h in range(N_HEADS):
                wl = slice(h * lanes, (h + 1) * lanes)
                w = slice(h * hd, (h + 1) * hd)
                qh, kh = q_ref[:, wl], k_ref[:, wl]
                s = _dot(qh, kh, NT)
                if scale != 1.0:
                    s = s * scale
                if decay:
                    s = s + (c_ref[:, h:h + 1] - ct_ref[h:h + 1, :])
                if diagonal:
                    s = jnp.where(keep, s, MASK_VALUE)
                p = jnp.exp(s - lse_ref[:, h:h + 1])
                doh = do_ref[:, w]
                dv_ref[:, w] += _dot(p.astype(BF16), doh, TN)
                dp = _dot(doh, v_ref[:, w], NT)
                ds = p * (dp - delta_ref[:, h:h + 1])
                if decay:
                    dct_ref[h:h + 1, :] -= _colsum(ds)
                if scale != 1.0:
                    ds = ds * scale
                ds_b = ds.astype(BF16)
                dk_ref[:, wl] += _dot(ds_b, qh, TN)
                dq_ref[rows, wl] += _dot(ds_b, kh)

        @pl.when(i > j)
        def _():
            step(False)

        @pl.when(i == j)
        def _():
            step(True)

        @pl.when((j == n_q - 1) & (i == n_q - 1))
        def _():
            pltpu.sync_copy(dq_ref, dq_hbm)

    qspec = lambda n: pl.BlockSpec((tq, n), lambda j, i: (jnp.maximum(i, j), 0))
    kspec = lambda n: pl.BlockSpec((tq, n), lambda j, i: (j, 0))
    in_specs = [qspec(N_HEADS * lanes), kspec(N_HEADS * lanes), kspec(512), qspec(512), qspec(N_HEADS), qspec(N_HEADS)]
    out_specs = [pl.BlockSpec(memory_space=pl.ANY), kspec(N_HEADS * lanes), kspec(512)]
    out_shape = [_sds((t, N_HEADS * lanes), F32), _sds((t, N_HEADS * lanes), F32), _sds((t, 512), F32)]
    args = [q, k, v, do, delta, lse]
    if decay:
        ctspec = pl.BlockSpec((N_HEADS, tq), lambda j, i: (0, j))
        in_specs += [qspec(N_HEADS), ctspec]
        out_specs.append(ctspec)
        out_shape.append(_sds((N_HEADS, t), F32))
        args += [c, c_t]
    return _gridded_call(body, name=name, grid=(n_q, n_q), in_specs=in_specs, out_specs=out_specs, out_shape=out_shape,
                         scratch_shapes=[pltpu.VMEM((t, N_HEADS * lanes), F32)], args=args, exchange=exchange)


def _mix_fwd(x, y_mla, y_fox, gates, b_gate, wbm_t, wbf_t, wo):
    t = x.shape[0]
    tm = min(PROJ_TILE, t)

    def body(x_ref, ym_ref, yf_ref, gt_ref, bg_ref, wbm_ref, wbf_ref, wo_ref, out_ref):
        um = _dot(ym_ref[...], wbm_ref[...], NT)
        uf = _dot(yf_ref[...], wbf_ref[...], NT)
        sm = jax.nn.sigmoid(gt_ref[:, 0:D_MODEL] + bg_ref[0:1, :])
        sf = jax.nn.sigmoid(gt_ref[:, D_MODEL:2 * D_MODEL] + bg_ref[1:2, :])
        mixed = sm * um + sf * uf
        out_ref[...] = x_ref[...] + _dot(mixed.astype(BF16), wo_ref[...])

    tok = _tok(tm, D_MODEL)
    return pl.pallas_call(
        body, name="mix_fwd", grid=(t // tm,),
        in_specs=[tok, _tok(tm, 512), _tok(tm, 512), _tok(tm, 2 * D_MODEL), _full((2, D_MODEL)), _full(wbm_t.shape),
                  _full(wbf_t.shape), _full(wo.shape)],
        out_specs=tok, out_shape=_sds((t, D_MODEL), F32),
        compiler_params=pltpu.CompilerParams(dimension_semantics=("arbitrary",)),
    )(x, y_mla, y_fox, gates, b_gate, wbm_t, wbf_t, wo)


def _mix_bwd(dx, y_mla, y_fox, y_mla32, y_fox32, gates, b_gate, wbm_t, wbf_t, wo):
    t = dx.shape[0]
    tm = min(PROJ_TILE, t)

    def body(dx_ref, ym_ref, yf_ref, ym32_ref, yf32_ref, gt_ref, bg_ref, wbm_ref, wbf_ref, wo_ref,
             dym_ref, dyf_ref, dlm_ref, dlf_ref, dgt_ref, mixed_ref, dum_ref, duf_ref, dxb_ref, dbg_ref, prod):
        i = pl.program_id(0)
        dxb = dx_ref[...].astype(BF16)
        dxb_ref[...] = dxb
        dmixed = _dot(dxb, wo_ref[...], NT)
        um = _dot(ym_ref[...], wbm_ref[...], NT)
        uf = _dot(yf_ref[...], wbf_ref[...], NT)
        sm = jax.nn.sigmoid(gt_ref[:, 0:D_MODEL] + bg_ref[0:1, :])
        sf = jax.nn.sigmoid(gt_ref[:, D_MODEL:2 * D_MODEL] + bg_ref[1:2, :])
        mixed_ref[...] = (sm * um + sf * uf).astype(BF16)
        dum = (dmixed * sm).astype(BF16)
        duf = (dmixed * sf).astype(BF16)
        dum_ref[...] = dum
        duf_ref[...] = duf
        dgm = dmixed * um * (sm * (1.0 - sm))
        dgf = dmixed * uf * (sf * (1.0 - sf))
        dgt_ref[:, 0:D_MODEL] = dgm.astype(BF16)
        dgt_ref[:, D_MODEL:2 * D_MODEL] = dgf.astype(BF16)
        for du, wb_ref, y32_ref, dy_ref, dl_ref in ((dum, wbm_ref, ym32_ref, dym_ref, dlm_ref),
                                                    (duf, wbf_ref, yf32_ref, dyf_ref, dlf_ref)):
            dy = _dot(du, wb_ref[...])
            dy_ref[...] = dy.astype(BF16)
            prod[...] = dy.astype(BF16).astype(F32) * y32_ref[...]
            for h in range(N_HEADS):
                dl_ref[:, h:h + 1] = jnp.sum(prod[:, h * HEAD_DIM:(h + 1) * HEAD_DIM], axis=1, keepdims=True)

        @pl.when(i == 0)
        def _():
            dbg_ref[...] = jnp.zeros_like(dbg_ref)

        dbg_ref[0:1, :] += _colsum(dgm)
        dbg_ref[1:2, :] += _colsum(dgf)

    tok = _tok(tm, D_MODEL)
    tokb = _sds((t, D_MODEL), BF16)
    t512, t8 = _tok(tm, 512), _tok(tm, N_HEADS)
    return pl.pallas_call(
        body, name="mix_bwd", grid=(t // tm,),
        in_specs=[tok, t512, t512, t512, t512, _tok(tm, 2 * D_MODEL), _full((2, D_MODEL)), _full(wbm_t.shape),
                  _full(wbf_t.shape), _full(wo.shape)],
        out_specs=[t512, t512, t8, t8, _tok(tm, 2 * D_MODEL), tok, tok, tok, tok, _full((2, D_MODEL))],
        out_shape=[_sds((t, 512), BF16), _sds((t, 512), BF16), _sds((t, N_HEADS), F32), _sds((t, N_HEADS), F32),
                   _sds((t, 2 * D_MODEL), BF16), tokb, tokb, tokb, tokb, _sds((2, D_MODEL), F32)],
        scratch_shapes=[pltpu.VMEM((tm, 512), F32)],
        compiler_params=pltpu.CompilerParams(dimension_semantics=("arbitrary",)),
    )(dx, y_mla, y_fox, y_mla32, y_fox32, gates, b_gate, wbm_t, wbf_t, wo)


def _my_position():
    x, y, c = lax.axis_index("x"), lax.axis_index("y"), lax.axis_index("c")
    return x, y, c, 4 * x + 2 * y + c


def _peer(x, y, c, mask):
    px = 1 - x if mask & 4 else x
    py = 1 - y if mask & 2 else y
    pc = 1 - c if mask & 1 else c
    return (px, py, pc), 4 * px + 2 * py + pc


def _chip_peer(x, y, km):
    px = 1 - x if km & 2 else x
    py = 1 - y if km & 1 else y
    return px, py, 2 * px + py


_HBM = pl.BlockSpec(memory_space=pl.ANY)


def _wait_all(copies):
    for cp in copies:
        cp.wait()


class _ChipExchange:
    def __init__(self, gather, arrays):
        self.gather, self.arrays = gather, list(arrays)
        n = len(self.arrays)
        self.out_shape = [_sds((N_DEV * a.shape[0],) + a.shape[1:], a.dtype) if gather else _sds(a.shape, a.dtype)
                          for a in self.arrays]
        self.scratch_shapes = [pltpu.SemaphoreType.DMA((n, N_CHIP)), pltpu.SemaphoreType.DMA((n, N_CHIP)),
                               pltpu.SemaphoreType.DMA((n,))]

    def copies(self, srcs, dsts, send_sems, recv_sems, local_sems):
        x, y, c, me = _my_position()
        q_me = 2 * x + y
        out = []
        for a in range(len(self.arrays)):
            if self.gather:
                r = srcs[a].shape[0]
                local_src, dst = srcs[a], dsts[a].at[pl.ds(me * r, r)]
            else:
                local_src, dst = srcs[a].at[q_me], dsts[a].at[q_me]
            out.append(pltpu.make_async_copy(local_src, dst, local_sems.at[a]))
            for km in range(1, N_CHIP):
                px, py, q_peer = _chip_peer(x, y, km)
                out.append(pltpu.make_async_remote_copy(
                    src_ref=srcs[a] if self.gather else srcs[a].at[q_peer], dst_ref=dst, send_sem=send_sems.at[a, km],
                    recv_sem=recv_sems.at[a, km], device_id=(px, py, c), device_id_type=MESH))
        return out

    def standalone(self, name):
        n = len(self.arrays)

        def body(*refs):
            copies = self.copies(refs[:n], refs[n:2 * n], *refs[2 * n:])
            for cp in copies:
                cp.start()
            _wait_all(copies)

        return pl.pallas_call(body, name=name, in_specs=[_HBM] * n, out_specs=[_HBM] * n, out_shape=self.out_shape,
                              scratch_shapes=self.scratch_shapes)(*self.arrays)


def _gridded_call(body, *, name, grid, in_specs, out_specs, out_shape, scratch_shapes, args, exchange=None):
    params = pltpu.CompilerParams(dimension_semantics=("arbitrary",) * len(grid))
    if exchange is None:
        return pl.pallas_call(body, name=name, grid=grid, in_specs=in_specs, out_specs=out_specs, out_shape=out_shape,
                              scratch_shapes=scratch_shapes, compiler_params=params)(*args), None
    n_in, n_out, n_scr, n_x = len(in_specs), len(out_specs), len(scratch_shapes), len(exchange.arrays)

    def carrier(*refs):
        ins, x_src, refs = refs[:n_in], refs[n_in:n_in + n_x], refs[n_in + n_x:]
        outs, x_dst, refs = refs[:n_out], refs[n_out:n_out + n_x], refs[n_out + n_x:]
        copies = exchange.copies(x_src, x_dst, *refs[n_scr:])
        pids = [pl.program_id(d) for d in range(len(grid))]
        first = functools.reduce(jnp.logical_and, [p == 0 for p in pids])
        last = functools.reduce(jnp.logical_and, [p == g - 1 for p, g in zip(pids, grid)])

        @pl.when(first)
        def _():
            for cp in copies:
                cp.start()

        body(*ins, *outs, *refs[:n_scr])

        @pl.when(last)
        def _():
            _wait_all(copies)

    res = pl.pallas_call(
        carrier, name=name, grid=grid, in_specs=list(in_specs) + [_HBM] * n_x, out_specs=list(out_specs) + [_HBM] * n_x,
        out_shape=list(out_shape) + exchange.out_shape, scratch_shapes=list(scratch_shapes) + exchange.scratch_shapes,
        compiler_params=params)(*args, *exchange.arrays)
    return res[:n_out], res[n_out:]


def _gather_over_cores(arrays, name):
    n_arr = len(arrays)

    def body(*refs):
        srcs, dsts = refs[:n_arr], refs[n_arr:2 * n_arr]
        send_sems, recv_sems = refs[2 * n_arr:]
        x, y, c, _ = _my_position()
        copies = []
        for a in range(n_arr):
            r = srcs[a].shape[0] // N_DEV
            for q in range(N_CHIP):
                rows = pl.ds((2 * q + c) * r, r)
                copies.append(pltpu.make_async_remote_copy(
                    src_ref=srcs[a].at[rows], dst_ref=dsts[a].at[rows], send_sem=send_sems.at[a, q],
                    recv_sem=recv_sems.at[a, q], device_id=(x, y, 1 - c), device_id_type=MESH))
        for cp in copies:
            cp.start()
        _wait_all(copies)

    return pl.pallas_call(
        body, name=name, in_specs=[_HBM] * n_arr, out_specs=[_HBM] * n_arr,
        out_shape=[_sds(a.shape, a.dtype) for a in arrays], input_output_aliases={a: a for a in range(n_arr)},
        scratch_shapes=[pltpu.SemaphoreType.DMA((n_arr, N_CHIP)), pltpu.SemaphoreType.DMA((n_arr, N_CHIP))],
    )(*arrays)


def _grads_to_sibling(grads, name):
    n_arr = len(grads)

    def body(*refs):
        srcs, dsts = refs[:n_arr], refs[n_arr:2 * n_arr]
        send_sems, recv_sems = refs[2 * n_arr:]
        x, y, c, _ = _my_position()
        copies = []
        for a in range(n_arr):
            r = srcs[a].shape[0] // N_DEV
            for q in range(N_CHIP):
                copies.append(pltpu.make_async_remote_copy(
                    src_ref=srcs[a].at[pl.ds((2 * q + 1 - c) * r, r)], dst_ref=dsts[a].at[q], send_sem=send_sems.at[a, q],
                    recv_sem=recv_sems.at[a, q], device_id=(x, y, 1 - c), device_id_type=MESH))
        for cp in copies:
            cp.start()
        _wait_all(copies)

    return pl.pallas_call(
        body, name=name, in_specs=[_HBM] * n_arr, out_specs=[_HBM] * n_arr,
        out_shape=[_sds((N_CHIP, g.shape[0] // N_DEV) + g.shape[1:], g.dtype) for g in grads],
        scratch_shapes=[pltpu.SemaphoreType.DMA((n_arr, N_CHIP)), pltpu.SemaphoreType.DMA((n_arr, N_CHIP))],
    )(*grads)


def _pair_sum(grad, from_sibling, name):
    r, n = from_sibling.shape[1:]

    def body(g_ref, s_ref, o_ref):
        c = lax.axis_index("c")
        o_ref[...] = (g_ref[c].astype(F32) + s_ref[...].astype(F32)).astype(BF16)

    return pl.pallas_call(
        body, name=name, grid=(N_CHIP,),
        in_specs=[pl.BlockSpec((None, 2, r, n), lambda q: (q, 0, 0, 0)), pl.BlockSpec((None, r, n), lambda q: (q, 0, 0))],
        out_specs=pl.BlockSpec((None, r, n), lambda q: (q, 0, 0)), out_shape=_sds((N_CHIP, r, n), BF16),
    )(grad.reshape(N_CHIP, 2, r, n), from_sibling)


def _all_reduce_small(vec):
    r = vec.shape[0]

    def body(v_ref, o_ref, buf, send_sems, recv_sems):
        x, y, c, me = _my_position()
        buf[me] = v_ref[...]
        copies = []
        for mask in range(1, N_DEV):
            peer, _ = _peer(x, y, c, mask)
            cp = pltpu.make_async_remote_copy(src_ref=v_ref, dst_ref=buf.at[me], send_sem=send_sems.at[mask],
                                              recv_sem=recv_sems.at[mask], device_id=peer, device_id_type=MESH)
            cp.start()
            copies.append(cp)
        for cp in copies:
            cp.wait()
        total = buf[0]
        for s in range(1, N_DEV):
            total = total + buf[s]
        o_ref[...] = total

    vm = pl.BlockSpec(memory_space=pltpu.VMEM)
    return pl.pallas_call(
        body, name="all_reduce_small", in_specs=[vm], out_specs=vm, out_shape=_sds(vec.shape, F32),
        scratch_shapes=[pltpu.VMEM((N_DEV, r, 128), F32), pltpu.SemaphoreType.DMA((N_DEV,)), pltpu.SemaphoreType.DMA((N_DEV,))],
    )(vec)


def _adamw_math(w, g, m, v):
    m = ADAM_B1 * m + (1.0 - ADAM_B1) * g
    v = ADAM_B2 * v + (1.0 - ADAM_B2) * (g * g)
    m_hat = m / (1.0 - ADAM_B1 ** ADAM_STEP)
    v_hat = v / (1.0 - ADAM_B2 ** ADAM_STEP)
    delta = -ADAM_LR * (m_hat / (jnp.sqrt(v_hat) + ADAM_EPS) + ADAM_WD * w)
    return delta, m, v


def _reduce_adamw(slots, w, m, v, *, transpose, name):
    r, n = slots.shape[1:]
    padded = not transpose and w.shape[0] != r

    def body(s_ref, w_ref, m_ref, v_ref, g_ref, d_ref, nm_ref, nv_ref, *scr):
        g = s_ref[0].astype(F32)
        for s in range(1, slots.shape[0]):
            g = g + s_ref[s].astype(F32)
        if transpose:
            scr[0][...] = g.T
            g = scr[0][:, 0:w_ref.shape[1]]
        elif padded:
            scr[0][...] = g
            g = scr[0][0:w_ref.shape[0], :]
        g_ref[...] = g
        d_ref[...], nm_ref[...], nv_ref[...] = _adamw_math(w_ref[...], g, m_ref[...], v_ref[...])

    out = _sds(w.shape, F32)
    return pl.pallas_call(
        body, name=name, out_shape=[out, out, out, out],
        scratch_shapes=[pltpu.VMEM((n, r) if transpose else (r, n), F32)] if transpose or padded else [],
    )(slots, w, m, v)


def _adamw(g, w, m, v, name):
    def body(g_ref, w_ref, m_ref, v_ref, d_ref, nm_ref, nv_ref):
        d_ref[...], nm_ref[...], nv_ref[...] = _adamw_math(w_ref[...], g_ref[...], m_ref[...], v_ref[...])

    out = _sds(w.shape, F32)
    return pl.pallas_call(body, name=name, out_shape=[out, out, out])(g, w, m, v)


_SMALL = ["ffn1_norm", "mix_norm", "ffn2_norm", "mla_q_lat_norm", "mla_kv_lat_norm", "mla_q_nope_gain", "mla_q_rope_gain",
          "mla_k_nope_gain", "mla_k_rope_gain", "fox_q_gain", "fox_k_gain", "fox_b_f"]
_WEIGHTS = ["ffn1_norm", "ffn1_w_gate", "ffn1_w_up", "ffn1_w_down", "mix_norm", "w_in", "mla_q_lat_norm", "mla_w_qb",
            "mla_kv_lat_norm", "mla_w_kvb", "mla_q_nope_gain", "mla_q_rope_gain", "mla_k_nope_gain", "mla_k_rope_gain",
            "fox_q_gain", "fox_k_gain", "fox_b_f", "w_branch_mla", "w_branch_fox", "b_gate", "w_o", "ffn2_norm",
            "ffn2_w_gate", "ffn2_w_up", "ffn2_w_down"]
_IN_Q, _IN_KV, _IN_KR, _IN_FOX, _IN_F, _IN_GATES = (0, 192), (192, 128), (320, 32), (352, 1536), (1888, 8), (1896, 2048)


def _rows(a, seg):
    return a[seg[0]:seg[0] + seg[1]]


def _split_w_in(win_t):
    z = lambda n: jnp.zeros((n, D_MODEL), win_t.dtype)
    lat = jnp.concatenate([_rows(win_t, _IN_Q), z(LAT_KV - Q_LORA), _rows(win_t, _IN_KV), _rows(win_t, _IN_KR),
                           _rows(win_t, _IN_F), z(LAT_W - LAT_F - N_HEADS)], axis=0)
    return _rows(win_t, _IN_GATES), _rows(win_t, _IN_FOX), lat


def _join_w_in(d_gates, d_fox, d_lat):
    return jnp.concatenate([d_lat[LAT_Q:LAT_Q + Q_LORA], d_lat[LAT_KV:LAT_KV + KV_LORA], d_lat[LAT_KR:LAT_KR + ROPE_DIM],
                            d_fox, d_lat[LAT_F:LAT_F + N_HEADS], d_gates], axis=0)


def kernel(x, positions, ffn1_norm, ffn1_w_gate, ffn1_w_up, ffn1_w_down, mix_norm, w_in, mla_q_lat_norm, mla_w_qb, mla_kv_lat_norm, mla_w_kvb, mla_q_nope_gain, mla_q_rope_gain, mla_k_nope_gain, mla_k_rope_gain, fox_q_gain, fox_k_gain, fox_b_f, w_branch_mla, w_branch_fox, b_gate, w_o, ffn2_norm, ffn2_w_gate, ffn2_w_up, ffn2_w_down, loss_target, m_ffn1_norm, m_ffn1_w_gate, m_ffn1_w_up, m_ffn1_w_down, m_mix_norm, m_w_in, m_mla_q_lat_norm, m_mla_w_qb, m_mla_kv_lat_norm, m_mla_w_kvb, m_mla_q_nope_gain, m_mla_q_rope_gain, m_mla_k_nope_gain, m_mla_k_rope_gain, m_fox_q_gain, m_fox_k_gain, m_fox_b_f, m_w_branch_mla, m_w_branch_fox, m_b_gate, m_w_o, m_ffn2_norm, m_ffn2_w_gate, m_ffn2_w_up, m_ffn2_w_down, v_ffn1_norm, v_ffn1_w_gate, v_ffn1_w_up, v_ffn1_w_down, v_mix_norm, v_w_in, v_mla_q_lat_norm, v_mla_w_qb, v_mla_kv_lat_norm, v_mla_w_kvb, v_mla_q_nope_gain, v_mla_q_rope_gain, v_mla_k_nope_gain, v_mla_k_rope_gain, v_fox_q_gain, v_fox_k_gain, v_fox_b_f, v_w_branch_mla, v_w_branch_fox, v_b_gate, v_w_o, v_ffn2_norm, v_ffn2_w_gate, v_ffn2_w_up, v_ffn2_w_down):
    env = dict(locals())
    strip = lambda n, a: a if n in _SMALL else a[0]
    W = {n: strip(n, env[n]) for n in _WEIGHTS}
    M = {n: strip(n, env["m_" + n]) for n in _WEIGHTS}
    V = {n: strip(n, env["v_" + n]) for n in _WEIGHTS}
    xs = x[0]
    t = xs.shape[0]

    col_split = ["ffn1_w_gate", "ffn1_w_up", "ffn2_w_gate", "ffn2_w_up", "mla_w_qb", "mla_w_kvb", "w_branch_mla", "w_branch_fox"]
    row_split = ["ffn1_w_down", "ffn2_w_down", "w_o"]
    pieces = {n: W[n].T.astype(BF16) for n in col_split}
    pieces.update({n: W[n].astype(BF16) for n in row_split})
    pieces["w_in"] = jnp.pad(W["w_in"].T.astype(BF16), ((0, W_IN_PIECE_PAD - W_IN_PIECE), (0, 0)))
    pieces["b_gate"] = W["b_gate"].T
    group_a = ["ffn1_w_gate", "ffn1_w_up", "ffn1_w_down"]
    group_b = ["w_in", "mla_w_qb", "mla_w_kvb", "w_branch_mla", "w_branch_fox", "w_o", "b_gate"]
    group_c = ["ffn2_w_gate", "ffn2_w_up", "ffn2_w_down"]
    gather = lambda group: _ChipExchange(True, [pieces[n] for n in group])
    G = dict(zip(group_a, _gather_over_cores(gather(group_a).standalone("gather_ici_a"), "gather_d2d_a")))

    inv_freq = ROPE_THETA ** (-jnp.arange(ROPE_HALF, dtype=F32) / ROPE_HALF)
    ang = positions[0].astype(F32)[:, None] * inv_freq
    cos, sin = jnp.cos(ang), jnp.sin(ang)
    prep_small = [W[n] for n in ["mla_q_lat_norm", "mla_kv_lat_norm", "mla_q_nope_gain", "mla_q_rope_gain", "mla_k_nope_gain",
                                 "mla_k_rope_gain", "fox_q_gain", "fox_k_gain", "fox_b_f"]]

    (x1, a1, b1), got_b = _ffn_fwd(xs, W["ffn1_norm"], G["ffn1_w_gate"], G["ffn1_w_up"], G["ffn1_w_down"],
                                   exchange=gather(group_b))
    G.update(zip(group_b, _gather_over_cores(got_b, "gather_d2d_b")))
    win_t = G["w_in"].reshape(N_DEV, W_IN_PIECE_PAD, D_MODEL)[:, :W_IN_PIECE].reshape(N_DEV * W_IN_PIECE, D_MODEL)
    wgate_t, wfox_t, wlat_t = _split_w_in(win_t)
    bg = G["b_gate"].T
    gates, fox, lat = _proj_fwd(x1, W["mix_norm"], wgate_t, wfox_t, wlat_t)
    fq, fk, fv, c, mq, mk, mv = _prep_fwd(fox, lat, cos, sin, G["mla_w_qb"], G["mla_w_kvb"], *prep_small)
    c_t = c.T
    b_fox, ok_fox = _logit_bound(fq, fk, lanes=HEAD_DIM, scale=1.0, name="fox_norms")
    b_mla, ok_mla = _logit_bound(mq, mk, lanes=MLA_QK_LANES, scale=MLA_SCALE, name="mla_norms")
    (y_fox, y_fox32, lse_fox), got_c = _attn_fwd(fq, fk, fv, b_fox, ok_fox, c, c_t, lanes=HEAD_DIM, scale=1.0, name="fox_fwd",
                                                 exchange=gather(group_c))
    G.update(zip(group_c, _gather_over_cores(got_c, "gather_d2d_c")))
    (y_mla, y_mla32, lse_mla), _ = _attn_fwd(mq, mk, mv, b_mla, ok_mla, None, None, lanes=MLA_QK_LANES, scale=MLA_SCALE,
                                             name="mla_fwd")
    x2 = _mix_fwd(x1, y_mla, y_fox, gates, bg, G["w_branch_mla"], G["w_branch_fox"], G["w_o"])
    (dx3, a2, b2, loss_vec), _ = _ffn_fwd(x2, W["ffn2_norm"], G["ffn2_w_gate"], G["ffn2_w_up"], G["ffn2_w_down"],
                                          target=loss_target[0])

    def chip_sums(group, tag):
        from_sibling = _grads_to_sibling([grads[n] for n in group], "grads_d2d_" + tag)
        return _ChipExchange(False, [_pair_sum(grads[n], s, "pair_sum_" + n) for n, s in zip(group, from_sibling)])

    (dx2, dg_ffn2, da2, db2, h2, n2, dyh2), _ = _ffn_bwd(dx3, x2, W["ffn2_norm"], a2, b2, G["ffn2_w_gate"], G["ffn2_w_up"],
                                                        G["ffn2_w_down"], "ffn2_bwd")
    grads = {"ffn2_w_gate": _tn_matmul(da2, n2, "ffn2_dgate"), "ffn2_w_up": _tn_matmul(db2, n2, "ffn2_dup"),
             "ffn2_w_down": _tn_matmul(h2, dyh2, "ffn2_ddown")}
    dy_mla, dy_fox, delta_mla, delta_fox, dgates, mixed, dum, duf, dx2b, dbg = _mix_bwd(
        dx2, y_mla, y_fox, y_mla32, y_fox32, gates, bg, G["w_branch_mla"], G["w_branch_fox"], G["w_o"])
    grads["w_o"] = _tn_matmul(mixed, dx2b, "d_w_o")
    grads["w_branch_mla"] = _tn_matmul(dum, y_mla, "d_w_branch_mla")
    grads["w_branch_fox"] = _tn_matmul(duf, y_fox, "d_w_branch_fox")
    (dfq, dfk, dfv, dc_t), slots_c = _attn_bwd(fq, fk, fv, dy_fox, delta_fox, lse_fox, c, c_t, lanes=HEAD_DIM, scale=1.0,
                                               name="fox_bwd", exchange=chip_sums(group_c, "c"))
    slots = dict(zip(group_c, slots_c))
    (dmq, dmk, dmv), _ = _attn_bwd(mq, mk, mv, dy_mla, delta_mla, lse_mla, None, None, lanes=MLA_QK_LANES, scale=MLA_SCALE,
                                   name="mla_bwd")
    prep_out = _prep_bwd(fox, lat, cos, sin, G["mla_w_qb"], G["mla_w_kvb"], *prep_small, dfq, dfk, dfv, dc_t.T, dmq, dmk, dmv)
    dfox, dlat, grads["mla_w_qb"], grads["mla_w_kvb"] = prep_out[:4]
    d_prep_small = prep_out[4:]
    dx1, dg_mix, nmix = _proj_bwd(dgates, dfox, dlat, wgate_t, wfox_t, wlat_t, x1, W["mix_norm"], dx2)
    dwin_t = _join_w_in(_tn_matmul(dgates, nmix, "d_w_in_gates"), _tn_matmul(dfox, nmix, "d_w_in_fox"),
                        _tn_matmul(dlat, nmix, "d_w_in_lat"))
    grads["w_in"] = jnp.pad(dwin_t.reshape(N_DEV, W_IN_PIECE, D_MODEL), ((0, 0), (0, W_IN_PIECE_PAD - W_IN_PIECE), (0, 0))
                            ).reshape(N_DEV * W_IN_PIECE_PAD, D_MODEL)
    grad_group_b = [n for n in group_b if n != "b_gate"]
    (dx0, dg_ffn1, da1, db1, h1, n1, dyh1), slots_b = _ffn_bwd(dx1, xs, W["ffn1_norm"], a1, b1, G["ffn1_w_gate"], G["ffn1_w_up"],
                                                              G["ffn1_w_down"], "ffn1_bwd", exchange=chip_sums(grad_group_b, "b"))
    slots.update(zip(grad_group_b, slots_b))
    grads["ffn1_w_gate"] = _tn_matmul(da1, n1, "ffn1_dgate")
    grads["ffn1_w_up"], got = _tn_matmul(db1, n1, "ffn1_dup", exchange=chip_sums(["ffn1_w_gate"], "a_gate"))
    slots["ffn1_w_gate"] = got[0]
    grads["ffn1_w_down"], got = _tn_matmul(h1, dyh1, "ffn1_ddown", exchange=chip_sums(["ffn1_w_up"], "a_up"))
    slots["ffn1_w_up"] = got[0]
    slots["ffn1_w_down"] = chip_sums(["ffn1_w_down"], "a_down").standalone("grads_ici_a_down")[0]

    small_parts = [dg_ffn1, dg_mix, dg_ffn2] + list(d_prep_small) + [dbg.reshape(1, 2 * D_MODEL), loss_vec]
    flat = jnp.concatenate([p.reshape(-1) for p in small_parts])
    n_flat = flat.shape[0]
    rows = -(-n_flat // (8 * 128)) * 8
    total = _all_reduce_small(jnp.pad(flat, (0, rows * 128 - n_flat)).reshape(rows, 128)).reshape(-1)
    offs, small_g = 0, {}
    for n in _SMALL:
        small_g[n] = total[offs:offs + W[n].shape[1]].reshape(W[n].shape)
        offs += W[n].shape[1]
    bg_full = total[offs:offs + 2 * D_MODEL].reshape(2, D_MODEL)
    offs += 2 * D_MODEL
    loss = (0.5 / D_MODEL) * jnp.sum(total[offs:offs + D_MODEL])
    _, _, _, me = _my_position()
    small_g["b_gate"] = lax.dynamic_slice_in_dim(bg_full, me * (D_MODEL // N_DEV), D_MODEL // N_DEV, axis=1)

    transposed_in_memory = ["ffn1_w_gate", "ffn1_w_up", "ffn2_w_gate", "ffn2_w_up", "w_in"]
    res = {}
    for n in _WEIGHTS:
        if n in small_g:
            res[n] = (small_g[n],) + tuple(_adamw(small_g[n], W[n], M[n], V[n], "adamw_" + n))
        elif n in transposed_in_memory:
            res[n] = tuple(o.T for o in _reduce_adamw(slots[n], W[n].T, M[n].T, V[n].T, transpose=False, name="adamw_" + n))
        else:
            res[n] = tuple(_reduce_adamw(slots[n], W[n], M[n], V[n], transpose=n in col_split, name="adamw_" + n))
    outs = [loss, dx0[None]]
    for k in range(4):
        outs += [res[n][k] if n in _SMALL else res[n][k][None] for n in _WEIGHTS]
    return tuple(outs)
```

```python
import functools

import jax
import jax.numpy as jnp
from jax import lax
from jax.experimental import pallas as pl
from jax.experimental.pallas import tpu as pltpu

F32 = jnp.float32
BF16 = jnp.bfloat16

D_MODEL = 1024
FFN_HIDDEN = 2816
N_HEADS = 8
HEAD_DIM = 64
ROPE_DIM = 32
ROPE_HALF = 16
Q_LORA = 192
KV_LORA = 128
ROPE_THETA = 10000.0
RMS_EPS = 1e-6
MLA_SCALE = (HEAD_DIM + ROPE_DIM) ** -0.5
FOX_SCALE = HEAD_DIM ** -0.5
MLA_QK_LANES = 128
ADAM_LR, ADAM_B1, ADAM_B2, ADAM_EPS, ADAM_WD, ADAM_STEP = 0.001, 0.9, 0.999, 1e-08, 0.01, 10
N_DEV = 8
N_CHIP = 4
W_IN_PIECE = 493
W_IN_PIECE_PAD = 496
LAT_W = 512
LAT_Q, LAT_KV, LAT_KR, LAT_F = 0, 256, 384, 416
MASK_VALUE = -1e30
FIXED_SHIFT_MAX_BOUND = 30.0

TOK_TILE = 512
PROJ_TILE = 256
ATT_TILE = 512
ATT_COL_CHUNK = 256
FFN_HID_TILE = 1408
FFN_HID_SPLIT = ((0, 768), (768, 1408))

NT = (((1,), (1,)), ((), ()))
TN = (((0,), (0,)), ((), ()))
NN = (((1,), (0,)), ((), ()))
MESH = pl.DeviceIdType.MESH


def _dot(a, b, dims=NN):
    return lax.dot_general(a, b, dims, preferred_element_type=F32)


def _sds(shape, dtype):
    return jax.ShapeDtypeStruct(shape, dtype)


def _rms_fwd(x, g):
    r = lax.rsqrt(jnp.mean(x * x, axis=-1, keepdims=True) + RMS_EPS)
    return x * r * g, r


def _rms_bwd(dy, x, g, r):
    xn = x * r
    dyg = dy * g
    dx = r * (dyg - xn * jnp.mean(dyg * xn, axis=-1, keepdims=True))
    return dx, dy * xn


def _colsum(x):
    return jnp.sum(x, axis=0, keepdims=True)


def _full(shape):
    return pl.BlockSpec(shape, lambda *_: (0,) * len(shape))


def _tok(tm, n):
    return pl.BlockSpec((tm, n), lambda i, *_: (i, 0))


def _ffn_fwd(x, gain, wg_t, wu_t, wd, target=None, exchange=None):
    t = x.shape[0]
    tm = min(TOK_TILE, t)
    tf = FFN_HID_TILE
    n_t, n_f = t // tm, FFN_HIDDEN // tf
    with_loss = target is not None

    def body(*refs):
        if with_loss:
            x_ref, g_ref, wg_ref, wu_ref, wd_ref, t_ref, out_ref, a_ref, b_ref, lvec_ref, n_scr, acc = refs
        else:
            x_ref, g_ref, wg_ref, wu_ref, wd_ref, out_ref, a_ref, b_ref, n_scr, acc = refs
        i, j = pl.program_id(0), pl.program_id(1)

        @pl.when(j == 0)
        def _():
            xn, _ = _rms_fwd(x_ref[...], g_ref[...])
            n_scr[...] = xn.astype(BF16)
            acc[...] = jnp.zeros_like(acc)

        n = n_scr[...]
        a = _dot(n, wg_ref[...], NT)
        b = _dot(n, wu_ref[...], NT)
        a_ref[...] = a.astype(BF16)
        b_ref[...] = b.astype(BF16)
        h = (a * jax.nn.sigmoid(a)) * b
        acc[...] += _dot(h.astype(BF16), wd_ref[...])

        @pl.when(j == n_f - 1)
        def _():
            y = x_ref[...] + 0.5 * acc[...]
            if with_loss:
                diff = y - t_ref[...]
                out_ref[...] = diff * (1.0 / D_MODEL)
                sq = _colsum(diff * diff)

                @pl.when(i == 0)
                def _():
                    lvec_ref[...] = sq

                @pl.when(i > 0)
                def _():
                    lvec_ref[...] += sq
            else:
                out_ref[...] = y

    wspec = pl.BlockSpec((tf, D_MODEL), lambda i, j: (j, 0))
    hspec = pl.BlockSpec((tm, tf), lambda i, j: (i, j))
    in_specs = [_tok(tm, D_MODEL), _full((1, D_MODEL)), wspec, wspec, wspec]
    out_specs = [_tok(tm, D_MODEL), hspec, hspec]
    out_shape = [_sds((t, D_MODEL), F32), _sds((t, FFN_HIDDEN), BF16), _sds((t, FFN_HIDDEN), BF16)]
    args = [x, gain, wg_t, wu_t, wd]
    if with_loss:
        in_specs.append(_tok(tm, D_MODEL))
        out_specs.append(_full((1, D_MODEL)))
        out_shape.append(_sds((1, D_MODEL), F32))
        args.append(target)
    return _gridded_call(
        body, name="ffn_fwd_loss" if with_loss else "ffn_fwd", grid=(n_t, n_f), in_specs=in_specs, out_specs=out_specs,
        out_shape=out_shape, scratch_shapes=[pltpu.VMEM((tm, D_MODEL), BF16), pltpu.VMEM((tm, D_MODEL), F32)],
        args=args, exchange=exchange)


def _ffn_bwd(dy, x, gain, a, b, wg_t, wu_t, wd, name, exchange=None):
    t = x.shape[0]
    tm = min(TOK_TILE, t)
    tf = FFN_HID_TILE
    n_t, n_f = t // tm, FFN_HIDDEN // tf

    def body(dy_ref, x_ref, g_ref, a_ref, b_ref, wg_ref, wu_ref, wd_ref,
             dx_ref, dg_ref, da_ref, db_ref, h_ref, n_ref, dyh_ref, acc):
        i, j = pl.program_id(0), pl.program_id(1)

        @pl.when(j == 0)
        def _():
            xn, _ = _rms_fwd(x_ref[...], g_ref[...])
            n_ref[...] = xn.astype(BF16)
            dyh_ref[...] = (0.5 * dy_ref[...]).astype(BF16)
            acc[...] = jnp.zeros_like(acc)

        dyh = dyh_ref[...]
        for lo, hi in FFN_HID_SPLIT:
            dh = _dot(dyh, wd_ref[lo:hi, :], NT)
            av = a_ref[:, lo:hi].astype(F32)
            bv = b_ref[:, lo:hi].astype(F32)
            s = jax.nn.sigmoid(av)
            silu = av * s
            da = (dh * bv * (s * (1.0 + av * (1.0 - s)))).astype(BF16)
            db = (dh * silu).astype(BF16)
            da_ref[:, lo:hi] = da
            db_ref[:, lo:hi] = db
            h_ref[:, lo:hi] = (silu * bv).astype(BF16)
            acc[...] += _dot(da, wg_ref[lo:hi, :]) + _dot(db, wu_ref[lo:hi, :])

        @pl.when(j == n_f - 1)
        def _():
            xv, g = x_ref[...], g_ref[...]
            r = lax.rsqrt(jnp.mean(xv * xv, axis=-1, keepdims=True) + RMS_EPS)
            dx, dg_rows = _rms_bwd(acc[...], xv, g, r)
            dx_ref[...] = dy_ref[...] + dx
            dg = _colsum(dg_rows)

            @pl.when(i == 0)
            def _():
                dg_ref[...] = dg

            @pl.when(i > 0)
            def _():
                dg_ref[...] += dg

    wspec = pl.BlockSpec((tf, D_MODEL), lambda i, j: (j, 0))
    hspec = pl.BlockSpec((tm, tf), lambda i, j: (i, j))
    tok = _tok(tm, D_MODEL)
    hid = _sds((t, FFN_HIDDEN), BF16)
    return _gridded_call(
        body, name=name, grid=(n_t, n_f),
        in_specs=[tok, tok, _full((1, D_MODEL)), hspec, hspec, wspec, wspec, wspec],
        out_specs=[tok, _full((1, D_MODEL)), hspec, hspec, hspec, tok, tok],
        out_shape=[_sds((t, D_MODEL), F32), _sds((1, D_MODEL), F32), hid, hid, hid,
                   _sds((t, D_MODEL), BF16), _sds((t, D_MODEL), BF16)],
        scratch_shapes=[pltpu.VMEM((tm, D_MODEL), F32)], args=[dy, x, gain, a, b, wg_t, wu_t, wd], exchange=exchange)


def _tn_matmul(a, b, name, exchange=None):
    t, m = a.shape
    n = b.shape[1]
    tk = min(TOK_TILE, t)
    n_k = t // tk

    def body(a_ref, b_ref, o_ref, acc):
        k = pl.program_id(0)
        p = _dot(a_ref[...], b_ref[...], TN)

        @pl.when(k == 0)
        def _():
            acc[...] = p

        @pl.when(k > 0)
        def _():
            acc[...] += p

        @pl.when(k == n_k - 1)
        def _():
            o_ref[...] = acc[...].astype(BF16)

    (out,), got = _gridded_call(
        body, name=name, grid=(n_k,), in_specs=[_tok(tk, m), _tok(tk, n)], out_specs=[_full((m, n))],
        out_shape=[_sds((m, n), BF16)], scratch_shapes=[pltpu.VMEM((m, n), F32)], args=[a, b], exchange=exchange)
    return out if exchange is None else (out, got)


def _proj_fwd(x, gain, wgate_t, wfox_t, wlat_t):
    t = x.shape[0]
    tm = min(PROJ_TILE, t)

    def body(x_ref, g_ref, wg_ref, wf_ref, wl_ref, og_ref, of_ref, ol_ref):
        xn, _ = _rms_fwd(x_ref[...], g_ref[...])
        n = xn.astype(BF16)
        og_ref[...] = _dot(n, wg_ref[...], NT)
        of_ref[...] = _dot(n, wf_ref[...], NT)
        ol_ref[...] = _dot(n, wl_ref[...], NT)

    return pl.pallas_call(
        body, name="proj_fwd", grid=(t // tm,),
        in_specs=[_tok(tm, D_MODEL), _full((1, D_MODEL)), _full(wgate_t.shape), _full(wfox_t.shape), _full(wlat_t.shape)],
        out_specs=[_tok(tm, 2 * D_MODEL), _tok(tm, 3 * 512), _tok(tm, LAT_W)],
        out_shape=[_sds((t, 2 * D_MODEL), F32), _sds((t, 3 * 512), F32), _sds((t, LAT_W), F32)],
        compiler_params=pltpu.CompilerParams(dimension_semantics=("arbitrary",)),
    )(x, gain, wgate_t, wfox_t, wlat_t)


def _proj_bwd(dgates, dfox, dlat, wgate_t, wfox_t, wlat_t, x, gain, dres):
    t = x.shape[0]
    tm = min(PROJ_TILE, t)

    def body(dg_ref, df_ref, dl_ref, wg_ref, wf_ref, wl_ref, x_ref, g_ref, dres_ref, dx_ref, dgain_ref, n_ref):
        i = pl.program_id(0)
        dn = _dot(dg_ref[...], wg_ref[...]) + _dot(df_ref[...], wf_ref[...]) + _dot(dl_ref[...], wl_ref[...])
        xv, g = x_ref[...], g_ref[...]
        xn, r = _rms_fwd(xv, g)
        n_ref[...] = xn.astype(BF16)
        dx, dg_rows = _rms_bwd(dn, xv, g, r)
        dx_ref[...] = dres_ref[...] + dx
        dgn = _colsum(dg_rows)

        @pl.when(i == 0)
        def _():
            dgain_ref[...] = dgn

        @pl.when(i > 0)
        def _():
            dgain_ref[...] += dgn

    tok = _tok(tm, D_MODEL)
    return pl.pallas_call(
        body, name="proj_bwd", grid=(t // tm,),
        in_specs=[_tok(tm, 2 * D_MODEL), _tok(tm, 3 * 512), _tok(tm, LAT_W), _full(wgate_t.shape), _full(wfox_t.shape),
                  _full(wlat_t.shape), tok, _full((1, D_MODEL)), tok],
        out_specs=[tok, _full((1, D_MODEL)), tok],
        out_shape=[_sds((t, D_MODEL), F32), _sds((1, D_MODEL), F32), _sds((t, D_MODEL), BF16)],
        compiler_params=pltpu.CompilerParams(dimension_semantics=("arbitrary",)),
    )(dgates, dfox, dlat, wgate_t, wfox_t, wlat_t, x, gain, dres)


def _tri(n, lower):
    r = lax.broadcasted_iota(jnp.int32, (n, n), 0)
    c = lax.broadcasted_iota(jnp.int32, (n, n), 1)
    return ((c <= r) if lower else (c >= r)).astype(F32)


def _log_sigmoid(z):
    return jnp.minimum(z, 0.0) - jnp.log1p(jnp.exp(-jnp.abs(z)))


N_GROUPS = 16
LANES = MLA_QK_LANES


def _split_dot(a, g):
    hi = a.astype(BF16)
    lo = (a - hi.astype(F32)).astype(BF16)
    return _dot(hi, g) + _dot(lo, g)


def _rope_fwd(y, c, s1, s2):
    return y * c + pltpu.roll(y, LANES - ROPE_HALF, 1) * s1 + pltpu.roll(y, ROPE_HALF, 1) * s2


def _rope_bwd(do, c, s1, s2):
    return do * c + pltpu.roll(do * s1, ROPE_HALF, 1) + pltpu.roll(do * s2, LANES - ROPE_HALF, 1)


def _prep_tables(cos, sin):
    t = cos.shape[0]
    z = lambda n: jnp.zeros((t, n), F32)
    c = jnp.concatenate([jnp.ones((t, HEAD_DIM), F32), cos, cos, z(LANES - HEAD_DIM - ROPE_DIM)], axis=1)
    s1 = jnp.concatenate([z(HEAD_DIM), -sin, z(LANES - HEAD_DIM - ROPE_HALF)], axis=1)
    s2 = jnp.concatenate([z(HEAD_DIM + ROPE_HALF), sin, z(LANES - HEAD_DIM - ROPE_DIM)], axis=1)
    lane = jnp.arange(LANES)
    rope_mask = ((lane >= HEAD_DIM) & (lane < HEAD_DIM + ROPE_DIM)).astype(F32)[None, :]
    return c, s1, s2, rope_mask


def _group_matrices():
    lane = jnp.arange(N_HEADS * LANES)
    head, d = (lane // LANES)[:, None], (lane % LANES)[:, None]
    col = jnp.arange(N_GROUPS)[None, :]
    g_mla = ((col == head) & (d < HEAD_DIM)) | ((col == N_HEADS + head) & (d >= HEAD_DIM) & (d < HEAD_DIM + ROPE_DIM))
    g_fox = col == (jnp.arange(N_HEADS * HEAD_DIM) // HEAD_DIM)[:, None]
    inv_mla = jnp.concatenate([jnp.full((1, N_HEADS), 1.0 / HEAD_DIM, F32), jnp.full((1, N_HEADS), 1.0 / ROPE_DIM, F32)], axis=1)
    inv_fox = jnp.full((1, N_GROUPS), 1.0 / HEAD_DIM, F32)
    return g_mla.astype(BF16), g_mla.T.astype(BF16), inv_mla, g_fox.astype(BF16), g_fox.T.astype(BF16), inv_fox


def _interleave_weights(wqb_t, wkvb_t):
    wq = jnp.pad(wqb_t.reshape(N_HEADS, HEAD_DIM + ROPE_DIM, Q_LORA), ((0, 0), (0, LANES - HEAD_DIM - ROPE_DIM), (0, 0)))
    kv = wkvb_t.reshape(N_HEADS, 2, HEAD_DIM, KV_LORA)
    wk = jnp.pad(kv[:, 0], ((0, 0), (0, LANES - HEAD_DIM), (0, 0)))
    return wq.reshape(N_HEADS * LANES, Q_LORA), wk.reshape(N_HEADS * LANES, KV_LORA), kv[:, 1].reshape(N_HEADS * HEAD_DIM, KV_LORA)


def _deinterleave_grads(dwq, dwk, dwv):
    dq = dwq.reshape(N_HEADS, LANES, Q_LORA)[:, :HEAD_DIM + ROPE_DIM].reshape(N_HEADS * (HEAD_DIM + ROPE_DIM), Q_LORA)
    dk = dwk.reshape(N_HEADS, LANES, KV_LORA)[:, :HEAD_DIM]
    dkv = jnp.stack([dk, dwv.reshape(N_HEADS, HEAD_DIM, KV_LORA)], axis=1)
    return dq, dkv.reshape(N_HEADS * 2 * HEAD_DIM, KV_LORA)


def _head_gains(g_qn, g_qr, g_kn, g_kr, g_fq, g_fk):
    z = lambda n: jnp.zeros((1, n), F32)
    gq = jnp.concatenate([g_qn, g_qr, z(LANES - HEAD_DIM - ROPE_DIM)], axis=1)
    gk = jnp.concatenate([g_kn, z(LANES - HEAD_DIM)], axis=1)
    gkr = jnp.concatenate([z(HEAD_DIM), g_kr, z(LANES - HEAD_DIM - ROPE_DIM)], axis=1)
    return jnp.tile(gq, (1, N_HEADS)), jnp.tile(gk, (1, N_HEADS)), gkr, jnp.tile(g_fq, (1, N_HEADS)), jnp.tile(g_fk, (1, N_HEADS))


def _group_rms(x, g, g_t, inv):
    r = lax.rsqrt(_split_dot(x * x, g) * inv + RMS_EPS)
    return _split_dot(r, g_t)


def _prep_fwd(fox, lat, tables, mats, wq, wk, wv, g_qlat, g_kvlat, gains, b_f):
    t = fox.shape[0]
    tm = min(PROJ_TILE, t)
    c_tab, s1_tab, s2_tab, rope_mask = tables

    def body(fox_ref, lat_ref, c_ref, s1_ref, s2_ref, rm_ref, gm_ref, gmt_ref, im_ref, gf_ref, gft_ref, if_ref,
             wq_ref, wk_ref, wv_ref, gql_ref, gkvl_ref, gq_ref, gk_ref, gkr_ref, gfq_ref, gfk_ref, bf_ref,
             fq_ref, fk_ref, fv_ref, cc_ref, mq_ref, mk_ref, mv_ref, carry):
        i = pl.program_id(0)
        ct, s1, s2 = c_ref[...], s1_ref[...], s2_ref[...]

        xq = fox_ref[:, 0:512]
        fq_ref[...] = (xq * _group_rms(xq, gf_ref[...], gft_ref[...], if_ref[...]) * (gfq_ref[...] * FOX_SCALE)).astype(BF16)
        xk = fox_ref[:, 512:1024]
        fk_ref[...] = (xk * _group_rms(xk, gf_ref[...], gft_ref[...], if_ref[...]) * gfk_ref[...]).astype(BF16)
        fv_ref[...] = fox_ref[:, 1024:1536].astype(BF16)

        @pl.when(i == 0)
        def _():
            carry[...] = jnp.zeros_like(carry)

        logf = _log_sigmoid(lat_ref[:, LAT_F:LAT_F + N_HEADS] + bf_ref[...])
        cc_ref[...] = jnp.dot(_tri(tm, True), logf, precision=lax.Precision.HIGHEST, preferred_element_type=F32) + carry[...]
        carry[...] += _colsum(logf)

        qlat_n, _ = _rms_fwd(lat_ref[:, LAT_Q:LAT_Q + Q_LORA], gql_ref[...])
        p = _dot(qlat_n.astype(BF16), wq_ref[...], NT)
        y = p * _group_rms(p, gm_ref[...], gmt_ref[...], im_ref[...]) * gq_ref[...]
        for h in range(N_HEADS):
            w = slice(h * LANES, (h + 1) * LANES)
            mq_ref[:, w] = _rope_fwd(y[:, w], ct, s1, s2).astype(BF16)

        kv_n, _ = _rms_fwd(lat_ref[:, LAT_KV:LAT_KV + KV_LORA], gkvl_ref[...])
        kv_b = kv_n.astype(BF16)
        pk = _dot(kv_b, wk_ref[...], NT)
        kn = pk * _group_rms(pk, gm_ref[...], gmt_ref[...], im_ref[...]) * gk_ref[...]
        rm = rm_ref[...]
        kr = pltpu.roll(lat_ref[:, LAT_KR:LAT_KR + LANES], HEAD_DIM, 1) * rm
        rr = lax.rsqrt(jnp.sum(kr * kr, axis=1, keepdims=True) * (1.0 / ROPE_DIM) + RMS_EPS)
        okr = _rope_fwd(kr * rr * gkr_ref[...], ct * rm, s1, s2)
        for h in range(N_HEADS):
            w = slice(h * LANES, (h + 1) * LANES)
            mk_ref[:, w] = (kn[:, w] + okr).astype(BF16)
        mv_ref[...] = _dot(kv_b, wv_ref[...], NT).astype(BF16)

    consts = [rope_mask, *mats, wq, wk, wv, g_qlat, g_kvlat, *gains, b_f]
    t512, tl = _tok(tm, 512), _tok(tm, LANES)
    return pl.pallas_call(
        body, name="prep_fwd", grid=(t // tm,),
        in_specs=[_tok(tm, 1536), _tok(tm, LAT_W), tl, tl, tl] + [_full(a.shape) for a in consts],
        out_specs=[t512, t512, t512, _tok(tm, N_HEADS), _tok(tm, N_HEADS * LANES), _tok(tm, N_HEADS * LANES), t512],
        out_shape=[_sds((t, 512), BF16), _sds((t, 512), BF16), _sds((t, 512), BF16), _sds((t, N_HEADS), F32),
                   _sds((t, N_HEADS * LANES), BF16), _sds((t, N_HEADS * LANES), BF16), _sds((t, 512), BF16)],
        scratch_shapes=[pltpu.VMEM((1, N_HEADS), F32)],
        compiler_params=pltpu.CompilerParams(dimension_semantics=("arbitrary",)),
    )(fox, lat, c_tab, s1_tab, s2_tab, *consts)


def _prep_bwd(fox, lat, tables, mats, wq, wk, wv, g_qlat, g_kvlat, gains, b_f, dfq, dfk, dfv, dc, dmq, dmk, dmv):
    t = fox.shape[0]
    tm = min(PROJ_TILE, t)
    n_t = t // tm
    c_tab, s1_tab, s2_tab, rope_mask = tables

    def body(fox_ref, lat_ref, c_ref, s1_ref, s2_ref, rm_ref, gm_ref, gmt_ref, im_ref, gf_ref, gft_ref, if_ref,
             wq_ref, wk_ref, wv_ref, gql_ref, gkvl_ref, gq_ref, gk_ref, gkr_ref, gfq_ref, gfk_ref, bf_ref,
             dfq_ref, dfk_ref, dfv_ref, dc_ref, dmq_ref, dmk_ref, dmv_ref,
             dfox_ref, dlat_ref, dwq_ref, dwk_ref, dwv_ref, o_gql, o_gkvl, o_gq, o_gk, o_gkr, o_gfq, o_gfk, o_bf,
             carry, lscr, wq_acc, wk_acc, wv_acc, dyscr):
        i = pl.program_id(0)
        ct, s1, s2, rm = c_ref[...], s1_ref[...], s2_ref[...], rm_ref[...]

        @pl.when(i == 0)
        def _():
            for ref in [carry, wq_acc, wk_acc, wv_acc, o_gql, o_gkvl, o_gq, o_gk, o_gkr, o_gfq, o_gfk, o_bf]:
                ref[...] = jnp.zeros_like(ref)

        def group_rms_bwd(dy, x, r, gain, g, g_t, inv):
            xn = x * r
            dyg = dy * gain
            mu = _split_dot(_split_dot(dyg * xn, g) * inv, g_t)
            return r * (dyg - xn * mu), dy * xn

        gf, gft, invf = gf_ref[...], gft_ref[...], if_ref[...]
        xq = fox_ref[:, 0:512]
        dxq, rows = group_rms_bwd(dfq_ref[...] * FOX_SCALE, xq, _group_rms(xq, gf, gft, invf), gfq_ref[...], gf, gft, invf)
        dfox_ref[:, 0:512] = dxq.astype(BF16)
        o_gfq[...] += _colsum(rows)
        xk = fox_ref[:, 512:1024]
        dxk, rows = group_rms_bwd(dfk_ref[...], xk, _group_rms(xk, gf, gft, invf), gfk_ref[...], gf, gft, invf)
        dfox_ref[:, 512:1024] = dxk.astype(BF16)
        o_gfk[...] += _colsum(rows)
        dfox_ref[:, 1024:1536] = dfv_ref[...].astype(BF16)

        lscr[...] = jnp.zeros_like(lscr)

        xql = lat_ref[:, LAT_Q:LAT_Q + Q_LORA]
        qlat_n, r_ql = _rms_fwd(xql, gql_ref[...])
        qlat_b = qlat_n.astype(BF16)
        gm, gmt, invm = gm_ref[...], gmt_ref[...], im_ref[...]
        p = _dot(qlat_b, wq_ref[...], NT)
        for h in range(N_HEADS):
            w = slice(h * LANES, (h + 1) * LANES)
            dyscr[:, w] = _rope_bwd(dmq_ref[:, w], ct, s1, s2)
        dp, rows = group_rms_bwd(dyscr[...], p, _group_rms(p, gm, gmt, invm), gq_ref[...], gm, gmt, invm)
        o_gq[...] += _colsum(rows)
        dp_b = dp.astype(BF16)
        wq_acc[...] += _dot(dp_b, qlat_b, TN)
        dxql, rows = _rms_bwd(_dot(dp_b, wq_ref[...]), xql, gql_ref[...], r_ql)
        lscr[:, LAT_Q:LAT_Q + Q_LORA] = dxql
        o_gql[...] += _colsum(rows)

        xkv = lat_ref[:, LAT_KV:LAT_KV + KV_LORA]
        kv_n, r_kv = _rms_fwd(xkv, gkvl_ref[...])
        kv_b = kv_n.astype(BF16)
        pk = _dot(kv_b, wk_ref[...], NT)
        dpk, rows = group_rms_bwd(dmk_ref[...], pk, _group_rms(pk, gm, gmt, invm), gk_ref[...], gm, gmt, invm)
        o_gk[...] += _colsum(rows)
        dpk_b = dpk.astype(BF16)
        dv_b = dmv_ref[...].astype(BF16)
        wk_acc[...] += _dot(dpk_b, kv_b, TN)
        wv_acc[...] += _dot(dv_b, kv_b, TN)
        dxkv, rows = _rms_bwd(_dot(dpk_b, wk_ref[...]) + _dot(dv_b, wv_ref[...]), xkv, gkvl_ref[...], r_kv)
        lscr[:, LAT_KV:LAT_KV + KV_LORA] = dxkv
        o_gkvl[...] += _colsum(rows)

        dokr = dmk_ref[:, 0:LANES]
        for h in range(1, N_HEADS):
            dokr = dokr + dmk_ref[:, h * LANES:(h + 1) * LANES]
        dykr = _rope_bwd(dokr * rm, ct * rm, s1, s2)
        kr = pltpu.roll(lat_ref[:, LAT_KR:LAT_KR + LANES], HEAD_DIM, 1) * rm
        rr = lax.rsqrt(jnp.sum(kr * kr, axis=1, keepdims=True) * (1.0 / ROPE_DIM) + RMS_EPS)
        krn = kr * rr
        dyg = dykr * gkr_ref[...]
        dkr = rr * (dyg - krn * (jnp.sum(dyg * krn, axis=1, keepdims=True) * (1.0 / ROPE_DIM)))
        o_gkr[...] += _colsum(dykr * krn)
        lscr[:, LAT_KR:LAT_KR + LANES] = pltpu.roll(dkr, LANES - HEAD_DIM, 1)

        dcv = dc_ref[...]
        dlogf = jnp.dot(_tri(tm, False), dcv, precision=lax.Precision.HIGHEST, preferred_element_type=F32) + carry[...]
        carry[...] += _colsum(dcv)
        dz = dlogf * jax.nn.sigmoid(-(lat_ref[:, LAT_F:LAT_F + N_HEADS] + bf_ref[...]))
        lscr[:, LAT_F:LAT_F + N_HEADS] = dz
        o_bf[...] += _colsum(dz)

        dlat_ref[...] = lscr[...].astype(BF16)

        @pl.when(i == n_t - 1)
        def _():
            dwq_ref[...] = wq_acc[...].astype(BF16)
            dwk_ref[...] = wk_acc[...].astype(BF16)
            dwv_ref[...] = wv_acc[...].astype(BF16)

    consts = [rope_mask, *mats, wq, wk, wv, g_qlat, g_kvlat, *gains, b_f]

    def rtok(n):
        return pl.BlockSpec((tm, n), lambda i: (n_t - 1 - i, 0))

    sums = [(1, Q_LORA), (1, KV_LORA), (1, N_HEADS * LANES), (1, N_HEADS * LANES), (1, LANES), (1, 512), (1, 512), (1, N_HEADS)]
    return pl.pallas_call(
        body, name="prep_bwd", grid=(n_t,),
        in_specs=[rtok(1536), rtok(LAT_W), rtok(LANES), rtok(LANES), rtok(LANES)] + [_full(a.shape) for a in consts]
        + [rtok(512), rtok(512), rtok(512), rtok(N_HEADS), rtok(N_HEADS * LANES), rtok(N_HEADS * LANES), rtok(512)],
        out_specs=[rtok(1536), rtok(LAT_W), _full(wq.shape), _full(wk.shape), _full(wv.shape)] + [_full(s) for s in sums],
        out_shape=[_sds((t, 1536), BF16), _sds((t, LAT_W), BF16), _sds(wq.shape, BF16), _sds(wk.shape, BF16), _sds(wv.shape, BF16)]
        + [_sds(s, F32) for s in sums],
        scratch_shapes=[pltpu.VMEM((1, N_HEADS), F32), pltpu.VMEM((tm, LAT_W), F32), pltpu.VMEM(wq.shape, F32),
                        pltpu.VMEM(wk.shape, F32), pltpu.VMEM(wv.shape, F32), pltpu.VMEM((tm, N_HEADS * LANES), F32)],
        compiler_params=pltpu.CompilerParams(dimension_semantics=("arbitrary",)),
    )(fox, lat, c_tab, s1_tab, s2_tab, *consts, dfq, dfk, dfv, dc, dmq, dmk, dmv)


def _qk_norms(q, k, *, lanes, name):
    t = q.shape[0]
    tm = min(TOK_TILE, t)

    def body(q_ref, k_ref, qn_ref, kmax_ref):
        i = pl.program_id(0)

        @pl.when(i == 0)
        def _():
            kmax_ref[...] = jnp.zeros_like(kmax_ref)

        for h in range(N_HEADS):
            qh = q_ref[:, h * lanes:(h + 1) * lanes].astype(F32)
            kh = k_ref[:, h * lanes:(h + 1) * lanes].astype(F32)
            qn_ref[:, h:h + 1] = jnp.sum(qh * qh, axis=1, keepdims=True)
            kmax = jnp.max(jnp.sum(kh * kh, axis=1, keepdims=True), axis=0, keepdims=True)
            kmax_ref[:, h:h + 1] = jnp.maximum(kmax_ref[:, h:h + 1], kmax)

    return pl.pallas_call(
        body, name=name, grid=(t // tm,), in_specs=[_tok(tm, N_HEADS * lanes), _tok(tm, N_HEADS * lanes)],
        out_specs=[_tok(tm, N_HEADS), _full((1, N_HEADS))], out_shape=[_sds((t, N_HEADS), F32), _sds((1, N_HEADS), F32)],
        compiler_params=pltpu.CompilerParams(dimension_semantics=("arbitrary",)),
    )(q, k)


def _logit_bound(q, k, *, lanes, scale, name):
    qn, kmax = _qk_norms(q, k, lanes=lanes, name=name)
    bound = jnp.sqrt(qn * kmax) * (scale * (1.0 + 2.0 ** -10)) + 2.0 ** -10
    flag = (jnp.max(bound) <= FIXED_SHIFT_MAX_BOUND).astype(F32).reshape(1, 1)
    return bound, flag


def _attn_fwd(q, k, v, bound, fixed_ok, c, c_t, *, lanes, scale, name, exchange=None):
    t = q.shape[0]
    tq = min(ATT_TILE, t)
    n_q = t // tq
    hd = HEAD_DIM
    ch = min(ATT_COL_CHUNK, tq)
    decay = c is not None

    def body(*refs):
        if decay:
            q_ref, k_ref, v_ref, b_ref, ok_ref, c_ref, ct_ref, o_ref, o32_ref, lse_ref, m_scr, l_scr, acc = refs
        else:
            q_ref, k_ref, v_ref, b_ref, ok_ref, o_ref, o32_ref, lse_ref, m_scr, l_scr, acc = refs
        i, j = pl.program_id(0), pl.program_id(1)
        fixed = ok_ref[0, 0] > 0.5

        @pl.when(j == 0)
        def _():
            m_scr[...] = jnp.full_like(m_scr, MASK_VALUE)
            l_scr[...] = jnp.zeros_like(l_scr)
            acc[...] = jnp.zeros_like(acc)

        def fixed_step(diagonal):
            for h in range(N_HEADS):
                wl = slice(h * lanes, (h + 1) * lanes)
                w = slice(h * hd, (h + 1) * hd)
                qh = q_ref[:, wl]
                row = (c_ref[:, h:h + 1] - b_ref[:, h:h + 1]) if decay else -b_ref[:, h:h + 1]
                l_new = jnp.zeros((tq, 1), F32)
                o_hi = jnp.zeros((tq, hd), F32)
                o_lo = jnp.zeros((tq, hd), F32)
                for cc in range(tq // ch):
                    cols = slice(cc * ch, (cc + 1) * ch)
                    s = _dot(qh, k_ref[cols, wl], NT)
                    if scale != 1.0:
                        s = s * scale
                    s = s + ((row - ct_ref[h:h + 1, cols]) if decay else row)
                    if diagonal:
                        keep = (lax.broadcasted_iota(jnp.int32, (tq, ch), 0)
                                >= lax.broadcasted_iota(jnp.int32, (tq, ch), 1) + cc * ch)
                        s = jnp.where(keep, s, MASK_VALUE)
                    p = jnp.exp(s)
                    l_new = l_new + jnp.sum(p, axis=1, keepdims=True)
                    p_b = p.astype(BF16)
                    o_hi = o_hi + _dot(p_b, v_ref[cols, w])
                    if decay:
                        o_lo = o_lo + _dot((p - p_b.astype(F32)).astype(BF16), v_ref[cols, w])
                l_scr[h] += l_new
                acc[0, :, w] += o_hi
                if decay:
                    acc[1, :, w] += o_lo

        def step(diagonal):
            if diagonal:
                keep = lax.broadcasted_iota(jnp.int32, (tq, tq), 0) >= lax.broadcasted_iota(jnp.int32, (tq, tq), 1)
            for h in range(N_HEADS):
                s = _dot(q_ref[:, h * lanes:(h + 1) * lanes], k_ref[:, h * lanes:(h + 1) * lanes], NT)
                if scale != 1.0:
                    s = s * scale
                if decay:
                    s = s + (c_ref[:, h:h + 1] - ct_ref[h:h + 1, :])
                if diagonal:
                    s = jnp.where(keep, s, MASK_VALUE)
                m_prev = m_scr[h]
                m_new = jnp.maximum(m_prev, jnp.max(s, axis=1, keepdims=True))
                alpha = jnp.exp(m_prev - m_new)
                p = jnp.exp(s - m_new)
                l_scr[h] = alpha * l_scr[h] + jnp.sum(p, axis=1, keepdims=True)
                w = slice(h * hd, (h + 1) * hd)
                p_b = p.astype(BF16)
                acc[0, :, w] = alpha * acc[0, :, w] + _dot(p_b, v_ref[:, w])
                if decay:
                    p_lo = (p - p_b.astype(F32)).astype(BF16)
                    acc[1, :, w] = alpha * acc[1, :, w] + _dot(p_lo, v_ref[:, w])
                m_scr[h] = m_new

        for diagonal, here in ((False, j < i), (True, j == i)):
            @pl.when(here & fixed)
            def _():
                fixed_step(diagonal)

            @pl.when(here & jnp.logical_not(fixed))
            def _():
                step(diagonal)

        @pl.when(j == i)
        def _():
            for h in range(N_HEADS):
                w = slice(h * hd, (h + 1) * hd)
                l = l_scr[h]
                o_ref[:, w] = (acc[0, :, w] / l).astype(BF16)
                o32_ref[:, w] = ((acc[0, :, w] + acc[1, :, w]) if decay else acc[0, :, w]) / l
                lse_ref[:, h:h + 1] = jnp.where(fixed, b_ref[:, h:h + 1], m_scr[h]) + jnp.log(l)

    qspec = lambda n: pl.BlockSpec((tq, n), lambda i, j: (i, 0))
    kspec = lambda n: pl.BlockSpec((tq, n), lambda i, j: (jnp.minimum(i, j), 0))
    in_specs = [qspec(N_HEADS * lanes), kspec(N_HEADS * lanes), kspec(512), qspec(N_HEADS),
                pl.BlockSpec(memory_space=pltpu.SMEM)]
    args = [q, k, v, bound, fixed_ok]
    if decay:
        in_specs += [qspec(N_HEADS), pl.BlockSpec((N_HEADS, tq), lambda i, j: (0, jnp.minimum(i, j)))]
        args += [c, c_t]
    return _gridded_call(
        body, name=name, grid=(n_q, n_q), in_specs=in_specs, out_specs=[qspec(512), qspec(512), qspec(N_HEADS)],
        out_shape=[_sds((t, 512), BF16), _sds((t, 512), F32), _sds((t, N_HEADS), F32)],
        scratch_shapes=[pltpu.VMEM((N_HEADS, tq, 1), F32), pltpu.VMEM((N_HEADS, tq, 1), F32), pltpu.VMEM((2, tq, 512), F32)],
        args=args, exchange=exchange)


def _attn_bwd(q, k, v, do, delta, lse, c, c_t, *, lanes, scale, name, exchange=None):
    t = q.shape[0]
    tq = min(ATT_TILE, t)
    n_q = t // tq
    hd = HEAD_DIM
    decay = c is not None

    def body(*refs):
        if decay:
            q_ref, k_ref, v_ref, do_ref, delta_ref, lse_ref, c_ref, ct_ref, dq_hbm, dk_ref, dv_ref, dct_ref, dq_ref = refs
        else:
            q_ref, k_ref, v_ref, do_ref, delta_ref, lse_ref, dq_hbm, dk_ref, dv_ref, dq_ref = refs
        j, i = pl.program_id(0), pl.program_id(1)

        @pl.when((j == 0) & (i == 0))
        def _():
            dq_ref[...] = jnp.zeros_like(dq_ref)

        @pl.when(i == j)
        def _():
            dk_ref[...] = jnp.zeros_like(dk_ref)
            dv_ref[...] = jnp.zeros_like(dv_ref)
            if decay:
                dct_ref[...] = jnp.zeros_like(dct_ref)

        def step(diagonal):
            if diagonal:
                keep = lax.broadcasted_iota(jnp.int32, (tq, tq), 0) >= lax.broadcasted_iota(jnp.int32, (tq, tq), 1)
            rows = pl.ds(pl.multiple_of(i * tq, tq), tq)
            for h in range(N_HEADS):
                wl = slice(h * lanes, (h + 1) * lanes)
                w = slice(h * hd, (h + 1) * hd)
                qh, kh = q_ref[:, wl], k_ref[:, wl]
                s = _dot(qh, kh, NT)
                if scale != 1.0:
                    s = s * scale
                if decay:
                    s = s + (c_ref[:, h:h + 1] - ct_ref[h:h + 1, :])
                if diagonal:
                    s = jnp.where(keep, s, MASK_VALUE)
                p = jnp.exp(s - lse_ref[:, h:h + 1])
                doh = do_ref[:, w]
                dv_ref[:, w] += _dot(p.astype(BF16), doh, TN)
                dp = _dot(doh, v_ref[:, w], NT)
                ds = p * (dp - delta_ref[:, h:h + 1])
                if decay:
                    dct_ref[h:h + 1, :] -= _colsum(ds)
                if scale != 1.0:
                    ds = ds * scale
                ds_b = ds.astype(BF16)
                dk_ref[:, wl] += _dot(ds_b, qh, TN)
                dq_ref[rows, wl] += _dot(ds_b, kh)

        @pl.when(i > j)
        def _():
            step(False)

        @pl.when(i == j)
        def _():
            step(True)

        @pl.when((j == n_q - 1) & (i == n_q - 1))
        def _():
            pltpu.sync_copy(dq_ref, dq_hbm)

    qspec = lambda n: pl.BlockSpec((tq, n), lambda j, i: (jnp.maximum(i, j), 0))
    kspec = lambda n: pl.BlockSpec((tq, n), lambda j, i: (j, 0))
    in_specs = [qspec(N_HEADS * lanes), kspec(N_HEADS * lanes), kspec(512), qspec(512), qspec(N_HEADS), qspec(N_HEADS)]
    out_specs = [pl.BlockSpec(memory_space=pl.ANY), kspec(N_HEADS * lanes), kspec(512)]
    out_shape = [_sds((t, N_HEADS * lanes), F32), _sds((t, N_HEADS * lanes), F32), _sds((t, 512), F32)]
    args = [q, k, v, do, delta, lse]
    if decay:
        ctspec = pl.BlockSpec((N_HEADS, tq), lambda j, i: (0, j))
        in_specs += [qspec(N_HEADS), ctspec]
        out_specs.append(ctspec)
        out_shape.append(_sds((N_HEADS, t), F32))
        args += [c, c_t]
    return _gridded_call(body, name=name, grid=(n_q, n_q), in_specs=in_specs, out_specs=out_specs, out_shape=out_shape,
                         scratch_shapes=[pltpu.VMEM((t, N_HEADS * lanes), F32)], args=args, exchange=exchange)


def _mix_fwd(x, y_mla, y_fox, gates, b_gate, wbm_t, wbf_t, wo):
    t = x.shape[0]
    tm = min(PROJ_TILE, t)

    def body(x_ref, ym_ref, yf_ref, gt_ref, bg_ref, wbm_ref, wbf_ref, wo_ref, out_ref):
        um = _dot(ym_ref[...], wbm_ref[...], NT)
        uf = _dot(yf_ref[...], wbf_ref[...], NT)
        sm = jax.nn.sigmoid(gt_ref[:, 0:D_MODEL] + bg_ref[0:1, :])
        sf = jax.nn.sigmoid(gt_ref[:, D_MODEL:2 * D_MODEL] + bg_ref[1:2, :])
        mixed = sm * um + sf * uf
        out_ref[...] = x_ref[...] + _dot(mixed.astype(BF16), wo_ref[...])

    tok = _tok(tm, D_MODEL)
    return pl.pallas_call(
        body, name="mix_fwd", grid=(t // tm,),
        in_specs=[tok, _tok(tm, 512), _tok(tm, 512), _tok(tm, 2 * D_MODEL), _full((2, D_MODEL)), _full(wbm_t.shape),
                  _full(wbf_t.shape), _full(wo.shape)],
        out_specs=tok, out_shape=_sds((t, D_MODEL), F32),
        compiler_params=pltpu.CompilerParams(dimension_semantics=("arbitrary",)),
    )(x, y_mla, y_fox, gates, b_gate, wbm_t, wbf_t, wo)


def _mix_bwd(dx, y_mla, y_fox, y_mla32, y_fox32, gates, b_gate, wbm_t, wbf_t, wo):
    t = dx.shape[0]
    tm = min(PROJ_TILE, t)

    def body(dx_ref, ym_ref, yf_ref, ym32_ref, yf32_ref, gt_ref, bg_ref, wbm_ref, wbf_ref, wo_ref,
             dym_ref, dyf_ref, dlm_ref, dlf_ref, dgt_ref, mixed_ref, dum_ref, duf_ref, dxb_ref, dbg_ref, prod):
        i = pl.program_id(0)
        dxb = dx_ref[...].astype(BF16)
        dxb_ref[...] = dxb
        dmixed = _dot(dxb, wo_ref[...], NT)
        um = _dot(ym_ref[...], wbm_ref[...], NT)
        uf = _dot(yf_ref[...], wbf_ref[...], NT)
        sm = jax.nn.sigmoid(gt_ref[:, 0:D_MODEL] + bg_ref[0:1, :])
        sf = jax.nn.sigmoid(gt_ref[:, D_MODEL:2 * D_MODEL] + bg_ref[1:2, :])
        mixed_ref[...] = (sm * um + sf * uf).astype(BF16)
        dum = (dmixed * sm).astype(BF16)
        duf = (dmixed * sf).astype(BF16)
        dum_ref[...] = dum
        duf_ref[...] = duf
        dgm = dmixed * um * (sm * (1.0 - sm))
        dgf = dmixed * uf * (sf * (1.0 - sf))
        dgt_ref[:, 0:D_MODEL] = dgm.astype(BF16)
        dgt_ref[:, D_MODEL:2 * D_MODEL] = dgf.astype(BF16)
        for du, wb_ref, y32_ref, dy_ref, dl_ref in ((dum, wbm_ref, ym32_ref, dym_ref, dlm_ref),
                                                    (duf, wbf_ref, yf32_ref, dyf_ref, dlf_ref)):
            dy = _dot(du, wb_ref[...])
            dy_ref[...] = dy.astype(BF16)
            prod[...] = dy.astype(BF16).astype(F32) * y32_ref[...]
            for h in range(N_HEADS):
                dl_ref[:, h:h + 1] = jnp.sum(prod[:, h * HEAD_DIM:(h + 1) * HEAD_DIM], axis=1, keepdims=True)

        @pl.when(i == 0)
        def _():
            dbg_ref[...] = jnp.zeros_like(dbg_ref)

        dbg_ref[0:1, :] += _colsum(dgm)
        dbg_ref[1:2, :] += _colsum(dgf)

    tok = _tok(tm, D_MODEL)
    tokb = _sds((t, D_MODEL), BF16)
    t512, t8 = _tok(tm, 512), _tok(tm, N_HEADS)
    return pl.pallas_call(
        body, name="mix_bwd", grid=(t // tm,),
        in_specs=[tok, t512, t512, t512, t512, _tok(tm, 2 * D_MODEL), _full((2, D_MODEL)), _full(wbm_t.shape),
                  _full(wbf_t.shape), _full(wo.shape)],
        out_specs=[t512, t512, t8, t8, _tok(tm, 2 * D_MODEL), tok, tok, tok, tok, _full((2, D_MODEL))],
        out_shape=[_sds((t, 512), BF16), _sds((t, 512), BF16), _sds((t, N_HEADS), F32), _sds((t, N_HEADS), F32),
                   _sds((t, 2 * D_MODEL), BF16), tokb, tokb, tokb, tokb, _sds((2, D_MODEL), F32)],
        scratch_shapes=[pltpu.VMEM((tm, 512), F32)],
        compiler_params=pltpu.CompilerParams(dimension_semantics=("arbitrary",)),
    )(dx, y_mla, y_fox, y_mla32, y_fox32, gates, b_gate, wbm_t, wbf_t, wo)


def _my_position():
    x, y, c = lax.axis_index("x"), lax.axis_index("y"), lax.axis_index("c")
    return x, y, c, 4 * x + 2 * y + c


def _peer(x, y, c, mask):
    px = 1 - x if mask & 4 else x
    py = 1 - y if mask & 2 else y
    pc = 1 - c if mask & 1 else c
    return (px, py, pc), 4 * px + 2 * py + pc


def _chip_peer(x, y, km):
    px = 1 - x if km & 2 else x
    py = 1 - y if km & 1 else y
    return px, py, 2 * px + py


_HBM = pl.BlockSpec(memory_space=pl.ANY)


def _wait_all(copies):
    for cp in copies:
        cp.wait()


class _ChipExchange:
    def __init__(self, gather, arrays):
        self.gather, self.arrays = gather, list(arrays)
        n = len(self.arrays)
        self.out_shape = [_sds((N_DEV * a.shape[0],) + a.shape[1:], a.dtype) if gather else _sds(a.shape, a.dtype)
                          for a in self.arrays]
        self.scratch_shapes = [pltpu.SemaphoreType.DMA((n, N_CHIP)), pltpu.SemaphoreType.DMA((n, N_CHIP)),
                               pltpu.SemaphoreType.DMA((n,))]

    def copies(self, srcs, dsts, send_sems, recv_sems, local_sems):
        x, y, c, me = _my_position()
        q_me = 2 * x + y
        out = []
        for a in range(len(self.arrays)):
            if self.gather:
                r = srcs[a].shape[0]
                local_src, dst = srcs[a], dsts[a].at[pl.ds(me * r, r)]
            else:
                local_src, dst = srcs[a].at[q_me], dsts[a].at[q_me]
            out.append(pltpu.make_async_copy(local_src, dst, local_sems.at[a]))
            for km in range(1, N_CHIP):
                px, py, q_peer = _chip_peer(x, y, km)
                out.append(pltpu.make_async_remote_copy(
                    src_ref=srcs[a] if self.gather else srcs[a].at[q_peer], dst_ref=dst, send_sem=send_sems.at[a, km],
                    recv_sem=recv_sems.at[a, km], device_id=(px, py, c), device_id_type=MESH))
        return out

    def standalone(self, name):
        n = len(self.arrays)

        def body(*refs):
            copies = self.copies(refs[:n], refs[n:2 * n], *refs[2 * n:])
            for cp in copies:
                cp.start()
            _wait_all(copies)

        return pl.pallas_call(body, name=name, in_specs=[_HBM] * n, out_specs=[_HBM] * n, out_shape=self.out_shape,
                              scratch_shapes=self.scratch_shapes)(*self.arrays)


def _gridded_call(body, *, name, grid, in_specs, out_specs, out_shape, scratch_shapes, args, exchange=None):
    params = pltpu.CompilerParams(dimension_semantics=("arbitrary",) * len(grid))
    if exchange is None:
        return pl.pallas_call(body, name=name, grid=grid, in_specs=in_specs, out_specs=out_specs, out_shape=out_shape,
                              scratch_shapes=scratch_shapes, compiler_params=params)(*args), None
    n_in, n_out, n_scr, n_x = len(in_specs), len(out_specs), len(scratch_shapes), len(exchange.arrays)

    def carrier(*refs):
        ins, x_src, refs = refs[:n_in], refs[n_in:n_in + n_x], refs[n_in + n_x:]
        outs, x_dst, refs = refs[:n_out], refs[n_out:n_out + n_x], refs[n_out + n_x:]
        copies = exchange.copies(x_src, x_dst, *refs[n_scr:])
        pids = [pl.program_id(d) for d in range(len(grid))]
        first = functools.reduce(jnp.logical_and, [p == 0 for p in pids])
        last = functools.reduce(jnp.logical_and, [p == g - 1 for p, g in zip(pids, grid)])

        @pl.when(first)
        def _():
            for cp in copies:
                cp.start()

        body(*ins, *outs, *refs[:n_scr])

        @pl.when(last)
        def _():
            _wait_all(copies)

    res = pl.pallas_call(
        carrier, name=name, grid=grid, in_specs=list(in_specs) + [_HBM] * n_x, out_specs=list(out_specs) + [_HBM] * n_x,
        out_shape=list(out_shape) + exchange.out_shape, scratch_shapes=list(scratch_shapes) + exchange.scratch_shapes,
        compiler_params=params)(*args, *exchange.arrays)
    return res[:n_out], res[n_out:]


def _gather_over_cores(arrays, name):
    n_arr = len(arrays)

    def body(*refs):
        srcs, dsts = refs[:n_arr], refs[n_arr:2 * n_arr]
        send_sems, recv_sems = refs[2 * n_arr:]
        x, y, c, _ = _my_position()
        copies = []
        for a in range(n_arr):
            r = srcs[a].shape[0] // N_DEV
            for q in range(N_CHIP):
                rows = pl.ds((2 * q + c) * r, r)
                copies.append(pltpu.make_async_remote_copy(
                    src_ref=srcs[a].at[rows], dst_ref=dsts[a].at[rows], send_sem=send_sems.at[a, q],
                    recv_sem=recv_sems.at[a, q], device_id=(x, y, 1 - c), device_id_type=MESH))
        for cp in copies:
            cp.start()
        _wait_all(copies)

    return pl.pallas_call(
        body, name=name, in_specs=[_HBM] * n_arr, out_specs=[_HBM] * n_arr,
        out_shape=[_sds(a.shape, a.dtype) for a in arrays], input_output_aliases={a: a for a in range(n_arr)},
        scratch_shapes=[pltpu.SemaphoreType.DMA((n_arr, N_CHIP)), pltpu.SemaphoreType.DMA((n_arr, N_CHIP))],
    )(*arrays)


def _grads_to_sibling(grads, name):
    n_arr = len(grads)

    def body(*refs):
        srcs, dsts = refs[:n_arr], refs[n_arr:2 * n_arr]
        send_sems, recv_sems = refs[2 * n_arr:]
        x, y, c, _ = _my_position()
        copies = []
        for a in range(n_arr):
            r = srcs[a].shape[0] // N_DEV
            for q in range(N_CHIP):
                copies.append(pltpu.make_async_remote_copy(
                    src_ref=srcs[a].at[pl.ds((2 * q + 1 - c) * r, r)], dst_ref=dsts[a].at[q], send_sem=send_sems.at[a, q],
                    recv_sem=recv_sems.at[a, q], device_id=(x, y, 1 - c), device_id_type=MESH))
        for cp in copies:
            cp.start()
        _wait_all(copies)

    return pl.pallas_call(
        body, name=name, in_specs=[_HBM] * n_arr, out_specs=[_HBM] * n_arr,
        out_shape=[_sds((N_CHIP, g.shape[0] // N_DEV) + g.shape[1:], g.dtype) for g in grads],
        scratch_shapes=[pltpu.SemaphoreType.DMA((n_arr, N_CHIP)), pltpu.SemaphoreType.DMA((n_arr, N_CHIP))],
    )(*grads)


def _pair_sum(grad, from_sibling, name):
    r, n = from_sibling.shape[1:]

    def body(g_ref, s_ref, o_ref):
        c = lax.axis_index("c")
        o_ref[...] = (g_ref[c].astype(F32) + s_ref[...].astype(F32)).astype(BF16)

    return pl.pallas_call(
        body, name=name, grid=(N_CHIP,),
        in_specs=[pl.BlockSpec((None, 2, r, n), lambda q: (q, 0, 0, 0)), pl.BlockSpec((None, r, n), lambda q: (q, 0, 0))],
        out_specs=pl.BlockSpec((None, r, n), lambda q: (q, 0, 0)), out_shape=_sds((N_CHIP, r, n), BF16),
    )(grad.reshape(N_CHIP, 2, r, n), from_sibling)


def _all_reduce_small(vec):
    r = vec.shape[0]

    def body(v_ref, o_ref, buf, send_sems, recv_sems):
        x, y, c, me = _my_position()
        buf[me] = v_ref[...]
        copies = []
        for mask in range(1, N_DEV):
            peer, _ = _peer(x, y, c, mask)
            cp = pltpu.make_async_remote_copy(src_ref=v_ref, dst_ref=buf.at[me], send_sem=send_sems.at[mask],
                                              recv_sem=recv_sems.at[mask], device_id=peer, device_id_type=MESH)
            cp.start()
            copies.append(cp)
        for cp in copies:
            cp.wait()
        total = buf[0]
        for s in range(1, N_DEV):
            total = total + buf[s]
        o_ref[...] = total

    vm = pl.BlockSpec(memory_space=pltpu.VMEM)
    return pl.pallas_call(
        body, name="all_reduce_small", in_specs=[vm], out_specs=vm, out_shape=_sds(vec.shape, F32),
        scratch_shapes=[pltpu.VMEM((N_DEV, r, 128), F32), pltpu.SemaphoreType.DMA((N_DEV,)), pltpu.SemaphoreType.DMA((N_DEV,))],
    )(vec)


def _adamw_math(w, g, m, v):
    m = ADAM_B1 * m + (1.0 - ADAM_B1) * g
    v = ADAM_B2 * v + (1.0 - ADAM_B2) * (g * g)
    m_hat = m / (1.0 - ADAM_B1 ** ADAM_STEP)
    v_hat = v / (1.0 - ADAM_B2 ** ADAM_STEP)
    delta = -ADAM_LR * (m_hat / (jnp.sqrt(v_hat) + ADAM_EPS) + ADAM_WD * w)
    return delta, m, v


def _reduce_adamw(slots, w, m, v, *, transpose, name):
    r, n = slots.shape[1:]
    padded = not transpose and w.shape[0] != r

    def body(s_ref, w_ref, m_ref, v_ref, g_ref, d_ref, nm_ref, nv_ref, *scr):
        g = s_ref[0].astype(F32)
        for s in range(1, slots.shape[0]):
            g = g + s_ref[s].astype(F32)
        if transpose:
            scr[0][...] = g.T
            g = scr[0][:, 0:w_ref.shape[1]]
        elif padded:
            scr[0][...] = g
            g = scr[0][0:w_ref.shape[0], :]
        g_ref[...] = g
        d_ref[...], nm_ref[...], nv_ref[...] = _adamw_math(w_ref[...], g, m_ref[...], v_ref[...])

    out = _sds(w.shape, F32)
    return pl.pallas_call(
        body, name=name, out_shape=[out, out, out, out],
        scratch_shapes=[pltpu.VMEM((n, r) if transpose else (r, n), F32)] if transpose or padded else [],
    )(slots, w, m, v)


def _adamw(g, w, m, v, name):
    def body(g_ref, w_ref, m_ref, v_ref, d_ref, nm_ref, nv_ref):
        d_ref[...], nm_ref[...], nv_ref[...] = _adamw_math(w_ref[...], g_ref[...], m_ref[...], v_ref[...])

    out = _sds(w.shape, F32)
    return pl.pallas_call(body, name=name, out_shape=[out, out, out])(g, w, m, v)


_SMALL = ["ffn1_norm", "mix_norm", "ffn2_norm", "mla_q_lat_norm", "mla_kv_lat_norm", "mla_q_nope_gain", "mla_q_rope_gain",
          "mla_k_nope_gain", "mla_k_rope_gain", "fox_q_gain", "fox_k_gain", "fox_b_f"]
_WEIGHTS = ["ffn1_norm", "ffn1_w_gate", "ffn1_w_up", "ffn1_w_down", "mix_norm", "w_in", "mla_q_lat_norm", "mla_w_qb",
            "mla_kv_lat_norm", "mla_w_kvb", "mla_q_nope_gain", "mla_q_rope_gain", "mla_k_nope_gain", "mla_k_rope_gain",
            "fox_q_gain", "fox_k_gain", "fox_b_f", "w_branch_mla", "w_branch_fox", "b_gate", "w_o", "ffn2_norm",
            "ffn2_w_gate", "ffn2_w_up", "ffn2_w_down"]
_IN_Q, _IN_KV, _IN_KR, _IN_FOX, _IN_F, _IN_GATES = (0, 192), (192, 128), (320, 32), (352, 1536), (1888, 8), (1896, 2048)


def _rows(a, seg):
    return a[seg[0]:seg[0] + seg[1]]


def _split_w_in(win_t):
    z = lambda n: jnp.zeros((n, D_MODEL), win_t.dtype)
    lat = jnp.concatenate([_rows(win_t, _IN_Q), z(LAT_KV - Q_LORA), _rows(win_t, _IN_KV), _rows(win_t, _IN_KR),
                           _rows(win_t, _IN_F), z(LAT_W - LAT_F - N_HEADS)], axis=0)
    return _rows(win_t, _IN_GATES), _rows(win_t, _IN_FOX), lat


def _join_w_in(d_gates, d_fox, d_lat):
    return jnp.concatenate([d_lat[LAT_Q:LAT_Q + Q_LORA], d_lat[LAT_KV:LAT_KV + KV_LORA], d_lat[LAT_KR:LAT_KR + ROPE_DIM],
                            d_fox, d_lat[LAT_F:LAT_F + N_HEADS], d_gates], axis=0)


def kernel(x, positions, ffn1_norm, ffn1_w_gate, ffn1_w_up, ffn1_w_down, mix_norm, w_in, mla_q_lat_norm, mla_w_qb, mla_kv_lat_norm, mla_w_kvb, mla_q_nope_gain, mla_q_rope_gain, mla_k_nope_gain, mla_k_rope_gain, fox_q_gain, fox_k_gain, fox_b_f, w_branch_mla, w_branch_fox, b_gate, w_o, ffn2_norm, ffn2_w_gate, ffn2_w_up, ffn2_w_down, loss_target, m_ffn1_norm, m_ffn1_w_gate, m_ffn1_w_up, m_ffn1_w_down, m_mix_norm, m_w_in, m_mla_q_lat_norm, m_mla_w_qb, m_mla_kv_lat_norm, m_mla_w_kvb, m_mla_q_nope_gain, m_mla_q_rope_gain, m_mla_k_nope_gain, m_mla_k_rope_gain, m_fox_q_gain, m_fox_k_gain, m_fox_b_f, m_w_branch_mla, m_w_branch_fox, m_b_gate, m_w_o, m_ffn2_norm, m_ffn2_w_gate, m_ffn2_w_up, m_ffn2_w_down, v_ffn1_norm, v_ffn1_w_gate, v_ffn1_w_up, v_ffn1_w_down, v_mix_norm, v_w_in, v_mla_q_lat_norm, v_mla_w_qb, v_mla_kv_lat_norm, v_mla_w_kvb, v_mla_q_nope_gain, v_mla_q_rope_gain, v_mla_k_nope_gain, v_mla_k_rope_gain, v_fox_q_gain, v_fox_k_gain, v_fox_b_f, v_w_branch_mla, v_w_branch_fox, v_b_gate, v_w_o, v_ffn2_norm, v_ffn2_w_gate, v_ffn2_w_up, v_ffn2_w_down):
    env = dict(locals())
    strip = lambda n, a: a if n in _SMALL else a[0]
    W = {n: strip(n, env[n]) for n in _WEIGHTS}
    M = {n: strip(n, env["m_" + n]) for n in _WEIGHTS}
    V = {n: strip(n, env["v_" + n]) for n in _WEIGHTS}
    xs = x[0]
    t = xs.shape[0]

    col_split = ["ffn1_w_gate", "ffn1_w_up", "ffn2_w_gate", "ffn2_w_up", "mla_w_qb", "mla_w_kvb", "w_branch_mla", "w_branch_fox"]
    row_split = ["ffn1_w_down", "ffn2_w_down", "w_o"]
    pieces = {n: W[n].T.astype(BF16) for n in col_split}
    pieces.update({n: W[n].astype(BF16) for n in row_split})
    pieces["w_in"] = jnp.pad(W["w_in"].T.astype(BF16), ((0, W_IN_PIECE_PAD - W_IN_PIECE), (0, 0)))
    pieces["b_gate"] = W["b_gate"].T
    group_a = ["ffn1_w_gate", "ffn1_w_up", "ffn1_w_down"]
    group_b = ["w_in", "mla_w_qb", "mla_w_kvb", "w_branch_mla", "w_branch_fox", "w_o", "b_gate"]
    group_c = ["ffn2_w_gate", "ffn2_w_up", "ffn2_w_down"]
    gather = lambda group: _ChipExchange(True, [pieces[n] for n in group])
    G = dict(zip(group_a, _gather_over_cores(gather(group_a).standalone("gather_ici_a"), "gather_d2d_a")))

    inv_freq = ROPE_THETA ** (-jnp.arange(ROPE_HALF, dtype=F32) / ROPE_HALF)
    ang = positions[0].astype(F32)[:, None] * inv_freq
    tables, mats = _prep_tables(jnp.cos(ang), jnp.sin(ang)), _group_matrices()
    gains = _head_gains(*[W[n] for n in ["mla_q_nope_gain", "mla_q_rope_gain", "mla_k_nope_gain", "mla_k_rope_gain",
                                         "fox_q_gain", "fox_k_gain"]])

    (x1, a1, b1), got_b = _ffn_fwd(xs, W["ffn1_norm"], G["ffn1_w_gate"], G["ffn1_w_up"], G["ffn1_w_down"],
                                   exchange=gather(group_b))
    G.update(zip(group_b, _gather_over_cores(got_b, "gather_d2d_b")))
    win_t = G["w_in"].reshape(N_DEV, W_IN_PIECE_PAD, D_MODEL)[:, :W_IN_PIECE].reshape(N_DEV * W_IN_PIECE, D_MODEL)
    wgate_t, wfox_t, wlat_t = _split_w_in(win_t)
    bg = G["b_gate"].T
    gates, fox, lat = _proj_fwd(x1, W["mix_norm"], wgate_t, wfox_t, wlat_t)
    prep_args = (fox, lat, tables, mats, *_interleave_weights(G["mla_w_qb"], G["mla_w_kvb"]), W["mla_q_lat_norm"],
                 W["mla_kv_lat_norm"], gains, W["fox_b_f"])
    fq, fk, fv, c, mq, mk, mv = _prep_fwd(*prep_args)
    c_t = c.T
    b_fox, ok_fox = _logit_bound(fq, fk, lanes=HEAD_DIM, scale=1.0, name="fox_norms")
    b_mla, ok_mla = _logit_bound(mq, mk, lanes=MLA_QK_LANES, scale=MLA_SCALE, name="mla_norms")
    (y_fox, y_fox32, lse_fox), got_c = _attn_fwd(fq, fk, fv, b_fox, ok_fox, c, c_t, lanes=HEAD_DIM, scale=1.0, name="fox_fwd",
                                                 exchange=gather(group_c))
    G.update(zip(group_c, _gather_over_cores(got_c, "gather_d2d_c")))
    (y_mla, y_mla32, lse_mla), _ = _attn_fwd(mq, mk, mv, b_mla, ok_mla, None, None, lanes=MLA_QK_LANES, scale=MLA_SCALE,
                                             name="mla_fwd")
    x2 = _mix_fwd(x1, y_mla, y_fox, gates, bg, G["w_branch_mla"], G["w_branch_fox"], G["w_o"])
    (dx3, a2, b2, loss_vec), _ = _ffn_fwd(x2, W["ffn2_norm"], G["ffn2_w_gate"], G["ffn2_w_up"], G["ffn2_w_down"],
                                          target=loss_target[0])

    def chip_sums(group, tag):
        from_sibling = _grads_to_sibling([grads[n] for n in group], "grads_d2d_" + tag)
        return _ChipExchange(False, [_pair_sum(grads[n], s, "pair_sum_" + n) for n, s in zip(group, from_sibling)])

    (dx2, dg_ffn2, da2, db2, h2, n2, dyh2), _ = _ffn_bwd(dx3, x2, W["ffn2_norm"], a2, b2, G["ffn2_w_gate"], G["ffn2_w_up"],
                                                        G["ffn2_w_down"], "ffn2_bwd")
    grads = {"ffn2_w_gate": _tn_matmul(da2, n2, "ffn2_dgate"), "ffn2_w_up": _tn_matmul(db2, n2, "ffn2_dup"),
             "ffn2_w_down": _tn_matmul(h2, dyh2, "ffn2_ddown")}
    dy_mla, dy_fox, delta_mla, delta_fox, dgates, mixed, dum, duf, dx2b, dbg = _mix_bwd(
        dx2, y_mla, y_fox, y_mla32, y_fox32, gates, bg, G["w_branch_mla"], G["w_branch_fox"], G["w_o"])
    grads["w_o"] = _tn_matmul(mixed, dx2b, "d_w_o")
    grads["w_branch_mla"] = _tn_matmul(dum, y_mla, "d_w_branch_mla")
    grads["w_branch_fox"] = _tn_matmul(duf, y_fox, "d_w_branch_fox")
    (dfq, dfk, dfv, dc_t), slots_c = _attn_bwd(fq, fk, fv, dy_fox, delta_fox, lse_fox, c, c_t, lanes=HEAD_DIM, scale=1.0,
                                               name="fox_bwd", exchange=chip_sums(group_c, "c"))
    slots = dict(zip(group_c, slots_c))
    (dmq, dmk, dmv), _ = _attn_bwd(mq, mk, mv, dy_mla, delta_mla, lse_mla, None, None, lanes=MLA_QK_LANES, scale=MLA_SCALE,
                                   name="mla_bwd")
    dfox, dlat, dwq, dwk, dwv, d_gql, d_gkvl, d_gq, d_gk, d_gkr, d_gfq, d_gfk, d_bf = _prep_bwd(
        *prep_args, dfq, dfk, dfv, dc_t.T, dmq, dmk, dmv)
    grads["mla_w_qb"], grads["mla_w_kvb"] = _deinterleave_grads(dwq, dwk, dwv)
    fold = lambda a, width: a.reshape(N_HEADS, width).sum(axis=0)[None]
    d_gq, d_gk = fold(d_gq, LANES), fold(d_gk, LANES)
    d_prep_small = [d_gql, d_gkvl, d_gq[:, :HEAD_DIM], d_gq[:, HEAD_DIM:HEAD_DIM + ROPE_DIM], d_gk[:, :HEAD_DIM],
                    d_gkr[:, HEAD_DIM:HEAD_DIM + ROPE_DIM], fold(d_gfq, HEAD_DIM), fold(d_gfk, HEAD_DIM), d_bf]
    dx1, dg_mix, nmix = _proj_bwd(dgates, dfox, dlat, wgate_t, wfox_t, wlat_t, x1, W["mix_norm"], dx2)
    dwin_t = _join_w_in(_tn_matmul(dgates, nmix, "d_w_in_gates"), _tn_matmul(dfox, nmix, "d_w_in_fox"),
                        _tn_matmul(dlat, nmix, "d_w_in_lat"))
    grads["w_in"] = jnp.pad(dwin_t.reshape(N_DEV, W_IN_PIECE, D_MODEL), ((0, 0), (0, W_IN_PIECE_PAD - W_IN_PIECE), (0, 0))
                            ).reshape(N_DEV * W_IN_PIECE_PAD, D_MODEL)
    grad_group_b = [n for n in group_b if n != "b_gate"]
    (dx0, dg_ffn1, da1, db1, h1, n1, dyh1), slots_b = _ffn_bwd(dx1, xs, W["ffn1_norm"], a1, b1, G["ffn1_w_gate"], G["ffn1_w_up"],
                                                              G["ffn1_w_down"], "ffn1_bwd", exchange=chip_sums(grad_group_b, "b"))
    slots.update(zip(grad_group_b, slots_b))
    grads["ffn1_w_gate"] = _tn_matmul(da1, n1, "ffn1_dgate")
    grads["ffn1_w_up"], got = _tn_matmul(db1, n1, "ffn1_dup", exchange=chip_sums(["ffn1_w_gate"], "a_gate"))
    slots["ffn1_w_gate"] = got[0]
    grads["ffn1_w_down"], got = _tn_matmul(h1, dyh1, "ffn1_ddown", exchange=chip_sums(["ffn1_w_up"], "a_up"))
    slots["ffn1_w_up"] = got[0]
    slots["ffn1_w_down"] = chip_sums(["ffn1_w_down"], "a_down").standalone("grads_ici_a_down")[0]

    small_parts = [dg_ffn1, dg_mix, dg_ffn2] + list(d_prep_small) + [dbg.reshape(1, 2 * D_MODEL), loss_vec]
    flat = jnp.concatenate([p.reshape(-1) for p in small_parts])
    n_flat = flat.shape[0]
    rows = -(-n_flat // (8 * 128)) * 8
    total = _all_reduce_small(jnp.pad(flat, (0, rows * 128 - n_flat)).reshape(rows, 128)).reshape(-1)
    offs, small_g = 0, {}
    for n in _SMALL:
        small_g[n] = total[offs:offs + W[n].shape[1]].reshape(W[n].shape)
        offs += W[n].shape[1]
    bg_full = total[offs:offs + 2 * D_MODEL].reshape(2, D_MODEL)
    offs += 2 * D_MODEL
    loss = (0.5 / D_MODEL) * jnp.sum(total[offs:offs + D_MODEL])
    _, _, _, me = _my_position()
    small_g["b_gate"] = lax.dynamic_slice_in_dim(bg_full, me * (D_MODEL // N_DEV), D_MODEL // N_DEV, axis=1)

    transposed_in_memory = ["ffn1_w_gate", "ffn1_w_up", "ffn2_w_gate", "ffn2_w_up", "w_in"]
    res = {}
    for n in _WEIGHTS:
        if n in small_g:
            res[n] = (small_g[n],) + tuple(_adamw(small_g[n], W[n], M[n], V[n], "adamw_" + n))
        elif n in transposed_in_memory:
            res[n] = tuple(o.T for o in _reduce_adamw(slots[n], W[n].T, M[n].T, V[n].T, transpose=False, name="adamw_" + n))
        else:
            res[n] = tuple(_reduce_adamw(slots[n], W[n], M[n], V[n], transpose=n in col_split, name="adamw_" + n))
    outs = [loss, dx0[None]]
    for k in range(4):
        outs += [res[n][k] if n in _SMALL else res[n][k][None] for n in _WEIGHTS]
    return tuple(outs)
```

```python
import functools

import jax
import jax.numpy as jnp
from jax import lax
from jax.experimental import pallas as pl
from jax.experimental.pallas import tpu as pltpu

F32 = jnp.float32
BF16 = jnp.bfloat16

D_MODEL = 1024
FFN_HIDDEN = 2816
N_HEADS = 8
HEAD_DIM = 64
ROPE_DIM = 32
ROPE_HALF = 16
Q_LORA = 192
KV_LORA = 128
ROPE_THETA = 10000.0
RMS_EPS = 1e-6
MLA_SCALE = (HEAD_DIM + ROPE_DIM) ** -0.5
FOX_SCALE = HEAD_DIM ** -0.5
MLA_QK_LANES = 128
ADAM_LR, ADAM_B1, ADAM_B2, ADAM_EPS, ADAM_WD, ADAM_STEP = 0.001, 0.9, 0.999, 1e-08, 0.01, 10
N_DEV = 8
N_CHIP = 4
W_IN_PIECE = 493
W_IN_PIECE_PAD = 496
LAT_W = 512
LAT_Q, LAT_KV, LAT_KR, LAT_F = 0, 256, 384, 416
MASK_VALUE = -1e30
FIXED_SHIFT_MAX_BOUND = 30.0

TOK_TILE = 512
DW_TOK_TILE = 1024
PROJ_TILE = 256
ATT_TILE = 512
ATT_COL_CHUNK = 256
FFN_HID_TILE = 1408
FFN_HID_SPLIT = ((0, 768), (768, 1408))

NT = (((1,), (1,)), ((), ()))
TN = (((0,), (0,)), ((), ()))
NN = (((1,), (0,)), ((), ()))
MESH = pl.DeviceIdType.MESH


def _dot(a, b, dims=NN):
    return lax.dot_general(a, b, dims, preferred_element_type=F32)


def _sds(shape, dtype):
    return jax.ShapeDtypeStruct(shape, dtype)


def _rms_fwd(x, g):
    r = lax.rsqrt(jnp.mean(x * x, axis=-1, keepdims=True) + RMS_EPS)
    return x * r * g, r


def _rms_bwd(dy, x, g, r):
    xn = x * r
    dyg = dy * g
    dx = r * (dyg - xn * jnp.mean(dyg * xn, axis=-1, keepdims=True))
    return dx, dy * xn


def _colsum(x):
    return jnp.sum(x, axis=0, keepdims=True)


def _full(shape):
    return pl.BlockSpec(shape, lambda *_: (0,) * len(shape))


def _tok(tm, n):
    return pl.BlockSpec((tm, n), lambda i, *_: (i, 0))


def _ffn_fwd(x, gain, wg_t, wu_t, wd, target=None, exchange=None):
    t = x.shape[0]
    tm = min(TOK_TILE, t)
    tf = FFN_HID_TILE
    n_t, n_f = t // tm, FFN_HIDDEN // tf
    with_loss = target is not None

    def body(*refs):
        if with_loss:
            x_ref, g_ref, wg_ref, wu_ref, wd_ref, t_ref, out_ref, a_ref, b_ref, lvec_ref, n_scr, acc = refs
        else:
            x_ref, g_ref, wg_ref, wu_ref, wd_ref, out_ref, a_ref, b_ref, n_scr, acc = refs
        i, j = pl.program_id(0), pl.program_id(1)

        @pl.when(j == 0)
        def _():
            xn, _ = _rms_fwd(x_ref[...], g_ref[...])
            n_scr[...] = xn.astype(BF16)
            acc[...] = jnp.zeros_like(acc)

        n = n_scr[...]
        a = _dot(n, wg_ref[...], NT)
        b = _dot(n, wu_ref[...], NT)
        a_ref[...] = a.astype(BF16)
        b_ref[...] = b.astype(BF16)
        h = (a * jax.nn.sigmoid(a)) * b
        acc[...] += _dot(h.astype(BF16), wd_ref[...])

        @pl.when(j == n_f - 1)
        def _():
            y = x_ref[...] + 0.5 * acc[...]
            if with_loss:
                diff = y - t_ref[...]
                out_ref[...] = diff * (1.0 / D_MODEL)
                sq = _colsum(diff * diff)

                @pl.when(i == 0)
                def _():
                    lvec_ref[...] = sq

                @pl.when(i > 0)
                def _():
                    lvec_ref[...] += sq
            else:
                out_ref[...] = y

    wspec = pl.BlockSpec((tf, D_MODEL), lambda i, j: (j, 0))
    hspec = pl.BlockSpec((tm, tf), lambda i, j: (i, j))
    in_specs = [_tok(tm, D_MODEL), _full((1, D_MODEL)), wspec, wspec, wspec]
    out_specs = [_tok(tm, D_MODEL), hspec, hspec]
    out_shape = [_sds((t, D_MODEL), F32), _sds((t, FFN_HIDDEN), BF16), _sds((t, FFN_HIDDEN), BF16)]
    args = [x, gain, wg_t, wu_t, wd]
    if with_loss:
        in_specs.append(_tok(tm, D_MODEL))
        out_specs.append(_full((1, D_MODEL)))
        out_shape.append(_sds((1, D_MODEL), F32))
        args.append(target)
    return _gridded_call(
        body, name="ffn_fwd_loss" if with_loss else "ffn_fwd", grid=(n_t, n_f), in_specs=in_specs, out_specs=out_specs,
        out_shape=out_shape, scratch_shapes=[pltpu.VMEM((tm, D_MODEL), BF16), pltpu.VMEM((tm, D_MODEL), F32)],
        args=args, exchange=exchange)


def _ffn_bwd(dy, x, gain, a, b, wg_t, wu_t, wd, name, exchange=None):
    t = x.shape[0]
    tm = min(TOK_TILE, t)
    tf = FFN_HID_TILE
    n_t, n_f = t // tm, FFN_HIDDEN // tf

    def body(dy_ref, x_ref, g_ref, a_ref, b_ref, wg_ref, wu_ref, wd_ref,
             dx_ref, dg_ref, da_ref, db_ref, h_ref, n_ref, dyh_ref, acc):
        i, j = pl.program_id(0), pl.program_id(1)

        @pl.when(j == 0)
        def _():
            xn, _ = _rms_fwd(x_ref[...], g_ref[...])
            n_ref[...] = xn.astype(BF16)
            dyh_ref[...] = (0.5 * dy_ref[...]).astype(BF16)
            acc[...] = jnp.zeros_like(acc)

        dyh = dyh_ref[...]
        for lo, hi in FFN_HID_SPLIT:
            dh = _dot(dyh, wd_ref[lo:hi, :], NT)
            av = a_ref[:, lo:hi].astype(F32)
            bv = b_ref[:, lo:hi].astype(F32)
            s = jax.nn.sigmoid(av)
            silu = av * s
            da = (dh * bv * (s * (1.0 + av * (1.0 - s)))).astype(BF16)
            db = (dh * silu).astype(BF16)
            da_ref[:, lo:hi] = da
            db_ref[:, lo:hi] = db
            h_ref[:, lo:hi] = (silu * bv).astype(BF16)
            acc[...] += _dot(da, wg_ref[lo:hi, :]) + _dot(db, wu_ref[lo:hi, :])

        @pl.when(j == n_f - 1)
        def _():
            xv, g = x_ref[...], g_ref[...]
            r = lax.rsqrt(jnp.mean(xv * xv, axis=-1, keepdims=True) + RMS_EPS)
            dx, dg_rows = _rms_bwd(acc[...], xv, g, r)
            dx_ref[...] = dy_ref[...] + dx
            dg = _colsum(dg_rows)

            @pl.when(i == 0)
            def _():
                dg_ref[...] = dg

            @pl.when(i > 0)
            def _():
                dg_ref[...] += dg

    wspec = pl.BlockSpec((tf, D_MODEL), lambda i, j: (j, 0))
    hspec = pl.BlockSpec((tm, tf), lambda i, j: (i, j))
    tok = _tok(tm, D_MODEL)
    hid = _sds((t, FFN_HIDDEN), BF16)
    return _gridded_call(
        body, name=name, grid=(n_t, n_f),
        in_specs=[tok, tok, _full((1, D_MODEL)), hspec, hspec, wspec, wspec, wspec],
        out_specs=[tok, _full((1, D_MODEL)), hspec, hspec, hspec, tok, tok],
        out_shape=[_sds((t, D_MODEL), F32), _sds((1, D_MODEL), F32), hid, hid, hid,
                   _sds((t, D_MODEL), BF16), _sds((t, D_MODEL), BF16)],
        scratch_shapes=[pltpu.VMEM((tm, D_MODEL), F32)], args=[dy, x, gain, a, b, wg_t, wu_t, wd], exchange=exchange)


def _tn_matmul(a, b, name, exchange=None):
    t, m = a.shape
    n = b.shape[1]
    tk = min(DW_TOK_TILE, t)
    n_k = t // tk

    def body(a_ref, b_ref, o_ref, acc):
        k = pl.program_id(0)
        p = _dot(a_ref[...], b_ref[...], TN)

        @pl.when(k == 0)
        def _():
            acc[...] = p

        @pl.when(k > 0)
        def _():
            acc[...] += p

        @pl.when(k == n_k - 1)
        def _():
            o_ref[...] = acc[...].astype(BF16)

    (out,), got = _gridded_call(
        body, name=name, grid=(n_k,), in_specs=[_tok(tk, m), _tok(tk, n)], out_specs=[_full((m, n))],
        out_shape=[_sds((m, n), BF16)], scratch_shapes=[pltpu.VMEM((m, n), F32)], args=[a, b], exchange=exchange)
    return out if exchange is None else (out, got)


def _proj_fwd(x, gain, wgate_t, wfox_t, wlat_t):
    t = x.shape[0]
    tm = min(PROJ_TILE, t)

    def body(x_ref, g_ref, wg_ref, wf_ref, wl_ref, og_ref, of_ref, ol_ref):
        xn, _ = _rms_fwd(x_ref[...], g_ref[...])
        n = xn.astype(BF16)
        og_ref[...] = _dot(n, wg_ref[...], NT)
        of_ref[...] = _dot(n, wf_ref[...], NT)
        ol_ref[...] = _dot(n, wl_ref[...], NT)

    return pl.pallas_call(
        body, name="proj_fwd", grid=(t // tm,),
        in_specs=[_tok(tm, D_MODEL), _full((1, D_MODEL)), _full(wgate_t.shape), _full(wfox_t.shape), _full(wlat_t.shape)],
        out_specs=[_tok(tm, 2 * D_MODEL), _tok(tm, 3 * 512), _tok(tm, LAT_W)],
        out_shape=[_sds((t, 2 * D_MODEL), F32), _sds((t, 3 * 512), F32), _sds((t, LAT_W), F32)],
        compiler_params=pltpu.CompilerParams(dimension_semantics=("arbitrary",)),
    )(x, gain, wgate_t, wfox_t, wlat_t)


def _proj_bwd(dgates, dfox, dlat, wgate_t, wfox_t, wlat_t, x, gain, dres):
    t = x.shape[0]
    tm = min(PROJ_TILE, t)

    def body(dg_ref, df_ref, dl_ref, wg_ref, wf_ref, wl_ref, x_ref, g_ref, dres_ref, dx_ref, dgain_ref, n_ref):
        i = pl.program_id(0)
        dn = _dot(dg_ref[...], wg_ref[...]) + _dot(df_ref[...], wf_ref[...]) + _dot(dl_ref[...], wl_ref[...])
        xv, g = x_ref[...], g_ref[...]
        xn, r = _rms_fwd(xv, g)
        n_ref[...] = xn.astype(BF16)
        dx, dg_rows = _rms_bwd(dn, xv, g, r)
        dx_ref[...] = dres_ref[...] + dx
        dgn = _colsum(dg_rows)

        @pl.when(i == 0)
        def _():
            dgain_ref[...] = dgn

        @pl.when(i > 0)
        def _():
            dgain_ref[...] += dgn

    tok = _tok(tm, D_MODEL)
    return pl.pallas_call(
        body, name="proj_bwd", grid=(t // tm,),
        in_specs=[_tok(tm, 2 * D_MODEL), _tok(tm, 3 * 512), _tok(tm, LAT_W), _full(wgate_t.shape), _full(wfox_t.shape),
                  _full(wlat_t.shape), tok, _full((1, D_MODEL)), tok],
        out_specs=[tok, _full((1, D_MODEL)), tok],
        out_shape=[_sds((t, D_MODEL), F32), _sds((1, D_MODEL), F32), _sds((t, D_MODEL), BF16)],
        compiler_params=pltpu.CompilerParams(dimension_semantics=("arbitrary",)),
    )(dgates, dfox, dlat, wgate_t, wfox_t, wlat_t, x, gain, dres)


def _tri(n, lower):
    r = lax.broadcasted_iota(jnp.int32, (n, n), 0)
    c = lax.broadcasted_iota(jnp.int32, (n, n), 1)
    return ((c <= r) if lower else (c >= r)).astype(F32)


def _log_sigmoid(z):
    return jnp.minimum(z, 0.0) - jnp.log1p(jnp.exp(-jnp.abs(z)))


N_GROUPS = 16
LANES = MLA_QK_LANES


def _split_dot(a, g):
    hi = a.astype(BF16)
    lo = (a - hi.astype(F32)).astype(BF16)
    return _dot(hi, g) + _dot(lo, g)


def _rope_fwd(y, c, s1, s2):
    return y * c + pltpu.roll(y, LANES - ROPE_HALF, 1) * s1 + pltpu.roll(y, ROPE_HALF, 1) * s2


def _rope_bwd(do, c, s1, s2):
    return do * c + pltpu.roll(do * s1, ROPE_HALF, 1) + pltpu.roll(do * s2, LANES - ROPE_HALF, 1)


def _prep_tables(cos, sin):
    t = cos.shape[0]
    z = lambda n: jnp.zeros((t, n), F32)
    c = jnp.concatenate([jnp.ones((t, HEAD_DIM), F32), cos, cos, z(LANES - HEAD_DIM - ROPE_DIM)], axis=1)
    s1 = jnp.concatenate([z(HEAD_DIM), -sin, z(LANES - HEAD_DIM - ROPE_HALF)], axis=1)
    s2 = jnp.concatenate([z(HEAD_DIM + ROPE_HALF), sin, z(LANES - HEAD_DIM - ROPE_DIM)], axis=1)
    lane = jnp.arange(LANES)
    rope_mask = ((lane >= HEAD_DIM) & (lane < HEAD_DIM + ROPE_DIM)).astype(F32)[None, :]
    return c, s1, s2, rope_mask


def _group_matrices():
    lane = jnp.arange(N_HEADS * LANES)
    head, d = (lane // LANES)[:, None], (lane % LANES)[:, None]
    col = jnp.arange(N_GROUPS)[None, :]
    g_mla = ((col == head) & (d < HEAD_DIM)) | ((col == N_HEADS + head) & (d >= HEAD_DIM) & (d < HEAD_DIM + ROPE_DIM))
    g_fox = col == (jnp.arange(N_HEADS * HEAD_DIM) // HEAD_DIM)[:, None]
    inv_mla = jnp.concatenate([jnp.full((1, N_HEADS), 1.0 / HEAD_DIM, F32), jnp.full((1, N_HEADS), 1.0 / ROPE_DIM, F32)], axis=1)
    inv_fox = jnp.full((1, N_GROUPS), 1.0 / HEAD_DIM, F32)
    return g_mla.astype(BF16), g_mla.T.astype(BF16), inv_mla, g_fox.astype(BF16), g_fox.T.astype(BF16), inv_fox


def _interleave_weights(wqb_t, wkvb_t):
    wq = jnp.pad(wqb_t.reshape(N_HEADS, HEAD_DIM + ROPE_DIM, Q_LORA), ((0, 0), (0, LANES - HEAD_DIM - ROPE_DIM), (0, 0)))
    kv = wkvb_t.reshape(N_HEADS, 2, HEAD_DIM, KV_LORA)
    wk = jnp.pad(kv[:, 0], ((0, 0), (0, LANES - HEAD_DIM), (0, 0)))
    return wq.reshape(N_HEADS * LANES, Q_LORA), wk.reshape(N_HEADS * LANES, KV_LORA), kv[:, 1].reshape(N_HEADS * HEAD_DIM, KV_LORA)


def _deinterleave_grads(dwq, dwk, dwv):
    dq = dwq.reshape(N_HEADS, LANES, Q_LORA)[:, :HEAD_DIM + ROPE_DIM].reshape(N_HEADS * (HEAD_DIM + ROPE_DIM), Q_LORA)
    dk = dwk.reshape(N_HEADS, LANES, KV_LORA)[:, :HEAD_DIM]
    dkv = jnp.stack([dk, dwv.reshape(N_HEADS, HEAD_DIM, KV_LORA)], axis=1)
    return dq, dkv.reshape(N_HEADS * 2 * HEAD_DIM, KV_LORA)


def _head_gains(g_qn, g_qr, g_kn, g_kr, g_fq, g_fk):
    z = lambda n: jnp.zeros((1, n), F32)
    gq = jnp.concatenate([g_qn, g_qr, z(LANES - HEAD_DIM - ROPE_DIM)], axis=1)
    gk = jnp.concatenate([g_kn, z(LANES - HEAD_DIM)], axis=1)
    gkr = jnp.concatenate([z(HEAD_DIM), g_kr, z(LANES - HEAD_DIM - ROPE_DIM)], axis=1)
    return jnp.tile(gq, (1, N_HEADS)), jnp.tile(gk, (1, N_HEADS)), gkr, jnp.tile(g_fq, (1, N_HEADS)), jnp.tile(g_fk, (1, N_HEADS))


def _group_rms(x, g, g_t, inv):
    r = lax.rsqrt(_split_dot(x * x, g) * inv + RMS_EPS)
    return _split_dot(r, g_t)


def _prep_fwd(fox, lat, tables, mats, wq, wk, wv, g_qlat, g_kvlat, gains, b_f):
    t = fox.shape[0]
    tm = min(PROJ_TILE, t)
    c_tab, s1_tab, s2_tab, rope_mask = tables

    def body(fox_ref, lat_ref, c_ref, s1_ref, s2_ref, rm_ref, gm_ref, gmt_ref, im_ref, gf_ref, gft_ref, if_ref,
             wq_ref, wk_ref, wv_ref, gql_ref, gkvl_ref, gq_ref, gk_ref, gkr_ref, gfq_ref, gfk_ref, bf_ref,
             fq_ref, fk_ref, fv_ref, cc_ref, mq_ref, mk_ref, mv_ref, fqn_ref, fkmax_ref, mqn_ref, mkmax_ref, carry):
        i = pl.program_id(0)
        ct, s1, s2 = c_ref[...], s1_ref[...], s2_ref[...]

        @pl.when(i == 0)
        def _():
            carry[...] = jnp.zeros_like(carry)
            fkmax_ref[...] = jnp.zeros_like(fkmax_ref)
            mkmax_ref[...] = jnp.zeros_like(mkmax_ref)

        def square_sums(ref, g):
            v = ref[...].astype(F32)
            return _split_dot(v * v, g)

        xq = fox_ref[:, 0:512]
        fq_ref[...] = (xq * _group_rms(xq, gf_ref[...], gft_ref[...], if_ref[...]) * (gfq_ref[...] * FOX_SCALE)).astype(BF16)
        xk = fox_ref[:, 512:1024]
        fk_ref[...] = (xk * _group_rms(xk, gf_ref[...], gft_ref[...], if_ref[...]) * gfk_ref[...]).astype(BF16)
        fv_ref[...] = fox_ref[:, 1024:1536].astype(BF16)
        fqn_ref[...] = square_sums(fq_ref, gf_ref[...])
        fkmax_ref[...] = jnp.maximum(fkmax_ref[...], jnp.max(square_sums(fk_ref, gf_ref[...]), axis=0, keepdims=True))


        logf = _log_sigmoid(lat_ref[:, LAT_F:LAT_F + N_HEADS] + bf_ref[...])
        cc_ref[...] = jnp.dot(_tri(tm, True), logf, precision=lax.Precision.HIGHEST, preferred_element_type=F32) + carry[...]
        carry[...] += _colsum(logf)

        qlat_n, _ = _rms_fwd(lat_ref[:, LAT_Q:LAT_Q + Q_LORA], gql_ref[...])
        p = _dot(qlat_n.astype(BF16), wq_ref[...], NT)
        y = p * _group_rms(p, gm_ref[...], gmt_ref[...], im_ref[...]) * (gq_ref[...] * MLA_SCALE)
        for h in range(N_HEADS):
            w = slice(h * LANES, (h + 1) * LANES)
            mq_ref[:, w] = _rope_fwd(y[:, w], ct, s1, s2).astype(BF16)

        kv_n, _ = _rms_fwd(lat_ref[:, LAT_KV:LAT_KV + KV_LORA], gkvl_ref[...])
        kv_b = kv_n.astype(BF16)
        pk = _dot(kv_b, wk_ref[...], NT)
        kn = pk * _group_rms(pk, gm_ref[...], gmt_ref[...], im_ref[...]) * gk_ref[...]
        rm = rm_ref[...]
        kr = pltpu.roll(lat_ref[:, LAT_KR:LAT_KR + LANES], HEAD_DIM, 1) * rm
        rr = lax.rsqrt(jnp.sum(kr * kr, axis=1, keepdims=True) * (1.0 / ROPE_DIM) + RMS_EPS)
        okr = _rope_fwd(kr * rr * gkr_ref[...], ct * rm, s1, s2)
        for h in range(N_HEADS):
            w = slice(h * LANES, (h + 1) * LANES)
            mk_ref[:, w] = (kn[:, w] + okr).astype(BF16)
        mv_ref[...] = _dot(kv_b, wv_ref[...], NT).astype(BF16)
        mqn_ref[...] = square_sums(mq_ref, gm_ref[...])
        mkmax_ref[...] = jnp.maximum(mkmax_ref[...], jnp.max(square_sums(mk_ref, gm_ref[...]), axis=0, keepdims=True))

    consts = [rope_mask, *mats, wq, wk, wv, g_qlat, g_kvlat, *gains, b_f]
    t512, tl, tg, g1 = _tok(tm, 512), _tok(tm, LANES), _tok(tm, N_GROUPS), _full((1, N_GROUPS))
    return pl.pallas_call(
        body, name="prep_fwd", grid=(t // tm,),
        in_specs=[_tok(tm, 1536), _tok(tm, LAT_W), tl, tl, tl] + [_full(a.shape) for a in consts],
        out_specs=[t512, t512, t512, _tok(tm, N_HEADS), _tok(tm, N_HEADS * LANES), _tok(tm, N_HEADS * LANES), t512,
                   tg, g1, tg, g1],
        out_shape=[_sds((t, 512), BF16), _sds((t, 512), BF16), _sds((t, 512), BF16), _sds((t, N_HEADS), F32),
                   _sds((t, N_HEADS * LANES), BF16), _sds((t, N_HEADS * LANES), BF16), _sds((t, 512), BF16),
                   _sds((t, N_GROUPS), F32), _sds((1, N_GROUPS), F32), _sds((t, N_GROUPS), F32), _sds((1, N_GROUPS), F32)],
        scratch_shapes=[pltpu.VMEM((1, N_HEADS), F32)],
        compiler_params=pltpu.CompilerParams(dimension_semantics=("arbitrary",)),
    )(fox, lat, c_tab, s1_tab, s2_tab, *consts)


def _prep_bwd(fox, lat, tables, mats, wq, wk, wv, g_qlat, g_kvlat, gains, b_f, dfq, dfk, dfv, dc, dmq, dmk, dmv):
    t = fox.shape[0]
    tm = min(PROJ_TILE, t)
    n_t = t // tm
    c_tab, s1_tab, s2_tab, rope_mask = tables

    def body(fox_ref, lat_ref, c_ref, s1_ref, s2_ref, rm_ref, gm_ref, gmt_ref, im_ref, gf_ref, gft_ref, if_ref,
             wq_ref, wk_ref, wv_ref, gql_ref, gkvl_ref, gq_ref, gk_ref, gkr_ref, gfq_ref, gfk_ref, bf_ref,
             dfq_ref, dfk_ref, dfv_ref, dc_ref, dmq_ref, dmk_ref, dmv_ref,
             dfox_ref, dlat_ref, dwq_ref, dwk_ref, dwv_ref, o_gql, o_gkvl, o_gq, o_gk, o_gkr, o_gfq, o_gfk, o_bf,
             carry, lscr, wq_acc, wk_acc, wv_acc, dyscr):
        i = pl.program_id(0)
        ct, s1, s2, rm = c_ref[...], s1_ref[...], s2_ref[...], rm_ref[...]

        @pl.when(i == 0)
        def _():
            for ref in [carry, wq_acc, wk_acc, wv_acc, o_gql, o_gkvl, o_gq, o_gk, o_gkr, o_gfq, o_gfk, o_bf]:
                ref[...] = jnp.zeros_like(ref)

        def group_rms_bwd(dy, x, r, gain, g, g_t, inv):
            xn = x * r
            dyg = dy * gain
            mu = _split_dot(_split_dot(dyg * xn, g) * inv, g_t)
            return r * (dyg - xn * mu), dy * xn

        gf, gft, invf = gf_ref[...], gft_ref[...], if_ref[...]
        xq = fox_ref[:, 0:512]
        dxq, rows = group_rms_bwd(dfq_ref[...] * FOX_SCALE, xq, _group_rms(xq, gf, gft, invf), gfq_ref[...], gf, gft, invf)
        dfox_ref[:, 0:512] = dxq.astype(BF16)
        o_gfq[...] += _colsum(rows)
        xk = fox_ref[:, 512:1024]
        dxk, rows = group_rms_bwd(dfk_ref[...], xk, _group_rms(xk, gf, gft, invf), gfk_ref[...], gf, gft, invf)
        dfox_ref[:, 512:1024] = dxk.astype(BF16)
        o_gfk[...] += _colsum(rows)
        dfox_ref[:, 1024:1536] = dfv_ref[...].astype(BF16)

        lscr[...] = jnp.zeros_like(lscr)

        xql = lat_ref[:, LAT_Q:LAT_Q + Q_LORA]
        qlat_n, r_ql = _rms_fwd(xql, gql_ref[...])
        qlat_b = qlat_n.astype(BF16)
        gm, gmt, invm = gm_ref[...], gmt_ref[...], im_ref[...]
        p = _dot(qlat_b, wq_ref[...], NT)
        for h in range(N_HEADS):
            w = slice(h * LANES, (h + 1) * LANES)
            dyscr[:, w] = _rope_bwd(dmq_ref[:, w], ct, s1, s2) * MLA_SCALE
        dp, rows = group_rms_bwd(dyscr[...], p, _group_rms(p, gm, gmt, invm), gq_ref[...], gm, gmt, invm)
        o_gq[...] += _colsum(rows)
        dp_b = dp.astype(BF16)
        wq_acc[...] += _dot(dp_b, qlat_b, TN)
        dxql, rows = _rms_bwd(_dot(dp_b, wq_ref[...]), xql, gql_ref[...], r_ql)
        lscr[:, LAT_Q:LAT_Q + Q_LORA] = dxql
        o_gql[...] += _colsum(rows)

        xkv = lat_ref[:, LAT_KV:LAT_KV + KV_LORA]
        kv_n, r_kv = _rms_fwd(xkv, gkvl_ref[...])
        kv_b = kv_n.astype(BF16)
        pk = _dot(kv_b, wk_ref[...], NT)
        dpk, rows = group_rms_bwd(dmk_ref[...], pk, _group_rms(pk, gm, gmt, invm), gk_ref[...], gm, gmt, invm)
        o_gk[...] += _colsum(rows)
        dpk_b = dpk.astype(BF16)
        dv_b = dmv_ref[...].astype(BF16)
        wk_acc[...] += _dot(dpk_b, kv_b, TN)
        wv_acc[...] += _dot(dv_b, kv_b, TN)
        dxkv, rows = _rms_bwd(_dot(dpk_b, wk_ref[...]) + _dot(dv_b, wv_ref[...]), xkv, gkvl_ref[...], r_kv)
        lscr[:, LAT_KV:LAT_KV + KV_LORA] = dxkv
        o_gkvl[...] += _colsum(rows)

        dokr = dmk_ref[:, 0:LANES]
        for h in range(1, N_HEADS):
            dokr = dokr + dmk_ref[:, h * LANES:(h + 1) * LANES]
        dykr = _rope_bwd(dokr * rm, ct * rm, s1, s2)
        kr = pltpu.roll(lat_ref[:, LAT_KR:LAT_KR + LANES], HEAD_DIM, 1) * rm
        rr = lax.rsqrt(jnp.sum(kr * kr, axis=1, keepdims=True) * (1.0 / ROPE_DIM) + RMS_EPS)
        krn = kr * rr
        dyg = dykr * gkr_ref[...]
        dkr = rr * (dyg - krn * (jnp.sum(dyg * krn, axis=1, keepdims=True) * (1.0 / ROPE_DIM)))
        o_gkr[...] += _colsum(dykr * krn)
        lscr[:, LAT_KR:LAT_KR + LANES] = pltpu.roll(dkr, LANES - HEAD_DIM, 1)

        dcv = dc_ref[...]
        dlogf = jnp.dot(_tri(tm, False), dcv, precision=lax.Precision.HIGHEST, preferred_element_type=F32) + carry[...]
        carry[...] += _colsum(dcv)
        dz = dlogf * jax.nn.sigmoid(-(lat_ref[:, LAT_F:LAT_F + N_HEADS] + bf_ref[...]))
        lscr[:, LAT_F:LAT_F + N_HEADS] = dz
        o_bf[...] += _colsum(dz)

        dlat_ref[...] = lscr[...].astype(BF16)

        @pl.when(i == n_t - 1)
        def _():
            dwq_ref[...] = wq_acc[...].astype(BF16)
            dwk_ref[...] = wk_acc[...].astype(BF16)
            dwv_ref[...] = wv_acc[...].astype(BF16)

    consts = [rope_mask, *mats, wq, wk, wv, g_qlat, g_kvlat, *gains, b_f]

    def rtok(n):
        return pl.BlockSpec((tm, n), lambda i: (n_t - 1 - i, 0))

    sums = [(1, Q_LORA), (1, KV_LORA), (1, N_HEADS * LANES), (1, N_HEADS * LANES), (1, LANES), (1, 512), (1, 512), (1, N_HEADS)]
    return pl.pallas_call(
        body, name="prep_bwd", grid=(n_t,),
        in_specs=[rtok(1536), rtok(LAT_W), rtok(LANES), rtok(LANES), rtok(LANES)] + [_full(a.shape) for a in consts]
        + [rtok(512), rtok(512), rtok(512), rtok(N_HEADS), rtok(N_HEADS * LANES), rtok(N_HEADS * LANES), rtok(512)],
        out_specs=[rtok(1536), rtok(LAT_W), _full(wq.shape), _full(wk.shape), _full(wv.shape)] + [_full(s) for s in sums],
        out_shape=[_sds((t, 1536), BF16), _sds((t, LAT_W), BF16), _sds(wq.shape, BF16), _sds(wk.shape, BF16), _sds(wv.shape, BF16)]
        + [_sds(s, F32) for s in sums],
        scratch_shapes=[pltpu.VMEM((1, N_HEADS), F32), pltpu.VMEM((tm, LAT_W), F32), pltpu.VMEM(wq.shape, F32),
                        pltpu.VMEM(wk.shape, F32), pltpu.VMEM(wv.shape, F32), pltpu.VMEM((tm, N_HEADS * LANES), F32)],
        compiler_params=pltpu.CompilerParams(dimension_semantics=("arbitrary",)),
    )(fox, lat, c_tab, s1_tab, s2_tab, *consts, dfq, dfk, dfv, dc, dmq, dmk, dmv)


def _logit_bound(q_sq, k_sq_max):
    qn = q_sq[:, :N_HEADS] + q_sq[:, N_HEADS:]
    kmax = k_sq_max[:, :N_HEADS] + k_sq_max[:, N_HEADS:]
    bound = jnp.sqrt(qn * kmax) * (1.0 + 2.0 ** -10) + 2.0 ** -10
    flag = (jnp.max(bound) <= FIXED_SHIFT_MAX_BOUND).astype(F32).reshape(1, 1)
    return bound, flag


def _attn_fwd(q, k, v, bound, fixed_ok, c, c_t, *, lanes, name, exchange=None):
    t = q.shape[0]
    tq = min(ATT_TILE, t)
    n_q = t // tq
    hd = HEAD_DIM
    ch = min(ATT_COL_CHUNK, tq)
    decay = c is not None

    def body(*refs):
        if decay:
            q_ref, k_ref, v_ref, b_ref, ok_ref, c_ref, ct_ref, o_ref, o32_ref, lse_ref, m_scr, l_scr, acc = refs
        else:
            q_ref, k_ref, v_ref, b_ref, ok_ref, o_ref, o32_ref, lse_ref, m_scr, l_scr, acc = refs
        i, j = pl.program_id(0), pl.program_id(1)
        fixed = ok_ref[0, 0] > 0.5

        @pl.when(j == 0)
        def _():
            m_scr[...] = jnp.full_like(m_scr, MASK_VALUE)
            l_scr[...] = jnp.zeros_like(l_scr)
            acc[...] = jnp.zeros_like(acc)

        def fixed_step(diagonal):
            for h in range(N_HEADS):
                wl = slice(h * lanes, (h + 1) * lanes)
                w = slice(h * hd, (h + 1) * hd)
                qh = q_ref[:, wl]
                row = (c_ref[:, h:h + 1] - b_ref[:, h:h + 1]) if decay else -b_ref[:, h:h + 1]
                l_new = jnp.zeros((tq, 1), F32)
                o_hi = jnp.zeros((tq, hd), F32)
                o_lo = jnp.zeros((tq, hd), F32)
                for cc in range(tq // ch):
                    cols = slice(cc * ch, (cc + 1) * ch)
                    s = _dot(qh, k_ref[cols, wl], NT)
                    s = s + ((row - ct_ref[h:h + 1, cols]) if decay else row)
                    if diagonal:
                        keep = (lax.broadcasted_iota(jnp.int32, (tq, ch), 0)
                                >= lax.broadcasted_iota(jnp.int32, (tq, ch), 1) + cc * ch)
                        s = jnp.where(keep, s, MASK_VALUE)
                    p = jnp.exp(s)
                    l_new = l_new + jnp.sum(p, axis=1, keepdims=True)
                    p_b = p.astype(BF16)
                    o_hi = o_hi + _dot(p_b, v_ref[cols, w])
                    if decay:
                        o_lo = o_lo + _dot((p - p_b.astype(F32)).astype(BF16), v_ref[cols, w])
                l_scr[h] += l_new
                acc[0, :, w] += o_hi
                if decay:
                    acc[1, :, w] += o_lo

        def step(diagonal):
            if diagonal:
                keep = lax.broadcasted_iota(jnp.int32, (tq, tq), 0) >= lax.broadcasted_iota(jnp.int32, (tq, tq), 1)
            for h in range(N_HEADS):
                s = _dot(q_ref[:, h * lanes:(h + 1) * lanes], k_ref[:, h * lanes:(h + 1) * lanes], NT)
                if decay:
                    s = s + (c_ref[:, h:h + 1] - ct_ref[h:h + 1, :])
                if diagonal:
                    s = jnp.where(keep, s, MASK_VALUE)
                m_prev = m_scr[h]
                m_new = jnp.maximum(m_prev, jnp.max(s, axis=1, keepdims=True))
                alpha = jnp.exp(m_prev - m_new)
                p = jnp.exp(s - m_new)
                l_scr[h] = alpha * l_scr[h] + jnp.sum(p, axis=1, keepdims=True)
                w = slice(h * hd, (h + 1) * hd)
                p_b = p.astype(BF16)
                acc[0, :, w] = alpha * acc[0, :, w] + _dot(p_b, v_ref[:, w])
                if decay:
                    p_lo = (p - p_b.astype(F32)).astype(BF16)
                    acc[1, :, w] = alpha * acc[1, :, w] + _dot(p_lo, v_ref[:, w])
                m_scr[h] = m_new

        for diagonal, here in ((False, j < i), (True, j == i)):
            @pl.when(here & fixed)
            def _():
                fixed_step(diagonal)

            @pl.when(here & jnp.logical_not(fixed))
            def _():
                step(diagonal)

        @pl.when(j == i)
        def _():
            for h in range(N_HEADS):
                w = slice(h * hd, (h + 1) * hd)
                l = l_scr[h]
                o_ref[:, w] = (acc[0, :, w] / l).astype(BF16)
                o32_ref[:, w] = ((acc[0, :, w] + acc[1, :, w]) if decay else acc[0, :, w]) / l
                lse_ref[:, h:h + 1] = jnp.where(fixed, b_ref[:, h:h + 1], m_scr[h]) + jnp.log(l)

    qspec = lambda n: pl.BlockSpec((tq, n), lambda i, j: (i, 0))
    kspec = lambda n: pl.BlockSpec((tq, n), lambda i, j: (jnp.minimum(i, j), 0))
    in_specs = [qspec(N_HEADS * lanes), kspec(N_HEADS * lanes), kspec(512), qspec(N_HEADS),
                pl.BlockSpec(memory_space=pltpu.SMEM)]
    args = [q, k, v, bound, fixed_ok]
    if decay:
        in_specs += [qspec(N_HEADS), pl.BlockSpec((N_HEADS, tq), lambda i, j: (0, jnp.minimum(i, j)))]
        args += [c, c_t]
    return _gridded_call(
        body, name=name, grid=(n_q, n_q), in_specs=in_specs, out_specs=[qspec(512), qspec(512), qspec(N_HEADS)],
        out_shape=[_sds((t, 512), BF16), _sds((t, 512), F32), _sds((t, N_HEADS), F32)],
        scratch_shapes=[pltpu.VMEM((N_HEADS, tq, 1), F32), pltpu.VMEM((N_HEADS, tq, 1), F32), pltpu.VMEM((2, tq, 512), F32)],
        args=args, exchange=exchange)


def _attn_bwd(q, k, v, do, delta, lse, c, c_t, *, lanes, name, exchange=None):
    t = q.shape[0]
    tq = min(ATT_TILE, t)
    n_q = t // tq
    hd = HEAD_DIM
    decay = c is not None

    def body(*refs):
        if decay:
            q_ref, k_ref, v_ref, do_ref, delta_ref, lse_ref, c_ref, ct_ref, dq_hbm, dk_ref, dv_ref, dct_ref, dq_ref = refs
        else:
            q_ref, k_ref, v_ref, do_ref, delta_ref, lse_ref, dq_hbm, dk_ref, dv_ref, dq_ref = refs
        j, i = pl.program_id(0), pl.program_id(1)

        @pl.when((j == 0) & (i == 0))
        def _():
            dq_ref[...] = jnp.zeros_like(dq_ref)

        @pl.when(i == j)
        def _():
            dk_ref[...] = jnp.zeros_like(dk_ref)
            dv_ref[...] = jnp.zeros_like(dv_ref)
            if decay:
                dct_ref[...] = jnp.zeros_like(dct_ref)

        def step(diagonal):
            if diagonal:
                keep = lax.broadcasted_iota(jnp.int32, (tq, tq), 0) >= lax.broadcasted_iota(jnp.int32, (tq, tq), 1)
            rows = pl.ds(pl.multiple_of(i * tq, tq), tq)
            for h in range(N_HEADS):
                wl = slice(h * lanes, (h + 1) * lanes)
                w = slice(h * hd, (h + 1) * hd)
                qh, kh = q_ref[:, wl], k_ref[:, wl]
                s = _dot(qh, kh, NT)
                if decay:
                    s = s + (c_ref[:, h:h + 1] - ct_ref[h:h + 1, :])
                if diagonal:
                    s = jnp.where(keep, s, MASK_VALUE)
                p = jnp.exp(s - lse_ref[:, h:h + 1])
                doh = do_ref[:, w]
                dv_ref[:, w] += _dot(p.astype(BF16), doh, TN)
                dp = _dot(doh, v_ref[:, w], NT)
                ds = p * (dp - delta_ref[:, h:h + 1])
                if decay:
                    dct_ref[h:h + 1, :] -= _colsum(ds)
                ds_b = ds.astype(BF16)
                dk_ref[:, wl] += _dot(ds_b, qh, TN)
                dq_ref[rows, wl] += _dot(ds_b, kh)

        @pl.when(i > j)
        def _():
            step(False)

        @pl.when(i == j)
        def _():
            step(True)

        @pl.when((j == n_q - 1) & (i == n_q - 1))
        def _():
            pltpu.sync_copy(dq_ref, dq_hbm)

    qspec = lambda n: pl.BlockSpec((tq, n), lambda j, i: (jnp.maximum(i, j), 0))
    kspec = lambda n: pl.BlockSpec((tq, n), lambda j, i: (j, 0))
    in_specs = [qspec(N_HEADS * lanes), kspec(N_HEADS * lanes), kspec(512), qspec(512), qspec(N_HEADS), qspec(N_HEADS)]
    out_specs = [pl.BlockSpec(memory_space=pl.ANY), kspec(N_HEADS * lanes), kspec(512)]
    out_shape = [_sds((t, N_HEADS * lanes), F32), _sds((t, N_HEADS * lanes), F32), _sds((t, 512), F32)]
    args = [q, k, v, do, delta, lse]
    if decay:
        ctspec = pl.BlockSpec((N_HEADS, tq), lambda j, i: (0, j))
        in_specs += [qspec(N_HEADS), ctspec]
        out_specs.append(ctspec)
        out_shape.append(_sds((N_HEADS, t), F32))
        args += [c, c_t]
    return _gridded_call(body, name=name, grid=(n_q, n_q), in_specs=in_specs, out_specs=out_specs, out_shape=out_shape,
                         scratch_shapes=[pltpu.VMEM((t, N_HEADS * lanes), F32)], args=args, exchange=exchange)


def _mix_fwd(x, y_mla, y_fox, gates, b_gate, wbm_t, wbf_t, wo):
    t = x.shape[0]
    tm = min(PROJ_TILE, t)

    def body(x_ref, ym_ref, yf_ref, gt_ref, bg_ref, wbm_ref, wbf_ref, wo_ref, out_ref):
        um = _dot(ym_ref[...], wbm_ref[...], NT)
        uf = _dot(yf_ref[...], wbf_ref[...], NT)
        sm = jax.nn.sigmoid(gt_ref[:, 0:D_MODEL] + bg_ref[0:1, :])
        sf = jax.nn.sigmoid(gt_ref[:, D_MODEL:2 * D_MODEL] + bg_ref[1:2, :])
        mixed = sm * um + sf * uf
        out_ref[...] = x_ref[...] + _dot(mixed.astype(BF16), wo_ref[...])

    tok = _tok(tm, D_MODEL)
    return pl.pallas_call(
        body, name="mix_fwd", grid=(t // tm,),
        in_specs=[tok, _tok(tm, 512), _tok(tm, 512), _tok(tm, 2 * D_MODEL), _full((2, D_MODEL)), _full(wbm_t.shape),
                  _full(wbf_t.shape), _full(wo.shape)],
        out_specs=tok, out_shape=_sds((t, D_MODEL), F32),
        compiler_params=pltpu.CompilerParams(dimension_semantics=("arbitrary",)),
    )(x, y_mla, y_fox, gates, b_gate, wbm_t, wbf_t, wo)


def _mix_bwd(dx, y_mla, y_fox, y_mla32, y_fox32, gates, b_gate, wbm_t, wbf_t, wo):
    t = dx.shape[0]
    tm = min(PROJ_TILE, t)

    def body(dx_ref, ym_ref, yf_ref, ym32_ref, yf32_ref, gt_ref, bg_ref, wbm_ref, wbf_ref, wo_ref,
             dym_ref, dyf_ref, dlm_ref, dlf_ref, dgt_ref, mixed_ref, dum_ref, duf_ref, dxb_ref, dbg_ref, prod):
        i = pl.program_id(0)
        dxb = dx_ref[...].astype(BF16)
        dxb_ref[...] = dxb
        dmixed = _dot(dxb, wo_ref[...], NT)
        um = _dot(ym_ref[...], wbm_ref[...], NT)
        uf = _dot(yf_ref[...], wbf_ref[...], NT)
        sm = jax.nn.sigmoid(gt_ref[:, 0:D_MODEL] + bg_ref[0:1, :])
        sf = jax.nn.sigmoid(gt_ref[:, D_MODEL:2 * D_MODEL] + bg_ref[1:2, :])
        mixed_ref[...] = (sm * um + sf * uf).astype(BF16)
        dum = (dmixed * sm).astype(BF16)
        duf = (dmixed * sf).astype(BF16)
        dum_ref[...] = dum
        duf_ref[...] = duf
        dgm = dmixed * um * (sm * (1.0 - sm))
        dgf = dmixed * uf * (sf * (1.0 - sf))
        dgt_ref[:, 0:D_MODEL] = dgm.astype(BF16)
        dgt_ref[:, D_MODEL:2 * D_MODEL] = dgf.astype(BF16)
        for du, wb_ref, y32_ref, dy_ref, dl_ref in ((dum, wbm_ref, ym32_ref, dym_ref, dlm_ref),
                                                    (duf, wbf_ref, yf32_ref, dyf_ref, dlf_ref)):
            dy = _dot(du, wb_ref[...])
            dy_ref[...] = dy.astype(BF16)
            prod[...] = dy.astype(BF16).astype(F32) * y32_ref[...]
            for h in range(N_HEADS):
                dl_ref[:, h:h + 1] = jnp.sum(prod[:, h * HEAD_DIM:(h + 1) * HEAD_DIM], axis=1, keepdims=True)

        @pl.when(i == 0)
        def _():
            dbg_ref[...] = jnp.zeros_like(dbg_ref)

        dbg_ref[0:1, :] += _colsum(dgm)
        dbg_ref[1:2, :] += _colsum(dgf)

    tok = _tok(tm, D_MODEL)
    tokb = _sds((t, D_MODEL), BF16)
    t512, t8 = _tok(tm, 512), _tok(tm, N_HEADS)
    return pl.pallas_call(
        body, name="mix_bwd", grid=(t // tm,),
        in_specs=[tok, t512, t512, t512, t512, _tok(tm, 2 * D_MODEL), _full((2, D_MODEL)), _full(wbm_t.shape),
                  _full(wbf_t.shape), _full(wo.shape)],
        out_specs=[t512, t512, t8, t8, _tok(tm, 2 * D_MODEL), tok, tok, tok, tok, _full((2, D_MODEL))],
        out_shape=[_sds((t, 512), BF16), _sds((t, 512), BF16), _sds((t, N_HEADS), F32), _sds((t, N_HEADS), F32),
                   _sds((t, 2 * D_MODEL), BF16), tokb, tokb, tokb, tokb, _sds((2, D_MODEL), F32)],
        scratch_shapes=[pltpu.VMEM((tm, 512), F32)],
        compiler_params=pltpu.CompilerParams(dimension_semantics=("arbitrary",)),
    )(dx, y_mla, y_fox, y_mla32, y_fox32, gates, b_gate, wbm_t, wbf_t, wo)


def _my_position():
    x, y, c = lax.axis_index("x"), lax.axis_index("y"), lax.axis_index("c")
    return x, y, c, 4 * x + 2 * y + c


def _peer(x, y, c, mask):
    px = 1 - x if mask & 4 else x
    py = 1 - y if mask & 2 else y
    pc = 1 - c if mask & 1 else c
    return (px, py, pc), 4 * px + 2 * py + pc


def _chip_peer(x, y, km):
    px = 1 - x if km & 2 else x
    py = 1 - y if km & 1 else y
    return px, py, 2 * px + py


_HBM = pl.BlockSpec(memory_space=pl.ANY)


def _wait_all(copies):
    for cp in copies:
        cp.wait()


class _ChipExchange:
    def __init__(self, gather, arrays):
        self.gather, self.arrays = gather, list(arrays)
        n = len(self.arrays)
        self.out_shape = [_sds((N_DEV * a.shape[0],) + a.shape[1:], a.dtype) if gather else _sds(a.shape, a.dtype)
                          for a in self.arrays]
        self.scratch_shapes = [pltpu.SemaphoreType.DMA((n, N_CHIP)), pltpu.SemaphoreType.DMA((n, N_CHIP)),
                               pltpu.SemaphoreType.DMA((n,))]

    def copies(self, srcs, dsts, send_sems, recv_sems, local_sems):
        x, y, c, me = _my_position()
        q_me = 2 * x + y
        out = []
        for a in range(len(self.arrays)):
            if self.gather:
                r = srcs[a].shape[0]
                local_src, dst = srcs[a], dsts[a].at[pl.ds(me * r, r)]
            else:
                local_src, dst = srcs[a].at[q_me], dsts[a].at[q_me]
            out.append(pltpu.make_async_copy(local_src, dst, local_sems.at[a]))
            for km in range(1, N_CHIP):
                px, py, q_peer = _chip_peer(x, y, km)
                out.append(pltpu.make_async_remote_copy(
                    src_ref=srcs[a] if self.gather else srcs[a].at[q_peer], dst_ref=dst, send_sem=send_sems.at[a, km],
                    recv_sem=recv_sems.at[a, km], device_id=(px, py, c), device_id_type=MESH))
        return out

    def standalone(self, name):
        n = len(self.arrays)

        def body(*refs):
            copies = self.copies(refs[:n], refs[n:2 * n], *refs[2 * n:])
            for cp in copies:
                cp.start()
            _wait_all(copies)

        return pl.pallas_call(body, name=name, in_specs=[_HBM] * n, out_specs=[_HBM] * n, out_shape=self.out_shape,
                              scratch_shapes=self.scratch_shapes)(*self.arrays)


def _gridded_call(body, *, name, grid, in_specs, out_specs, out_shape, scratch_shapes, args, exchange=None):
    params = pltpu.CompilerParams(dimension_semantics=("arbitrary",) * len(grid))
    if exchange is None:
        return pl.pallas_call(body, name=name, grid=grid, in_specs=in_specs, out_specs=out_specs, out_shape=out_shape,
                              scratch_shapes=scratch_shapes, compiler_params=params)(*args), None
    n_in, n_out, n_scr, n_x = len(in_specs), len(out_specs), len(scratch_shapes), len(exchange.arrays)

    def carrier(*refs):
        ins, x_src, refs = refs[:n_in], refs[n_in:n_in + n_x], refs[n_in + n_x:]
        outs, x_dst, refs = refs[:n_out], refs[n_out:n_out + n_x], refs[n_out + n_x:]
        copies = exchange.copies(x_src, x_dst, *refs[n_scr:])
        pids = [pl.program_id(d) for d in range(len(grid))]
        first = functools.reduce(jnp.logical_and, [p == 0 for p in pids])
        last = functools.reduce(jnp.logical_and, [p == g - 1 for p, g in zip(pids, grid)])

        @pl.when(first)
        def _():
            for cp in copies:
                cp.start()

        body(*ins, *outs, *refs[:n_scr])

        @pl.when(last)
        def _():
            _wait_all(copies)

    res = pl.pallas_call(
        carrier, name=name, grid=grid, in_specs=list(in_specs) + [_HBM] * n_x, out_specs=list(out_specs) + [_HBM] * n_x,
        out_shape=list(out_shape) + exchange.out_shape, scratch_shapes=list(scratch_shapes) + exchange.scratch_shapes,
        compiler_params=params)(*args, *exchange.arrays)
    return res[:n_out], res[n_out:]


def _gather_over_cores(arrays, name):
    n_arr = len(arrays)

    def body(*refs):
        srcs, dsts = refs[:n_arr], refs[n_arr:2 * n_arr]
        send_sems, recv_sems = refs[2 * n_arr:]
        x, y, c, _ = _my_position()
        copies = []
        for a in range(n_arr):
            r = srcs[a].shape[0] // N_DEV
            for q in range(N_CHIP):
                rows = pl.ds((2 * q + c) * r, r)
                copies.append(pltpu.make_async_remote_copy(
                    src_ref=srcs[a].at[rows], dst_ref=dsts[a].at[rows], send_sem=send_sems.at[a, q],
                    recv_sem=recv_sems.at[a, q], device_id=(x, y, 1 - c), device_id_type=MESH))
        for cp in copies:
            cp.start()
        _wait_all(copies)

    return pl.pallas_call(
        body, name=name, in_specs=[_HBM] * n_arr, out_specs=[_HBM] * n_arr,
        out_shape=[_sds(a.shape, a.dtype) for a in arrays], input_output_aliases={a: a for a in range(n_arr)},
        scratch_shapes=[pltpu.SemaphoreType.DMA((n_arr, N_CHIP)), pltpu.SemaphoreType.DMA((n_arr, N_CHIP))],
    )(*arrays)


def _grads_to_sibling(grads, name):
    n_arr = len(grads)

    def body(*refs):
        srcs, dsts = refs[:n_arr], refs[n_arr:2 * n_arr]
        send_sems, recv_sems = refs[2 * n_arr:]
        x, y, c, _ = _my_position()
        copies = []
        for a in range(n_arr):
            r = srcs[a].shape[0] // N_DEV
            for q in range(N_CHIP):
                copies.append(pltpu.make_async_remote_copy(
                    src_ref=srcs[a].at[pl.ds((2 * q + 1 - c) * r, r)], dst_ref=dsts[a].at[q], send_sem=send_sems.at[a, q],
                    recv_sem=recv_sems.at[a, q], device_id=(x, y, 1 - c), device_id_type=MESH))
        for cp in copies:
            cp.start()
        _wait_all(copies)

    return pl.pallas_call(
        body, name=name, in_specs=[_HBM] * n_arr, out_specs=[_HBM] * n_arr,
        out_shape=[_sds((N_CHIP, g.shape[0] // N_DEV) + g.shape[1:], g.dtype) for g in grads],
        scratch_shapes=[pltpu.SemaphoreType.DMA((n_arr, N_CHIP)), pltpu.SemaphoreType.DMA((n_arr, N_CHIP))],
    )(*grads)


def _pair_sum(grad, from_sibling, name):
    r, n = from_sibling.shape[1:]

    def body(g_ref, s_ref, o_ref):
        c = lax.axis_index("c")
        o_ref[...] = (g_ref[c].astype(F32) + s_ref[...].astype(F32)).astype(BF16)

    return pl.pallas_call(
        body, name=name, grid=(N_CHIP,),
        in_specs=[pl.BlockSpec((None, 2, r, n), lambda q: (q, 0, 0, 0)), pl.BlockSpec((None, r, n), lambda q: (q, 0, 0))],
        out_specs=pl.BlockSpec((None, r, n), lambda q: (q, 0, 0)), out_shape=_sds((N_CHIP, r, n), BF16),
    )(grad.reshape(N_CHIP, 2, r, n), from_sibling)


def _all_reduce_small(vec):
    r = vec.shape[0]

    def body(v_ref, o_ref, buf, send_sems, recv_sems):
        x, y, c, me = _my_position()
        buf[me] = v_ref[...]
        copies = []
        for mask in range(1, N_DEV):
            peer, _ = _peer(x, y, c, mask)
            cp = pltpu.make_async_remote_copy(src_ref=v_ref, dst_ref=buf.at[me], send_sem=send_sems.at[mask],
                                              recv_sem=recv_sems.at[mask], device_id=peer, device_id_type=MESH)
            cp.start()
            copies.append(cp)
        for cp in copies:
            cp.wait()
        total = buf[0]
        for s in range(1, N_DEV):
            total = total + buf[s]
        o_ref[...] = total

    vm = pl.BlockSpec(memory_space=pltpu.VMEM)
    return pl.pallas_call(
        body, name="all_reduce_small", in_specs=[vm], out_specs=vm, out_shape=_sds(vec.shape, F32),
        scratch_shapes=[pltpu.VMEM((N_DEV, r, 128), F32), pltpu.SemaphoreType.DMA((N_DEV,)), pltpu.SemaphoreType.DMA((N_DEV,))],
    )(vec)


def _adamw_math(w, g, m, v):
    m = ADAM_B1 * m + (1.0 - ADAM_B1) * g
    v = ADAM_B2 * v + (1.0 - ADAM_B2) * (g * g)
    m_hat = m / (1.0 - ADAM_B1 ** ADAM_STEP)
    v_hat = v / (1.0 - ADAM_B2 ** ADAM_STEP)
    delta = -ADAM_LR * (m_hat / (jnp.sqrt(v_hat) + ADAM_EPS) + ADAM_WD * w)
    return delta, m, v


def _reduce_adamw(slots, w, m, v, *, transpose, name):
    r, n = slots.shape[1:]
    padded = not transpose and w.shape[0] != r

    def body(s_ref, w_ref, m_ref, v_ref, g_ref, d_ref, nm_ref, nv_ref, *scr):
        g = s_ref[0].astype(F32)
        for s in range(1, slots.shape[0]):
            g = g + s_ref[s].astype(F32)
        if transpose:
            scr[0][...] = g.T
            g = scr[0][:, 0:w_ref.shape[1]]
        elif padded:
            scr[0][...] = g
            g = scr[0][0:w_ref.shape[0], :]
        g_ref[...] = g
        d_ref[...], nm_ref[...], nv_ref[...] = _adamw_math(w_ref[...], g, m_ref[...], v_ref[...])

    out = _sds(w.shape, F32)
    return pl.pallas_call(
        body, name=name, out_shape=[out, out, out, out],
        scratch_shapes=[pltpu.VMEM((n, r) if transpose else (r, n), F32)] if transpose or padded else [],
    )(slots, w, m, v)


def _adamw(g, w, m, v, name):
    def body(g_ref, w_ref, m_ref, v_ref, d_ref, nm_ref, nv_ref):
        d_ref[...], nm_ref[...], nv_ref[...] = _adamw_math(w_ref[...], g_ref[...], m_ref[...], v_ref[...])

    out = _sds(w.shape, F32)
    return pl.pallas_call(body, name=name, out_shape=[out, out, out])(g, w, m, v)


_SMALL = ["ffn1_norm", "mix_norm", "ffn2_norm", "mla_q_lat_norm", "mla_kv_lat_norm", "mla_q_nope_gain", "mla_q_rope_gain",
          "mla_k_nope_gain", "mla_k_rope_gain", "fox_q_gain", "fox_k_gain", "fox_b_f"]
_WEIGHTS = ["ffn1_norm", "ffn1_w_gate", "ffn1_w_up", "ffn1_w_down", "mix_norm", "w_in", "mla_q_lat_norm", "mla_w_qb",
            "mla_kv_lat_norm", "mla_w_kvb", "mla_q_nope_gain", "mla_q_rope_gain", "mla_k_nope_gain", "mla_k_rope_gain",
            "fox_q_gain", "fox_k_gain", "fox_b_f", "w_branch_mla", "w_branch_fox", "b_gate", "w_o", "ffn2_norm",
            "ffn2_w_gate", "ffn2_w_up", "ffn2_w_down"]
_IN_Q, _IN_KV, _IN_KR, _IN_FOX, _IN_F, _IN_GATES = (0, 192), (192, 128), (320, 32), (352, 1536), (1888, 8), (1896, 2048)


def _rows(a, seg):
    return a[seg[0]:seg[0] + seg[1]]


def _split_w_in(win_t):
    z = lambda n: jnp.zeros((n, D_MODEL), win_t.dtype)
    lat = jnp.concatenate([_rows(win_t, _IN_Q), z(LAT_KV - Q_LORA), _rows(win_t, _IN_KV), _rows(win_t, _IN_KR),
                           _rows(win_t, _IN_F), z(LAT_W - LAT_F - N_HEADS)], axis=0)
    return _rows(win_t, _IN_GATES), _rows(win_t, _IN_FOX), lat


def _join_w_in(d_gates, d_fox, d_lat):
    return jnp.concatenate([d_lat[LAT_Q:LAT_Q + Q_LORA], d_lat[LAT_KV:LAT_KV + KV_LORA], d_lat[LAT_KR:LAT_KR + ROPE_DIM],
                            d_fox, d_lat[LAT_F:LAT_F + N_HEADS], d_gates], axis=0)


def kernel(x, positions, ffn1_norm, ffn1_w_gate, ffn1_w_up, ffn1_w_down, mix_norm, w_in, mla_q_lat_norm, mla_w_qb, mla_kv_lat_norm, mla_w_kvb, mla_q_nope_gain, mla_q_rope_gain, mla_k_nope_gain, mla_k_rope_gain, fox_q_gain, fox_k_gain, fox_b_f, w_branch_mla, w_branch_fox, b_gate, w_o, ffn2_norm, ffn2_w_gate, ffn2_w_up, ffn2_w_down, loss_target, m_ffn1_norm, m_ffn1_w_gate, m_ffn1_w_up, m_ffn1_w_down, m_mix_norm, m_w_in, m_mla_q_lat_norm, m_mla_w_qb, m_mla_kv_lat_norm, m_mla_w_kvb, m_mla_q_nope_gain, m_mla_q_rope_gain, m_mla_k_nope_gain, m_mla_k_rope_gain, m_fox_q_gain, m_fox_k_gain, m_fox_b_f, m_w_branch_mla, m_w_branch_fox, m_b_gate, m_w_o, m_ffn2_norm, m_ffn2_w_gate, m_ffn2_w_up, m_ffn2_w_down, v_ffn1_norm, v_ffn1_w_gate, v_ffn1_w_up, v_ffn1_w_down, v_mix_norm, v_w_in, v_mla_q_lat_norm, v_mla_w_qb, v_mla_kv_lat_norm, v_mla_w_kvb, v_mla_q_nope_gain, v_mla_q_rope_gain, v_mla_k_nope_gain, v_mla_k_rope_gain, v_fox_q_gain, v_fox_k_gain, v_fox_b_f, v_w_branch_mla, v_w_branch_fox, v_b_gate, v_w_o, v_ffn2_norm, v_ffn2_w_gate, v_ffn2_w_up, v_ffn2_w_down):
    env = dict(locals())
    strip = lambda n, a: a if n in _SMALL else a[0]
    W = {n: strip(n, env[n]) for n in _WEIGHTS}
    M = {n: strip(n, env["m_" + n]) for n in _WEIGHTS}
    V = {n: strip(n, env["v_" + n]) for n in _WEIGHTS}
    xs = x[0]
    t = xs.shape[0]

    col_split = ["ffn1_w_gate", "ffn1_w_up", "ffn2_w_gate", "ffn2_w_up", "mla_w_qb", "mla_w_kvb", "w_branch_mla", "w_branch_fox"]
    row_split = ["ffn1_w_down", "ffn2_w_down", "w_o"]
    pieces = {n: W[n].T.astype(BF16) for n in col_split}
    pieces.update({n: W[n].astype(BF16) for n in row_split})
    pieces["w_in"] = jnp.pad(W["w_in"].T.astype(BF16), ((0, W_IN_PIECE_PAD - W_IN_PIECE), (0, 0)))
    pieces["b_gate"] = W["b_gate"].T
    group_a = ["ffn1_w_gate", "ffn1_w_up", "ffn1_w_down"]
    group_b = ["w_in", "mla_w_qb", "mla_w_kvb", "w_branch_mla", "w_branch_fox", "w_o", "b_gate"]
    group_c = ["ffn2_w_gate", "ffn2_w_up", "ffn2_w_down"]
    gather = lambda group: _ChipExchange(True, [pieces[n] for n in group])
    G = dict(zip(group_a, _gather_over_cores(gather(group_a).standalone("gather_ici_a"), "gather_d2d_a")))

    inv_freq = ROPE_THETA ** (-jnp.arange(ROPE_HALF, dtype=F32) / ROPE_HALF)
    ang = positions[0].astype(F32)[:, None] * inv_freq
    tables, mats = _prep_tables(jnp.cos(ang), jnp.sin(ang)), _group_matrices()
    gains = _head_gains(*[W[n] for n in ["mla_q_nope_gain", "mla_q_rope_gain", "mla_k_nope_gain", "mla_k_rope_gain",
                                         "fox_q_gain", "fox_k_gain"]])

    (x1, a1, b1), got_b = _ffn_fwd(xs, W["ffn1_norm"], G["ffn1_w_gate"], G["ffn1_w_up"], G["ffn1_w_down"],
                                   exchange=gather(group_b))
    G.update(zip(group_b, _gather_over_cores(got_b, "gather_d2d_b")))
    win_t = G["w_in"].reshape(N_DEV, W_IN_PIECE_PAD, D_MODEL)[:, :W_IN_PIECE].reshape(N_DEV * W_IN_PIECE, D_MODEL)
    wgate_t, wfox_t, wlat_t = _split_w_in(win_t)
    bg = G["b_gate"].T
    gates, fox, lat = _proj_fwd(x1, W["mix_norm"], wgate_t, wfox_t, wlat_t)
    prep_args = (fox, lat, tables, mats, *_interleave_weights(G["mla_w_qb"], G["mla_w_kvb"]), W["mla_q_lat_norm"],
                 W["mla_kv_lat_norm"], gains, W["fox_b_f"])
    fq, fk, fv, c, mq, mk, mv, fq_sq, fk_sq_max, mq_sq, mk_sq_max = _prep_fwd(*prep_args)
    c_t = c.T
    b_fox, ok_fox = _logit_bound(fq_sq, fk_sq_max)
    b_mla, ok_mla = _logit_bound(mq_sq, mk_sq_max)
    (y_fox, y_fox32, lse_fox), got_c = _attn_fwd(fq, fk, fv, b_fox, ok_fox, c, c_t, lanes=HEAD_DIM, name="fox_fwd",
                                                 exchange=gather(group_c))
    G.update(zip(group_c, _gather_over_cores(got_c, "gather_d2d_c")))
    (y_mla, y_mla32, lse_mla), _ = _attn_fwd(mq, mk, mv, b_mla, ok_mla, None, None, lanes=MLA_QK_LANES, name="mla_fwd")
    x2 = _mix_fwd(x1, y_mla, y_fox, gates, bg, G["w_branch_mla"], G["w_branch_fox"], G["w_o"])
    (dx3, a2, b2, loss_vec), _ = _ffn_fwd(x2, W["ffn2_norm"], G["ffn2_w_gate"], G["ffn2_w_up"], G["ffn2_w_down"],
                                          target=loss_target[0])

    def chip_sums(group, tag):
        from_sibling = _grads_to_sibling([grads[n] for n in group], "grads_d2d_" + tag)
        return _ChipExchange(False, [_pair_sum(grads[n], s, "pair_sum_" + n) for n, s in zip(group, from_sibling)])

    (dx2, dg_ffn2, da2, db2, h2, n2, dyh2), _ = _ffn_bwd(dx3, x2, W["ffn2_norm"], a2, b2, G["ffn2_w_gate"], G["ffn2_w_up"],
                                                        G["ffn2_w_down"], "ffn2_bwd")
    grads = {"ffn2_w_gate": _tn_matmul(da2, n2, "ffn2_dgate"), "ffn2_w_up": _tn_matmul(db2, n2, "ffn2_dup"),
             "ffn2_w_down": _tn_matmul(h2, dyh2, "ffn2_ddown")}
    dy_mla, dy_fox, delta_mla, delta_fox, dgates, mixed, dum, duf, dx2b, dbg = _mix_bwd(
        dx2, y_mla, y_fox, y_mla32, y_fox32, gates, bg, G["w_branch_mla"], G["w_branch_fox"], G["w_o"])
    grads["w_o"] = _tn_matmul(mixed, dx2b, "d_w_o")
    grads["w_branch_mla"] = _tn_matmul(dum, y_mla, "d_w_branch_mla")
    grads["w_branch_fox"] = _tn_matmul(duf, y_fox, "d_w_branch_fox")
    (dfq, dfk, dfv, dc_t), slots_c = _attn_bwd(fq, fk, fv, dy_fox, delta_fox, lse_fox, c, c_t, lanes=HEAD_DIM,
                                               name="fox_bwd", exchange=chip_sums(group_c, "c"))
    slots = dict(zip(group_c, slots_c))
    (dmq, dmk, dmv), _ = _attn_bwd(mq, mk, mv, dy_mla, delta_mla, lse_mla, None, None, lanes=MLA_QK_LANES, name="mla_bwd")
    dfox, dlat, dwq, dwk, dwv, d_gql, d_gkvl, d_gq, d_gk, d_gkr, d_gfq, d_gfk, d_bf = _prep_bwd(
        *prep_args, dfq, dfk, dfv, dc_t.T, dmq, dmk, dmv)
    grads["mla_w_qb"], grads["mla_w_kvb"] = _deinterleave_grads(dwq, dwk, dwv)
    fold = lambda a, width: a.reshape(N_HEADS, width).sum(axis=0)[None]
    d_gq, d_gk = fold(d_gq, LANES), fold(d_gk, LANES)
    d_prep_small = [d_gql, d_gkvl, d_gq[:, :HEAD_DIM], d_gq[:, HEAD_DIM:HEAD_DIM + ROPE_DIM], d_gk[:, :HEAD_DIM],
                    d_gkr[:, HEAD_DIM:HEAD_DIM + ROPE_DIM], fold(d_gfq, HEAD_DIM), fold(d_gfk, HEAD_DIM), d_bf]
    dx1, dg_mix, nmix = _proj_bwd(dgates, dfox, dlat, wgate_t, wfox_t, wlat_t, x1, W["mix_norm"], dx2)
    dwin_t = _join_w_in(_tn_matmul(dgates, nmix, "d_w_in_gates"), _tn_matmul(dfox, nmix, "d_w_in_fox"),
                        _tn_matmul(dlat, nmix, "d_w_in_lat"))
    grads["w_in"] = jnp.pad(dwin_t.reshape(N_DEV, W_IN_PIECE, D_MODEL), ((0, 0), (0, W_IN_PIECE_PAD - W_IN_PIECE), (0, 0))
                            ).reshape(N_DEV * W_IN_PIECE_PAD, D_MODEL)
    grad_group_b = [n for n in group_b if n != "b_gate"]
    (dx0, dg_ffn1, da1, db1, h1, n1, dyh1), slots_b = _ffn_bwd(dx1, xs, W["ffn1_norm"], a1, b1, G["ffn1_w_gate"], G["ffn1_w_up"],
                                                              G["ffn1_w_down"], "ffn1_bwd", exchange=chip_sums(grad_group_b, "b"))
    slots.update(zip(grad_group_b, slots_b))
    grads["ffn1_w_gate"] = _tn_matmul(da1, n1, "ffn1_dgate")
    grads["ffn1_w_up"], got = _tn_matmul(db1, n1, "ffn1_dup", exchange=chip_sums(["ffn1_w_gate"], "a_gate"))
    slots["ffn1_w_gate"] = got[0]
    grads["ffn1_w_down"], got = _tn_matmul(h1, dyh1, "ffn1_ddown", exchange=chip_sums(["ffn1_w_up"], "a_up"))
    slots["ffn1_w_up"] = got[0]
    slots["ffn1_w_down"] = chip_sums(["ffn1_w_down"], "a_down").standalone("grads_ici_a_down")[0]

    small_parts = [dg_ffn1, dg_mix, dg_ffn2] + list(d_prep_small) + [dbg.reshape(1, 2 * D_MODEL), loss_vec]
    flat = jnp.concatenate([p.reshape(-1) for p in small_parts])
    n_flat = flat.shape[0]
    rows = -(-n_flat // (8 * 128)) * 8
    total = _all_reduce_small(jnp.pad(flat, (0, rows * 128 - n_flat)).reshape(rows, 128)).reshape(-1)
    offs, small_g = 0, {}
    for n in _SMALL:
        small_g[n] = total[offs:offs + W[n].shape[1]].reshape(W[n].shape)
        offs += W[n].shape[1]
    bg_full = total[offs:offs + 2 * D_MODEL].reshape(2, D_MODEL)
    offs += 2 * D_MODEL
    loss = (0.5 / D_MODEL) * jnp.sum(total[offs:offs + D_MODEL])
    _, _, _, me = _my_position()
    small_g["b_gate"] = lax.dynamic_slice_in_dim(bg_full, me * (D_MODEL // N_DEV), D_MODEL // N_DEV, axis=1)

    transposed_in_memory = ["ffn1_w_gate", "ffn1_w_up", "ffn2_w_gate", "ffn2_w_up", "w_in"]
    res = {}
    for n in _WEIGHTS:
        if n in small_g:
            res[n] = (small_g[n],) + tuple(_adamw(small_g[n], W[n], M[n], V[n], "adamw_" + n))
        elif n in transposed_in_memory:
            res[n] = tuple(o.T for o in _reduce_adamw(slots[n], W[n].T, M[n].T, V[n].T, transpose=False, name="adamw_" + n))
        else:
            res[n] = tuple(_reduce_adamw(slots[n], W[n], M[n], V[n], transpose=n in col_split, name="adamw_" + n))
    outs = [loss, dx0[None]]
    for k in range(4):
        outs += [res[n][k] if n in _SMALL else res[n][k][None] for n in _WEIGHTS]
    return tuple(outs)
```

```python
import functools

import jax
import jax.numpy as jnp
from jax import lax
from jax.experimental import pallas as pl
from jax.experimental.pallas import tpu as pltpu

F32 = jnp.float32
BF16 = jnp.bfloat16

D_MODEL = 1024
FFN_HIDDEN = 2816
N_HEADS = 8
HEAD_DIM = 64
ROPE_DIM = 32
ROPE_HALF = 16
Q_LORA = 192
KV_LORA = 128
ROPE_THETA = 10000.0
RMS_EPS = 1e-6
MLA_SCALE = (HEAD_DIM + ROPE_DIM) ** -0.5
FOX_SCALE = HEAD_DIM ** -0.5
MLA_QK_LANES = 128
ADAM_LR, ADAM_B1, ADAM_B2, ADAM_EPS, ADAM_WD, ADAM_STEP = 0.001, 0.9, 0.999, 1e-08, 0.01, 10
N_DEV = 8
N_CHIP = 4
W_IN_PIECE = 493
W_IN_PIECE_PAD = 496
LAT_W = 512
LAT_Q, LAT_KV, LAT_KR, LAT_F = 0, 256, 384, 416
MASK_VALUE = -1e30
FIXED_SHIFT_MAX_BOUND = 30.0

TOK_TILE = 512
DW_TOK_TILE = 1024
PROJ_TILE = 512
ATT_TILE = 512
ATT_COL_CHUNK = 256
FFN_HID_TILE = 1408
FFN_HID_SPLIT = ((0, 768), (768, 1408))

NT = (((1,), (1,)), ((), ()))
TN = (((0,), (0,)), ((), ()))
NN = (((1,), (0,)), ((), ()))
MESH = pl.DeviceIdType.MESH


def _dot(a, b, dims=NN):
    return lax.dot_general(a, b, dims, preferred_element_type=F32)


def _sds(shape, dtype):
    return jax.ShapeDtypeStruct(shape, dtype)


def _rms_fwd(x, g):
    r = lax.rsqrt(jnp.mean(x * x, axis=-1, keepdims=True) + RMS_EPS)
    return x * r * g, r


def _rms_bwd(dy, x, g, r):
    xn = x * r
    dyg = dy * g
    dx = r * (dyg - xn * jnp.mean(dyg * xn, axis=-1, keepdims=True))
    return dx, dy * xn


def _colsum(x):
    return jnp.sum(x, axis=0, keepdims=True)


def _full(shape):
    return pl.BlockSpec(shape, lambda *_: (0,) * len(shape))


def _tok(tm, n):
    return pl.BlockSpec((tm, n), lambda i, *_: (i, 0))


def _ffn_fwd(x, gain, wg_t, wu_t, wd, target=None, exchange=None):
    t = x.shape[0]
    tm = min(TOK_TILE, t)
    tf = FFN_HID_TILE
    n_t, n_f = t // tm, FFN_HIDDEN // tf
    with_loss = target is not None

    def body(*refs):
        if with_loss:
            x_ref, g_ref, wg_ref, wu_ref, wd_ref, t_ref, out_ref, a_ref, b_ref, lvec_ref, n_scr, acc = refs
        else:
            x_ref, g_ref, wg_ref, wu_ref, wd_ref, out_ref, a_ref, b_ref, n_scr, acc = refs
        i, j = pl.program_id(0), pl.program_id(1)

        @pl.when(j == 0)
        def _():
            xn, _ = _rms_fwd(x_ref[...], g_ref[...])
            n_scr[...] = xn.astype(BF16)
            acc[...] = jnp.zeros_like(acc)

        n = n_scr[...]
        a = _dot(n, wg_ref[...], NT)
        b = _dot(n, wu_ref[...], NT)
        a_ref[...] = a.astype(BF16)
        b_ref[...] = b.astype(BF16)
        h = (a * jax.nn.sigmoid(a)) * b
        acc[...] += _dot(h.astype(BF16), wd_ref[...])

        @pl.when(j == n_f - 1)
        def _():
            y = x_ref[...] + 0.5 * acc[...]
            if with_loss:
                diff = y - t_ref[...]
                out_ref[...] = diff * (1.0 / D_MODEL)
                sq = _colsum(diff * diff)

                @pl.when(i == 0)
                def _():
                    lvec_ref[...] = sq

                @pl.when(i > 0)
                def _():
                    lvec_ref[...] += sq
            else:
                out_ref[...] = y

    wspec = pl.BlockSpec((tf, D_MODEL), lambda i, j: (j, 0))
    hspec = pl.BlockSpec((tm, tf), lambda i, j: (i, j))
    in_specs = [_tok(tm, D_MODEL), _full((1, D_MODEL)), wspec, wspec, wspec]
    out_specs = [_tok(tm, D_MODEL), hspec, hspec]
    out_shape = [_sds((t, D_MODEL), F32), _sds((t, FFN_HIDDEN), BF16), _sds((t, FFN_HIDDEN), BF16)]
    args = [x, gain, wg_t, wu_t, wd]
    if with_loss:
        in_specs.append(_tok(tm, D_MODEL))
        out_specs.append(_full((1, D_MODEL)))
        out_shape.append(_sds((1, D_MODEL), F32))
        args.append(target)
    return _gridded_call(
        body, name="ffn_fwd_loss" if with_loss else "ffn_fwd", grid=(n_t, n_f), in_specs=in_specs, out_specs=out_specs,
        out_shape=out_shape, scratch_shapes=[pltpu.VMEM((tm, D_MODEL), BF16), pltpu.VMEM((tm, D_MODEL), F32)],
        args=args, exchange=exchange)


def _ffn_bwd(dy, x, gain, a, b, wg_t, wu_t, wd, name, exchange=None):
    t = x.shape[0]
    tm = min(TOK_TILE, t)
    tf = FFN_HID_TILE
    n_t, n_f = t // tm, FFN_HIDDEN // tf

    def body(dy_ref, x_ref, g_ref, a_ref, b_ref, wg_ref, wu_ref, wd_ref,
             dx_ref, dg_ref, da_ref, db_ref, h_ref, n_ref, dyh_ref, acc):
        i, j = pl.program_id(0), pl.program_id(1)

        @pl.when(j == 0)
        def _():
            xn, _ = _rms_fwd(x_ref[...], g_ref[...])
            n_ref[...] = xn.astype(BF16)
            dyh_ref[...] = (0.5 * dy_ref[...]).astype(BF16)
            acc[...] = jnp.zeros_like(acc)

        dyh = dyh_ref[...]
        for lo, hi in FFN_HID_SPLIT:
            dh = _dot(dyh, wd_ref[lo:hi, :], NT)
            av = a_ref[:, lo:hi].astype(F32)
            bv = b_ref[:, lo:hi].astype(F32)
            s = jax.nn.sigmoid(av)
            silu = av * s
            da = (dh * bv * (s * (1.0 + av * (1.0 - s)))).astype(BF16)
            db = (dh * silu).astype(BF16)
            da_ref[:, lo:hi] = da
            db_ref[:, lo:hi] = db
            h_ref[:, lo:hi] = (silu * bv).astype(BF16)
            acc[...] += _dot(da, wg_ref[lo:hi, :]) + _dot(db, wu_ref[lo:hi, :])

        @pl.when(j == n_f - 1)
        def _():
            xv, g = x_ref[...], g_ref[...]
            r = lax.rsqrt(jnp.mean(xv * xv, axis=-1, keepdims=True) + RMS_EPS)
            dx, dg_rows = _rms_bwd(acc[...], xv, g, r)
            dx_ref[...] = dy_ref[...] + dx
            dg = _colsum(dg_rows)

            @pl.when(i == 0)
            def _():
                dg_ref[...] = dg

            @pl.when(i > 0)
            def _():
                dg_ref[...] += dg

    wspec = pl.BlockSpec((tf, D_MODEL), lambda i, j: (j, 0))
    hspec = pl.BlockSpec((tm, tf), lambda i, j: (i, j))
    tok = _tok(tm, D_MODEL)
    hid = _sds((t, FFN_HIDDEN), BF16)
    return _gridded_call(
        body, name=name, grid=(n_t, n_f),
        in_specs=[tok, tok, _full((1, D_MODEL)), hspec, hspec, wspec, wspec, wspec],
        out_specs=[tok, _full((1, D_MODEL)), hspec, hspec, hspec, tok, tok],
        out_shape=[_sds((t, D_MODEL), F32), _sds((1, D_MODEL), F32), hid, hid, hid,
                   _sds((t, D_MODEL), BF16), _sds((t, D_MODEL), BF16)],
        scratch_shapes=[pltpu.VMEM((tm, D_MODEL), F32)], args=[dy, x, gain, a, b, wg_t, wu_t, wd], exchange=exchange)


def _tn_matmul(a, b, name, exchange=None):
    t, m = a.shape
    n = b.shape[1]
    tk = min(DW_TOK_TILE, t)
    n_k = t // tk

    def body(a_ref, b_ref, o_ref, acc):
        k = pl.program_id(0)
        p = _dot(a_ref[...], b_ref[...], TN)

        @pl.when(k == 0)
        def _():
            acc[...] = p

        @pl.when(k > 0)
        def _():
            acc[...] += p

        @pl.when(k == n_k - 1)
        def _():
            o_ref[...] = acc[...].astype(BF16)

    (out,), got = _gridded_call(
        body, name=name, grid=(n_k,), in_specs=[_tok(tk, m), _tok(tk, n)], out_specs=[_full((m, n))],
        out_shape=[_sds((m, n), BF16)], scratch_shapes=[pltpu.VMEM((m, n), F32)], args=[a, b], exchange=exchange)
    return out if exchange is None else (out, got)


def _proj_fwd(x, gain, wgate_t, wfox_t, wlat_t):
    t = x.shape[0]
    tm = min(PROJ_TILE, t)

    def body(x_ref, g_ref, wg_ref, wf_ref, wl_ref, og_ref, of_ref, ol_ref):
        xn, _ = _rms_fwd(x_ref[...], g_ref[...])
        n = xn.astype(BF16)
        og_ref[...] = _dot(n, wg_ref[...], NT)
        of_ref[...] = _dot(n, wf_ref[...], NT)
        ol_ref[...] = _dot(n, wl_ref[...], NT)

    return pl.pallas_call(
        body, name="proj_fwd", grid=(t // tm,),
        in_specs=[_tok(tm, D_MODEL), _full((1, D_MODEL)), _full(wgate_t.shape), _full(wfox_t.shape), _full(wlat_t.shape)],
        out_specs=[_tok(tm, 2 * D_MODEL), _tok(tm, 3 * 512), _tok(tm, LAT_W)],
        out_shape=[_sds((t, 2 * D_MODEL), F32), _sds((t, 3 * 512), F32), _sds((t, LAT_W), F32)],
        compiler_params=pltpu.CompilerParams(dimension_semantics=("arbitrary",)),
    )(x, gain, wgate_t, wfox_t, wlat_t)


def _proj_bwd(dgates, dfox, dlat, wgate_t, wfox_t, wlat_t, x, gain, dres):
    t = x.shape[0]
    tm = min(PROJ_TILE, t)

    def body(dg_ref, df_ref, dl_ref, wg_ref, wf_ref, wl_ref, x_ref, g_ref, dres_ref, dx_ref, dgain_ref, n_ref):
        i = pl.program_id(0)
        dn = _dot(dg_ref[...], wg_ref[...]) + _dot(df_ref[...], wf_ref[...]) + _dot(dl_ref[...], wl_ref[...])
        xv, g = x_ref[...], g_ref[...]
        xn, r = _rms_fwd(xv, g)
        n_ref[...] = xn.astype(BF16)
        dx, dg_rows = _rms_bwd(dn, xv, g, r)
        dx_ref[...] = dres_ref[...] + dx
        dgn = _colsum(dg_rows)

        @pl.when(i == 0)
        def _():
            dgain_ref[...] = dgn

        @pl.when(i > 0)
        def _():
            dgain_ref[...] += dgn

    tok = _tok(tm, D_MODEL)
    return pl.pallas_call(
        body, name="proj_bwd", grid=(t // tm,),
        in_specs=[_tok(tm, 2 * D_MODEL), _tok(tm, 3 * 512), _tok(tm, LAT_W), _full(wgate_t.shape), _full(wfox_t.shape),
                  _full(wlat_t.shape), tok, _full((1, D_MODEL)), tok],
        out_specs=[tok, _full((1, D_MODEL)), tok],
        out_shape=[_sds((t, D_MODEL), F32), _sds((1, D_MODEL), F32), _sds((t, D_MODEL), BF16)],
        compiler_params=pltpu.CompilerParams(dimension_semantics=("arbitrary",)),
    )(dgates, dfox, dlat, wgate_t, wfox_t, wlat_t, x, gain, dres)


def _tri(n, lower):
    r = lax.broadcasted_iota(jnp.int32, (n, n), 0)
    c = lax.broadcasted_iota(jnp.int32, (n, n), 1)
    return ((c <= r) if lower else (c >= r)).astype(F32)


def _log_sigmoid(z):
    return jnp.minimum(z, 0.0) - jnp.log1p(jnp.exp(-jnp.abs(z)))


N_GROUPS = 16
LANES = MLA_QK_LANES


def _split_dot(a, g):
    hi = a.astype(BF16)
    lo = (a - hi.astype(F32)).astype(BF16)
    return _dot(hi, g) + _dot(lo, g)


def _rope_fwd(y, c, s1, s2):
    return y * c + pltpu.roll(y, LANES - ROPE_HALF, 1) * s1 + pltpu.roll(y, ROPE_HALF, 1) * s2


def _rope_bwd(do, c, s1, s2):
    return do * c + pltpu.roll(do * s1, ROPE_HALF, 1) + pltpu.roll(do * s2, LANES - ROPE_HALF, 1)


def _prep_tables(cos, sin):
    t = cos.shape[0]
    z = lambda n: jnp.zeros((t, n), F32)
    c = jnp.concatenate([jnp.ones((t, HEAD_DIM), F32), cos, cos, z(LANES - HEAD_DIM - ROPE_DIM)], axis=1)
    s1 = jnp.concatenate([z(HEAD_DIM), -sin, z(LANES - HEAD_DIM - ROPE_HALF)], axis=1)
    s2 = jnp.concatenate([z(HEAD_DIM + ROPE_HALF), sin, z(LANES - HEAD_DIM - ROPE_DIM)], axis=1)
    lane = jnp.arange(LANES)
    rope_mask = ((lane >= HEAD_DIM) & (lane < HEAD_DIM + ROPE_DIM)).astype(F32)[None, :]
    return c, s1, s2, rope_mask


def _group_matrices():
    lane = jnp.arange(N_HEADS * LANES)
    head, d = (lane // LANES)[:, None], (lane % LANES)[:, None]
    col = jnp.arange(N_GROUPS)[None, :]
    g_mla = ((col == head) & (d < HEAD_DIM)) | ((col == N_HEADS + head) & (d >= HEAD_DIM) & (d < HEAD_DIM + ROPE_DIM))
    g_fox = col == (jnp.arange(N_HEADS * HEAD_DIM) // HEAD_DIM)[:, None]
    inv_mla = jnp.concatenate([jnp.full((1, N_HEADS), 1.0 / HEAD_DIM, F32), jnp.full((1, N_HEADS), 1.0 / ROPE_DIM, F32)], axis=1)
    inv_fox = jnp.full((1, N_GROUPS), 1.0 / HEAD_DIM, F32)
    return g_mla.astype(BF16), g_mla.T.astype(BF16), inv_mla, g_fox.astype(BF16), g_fox.T.astype(BF16), inv_fox


def _interleave_weights(wqb_t, wkvb_t):
    wq = jnp.pad(wqb_t.reshape(N_HEADS, HEAD_DIM + ROPE_DIM, Q_LORA), ((0, 0), (0, LANES - HEAD_DIM - ROPE_DIM), (0, 0)))
    kv = wkvb_t.reshape(N_HEADS, 2, HEAD_DIM, KV_LORA)
    wk = jnp.pad(kv[:, 0], ((0, 0), (0, LANES - HEAD_DIM), (0, 0)))
    return wq.reshape(N_HEADS * LANES, Q_LORA), wk.reshape(N_HEADS * LANES, KV_LORA), kv[:, 1].reshape(N_HEADS * HEAD_DIM, KV_LORA)


def _deinterleave_grads(dwq, dwk, dwv):
    dq = dwq.reshape(N_HEADS, LANES, Q_LORA)[:, :HEAD_DIM + ROPE_DIM].reshape(N_HEADS * (HEAD_DIM + ROPE_DIM), Q_LORA)
    dk = dwk.reshape(N_HEADS, LANES, KV_LORA)[:, :HEAD_DIM]
    dkv = jnp.stack([dk, dwv.reshape(N_HEADS, HEAD_DIM, KV_LORA)], axis=1)
    return dq, dkv.reshape(N_HEADS * 2 * HEAD_DIM, KV_LORA)


def _head_gains(g_qn, g_qr, g_kn, g_kr, g_fq, g_fk):
    z = lambda n: jnp.zeros((1, n), F32)
    gq = jnp.concatenate([g_qn, g_qr, z(LANES - HEAD_DIM - ROPE_DIM)], axis=1)
    gk = jnp.concatenate([g_kn, z(LANES - HEAD_DIM)], axis=1)
    gkr = jnp.concatenate([z(HEAD_DIM), g_kr, z(LANES - HEAD_DIM - ROPE_DIM)], axis=1)
    return jnp.tile(gq, (1, N_HEADS)), jnp.tile(gk, (1, N_HEADS)), gkr, jnp.tile(g_fq, (1, N_HEADS)), jnp.tile(g_fk, (1, N_HEADS))


def _group_rms(x, g, g_t, inv):
    r = lax.rsqrt(_split_dot(x * x, g) * inv + RMS_EPS)
    return _split_dot(r, g_t)


def _prep_fwd(fox, lat, tables, mats, wq, wk, wv, g_qlat, g_kvlat, gains, b_f):
    t = fox.shape[0]
    tm = min(PROJ_TILE, t)
    c_tab, s1_tab, s2_tab, rope_mask = tables

    def body(fox_ref, lat_ref, c_ref, s1_ref, s2_ref, rm_ref, gm_ref, gmt_ref, im_ref, gf_ref, gft_ref, if_ref,
             wq_ref, wk_ref, wv_ref, gql_ref, gkvl_ref, gq_ref, gk_ref, gkr_ref, gfq_ref, gfk_ref, bf_ref,
             fq_ref, fk_ref, fv_ref, cc_ref, mq_ref, mk_ref, mv_ref, fqn_ref, fkmax_ref, mqn_ref, mkmax_ref, carry):
        i = pl.program_id(0)
        ct, s1, s2 = c_ref[...], s1_ref[...], s2_ref[...]

        @pl.when(i == 0)
        def _():
            carry[...] = jnp.zeros_like(carry)
            fkmax_ref[...] = jnp.zeros_like(fkmax_ref)
            mkmax_ref[...] = jnp.zeros_like(mkmax_ref)

        def square_sums(ref, g):
            v = ref[...].astype(F32)
            return _split_dot(v * v, g)

        xq = fox_ref[:, 0:512]
        fq_ref[...] = (xq * _group_rms(xq, gf_ref[...], gft_ref[...], if_ref[...]) * (gfq_ref[...] * FOX_SCALE)).astype(BF16)
        xk = fox_ref[:, 512:1024]
        fk_ref[...] = (xk * _group_rms(xk, gf_ref[...], gft_ref[...], if_ref[...]) * gfk_ref[...]).astype(BF16)
        fv_ref[...] = fox_ref[:, 1024:1536].astype(BF16)
        fqn_ref[...] = square_sums(fq_ref, gf_ref[...])
        fkmax_ref[...] = jnp.maximum(fkmax_ref[...], jnp.max(square_sums(fk_ref, gf_ref[...]), axis=0, keepdims=True))


        logf = _log_sigmoid(lat_ref[:, LAT_F:LAT_F + N_HEADS] + bf_ref[...])
        cc_ref[...] = jnp.dot(_tri(tm, True), logf, precision=lax.Precision.HIGHEST, preferred_element_type=F32) + carry[...]
        carry[...] += _colsum(logf)

        qlat_n, _ = _rms_fwd(lat_ref[:, LAT_Q:LAT_Q + Q_LORA], gql_ref[...])
        p = _dot(qlat_n.astype(BF16), wq_ref[...], NT)
        y = p * _group_rms(p, gm_ref[...], gmt_ref[...], im_ref[...]) * (gq_ref[...] * MLA_SCALE)
        for h in range(N_HEADS):
            w = slice(h * LANES, (h + 1) * LANES)
            mq_ref[:, w] = _rope_fwd(y[:, w], ct, s1, s2).astype(BF16)

        kv_n, _ = _rms_fwd(lat_ref[:, LAT_KV:LAT_KV + KV_LORA], gkvl_ref[...])
        kv_b = kv_n.astype(BF16)
        pk = _dot(kv_b, wk_ref[...], NT)
        kn = pk * _group_rms(pk, gm_ref[...], gmt_ref[...], im_ref[...]) * gk_ref[...]
        rm = rm_ref[...]
        kr = pltpu.roll(lat_ref[:, LAT_KR:LAT_KR + LANES], HEAD_DIM, 1) * rm
        rr = lax.rsqrt(jnp.sum(kr * kr, axis=1, keepdims=True) * (1.0 / ROPE_DIM) + RMS_EPS)
        okr = _rope_fwd(kr * rr * gkr_ref[...], ct * rm, s1, s2)
        for h in range(N_HEADS):
            w = slice(h * LANES, (h + 1) * LANES)
            mk_ref[:, w] = (kn[:, w] + okr).astype(BF16)
        mv_ref[...] = _dot(kv_b, wv_ref[...], NT).astype(BF16)
        mqn_ref[...] = square_sums(mq_ref, gm_ref[...])
        mkmax_ref[...] = jnp.maximum(mkmax_ref[...], jnp.max(square_sums(mk_ref, gm_ref[...]), axis=0, keepdims=True))

    consts = [rope_mask, *mats, wq, wk, wv, g_qlat, g_kvlat, *gains, b_f]
    t512, tl, tg, g1 = _tok(tm, 512), _tok(tm, LANES), _tok(tm, N_GROUPS), _full((1, N_GROUPS))
    return pl.pallas_call(
        body, name="prep_fwd", grid=(t // tm,),
        in_specs=[_tok(tm, 1536), _tok(tm, LAT_W), tl, tl, tl] + [_full(a.shape) for a in consts],
        out_specs=[t512, t512, t512, _tok(tm, N_HEADS), _tok(tm, N_HEADS * LANES), _tok(tm, N_HEADS * LANES), t512,
                   tg, g1, tg, g1],
        out_shape=[_sds((t, 512), BF16), _sds((t, 512), BF16), _sds((t, 512), BF16), _sds((t, N_HEADS), F32),
                   _sds((t, N_HEADS * LANES), BF16), _sds((t, N_HEADS * LANES), BF16), _sds((t, 512), BF16),
                   _sds((t, N_GROUPS), F32), _sds((1, N_GROUPS), F32), _sds((t, N_GROUPS), F32), _sds((1, N_GROUPS), F32)],
        scratch_shapes=[pltpu.VMEM((1, N_HEADS), F32)],
        compiler_params=pltpu.CompilerParams(dimension_semantics=("arbitrary",)),
    )(fox, lat, c_tab, s1_tab, s2_tab, *consts)


def _prep_bwd(fox, lat, tables, mats, wq, wk, wv, g_qlat, g_kvlat, gains, b_f, dfq, dfk, dfv, dc, dmq, dmk, dmv):
    t = fox.shape[0]
    tm = min(PROJ_TILE, t)
    n_t = t // tm
    c_tab, s1_tab, s2_tab, rope_mask = tables

    def body(fox_ref, lat_ref, c_ref, s1_ref, s2_ref, rm_ref, gm_ref, gmt_ref, im_ref, gf_ref, gft_ref, if_ref,
             wq_ref, wk_ref, wv_ref, gql_ref, gkvl_ref, gq_ref, gk_ref, gkr_ref, gfq_ref, gfk_ref, bf_ref,
             dfq_ref, dfk_ref, dfv_ref, dc_ref, dmq_ref, dmk_ref, dmv_ref,
             dfox_ref, dlat_ref, dwq_ref, dwk_ref, dwv_ref, o_gql, o_gkvl, o_gq, o_gk, o_gkr, o_gfq, o_gfk, o_bf,
             carry, lscr, wq_acc, wk_acc, wv_acc, dyscr):
        i = pl.program_id(0)
        ct, s1, s2, rm = c_ref[...], s1_ref[...], s2_ref[...], rm_ref[...]

        @pl.when(i == 0)
        def _():
            for ref in [carry, wq_acc, wk_acc, wv_acc, o_gql, o_gkvl, o_gq, o_gk, o_gkr, o_gfq, o_gfk, o_bf]:
                ref[...] = jnp.zeros_like(ref)

        def group_rms_bwd(dy, x, r, gain, g, g_t, inv):
            xn = x * r
            dyg = dy * gain
            mu = _split_dot(_split_dot(dyg * xn, g) * inv, g_t)
            return r * (dyg - xn * mu), dy * xn

        gf, gft, invf = gf_ref[...], gft_ref[...], if_ref[...]
        xq = fox_ref[:, 0:512]
        dxq, rows = group_rms_bwd(dfq_ref[...] * FOX_SCALE, xq, _group_rms(xq, gf, gft, invf), gfq_ref[...], gf, gft, invf)
        dfox_ref[:, 0:512] = dxq.astype(BF16)
        o_gfq[...] += _colsum(rows)
        xk = fox_ref[:, 512:1024]
        dxk, rows = group_rms_bwd(dfk_ref[...], xk, _group_rms(xk, gf, gft, invf), gfk_ref[...], gf, gft, invf)
        dfox_ref[:, 512:1024] = dxk.astype(BF16)
        o_gfk[...] += _colsum(rows)
        dfox_ref[:, 1024:1536] = dfv_ref[...].astype(BF16)

        lscr[...] = jnp.zeros_like(lscr)

        xql = lat_ref[:, LAT_Q:LAT_Q + Q_LORA]
        qlat_n, r_ql = _rms_fwd(xql, gql_ref[...])
        qlat_b = qlat_n.astype(BF16)
        gm, gmt, invm = gm_ref[...], gmt_ref[...], im_ref[...]
        p = _dot(qlat_b, wq_ref[...], NT)
        for h in range(N_HEADS):
            w = slice(h * LANES, (h + 1) * LANES)
            dyscr[:, w] = _rope_bwd(dmq_ref[:, w], ct, s1, s2) * MLA_SCALE
        dp, rows = group_rms_bwd(dyscr[...], p, _group_rms(p, gm, gmt, invm), gq_ref[...], gm, gmt, invm)
        o_gq[...] += _colsum(rows)
        dp_b = dp.astype(BF16)
        wq_acc[...] += _dot(dp_b, qlat_b, TN)
        dxql, rows = _rms_bwd(_dot(dp_b, wq_ref[...]), xql, gql_ref[...], r_ql)
        lscr[:, LAT_Q:LAT_Q + Q_LORA] = dxql
        o_gql[...] += _colsum(rows)

        xkv = lat_ref[:, LAT_KV:LAT_KV + KV_LORA]
        kv_n, r_kv = _rms_fwd(xkv, gkvl_ref[...])
        kv_b = kv_n.astype(BF16)
        pk = _dot(kv_b, wk_ref[...], NT)
        dpk, rows = group_rms_bwd(dmk_ref[...], pk, _group_rms(pk, gm, gmt, invm), gk_ref[...], gm, gmt, invm)
        o_gk[...] += _colsum(rows)
        dpk_b = dpk.astype(BF16)
        dv_b = dmv_ref[...].astype(BF16)
        wk_acc[...] += _dot(dpk_b, kv_b, TN)
        wv_acc[...] += _dot(dv_b, kv_b, TN)
        dxkv, rows = _rms_bwd(_dot(dpk_b, wk_ref[...]) + _dot(dv_b, wv_ref[...]), xkv, gkvl_ref[...], r_kv)
        lscr[:, LAT_KV:LAT_KV + KV_LORA] = dxkv
        o_gkvl[...] += _colsum(rows)

        dokr = dmk_ref[:, 0:LANES]
        for h in range(1, N_HEADS):
            dokr = dokr + dmk_ref[:, h * LANES:(h + 1) * LANES]
        dykr = _rope_bwd(dokr * rm, ct * rm, s1, s2)
        kr = pltpu.roll(lat_ref[:, LAT_KR:LAT_KR + LANES], HEAD_DIM, 1) * rm
        rr = lax.rsqrt(jnp.sum(kr * kr, axis=1, keepdims=True) * (1.0 / ROPE_DIM) + RMS_EPS)
        krn = kr * rr
        dyg = dykr * gkr_ref[...]
        dkr = rr * (dyg - krn * (jnp.sum(dyg * krn, axis=1, keepdims=True) * (1.0 / ROPE_DIM)))
        o_gkr[...] += _colsum(dykr * krn)
        lscr[:, LAT_KR:LAT_KR + LANES] = pltpu.roll(dkr, LANES - HEAD_DIM, 1)

        dcv = dc_ref[...]
        dlogf = jnp.dot(_tri(tm, False), dcv, precision=lax.Precision.HIGHEST, preferred_element_type=F32) + carry[...]
        carry[...] += _colsum(dcv)
        dz = dlogf * jax.nn.sigmoid(-(lat_ref[:, LAT_F:LAT_F + N_HEADS] + bf_ref[...]))
        lscr[:, LAT_F:LAT_F + N_HEADS] = dz
        o_bf[...] += _colsum(dz)

        dlat_ref[...] = lscr[...].astype(BF16)

        @pl.when(i == n_t - 1)
        def _():
            dwq_ref[...] = wq_acc[...].astype(BF16)
            dwk_ref[...] = wk_acc[...].astype(BF16)
            dwv_ref[...] = wv_acc[...].astype(BF16)

    consts = [rope_mask, *mats, wq, wk, wv, g_qlat, g_kvlat, *gains, b_f]

    def rtok(n):
        return pl.BlockSpec((tm, n), lambda i: (n_t - 1 - i, 0))

    sums = [(1, Q_LORA), (1, KV_LORA), (1, N_HEADS * LANES), (1, N_HEADS * LANES), (1, LANES), (1, 512), (1, 512), (1, N_HEADS)]
    return pl.pallas_call(
        body, name="prep_bwd", grid=(n_t,),
        in_specs=[rtok(1536), rtok(LAT_W), rtok(LANES), rtok(LANES), rtok(LANES)] + [_full(a.shape) for a in consts]
        + [rtok(512), rtok(512), rtok(512), rtok(N_HEADS), rtok(N_HEADS * LANES), rtok(N_HEADS * LANES), rtok(512)],
        out_specs=[rtok(1536), rtok(LAT_W), _full(wq.shape), _full(wk.shape), _full(wv.shape)] + [_full(s) for s in sums],
        out_shape=[_sds((t, 1536), BF16), _sds((t, LAT_W), BF16), _sds(wq.shape, BF16), _sds(wk.shape, BF16), _sds(wv.shape, BF16)]
        + [_sds(s, F32) for s in sums],
        scratch_shapes=[pltpu.VMEM((1, N_HEADS), F32), pltpu.VMEM((tm, LAT_W), F32), pltpu.VMEM(wq.shape, F32),
                        pltpu.VMEM(wk.shape, F32), pltpu.VMEM(wv.shape, F32), pltpu.VMEM((tm, N_HEADS * LANES), F32)],
        compiler_params=pltpu.CompilerParams(dimension_semantics=("arbitrary",)),
    )(fox, lat, c_tab, s1_tab, s2_tab, *consts, dfq, dfk, dfv, dc, dmq, dmk, dmv)


def _causal_pairs(n, query_major):
    pairs = [(i, j) for i in range(n) for j in range(i + 1)] if query_major else [(i, j) for j in range(n) for i in range(j, n)]
    return jnp.asarray([p[0] for p in pairs], jnp.int32), jnp.asarray([p[1] for p in pairs], jnp.int32)


def _logit_bound(q_sq, k_sq_max):
    qn = q_sq[:, :N_HEADS] + q_sq[:, N_HEADS:]
    kmax = k_sq_max[:, :N_HEADS] + k_sq_max[:, N_HEADS:]
    bound = jnp.sqrt(qn * kmax) * (1.0 + 2.0 ** -10) + 2.0 ** -10
    flag = (jnp.max(bound) <= FIXED_SHIFT_MAX_BOUND).astype(F32).reshape(1, 1)
    return bound, flag


def _attn_fwd(q, k, v, bound, fixed_ok, c, c_t, *, lanes, name, exchange=None):
    t = q.shape[0]
    tq = min(ATT_TILE, t)
    n_q = t // tq
    hd = HEAD_DIM
    ch = min(ATT_COL_CHUNK, tq)
    decay = c is not None

    def body(qi_ref, kj_ref, *refs):
        if decay:
            q_ref, k_ref, v_ref, b_ref, ok_ref, c_ref, ct_ref, o_ref, o32_ref, lse_ref, m_scr, l_scr, acc = refs
        else:
            q_ref, k_ref, v_ref, b_ref, ok_ref, o_ref, o32_ref, lse_ref, m_scr, l_scr, acc = refs
        i, j = qi_ref[pl.program_id(0)], kj_ref[pl.program_id(0)]
        fixed = ok_ref[0, 0] > 0.5

        @pl.when(j == 0)
        def _():
            m_scr[...] = jnp.full_like(m_scr, MASK_VALUE)
            l_scr[...] = jnp.zeros_like(l_scr)
            acc[...] = jnp.zeros_like(acc)

        def fixed_step(diagonal):
            for h in range(N_HEADS):
                wl = slice(h * lanes, (h + 1) * lanes)
                w = slice(h * hd, (h + 1) * hd)
                qh = q_ref[:, wl]
                row = (c_ref[:, h:h + 1] - b_ref[:, h:h + 1]) if decay else -b_ref[:, h:h + 1]
                l_new = jnp.zeros((tq, 1), F32)
                o_hi = jnp.zeros((tq, hd), F32)
                o_lo = jnp.zeros((tq, hd), F32)
                for cc in range(tq // ch):
                    cols = slice(cc * ch, (cc + 1) * ch)
                    s = _dot(qh, k_ref[cols, wl], NT)
                    s = s + ((row - ct_ref[h:h + 1, cols]) if decay else row)
                    if diagonal:
                        keep = (lax.broadcasted_iota(jnp.int32, (tq, ch), 0)
                                >= lax.broadcasted_iota(jnp.int32, (tq, ch), 1) + cc * ch)
                        s = jnp.where(keep, s, MASK_VALUE)
                    p = jnp.exp(s)
                    l_new = l_new + jnp.sum(p, axis=1, keepdims=True)
                    p_b = p.astype(BF16)
                    o_hi = o_hi + _dot(p_b, v_ref[cols, w])
                    if decay:
                        o_lo = o_lo + _dot((p - p_b.astype(F32)).astype(BF16), v_ref[cols, w])
                l_scr[h] += l_new
                acc[0, :, w] += o_hi
                if decay:
                    acc[1, :, w] += o_lo

        def step(diagonal):
            if diagonal:
                keep = lax.broadcasted_iota(jnp.int32, (tq, tq), 0) >= lax.broadcasted_iota(jnp.int32, (tq, tq), 1)
            for h in range(N_HEADS):
                s = _dot(q_ref[:, h * lanes:(h + 1) * lanes], k_ref[:, h * lanes:(h + 1) * lanes], NT)
                if decay:
                    s = s + (c_ref[:, h:h + 1] - ct_ref[h:h + 1, :])
                if diagonal:
                    s = jnp.where(keep, s, MASK_VALUE)
                m_prev = m_scr[h]
                m_new = jnp.maximum(m_prev, jnp.max(s, axis=1, keepdims=True))
                alpha = jnp.exp(m_prev - m_new)
                p = jnp.exp(s - m_new)
                l_scr[h] = alpha * l_scr[h] + jnp.sum(p, axis=1, keepdims=True)
                w = slice(h * hd, (h + 1) * hd)
                p_b = p.astype(BF16)
                acc[0, :, w] = alpha * acc[0, :, w] + _dot(p_b, v_ref[:, w])
                if decay:
                    p_lo = (p - p_b.astype(F32)).astype(BF16)
                    acc[1, :, w] = alpha * acc[1, :, w] + _dot(p_lo, v_ref[:, w])
                m_scr[h] = m_new

        for diagonal, here in ((False, j < i), (True, j == i)):
            @pl.when(here & fixed)
            def _():
                fixed_step(diagonal)

            @pl.when(here & jnp.logical_not(fixed))
            def _():
                step(diagonal)

        @pl.when(j == i)
        def _():
            for h in range(N_HEADS):
                w = slice(h * hd, (h + 1) * hd)
                l = l_scr[h]
                o_ref[:, w] = (acc[0, :, w] / l).astype(BF16)
                o32_ref[:, w] = ((acc[0, :, w] + acc[1, :, w]) if decay else acc[0, :, w]) / l
                lse_ref[:, h:h + 1] = jnp.where(fixed, b_ref[:, h:h + 1], m_scr[h]) + jnp.log(l)

    qspec = lambda n: pl.BlockSpec((tq, n), lambda s, qi, kj: (qi[s], 0))
    kspec = lambda n: pl.BlockSpec((tq, n), lambda s, qi, kj: (kj[s], 0))
    in_specs = [qspec(N_HEADS * lanes), kspec(N_HEADS * lanes), kspec(512), qspec(N_HEADS),
                pl.BlockSpec(memory_space=pltpu.SMEM)]
    args = [q, k, v, bound, fixed_ok]
    if decay:
        in_specs += [qspec(N_HEADS), pl.BlockSpec((N_HEADS, tq), lambda s, qi, kj: (0, kj[s]))]
        args += [c, c_t]
    pairs = _causal_pairs(n_q, query_major=True)
    return _gridded_call(
        body, name=name, grid=(pairs[0].shape[0],), in_specs=in_specs, out_specs=[qspec(512), qspec(512), qspec(N_HEADS)],
        out_shape=[_sds((t, 512), BF16), _sds((t, 512), F32), _sds((t, N_HEADS), F32)],
        scratch_shapes=[pltpu.VMEM((N_HEADS, tq, 1), F32), pltpu.VMEM((N_HEADS, tq, 1), F32), pltpu.VMEM((2, tq, 512), F32)],
        args=args, exchange=exchange, prefetch=pairs)


def _attn_bwd(q, k, v, do, delta, lse, c, c_t, *, lanes, name, exchange=None):
    t = q.shape[0]
    tq = min(ATT_TILE, t)
    n_q = t // tq
    hd = HEAD_DIM
    decay = c is not None

    pairs = _causal_pairs(n_q, query_major=False)
    n_steps = pairs[0].shape[0]

    def body(qi_ref, kj_ref, *refs):
        if decay:
            q_ref, k_ref, v_ref, do_ref, delta_ref, lse_ref, c_ref, ct_ref, dq_hbm, dk_ref, dv_ref, dct_ref, dq_ref = refs
        else:
            q_ref, k_ref, v_ref, do_ref, delta_ref, lse_ref, dq_hbm, dk_ref, dv_ref, dq_ref = refs
        step_id = pl.program_id(0)
        i, j = qi_ref[step_id], kj_ref[step_id]

        @pl.when(step_id == 0)
        def _():
            dq_ref[...] = jnp.zeros_like(dq_ref)

        @pl.when(i == j)
        def _():
            dk_ref[...] = jnp.zeros_like(dk_ref)
            dv_ref[...] = jnp.zeros_like(dv_ref)
            if decay:
                dct_ref[...] = jnp.zeros_like(dct_ref)

        def step(diagonal):
            if diagonal:
                keep = lax.broadcasted_iota(jnp.int32, (tq, tq), 0) >= lax.broadcasted_iota(jnp.int32, (tq, tq), 1)
            rows = pl.ds(pl.multiple_of(i * tq, tq), tq)
            for h in range(N_HEADS):
                wl = slice(h * lanes, (h + 1) * lanes)
                w = slice(h * hd, (h + 1) * hd)
                qh, kh = q_ref[:, wl], k_ref[:, wl]
                s = _dot(qh, kh, NT)
                if decay:
                    s = s + (c_ref[:, h:h + 1] - ct_ref[h:h + 1, :])
                if diagonal:
                    s = jnp.where(keep, s, MASK_VALUE)
                p = jnp.exp(s - lse_ref[:, h:h + 1])
                doh = do_ref[:, w]
                dv_ref[:, w] += _dot(p.astype(BF16), doh, TN)
                dp = _dot(doh, v_ref[:, w], NT)
                ds = p * (dp - delta_ref[:, h:h + 1])
                if decay:
                    dct_ref[h:h + 1, :] -= _colsum(ds)
                ds_b = ds.astype(BF16)
                dk_ref[:, wl] += _dot(ds_b, qh, TN)
                dq_ref[rows, wl] += _dot(ds_b, kh)

        @pl.when(i > j)
        def _():
            step(False)

        @pl.when(i == j)
        def _():
            step(True)

        @pl.when(step_id == n_steps - 1)
        def _():
            pltpu.sync_copy(dq_ref, dq_hbm)

    qspec = lambda n: pl.BlockSpec((tq, n), lambda s, qi, kj: (qi[s], 0))
    kspec = lambda n: pl.BlockSpec((tq, n), lambda s, qi, kj: (kj[s], 0))
    in_specs = [qspec(N_HEADS * lanes), kspec(N_HEADS * lanes), kspec(512), qspec(512), qspec(N_HEADS), qspec(N_HEADS)]
    out_specs = [pl.BlockSpec(memory_space=pl.ANY), kspec(N_HEADS * lanes), kspec(512)]
    out_shape = [_sds((t, N_HEADS * lanes), F32), _sds((t, N_HEADS * lanes), F32), _sds((t, 512), F32)]
    args = [q, k, v, do, delta, lse]
    if decay:
        ctspec = pl.BlockSpec((N_HEADS, tq), lambda s, qi, kj: (0, kj[s]))
        in_specs += [qspec(N_HEADS), ctspec]
        out_specs.append(ctspec)
        out_shape.append(_sds((N_HEADS, t), F32))
        args += [c, c_t]
    return _gridded_call(body, name=name, grid=(n_steps,), in_specs=in_specs, out_specs=out_specs, out_shape=out_shape,
                         scratch_shapes=[pltpu.VMEM((t, N_HEADS * lanes), F32)], args=args, exchange=exchange, prefetch=pairs)


def _mix_fwd(x, y_mla, y_fox, gates, b_gate, wbm_t, wbf_t, wo):
    t = x.shape[0]
    tm = min(PROJ_TILE, t)

    def body(x_ref, ym_ref, yf_ref, gt_ref, bg_ref, wbm_ref, wbf_ref, wo_ref, out_ref):
        um = _dot(ym_ref[...], wbm_ref[...], NT)
        uf = _dot(yf_ref[...], wbf_ref[...], NT)
        sm = jax.nn.sigmoid(gt_ref[:, 0:D_MODEL] + bg_ref[0:1, :])
        sf = jax.nn.sigmoid(gt_ref[:, D_MODEL:2 * D_MODEL] + bg_ref[1:2, :])
        mixed = sm * um + sf * uf
        out_ref[...] = x_ref[...] + _dot(mixed.astype(BF16), wo_ref[...])

    tok = _tok(tm, D_MODEL)
    return pl.pallas_call(
        body, name="mix_fwd", grid=(t // tm,),
        in_specs=[tok, _tok(tm, 512), _tok(tm, 512), _tok(tm, 2 * D_MODEL), _full((2, D_MODEL)), _full(wbm_t.shape),
                  _full(wbf_t.shape), _full(wo.shape)],
        out_specs=tok, out_shape=_sds((t, D_MODEL), F32),
        compiler_params=pltpu.CompilerParams(dimension_semantics=("arbitrary",)),
    )(x, y_mla, y_fox, gates, b_gate, wbm_t, wbf_t, wo)


def _mix_bwd(dx, y_mla, y_fox, y_mla32, y_fox32, gates, b_gate, wbm_t, wbf_t, wo):
    t = dx.shape[0]
    tm = min(PROJ_TILE, t)

    def body(dx_ref, ym_ref, yf_ref, ym32_ref, yf32_ref, gt_ref, bg_ref, wbm_ref, wbf_ref, wo_ref,
             dym_ref, dyf_ref, dlm_ref, dlf_ref, dgt_ref, mixed_ref, dum_ref, duf_ref, dxb_ref, dbg_ref, prod):
        i = pl.program_id(0)
        dxb = dx_ref[...].astype(BF16)
        dxb_ref[...] = dxb
        dmixed = _dot(dxb, wo_ref[...], NT)
        um = _dot(ym_ref[...], wbm_ref[...], NT)
        uf = _dot(yf_ref[...], wbf_ref[...], NT)
        sm = jax.nn.sigmoid(gt_ref[:, 0:D_MODEL] + bg_ref[0:1, :])
        sf = jax.nn.sigmoid(gt_ref[:, D_MODEL:2 * D_MODEL] + bg_ref[1:2, :])
        mixed_ref[...] = (sm * um + sf * uf).astype(BF16)
        dum = (dmixed * sm).astype(BF16)
        duf = (dmixed * sf).astype(BF16)
        dum_ref[...] = dum
        duf_ref[...] = duf
        dgm = dmixed * um * (sm * (1.0 - sm))
        dgf = dmixed * uf * (sf * (1.0 - sf))
        dgt_ref[:, 0:D_MODEL] = dgm.astype(BF16)
        dgt_ref[:, D_MODEL:2 * D_MODEL] = dgf.astype(BF16)
        for du, wb_ref, y32_ref, dy_ref, dl_ref in ((dum, wbm_ref, ym32_ref, dym_ref, dlm_ref),
                                                    (duf, wbf_ref, yf32_ref, dyf_ref, dlf_ref)):
            dy = _dot(du, wb_ref[...])
            dy_ref[...] = dy.astype(BF16)
            prod[...] = dy.astype(BF16).astype(F32) * y32_ref[...]
            for h in range(N_HEADS):
                dl_ref[:, h:h + 1] = jnp.sum(prod[:, h * HEAD_DIM:(h + 1) * HEAD_DIM], axis=1, keepdims=True)

        @pl.when(i == 0)
        def _():
            dbg_ref[...] = jnp.zeros_like(dbg_ref)

        dbg_ref[0:1, :] += _colsum(dgm)
        dbg_ref[1:2, :] += _colsum(dgf)

    tok = _tok(tm, D_MODEL)
    tokb = _sds((t, D_MODEL), BF16)
    t512, t8 = _tok(tm, 512), _tok(tm, N_HEADS)
    return pl.pallas_call(
        body, name="mix_bwd", grid=(t // tm,),
        in_specs=[tok, t512, t512, t512, t512, _tok(tm, 2 * D_MODEL), _full((2, D_MODEL)), _full(wbm_t.shape),
                  _full(wbf_t.shape), _full(wo.shape)],
        out_specs=[t512, t512, t8, t8, _tok(tm, 2 * D_MODEL), tok, tok, tok, tok, _full((2, D_MODEL))],
        out_shape=[_sds((t, 512), BF16), _sds((t, 512), BF16), _sds((t, N_HEADS), F32), _sds((t, N_HEADS), F32),
                   _sds((t, 2 * D_MODEL), BF16), tokb, tokb, tokb, tokb, _sds((2, D_MODEL), F32)],
        scratch_shapes=[pltpu.VMEM((tm, 512), F32)],
        compiler_params=pltpu.CompilerParams(dimension_semantics=("arbitrary",)),
    )(dx, y_mla, y_fox, y_mla32, y_fox32, gates, b_gate, wbm_t, wbf_t, wo)


def _my_position():
    x, y, c = lax.axis_index("x"), lax.axis_index("y"), lax.axis_index("c")
    return x, y, c, 4 * x + 2 * y + c


def _peer(x, y, c, mask):
    px = 1 - x if mask & 4 else x
    py = 1 - y if mask & 2 else y
    pc = 1 - c if mask & 1 else c
    return (px, py, pc), 4 * px + 2 * py + pc


def _chip_peer(x, y, km):
    px = 1 - x if km & 2 else x
    py = 1 - y if km & 1 else y
    return px, py, 2 * px + py


_HBM = pl.BlockSpec(memory_space=pl.ANY)


def _wait_all(copies):
    for cp in copies:
        cp.wait()


class _ChipExchange:
    def __init__(self, gather, arrays):
        self.gather, self.arrays = gather, list(arrays)
        n = len(self.arrays)
        self.out_shape = [_sds((N_DEV * a.shape[0],) + a.shape[1:], a.dtype) if gather else _sds(a.shape, a.dtype)
                          for a in self.arrays]
        self.scratch_shapes = [pltpu.SemaphoreType.DMA((n, N_CHIP)), pltpu.SemaphoreType.DMA((n, N_CHIP)),
                               pltpu.SemaphoreType.DMA((n,))]

    def copies(self, srcs, dsts, send_sems, recv_sems, local_sems):
        x, y, c, me = _my_position()
        q_me = 2 * x + y
        out = []
        for a in range(len(self.arrays)):
            if self.gather:
                r = srcs[a].shape[0]
                local_src, dst = srcs[a], dsts[a].at[pl.ds(me * r, r)]
            else:
                local_src, dst = srcs[a].at[q_me], dsts[a].at[q_me]
            out.append(pltpu.make_async_copy(local_src, dst, local_sems.at[a]))
            for km in range(1, N_CHIP):
                px, py, q_peer = _chip_peer(x, y, km)
                out.append(pltpu.make_async_remote_copy(
                    src_ref=srcs[a] if self.gather else srcs[a].at[q_peer], dst_ref=dst, send_sem=send_sems.at[a, km],
                    recv_sem=recv_sems.at[a, km], device_id=(px, py, c), device_id_type=MESH))
        return out

    def standalone(self, name):
        n = len(self.arrays)

        def body(*refs):
            copies = self.copies(refs[:n], refs[n:2 * n], *refs[2 * n:])
            for cp in copies:
                cp.start()
            _wait_all(copies)

        return pl.pallas_call(body, name=name, in_specs=[_HBM] * n, out_specs=[_HBM] * n, out_shape=self.out_shape,
                              scratch_shapes=self.scratch_shapes)(*self.arrays)


def _gridded_call(body, *, name, grid, in_specs, out_specs, out_shape, scratch_shapes, args, exchange=None, prefetch=()):
    params = pltpu.CompilerParams(dimension_semantics=("arbitrary",) * len(grid))
    n_pre, n_in, n_out, n_scr = len(prefetch), len(in_specs), len(out_specs), len(scratch_shapes)
    n_x = 0 if exchange is None else len(exchange.arrays)

    def carrier(*refs):
        pre, refs = refs[:n_pre], refs[n_pre:]
        ins, x_src, refs = refs[:n_in], refs[n_in:n_in + n_x], refs[n_in + n_x:]
        outs, x_dst, refs = refs[:n_out], refs[n_out:n_out + n_x], refs[n_out + n_x:]
        copies = exchange.copies(x_src, x_dst, *refs[n_scr:])
        pids = [pl.program_id(d) for d in range(len(grid))]
        first = functools.reduce(jnp.logical_and, [p == 0 for p in pids])
        last = functools.reduce(jnp.logical_and, [p == g - 1 for p, g in zip(pids, grid)])

        @pl.when(first)
        def _():
            for cp in copies:
                cp.start()

        body(*pre, *ins, *outs, *refs[:n_scr])

        @pl.when(last)
        def _():
            _wait_all(copies)

    x_shapes, x_scratch, x_args = ([], [], []) if exchange is None else (exchange.out_shape, exchange.scratch_shapes, exchange.arrays)
    spec = pltpu.PrefetchScalarGridSpec(
        num_scalar_prefetch=n_pre, grid=grid, in_specs=list(in_specs) + [_HBM] * n_x, out_specs=list(out_specs) + [_HBM] * n_x,
        scratch_shapes=list(scratch_shapes) + x_scratch)
    res = pl.pallas_call(body if exchange is None else carrier, name=name, grid_spec=spec, out_shape=list(out_shape) + x_shapes,
                         compiler_params=params)(*prefetch, *args, *x_args)
    return res[:n_out], (None if exchange is None else res[n_out:])


def _gather_over_cores(arrays, name):
    n_arr = len(arrays)

    def body(*refs):
        srcs, dsts = refs[:n_arr], refs[n_arr:2 * n_arr]
        send_sems, recv_sems = refs[2 * n_arr:]
        x, y, c, _ = _my_position()
        copies = []
        for a in range(n_arr):
            r = srcs[a].shape[0] // N_DEV
            for q in range(N_CHIP):
                rows = pl.ds((2 * q + c) * r, r)
                copies.append(pltpu.make_async_remote_copy(
                    src_ref=srcs[a].at[rows], dst_ref=dsts[a].at[rows], send_sem=send_sems.at[a, q],
                    recv_sem=recv_sems.at[a, q], device_id=(x, y, 1 - c), device_id_type=MESH))
        for cp in copies:
            cp.start()
        _wait_all(copies)

    return pl.pallas_call(
        body, name=name, in_specs=[_HBM] * n_arr, out_specs=[_HBM] * n_arr,
        out_shape=[_sds(a.shape, a.dtype) for a in arrays], input_output_aliases={a: a for a in range(n_arr)},
        scratch_shapes=[pltpu.SemaphoreType.DMA((n_arr, N_CHIP)), pltpu.SemaphoreType.DMA((n_arr, N_CHIP))],
    )(*arrays)


def _grads_to_sibling(grads, name):
    n_arr = len(grads)

    def body(*refs):
        srcs, dsts = refs[:n_arr], refs[n_arr:2 * n_arr]
        send_sems, recv_sems = refs[2 * n_arr:]
        x, y, c, _ = _my_position()
        copies = []
        for a in range(n_arr):
            r = srcs[a].shape[0] // N_DEV
            for q in range(N_CHIP):
                copies.append(pltpu.make_async_remote_copy(
                    src_ref=srcs[a].at[pl.ds((2 * q + 1 - c) * r, r)], dst_ref=dsts[a].at[q], send_sem=send_sems.at[a, q],
                    recv_sem=recv_sems.at[a, q], device_id=(x, y, 1 - c), device_id_type=MESH))
        for cp in copies:
            cp.start()
        _wait_all(copies)

    return pl.pallas_call(
        body, name=name, in_specs=[_HBM] * n_arr, out_specs=[_HBM] * n_arr,
        out_shape=[_sds((N_CHIP, g.shape[0] // N_DEV) + g.shape[1:], g.dtype) for g in grads],
        scratch_shapes=[pltpu.SemaphoreType.DMA((n_arr, N_CHIP)), pltpu.SemaphoreType.DMA((n_arr, N_CHIP))],
    )(*grads)


def _pair_sum(grad, from_sibling, name):
    r, n = from_sibling.shape[1:]

    def body(g_ref, s_ref, o_ref):
        c = lax.axis_index("c")
        o_ref[...] = (g_ref[c].astype(F32) + s_ref[...].astype(F32)).astype(BF16)

    return pl.pallas_call(
        body, name=name, grid=(N_CHIP,),
        in_specs=[pl.BlockSpec((None, 2, r, n), lambda q: (q, 0, 0, 0)), pl.BlockSpec((None, r, n), lambda q: (q, 0, 0))],
        out_specs=pl.BlockSpec((None, r, n), lambda q: (q, 0, 0)), out_shape=_sds((N_CHIP, r, n), BF16),
    )(grad.reshape(N_CHIP, 2, r, n), from_sibling)


def _all_reduce_small(vec):
    r = vec.shape[0]

    def body(v_ref, o_ref, buf, send_sems, recv_sems):
        x, y, c, me = _my_position()
        buf[me] = v_ref[...]
        copies = []
        for mask in range(1, N_DEV):
            peer, _ = _peer(x, y, c, mask)
            cp = pltpu.make_async_remote_copy(src_ref=v_ref, dst_ref=buf.at[me], send_sem=send_sems.at[mask],
                                              recv_sem=recv_sems.at[mask], device_id=peer, device_id_type=MESH)
            cp.start()
            copies.append(cp)
        for cp in copies:
            cp.wait()
        total = buf[0]
        for s in range(1, N_DEV):
            total = total + buf[s]
        o_ref[...] = total

    vm = pl.BlockSpec(memory_space=pltpu.VMEM)
    return pl.pallas_call(
        body, name="all_reduce_small", in_specs=[vm], out_specs=vm, out_shape=_sds(vec.shape, F32),
        scratch_shapes=[pltpu.VMEM((N_DEV, r, 128), F32), pltpu.SemaphoreType.DMA((N_DEV,)), pltpu.SemaphoreType.DMA((N_DEV,))],
    )(vec)


def _adamw_math(w, g, m, v):
    m = ADAM_B1 * m + (1.0 - ADAM_B1) * g
    v = ADAM_B2 * v + (1.0 - ADAM_B2) * (g * g)
    m_hat = m / (1.0 - ADAM_B1 ** ADAM_STEP)
    v_hat = v / (1.0 - ADAM_B2 ** ADAM_STEP)
    delta = -ADAM_LR * (m_hat / (jnp.sqrt(v_hat) + ADAM_EPS) + ADAM_WD * w)
    return delta, m, v


def _reduce_adamw(slots, w, m, v, *, transpose, name):
    r, n = slots.shape[1:]
    padded = not transpose and w.shape[0] != r

    def body(s_ref, w_ref, m_ref, v_ref, g_ref, d_ref, nm_ref, nv_ref, *scr):
        g = s_ref[0].astype(F32)
        for s in range(1, slots.shape[0]):
            g = g + s_ref[s].astype(F32)
        if transpose:
            scr[0][...] = g.T
            g = scr[0][:, 0:w_ref.shape[1]]
        elif padded:
            scr[0][...] = g
            g = scr[0][0:w_ref.shape[0], :]
        g_ref[...] = g
        d_ref[...], nm_ref[...], nv_ref[...] = _adamw_math(w_ref[...], g, m_ref[...], v_ref[...])

    out = _sds(w.shape, F32)
    return pl.pallas_call(
        body, name=name, out_shape=[out, out, out, out],
        scratch_shapes=[pltpu.VMEM((n, r) if transpose else (r, n), F32)] if transpose or padded else [],
    )(slots, w, m, v)


def _adamw(g, w, m, v, name):
    def body(g_ref, w_ref, m_ref, v_ref, d_ref, nm_ref, nv_ref):
        d_ref[...], nm_ref[...], nv_ref[...] = _adamw_math(w_ref[...], g_ref[...], m_ref[...], v_ref[...])

    out = _sds(w.shape, F32)
    return pl.pallas_call(body, name=name, out_shape=[out, out, out])(g, w, m, v)


_SMALL = ["ffn1_norm", "mix_norm", "ffn2_norm", "mla_q_lat_norm", "mla_kv_lat_norm", "mla_q_nope_gain", "mla_q_rope_gain",
          "mla_k_nope_gain", "mla_k_rope_gain", "fox_q_gain", "fox_k_gain", "fox_b_f"]
_WEIGHTS = ["ffn1_norm", "ffn1_w_gate", "ffn1_w_up", "ffn1_w_down", "mix_norm", "w_in", "mla_q_lat_norm", "mla_w_qb",
            "mla_kv_lat_norm", "mla_w_kvb", "mla_q_nope_gain", "mla_q_rope_gain", "mla_k_nope_gain", "mla_k_rope_gain",
            "fox_q_gain", "fox_k_gain", "fox_b_f", "w_branch_mla", "w_branch_fox", "b_gate", "w_o", "ffn2_norm",
            "ffn2_w_gate", "ffn2_w_up", "ffn2_w_down"]
_IN_Q, _IN_KV, _IN_KR, _IN_FOX, _IN_F, _IN_GATES = (0, 192), (192, 128), (320, 32), (352, 1536), (1888, 8), (1896, 2048)


def _rows(a, seg):
    return a[seg[0]:seg[0] + seg[1]]


def _split_w_in(win_t):
    z = lambda n: jnp.zeros((n, D_MODEL), win_t.dtype)
    lat = jnp.concatenate([_rows(win_t, _IN_Q), z(LAT_KV - Q_LORA), _rows(win_t, _IN_KV), _rows(win_t, _IN_KR),
                           _rows(win_t, _IN_F), z(LAT_W - LAT_F - N_HEADS)], axis=0)
    return _rows(win_t, _IN_GATES), _rows(win_t, _IN_FOX), lat


def _join_w_in(d_gates, d_fox, d_lat):
    return jnp.concatenate([d_lat[LAT_Q:LAT_Q + Q_LORA], d_lat[LAT_KV:LAT_KV + KV_LORA], d_lat[LAT_KR:LAT_KR + ROPE_DIM],
                            d_fox, d_lat[LAT_F:LAT_F + N_HEADS], d_gates], axis=0)


def kernel(x, positions, ffn1_norm, ffn1_w_gate, ffn1_w_up, ffn1_w_down, mix_norm, w_in, mla_q_lat_norm, mla_w_qb, mla_kv_lat_norm, mla_w_kvb, mla_q_nope_gain, mla_q_rope_gain, mla_k_nope_gain, mla_k_rope_gain, fox_q_gain, fox_k_gain, fox_b_f, w_branch_mla, w_branch_fox, b_gate, w_o, ffn2_norm, ffn2_w_gate, ffn2_w_up, ffn2_w_down, loss_target, m_ffn1_norm, m_ffn1_w_gate, m_ffn1_w_up, m_ffn1_w_down, m_mix_norm, m_w_in, m_mla_q_lat_norm, m_mla_w_qb, m_mla_kv_lat_norm, m_mla_w_kvb, m_mla_q_nope_gain, m_mla_q_rope_gain, m_mla_k_nope_gain, m_mla_k_rope_gain, m_fox_q_gain, m_fox_k_gain, m_fox_b_f, m_w_branch_mla, m_w_branch_fox, m_b_gate, m_w_o, m_ffn2_norm, m_ffn2_w_gate, m_ffn2_w_up, m_ffn2_w_down, v_ffn1_norm, v_ffn1_w_gate, v_ffn1_w_up, v_ffn1_w_down, v_mix_norm, v_w_in, v_mla_q_lat_norm, v_mla_w_qb, v_mla_kv_lat_norm, v_mla_w_kvb, v_mla_q_nope_gain, v_mla_q_rope_gain, v_mla_k_nope_gain, v_mla_k_rope_gain, v_fox_q_gain, v_fox_k_gain, v_fox_b_f, v_w_branch_mla, v_w_branch_fox, v_b_gate, v_w_o, v_ffn2_norm, v_ffn2_w_gate, v_ffn2_w_up, v_ffn2_w_down):
    env = dict(locals())
    strip = lambda n, a: a if n in _SMALL else a[0]
    W = {n: strip(n, env[n]) for n in _WEIGHTS}
    M = {n: strip(n, env["m_" + n]) for n in _WEIGHTS}
    V = {n: strip(n, env["v_" + n]) for n in _WEIGHTS}
    xs = x[0]
    t = xs.shape[0]

    col_split = ["ffn1_w_gate", "ffn1_w_up", "ffn2_w_gate", "ffn2_w_up", "mla_w_qb", "mla_w_kvb", "w_branch_mla", "w_branch_fox"]
    row_split = ["ffn1_w_down", "ffn2_w_down", "w_o"]
    pieces = {n: W[n].T.astype(BF16) for n in col_split}
    pieces.update({n: W[n].astype(BF16) for n in row_split})
    pieces["w_in"] = jnp.pad(W["w_in"].T.astype(BF16), ((0, W_IN_PIECE_PAD - W_IN_PIECE), (0, 0)))
    pieces["b_gate"] = W["b_gate"].T
    group_a = ["ffn1_w_gate", "ffn1_w_up", "ffn1_w_down"]
    group_b = ["w_in", "mla_w_qb", "mla_w_kvb", "w_branch_mla", "w_branch_fox", "w_o", "b_gate"]
    group_c = ["ffn2_w_gate", "ffn2_w_up", "ffn2_w_down"]
    gather = lambda group: _ChipExchange(True, [pieces[n] for n in group])
    G = dict(zip(group_a, _gather_over_cores(gather(group_a).standalone("gather_ici_a"), "gather_d2d_a")))

    inv_freq = ROPE_THETA ** (-jnp.arange(ROPE_HALF, dtype=F32) / ROPE_HALF)
    ang = positions[0].astype(F32)[:, None] * inv_freq
    tables, mats = _prep_tables(jnp.cos(ang), jnp.sin(ang)), _group_matrices()
    gains = _head_gains(*[W[n] for n in ["mla_q_nope_gain", "mla_q_rope_gain", "mla_k_nope_gain", "mla_k_rope_gain",
                                         "fox_q_gain", "fox_k_gain"]])

    (x1, a1, b1), got_b = _ffn_fwd(xs, W["ffn1_norm"], G["ffn1_w_gate"], G["ffn1_w_up"], G["ffn1_w_down"],
                                   exchange=gather(group_b))
    G.update(zip(group_b, _gather_over_cores(got_b, "gather_d2d_b")))
    win_t = G["w_in"].reshape(N_DEV, W_IN_PIECE_PAD, D_MODEL)[:, :W_IN_PIECE].reshape(N_DEV * W_IN_PIECE, D_MODEL)
    wgate_t, wfox_t, wlat_t = _split_w_in(win_t)
    bg = G["b_gate"].T
    gates, fox, lat = _proj_fwd(x1, W["mix_norm"], wgate_t, wfox_t, wlat_t)
    prep_args = (fox, lat, tables, mats, *_interleave_weights(G["mla_w_qb"], G["mla_w_kvb"]), W["mla_q_lat_norm"],
                 W["mla_kv_lat_norm"], gains, W["fox_b_f"])
    fq, fk, fv, c, mq, mk, mv, fq_sq, fk_sq_max, mq_sq, mk_sq_max = _prep_fwd(*prep_args)
    c_t = c.T
    b_fox, ok_fox = _logit_bound(fq_sq, fk_sq_max)
    b_mla, ok_mla = _logit_bound(mq_sq, mk_sq_max)
    (y_fox, y_fox32, lse_fox), got_c = _attn_fwd(fq, fk, fv, b_fox, ok_fox, c, c_t, lanes=HEAD_DIM, name="fox_fwd",
                                                 exchange=gather(group_c))
    G.update(zip(group_c, _gather_over_cores(got_c, "gather_d2d_c")))
    (y_mla, y_mla32, lse_mla), _ = _attn_fwd(mq, mk, mv, b_mla, ok_mla, None, None, lanes=MLA_QK_LANES, name="mla_fwd")
    x2 = _mix_fwd(x1, y_mla, y_fox, gates, bg, G["w_branch_mla"], G["w_branch_fox"], G["w_o"])
    (dx3, a2, b2, loss_vec), _ = _ffn_fwd(x2, W["ffn2_norm"], G["ffn2_w_gate"], G["ffn2_w_up"], G["ffn2_w_down"],
                                          target=loss_target[0])

    def chip_sums(group, tag):
        from_sibling = _grads_to_sibling([grads[n] for n in group], "grads_d2d_" + tag)
        return _ChipExchange(False, [_pair_sum(grads[n], s, "pair_sum_" + n) for n, s in zip(group, from_sibling)])

    (dx2, dg_ffn2, da2, db2, h2, n2, dyh2), _ = _ffn_bwd(dx3, x2, W["ffn2_norm"], a2, b2, G["ffn2_w_gate"], G["ffn2_w_up"],
                                                        G["ffn2_w_down"], "ffn2_bwd")
    grads = {"ffn2_w_gate": _tn_matmul(da2, n2, "ffn2_dgate"), "ffn2_w_up": _tn_matmul(db2, n2, "ffn2_dup"),
             "ffn2_w_down": _tn_matmul(h2, dyh2, "ffn2_ddown")}
    dy_mla, dy_fox, delta_mla, delta_fox, dgates, mixed, dum, duf, dx2b, dbg = _mix_bwd(
        dx2, y_mla, y_fox, y_mla32, y_fox32, gates, bg, G["w_branch_mla"], G["w_branch_fox"], G["w_o"])
    grads["w_o"] = _tn_matmul(mixed, dx2b, "d_w_o")
    grads["w_branch_mla"] = _tn_matmul(dum, y_mla, "d_w_branch_mla")
    grads["w_branch_fox"] = _tn_matmul(duf, y_fox, "d_w_branch_fox")
    (dfq, dfk, dfv, dc_t), slots_c = _attn_bwd(fq, fk, fv, dy_fox, delta_fox, lse_fox, c, c_t, lanes=HEAD_DIM,
                                               name="fox_bwd", exchange=chip_sums(group_c, "c"))
    slots = dict(zip(group_c, slots_c))
    (dmq, dmk, dmv), _ = _attn_bwd(mq, mk, mv, dy_mla, delta_mla, lse_mla, None, None, lanes=MLA_QK_LANES, name="mla_bwd")
    dfox, dlat, dwq, dwk, dwv, d_gql, d_gkvl, d_gq, d_gk, d_gkr, d_gfq, d_gfk, d_bf = _prep_bwd(
        *prep_args, dfq, dfk, dfv, dc_t.T, dmq, dmk, dmv)
    grads["mla_w_qb"], grads["mla_w_kvb"] = _deinterleave_grads(dwq, dwk, dwv)
    fold = lambda a, width: a.reshape(N_HEADS, width).sum(axis=0)[None]
    d_gq, d_gk = fold(d_gq, LANES), fold(d_gk, LANES)
    d_prep_small = [d_gql, d_gkvl, d_gq[:, :HEAD_DIM], d_gq[:, HEAD_DIM:HEAD_DIM + ROPE_DIM], d_gk[:, :HEAD_DIM],
                    d_gkr[:, HEAD_DIM:HEAD_DIM + ROPE_DIM], fold(d_gfq, HEAD_DIM), fold(d_gfk, HEAD_DIM), d_bf]
    dx1, dg_mix, nmix = _proj_bwd(dgates, dfox, dlat, wgate_t, wfox_t, wlat_t, x1, W["mix_norm"], dx2)
    dwin_t = _join_w_in(_tn_matmul(dgates, nmix, "d_w_in_gates"), _tn_matmul(dfox, nmix, "d_w_in_fox"),
                        _tn_matmul(dlat, nmix, "d_w_in_lat"))
    grads["w_in"] = jnp.pad(dwin_t.reshape(N_DEV, W_IN_PIECE, D_MODEL), ((0, 0), (0, W_IN_PIECE_PAD - W_IN_PIECE), (0, 0))
                            ).reshape(N_DEV * W_IN_PIECE_PAD, D_MODEL)
    grad_group_b = [n for n in group_b if n != "b_gate"]
    (dx0, dg_ffn1, da1, db1, h1, n1, dyh1), slots_b = _ffn_bwd(dx1, xs, W["ffn1_norm"], a1, b1, G["ffn1_w_gate"], G["ffn1_w_up"],
                                                              G["ffn1_w_down"], "ffn1_bwd", exchange=chip_sums(grad_group_b, "b"))
    slots.update(zip(grad_group_b, slots_b))
    grads["ffn1_w_gate"] = _tn_matmul(da1, n1, "ffn1_dgate")
    grads["ffn1_w_up"], got = _tn_matmul(db1, n1, "ffn1_dup", exchange=chip_sums(["ffn1_w_gate"], "a_gate"))
    slots["ffn1_w_gate"] = got[0]
    grads["ffn1_w_down"], got = _tn_matmul(h1, dyh1, "ffn1_ddown", exchange=chip_sums(["ffn1_w_up"], "a_up"))
    slots["ffn1_w_up"] = got[0]
    slots["ffn1_w_down"] = chip_sums(["ffn1_w_down"], "a_down").standalone("grads_ici_a_down")[0]

    small_parts = [dg_ffn1, dg_mix, dg_ffn2] + list(d_prep_small) + [dbg.reshape(1, 2 * D_MODEL), loss_vec]
    flat = jnp.concatenate([p.reshape(-1) for p in small_parts])
    n_flat = flat.shape[0]
    rows = -(-n_flat // (8 * 128)) * 8
    total = _all_reduce_small(jnp.pad(flat, (0, rows * 128 - n_flat)).reshape(rows, 128)).reshape(-1)
    offs, small_g = 0, {}
    for n in _SMALL:
        small_g[n] = total[offs:offs + W[n].shape[1]].reshape(W[n].shape)
        offs += W[n].shape[1]
    bg_full = total[offs:offs + 2 * D_MODEL].reshape(2, D_MODEL)
    offs += 2 * D_MODEL
    loss = (0.5 / D_MODEL) * jnp.sum(total[offs:offs + D_MODEL])
    _, _, _, me = _my_position()
    small_g["b_gate"] = lax.dynamic_slice_in_dim(bg_full, me * (D_MODEL // N_DEV), D_MODEL // N_DEV, axis=1)

    transposed_in_memory = ["ffn1_w_gate", "ffn1_w_up", "ffn2_w_gate", "ffn2_w_up", "w_in"]
    res = {}
    for n in _WEIGHTS:
        if n in small_g:
            res[n] = (small_g[n],) + tuple(_adamw(small_g[n], W[n], M[n], V[n], "adamw_" + n))
        elif n in transposed_in_memory:
            res[n] = tuple(o.T for o in _reduce_adamw(slots[n], W[n].T, M[n].T, V[n].T, transpose=False, name="adamw_" + n))
        else:
            res[n] = tuple(_reduce_adamw(slots[n], W[n], M[n], V[n], transpose=n in col_split, name="adamw_" + n))
    outs = [loss, dx0[None]]
    for k in range(4):
        outs += [res[n][k] if n in _SMALL else res[n][k][None] for n in _WEIGHTS]
    return tuple(outs)
```

```python
import functools

import jax
import jax.numpy as jnp
from jax import lax
from jax.experimental import pallas as pl
from jax.experimental.pallas import tpu as pltpu

F32 = jnp.float32
BF16 = jnp.bfloat16

D_MODEL = 1024
FFN_HIDDEN = 2816
N_HEADS = 8
HEAD_DIM = 64
ROPE_DIM = 32
ROPE_HALF = 16
Q_LORA = 192
KV_LORA = 128
ROPE_THETA = 10000.0
RMS_EPS = 1e-6
MLA_SCALE = (HEAD_DIM + ROPE_DIM) ** -0.5
FOX_SCALE = HEAD_DIM ** -0.5
MLA_QK_LANES = 128
ADAM_LR, ADAM_B1, ADAM_B2, ADAM_EPS, ADAM_WD, ADAM_STEP = 0.001, 0.9, 0.999, 1e-08, 0.01, 10
N_DEV = 8
N_CHIP = 4
W_IN_PIECE = 493
BF16_ROW_TILE = 16
W_IN_PIECE_PAD = 496
LAT_W = 512
LAT_Q, LAT_KV, LAT_KR, LAT_F = 0, 256, 384, 416
MASK_VALUE = -1e30
FIXED_SHIFT_MAX_BOUND = 30.0

TOK_TILE = 512
DW_TOK_TILE = 1024
PROJ_TILE = 512
PREP_TILE = 256
ATT_TILE = 512
ATT_COL_CHUNK = 256
FFN_HID_TILE = 1408
FFN_HID_SPLIT = ((0, 768), (768, 1408))

NT = (((1,), (1,)), ((), ()))
TN = (((0,), (0,)), ((), ()))
NN = (((1,), (0,)), ((), ()))
MESH = pl.DeviceIdType.MESH


def _dot(a, b, dims=NN):
    return lax.dot_general(a, b, dims, preferred_element_type=F32)


def _sds(shape, dtype):
    return jax.ShapeDtypeStruct(shape, dtype)


def _rms_fwd(x, g):
    r = lax.rsqrt(jnp.mean(x * x, axis=-1, keepdims=True) + RMS_EPS)
    return x * r * g, r


def _rms_bwd(dy, x, g, r):
    xn = x * r
    dyg = dy * g
    dx = r * (dyg - xn * jnp.mean(dyg * xn, axis=-1, keepdims=True))
    return dx, dy * xn


def _colsum(x):
    return jnp.sum(x, axis=0, keepdims=True)


def _full(shape):
    return pl.BlockSpec(shape, lambda *_: (0,) * len(shape))


def _tok(tm, n):
    return pl.BlockSpec((tm, n), lambda i, *_: (i, 0))


def _ffn_fwd(x, gain, wg_t, wu_t, wd, target=None, exchange=None):
    t = x.shape[0]
    tm = min(TOK_TILE, t)
    tf = FFN_HID_TILE
    n_t, n_f = t // tm, FFN_HIDDEN // tf
    with_loss = target is not None

    def body(*refs):
        if with_loss:
            x_ref, g_ref, wg_ref, wu_ref, wd_ref, t_ref, out_ref, a_ref, b_ref, lvec_ref, n_scr, acc = refs
        else:
            x_ref, g_ref, wg_ref, wu_ref, wd_ref, out_ref, a_ref, b_ref, n_scr, acc = refs
        i, j = pl.program_id(0), pl.program_id(1)

        @pl.when(j == 0)
        def _():
            xn, _ = _rms_fwd(x_ref[...], g_ref[...])
            n_scr[...] = xn.astype(BF16)
            acc[...] = jnp.zeros_like(acc)

        n = n_scr[...]
        a = _dot(n, wg_ref[...], NT)
        b = _dot(n, wu_ref[...], NT)
        a_ref[...] = a.astype(BF16)
        b_ref[...] = b.astype(BF16)
        h = (a * jax.nn.sigmoid(a)) * b
        acc[...] += _dot(h.astype(BF16), wd_ref[...])

        @pl.when(j == n_f - 1)
        def _():
            y = x_ref[...] + 0.5 * acc[...]
            if with_loss:
                diff = y - t_ref[...]
                out_ref[...] = diff * (1.0 / D_MODEL)
                sq = _colsum(diff * diff)

                @pl.when(i == 0)
                def _():
                    lvec_ref[...] = sq

                @pl.when(i > 0)
                def _():
                    lvec_ref[...] += sq
            else:
                out_ref[...] = y

    wspec = pl.BlockSpec((tf, D_MODEL), lambda i, j: (j, 0))
    hspec = pl.BlockSpec((tm, tf), lambda i, j: (i, j))
    in_specs = [_tok(tm, D_MODEL), _full((1, D_MODEL)), wspec, wspec, wspec]
    out_specs = [_tok(tm, D_MODEL), hspec, hspec]
    out_shape = [_sds((t, D_MODEL), F32), _sds((t, FFN_HIDDEN), BF16), _sds((t, FFN_HIDDEN), BF16)]
    args = [x, gain, wg_t, wu_t, wd]
    if with_loss:
        in_specs.append(_tok(tm, D_MODEL))
        out_specs.append(_full((1, D_MODEL)))
        out_shape.append(_sds((1, D_MODEL), F32))
        args.append(target)
    return _gridded_call(
        body, name="ffn_fwd_loss" if with_loss else "ffn_fwd", grid=(n_t, n_f), in_specs=in_specs, out_specs=out_specs,
        out_shape=out_shape, scratch_shapes=[pltpu.VMEM((tm, D_MODEL), BF16), pltpu.VMEM((tm, D_MODEL), F32)],
        args=args, exchange=exchange)


def _ffn_bwd(dy, x, gain, a, b, wg_t, wu_t, wd, name, exchange=None):
    t = x.shape[0]
    tm = min(TOK_TILE, t)
    tf = FFN_HID_TILE
    n_t, n_f = t // tm, FFN_HIDDEN // tf

    def body(dy_ref, x_ref, g_ref, a_ref, b_ref, wg_ref, wu_ref, wd_ref,
             dx_ref, dg_ref, da_ref, db_ref, h_ref, n_ref, dyh_ref, acc):
        i, j = pl.program_id(0), pl.program_id(1)

        @pl.when(j == 0)
        def _():
            xn, _ = _rms_fwd(x_ref[...], g_ref[...])
            n_ref[...] = xn.astype(BF16)
            dyh_ref[...] = (0.5 * dy_ref[...]).astype(BF16)
            acc[...] = jnp.zeros_like(acc)

        dyh = dyh_ref[...]
        for lo, hi in FFN_HID_SPLIT:
            dh = _dot(dyh, wd_ref[lo:hi, :], NT)
            av = a_ref[:, lo:hi].astype(F32)
            bv = b_ref[:, lo:hi].astype(F32)
            s = jax.nn.sigmoid(av)
            silu = av * s
            da = (dh * bv * (s * (1.0 + av * (1.0 - s)))).astype(BF16)
            db = (dh * silu).astype(BF16)
            da_ref[:, lo:hi] = da
            db_ref[:, lo:hi] = db
            h_ref[:, lo:hi] = (silu * bv).astype(BF16)
            acc[...] += _dot(da, wg_ref[lo:hi, :]) + _dot(db, wu_ref[lo:hi, :])

        @pl.when(j == n_f - 1)
        def _():
            xv, g = x_ref[...], g_ref[...]
            r = lax.rsqrt(jnp.mean(xv * xv, axis=-1, keepdims=True) + RMS_EPS)
            dx, dg_rows = _rms_bwd(acc[...], xv, g, r)
            dx_ref[...] = dy_ref[...] + dx
            dg = _colsum(dg_rows)

            @pl.when(i == 0)
            def _():
                dg_ref[...] = dg

            @pl.when(i > 0)
            def _():
                dg_ref[...] += dg

    wspec = pl.BlockSpec((tf, D_MODEL), lambda i, j: (j, 0))
    hspec = pl.BlockSpec((tm, tf), lambda i, j: (i, j))
    tok = _tok(tm, D_MODEL)
    hid = _sds((t, FFN_HIDDEN), BF16)
    return _gridded_call(
        body, name=name, grid=(n_t, n_f),
        in_specs=[tok, tok, _full((1, D_MODEL)), hspec, hspec, wspec, wspec, wspec],
        out_specs=[tok, _full((1, D_MODEL)), hspec, hspec, hspec, tok, tok],
        out_shape=[_sds((t, D_MODEL), F32), _sds((1, D_MODEL), F32), hid, hid, hid,
                   _sds((t, D_MODEL), BF16), _sds((t, D_MODEL), BF16)],
        scratch_shapes=[pltpu.VMEM((tm, D_MODEL), F32)], args=[dy, x, gain, a, b, wg_t, wu_t, wd], exchange=exchange)


def _tn_matmul(a, b, name, exchange=None):
    t, m = a.shape
    n = b.shape[1]
    tk = min(DW_TOK_TILE, t)
    n_k = t // tk

    def body(a_ref, b_ref, o_ref, acc):
        k = pl.program_id(0)
        p = _dot(a_ref[...], b_ref[...], TN)

        @pl.when(k == 0)
        def _():
            acc[...] = p

        @pl.when(k > 0)
        def _():
            acc[...] += p

        @pl.when(k == n_k - 1)
        def _():
            o_ref[...] = acc[...].astype(BF16)

    (out,), got = _gridded_call(
        body, name=name, grid=(n_k,), in_specs=[_tok(tk, m), _tok(tk, n)], out_specs=[_full((m, n))],
        out_shape=[_sds((m, n), BF16)], scratch_shapes=[pltpu.VMEM((m, n), F32)], args=[a, b], exchange=exchange)
    return out if exchange is None else (out, got)


def _proj_fwd(x, gain, wgate_t, wfox_t, wlat_t):
    t = x.shape[0]
    tm = min(PROJ_TILE, t)

    def body(x_ref, g_ref, wg_ref, wf_ref, wl_ref, og_ref, of_ref, ol_ref):
        xn, _ = _rms_fwd(x_ref[...], g_ref[...])
        n = xn.astype(BF16)
        og_ref[...] = _dot(n, wg_ref[...], NT)
        of_ref[...] = _dot(n, wf_ref[...], NT)
        ol_ref[...] = _dot(n, wl_ref[...], NT)

    return pl.pallas_call(
        body, name="proj_fwd", grid=(t // tm,),
        in_specs=[_tok(tm, D_MODEL), _full((1, D_MODEL)), _full(wgate_t.shape), _full(wfox_t.shape), _full(wlat_t.shape)],
        out_specs=[_tok(tm, 2 * D_MODEL), _tok(tm, 3 * 512), _tok(tm, LAT_W)],
        out_shape=[_sds((t, 2 * D_MODEL), F32), _sds((t, 3 * 512), F32), _sds((t, LAT_W), F32)],
        compiler_params=pltpu.CompilerParams(dimension_semantics=("arbitrary",)),
    )(x, gain, wgate_t, wfox_t, wlat_t)


def _proj_bwd(dgates, dfox, dlat, wgate_t, wfox_t, wlat_t, x, gain, dres):
    t = x.shape[0]
    tm = min(PROJ_TILE, t)

    def body(dg_ref, df_ref, dl_ref, wg_ref, wf_ref, wl_ref, x_ref, g_ref, dres_ref, dx_ref, dgain_ref, n_ref):
        i = pl.program_id(0)
        dn = _dot(dg_ref[...], wg_ref[...]) + _dot(df_ref[...], wf_ref[...]) + _dot(dl_ref[...], wl_ref[...])
        xv, g = x_ref[...], g_ref[...]
        xn, r = _rms_fwd(xv, g)
        n_ref[...] = xn.astype(BF16)
        dx, dg_rows = _rms_bwd(dn, xv, g, r)
        dx_ref[...] = dres_ref[...] + dx
        dgn = _colsum(dg_rows)

        @pl.when(i == 0)
        def _():
            dgain_ref[...] = dgn

        @pl.when(i > 0)
        def _():
            dgain_ref[...] += dgn

    tok = _tok(tm, D_MODEL)
    return pl.pallas_call(
        body, name="proj_bwd", grid=(t // tm,),
        in_specs=[_tok(tm, 2 * D_MODEL), _tok(tm, 3 * 512), _tok(tm, LAT_W), _full(wgate_t.shape), _full(wfox_t.shape),
                  _full(wlat_t.shape), tok, _full((1, D_MODEL)), tok],
        out_specs=[tok, _full((1, D_MODEL)), tok],
        out_shape=[_sds((t, D_MODEL), F32), _sds((1, D_MODEL), F32), _sds((t, D_MODEL), BF16)],
        compiler_params=pltpu.CompilerParams(dimension_semantics=("arbitrary",)),
    )(dgates, dfox, dlat, wgate_t, wfox_t, wlat_t, x, gain, dres)


def _tri(n, lower):
    r = lax.broadcasted_iota(jnp.int32, (n, n), 0)
    c = lax.broadcasted_iota(jnp.int32, (n, n), 1)
    return ((c <= r) if lower else (c >= r)).astype(F32)


def _log_sigmoid(z):
    return jnp.minimum(z, 0.0) - jnp.log1p(jnp.exp(-jnp.abs(z)))


N_GROUPS = 16
LANES = MLA_QK_LANES


def _split_dot(a, g):
    hi = a.astype(BF16)
    lo = (a - hi.astype(F32)).astype(BF16)
    return _dot(hi, g) + _dot(lo, g)


def _rope_fwd(y, c, s1, s2):
    return y * c + pltpu.roll(y, LANES - ROPE_HALF, 1) * s1 + pltpu.roll(y, ROPE_HALF, 1) * s2


def _rope_bwd(do, c, s1, s2):
    return do * c + pltpu.roll(do * s1, ROPE_HALF, 1) + pltpu.roll(do * s2, LANES - ROPE_HALF, 1)


def _prep_tables(cos, sin):
    t = cos.shape[0]
    z = lambda n: jnp.zeros((t, n), F32)
    c = jnp.concatenate([jnp.ones((t, HEAD_DIM), F32), cos, cos, z(LANES - HEAD_DIM - ROPE_DIM)], axis=1)
    s1 = jnp.concatenate([z(HEAD_DIM), -sin, z(LANES - HEAD_DIM - ROPE_HALF)], axis=1)
    s2 = jnp.concatenate([z(HEAD_DIM + ROPE_HALF), sin, z(LANES - HEAD_DIM - ROPE_DIM)], axis=1)
    lane = jnp.arange(LANES)
    rope_mask = ((lane >= HEAD_DIM) & (lane < HEAD_DIM + ROPE_DIM)).astype(F32)[None, :]
    return c, s1, s2, rope_mask


def _group_matrices():
    lane = jnp.arange(N_HEADS * LANES)
    head, d = (lane // LANES)[:, None], (lane % LANES)[:, None]
    col = jnp.arange(N_GROUPS)[None, :]
    g_mla = ((col == head) & (d < HEAD_DIM)) | ((col == N_HEADS + head) & (d >= HEAD_DIM) & (d < HEAD_DIM + ROPE_DIM))
    g_fox = col == (jnp.arange(N_HEADS * HEAD_DIM) // HEAD_DIM)[:, None]
    inv_mla = jnp.concatenate([jnp.full((1, N_HEADS), 1.0 / HEAD_DIM, F32), jnp.full((1, N_HEADS), 1.0 / ROPE_DIM, F32)], axis=1)
    inv_fox = jnp.full((1, N_GROUPS), 1.0 / HEAD_DIM, F32)
    return g_mla.astype(BF16), g_mla.T.astype(BF16), inv_mla, g_fox.astype(BF16), g_fox.T.astype(BF16), inv_fox


def _interleave_weights(wqb_t, wkvb_t):
    wq = jnp.pad(wqb_t.reshape(N_HEADS, HEAD_DIM + ROPE_DIM, Q_LORA), ((0, 0), (0, LANES - HEAD_DIM - ROPE_DIM), (0, 0)))
    kv = wkvb_t.reshape(N_HEADS, 2, HEAD_DIM, KV_LORA)
    wk = jnp.pad(kv[:, 0], ((0, 0), (0, LANES - HEAD_DIM), (0, 0)))
    return wq.reshape(N_HEADS * LANES, Q_LORA), wk.reshape(N_HEADS * LANES, KV_LORA), kv[:, 1].reshape(N_HEADS * HEAD_DIM, KV_LORA)


def _deinterleave_grads(dwq, dwk, dwv):
    dq = dwq.reshape(N_HEADS, LANES, Q_LORA)[:, :HEAD_DIM + ROPE_DIM].reshape(N_HEADS * (HEAD_DIM + ROPE_DIM), Q_LORA)
    dk = dwk.reshape(N_HEADS, LANES, KV_LORA)[:, :HEAD_DIM]
    dkv = jnp.stack([dk, dwv.reshape(N_HEADS, HEAD_DIM, KV_LORA)], axis=1)
    return dq, dkv.reshape(N_HEADS * 2 * HEAD_DIM, KV_LORA)


def _head_gains(g_qn, g_qr, g_kn, g_kr, g_fq, g_fk):
    z = lambda n: jnp.zeros((1, n), F32)
    gq = jnp.concatenate([g_qn, g_qr, z(LANES - HEAD_DIM - ROPE_DIM)], axis=1)
    gk = jnp.concatenate([g_kn, z(LANES - HEAD_DIM)], axis=1)
    gkr = jnp.concatenate([z(HEAD_DIM), g_kr, z(LANES - HEAD_DIM - ROPE_DIM)], axis=1)
    return jnp.tile(gq, (1, N_HEADS)), jnp.tile(gk, (1, N_HEADS)), gkr, jnp.tile(g_fq, (1, N_HEADS)), jnp.tile(g_fk, (1, N_HEADS))


def _group_rms(x, g, g_t, inv):
    r = lax.rsqrt(_split_dot(x * x, g) * inv + RMS_EPS)
    return _split_dot(r, g_t)


def _prep_fwd(fox, lat, tables, mats, wq, wk, wv, g_qlat, g_kvlat, gains, b_f):
    t = fox.shape[0]
    tm = min(PREP_TILE, t)
    c_tab, s1_tab, s2_tab, rope_mask = tables

    def body(fox_ref, lat_ref, c_ref, s1_ref, s2_ref, rm_ref, gm_ref, gmt_ref, im_ref, gf_ref, gft_ref, if_ref,
             wq_ref, wk_ref, wv_ref, gql_ref, gkvl_ref, gq_ref, gk_ref, gkr_ref, gfq_ref, gfk_ref, bf_ref,
             fq_ref, fk_ref, fv_ref, cc_ref, mq_ref, mk_ref, mv_ref, fqn_ref, fkmax_ref, mqn_ref, mkmax_ref, carry):
        i = pl.program_id(0)
        ct, s1, s2 = c_ref[...], s1_ref[...], s2_ref[...]

        @pl.when(i == 0)
        def _():
            carry[...] = jnp.zeros_like(carry)
            fkmax_ref[...] = jnp.zeros_like(fkmax_ref)
            mkmax_ref[...] = jnp.zeros_like(mkmax_ref)

        def square_sums(ref, g):
            v = ref[...].astype(F32)
            return _split_dot(v * v, g)

        xq = fox_ref[:, 0:512]
        fq_ref[...] = (xq * _group_rms(xq, gf_ref[...], gft_ref[...], if_ref[...]) * (gfq_ref[...] * FOX_SCALE)).astype(BF16)
        xk = fox_ref[:, 512:1024]
        fk_ref[...] = (xk * _group_rms(xk, gf_ref[...], gft_ref[...], if_ref[...]) * gfk_ref[...]).astype(BF16)
        fv_ref[...] = fox_ref[:, 1024:1536].astype(BF16)
        fqn_ref[...] = square_sums(fq_ref, gf_ref[...])
        fkmax_ref[...] = jnp.maximum(fkmax_ref[...], jnp.max(square_sums(fk_ref, gf_ref[...]), axis=0, keepdims=True))


        logf = _log_sigmoid(lat_ref[:, LAT_F:LAT_F + N_HEADS] + bf_ref[...])
        cc_ref[...] = jnp.dot(_tri(tm, True), logf, precision=lax.Precision.HIGHEST, preferred_element_type=F32) + carry[...]
        carry[...] += _colsum(logf)

        qlat_n, _ = _rms_fwd(lat_ref[:, LAT_Q:LAT_Q + Q_LORA], gql_ref[...])
        p = _dot(qlat_n.astype(BF16), wq_ref[...], NT)
        y = p * _group_rms(p, gm_ref[...], gmt_ref[...], im_ref[...]) * (gq_ref[...] * MLA_SCALE)
        for h in range(N_HEADS):
            w = slice(h * LANES, (h + 1) * LANES)
            mq_ref[:, w] = _rope_fwd(y[:, w], ct, s1, s2).astype(BF16)

        kv_n, _ = _rms_fwd(lat_ref[:, LAT_KV:LAT_KV + KV_LORA], gkvl_ref[...])
        kv_b = kv_n.astype(BF16)
        pk = _dot(kv_b, wk_ref[...], NT)
        kn = pk * _group_rms(pk, gm_ref[...], gmt_ref[...], im_ref[...]) * gk_ref[...]
        rm = rm_ref[...]
        kr = pltpu.roll(lat_ref[:, LAT_KR:LAT_KR + LANES], HEAD_DIM, 1) * rm
        rr = lax.rsqrt(jnp.sum(kr * kr, axis=1, keepdims=True) * (1.0 / ROPE_DIM) + RMS_EPS)
        okr = _rope_fwd(kr * rr * gkr_ref[...], ct * rm, s1, s2)
        for h in range(N_HEADS):
            w = slice(h * LANES, (h + 1) * LANES)
            mk_ref[:, w] = (kn[:, w] + okr).astype(BF16)
        mv_ref[...] = _dot(kv_b, wv_ref[...], NT).astype(BF16)
        mqn_ref[...] = square_sums(mq_ref, gm_ref[...])
        mkmax_ref[...] = jnp.maximum(mkmax_ref[...], jnp.max(square_sums(mk_ref, gm_ref[...]), axis=0, keepdims=True))

    consts = [rope_mask, *mats, wq, wk, wv, g_qlat, g_kvlat, *gains, b_f]
    t512, tl, tg, g1 = _tok(tm, 512), _tok(tm, LANES), _tok(tm, N_GROUPS), _full((1, N_GROUPS))
    return pl.pallas_call(
        body, name="prep_fwd", grid=(t // tm,),
        in_specs=[_tok(tm, 1536), _tok(tm, LAT_W), tl, tl, tl] + [_full(a.shape) for a in consts],
        out_specs=[t512, t512, t512, _tok(tm, N_HEADS), _tok(tm, N_HEADS * LANES), _tok(tm, N_HEADS * LANES), t512,
                   tg, g1, tg, g1],
        out_shape=[_sds((t, 512), BF16), _sds((t, 512), BF16), _sds((t, 512), BF16), _sds((t, N_HEADS), F32),
                   _sds((t, N_HEADS * LANES), BF16), _sds((t, N_HEADS * LANES), BF16), _sds((t, 512), BF16),
                   _sds((t, N_GROUPS), F32), _sds((1, N_GROUPS), F32), _sds((t, N_GROUPS), F32), _sds((1, N_GROUPS), F32)],
        scratch_shapes=[pltpu.VMEM((1, N_HEADS), F32)],
        compiler_params=pltpu.CompilerParams(dimension_semantics=("arbitrary",)),
    )(fox, lat, c_tab, s1_tab, s2_tab, *consts)


def _prep_bwd(fox, lat, tables, mats, wq, wk, wv, g_qlat, g_kvlat, gains, b_f, dfq, dfk, dfv, dc, dmq, dmk, dmv):
    t = fox.shape[0]
    tm = min(PREP_TILE, t)
    n_t = t // tm
    c_tab, s1_tab, s2_tab, rope_mask = tables

    def body(fox_ref, lat_ref, c_ref, s1_ref, s2_ref, rm_ref, gm_ref, gmt_ref, im_ref, gf_ref, gft_ref, if_ref,
             wq_ref, wk_ref, wv_ref, gql_ref, gkvl_ref, gq_ref, gk_ref, gkr_ref, gfq_ref, gfk_ref, bf_ref,
             dfq_ref, dfk_ref, dfv_ref, dc_ref, dmq_ref, dmk_ref, dmv_ref,
             dfox_ref, dlat_ref, dwq_ref, dwk_ref, dwv_ref, o_gql, o_gkvl, o_gq, o_gk, o_gkr, o_gfq, o_gfk, o_bf,
             carry, lscr, wq_acc, wk_acc, wv_acc, dyscr):
        i = pl.program_id(0)
        ct, s1, s2, rm = c_ref[...], s1_ref[...], s2_ref[...], rm_ref[...]

        @pl.when(i == 0)
        def _():
            for ref in [carry, wq_acc, wk_acc, wv_acc, o_gql, o_gkvl, o_gq, o_gk, o_gkr, o_gfq, o_gfk, o_bf]:
                ref[...] = jnp.zeros_like(ref)

        def group_rms_bwd(dy, x, r, gain, g, g_t, inv):
            xn = x * r
            dyg = dy * gain
            mu = _split_dot(_split_dot(dyg * xn, g) * inv, g_t)
            return r * (dyg - xn * mu), dy * xn

        gf, gft, invf = gf_ref[...], gft_ref[...], if_ref[...]
        xq = fox_ref[:, 0:512]
        dxq, rows = group_rms_bwd(dfq_ref[...] * FOX_SCALE, xq, _group_rms(xq, gf, gft, invf), gfq_ref[...], gf, gft, invf)
        dfox_ref[:, 0:512] = dxq.astype(BF16)
        o_gfq[...] += _colsum(rows)
        xk = fox_ref[:, 512:1024]
        dxk, rows = group_rms_bwd(dfk_ref[...], xk, _group_rms(xk, gf, gft, invf), gfk_ref[...], gf, gft, invf)
        dfox_ref[:, 512:1024] = dxk.astype(BF16)
        o_gfk[...] += _colsum(rows)
        dfox_ref[:, 1024:1536] = dfv_ref[...].astype(BF16)

        lscr[...] = jnp.zeros_like(lscr)

        xql = lat_ref[:, LAT_Q:LAT_Q + Q_LORA]
        qlat_n, r_ql = _rms_fwd(xql, gql_ref[...])
        qlat_b = qlat_n.astype(BF16)
        gm, gmt, invm = gm_ref[...], gmt_ref[...], im_ref[...]
        p = _dot(qlat_b, wq_ref[...], NT)
        for h in range(N_HEADS):
            w = slice(h * LANES, (h + 1) * LANES)
            dyscr[:, w] = _rope_bwd(dmq_ref[:, w], ct, s1, s2) * MLA_SCALE
        dp, rows = group_rms_bwd(dyscr[...], p, _group_rms(p, gm, gmt, invm), gq_ref[...], gm, gmt, invm)
        o_gq[...] += _colsum(rows)
        dp_b = dp.astype(BF16)
        wq_acc[...] += _dot(dp_b, qlat_b, TN)
        dxql, rows = _rms_bwd(_dot(dp_b, wq_ref[...]), xql, gql_ref[...], r_ql)
        lscr[:, LAT_Q:LAT_Q + Q_LORA] = dxql
        o_gql[...] += _colsum(rows)

        xkv = lat_ref[:, LAT_KV:LAT_KV + KV_LORA]
        kv_n, r_kv = _rms_fwd(xkv, gkvl_ref[...])
        kv_b = kv_n.astype(BF16)
        pk = _dot(kv_b, wk_ref[...], NT)
        dpk, rows = group_rms_bwd(dmk_ref[...], pk, _group_rms(pk, gm, gmt, invm), gk_ref[...], gm, gmt, invm)
        o_gk[...] += _colsum(rows)
        dpk_b = dpk.astype(BF16)
        dv_b = dmv_ref[...].astype(BF16)
        wk_acc[...] += _dot(dpk_b, kv_b, TN)
        wv_acc[...] += _dot(dv_b, kv_b, TN)
        dxkv, rows = _rms_bwd(_dot(dpk_b, wk_ref[...]) + _dot(dv_b, wv_ref[...]), xkv, gkvl_ref[...], r_kv)
        lscr[:, LAT_KV:LAT_KV + KV_LORA] = dxkv
        o_gkvl[...] += _colsum(rows)

        dokr = dmk_ref[:, 0:LANES]
        for h in range(1, N_HEADS):
            dokr = dokr + dmk_ref[:, h * LANES:(h + 1) * LANES]
        dykr = _rope_bwd(dokr * rm, ct * rm, s1, s2)
        kr = pltpu.roll(lat_ref[:, LAT_KR:LAT_KR + LANES], HEAD_DIM, 1) * rm
        rr = lax.rsqrt(jnp.sum(kr * kr, axis=1, keepdims=True) * (1.0 / ROPE_DIM) + RMS_EPS)
        krn = kr * rr
        dyg = dykr * gkr_ref[...]
        dkr = rr * (dyg - krn * (jnp.sum(dyg * krn, axis=1, keepdims=True) * (1.0 / ROPE_DIM)))
        o_gkr[...] += _colsum(dykr * krn)
        lscr[:, LAT_KR:LAT_KR + LANES] = pltpu.roll(dkr, LANES - HEAD_DIM, 1)

        dcv = dc_ref[...]
        dlogf = jnp.dot(_tri(tm, False), dcv, precision=lax.Precision.HIGHEST, preferred_element_type=F32) + carry[...]
        carry[...] += _colsum(dcv)
        dz = dlogf * jax.nn.sigmoid(-(lat_ref[:, LAT_F:LAT_F + N_HEADS] + bf_ref[...]))
        lscr[:, LAT_F:LAT_F + N_HEADS] = dz
        o_bf[...] += _colsum(dz)

        dlat_ref[...] = lscr[...].astype(BF16)

        @pl.when(i == n_t - 1)
        def _():
            dwq_ref[...] = wq_acc[...].astype(BF16)
            dwk_ref[...] = wk_acc[...].astype(BF16)
            dwv_ref[...] = wv_acc[...].astype(BF16)

    consts = [rope_mask, *mats, wq, wk, wv, g_qlat, g_kvlat, *gains, b_f]

    def rtok(n):
        return pl.BlockSpec((tm, n), lambda i: (n_t - 1 - i, 0))

    sums = [(1, Q_LORA), (1, KV_LORA), (1, N_HEADS * LANES), (1, N_HEADS * LANES), (1, LANES), (1, 512), (1, 512), (1, N_HEADS)]
    return pl.pallas_call(
        body, name="prep_bwd", grid=(n_t,),
        in_specs=[rtok(1536), rtok(LAT_W), rtok(LANES), rtok(LANES), rtok(LANES)] + [_full(a.shape) for a in consts]
        + [rtok(512), rtok(512), rtok(512), rtok(N_HEADS), rtok(N_HEADS * LANES), rtok(N_HEADS * LANES), rtok(512)],
        out_specs=[rtok(1536), rtok(LAT_W), _full(wq.shape), _full(wk.shape), _full(wv.shape)] + [_full(s) for s in sums],
        out_shape=[_sds((t, 1536), BF16), _sds((t, LAT_W), BF16), _sds(wq.shape, BF16), _sds(wk.shape, BF16), _sds(wv.shape, BF16)]
        + [_sds(s, F32) for s in sums],
        scratch_shapes=[pltpu.VMEM((1, N_HEADS), F32), pltpu.VMEM((tm, LAT_W), F32), pltpu.VMEM(wq.shape, F32),
                        pltpu.VMEM(wk.shape, F32), pltpu.VMEM(wv.shape, F32), pltpu.VMEM((tm, N_HEADS * LANES), F32)],
        compiler_params=pltpu.CompilerParams(dimension_semantics=("arbitrary",)),
    )(fox, lat, c_tab, s1_tab, s2_tab, *consts, dfq, dfk, dfv, dc, dmq, dmk, dmv)


def _causal_pairs(n, query_major):
    pairs = [(i, j) for i in range(n) for j in range(i + 1)] if query_major else [(i, j) for j in range(n) for i in range(j, n)]
    return jnp.asarray([p[0] for p in pairs], jnp.int32), jnp.asarray([p[1] for p in pairs], jnp.int32)


def _logit_bound(q_sq, k_sq_max):
    qn = q_sq[:, :N_HEADS] + q_sq[:, N_HEADS:]
    kmax = k_sq_max[:, :N_HEADS] + k_sq_max[:, N_HEADS:]
    bound = jnp.sqrt(qn * kmax) * (1.0 + 2.0 ** -10) + 2.0 ** -10
    flag = (jnp.max(bound) <= FIXED_SHIFT_MAX_BOUND).astype(F32).reshape(1, 1)
    return bound, flag


def _attn_fwd(q, k, v, bound, fixed_ok, c, c_t, *, lanes, name, exchange=None):
    t = q.shape[0]
    tq = min(ATT_TILE, t)
    n_q = t // tq
    hd = HEAD_DIM
    ch = min(ATT_COL_CHUNK, tq)
    decay = c is not None

    def body(qi_ref, kj_ref, *refs):
        if decay:
            q_ref, k_ref, v_ref, b_ref, ok_ref, c_ref, ct_ref, o_ref, o32_ref, lse_ref, m_scr, l_scr, acc = refs
        else:
            q_ref, k_ref, v_ref, b_ref, ok_ref, o_ref, o32_ref, lse_ref, m_scr, l_scr, acc = refs
        i, j = qi_ref[pl.program_id(0)], kj_ref[pl.program_id(0)]
        fixed = ok_ref[0, 0] > 0.5

        @pl.when(j == 0)
        def _():
            m_scr[...] = jnp.full_like(m_scr, MASK_VALUE)
            l_scr[...] = jnp.zeros_like(l_scr)
            acc[...] = jnp.zeros_like(acc)

        def fixed_step(diagonal):
            for h in range(N_HEADS):
                wl = slice(h * lanes, (h + 1) * lanes)
                w = slice(h * hd, (h + 1) * hd)
                for cc in range(tq // ch):
                    cols = slice(cc * ch, (cc + 1) * ch)
                    rs = slice(cc * ch if diagonal else 0, tq)
                    nr = rs.stop - rs.start
                    row = (c_ref[rs, h:h + 1] - b_ref[rs, h:h + 1]) if decay else -b_ref[rs, h:h + 1]
                    s = _dot(q_ref[rs, wl], k_ref[cols, wl], NT)
                    s = s + ((row - ct_ref[h:h + 1, cols]) if decay else row)
                    if diagonal:
                        keep = lax.broadcasted_iota(jnp.int32, (nr, ch), 0) >= lax.broadcasted_iota(jnp.int32, (nr, ch), 1)
                        s = jnp.where(keep, s, MASK_VALUE)
                    p = jnp.exp(s)
                    l_scr[h, rs] += jnp.sum(p, axis=1, keepdims=True)
                    p_b = p.astype(BF16)
                    acc[0, rs, w] += _dot(p_b, v_ref[cols, w])
                    if decay:
                        acc[1, rs, w] += _dot((p - p_b.astype(F32)).astype(BF16), v_ref[cols, w])

        def step(diagonal):
            if diagonal:
                keep = lax.broadcasted_iota(jnp.int32, (tq, tq), 0) >= lax.broadcasted_iota(jnp.int32, (tq, tq), 1)
            for h in range(N_HEADS):
                s = _dot(q_ref[:, h * lanes:(h + 1) * lanes], k_ref[:, h * lanes:(h + 1) * lanes], NT)
                if decay:
                    s = s + (c_ref[:, h:h + 1] - ct_ref[h:h + 1, :])
                if diagonal:
                    s = jnp.where(keep, s, MASK_VALUE)
                m_prev = m_scr[h]
                m_new = jnp.maximum(m_prev, jnp.max(s, axis=1, keepdims=True))
                alpha = jnp.exp(m_prev - m_new)
                p = jnp.exp(s - m_new)
                l_scr[h] = alpha * l_scr[h] + jnp.sum(p, axis=1, keepdims=True)
                w = slice(h * hd, (h + 1) * hd)
                p_b = p.astype(BF16)
                acc[0, :, w] = alpha * acc[0, :, w] + _dot(p_b, v_ref[:, w])
                if decay:
                    p_lo = (p - p_b.astype(F32)).astype(BF16)
                    acc[1, :, w] = alpha * acc[1, :, w] + _dot(p_lo, v_ref[:, w])
                m_scr[h] = m_new

        for diagonal, here in ((False, j < i), (True, j == i)):
            @pl.when(here & fixed)
            def _():
                fixed_step(diagonal)

            @pl.when(here & jnp.logical_not(fixed))
            def _():
                step(diagonal)

        @pl.when(j == i)
        def _():
            for h in range(N_HEADS):
                w = slice(h * hd, (h + 1) * hd)
                l = l_scr[h]
                o_ref[:, w] = (acc[0, :, w] / l).astype(BF16)
                o32_ref[:, w] = ((acc[0, :, w] + acc[1, :, w]) if decay else acc[0, :, w]) / l
                lse_ref[:, h:h + 1] = jnp.where(fixed, b_ref[:, h:h + 1], m_scr[h]) + jnp.log(l)

    qspec = lambda n: pl.BlockSpec((tq, n), lambda s, qi, kj: (qi[s], 0))
    kspec = lambda n: pl.BlockSpec((tq, n), lambda s, qi, kj: (kj[s], 0))
    in_specs = [qspec(N_HEADS * lanes), kspec(N_HEADS * lanes), kspec(512), qspec(N_HEADS),
                pl.BlockSpec(memory_space=pltpu.SMEM)]
    args = [q, k, v, bound, fixed_ok]
    if decay:
        in_specs += [qspec(N_HEADS), pl.BlockSpec((N_HEADS, tq), lambda s, qi, kj: (0, kj[s]))]
        args += [c, c_t]
    pairs = _causal_pairs(n_q, query_major=True)
    return _gridded_call(
        body, name=name, grid=(pairs[0].shape[0],), in_specs=in_specs, out_specs=[qspec(512), qspec(512), qspec(N_HEADS)],
        out_shape=[_sds((t, 512), BF16), _sds((t, 512), F32), _sds((t, N_HEADS), F32)],
        scratch_shapes=[pltpu.VMEM((N_HEADS, tq, 1), F32), pltpu.VMEM((N_HEADS, tq, 1), F32), pltpu.VMEM((2, tq, 512), F32)],
        args=args, exchange=exchange, prefetch=pairs)


def _attn_bwd(q, k, v, do, delta, lse, c, c_t, *, lanes, name, exchange=None):
    t = q.shape[0]
    tq = min(ATT_TILE, t)
    n_q = t // tq
    hd = HEAD_DIM
    decay = c is not None

    pairs = _causal_pairs(n_q, query_major=False)
    n_steps = pairs[0].shape[0]

    def body(qi_ref, kj_ref, *refs):
        if decay:
            q_ref, k_ref, v_ref, do_ref, delta_ref, lse_ref, c_ref, ct_ref, dq_hbm, dk_ref, dv_ref, dct_ref, dq_ref = refs
        else:
            q_ref, k_ref, v_ref, do_ref, delta_ref, lse_ref, dq_hbm, dk_ref, dv_ref, dq_ref = refs
        step_id = pl.program_id(0)
        i, j = qi_ref[step_id], kj_ref[step_id]

        @pl.when(step_id == 0)
        def _():
            dq_ref[...] = jnp.zeros_like(dq_ref)

        @pl.when(i == j)
        def _():
            dk_ref[...] = jnp.zeros_like(dk_ref)
            dv_ref[...] = jnp.zeros_like(dv_ref)
            if decay:
                dct_ref[...] = jnp.zeros_like(dct_ref)

        def piece(rs, cs, masked):
            nr, nc = rs.stop - rs.start, cs.stop - cs.start
            if masked:
                keep = lax.broadcasted_iota(jnp.int32, (nr, nc), 0) >= lax.broadcasted_iota(jnp.int32, (nr, nc), 1)
            rows = pl.ds(pl.multiple_of(i * tq + rs.start, 8), nr)
            for h in range(N_HEADS):
                wl = slice(h * lanes, (h + 1) * lanes)
                w = slice(h * hd, (h + 1) * hd)
                qh, kh = q_ref[rs, wl], k_ref[cs, wl]
                s = _dot(qh, kh, NT)
                if decay:
                    s = s + (c_ref[rs, h:h + 1] - ct_ref[h:h + 1, cs])
                if masked:
                    s = jnp.where(keep, s, MASK_VALUE)
                p = jnp.exp(s - lse_ref[rs, h:h + 1])
                doh = do_ref[rs, w]
                dv_ref[cs, w] += _dot(p.astype(BF16), doh, TN)
                dp = _dot(doh, v_ref[cs, w], NT)
                ds = p * (dp - delta_ref[rs, h:h + 1])
                if decay:
                    dct_ref[h:h + 1, cs] -= _colsum(ds)
                ds_b = ds.astype(BF16)
                dk_ref[cs, wl] += _dot(ds_b, qh, TN)
                dq_ref[rows, wl] += _dot(ds_b, kh)

        whole, lower, upper = slice(0, tq), slice(0, tq // 2), slice(tq // 2, tq)

        @pl.when(i > j)
        def _():
            piece(whole, whole, False)

        @pl.when(i == j)
        def _():
            piece(lower, lower, True)
            piece(upper, lower, False)
            piece(upper, upper, True)

        @pl.when(step_id == n_steps - 1)
        def _():
            pltpu.sync_copy(dq_ref, dq_hbm)

    qspec = lambda n: pl.BlockSpec((tq, n), lambda s, qi, kj: (qi[s], 0))
    kspec = lambda n: pl.BlockSpec((tq, n), lambda s, qi, kj: (kj[s], 0))
    in_specs = [qspec(N_HEADS * lanes), kspec(N_HEADS * lanes), kspec(512), qspec(512), qspec(N_HEADS), qspec(N_HEADS)]
    out_specs = [pl.BlockSpec(memory_space=pl.ANY), kspec(N_HEADS * lanes), kspec(512)]
    out_shape = [_sds((t, N_HEADS * lanes), F32), _sds((t, N_HEADS * lanes), F32), _sds((t, 512), F32)]
    args = [q, k, v, do, delta, lse]
    if decay:
        ctspec = pl.BlockSpec((N_HEADS, tq), lambda s, qi, kj: (0, kj[s]))
        in_specs += [qspec(N_HEADS), ctspec]
        out_specs.append(ctspec)
        out_shape.append(_sds((N_HEADS, t), F32))
        args += [c, c_t]
    return _gridded_call(body, name=name, grid=(n_steps,), in_specs=in_specs, out_specs=out_specs, out_shape=out_shape,
                         scratch_shapes=[pltpu.VMEM((t, N_HEADS * lanes), F32)], args=args, exchange=exchange, prefetch=pairs)


def _mix_fwd(x, y_mla, y_fox, gates, b_gate, wbm_t, wbf_t, wo):
    t = x.shape[0]
    tm = min(PROJ_TILE, t)

    def body(x_ref, ym_ref, yf_ref, gt_ref, bg_ref, wbm_ref, wbf_ref, wo_ref, out_ref):
        um = _dot(ym_ref[...], wbm_ref[...], NT)
        uf = _dot(yf_ref[...], wbf_ref[...], NT)
        sm = jax.nn.sigmoid(gt_ref[:, 0:D_MODEL] + bg_ref[0:1, :])
        sf = jax.nn.sigmoid(gt_ref[:, D_MODEL:2 * D_MODEL] + bg_ref[1:2, :])
        mixed = sm * um + sf * uf
        out_ref[...] = x_ref[...] + _dot(mixed.astype(BF16), wo_ref[...])

    tok = _tok(tm, D_MODEL)
    return pl.pallas_call(
        body, name="mix_fwd", grid=(t // tm,),
        in_specs=[tok, _tok(tm, 512), _tok(tm, 512), _tok(tm, 2 * D_MODEL), _full((2, D_MODEL)), _full(wbm_t.shape),
                  _full(wbf_t.shape), _full(wo.shape)],
        out_specs=tok, out_shape=_sds((t, D_MODEL), F32),
        compiler_params=pltpu.CompilerParams(dimension_semantics=("arbitrary",)),
    )(x, y_mla, y_fox, gates, b_gate, wbm_t, wbf_t, wo)


def _mix_bwd(dx, y_mla, y_fox, y_mla32, y_fox32, gates, b_gate, wbm_t, wbf_t, wo):
    t = dx.shape[0]
    tm = min(PROJ_TILE, t)

    def body(dx_ref, ym_ref, yf_ref, ym32_ref, yf32_ref, gt_ref, bg_ref, wbm_ref, wbf_ref, wo_ref,
             dym_ref, dyf_ref, dlm_ref, dlf_ref, dgt_ref, mixed_ref, dum_ref, duf_ref, dxb_ref, dbg_ref, prod):
        i = pl.program_id(0)
        dxb = dx_ref[...].astype(BF16)
        dxb_ref[...] = dxb
        dmixed = _dot(dxb, wo_ref[...], NT)
        um = _dot(ym_ref[...], wbm_ref[...], NT)
        uf = _dot(yf_ref[...], wbf_ref[...], NT)
        sm = jax.nn.sigmoid(gt_ref[:, 0:D_MODEL] + bg_ref[0:1, :])
        sf = jax.nn.sigmoid(gt_ref[:, D_MODEL:2 * D_MODEL] + bg_ref[1:2, :])
        mixed_ref[...] = (sm * um + sf * uf).astype(BF16)
        dum = (dmixed * sm).astype(BF16)
        duf = (dmixed * sf).astype(BF16)
        dum_ref[...] = dum
        duf_ref[...] = duf
        dgm = dmixed * um * (sm * (1.0 - sm))
        dgf = dmixed * uf * (sf * (1.0 - sf))
        dgt_ref[:, 0:D_MODEL] = dgm.astype(BF16)
        dgt_ref[:, D_MODEL:2 * D_MODEL] = dgf.astype(BF16)
        for du, wb_ref, y32_ref, dy_ref, dl_ref in ((dum, wbm_ref, ym32_ref, dym_ref, dlm_ref),
                                                    (duf, wbf_ref, yf32_ref, dyf_ref, dlf_ref)):
            dy = _dot(du, wb_ref[...])
            dy_ref[...] = dy.astype(BF16)
            prod[...] = dy.astype(BF16).astype(F32) * y32_ref[...]
            for h in range(N_HEADS):
                dl_ref[:, h:h + 1] = jnp.sum(prod[:, h * HEAD_DIM:(h + 1) * HEAD_DIM], axis=1, keepdims=True)

        @pl.when(i == 0)
        def _():
            dbg_ref[...] = jnp.zeros_like(dbg_ref)

        dbg_ref[0:1, :] += _colsum(dgm)
        dbg_ref[1:2, :] += _colsum(dgf)

    tok = _tok(tm, D_MODEL)
    tokb = _sds((t, D_MODEL), BF16)
    t512, t8 = _tok(tm, 512), _tok(tm, N_HEADS)
    return pl.pallas_call(
        body, name="mix_bwd", grid=(t // tm,),
        in_specs=[tok, t512, t512, t512, t512, _tok(tm, 2 * D_MODEL), _full((2, D_MODEL)), _full(wbm_t.shape),
                  _full(wbf_t.shape), _full(wo.shape)],
        out_specs=[t512, t512, t8, t8, _tok(tm, 2 * D_MODEL), tok, tok, tok, tok, _full((2, D_MODEL))],
        out_shape=[_sds((t, 512), BF16), _sds((t, 512), BF16), _sds((t, N_HEADS), F32), _sds((t, N_HEADS), F32),
                   _sds((t, 2 * D_MODEL), BF16), tokb, tokb, tokb, tokb, _sds((2, D_MODEL), F32)],
        scratch_shapes=[pltpu.VMEM((tm, 512), F32)],
        compiler_params=pltpu.CompilerParams(dimension_semantics=("arbitrary",)),
    )(dx, y_mla, y_fox, y_mla32, y_fox32, gates, b_gate, wbm_t, wbf_t, wo)


def _my_position():
    x, y, c = lax.axis_index("x"), lax.axis_index("y"), lax.axis_index("c")
    return x, y, c, 4 * x + 2 * y + c


def _peer(x, y, c, mask):
    px = 1 - x if mask & 4 else x
    py = 1 - y if mask & 2 else y
    pc = 1 - c if mask & 1 else c
    return (px, py, pc), 4 * px + 2 * py + pc


def _chip_peer(x, y, km):
    px = 1 - x if km & 2 else x
    py = 1 - y if km & 1 else y
    return px, py, 2 * px + py


_HBM = pl.BlockSpec(memory_space=pl.ANY)


def _wait_all(copies):
    for cp in copies:
        cp.wait()


class _ChipExchange:
    def __init__(self, gather, arrays):
        self.gather, self.arrays = gather, list(arrays)
        n = len(self.arrays)
        self.out_shape = [_sds((N_DEV * a.shape[0],) + a.shape[1:], a.dtype) if gather else _sds(a.shape, a.dtype)
                          for a in self.arrays]
        self.scratch_shapes = [pltpu.SemaphoreType.DMA((n, N_CHIP)), pltpu.SemaphoreType.DMA((n, N_CHIP)),
                               pltpu.SemaphoreType.DMA((n,))]

    def copies(self, srcs, dsts, send_sems, recv_sems, local_sems):
        x, y, c, me = _my_position()
        q_me = 2 * x + y
        out = []
        for a in range(len(self.arrays)):
            if self.gather:
                r = srcs[a].shape[0]
                local_src, dst = srcs[a], dsts[a].at[pl.ds(me * r, r)]
            else:
                local_src, dst = srcs[a].at[q_me], dsts[a].at[q_me]
            out.append(pltpu.make_async_copy(local_src, dst, local_sems.at[a]))
            for km in range(1, N_CHIP):
                px, py, q_peer = _chip_peer(x, y, km)
                out.append(pltpu.make_async_remote_copy(
                    src_ref=srcs[a] if self.gather else srcs[a].at[q_peer], dst_ref=dst, send_sem=send_sems.at[a, km],
                    recv_sem=recv_sems.at[a, km], device_id=(px, py, c), device_id_type=MESH))
        return out

    def standalone(self, name):
        n = len(self.arrays)

        def body(*refs):
            copies = self.copies(refs[:n], refs[n:2 * n], *refs[2 * n:])
            for cp in copies:
                cp.start()
            _wait_all(copies)

        return pl.pallas_call(body, name=name, in_specs=[_HBM] * n, out_specs=[_HBM] * n, out_shape=self.out_shape,
                              scratch_shapes=self.scratch_shapes)(*self.arrays)


def _gridded_call(body, *, name, grid, in_specs, out_specs, out_shape, scratch_shapes, args, exchange=None, prefetch=()):
    params = pltpu.CompilerParams(dimension_semantics=("arbitrary",) * len(grid))
    n_pre, n_in, n_out, n_scr = len(prefetch), len(in_specs), len(out_specs), len(scratch_shapes)
    n_x = 0 if exchange is None else len(exchange.arrays)

    def carrier(*refs):
        pre, refs = refs[:n_pre], refs[n_pre:]
        ins, x_src, refs = refs[:n_in], refs[n_in:n_in + n_x], refs[n_in + n_x:]
        outs, x_dst, refs = refs[:n_out], refs[n_out:n_out + n_x], refs[n_out + n_x:]
        copies = exchange.copies(x_src, x_dst, *refs[n_scr:])
        pids = [pl.program_id(d) for d in range(len(grid))]
        first = functools.reduce(jnp.logical_and, [p == 0 for p in pids])
        last = functools.reduce(jnp.logical_and, [p == g - 1 for p, g in zip(pids, grid)])

        @pl.when(first)
        def _():
            for cp in copies:
                cp.start()

        body(*pre, *ins, *outs, *refs[:n_scr])

        @pl.when(last)
        def _():
            _wait_all(copies)

    x_shapes, x_scratch, x_args = ([], [], []) if exchange is None else (exchange.out_shape, exchange.scratch_shapes, exchange.arrays)
    spec = pltpu.PrefetchScalarGridSpec(
        num_scalar_prefetch=n_pre, grid=grid, in_specs=list(in_specs) + [_HBM] * n_x, out_specs=list(out_specs) + [_HBM] * n_x,
        scratch_shapes=list(scratch_shapes) + x_scratch)
    res = pl.pallas_call(body if exchange is None else carrier, name=name, grid_spec=spec, out_shape=list(out_shape) + x_shapes,
                         compiler_params=params)(*prefetch, *args, *x_args)
    return res[:n_out], (None if exchange is None else res[n_out:])


def _gather_over_cores(arrays, name):
    n_arr = len(arrays)

    def body(*refs):
        srcs, dsts = refs[:n_arr], refs[n_arr:2 * n_arr]
        send_sems, recv_sems = refs[2 * n_arr:]
        x, y, c, _ = _my_position()
        copies = []
        for a in range(n_arr):
            r = srcs[a].shape[0] // N_DEV
            for q in range(N_CHIP):
                rows = pl.ds((2 * q + c) * r, r)
                copies.append(pltpu.make_async_remote_copy(
                    src_ref=srcs[a].at[rows], dst_ref=dsts[a].at[rows], send_sem=send_sems.at[a, q],
                    recv_sem=recv_sems.at[a, q], device_id=(x, y, 1 - c), device_id_type=MESH))
        for cp in copies:
            cp.start()
        _wait_all(copies)

    return pl.pallas_call(
        body, name=name, in_specs=[_HBM] * n_arr, out_specs=[_HBM] * n_arr,
        out_shape=[_sds(a.shape, a.dtype) for a in arrays], input_output_aliases={a: a for a in range(n_arr)},
        scratch_shapes=[pltpu.SemaphoreType.DMA((n_arr, N_CHIP)), pltpu.SemaphoreType.DMA((n_arr, N_CHIP))],
    )(*arrays)


def _grads_to_sibling(grads, name):
    n_arr = len(grads)

    def body(*refs):
        srcs, dsts = refs[:n_arr], refs[n_arr:2 * n_arr]
        send_sems, recv_sems = refs[2 * n_arr:]
        x, y, c, _ = _my_position()
        copies = []
        for a in range(n_arr):
            r = srcs[a].shape[0] // N_DEV
            for q in range(N_CHIP):
                copies.append(pltpu.make_async_remote_copy(
                    src_ref=srcs[a].at[pl.ds((2 * q + 1 - c) * r, r)], dst_ref=dsts[a].at[q], send_sem=send_sems.at[a, q],
                    recv_sem=recv_sems.at[a, q], device_id=(x, y, 1 - c), device_id_type=MESH))
        for cp in copies:
            cp.start()
        _wait_all(copies)

    return pl.pallas_call(
        body, name=name, in_specs=[_HBM] * n_arr, out_specs=[_HBM] * n_arr,
        out_shape=[_sds((N_CHIP, g.shape[0] // N_DEV) + g.shape[1:], g.dtype) for g in grads],
        scratch_shapes=[pltpu.SemaphoreType.DMA((n_arr, N_CHIP)), pltpu.SemaphoreType.DMA((n_arr, N_CHIP))],
    )(*grads)


def _pair_sum(grad, from_sibling, name):
    r, n = from_sibling.shape[1:]

    def body(g_ref, s_ref, o_ref):
        c = lax.axis_index("c")
        o_ref[...] = (g_ref[c].astype(F32) + s_ref[...].astype(F32)).astype(BF16)

    return pl.pallas_call(
        body, name=name, grid=(N_CHIP,),
        in_specs=[pl.BlockSpec((None, 2, r, n), lambda q: (q, 0, 0, 0)), pl.BlockSpec((None, r, n), lambda q: (q, 0, 0))],
        out_specs=pl.BlockSpec((None, r, n), lambda q: (q, 0, 0)), out_shape=_sds((N_CHIP, r, n), BF16),
    )(grad.reshape(N_CHIP, 2, r, n), from_sibling)


def _all_reduce_small(vec):
    r = vec.shape[0]

    def body(v_ref, o_ref, buf, send_sems, recv_sems):
        x, y, c, me = _my_position()
        buf[me] = v_ref[...]
        copies = []
        for mask in range(1, N_DEV):
            peer, _ = _peer(x, y, c, mask)
            cp = pltpu.make_async_remote_copy(src_ref=v_ref, dst_ref=buf.at[me], send_sem=send_sems.at[mask],
                                              recv_sem=recv_sems.at[mask], device_id=peer, device_id_type=MESH)
            cp.start()
            copies.append(cp)
        for cp in copies:
            cp.wait()
        total = buf[0]
        for s in range(1, N_DEV):
            total = total + buf[s]
        o_ref[...] = total

    vm = pl.BlockSpec(memory_space=pltpu.VMEM)
    return pl.pallas_call(
        body, name="all_reduce_small", in_specs=[vm], out_specs=vm, out_shape=_sds(vec.shape, F32),
        scratch_shapes=[pltpu.VMEM((N_DEV, r, 128), F32), pltpu.SemaphoreType.DMA((N_DEV,)), pltpu.SemaphoreType.DMA((N_DEV,))],
    )(vec)


def _adamw_math(w, g, m, v):
    m = ADAM_B1 * m + (1.0 - ADAM_B1) * g
    v = ADAM_B2 * v + (1.0 - ADAM_B2) * (g * g)
    m_hat = m / (1.0 - ADAM_B1 ** ADAM_STEP)
    v_hat = v / (1.0 - ADAM_B2 ** ADAM_STEP)
    delta = -ADAM_LR * (m_hat / (jnp.sqrt(v_hat) + ADAM_EPS) + ADAM_WD * w)
    return delta, m, v


def _reduce_adamw(slots, w, m, v, *, transpose, name):
    r, n = slots.shape[1:]
    padded = not transpose and w.shape[0] != r

    def body(s_ref, w_ref, m_ref, v_ref, g_ref, d_ref, nm_ref, nv_ref, *scr):
        g = s_ref[0].astype(F32)
        for s in range(1, slots.shape[0]):
            g = g + s_ref[s].astype(F32)
        if transpose:
            scr[0][...] = g.T
            g = scr[0][:, 0:w_ref.shape[1]]
        elif padded:
            scr[0][...] = g
            g = scr[0][0:w_ref.shape[0], :]
        g_ref[...] = g
        d_ref[...], nm_ref[...], nv_ref[...] = _adamw_math(w_ref[...], g, m_ref[...], v_ref[...])

    out = _sds(w.shape, F32)
    if not transpose and not padded and r % (2 * BF16_ROW_TILE) == 0:
        half = pl.BlockSpec((r // 2, n), lambda i: (i, 0))
        return pl.pallas_call(
            body, name=name, grid=(2,), in_specs=[pl.BlockSpec((slots.shape[0], r // 2, n), lambda i: (0, i, 0)), half, half, half],
            out_specs=[half] * 4, out_shape=[out, out, out, out])(slots, w, m, v)
    return pl.pallas_call(
        body, name=name, out_shape=[out, out, out, out],
        scratch_shapes=[pltpu.VMEM((n, r) if transpose else (r, n), F32)] if transpose or padded else [],
    )(slots, w, m, v)


def _adamw(g, w, m, v, name):
    def body(g_ref, w_ref, m_ref, v_ref, d_ref, nm_ref, nv_ref):
        d_ref[...], nm_ref[...], nv_ref[...] = _adamw_math(w_ref[...], g_ref[...], m_ref[...], v_ref[...])

    out = _sds(w.shape, F32)
    return pl.pallas_call(body, name=name, out_shape=[out, out, out])(g, w, m, v)


_SMALL = ["ffn1_norm", "mix_norm", "ffn2_norm", "mla_q_lat_norm", "mla_kv_lat_norm", "mla_q_nope_gain", "mla_q_rope_gain",
          "mla_k_nope_gain", "mla_k_rope_gain", "fox_q_gain", "fox_k_gain", "fox_b_f"]
_WEIGHTS = ["ffn1_norm", "ffn1_w_gate", "ffn1_w_up", "ffn1_w_down", "mix_norm", "w_in", "mla_q_lat_norm", "mla_w_qb",
            "mla_kv_lat_norm", "mla_w_kvb", "mla_q_nope_gain", "mla_q_rope_gain", "mla_k_nope_gain", "mla_k_rope_gain",
            "fox_q_gain", "fox_k_gain", "fox_b_f", "w_branch_mla", "w_branch_fox", "b_gate", "w_o", "ffn2_norm",
            "ffn2_w_gate", "ffn2_w_up", "ffn2_w_down"]
_IN_Q, _IN_KV, _IN_KR, _IN_FOX, _IN_F, _IN_GATES = (0, 192), (192, 128), (320, 32), (352, 1536), (1888, 8), (1896, 2048)


def _rows(a, seg):
    return a[seg[0]:seg[0] + seg[1]]


def _split_w_in(win_t):
    z = lambda n: jnp.zeros((n, D_MODEL), win_t.dtype)
    lat = jnp.concatenate([_rows(win_t, _IN_Q), z(LAT_KV - Q_LORA), _rows(win_t, _IN_KV), _rows(win_t, _IN_KR),
                           _rows(win_t, _IN_F), z(LAT_W - LAT_F - N_HEADS)], axis=0)
    return _rows(win_t, _IN_GATES), _rows(win_t, _IN_FOX), lat


def _join_w_in(d_gates, d_fox, d_lat):
    return jnp.concatenate([d_lat[LAT_Q:LAT_Q + Q_LORA], d_lat[LAT_KV:LAT_KV + KV_LORA], d_lat[LAT_KR:LAT_KR + ROPE_DIM],
                            d_fox, d_lat[LAT_F:LAT_F + N_HEADS], d_gates], axis=0)


def kernel(x, positions, ffn1_norm, ffn1_w_gate, ffn1_w_up, ffn1_w_down, mix_norm, w_in, mla_q_lat_norm, mla_w_qb, mla_kv_lat_norm, mla_w_kvb, mla_q_nope_gain, mla_q_rope_gain, mla_k_nope_gain, mla_k_rope_gain, fox_q_gain, fox_k_gain, fox_b_f, w_branch_mla, w_branch_fox, b_gate, w_o, ffn2_norm, ffn2_w_gate, ffn2_w_up, ffn2_w_down, loss_target, m_ffn1_norm, m_ffn1_w_gate, m_ffn1_w_up, m_ffn1_w_down, m_mix_norm, m_w_in, m_mla_q_lat_norm, m_mla_w_qb, m_mla_kv_lat_norm, m_mla_w_kvb, m_mla_q_nope_gain, m_mla_q_rope_gain, m_mla_k_nope_gain, m_mla_k_rope_gain, m_fox_q_gain, m_fox_k_gain, m_fox_b_f, m_w_branch_mla, m_w_branch_fox, m_b_gate, m_w_o, m_ffn2_norm, m_ffn2_w_gate, m_ffn2_w_up, m_ffn2_w_down, v_ffn1_norm, v_ffn1_w_gate, v_ffn1_w_up, v_ffn1_w_down, v_mix_norm, v_w_in, v_mla_q_lat_norm, v_mla_w_qb, v_mla_kv_lat_norm, v_mla_w_kvb, v_mla_q_nope_gain, v_mla_q_rope_gain, v_mla_k_nope_gain, v_mla_k_rope_gain, v_fox_q_gain, v_fox_k_gain, v_fox_b_f, v_w_branch_mla, v_w_branch_fox, v_b_gate, v_w_o, v_ffn2_norm, v_ffn2_w_gate, v_ffn2_w_up, v_ffn2_w_down):
    env = dict(locals())
    strip = lambda n, a: a if n in _SMALL else a[0]
    W = {n: strip(n, env[n]) for n in _WEIGHTS}
    M = {n: strip(n, env["m_" + n]) for n in _WEIGHTS}
    V = {n: strip(n, env["v_" + n]) for n in _WEIGHTS}
    xs = x[0]
    t = xs.shape[0]

    col_split = ["ffn1_w_gate", "ffn1_w_up", "ffn2_w_gate", "ffn2_w_up", "mla_w_qb", "mla_w_kvb", "w_branch_mla", "w_branch_fox"]
    row_split = ["ffn1_w_down", "ffn2_w_down", "w_o"]
    pieces = {n: W[n].T.astype(BF16) for n in col_split}
    pieces.update({n: W[n].astype(BF16) for n in row_split})
    pieces["w_in"] = jnp.pad(W["w_in"].T.astype(BF16), ((0, W_IN_PIECE_PAD - W_IN_PIECE), (0, 0)))
    pieces["b_gate"] = W["b_gate"].T
    group_a = ["ffn1_w_gate", "ffn1_w_up", "ffn1_w_down"]
    group_b = ["w_in", "mla_w_qb", "mla_w_kvb", "w_branch_mla", "w_branch_fox", "w_o", "b_gate"]
    group_c = ["ffn2_w_gate", "ffn2_w_up", "ffn2_w_down"]
    gather = lambda group: _ChipExchange(True, [pieces[n] for n in group])
    G = dict(zip(group_a, _gather_over_cores(gather(group_a).standalone("gather_ici_a"), "gather_d2d_a")))

    inv_freq = ROPE_THETA ** (-jnp.arange(ROPE_HALF, dtype=F32) / ROPE_HALF)
    ang = positions[0].astype(F32)[:, None] * inv_freq
    tables, mats = _prep_tables(jnp.cos(ang), jnp.sin(ang)), _group_matrices()
    gains = _head_gains(*[W[n] for n in ["mla_q_nope_gain", "mla_q_rope_gain", "mla_k_nope_gain", "mla_k_rope_gain",
                                         "fox_q_gain", "fox_k_gain"]])

    (x1, a1, b1), got_b = _ffn_fwd(xs, W["ffn1_norm"], G["ffn1_w_gate"], G["ffn1_w_up"], G["ffn1_w_down"],
                                   exchange=gather(group_b))
    G.update(zip(group_b, _gather_over_cores(got_b, "gather_d2d_b")))
    win_t = G["w_in"].reshape(N_DEV, W_IN_PIECE_PAD, D_MODEL)[:, :W_IN_PIECE].reshape(N_DEV * W_IN_PIECE, D_MODEL)
    wgate_t, wfox_t, wlat_t = _split_w_in(win_t)
    bg = G["b_gate"].T
    gates, fox, lat = _proj_fwd(x1, W["mix_norm"], wgate_t, wfox_t, wlat_t)
    prep_args = (fox, lat, tables, mats, *_interleave_weights(G["mla_w_qb"], G["mla_w_kvb"]), W["mla_q_lat_norm"],
                 W["mla_kv_lat_norm"], gains, W["fox_b_f"])
    fq, fk, fv, c, mq, mk, mv, fq_sq, fk_sq_max, mq_sq, mk_sq_max = _prep_fwd(*prep_args)
    c_t = c.T
    b_fox, ok_fox = _logit_bound(fq_sq, fk_sq_max)
    b_mla, ok_mla = _logit_bound(mq_sq, mk_sq_max)
    (y_fox, y_fox32, lse_fox), got_c = _attn_fwd(fq, fk, fv, b_fox, ok_fox, c, c_t, lanes=HEAD_DIM, name="fox_fwd",
                                                 exchange=gather(group_c))
    G.update(zip(group_c, _gather_over_cores(got_c, "gather_d2d_c")))
    (y_mla, y_mla32, lse_mla), _ = _attn_fwd(mq, mk, mv, b_mla, ok_mla, None, None, lanes=MLA_QK_LANES, name="mla_fwd")
    x2 = _mix_fwd(x1, y_mla, y_fox, gates, bg, G["w_branch_mla"], G["w_branch_fox"], G["w_o"])
    (dx3, a2, b2, loss_vec), _ = _ffn_fwd(x2, W["ffn2_norm"], G["ffn2_w_gate"], G["ffn2_w_up"], G["ffn2_w_down"],
                                          target=loss_target[0])

    def chip_sums(group, tag):
        from_sibling = _grads_to_sibling([grads[n] for n in group], "grads_d2d_" + tag)
        return _ChipExchange(False, [_pair_sum(grads[n], s, "pair_sum_" + n) for n, s in zip(group, from_sibling)])

    (dx2, dg_ffn2, da2, db2, h2, n2, dyh2), _ = _ffn_bwd(dx3, x2, W["ffn2_norm"], a2, b2, G["ffn2_w_gate"], G["ffn2_w_up"],
                                                        G["ffn2_w_down"], "ffn2_bwd")
    grads = {"ffn2_w_gate": _tn_matmul(da2, n2, "ffn2_dgate"), "ffn2_w_up": _tn_matmul(db2, n2, "ffn2_dup"),
             "ffn2_w_down": _tn_matmul(h2, dyh2, "ffn2_ddown")}
    dy_mla, dy_fox, delta_mla, delta_fox, dgates, mixed, dum, duf, dx2b, dbg = _mix_bwd(
        dx2, y_mla, y_fox, y_mla32, y_fox32, gates, bg, G["w_branch_mla"], G["w_branch_fox"], G["w_o"])
    grads["w_o"] = _tn_matmul(mixed, dx2b, "d_w_o")
    grads["w_branch_mla"] = _tn_matmul(dum, y_mla, "d_w_branch_mla")
    grads["w_branch_fox"] = _tn_matmul(duf, y_fox, "d_w_branch_fox")
    (dfq, dfk, dfv, dc_t), slots_c = _attn_bwd(fq, fk, fv, dy_fox, delta_fox, lse_fox, c, c_t, lanes=HEAD_DIM,
                                               name="fox_bwd", exchange=chip_sums(group_c, "c"))
    slots = dict(zip(group_c, slots_c))
    (dmq, dmk, dmv), _ = _attn_bwd(mq, mk, mv, dy_mla, delta_mla, lse_mla, None, None, lanes=MLA_QK_LANES, name="mla_bwd")
    dfox, dlat, dwq, dwk, dwv, d_gql, d_gkvl, d_gq, d_gk, d_gkr, d_gfq, d_gfk, d_bf = _prep_bwd(
        *prep_args, dfq, dfk, dfv, dc_t.T, dmq, dmk, dmv)
    grads["mla_w_qb"], grads["mla_w_kvb"] = _deinterleave_grads(dwq, dwk, dwv)
    fold = lambda a, width: a.reshape(N_HEADS, width).sum(axis=0)[None]
    d_gq, d_gk = fold(d_gq, LANES), fold(d_gk, LANES)
    d_prep_small = [d_gql, d_gkvl, d_gq[:, :HEAD_DIM], d_gq[:, HEAD_DIM:HEAD_DIM + ROPE_DIM], d_gk[:, :HEAD_DIM],
                    d_gkr[:, HEAD_DIM:HEAD_DIM + ROPE_DIM], fold(d_gfq, HEAD_DIM), fold(d_gfk, HEAD_DIM), d_bf]
    dx1, dg_mix, nmix = _proj_bwd(dgates, dfox, dlat, wgate_t, wfox_t, wlat_t, x1, W["mix_norm"], dx2)
    dwin_t = _join_w_in(_tn_matmul(dgates, nmix, "d_w_in_gates"), _tn_matmul(dfox, nmix, "d_w_in_fox"),
                        _tn_matmul(dlat, nmix, "d_w_in_lat"))
    grads["w_in"] = jnp.pad(dwin_t.reshape(N_DEV, W_IN_PIECE, D_MODEL), ((0, 0), (0, W_IN_PIECE_PAD - W_IN_PIECE), (0, 0))
                            ).reshape(N_DEV * W_IN_PIECE_PAD, D_MODEL)
    grad_group_b = [n for n in group_b if n != "b_gate"]
    (dx0, dg_ffn1, da1, db1, h1, n1, dyh1), slots_b = _ffn_bwd(dx1, xs, W["ffn1_norm"], a1, b1, G["ffn1_w_gate"], G["ffn1_w_up"],
                                                              G["ffn1_w_down"], "ffn1_bwd", exchange=chip_sums(grad_group_b, "b"))
    slots.update(zip(grad_group_b, slots_b))
    grads["ffn1_w_gate"] = _tn_matmul(da1, n1, "ffn1_dgate")
    grads["ffn1_w_up"], got = _tn_matmul(db1, n1, "ffn1_dup", exchange=chip_sums(["ffn1_w_gate"], "a_gate"))
    slots["ffn1_w_gate"] = got[0]
    grads["ffn1_w_down"], got = _tn_matmul(h1, dyh1, "ffn1_ddown", exchange=chip_sums(["ffn1_w_up"], "a_up"))
    slots["ffn1_w_up"] = got[0]
    slots["ffn1_w_down"] = chip_sums(["ffn1_w_down"], "a_down").standalone("grads_ici_a_down")[0]

    small_parts = [dg_ffn1, dg_mix, dg_ffn2] + list(d_prep_small) + [dbg.reshape(1, 2 * D_MODEL), loss_vec]
    flat = jnp.concatenate([p.reshape(-1) for p in small_parts])
    n_flat = flat.shape[0]
    rows = -(-n_flat // (8 * 128)) * 8
    total = _all_reduce_small(jnp.pad(flat, (0, rows * 128 - n_flat)).reshape(rows, 128)).reshape(-1)
    offs, small_g = 0, {}
    for n in _SMALL:
        small_g[n] = total[offs:offs + W[n].shape[1]].reshape(W[n].shape)
        offs += W[n].shape[1]
    bg_full = total[offs:offs + 2 * D_MODEL].reshape(2, D_MODEL)
    offs += 2 * D_MODEL
    loss = (0.5 / D_MODEL) * jnp.sum(total[offs:offs + D_MODEL])
    _, _, _, me = _my_position()
    small_g["b_gate"] = lax.dynamic_slice_in_dim(bg_full, me * (D_MODEL // N_DEV), D_MODEL // N_DEV, axis=1)

    transposed_in_memory = ["ffn1_w_gate", "ffn1_w_up", "ffn2_w_gate", "ffn2_w_up", "w_in"]
    res = {}
    for n in _WEIGHTS:
        if n in small_g:
            res[n] = (small_g[n],) + tuple(_adamw(small_g[n], W[n], M[n], V[n], "adamw_" + n))
        elif n in transposed_in_memory:
            res[n] = tuple(o.T for o in _reduce_adamw(slots[n], W[n].T, M[n].T, V[n].T, transpose=False, name="adamw_" + n))
        else:
            res[n] = tuple(_reduce_adamw(slots[n], W[n], M[n], V[n], transpose=n in col_split, name="adamw_" + n))
    outs = [loss, dx0[None]]
    for k in range(4):
        outs += [res[n][k] if n in _SMALL else res[n][k][None] for n in _WEIGHTS]
    return tuple(outs)
```

```python
import functools

import jax
import jax.numpy as jnp
from jax import lax
from jax.experimental import pallas as pl
from jax.experimental.pallas import tpu as pltpu

F32 = jnp.float32
BF16 = jnp.bfloat16

D_MODEL = 1024
FFN_HIDDEN = 2816
N_HEADS = 8
HEAD_DIM = 64
ROPE_DIM = 32
ROPE_HALF = 16
Q_LORA = 192
KV_LORA = 128
ROPE_THETA = 10000.0
RMS_EPS = 1e-6
MLA_SCALE = (HEAD_DIM + ROPE_DIM) ** -0.5
FOX_SCALE = HEAD_DIM ** -0.5
MLA_QK_LANES = 128
ADAM_LR, ADAM_B1, ADAM_B2, ADAM_EPS, ADAM_WD, ADAM_STEP = 0.001, 0.9, 0.999, 1e-08, 0.01, 10
N_DEV = 8
N_CHIP = 4
W_IN_PIECE = 493
BF16_ROW_TILE = 16
W_IN_PIECE_PAD = 496
LAT_W = 512
LAT_Q, LAT_KV, LAT_KR, LAT_F = 0, 256, 384, 416
MASK_VALUE = -1e30
FIXED_SHIFT_MAX_BOUND = 30.0

TOK_TILE = 512
DW_TOK_TILE = 1024
PROJ_TILE = 512
PREP_TILE = 256
ATT_TILE = 512
ATT_COL_CHUNK = 256
FFN_HID_TILE = 1408
FFN_HID_SPLIT = ((0, 768), (768, 1408))

NT = (((1,), (1,)), ((), ()))
TN = (((0,), (0,)), ((), ()))
NN = (((1,), (0,)), ((), ()))
MESH = pl.DeviceIdType.MESH


def _dot(a, b, dims=NN):
    return lax.dot_general(a, b, dims, preferred_element_type=F32)


def _sds(shape, dtype):
    return jax.ShapeDtypeStruct(shape, dtype)


def _rms_fwd(x, g):
    r = lax.rsqrt(jnp.mean(x * x, axis=-1, keepdims=True) + RMS_EPS)
    return x * r * g, r


def _rms_bwd(dy, x, g, r):
    xn = x * r
    dyg = dy * g
    dx = r * (dyg - xn * jnp.mean(dyg * xn, axis=-1, keepdims=True))
    return dx, dy * xn


def _colsum(x):
    return jnp.sum(x, axis=0, keepdims=True)


def _full(shape):
    return pl.BlockSpec(shape, lambda *_: (0,) * len(shape))


def _tok(tm, n):
    return pl.BlockSpec((tm, n), lambda i, *_: (i, 0))


def _ffn_fwd(x, gain, wg_t, wu_t, wd, target=None, exchange=None):
    t = x.shape[0]
    tm = min(TOK_TILE, t)
    tf = FFN_HID_TILE
    n_t, n_f = t // tm, FFN_HIDDEN // tf
    with_loss = target is not None

    def body(*refs):
        if with_loss:
            x_ref, g_ref, wg_ref, wu_ref, wd_ref, t_ref, out_ref, a_ref, b_ref, lvec_ref, n_scr, acc = refs
        else:
            x_ref, g_ref, wg_ref, wu_ref, wd_ref, out_ref, a_ref, b_ref, n_scr, acc = refs
        i, j = pl.program_id(0), pl.program_id(1)

        @pl.when(j == 0)
        def _():
            xn, _ = _rms_fwd(x_ref[...], g_ref[...])
            n_scr[...] = xn.astype(BF16)
            acc[...] = jnp.zeros_like(acc)

        n = n_scr[...]
        a = _dot(n, wg_ref[...], NT)
        b = _dot(n, wu_ref[...], NT)
        a_ref[...] = a.astype(BF16)
        b_ref[...] = b.astype(BF16)
        h = (a * jax.nn.sigmoid(a)) * b
        acc[...] += _dot(h.astype(BF16), wd_ref[...])

        @pl.when(j == n_f - 1)
        def _():
            y = x_ref[...] + 0.5 * acc[...]
            if with_loss:
                diff = y - t_ref[...]
                out_ref[...] = diff * (1.0 / D_MODEL)
                sq = _colsum(diff * diff)

                @pl.when(i == 0)
                def _():
                    lvec_ref[...] = sq

                @pl.when(i > 0)
                def _():
                    lvec_ref[...] += sq
            else:
                out_ref[...] = y

    wspec = pl.BlockSpec((tf, D_MODEL), lambda i, j: (j, 0))
    hspec = pl.BlockSpec((tm, tf), lambda i, j: (i, j))
    in_specs = [_tok(tm, D_MODEL), _full((1, D_MODEL)), wspec, wspec, wspec]
    out_specs = [_tok(tm, D_MODEL), hspec, hspec]
    out_shape = [_sds((t, D_MODEL), F32), _sds((t, FFN_HIDDEN), BF16), _sds((t, FFN_HIDDEN), BF16)]
    args = [x, gain, wg_t, wu_t, wd]
    if with_loss:
        in_specs.append(_tok(tm, D_MODEL))
        out_specs.append(_full((1, D_MODEL)))
        out_shape.append(_sds((1, D_MODEL), F32))
        args.append(target)
    return _gridded_call(
        body, name="ffn_fwd_loss" if with_loss else "ffn_fwd", grid=(n_t, n_f), in_specs=in_specs, out_specs=out_specs,
        out_shape=out_shape, scratch_shapes=[pltpu.VMEM((tm, D_MODEL), BF16), pltpu.VMEM((tm, D_MODEL), F32)],
        args=args, exchange=exchange)


def _ffn_bwd(dy, x, gain, a, b, wg_t, wu_t, wd, name, exchange=None):
    t = x.shape[0]
    tm = min(TOK_TILE, t)
    tf = FFN_HID_TILE
    n_t, n_f = t // tm, FFN_HIDDEN // tf

    def body(dy_ref, x_ref, g_ref, a_ref, b_ref, wg_ref, wu_ref, wd_ref,
             dx_ref, dg_ref, da_ref, db_ref, h_ref, n_ref, dyh_ref, acc):
        i, j = pl.program_id(0), pl.program_id(1)

        @pl.when(j == 0)
        def _():
            xn, _ = _rms_fwd(x_ref[...], g_ref[...])
            n_ref[...] = xn.astype(BF16)
            dyh_ref[...] = (0.5 * dy_ref[...]).astype(BF16)
            acc[...] = jnp.zeros_like(acc)

        dyh = dyh_ref[...]
        for lo, hi in FFN_HID_SPLIT:
            dh = _dot(dyh, wd_ref[lo:hi, :], NT)
            av = a_ref[:, lo:hi].astype(F32)
            bv = b_ref[:, lo:hi].astype(F32)
            s = jax.nn.sigmoid(av)
            silu = av * s
            da = (dh * bv * (s * (1.0 + av * (1.0 - s)))).astype(BF16)
            db = (dh * silu).astype(BF16)
            da_ref[:, lo:hi] = da
            db_ref[:, lo:hi] = db
            h_ref[:, lo:hi] = (silu * bv).astype(BF16)
            acc[...] += _dot(da, wg_ref[lo:hi, :]) + _dot(db, wu_ref[lo:hi, :])

        @pl.when(j == n_f - 1)
        def _():
            xv, g = x_ref[...], g_ref[...]
            r = lax.rsqrt(jnp.mean(xv * xv, axis=-1, keepdims=True) + RMS_EPS)
            dx, dg_rows = _rms_bwd(acc[...], xv, g, r)
            dx_ref[...] = dy_ref[...] + dx
            dg = _colsum(dg_rows)

            @pl.when(i == 0)
            def _():
                dg_ref[...] = dg

            @pl.when(i > 0)
            def _():
                dg_ref[...] += dg

    wspec = pl.BlockSpec((tf, D_MODEL), lambda i, j: (j, 0))
    hspec = pl.BlockSpec((tm, tf), lambda i, j: (i, j))
    tok = _tok(tm, D_MODEL)
    hid = _sds((t, FFN_HIDDEN), BF16)
    return _gridded_call(
        body, name=name, grid=(n_t, n_f),
        in_specs=[tok, tok, _full((1, D_MODEL)), hspec, hspec, wspec, wspec, wspec],
        out_specs=[tok, _full((1, D_MODEL)), hspec, hspec, hspec, tok, tok],
        out_shape=[_sds((t, D_MODEL), F32), _sds((1, D_MODEL), F32), hid, hid, hid,
                   _sds((t, D_MODEL), BF16), _sds((t, D_MODEL), BF16)],
        scratch_shapes=[pltpu.VMEM((tm, D_MODEL), F32)], args=[dy, x, gain, a, b, wg_t, wu_t, wd], exchange=exchange)


def _tn_matmul(a, b, name, exchange=None):
    t, m = a.shape
    n = b.shape[1]
    tk = min(DW_TOK_TILE, t)
    n_k = t // tk

    def body(a_ref, b_ref, o_ref, acc):
        k = pl.program_id(0)
        p = _dot(a_ref[...], b_ref[...], TN)

        @pl.when(k == 0)
        def _():
            acc[...] = p

        @pl.when(k > 0)
        def _():
            acc[...] += p

        @pl.when(k == n_k - 1)
        def _():
            o_ref[...] = acc[...].astype(BF16)

    (out,), got = _gridded_call(
        body, name=name, grid=(n_k,), in_specs=[_tok(tk, m), _tok(tk, n)], out_specs=[_full((m, n))],
        out_shape=[_sds((m, n), BF16)], scratch_shapes=[pltpu.VMEM((m, n), F32)], args=[a, b], exchange=exchange)
    return out if exchange is None else (out, got)


def _proj_fwd(x, gain, wgate_t, wfox_t, wlat_t):
    t = x.shape[0]
    tm = min(PROJ_TILE, t)

    def body(x_ref, g_ref, wg_ref, wf_ref, wl_ref, og_ref, of_ref, ol_ref):
        xn, _ = _rms_fwd(x_ref[...], g_ref[...])
        n = xn.astype(BF16)
        og_ref[...] = _dot(n, wg_ref[...], NT)
        of_ref[...] = _dot(n, wf_ref[...], NT)
        ol_ref[...] = _dot(n, wl_ref[...], NT)

    return pl.pallas_call(
        body, name="proj_fwd", grid=(t // tm,),
        in_specs=[_tok(tm, D_MODEL), _full((1, D_MODEL)), _full(wgate_t.shape), _full(wfox_t.shape), _full(wlat_t.shape)],
        out_specs=[_tok(tm, 2 * D_MODEL), _tok(tm, 3 * 512), _tok(tm, LAT_W)],
        out_shape=[_sds((t, 2 * D_MODEL), F32), _sds((t, 3 * 512), F32), _sds((t, LAT_W), F32)],
        compiler_params=pltpu.CompilerParams(dimension_semantics=("arbitrary",)),
    )(x, gain, wgate_t, wfox_t, wlat_t)


def _proj_bwd(dgates, dfox, dlat, wgate_t, wfox_t, wlat_t, x, gain, dres):
    t = x.shape[0]
    tm = min(PROJ_TILE, t)

    def body(dg_ref, df_ref, dl_ref, wg_ref, wf_ref, wl_ref, x_ref, g_ref, dres_ref, dx_ref, dgain_ref, n_ref):
        i = pl.program_id(0)
        dn = _dot(dg_ref[...], wg_ref[...]) + _dot(df_ref[...], wf_ref[...]) + _dot(dl_ref[...], wl_ref[...])
        xv, g = x_ref[...], g_ref[...]
        xn, r = _rms_fwd(xv, g)
        n_ref[...] = xn.astype(BF16)
        dx, dg_rows = _rms_bwd(dn, xv, g, r)
        dx_ref[...] = dres_ref[...] + dx
        dgn = _colsum(dg_rows)

        @pl.when(i == 0)
        def _():
            dgain_ref[...] = dgn

        @pl.when(i > 0)
        def _():
            dgain_ref[...] += dgn

    tok = _tok(tm, D_MODEL)
    return pl.pallas_call(
        body, name="proj_bwd", grid=(t // tm,),
        in_specs=[_tok(tm, 2 * D_MODEL), _tok(tm, 3 * 512), _tok(tm, LAT_W), _full(wgate_t.shape), _full(wfox_t.shape),
                  _full(wlat_t.shape), tok, _full((1, D_MODEL)), tok],
        out_specs=[tok, _full((1, D_MODEL)), tok],
        out_shape=[_sds((t, D_MODEL), F32), _sds((1, D_MODEL), F32), _sds((t, D_MODEL), BF16)],
        compiler_params=pltpu.CompilerParams(dimension_semantics=("arbitrary",)),
    )(dgates, dfox, dlat, wgate_t, wfox_t, wlat_t, x, gain, dres)


def _tri(n, lower):
    r = lax.broadcasted_iota(jnp.int32, (n, n), 0)
    c = lax.broadcasted_iota(jnp.int32, (n, n), 1)
    return ((c <= r) if lower else (c >= r)).astype(F32)


def _log_sigmoid(z):
    return jnp.minimum(z, 0.0) - jnp.log1p(jnp.exp(-jnp.abs(z)))


N_GROUPS = 16
LANES = MLA_QK_LANES


def _split_dot(a, g):
    hi = a.astype(BF16)
    lo = (a - hi.astype(F32)).astype(BF16)
    return _dot(hi, g) + _dot(lo, g)


def _rope_fwd(y, c, s1, s2):
    return y * c + pltpu.roll(y, LANES - ROPE_HALF, 1) * s1 + pltpu.roll(y, ROPE_HALF, 1) * s2


def _rope_bwd(do, c, s1, s2):
    return do * c + pltpu.roll(do * s1, ROPE_HALF, 1) + pltpu.roll(do * s2, LANES - ROPE_HALF, 1)


def _prep_tables(cos, sin):
    t = cos.shape[0]
    z = lambda n: jnp.zeros((t, n), F32)
    c = jnp.concatenate([jnp.ones((t, HEAD_DIM), F32), cos, cos, z(LANES - HEAD_DIM - ROPE_DIM)], axis=1)
    s1 = jnp.concatenate([z(HEAD_DIM), -sin, z(LANES - HEAD_DIM - ROPE_HALF)], axis=1)
    s2 = jnp.concatenate([z(HEAD_DIM + ROPE_HALF), sin, z(LANES - HEAD_DIM - ROPE_DIM)], axis=1)
    lane = jnp.arange(LANES)
    rope_mask = ((lane >= HEAD_DIM) & (lane < HEAD_DIM + ROPE_DIM)).astype(F32)[None, :]
    return c, s1, s2, rope_mask


def _group_matrices():
    lane = jnp.arange(N_HEADS * LANES)
    head, d = (lane // LANES)[:, None], (lane % LANES)[:, None]
    col = jnp.arange(N_GROUPS)[None, :]
    g_mla = ((col == head) & (d < HEAD_DIM)) | ((col == N_HEADS + head) & (d >= HEAD_DIM) & (d < HEAD_DIM + ROPE_DIM))
    g_fox = col == (jnp.arange(N_HEADS * HEAD_DIM) // HEAD_DIM)[:, None]
    inv_mla = jnp.concatenate([jnp.full((1, N_HEADS), 1.0 / HEAD_DIM, F32), jnp.full((1, N_HEADS), 1.0 / ROPE_DIM, F32)], axis=1)
    inv_fox = jnp.full((1, N_GROUPS), 1.0 / HEAD_DIM, F32)
    return g_mla.astype(BF16), g_mla.T.astype(BF16), inv_mla, g_fox.astype(BF16), g_fox.T.astype(BF16), inv_fox


def _interleave_weights(wqb_t, wkvb_t):
    wq = jnp.pad(wqb_t.reshape(N_HEADS, HEAD_DIM + ROPE_DIM, Q_LORA), ((0, 0), (0, LANES - HEAD_DIM - ROPE_DIM), (0, 0)))
    kv = wkvb_t.reshape(N_HEADS, 2, HEAD_DIM, KV_LORA)
    wk = jnp.pad(kv[:, 0], ((0, 0), (0, LANES - HEAD_DIM), (0, 0)))
    return wq.reshape(N_HEADS * LANES, Q_LORA), wk.reshape(N_HEADS * LANES, KV_LORA), kv[:, 1].reshape(N_HEADS * HEAD_DIM, KV_LORA)


def _deinterleave_grads(dwq, dwk, dwv):
    dq = dwq.reshape(N_HEADS, LANES, Q_LORA)[:, :HEAD_DIM + ROPE_DIM].reshape(N_HEADS * (HEAD_DIM + ROPE_DIM), Q_LORA)
    dk = dwk.reshape(N_HEADS, LANES, KV_LORA)[:, :HEAD_DIM]
    dkv = jnp.stack([dk, dwv.reshape(N_HEADS, HEAD_DIM, KV_LORA)], axis=1)
    return dq, dkv.reshape(N_HEADS * 2 * HEAD_DIM, KV_LORA)


def _head_gains(g_qn, g_qr, g_kn, g_kr, g_fq, g_fk):
    z = lambda n: jnp.zeros((1, n), F32)
    gq = jnp.concatenate([g_qn, g_qr, z(LANES - HEAD_DIM - ROPE_DIM)], axis=1)
    gk = jnp.concatenate([g_kn, z(LANES - HEAD_DIM)], axis=1)
    gkr = jnp.concatenate([z(HEAD_DIM), g_kr, z(LANES - HEAD_DIM - ROPE_DIM)], axis=1)
    return jnp.tile(gq, (1, N_HEADS)), jnp.tile(gk, (1, N_HEADS)), gkr, jnp.tile(g_fq, (1, N_HEADS)), jnp.tile(g_fk, (1, N_HEADS))


def _group_rms(x, g, g_t, inv):
    r = lax.rsqrt(_split_dot(x * x, g) * inv + RMS_EPS)
    return _split_dot(r, g_t)


def _prep_fwd(fox, lat, tables, mats, wq, wk, wv, g_qlat, g_kvlat, gains, b_f):
    t = fox.shape[0]
    tm = min(PREP_TILE, t)
    c_tab, s1_tab, s2_tab, rope_mask = tables

    def body(fox_ref, lat_ref, c_ref, s1_ref, s2_ref, rm_ref, gm_ref, gmt_ref, im_ref, gf_ref, gft_ref, if_ref,
             wq_ref, wk_ref, wv_ref, gql_ref, gkvl_ref, gq_ref, gk_ref, gkr_ref, gfq_ref, gfk_ref, bf_ref,
             fq_ref, fk_ref, fv_ref, cc_ref, mq_ref, mk_ref, mv_ref, fqn_ref, fkmax_ref, mqn_ref, mkmax_ref, carry):
        i = pl.program_id(0)
        ct, s1, s2 = c_ref[...], s1_ref[...], s2_ref[...]

        @pl.when(i == 0)
        def _():
            carry[...] = jnp.zeros_like(carry)
            fkmax_ref[...] = jnp.zeros_like(fkmax_ref)
            mkmax_ref[...] = jnp.zeros_like(mkmax_ref)

        def square_sums(ref, g):
            v = ref[...].astype(F32)
            return _split_dot(v * v, g)

        xq = fox_ref[:, 0:512]
        fq_ref[...] = (xq * _group_rms(xq, gf_ref[...], gft_ref[...], if_ref[...]) * (gfq_ref[...] * FOX_SCALE)).astype(BF16)
        xk = fox_ref[:, 512:1024]
        fk_ref[...] = (xk * _group_rms(xk, gf_ref[...], gft_ref[...], if_ref[...]) * gfk_ref[...]).astype(BF16)
        fv_ref[...] = fox_ref[:, 1024:1536].astype(BF16)
        fqn_ref[...] = square_sums(fq_ref, gf_ref[...])
        fkmax_ref[...] = jnp.maximum(fkmax_ref[...], jnp.max(square_sums(fk_ref, gf_ref[...]), axis=0, keepdims=True))


        logf = _log_sigmoid(lat_ref[:, LAT_F:LAT_F + N_HEADS] + bf_ref[...])
        cc_ref[...] = jnp.dot(_tri(tm, True), logf, precision=lax.Precision.HIGHEST, preferred_element_type=F32) + carry[...]
        carry[...] += _colsum(logf)

        qlat_n, _ = _rms_fwd(lat_ref[:, LAT_Q:LAT_Q + Q_LORA], gql_ref[...])
        p = _dot(qlat_n.astype(BF16), wq_ref[...], NT)
        y = p * _group_rms(p, gm_ref[...], gmt_ref[...], im_ref[...]) * (gq_ref[...] * MLA_SCALE)
        for h in range(N_HEADS):
            w = slice(h * LANES, (h + 1) * LANES)
            mq_ref[:, w] = _rope_fwd(y[:, w], ct, s1, s2).astype(BF16)

        kv_n, _ = _rms_fwd(lat_ref[:, LAT_KV:LAT_KV + KV_LORA], gkvl_ref[...])
        kv_b = kv_n.astype(BF16)
        pk = _dot(kv_b, wk_ref[...], NT)
        kn = pk * _group_rms(pk, gm_ref[...], gmt_ref[...], im_ref[...]) * gk_ref[...]
        rm = rm_ref[...]
        kr = pltpu.roll(lat_ref[:, LAT_KR:LAT_KR + LANES], HEAD_DIM, 1) * rm
        rr = lax.rsqrt(jnp.sum(kr * kr, axis=1, keepdims=True) * (1.0 / ROPE_DIM) + RMS_EPS)
        okr = _rope_fwd(kr * rr * gkr_ref[...], ct * rm, s1, s2)
        for h in range(N_HEADS):
            w = slice(h * LANES, (h + 1) * LANES)
            mk_ref[:, w] = (kn[:, w] + okr).astype(BF16)
        mv_ref[...] = _dot(kv_b, wv_ref[...], NT).astype(BF16)
        mqn_ref[...] = square_sums(mq_ref, gm_ref[...])
        mkmax_ref[...] = jnp.maximum(mkmax_ref[...], jnp.max(square_sums(mk_ref, gm_ref[...]), axis=0, keepdims=True))

    consts = [rope_mask, *mats, wq, wk, wv, g_qlat, g_kvlat, *gains, b_f]
    t512, tl, tg, g1 = _tok(tm, 512), _tok(tm, LANES), _tok(tm, N_GROUPS), _full((1, N_GROUPS))
    return pl.pallas_call(
        body, name="prep_fwd", grid=(t // tm,),
        in_specs=[_tok(tm, 1536), _tok(tm, LAT_W), tl, tl, tl] + [_full(a.shape) for a in consts],
        out_specs=[t512, t512, t512, _tok(tm, N_HEADS), _tok(tm, N_HEADS * LANES), _tok(tm, N_HEADS * LANES), t512,
                   tg, g1, tg, g1],
        out_shape=[_sds((t, 512), BF16), _sds((t, 512), BF16), _sds((t, 512), BF16), _sds((t, N_HEADS), F32),
                   _sds((t, N_HEADS * LANES), BF16), _sds((t, N_HEADS * LANES), BF16), _sds((t, 512), BF16),
                   _sds((t, N_GROUPS), F32), _sds((1, N_GROUPS), F32), _sds((t, N_GROUPS), F32), _sds((1, N_GROUPS), F32)],
        scratch_shapes=[pltpu.VMEM((1, N_HEADS), F32)],
        compiler_params=pltpu.CompilerParams(dimension_semantics=("arbitrary",)),
    )(fox, lat, c_tab, s1_tab, s2_tab, *consts)


def _prep_bwd(fox, lat, tables, mats, wq, wk, wv, g_qlat, g_kvlat, gains, b_f, dfq, dfk, dfv, dc, dmq, dmk, dmv):
    t = fox.shape[0]
    tm = min(PREP_TILE, t)
    n_t = t // tm
    c_tab, s1_tab, s2_tab, rope_mask = tables

    def body(fox_ref, lat_ref, c_ref, s1_ref, s2_ref, rm_ref, gm_ref, gmt_ref, im_ref, gf_ref, gft_ref, if_ref,
             wq_ref, wk_ref, wv_ref, gql_ref, gkvl_ref, gq_ref, gk_ref, gkr_ref, gfq_ref, gfk_ref, bf_ref,
             dfq_ref, dfk_ref, dfv_ref, dc_ref, dmq_ref, dmk_ref, dmv_ref,
             dfox_ref, dlat_ref, dwq_ref, dwk_ref, dwv_ref, o_gql, o_gkvl, o_gq, o_gk, o_gkr, o_gfq, o_gfk, o_bf,
             carry, lscr, wq_acc, wk_acc, wv_acc, dyscr):
        i = pl.program_id(0)
        ct, s1, s2, rm = c_ref[...], s1_ref[...], s2_ref[...], rm_ref[...]

        @pl.when(i == 0)
        def _():
            for ref in [carry, wq_acc, wk_acc, wv_acc, o_gql, o_gkvl, o_gq, o_gk, o_gkr, o_gfq, o_gfk, o_bf]:
                ref[...] = jnp.zeros_like(ref)

        def group_rms_bwd(dy, x, r, gain, g, g_t, inv):
            xn = x * r
            dyg = dy * gain
            mu = _split_dot(_split_dot(dyg * xn, g) * inv, g_t)
            return r * (dyg - xn * mu), dy * xn

        gf, gft, invf = gf_ref[...], gft_ref[...], if_ref[...]
        xq = fox_ref[:, 0:512]
        dxq, rows = group_rms_bwd(dfq_ref[...] * FOX_SCALE, xq, _group_rms(xq, gf, gft, invf), gfq_ref[...], gf, gft, invf)
        dfox_ref[:, 0:512] = dxq.astype(BF16)
        o_gfq[...] += _colsum(rows)
        xk = fox_ref[:, 512:1024]
        dxk, rows = group_rms_bwd(dfk_ref[...], xk, _group_rms(xk, gf, gft, invf), gfk_ref[...], gf, gft, invf)
        dfox_ref[:, 512:1024] = dxk.astype(BF16)
        o_gfk[...] += _colsum(rows)
        dfox_ref[:, 1024:1536] = dfv_ref[...].astype(BF16)

        lscr[...] = jnp.zeros_like(lscr)

        xql = lat_ref[:, LAT_Q:LAT_Q + Q_LORA]
        qlat_n, r_ql = _rms_fwd(xql, gql_ref[...])
        qlat_b = qlat_n.astype(BF16)
        gm, gmt, invm = gm_ref[...], gmt_ref[...], im_ref[...]
        p = _dot(qlat_b, wq_ref[...], NT)
        for h in range(N_HEADS):
            w = slice(h * LANES, (h + 1) * LANES)
            dyscr[:, w] = _rope_bwd(dmq_ref[:, w], ct, s1, s2) * MLA_SCALE
        dp, rows = group_rms_bwd(dyscr[...], p, _group_rms(p, gm, gmt, invm), gq_ref[...], gm, gmt, invm)
        o_gq[...] += _colsum(rows)
        dp_b = dp.astype(BF16)
        wq_acc[...] += _dot(dp_b, qlat_b, TN)
        dxql, rows = _rms_bwd(_dot(dp_b, wq_ref[...]), xql, gql_ref[...], r_ql)
        lscr[:, LAT_Q:LAT_Q + Q_LORA] = dxql
        o_gql[...] += _colsum(rows)

        xkv = lat_ref[:, LAT_KV:LAT_KV + KV_LORA]
        kv_n, r_kv = _rms_fwd(xkv, gkvl_ref[...])
        kv_b = kv_n.astype(BF16)
        pk = _dot(kv_b, wk_ref[...], NT)
        dpk, rows = group_rms_bwd(dmk_ref[...], pk, _group_rms(pk, gm, gmt, invm), gk_ref[...], gm, gmt, invm)
        o_gk[...] += _colsum(rows)
        dpk_b = dpk.astype(BF16)
        dv_b = dmv_ref[...].astype(BF16)
        wk_acc[...] += _dot(dpk_b, kv_b, TN)
        wv_acc[...] += _dot(dv_b, kv_b, TN)
        dxkv, rows = _rms_bwd(_dot(dpk_b, wk_ref[...]) + _dot(dv_b, wv_ref[...]), xkv, gkvl_ref[...], r_kv)
        lscr[:, LAT_KV:LAT_KV + KV_LORA] = dxkv
        o_gkvl[...] += _colsum(rows)

        dokr = dmk_ref[:, 0:LANES]
        for h in range(1, N_HEADS):
            dokr = dokr + dmk_ref[:, h * LANES:(h + 1) * LANES]
        dykr = _rope_bwd(dokr * rm, ct * rm, s1, s2)
        kr = pltpu.roll(lat_ref[:, LAT_KR:LAT_KR + LANES], HEAD_DIM, 1) * rm
        rr = lax.rsqrt(jnp.sum(kr * kr, axis=1, keepdims=True) * (1.0 / ROPE_DIM) + RMS_EPS)
        krn = kr * rr
        dyg = dykr * gkr_ref[...]
        dkr = rr * (dyg - krn * (jnp.sum(dyg * krn, axis=1, keepdims=True) * (1.0 / ROPE_DIM)))
        o_gkr[...] += _colsum(dykr * krn)
        lscr[:, LAT_KR:LAT_KR + LANES] = pltpu.roll(dkr, LANES - HEAD_DIM, 1)

        dcv = dc_ref[...]
        dlogf = jnp.dot(_tri(tm, False), dcv, precision=lax.Precision.HIGHEST, preferred_element_type=F32) + carry[...]
        carry[...] += _colsum(dcv)
        dz = dlogf * jax.nn.sigmoid(-(lat_ref[:, LAT_F:LAT_F + N_HEADS] + bf_ref[...]))
        lscr[:, LAT_F:LAT_F + N_HEADS] = dz
        o_bf[...] += _colsum(dz)

        dlat_ref[...] = lscr[...].astype(BF16)

        @pl.when(i == n_t - 1)
        def _():
            dwq_ref[...] = wq_acc[...].astype(BF16)
            dwk_ref[...] = wk_acc[...].astype(BF16)
            dwv_ref[...] = wv_acc[...].astype(BF16)

    consts = [rope_mask, *mats, wq, wk, wv, g_qlat, g_kvlat, *gains, b_f]

    def rtok(n):
        return pl.BlockSpec((tm, n), lambda i: (n_t - 1 - i, 0))

    sums = [(1, Q_LORA), (1, KV_LORA), (1, N_HEADS * LANES), (1, N_HEADS * LANES), (1, LANES), (1, 512), (1, 512), (1, N_HEADS)]
    return pl.pallas_call(
        body, name="prep_bwd", grid=(n_t,),
        in_specs=[rtok(1536), rtok(LAT_W), rtok(LANES), rtok(LANES), rtok(LANES)] + [_full(a.shape) for a in consts]
        + [rtok(512), rtok(512), rtok(512), rtok(N_HEADS), rtok(N_HEADS * LANES), rtok(N_HEADS * LANES), rtok(512)],
        out_specs=[rtok(1536), rtok(LAT_W), _full(wq.shape), _full(wk.shape), _full(wv.shape)] + [_full(s) for s in sums],
        out_shape=[_sds((t, 1536), BF16), _sds((t, LAT_W), BF16), _sds(wq.shape, BF16), _sds(wk.shape, BF16), _sds(wv.shape, BF16)]
        + [_sds(s, F32) for s in sums],
        scratch_shapes=[pltpu.VMEM((1, N_HEADS), F32), pltpu.VMEM((tm, LAT_W), F32), pltpu.VMEM(wq.shape, F32),
                        pltpu.VMEM(wk.shape, F32), pltpu.VMEM(wv.shape, F32), pltpu.VMEM((tm, N_HEADS * LANES), F32)],
        compiler_params=pltpu.CompilerParams(dimension_semantics=("arbitrary",)),
    )(fox, lat, c_tab, s1_tab, s2_tab, *consts, dfq, dfk, dfv, dc, dmq, dmk, dmv)


def _causal_pairs(n, query_major):
    pairs = [(i, j) for i in range(n) for j in range(i + 1)] if query_major else [(i, j) for j in range(n) for i in range(j, n)]
    return jnp.asarray([p[0] for p in pairs], jnp.int32), jnp.asarray([p[1] for p in pairs], jnp.int32)


def _logit_bound(q_sq, k_sq_max):
    qn = q_sq[:, :N_HEADS] + q_sq[:, N_HEADS:]
    kmax = k_sq_max[:, :N_HEADS] + k_sq_max[:, N_HEADS:]
    bound = jnp.sqrt(qn * kmax) * (1.0 + 2.0 ** -10) + 2.0 ** -10
    flag = (jnp.max(bound) <= FIXED_SHIFT_MAX_BOUND).astype(F32).reshape(1, 1)
    return bound, flag


def _attn_fwd(q, k, v, bound, fixed_ok, c, c_t, *, lanes, name, exchange=None):
    t = q.shape[0]
    tq = min(ATT_TILE, t)
    n_q = t // tq
    hd = HEAD_DIM
    ch = min(ATT_COL_CHUNK, tq)
    decay = c is not None

    def body(qi_ref, kj_ref, *refs):
        if decay:
            q_ref, k_ref, v_ref, b_ref, ok_ref, c_ref, ct_ref, o_ref, o32_ref, lse_ref, m_scr, l_scr, acc = refs
        else:
            q_ref, k_ref, v_ref, b_ref, ok_ref, o_ref, o32_ref, lse_ref, m_scr, l_scr, acc = refs
        i, j = qi_ref[pl.program_id(0)], kj_ref[pl.program_id(0)]
        fixed = ok_ref[0, 0] > 0.5

        @pl.when(j == 0)
        def _():
            m_scr[...] = jnp.full_like(m_scr, MASK_VALUE)
            l_scr[...] = jnp.zeros_like(l_scr)
            acc[...] = jnp.zeros_like(acc)

        def fixed_step(diagonal):
            for h in range(N_HEADS):
                wl = slice(h * lanes, (h + 1) * lanes)
                w = slice(h * hd, (h + 1) * hd)
                qh = q_ref[:, wl]
                row = (c_ref[:, h:h + 1] - b_ref[:, h:h + 1]) if decay else -b_ref[:, h:h + 1]
                l_new = jnp.zeros((tq, 1), F32)
                o_hi = jnp.zeros((tq, hd), F32)
                o_lo = jnp.zeros((tq, hd), F32)
                for cc in range(tq // ch):
                    cols = slice(cc * ch, (cc + 1) * ch)
                    s = _dot(qh, k_ref[cols, wl], NT)
                    s = s + ((row - ct_ref[h:h + 1, cols]) if decay else row)
                    if diagonal:
                        keep = (lax.broadcasted_iota(jnp.int32, (tq, ch), 0)
                                >= lax.broadcasted_iota(jnp.int32, (tq, ch), 1) + cc * ch)
                        s = jnp.where(keep, s, MASK_VALUE)
                    p = jnp.exp(s)
                    l_new = l_new + jnp.sum(p, axis=1, keepdims=True)
                    p_b = p.astype(BF16)
                    o_hi = o_hi + _dot(p_b, v_ref[cols, w])
                    if decay:
                        o_lo = o_lo + _dot((p - p_b.astype(F32)).astype(BF16), v_ref[cols, w])
                l_scr[h] += l_new
                acc[0, :, w] += o_hi
                if decay:
                    acc[1, :, w] += o_lo

        def step(diagonal):
            if diagonal:
                keep = lax.broadcasted_iota(jnp.int32, (tq, tq), 0) >= lax.broadcasted_iota(jnp.int32, (tq, tq), 1)
            for h in range(N_HEADS):
                s = _dot(q_ref[:, h * lanes:(h + 1) * lanes], k_ref[:, h * lanes:(h + 1) * lanes], NT)
                if decay:
                    s = s + (c_ref[:, h:h + 1] - ct_ref[h:h + 1, :])
                if diagonal:
                    s = jnp.where(keep, s, MASK_VALUE)
                m_prev = m_scr[h]
                m_new = jnp.maximum(m_prev, jnp.max(s, axis=1, keepdims=True))
                alpha = jnp.exp(m_prev - m_new)
                p = jnp.exp(s - m_new)
                l_scr[h] = alpha * l_scr[h] + jnp.sum(p, axis=1, keepdims=True)
                w = slice(h * hd, (h + 1) * hd)
                p_b = p.astype(BF16)
                acc[0, :, w] = alpha * acc[0, :, w] + _dot(p_b, v_ref[:, w])
                if decay:
                    p_lo = (p - p_b.astype(F32)).astype(BF16)
                    acc[1, :, w] = alpha * acc[1, :, w] + _dot(p_lo, v_ref[:, w])
                m_scr[h] = m_new

        for diagonal, here in ((False, j < i), (True, j == i)):
            @pl.when(here & fixed)
            def _():
                fixed_step(diagonal)

            @pl.when(here & jnp.logical_not(fixed))
            def _():
                step(diagonal)

        @pl.when(j == i)
        def _():
            for h in range(N_HEADS):
                w = slice(h * hd, (h + 1) * hd)
                l = l_scr[h]
                o_ref[:, w] = (acc[0, :, w] / l).astype(BF16)
                o32_ref[:, w] = ((acc[0, :, w] + acc[1, :, w]) if decay else acc[0, :, w]) / l
                lse_ref[:, h:h + 1] = jnp.where(fixed, b_ref[:, h:h + 1], m_scr[h]) + jnp.log(l)

    qspec = lambda n: pl.BlockSpec((tq, n), lambda s, qi, kj: (qi[s], 0))
    kspec = lambda n: pl.BlockSpec((tq, n), lambda s, qi, kj: (kj[s], 0))
    in_specs = [qspec(N_HEADS * lanes), kspec(N_HEADS * lanes), kspec(512), qspec(N_HEADS),
                pl.BlockSpec(memory_space=pltpu.SMEM)]
    args = [q, k, v, bound, fixed_ok]
    if decay:
        in_specs += [qspec(N_HEADS), pl.BlockSpec((N_HEADS, tq), lambda s, qi, kj: (0, kj[s]))]
        args += [c, c_t]
    pairs = _causal_pairs(n_q, query_major=True)
    return _gridded_call(
        body, name=name, grid=(pairs[0].shape[0],), in_specs=in_specs, out_specs=[qspec(512), qspec(512), qspec(N_HEADS)],
        out_shape=[_sds((t, 512), BF16), _sds((t, 512), F32), _sds((t, N_HEADS), F32)],
        scratch_shapes=[pltpu.VMEM((N_HEADS, tq, 1), F32), pltpu.VMEM((N_HEADS, tq, 1), F32), pltpu.VMEM((2, tq, 512), F32)],
        args=args, exchange=exchange, prefetch=pairs)


def _attn_bwd(q, k, v, do, delta, lse, c, c_t, *, lanes, name, exchange=None):
    t = q.shape[0]
    tq = min(ATT_TILE, t)
    n_q = t // tq
    hd = HEAD_DIM
    decay = c is not None

    pairs = _causal_pairs(n_q, query_major=False)
    n_steps = pairs[0].shape[0]

    def body(qi_ref, kj_ref, *refs):
        if decay:
            q_ref, k_ref, v_ref, do_ref, delta_ref, lse_ref, c_ref, ct_ref, dq_hbm, dk_ref, dv_ref, dct_ref, dq_ref = refs
        else:
            q_ref, k_ref, v_ref, do_ref, delta_ref, lse_ref, dq_hbm, dk_ref, dv_ref, dq_ref = refs
        step_id = pl.program_id(0)
        i, j = qi_ref[step_id], kj_ref[step_id]

        @pl.when(step_id == 0)
        def _():
            dq_ref[...] = jnp.zeros_like(dq_ref)

        @pl.when(i == j)
        def _():
            dk_ref[...] = jnp.zeros_like(dk_ref)
            dv_ref[...] = jnp.zeros_like(dv_ref)
            if decay:
                dct_ref[...] = jnp.zeros_like(dct_ref)

        def step(diagonal):
            if diagonal:
                keep = lax.broadcasted_iota(jnp.int32, (tq, tq), 0) >= lax.broadcasted_iota(jnp.int32, (tq, tq), 1)
            rows = pl.ds(pl.multiple_of(i * tq, tq), tq)
            for h in range(N_HEADS):
                wl = slice(h * lanes, (h + 1) * lanes)
                w = slice(h * hd, (h + 1) * hd)
                qh, kh = q_ref[:, wl], k_ref[:, wl]
                s = _dot(qh, kh, NT)
                if decay:
                    s = s + (c_ref[:, h:h + 1] - ct_ref[h:h + 1, :])
                if diagonal:
                    s = jnp.where(keep, s, MASK_VALUE)
                p = jnp.exp(s - lse_ref[:, h:h + 1])
                doh = do_ref[:, w]
                dv_ref[:, w] += _dot(p.astype(BF16), doh, TN)
                dp = _dot(doh, v_ref[:, w], NT)
                ds = p * (dp - delta_ref[:, h:h + 1])
                if decay:
                    dct_ref[h:h + 1, :] -= _colsum(ds)
                ds_b = ds.astype(BF16)
                dk_ref[:, wl] += _dot(ds_b, qh, TN)
                dq_ref[rows, wl] += _dot(ds_b, kh)

        @pl.when(i > j)
        def _():
            step(False)

        @pl.when(i == j)
        def _():
            step(True)

        @pl.when(step_id == n_steps - 1)
        def _():
            pltpu.sync_copy(dq_ref, dq_hbm)

    qspec = lambda n: pl.BlockSpec((tq, n), lambda s, qi, kj: (qi[s], 0))
    kspec = lambda n: pl.BlockSpec((tq, n), lambda s, qi, kj: (kj[s], 0))
    in_specs = [qspec(N_HEADS * lanes), kspec(N_HEADS * lanes), kspec(512), qspec(512), qspec(N_HEADS), qspec(N_HEADS)]
    out_specs = [pl.BlockSpec(memory_space=pl.ANY), kspec(N_HEADS * lanes), kspec(512)]
    out_shape = [_sds((t, N_HEADS * lanes), F32), _sds((t, N_HEADS * lanes), F32), _sds((t, 512), F32)]
    args = [q, k, v, do, delta, lse]
    if decay:
        ctspec = pl.BlockSpec((N_HEADS, tq), lambda s, qi, kj: (0, kj[s]))
        in_specs += [qspec(N_HEADS), ctspec]
        out_specs.append(ctspec)
        out_shape.append(_sds((N_HEADS, t), F32))
        args += [c, c_t]
    return _gridded_call(body, name=name, grid=(n_steps,), in_specs=in_specs, out_specs=out_specs, out_shape=out_shape,
                         scratch_shapes=[pltpu.VMEM((t, N_HEADS * lanes), F32)], args=args, exchange=exchange, prefetch=pairs)


def _mix_fwd(x, y_mla, y_fox, gates, b_gate, wbm_t, wbf_t, wo):
    t = x.shape[0]
    tm = min(PROJ_TILE, t)

    def body(x_ref, ym_ref, yf_ref, gt_ref, bg_ref, wbm_ref, wbf_ref, wo_ref, out_ref):
        um = _dot(ym_ref[...], wbm_ref[...], NT)
        uf = _dot(yf_ref[...], wbf_ref[...], NT)
        sm = jax.nn.sigmoid(gt_ref[:, 0:D_MODEL] + bg_ref[0:1, :])
        sf = jax.nn.sigmoid(gt_ref[:, D_MODEL:2 * D_MODEL] + bg_ref[1:2, :])
        mixed = sm * um + sf * uf
        out_ref[...] = x_ref[...] + _dot(mixed.astype(BF16), wo_ref[...])

    tok = _tok(tm, D_MODEL)
    return pl.pallas_call(
        body, name="mix_fwd", grid=(t // tm,),
        in_specs=[tok, _tok(tm, 512), _tok(tm, 512), _tok(tm, 2 * D_MODEL), _full((2, D_MODEL)), _full(wbm_t.shape),
                  _full(wbf_t.shape), _full(wo.shape)],
        out_specs=tok, out_shape=_sds((t, D_MODEL), F32),
        compiler_params=pltpu.CompilerParams(dimension_semantics=("arbitrary",)),
    )(x, y_mla, y_fox, gates, b_gate, wbm_t, wbf_t, wo)


def _mix_bwd(dx, y_mla, y_fox, y_mla32, y_fox32, gates, b_gate, wbm_t, wbf_t, wo, exchange=None):
    t = dx.shape[0]
    tm = min(PROJ_TILE, t)

    def body(dx_ref, ym_ref, yf_ref, ym32_ref, yf32_ref, gt_ref, bg_ref, wbm_ref, wbf_ref, wo_ref,
             dym_ref, dyf_ref, dlm_ref, dlf_ref, dgt_ref, mixed_ref, dum_ref, duf_ref, dxb_ref, dbg_ref, prod):
        i = pl.program_id(0)
        dxb = dx_ref[...].astype(BF16)
        dxb_ref[...] = dxb
        dmixed = _dot(dxb, wo_ref[...], NT)
        um = _dot(ym_ref[...], wbm_ref[...], NT)
        uf = _dot(yf_ref[...], wbf_ref[...], NT)
        sm = jax.nn.sigmoid(gt_ref[:, 0:D_MODEL] + bg_ref[0:1, :])
        sf = jax.nn.sigmoid(gt_ref[:, D_MODEL:2 * D_MODEL] + bg_ref[1:2, :])
        mixed_ref[...] = (sm * um + sf * uf).astype(BF16)
        dum = (dmixed * sm).astype(BF16)
        duf = (dmixed * sf).astype(BF16)
        dum_ref[...] = dum
        duf_ref[...] = duf
        dgm = dmixed * um * (sm * (1.0 - sm))
        dgf = dmixed * uf * (sf * (1.0 - sf))
        dgt_ref[:, 0:D_MODEL] = dgm.astype(BF16)
        dgt_ref[:, D_MODEL:2 * D_MODEL] = dgf.astype(BF16)
        for du, wb_ref, y32_ref, dy_ref, dl_ref in ((dum, wbm_ref, ym32_ref, dym_ref, dlm_ref),
                                                    (duf, wbf_ref, yf32_ref, dyf_ref, dlf_ref)):
            dy = _dot(du, wb_ref[...])
            dy_ref[...] = dy.astype(BF16)
            prod[...] = dy.astype(BF16).astype(F32) * y32_ref[...]
            for h in range(N_HEADS):
                dl_ref[:, h:h + 1] = jnp.sum(prod[:, h * HEAD_DIM:(h + 1) * HEAD_DIM], axis=1, keepdims=True)

        @pl.when(i == 0)
        def _():
            dbg_ref[...] = jnp.zeros_like(dbg_ref)

        dbg_ref[0:1, :] += _colsum(dgm)
        dbg_ref[1:2, :] += _colsum(dgf)

    tok = _tok(tm, D_MODEL)
    tokb = _sds((t, D_MODEL), BF16)
    t512, t8 = _tok(tm, 512), _tok(tm, N_HEADS)
    return _gridded_call(
        body, name="mix_bwd", grid=(t // tm,),
        in_specs=[tok, t512, t512, t512, t512, _tok(tm, 2 * D_MODEL), _full((2, D_MODEL)), _full(wbm_t.shape),
                  _full(wbf_t.shape), _full(wo.shape)],
        out_specs=[t512, t512, t8, t8, _tok(tm, 2 * D_MODEL), tok, tok, tok, tok, _full((2, D_MODEL))],
        out_shape=[_sds((t, 512), BF16), _sds((t, 512), BF16), _sds((t, N_HEADS), F32), _sds((t, N_HEADS), F32),
                   _sds((t, 2 * D_MODEL), BF16), tokb, tokb, tokb, tokb, _sds((2, D_MODEL), F32)],
        scratch_shapes=[pltpu.VMEM((tm, 512), F32)],
        args=[dx, y_mla, y_fox, y_mla32, y_fox32, gates, b_gate, wbm_t, wbf_t, wo], exchange=exchange)


def _my_position():
    x, y, c = lax.axis_index("x"), lax.axis_index("y"), lax.axis_index("c")
    return x, y, c, 4 * x + 2 * y + c


def _peer(x, y, c, mask):
    px = 1 - x if mask & 4 else x
    py = 1 - y if mask & 2 else y
    pc = 1 - c if mask & 1 else c
    return (px, py, pc), 4 * px + 2 * py + pc


def _chip_peer(x, y, km):
    px = 1 - x if km & 2 else x
    py = 1 - y if km & 1 else y
    return px, py, 2 * px + py


_HBM = pl.BlockSpec(memory_space=pl.ANY)


def _wait_all(copies):
    for cp in copies:
        cp.wait()


class _ChipExchange:
    def __init__(self, gather, arrays):
        self.gather, self.arrays, self.aliased = gather, list(arrays), False
        n = len(self.arrays)
        self.out_shape = [_sds((N_DEV * a.shape[0],) + a.shape[1:], a.dtype) if gather else _sds(a.shape, a.dtype)
                          for a in self.arrays]
        self.scratch_shapes = [pltpu.SemaphoreType.DMA((n, N_CHIP)), pltpu.SemaphoreType.DMA((n, N_CHIP)),
                               pltpu.SemaphoreType.DMA((n,))]

    def copies(self, srcs, dsts, send_sems, recv_sems, local_sems):
        x, y, c, me = _my_position()
        q_me = 2 * x + y
        out = []
        for a in range(len(self.arrays)):
            if self.gather:
                r = srcs[a].shape[0]
                local_src, dst = srcs[a], dsts[a].at[pl.ds(me * r, r)]
            else:
                local_src, dst = srcs[a].at[q_me], dsts[a].at[q_me]
            out.append(pltpu.make_async_copy(local_src, dst, local_sems.at[a]))
            for km in range(1, N_CHIP):
                px, py, q_peer = _chip_peer(x, y, km)
                out.append(pltpu.make_async_remote_copy(
                    src_ref=srcs[a] if self.gather else srcs[a].at[q_peer], dst_ref=dst, send_sem=send_sems.at[a, km],
                    recv_sem=recv_sems.at[a, km], device_id=(px, py, c), device_id_type=MESH))
        return out

    def standalone(self, name):
        n = len(self.arrays)

        def body(*refs):
            copies = self.copies(refs[:n], refs[n:2 * n], *refs[2 * n:])
            for cp in copies:
                cp.start()
            _wait_all(copies)

        return pl.pallas_call(body, name=name, in_specs=[_HBM] * n, out_specs=[_HBM] * n, out_shape=self.out_shape,
                              scratch_shapes=self.scratch_shapes)(*self.arrays)


def _gridded_call(body, *, name, grid, in_specs, out_specs, out_shape, scratch_shapes, args, exchange=None, prefetch=()):
    params = pltpu.CompilerParams(dimension_semantics=("arbitrary",) * len(grid))
    n_pre, n_in, n_out, n_scr = len(prefetch), len(in_specs), len(out_specs), len(scratch_shapes)
    n_x = 0 if exchange is None else len(exchange.arrays)

    def carrier(*refs):
        pre, refs = refs[:n_pre], refs[n_pre:]
        ins, x_src, refs = refs[:n_in], refs[n_in:n_in + n_x], refs[n_in + n_x:]
        outs, x_dst, refs = refs[:n_out], refs[n_out:n_out + n_x], refs[n_out + n_x:]
        copies = exchange.copies(x_src, x_dst, *refs[n_scr:])
        pids = [pl.program_id(d) for d in range(len(grid))]
        first = functools.reduce(jnp.logical_and, [p == 0 for p in pids])
        last = functools.reduce(jnp.logical_and, [p == g - 1 for p, g in zip(pids, grid)])

        @pl.when(first)
        def _():
            for cp in copies:
                cp.start()

        body(*pre, *ins, *outs, *refs[:n_scr])

        @pl.when(last)
        def _():
            _wait_all(copies)

    x_shapes, x_scratch, x_args = ([], [], []) if exchange is None else (exchange.out_shape, exchange.scratch_shapes, exchange.arrays)
    spec = pltpu.PrefetchScalarGridSpec(
        num_scalar_prefetch=n_pre, grid=grid, in_specs=list(in_specs) + [_HBM] * n_x, out_specs=list(out_specs) + [_HBM] * n_x,
        scratch_shapes=list(scratch_shapes) + x_scratch)
    in_place = {n_pre + n_in + k: n_out + k for k in range(n_x)} if n_x and exchange.aliased else {}
    res = pl.pallas_call(body if exchange is None else carrier, name=name, grid_spec=spec, out_shape=list(out_shape) + x_shapes,
                         input_output_aliases=in_place, compiler_params=params)(*prefetch, *args, *x_args)
    return res[:n_out], (None if exchange is None else res[n_out:])


class _CoreExchange:
    def __init__(self, gather, arrays):
        self.gather, self.arrays, self.aliased = gather, list(arrays), gather
        n = len(self.arrays)
        self.out_shape = [_sds(a.shape, a.dtype) if gather else _sds((N_CHIP, a.shape[0] // N_DEV) + a.shape[1:], a.dtype)
                          for a in self.arrays]
        self.scratch_shapes = [pltpu.SemaphoreType.DMA((n, N_CHIP)), pltpu.SemaphoreType.DMA((n, N_CHIP))]

    def copies(self, srcs, dsts, send_sems, recv_sems):
        x, y, c, _ = _my_position()
        out = []
        for a in range(len(self.arrays)):
            r = srcs[a].shape[0] // N_DEV
            for q in range(N_CHIP):
                if self.gather:
                    rows = pl.ds((2 * q + c) * r, r)
                    src, dst = srcs[a].at[rows], dsts[a].at[rows]
                else:
                    src, dst = srcs[a].at[pl.ds((2 * q + 1 - c) * r, r)], dsts[a].at[q]
                out.append(pltpu.make_async_remote_copy(src_ref=src, dst_ref=dst, send_sem=send_sems.at[a, q],
                                                        recv_sem=recv_sems.at[a, q], device_id=(x, y, 1 - c), device_id_type=MESH))
        return out

    def standalone(self, name):
        n = len(self.arrays)

        def body(*refs):
            copies = self.copies(refs[:n], refs[n:2 * n], *refs[2 * n:])
            for cp in copies:
                cp.start()
            _wait_all(copies)

        return pl.pallas_call(body, name=name, in_specs=[_HBM] * n, out_specs=[_HBM] * n, out_shape=self.out_shape,
                              input_output_aliases={a: a for a in range(n)} if self.aliased else {},
                              scratch_shapes=self.scratch_shapes)(*self.arrays)


def _gather_over_cores(arrays, name):
    return _CoreExchange(True, arrays).standalone(name)


def _grads_to_sibling(grads, name):
    return _CoreExchange(False, grads).standalone(name)


def _pair_sum(grad, from_sibling, name):
    r, n = from_sibling.shape[1:]

    def body(g_ref, s_ref, o_ref):
        c = lax.axis_index("c")
        o_ref[...] = (g_ref[c].astype(F32) + s_ref[...].astype(F32)).astype(BF16)

    return pl.pallas_call(
        body, name=name, grid=(N_CHIP,),
        in_specs=[pl.BlockSpec((None, 2, r, n), lambda q: (q, 0, 0, 0)), pl.BlockSpec((None, r, n), lambda q: (q, 0, 0))],
        out_specs=pl.BlockSpec((None, r, n), lambda q: (q, 0, 0)), out_shape=_sds((N_CHIP, r, n), BF16),
    )(grad.reshape(N_CHIP, 2, r, n), from_sibling)


def _all_reduce_small(vec):
    r = vec.shape[0]

    def body(v_ref, o_ref, buf, send_sems, recv_sems):
        x, y, c, me = _my_position()
        buf[me] = v_ref[...]
        copies = []
        for mask in range(1, N_DEV):
            peer, _ = _peer(x, y, c, mask)
            cp = pltpu.make_async_remote_copy(src_ref=v_ref, dst_ref=buf.at[me], send_sem=send_sems.at[mask],
                                              recv_sem=recv_sems.at[mask], device_id=peer, device_id_type=MESH)
            cp.start()
            copies.append(cp)
        for cp in copies:
            cp.wait()
        total = buf[0]
        for s in range(1, N_DEV):
            total = total + buf[s]
        o_ref[...] = total

    vm = pl.BlockSpec(memory_space=pltpu.VMEM)
    return pl.pallas_call(
        body, name="all_reduce_small", in_specs=[vm], out_specs=vm, out_shape=_sds(vec.shape, F32),
        scratch_shapes=[pltpu.VMEM((N_DEV, r, 128), F32), pltpu.SemaphoreType.DMA((N_DEV,)), pltpu.SemaphoreType.DMA((N_DEV,))],
    )(vec)


def _adamw_math(w, g, m, v):
    m = ADAM_B1 * m + (1.0 - ADAM_B1) * g
    v = ADAM_B2 * v + (1.0 - ADAM_B2) * (g * g)
    m_hat = m / (1.0 - ADAM_B1 ** ADAM_STEP)
    v_hat = v / (1.0 - ADAM_B2 ** ADAM_STEP)
    delta = -ADAM_LR * (m_hat / (jnp.sqrt(v_hat) + ADAM_EPS) + ADAM_WD * w)
    return delta, m, v


def _reduce_adamw(slots, w, m, v, *, transpose, name):
    r, n = slots.shape[1:]
    padded = not transpose and w.shape[0] != r

    def body(s_ref, w_ref, m_ref, v_ref, g_ref, d_ref, nm_ref, nv_ref, *scr):
        g = s_ref[0].astype(F32)
        for s in range(1, slots.shape[0]):
            g = g + s_ref[s].astype(F32)
        if transpose:
            scr[0][...] = g.T
            g = scr[0][:, 0:w_ref.shape[1]]
        elif padded:
            scr[0][...] = g
            g = scr[0][0:w_ref.shape[0], :]
        g_ref[...] = g
        d_ref[...], nm_ref[...], nv_ref[...] = _adamw_math(w_ref[...], g, m_ref[...], v_ref[...])

    out = _sds(w.shape, F32)
    if not transpose and not padded and r % (2 * BF16_ROW_TILE) == 0:
        half = pl.BlockSpec((r // 2, n), lambda i: (i, 0))
        return pl.pallas_call(
            body, name=name, grid=(2,), in_specs=[pl.BlockSpec((slots.shape[0], r // 2, n), lambda i: (0, i, 0)), half, half, half],
            out_specs=[half] * 4, out_shape=[out, out, out, out])(slots, w, m, v)
    return pl.pallas_call(
        body, name=name, out_shape=[out, out, out, out],
        scratch_shapes=[pltpu.VMEM((n, r) if transpose else (r, n), F32)] if transpose or padded else [],
    )(slots, w, m, v)


def _adamw(g, w, m, v, name):
    def body(g_ref, w_ref, m_ref, v_ref, d_ref, nm_ref, nv_ref):
        d_ref[...], nm_ref[...], nv_ref[...] = _adamw_math(w_ref[...], g_ref[...], m_ref[...], v_ref[...])

    out = _sds(w.shape, F32)
    return pl.pallas_call(body, name=name, out_shape=[out, out, out])(g, w, m, v)


_SMALL = ["ffn1_norm", "mix_norm", "ffn2_norm", "mla_q_lat_norm", "mla_kv_lat_norm", "mla_q_nope_gain", "mla_q_rope_gain",
          "mla_k_nope_gain", "mla_k_rope_gain", "fox_q_gain", "fox_k_gain", "fox_b_f"]
_WEIGHTS = ["ffn1_norm", "ffn1_w_gate", "ffn1_w_up", "ffn1_w_down", "mix_norm", "w_in", "mla_q_lat_norm", "mla_w_qb",
            "mla_kv_lat_norm", "mla_w_kvb", "mla_q_nope_gain", "mla_q_rope_gain", "mla_k_nope_gain", "mla_k_rope_gain",
            "fox_q_gain", "fox_k_gain", "fox_b_f", "w_branch_mla", "w_branch_fox", "b_gate", "w_o", "ffn2_norm",
            "ffn2_w_gate", "ffn2_w_up", "ffn2_w_down"]
_IN_Q, _IN_KV, _IN_KR, _IN_FOX, _IN_F, _IN_GATES = (0, 192), (192, 128), (320, 32), (352, 1536), (1888, 8), (1896, 2048)


def _rows(a, seg):
    return a[seg[0]:seg[0] + seg[1]]


def _split_w_in(win_t):
    z = lambda n: jnp.zeros((n, D_MODEL), win_t.dtype)
    lat = jnp.concatenate([_rows(win_t, _IN_Q), z(LAT_KV - Q_LORA), _rows(win_t, _IN_KV), _rows(win_t, _IN_KR),
                           _rows(win_t, _IN_F), z(LAT_W - LAT_F - N_HEADS)], axis=0)
    return _rows(win_t, _IN_GATES), _rows(win_t, _IN_FOX), lat


def _join_w_in(d_gates, d_fox, d_lat):
    return jnp.concatenate([d_lat[LAT_Q:LAT_Q + Q_LORA], d_lat[LAT_KV:LAT_KV + KV_LORA], d_lat[LAT_KR:LAT_KR + ROPE_DIM],
                            d_fox, d_lat[LAT_F:LAT_F + N_HEADS], d_gates], axis=0)


def kernel(x, positions, ffn1_norm, ffn1_w_gate, ffn1_w_up, ffn1_w_down, mix_norm, w_in, mla_q_lat_norm, mla_w_qb, mla_kv_lat_norm, mla_w_kvb, mla_q_nope_gain, mla_q_rope_gain, mla_k_nope_gain, mla_k_rope_gain, fox_q_gain, fox_k_gain, fox_b_f, w_branch_mla, w_branch_fox, b_gate, w_o, ffn2_norm, ffn2_w_gate, ffn2_w_up, ffn2_w_down, loss_target, m_ffn1_norm, m_ffn1_w_gate, m_ffn1_w_up, m_ffn1_w_down, m_mix_norm, m_w_in, m_mla_q_lat_norm, m_mla_w_qb, m_mla_kv_lat_norm, m_mla_w_kvb, m_mla_q_nope_gain, m_mla_q_rope_gain, m_mla_k_nope_gain, m_mla_k_rope_gain, m_fox_q_gain, m_fox_k_gain, m_fox_b_f, m_w_branch_mla, m_w_branch_fox, m_b_gate, m_w_o, m_ffn2_norm, m_ffn2_w_gate, m_ffn2_w_up, m_ffn2_w_down, v_ffn1_norm, v_ffn1_w_gate, v_ffn1_w_up, v_ffn1_w_down, v_mix_norm, v_w_in, v_mla_q_lat_norm, v_mla_w_qb, v_mla_kv_lat_norm, v_mla_w_kvb, v_mla_q_nope_gain, v_mla_q_rope_gain, v_mla_k_nope_gain, v_mla_k_rope_gain, v_fox_q_gain, v_fox_k_gain, v_fox_b_f, v_w_branch_mla, v_w_branch_fox, v_b_gate, v_w_o, v_ffn2_norm, v_ffn2_w_gate, v_ffn2_w_up, v_ffn2_w_down):
    env = dict(locals())
    strip = lambda n, a: a if n in _SMALL else a[0]
    W = {n: strip(n, env[n]) for n in _WEIGHTS}
    M = {n: strip(n, env["m_" + n]) for n in _WEIGHTS}
    V = {n: strip(n, env["v_" + n]) for n in _WEIGHTS}
    xs = x[0]
    t = xs.shape[0]

    col_split = ["ffn1_w_gate", "ffn1_w_up", "ffn2_w_gate", "ffn2_w_up", "mla_w_qb", "mla_w_kvb", "w_branch_mla", "w_branch_fox"]
    row_split = ["ffn1_w_down", "ffn2_w_down", "w_o"]
    pieces = {n: W[n].T.astype(BF16) for n in col_split}
    pieces.update({n: W[n].astype(BF16) for n in row_split})
    pieces["w_in"] = jnp.pad(W["w_in"].T.astype(BF16), ((0, W_IN_PIECE_PAD - W_IN_PIECE), (0, 0)))
    pieces["b_gate"] = W["b_gate"].T
    group_a = ["ffn1_w_gate", "ffn1_w_up", "ffn1_w_down"]
    group_b = ["w_in", "mla_w_qb", "mla_w_kvb", "w_branch_mla", "w_branch_fox", "w_o", "b_gate"]
    group_c = ["ffn2_w_gate", "ffn2_w_up", "ffn2_w_down"]
    gather = lambda group: _ChipExchange(True, [pieces[n] for n in group])
    G = dict(zip(group_a, _gather_over_cores(gather(group_a).standalone("gather_ici_a"), "gather_d2d_a")))

    inv_freq = ROPE_THETA ** (-jnp.arange(ROPE_HALF, dtype=F32) / ROPE_HALF)
    ang = positions[0].astype(F32)[:, None] * inv_freq
    tables, mats = _prep_tables(jnp.cos(ang), jnp.sin(ang)), _group_matrices()
    gains = _head_gains(*[W[n] for n in ["mla_q_nope_gain", "mla_q_rope_gain", "mla_k_nope_gain", "mla_k_rope_gain",
                                         "fox_q_gain", "fox_k_gain"]])

    (x1, a1, b1), got_b = _ffn_fwd(xs, W["ffn1_norm"], G["ffn1_w_gate"], G["ffn1_w_up"], G["ffn1_w_down"],
                                   exchange=gather(group_b))
    G.update(zip(group_b, _gather_over_cores(got_b, "gather_d2d_b")))
    win_t = G["w_in"].reshape(N_DEV, W_IN_PIECE_PAD, D_MODEL)[:, :W_IN_PIECE].reshape(N_DEV * W_IN_PIECE, D_MODEL)
    wgate_t, wfox_t, wlat_t = _split_w_in(win_t)
    bg = G["b_gate"].T
    gates, fox, lat = _proj_fwd(x1, W["mix_norm"], wgate_t, wfox_t, wlat_t)
    prep_args = (fox, lat, tables, mats, *_interleave_weights(G["mla_w_qb"], G["mla_w_kvb"]), W["mla_q_lat_norm"],
                 W["mla_kv_lat_norm"], gains, W["fox_b_f"])
    fq, fk, fv, c, mq, mk, mv, fq_sq, fk_sq_max, mq_sq, mk_sq_max = _prep_fwd(*prep_args)
    c_t = c.T
    b_fox, ok_fox = _logit_bound(fq_sq, fk_sq_max)
    b_mla, ok_mla = _logit_bound(mq_sq, mk_sq_max)
    (y_fox, y_fox32, lse_fox), got_c = _attn_fwd(fq, fk, fv, b_fox, ok_fox, c, c_t, lanes=HEAD_DIM, name="fox_fwd",
                                                 exchange=gather(group_c))
    (y_mla, y_mla32, lse_mla), got_c = _attn_fwd(mq, mk, mv, b_mla, ok_mla, None, None, lanes=MLA_QK_LANES, name="mla_fwd",
                                                 exchange=_CoreExchange(True, got_c))
    G.update(zip(group_c, got_c))
    x2 = _mix_fwd(x1, y_mla, y_fox, gates, bg, G["w_branch_mla"], G["w_branch_fox"], G["w_o"])
    (dx3, a2, b2, loss_vec), _ = _ffn_fwd(x2, W["ffn2_norm"], G["ffn2_w_gate"], G["ffn2_w_up"], G["ffn2_w_down"],
                                          target=loss_target[0])

    def chip_sums(group, tag, from_sibling=None):
        if from_sibling is None:
            from_sibling = _grads_to_sibling([grads[n] for n in group], "grads_d2d_" + tag)
        return _ChipExchange(False, [_pair_sum(grads[n], s, "pair_sum_" + n) for n, s in zip(group, from_sibling)])

    (dx2, dg_ffn2, da2, db2, h2, n2, dyh2), _ = _ffn_bwd(dx3, x2, W["ffn2_norm"], a2, b2, G["ffn2_w_gate"], G["ffn2_w_up"],
                                                        G["ffn2_w_down"], "ffn2_bwd")
    grads = {"ffn2_w_gate": _tn_matmul(da2, n2, "ffn2_dgate"), "ffn2_w_up": _tn_matmul(db2, n2, "ffn2_dup"),
             "ffn2_w_down": _tn_matmul(h2, dyh2, "ffn2_ddown")}
    (dy_mla, dy_fox, delta_mla, delta_fox, dgates, mixed, dum, duf, dx2b, dbg), from_sibling_c = _mix_bwd(
        dx2, y_mla, y_fox, y_mla32, y_fox32, gates, bg, G["w_branch_mla"], G["w_branch_fox"], G["w_o"],
        exchange=_CoreExchange(False, [grads[n] for n in group_c]))
    grads["w_o"] = _tn_matmul(mixed, dx2b, "d_w_o")
    grads["w_branch_mla"] = _tn_matmul(dum, y_mla, "d_w_branch_mla")
    grads["w_branch_fox"] = _tn_matmul(duf, y_fox, "d_w_branch_fox")
    (dfq, dfk, dfv, dc_t), slots_c = _attn_bwd(fq, fk, fv, dy_fox, delta_fox, lse_fox, c, c_t, lanes=HEAD_DIM,
                                               name="fox_bwd", exchange=chip_sums(group_c, "c", from_sibling_c))
    slots = dict(zip(group_c, slots_c))
    (dmq, dmk, dmv), _ = _attn_bwd(mq, mk, mv, dy_mla, delta_mla, lse_mla, None, None, lanes=MLA_QK_LANES, name="mla_bwd")
    dfox, dlat, dwq, dwk, dwv, d_gql, d_gkvl, d_gq, d_gk, d_gkr, d_gfq, d_gfk, d_bf = _prep_bwd(
        *prep_args, dfq, dfk, dfv, dc_t.T, dmq, dmk, dmv)
    grads["mla_w_qb"], grads["mla_w_kvb"] = _deinterleave_grads(dwq, dwk, dwv)
    fold = lambda a, width: a.reshape(N_HEADS, width).sum(axis=0)[None]
    d_gq, d_gk = fold(d_gq, LANES), fold(d_gk, LANES)
    d_prep_small = [d_gql, d_gkvl, d_gq[:, :HEAD_DIM], d_gq[:, HEAD_DIM:HEAD_DIM + ROPE_DIM], d_gk[:, :HEAD_DIM],
                    d_gkr[:, HEAD_DIM:HEAD_DIM + ROPE_DIM], fold(d_gfq, HEAD_DIM), fold(d_gfk, HEAD_DIM), d_bf]
    dx1, dg_mix, nmix = _proj_bwd(dgates, dfox, dlat, wgate_t, wfox_t, wlat_t, x1, W["mix_norm"], dx2)
    dwin_t = _join_w_in(_tn_matmul(dgates, nmix, "d_w_in_gates"), _tn_matmul(dfox, nmix, "d_w_in_fox"),
                        _tn_matmul(dlat, nmix, "d_w_in_lat"))
    grads["w_in"] = jnp.pad(dwin_t.reshape(N_DEV, W_IN_PIECE, D_MODEL), ((0, 0), (0, W_IN_PIECE_PAD - W_IN_PIECE), (0, 0))
                            ).reshape(N_DEV * W_IN_PIECE_PAD, D_MODEL)
    grad_group_b = [n for n in group_b if n != "b_gate"]
    (dx0, dg_ffn1, da1, db1, h1, n1, dyh1), slots_b = _ffn_bwd(dx1, xs, W["ffn1_norm"], a1, b1, G["ffn1_w_gate"], G["ffn1_w_up"],
                                                              G["ffn1_w_down"], "ffn1_bwd", exchange=chip_sums(grad_group_b, "b"))
    slots.update(zip(grad_group_b, slots_b))
    grads["ffn1_w_gate"] = _tn_matmul(da1, n1, "ffn1_dgate")
    grads["ffn1_w_up"], got = _tn_matmul(db1, n1, "ffn1_dup", exchange=chip_sums(["ffn1_w_gate"], "a_gate"))
    slots["ffn1_w_gate"] = got[0]
    grads["ffn1_w_down"], got = _tn_matmul(h1, dyh1, "ffn1_ddown", exchange=chip_sums(["ffn1_w_up"], "a_up"))
    slots["ffn1_w_up"] = got[0]
    slots["ffn1_w_down"] = chip_sums(["ffn1_w_down"], "a_down").standalone("grads_ici_a_down")[0]

    small_parts = [dg_ffn1, dg_mix, dg_ffn2] + list(d_prep_small) + [dbg.reshape(1, 2 * D_MODEL), loss_vec]
    flat = jnp.concatenate([p.reshape(-1) for p in small_parts])
    n_flat = flat.shape[0]
    rows = -(-n_flat // (8 * 128)) * 8
    total = _all_reduce_small(jnp.pad(flat, (0, rows * 128 - n_flat)).reshape(rows, 128)).reshape(-1)
    offs, small_g = 0, {}
    for n in _SMALL:
        small_g[n] = total[offs:offs + W[n].shape[1]].reshape(W[n].shape)
        offs += W[n].shape[1]
    bg_full = total[offs:offs + 2 * D_MODEL].reshape(2, D_MODEL)
    offs += 2 * D_MODEL
    loss = (0.5 / D_MODEL) * jnp.sum(total[offs:offs + D_MODEL])
    _, _, _, me = _my_position()
    small_g["b_gate"] = lax.dynamic_slice_in_dim(bg_full, me * (D_MODEL // N_DEV), D_MODEL // N_DEV, axis=1)

    transposed_in_memory = ["ffn1_w_gate", "ffn1_w_up", "ffn2_w_gate", "ffn2_w_up", "w_in"]
    res = {}
    for n in _WEIGHTS:
        if n in small_g:
            res[n] = (small_g[n],) + tuple(_adamw(small_g[n], W[n], M[n], V[n], "adamw_" + n))
        elif n in transposed_in_memory:
            res[n] = tuple(o.T for o in _reduce_adamw(slots[n], W[n].T, M[n].T, V[n].T, transpose=False, name="adamw_" + n))
        else:
            res[n] = tuple(_reduce_adamw(slots[n], W[n], M[n], V[n], transpose=n in col_split, name="adamw_" + n))
    outs = [loss, dx0[None]]
    for k in range(4):
        outs += [res[n][k] if n in _SMALL else res[n][k][None] for n in _WEIGHTS]
    return tuple(outs)
```

```python
import functools

import jax
import jax.numpy as jnp
from jax import lax
from jax.experimental import pallas as pl
from jax.experimental.pallas import tpu as pltpu

F32 = jnp.float32
BF16 = jnp.bfloat16

D_MODEL = 1024
FFN_HIDDEN = 2816
N_HEADS = 8
HEAD_DIM = 64
ROPE_DIM = 32
ROPE_HALF = 16
Q_LORA = 192
KV_LORA = 128
ROPE_THETA = 10000.0
RMS_EPS = 1e-6
MLA_SCALE = (HEAD_DIM + ROPE_DIM) ** -0.5
FOX_SCALE = HEAD_DIM ** -0.5
MLA_QK_LANES = 128
ADAM_LR, ADAM_B1, ADAM_B2, ADAM_EPS, ADAM_WD, ADAM_STEP = 0.001, 0.9, 0.999, 1e-08, 0.01, 10
N_DEV = 8
N_CHIP = 4
W_IN_PIECE = 493
BF16_ROW_TILE = 16
W_IN_PIECE_PAD = 496
LAT_W = 512
LAT_Q, LAT_KV, LAT_KR, LAT_F = 0, 256, 384, 416
MASK_VALUE = -1e30
FIXED_SHIFT_MAX_BOUND = 30.0

TOK_TILE = 512
DW_TOK_TILE = 1024
PROJ_TILE = 512
PREP_TILE = 256
ATT_TILE = 512
ATT_COL_CHUNK = 256
FFN_HID_TILE = 1408
FFN_HID_SPLIT = ((0, 768), (768, 1408))

NT = (((1,), (1,)), ((), ()))
TN = (((0,), (0,)), ((), ()))
NN = (((1,), (0,)), ((), ()))
MESH = pl.DeviceIdType.MESH


def _dot(a, b, dims=NN):
    return lax.dot_general(a, b, dims, preferred_element_type=F32)


def _sds(shape, dtype):
    return jax.ShapeDtypeStruct(shape, dtype)


def _rms_fwd(x, g):
    r = lax.rsqrt(jnp.mean(x * x, axis=-1, keepdims=True) + RMS_EPS)
    return x * r * g, r


def _rms_bwd(dy, x, g, r):
    xn = x * r
    dyg = dy * g
    dx = r * (dyg - xn * jnp.mean(dyg * xn, axis=-1, keepdims=True))
    return dx, dy * xn


def _colsum(x):
    return jnp.sum(x, axis=0, keepdims=True)


def _full(shape):
    return pl.BlockSpec(shape, lambda *_: (0,) * len(shape))


def _tok(tm, n):
    return pl.BlockSpec((tm, n), lambda i, *_: (i, 0))


def _ffn_fwd(x, gain, wg_t, wu_t, wd, target=None, exchange=None):
    t = x.shape[0]
    tm = min(TOK_TILE, t)
    tf = FFN_HID_TILE
    n_t, n_f = t // tm, FFN_HIDDEN // tf
    with_loss = target is not None

    def body(*refs):
        if with_loss:
            x_ref, g_ref, wg_ref, wu_ref, wd_ref, t_ref, out_ref, a_ref, b_ref, lvec_ref, n_scr, acc = refs
        else:
            x_ref, g_ref, wg_ref, wu_ref, wd_ref, out_ref, a_ref, b_ref, n_scr, acc = refs
        i, j = pl.program_id(0), pl.program_id(1)

        @pl.when(j == 0)
        def _():
            xn, _ = _rms_fwd(x_ref[...], g_ref[...])
            n_scr[...] = xn.astype(BF16)
            acc[...] = jnp.zeros_like(acc)

        n = n_scr[...]
        a = _dot(n, wg_ref[...], NT)
        b = _dot(n, wu_ref[...], NT)
        a_ref[...] = a.astype(BF16)
        b_ref[...] = b.astype(BF16)
        h = (a * jax.nn.sigmoid(a)) * b
        acc[...] += _dot(h.astype(BF16), wd_ref[...])

        @pl.when(j == n_f - 1)
        def _():
            y = x_ref[...] + 0.5 * acc[...]
            if with_loss:
                diff = y - t_ref[...]
                out_ref[...] = diff * (1.0 / D_MODEL)
                sq = _colsum(diff * diff)

                @pl.when(i == 0)
                def _():
                    lvec_ref[...] = sq

                @pl.when(i > 0)
                def _():
                    lvec_ref[...] += sq
            else:
                out_ref[...] = y

    wspec = pl.BlockSpec((tf, D_MODEL), lambda i, j: (j, 0))
    hspec = pl.BlockSpec((tm, tf), lambda i, j: (i, j))
    in_specs = [_tok(tm, D_MODEL), _full((1, D_MODEL)), wspec, wspec, wspec]
    out_specs = [_tok(tm, D_MODEL), hspec, hspec]
    out_shape = [_sds((t, D_MODEL), F32), _sds((t, FFN_HIDDEN), BF16), _sds((t, FFN_HIDDEN), BF16)]
    args = [x, gain, wg_t, wu_t, wd]
    if with_loss:
        in_specs.append(_tok(tm, D_MODEL))
        out_specs.append(_full((1, D_MODEL)))
        out_shape.append(_sds((1, D_MODEL), F32))
        args.append(target)
    return _gridded_call(
        body, name="ffn_fwd_loss" if with_loss else "ffn_fwd", grid=(n_t, n_f), in_specs=in_specs, out_specs=out_specs,
        out_shape=out_shape, scratch_shapes=[pltpu.VMEM((tm, D_MODEL), BF16), pltpu.VMEM((tm, D_MODEL), F32)],
        args=args, exchange=exchange)


def _ffn_bwd(dy, x, gain, a, b, wg_t, wu_t, wd, name, exchange=None):
    t = x.shape[0]
    tm = min(TOK_TILE, t)
    tf = FFN_HID_TILE
    n_t, n_f = t // tm, FFN_HIDDEN // tf

    def body(dy_ref, x_ref, g_ref, a_ref, b_ref, wg_ref, wu_ref, wd_ref,
             dx_ref, dg_ref, da_ref, db_ref, h_ref, n_ref, dyh_ref, acc):
        i, j = pl.program_id(0), pl.program_id(1)

        @pl.when(j == 0)
        def _():
            xn, _ = _rms_fwd(x_ref[...], g_ref[...])
            n_ref[...] = xn.astype(BF16)
            dyh_ref[...] = (0.5 * dy_ref[...]).astype(BF16)
            acc[...] = jnp.zeros_like(acc)

        dyh = dyh_ref[...]
        for lo, hi in FFN_HID_SPLIT:
            dh = _dot(dyh, wd_ref[lo:hi, :], NT)
            av = a_ref[:, lo:hi].astype(F32)
            bv = b_ref[:, lo:hi].astype(F32)
            s = jax.nn.sigmoid(av)
            silu = av * s
            da = (dh * bv * (s * (1.0 + av * (1.0 - s)))).astype(BF16)
            db = (dh * silu).astype(BF16)
            da_ref[:, lo:hi] = da
            db_ref[:, lo:hi] = db
            h_ref[:, lo:hi] = (silu * bv).astype(BF16)
            acc[...] += _dot(da, wg_ref[lo:hi, :]) + _dot(db, wu_ref[lo:hi, :])

        @pl.when(j == n_f - 1)
        def _():
            xv, g = x_ref[...], g_ref[...]
            r = lax.rsqrt(jnp.mean(xv * xv, axis=-1, keepdims=True) + RMS_EPS)
            dx, dg_rows = _rms_bwd(acc[...], xv, g, r)
            dx_ref[...] = dy_ref[...] + dx
            dg = _colsum(dg_rows)

            @pl.when(i == 0)
            def _():
                dg_ref[...] = dg

            @pl.when(i > 0)
            def _():
                dg_ref[...] += dg

    wspec = pl.BlockSpec((tf, D_MODEL), lambda i, j: (j, 0))
    hspec = pl.BlockSpec((tm, tf), lambda i, j: (i, j))
    tok = _tok(tm, D_MODEL)
    hid = _sds((t, FFN_HIDDEN), BF16)
    return _gridded_call(
        body, name=name, grid=(n_t, n_f),
        in_specs=[tok, tok, _full((1, D_MODEL)), hspec, hspec, wspec, wspec, wspec],
        out_specs=[tok, _full((1, D_MODEL)), hspec, hspec, hspec, tok, tok],
        out_shape=[_sds((t, D_MODEL), F32), _sds((1, D_MODEL), F32), hid, hid, hid,
                   _sds((t, D_MODEL), BF16), _sds((t, D_MODEL), BF16)],
        scratch_shapes=[pltpu.VMEM((tm, D_MODEL), F32)], args=[dy, x, gain, a, b, wg_t, wu_t, wd], exchange=exchange)


def _tn_matmul(a, b, name, exchange=None):
    t, m = a.shape
    n = b.shape[1]
    tk = min(DW_TOK_TILE, t)
    n_k = t // tk

    def body(a_ref, b_ref, o_ref, acc):
        k = pl.program_id(0)
        p = _dot(a_ref[...], b_ref[...], TN)

        @pl.when(k == 0)
        def _():
            acc[...] = p

        @pl.when(k > 0)
        def _():
            acc[...] += p

        @pl.when(k == n_k - 1)
        def _():
            o_ref[...] = acc[...].astype(BF16)

    (out,), got = _gridded_call(
        body, name=name, grid=(n_k,), in_specs=[_tok(tk, m), _tok(tk, n)], out_specs=[_full((m, n))],
        out_shape=[_sds((m, n), BF16)], scratch_shapes=[pltpu.VMEM((m, n), F32)], args=[a, b], exchange=exchange)
    return out if exchange is None else (out, got)


def _proj_fwd(x, gain, wgate_t, wfox_t, wlat_t):
    t = x.shape[0]
    tm = min(PROJ_TILE, t)

    def body(x_ref, g_ref, wg_ref, wf_ref, wl_ref, og_ref, of_ref, ol_ref):
        xn, _ = _rms_fwd(x_ref[...], g_ref[...])
        n = xn.astype(BF16)
        og_ref[...] = _dot(n, wg_ref[...], NT)
        of_ref[...] = _dot(n, wf_ref[...], NT)
        ol_ref[...] = _dot(n, wl_ref[...], NT)

    return pl.pallas_call(
        body, name="proj_fwd", grid=(t // tm,),
        in_specs=[_tok(tm, D_MODEL), _full((1, D_MODEL)), _full(wgate_t.shape), _full(wfox_t.shape), _full(wlat_t.shape)],
        out_specs=[_tok(tm, 2 * D_MODEL), _tok(tm, 3 * 512), _tok(tm, LAT_W)],
        out_shape=[_sds((t, 2 * D_MODEL), F32), _sds((t, 3 * 512), F32), _sds((t, LAT_W), F32)],
        compiler_params=pltpu.CompilerParams(dimension_semantics=("arbitrary",)),
    )(x, gain, wgate_t, wfox_t, wlat_t)


def _proj_bwd(dgates, dfox, dlat, wgate_t, wfox_t, wlat_t, x, gain, dres):
    t = x.shape[0]
    tm = min(PROJ_TILE, t)

    def body(dg_ref, df_ref, dl_ref, wg_ref, wf_ref, wl_ref, x_ref, g_ref, dres_ref, dx_ref, dgain_ref, n_ref):
        i = pl.program_id(0)
        dn = _dot(dg_ref[...], wg_ref[...]) + _dot(df_ref[...], wf_ref[...]) + _dot(dl_ref[...], wl_ref[...])
        xv, g = x_ref[...], g_ref[...]
        xn, r = _rms_fwd(xv, g)
        n_ref[...] = xn.astype(BF16)
        dx, dg_rows = _rms_bwd(dn, xv, g, r)
        dx_ref[...] = dres_ref[...] + dx
        dgn = _colsum(dg_rows)

        @pl.when(i == 0)
        def _():
            dgain_ref[...] = dgn

        @pl.when(i > 0)
        def _():
            dgain_ref[...] += dgn

    tok = _tok(tm, D_MODEL)
    return pl.pallas_call(
        body, name="proj_bwd", grid=(t // tm,),
        in_specs=[_tok(tm, 2 * D_MODEL), _tok(tm, 3 * 512), _tok(tm, LAT_W), _full(wgate_t.shape), _full(wfox_t.shape),
                  _full(wlat_t.shape), tok, _full((1, D_MODEL)), tok],
        out_specs=[tok, _full((1, D_MODEL)), tok],
        out_shape=[_sds((t, D_MODEL), F32), _sds((1, D_MODEL), F32), _sds((t, D_MODEL), BF16)],
        compiler_params=pltpu.CompilerParams(dimension_semantics=("arbitrary",)),
    )(dgates, dfox, dlat, wgate_t, wfox_t, wlat_t, x, gain, dres)


def _tri(n, lower):
    r = lax.broadcasted_iota(jnp.int32, (n, n), 0)
    c = lax.broadcasted_iota(jnp.int32, (n, n), 1)
    return ((c <= r) if lower else (c >= r)).astype(F32)


def _log_sigmoid(z):
    return jnp.minimum(z, 0.0) - jnp.log1p(jnp.exp(-jnp.abs(z)))


N_GROUPS = 16
LANES = MLA_QK_LANES


def _split_dot(a, g):
    hi = a.astype(BF16)
    lo = (a - hi.astype(F32)).astype(BF16)
    return _dot(hi, g) + _dot(lo, g)


def _rope_fwd(y, c, s1, s2):
    return y * c + pltpu.roll(y, LANES - ROPE_HALF, 1) * s1 + pltpu.roll(y, ROPE_HALF, 1) * s2


def _rope_bwd(do, c, s1, s2):
    return do * c + pltpu.roll(do * s1, ROPE_HALF, 1) + pltpu.roll(do * s2, LANES - ROPE_HALF, 1)


def _prep_tables(cos, sin):
    t = cos.shape[0]
    z = lambda n: jnp.zeros((t, n), F32)
    c = jnp.concatenate([jnp.ones((t, HEAD_DIM), F32), cos, cos, z(LANES - HEAD_DIM - ROPE_DIM)], axis=1)
    s1 = jnp.concatenate([z(HEAD_DIM), -sin, z(LANES - HEAD_DIM - ROPE_HALF)], axis=1)
    s2 = jnp.concatenate([z(HEAD_DIM + ROPE_HALF), sin, z(LANES - HEAD_DIM - ROPE_DIM)], axis=1)
    lane = jnp.arange(LANES)
    rope_mask = ((lane >= HEAD_DIM) & (lane < HEAD_DIM + ROPE_DIM)).astype(F32)[None, :]
    return c, s1, s2, rope_mask


def _group_matrices():
    lane = jnp.arange(N_HEADS * LANES)
    head, d = (lane // LANES)[:, None], (lane % LANES)[:, None]
    col = jnp.arange(N_GROUPS)[None, :]
    g_mla = ((col == head) & (d < HEAD_DIM)) | ((col == N_HEADS + head) & (d >= HEAD_DIM) & (d < HEAD_DIM + ROPE_DIM))
    g_fox = col == (jnp.arange(N_HEADS * HEAD_DIM) // HEAD_DIM)[:, None]
    inv_mla = jnp.concatenate([jnp.full((1, N_HEADS), 1.0 / HEAD_DIM, F32), jnp.full((1, N_HEADS), 1.0 / ROPE_DIM, F32)], axis=1)
    inv_fox = jnp.full((1, N_GROUPS), 1.0 / HEAD_DIM, F32)
    return g_mla.astype(BF16), g_mla.T.astype(BF16), inv_mla, g_fox.astype(BF16), g_fox.T.astype(BF16), inv_fox


def _interleave_weights(wqb_t, wkvb_t):
    wq = jnp.pad(wqb_t.reshape(N_HEADS, HEAD_DIM + ROPE_DIM, Q_LORA), ((0, 0), (0, LANES - HEAD_DIM - ROPE_DIM), (0, 0)))
    kv = wkvb_t.reshape(N_HEADS, 2, HEAD_DIM, KV_LORA)
    wk = jnp.pad(kv[:, 0], ((0, 0), (0, LANES - HEAD_DIM), (0, 0)))
    return wq.reshape(N_HEADS * LANES, Q_LORA), wk.reshape(N_HEADS * LANES, KV_LORA), kv[:, 1].reshape(N_HEADS * HEAD_DIM, KV_LORA)


def _deinterleave_grads(dwq, dwk, dwv):
    dq = dwq.reshape(N_HEADS, LANES, Q_LORA)[:, :HEAD_DIM + ROPE_DIM].reshape(N_HEADS * (HEAD_DIM + ROPE_DIM), Q_LORA)
    dk = dwk.reshape(N_HEADS, LANES, KV_LORA)[:, :HEAD_DIM]
    dkv = jnp.stack([dk, dwv.reshape(N_HEADS, HEAD_DIM, KV_LORA)], axis=1)
    return dq, dkv.reshape(N_HEADS * 2 * HEAD_DIM, KV_LORA)


def _head_gains(g_qn, g_qr, g_kn, g_kr, g_fq, g_fk):
    z = lambda n: jnp.zeros((1, n), F32)
    gq = jnp.concatenate([g_qn, g_qr, z(LANES - HEAD_DIM - ROPE_DIM)], axis=1)
    gk = jnp.concatenate([g_kn, z(LANES - HEAD_DIM)], axis=1)
    gkr = jnp.concatenate([z(HEAD_DIM), g_kr, z(LANES - HEAD_DIM - ROPE_DIM)], axis=1)
    return jnp.tile(gq, (1, N_HEADS)), jnp.tile(gk, (1, N_HEADS)), gkr, jnp.tile(g_fq, (1, N_HEADS)), jnp.tile(g_fk, (1, N_HEADS))


def _group_rms(x, g, g_t, inv):
    r = lax.rsqrt(_split_dot(x * x, g) * inv + RMS_EPS)
    return _split_dot(r, g_t)


def _prep_fwd(fox, lat, tables, mats, wq, wk, wv, g_qlat, g_kvlat, gains, b_f):
    t = fox.shape[0]
    tm = min(PREP_TILE, t)
    c_tab, s1_tab, s2_tab, rope_mask = tables

    def body(fox_ref, lat_ref, c_ref, s1_ref, s2_ref, rm_ref, gm_ref, gmt_ref, im_ref, gf_ref, gft_ref, if_ref,
             wq_ref, wk_ref, wv_ref, gql_ref, gkvl_ref, gq_ref, gk_ref, gkr_ref, gfq_ref, gfk_ref, bf_ref,
             fq_ref, fk_ref, fv_ref, cc_ref, mq_ref, mk_ref, mv_ref, fqn_ref, fkmax_ref, mqn_ref, mkmax_ref, carry):
        i = pl.program_id(0)
        ct, s1, s2 = c_ref[...], s1_ref[...], s2_ref[...]

        @pl.when(i == 0)
        def _():
            carry[...] = jnp.zeros_like(carry)
            fkmax_ref[...] = jnp.zeros_like(fkmax_ref)
            mkmax_ref[...] = jnp.zeros_like(mkmax_ref)

        def square_sums(ref, g):
            v = ref[...].astype(F32)
            return _split_dot(v * v, g)

        xq = fox_ref[:, 0:512]
        fq_ref[...] = (xq * _group_rms(xq, gf_ref[...], gft_ref[...], if_ref[...]) * (gfq_ref[...] * FOX_SCALE)).astype(BF16)
        xk = fox_ref[:, 512:1024]
        fk_ref[...] = (xk * _group_rms(xk, gf_ref[...], gft_ref[...], if_ref[...]) * gfk_ref[...]).astype(BF16)
        fv_ref[...] = fox_ref[:, 1024:1536].astype(BF16)
        fqn_ref[...] = square_sums(fq_ref, gf_ref[...])
        fkmax_ref[...] = jnp.maximum(fkmax_ref[...], jnp.max(square_sums(fk_ref, gf_ref[...]), axis=0, keepdims=True))


        logf = _log_sigmoid(lat_ref[:, LAT_F:LAT_F + N_HEADS] + bf_ref[...])
        cc_ref[...] = jnp.dot(_tri(tm, True), logf, precision=lax.Precision.HIGHEST, preferred_element_type=F32) + carry[...]
        carry[...] += _colsum(logf)

        qlat_n, _ = _rms_fwd(lat_ref[:, LAT_Q:LAT_Q + Q_LORA], gql_ref[...])
        p = _dot(qlat_n.astype(BF16), wq_ref[...], NT)
        y = p * _group_rms(p, gm_ref[...], gmt_ref[...], im_ref[...]) * (gq_ref[...] * MLA_SCALE)
        for h in range(N_HEADS):
            w = slice(h * LANES, (h + 1) * LANES)
            mq_ref[:, w] = _rope_fwd(y[:, w], ct, s1, s2).astype(BF16)

        kv_n, _ = _rms_fwd(lat_ref[:, LAT_KV:LAT_KV + KV_LORA], gkvl_ref[...])
        kv_b = kv_n.astype(BF16)
        pk = _dot(kv_b, wk_ref[...], NT)
        kn = pk * _group_rms(pk, gm_ref[...], gmt_ref[...], im_ref[...]) * gk_ref[...]
        rm = rm_ref[...]
        kr = pltpu.roll(lat_ref[:, LAT_KR:LAT_KR + LANES], HEAD_DIM, 1) * rm
        rr = lax.rsqrt(jnp.sum(kr * kr, axis=1, keepdims=True) * (1.0 / ROPE_DIM) + RMS_EPS)
        okr = _rope_fwd(kr * rr * gkr_ref[...], ct * rm, s1, s2)
        for h in range(N_HEADS):
            w = slice(h * LANES, (h + 1) * LANES)
            mk_ref[:, w] = (kn[:, w] + okr).astype(BF16)
        mv_ref[...] = _dot(kv_b, wv_ref[...], NT).astype(BF16)
        mqn_ref[...] = square_sums(mq_ref, gm_ref[...])
        mkmax_ref[...] = jnp.maximum(mkmax_ref[...], jnp.max(square_sums(mk_ref, gm_ref[...]), axis=0, keepdims=True))

    consts = [rope_mask, *mats, wq, wk, wv, g_qlat, g_kvlat, *gains, b_f]
    t512, tl, tg, g1 = _tok(tm, 512), _tok(tm, LANES), _tok(tm, N_GROUPS), _full((1, N_GROUPS))
    return pl.pallas_call(
        body, name="prep_fwd", grid=(t // tm,),
        in_specs=[_tok(tm, 1536), _tok(tm, LAT_W), tl, tl, tl] + [_full(a.shape) for a in consts],
        out_specs=[t512, t512, t512, _tok(tm, N_HEADS), _tok(tm, N_HEADS * LANES), _tok(tm, N_HEADS * LANES), t512,
                   tg, g1, tg, g1],
        out_shape=[_sds((t, 512), BF16), _sds((t, 512), BF16), _sds((t, 512), BF16), _sds((t, N_HEADS), F32),
                   _sds((t, N_HEADS * LANES), BF16), _sds((t, N_HEADS * LANES), BF16), _sds((t, 512), BF16),
                   _sds((t, N_GROUPS), F32), _sds((1, N_GROUPS), F32), _sds((t, N_GROUPS), F32), _sds((1, N_GROUPS), F32)],
        scratch_shapes=[pltpu.VMEM((1, N_HEADS), F32)],
        compiler_params=pltpu.CompilerParams(dimension_semantics=("arbitrary",)),
    )(fox, lat, c_tab, s1_tab, s2_tab, *consts)


def _prep_bwd(fox, lat, tables, mats, wq, wk, wv, g_qlat, g_kvlat, gains, b_f, dfq, dfk, dfv, dc, dmq, dmk, dmv):
    t = fox.shape[0]
    tm = min(PREP_TILE, t)
    n_t = t // tm
    c_tab, s1_tab, s2_tab, rope_mask = tables

    def body(fox_ref, lat_ref, c_ref, s1_ref, s2_ref, rm_ref, gm_ref, gmt_ref, im_ref, gf_ref, gft_ref, if_ref,
             wq_ref, wk_ref, wv_ref, gql_ref, gkvl_ref, gq_ref, gk_ref, gkr_ref, gfq_ref, gfk_ref, bf_ref,
             dfq_ref, dfk_ref, dfv_ref, dc_ref, dmq_ref, dmk_ref, dmv_ref,
             dfox_ref, dlat_ref, dwq_ref, dwk_ref, dwv_ref, o_gql, o_gkvl, o_gq, o_gk, o_gkr, o_gfq, o_gfk, o_bf,
             carry, lscr, wq_acc, wk_acc, wv_acc, dyscr):
        i = pl.program_id(0)
        ct, s1, s2, rm = c_ref[...], s1_ref[...], s2_ref[...], rm_ref[...]

        @pl.when(i == 0)
        def _():
            for ref in [carry, wq_acc, wk_acc, wv_acc, o_gql, o_gkvl, o_gq, o_gk, o_gkr, o_gfq, o_gfk, o_bf]:
                ref[...] = jnp.zeros_like(ref)

        def group_rms_bwd(dy, x, r, gain, g, g_t, inv):
            xn = x * r
            dyg = dy * gain
            mu = _split_dot(_split_dot(dyg * xn, g) * inv, g_t)
            return r * (dyg - xn * mu), dy * xn

        gf, gft, invf = gf_ref[...], gft_ref[...], if_ref[...]
        xq = fox_ref[:, 0:512]
        dxq, rows = group_rms_bwd(dfq_ref[...] * FOX_SCALE, xq, _group_rms(xq, gf, gft, invf), gfq_ref[...], gf, gft, invf)
        dfox_ref[:, 0:512] = dxq.astype(BF16)
        o_gfq[...] += _colsum(rows)
        xk = fox_ref[:, 512:1024]
        dxk, rows = group_rms_bwd(dfk_ref[...], xk, _group_rms(xk, gf, gft, invf), gfk_ref[...], gf, gft, invf)
        dfox_ref[:, 512:1024] = dxk.astype(BF16)
        o_gfk[...] += _colsum(rows)
        dfox_ref[:, 1024:1536] = dfv_ref[...].astype(BF16)

        lscr[...] = jnp.zeros_like(lscr)

        xql = lat_ref[:, LAT_Q:LAT_Q + Q_LORA]
        qlat_n, r_ql = _rms_fwd(xql, gql_ref[...])
        qlat_b = qlat_n.astype(BF16)
        gm, gmt, invm = gm_ref[...], gmt_ref[...], im_ref[...]
        p = _dot(qlat_b, wq_ref[...], NT)
        for h in range(N_HEADS):
            w = slice(h * LANES, (h + 1) * LANES)
            dyscr[:, w] = _rope_bwd(dmq_ref[:, w], ct, s1, s2) * MLA_SCALE
        dp, rows = group_rms_bwd(dyscr[...], p, _group_rms(p, gm, gmt, invm), gq_ref[...], gm, gmt, invm)
        o_gq[...] += _colsum(rows)
        dp_b = dp.astype(BF16)
        wq_acc[...] += _dot(dp_b, qlat_b, TN)
        dxql, rows = _rms_bwd(_dot(dp_b, wq_ref[...]), xql, gql_ref[...], r_ql)
        lscr[:, LAT_Q:LAT_Q + Q_LORA] = dxql
        o_gql[...] += _colsum(rows)

        xkv = lat_ref[:, LAT_KV:LAT_KV + KV_LORA]
        kv_n, r_kv = _rms_fwd(xkv, gkvl_ref[...])
        kv_b = kv_n.astype(BF16)
        pk = _dot(kv_b, wk_ref[...], NT)
        dpk, rows = group_rms_bwd(dmk_ref[...], pk, _group_rms(pk, gm, gmt, invm), gk_ref[...], gm, gmt, invm)
        o_gk[...] += _colsum(rows)
        dpk_b = dpk.astype(BF16)
        dv_b = dmv_ref[...].astype(BF16)
        wk_acc[...] += _dot(dpk_b, kv_b, TN)
        wv_acc[...] += _dot(dv_b, kv_b, TN)
        dxkv, rows = _rms_bwd(_dot(dpk_b, wk_ref[...]) + _dot(dv_b, wv_ref[...]), xkv, gkvl_ref[...], r_kv)
        lscr[:, LAT_KV:LAT_KV + KV_LORA] = dxkv
        o_gkvl[...] += _colsum(rows)

        dokr = dmk_ref[:, 0:LANES]
        for h in range(1, N_HEADS):
            dokr = dokr + dmk_ref[:, h * LANES:(h + 1) * LANES]
        dykr = _rope_bwd(dokr * rm, ct * rm, s1, s2)
        kr = pltpu.roll(lat_ref[:, LAT_KR:LAT_KR + LANES], HEAD_DIM, 1) * rm
        rr = lax.rsqrt(jnp.sum(kr * kr, axis=1, keepdims=True) * (1.0 / ROPE_DIM) + RMS_EPS)
        krn = kr * rr
        dyg = dykr * gkr_ref[...]
        dkr = rr * (dyg - krn * (jnp.sum(dyg * krn, axis=1, keepdims=True) * (1.0 / ROPE_DIM)))
        o_gkr[...] += _colsum(dykr * krn)
        lscr[:, LAT_KR:LAT_KR + LANES] = pltpu.roll(dkr, LANES - HEAD_DIM, 1)

        dcv = dc_ref[...]
        dlogf = jnp.dot(_tri(tm, False), dcv, precision=lax.Precision.HIGHEST, preferred_element_type=F32) + carry[...]
        carry[...] += _colsum(dcv)
        dz = dlogf * jax.nn.sigmoid(-(lat_ref[:, LAT_F:LAT_F + N_HEADS] + bf_ref[...]))
        lscr[:, LAT_F:LAT_F + N_HEADS] = dz
        o_bf[...] += _colsum(dz)

        dlat_ref[...] = lscr[...].astype(BF16)

        @pl.when(i == n_t - 1)
        def _():
            dwq_ref[...] = wq_acc[...].astype(BF16)
            dwk_ref[...] = wk_acc[...].astype(BF16)
            dwv_ref[...] = wv_acc[...].astype(BF16)

    consts = [rope_mask, *mats, wq, wk, wv, g_qlat, g_kvlat, *gains, b_f]

    def rtok(n):
        return pl.BlockSpec((tm, n), lambda i: (n_t - 1 - i, 0))

    sums = [(1, Q_LORA), (1, KV_LORA), (1, N_HEADS * LANES), (1, N_HEADS * LANES), (1, LANES), (1, 512), (1, 512), (1, N_HEADS)]
    return pl.pallas_call(
        body, name="prep_bwd", grid=(n_t,),
        in_specs=[rtok(1536), rtok(LAT_W), rtok(LANES), rtok(LANES), rtok(LANES)] + [_full(a.shape) for a in consts]
        + [rtok(512), rtok(512), rtok(512), rtok(N_HEADS), rtok(N_HEADS * LANES), rtok(N_HEADS * LANES), rtok(512)],
        out_specs=[rtok(1536), rtok(LAT_W), _full(wq.shape), _full(wk.shape), _full(wv.shape)] + [_full(s) for s in sums],
        out_shape=[_sds((t, 1536), BF16), _sds((t, LAT_W), BF16), _sds(wq.shape, BF16), _sds(wk.shape, BF16), _sds(wv.shape, BF16)]
        + [_sds(s, F32) for s in sums],
        scratch_shapes=[pltpu.VMEM((1, N_HEADS), F32), pltpu.VMEM((tm, LAT_W), F32), pltpu.VMEM(wq.shape, F32),
                        pltpu.VMEM(wk.shape, F32), pltpu.VMEM(wv.shape, F32), pltpu.VMEM((tm, N_HEADS * LANES), F32)],
        compiler_params=pltpu.CompilerParams(dimension_semantics=("arbitrary",)),
    )(fox, lat, c_tab, s1_tab, s2_tab, *consts, dfq, dfk, dfv, dc, dmq, dmk, dmv)


def _causal_pairs(n, query_major):
    pairs = [(i, j) for i in range(n) for j in range(i + 1)] if query_major else [(i, j) for j in range(n) for i in range(j, n)]
    return jnp.asarray([p[0] for p in pairs], jnp.int32), jnp.asarray([p[1] for p in pairs], jnp.int32)


def _logit_bound(q_sq, k_sq_max):
    qn = q_sq[:, :N_HEADS] + q_sq[:, N_HEADS:]
    kmax = k_sq_max[:, :N_HEADS] + k_sq_max[:, N_HEADS:]
    bound = jnp.sqrt(qn * kmax) * (1.0 + 2.0 ** -10) + 2.0 ** -10
    flag = (jnp.max(bound) <= FIXED_SHIFT_MAX_BOUND).astype(F32).reshape(1, 1)
    return bound, flag


def _with_ones_column(v):
    t = v.shape[0]
    heads = v.reshape(t, N_HEADS, HEAD_DIM)
    pad = jnp.zeros((t, N_HEADS, LANES - HEAD_DIM - 1), v.dtype)
    return jnp.concatenate([heads, jnp.ones((t, N_HEADS, 1), v.dtype), pad], axis=2).reshape(t, N_HEADS * LANES)


def _attn_fwd(q, k, v, bound, fixed_ok, c, c_t, *, lanes, name, exchange=None):
    t = q.shape[0]
    tq = min(ATT_TILE, t)
    n_q = t // tq
    hd = HEAD_DIM
    ch = min(ATT_COL_CHUNK, tq)
    decay = c is not None

    def body(qi_ref, kj_ref, *refs):
        if decay:
            q_ref, k_ref, v_ref, b_ref, ok_ref, c_ref, ct_ref, o_ref, o32_ref, lse_ref, m_scr, acc = refs
        else:
            q_ref, k_ref, v_ref, b_ref, ok_ref, o_ref, o32_ref, lse_ref, m_scr, acc = refs
        i, j = qi_ref[pl.program_id(0)], kj_ref[pl.program_id(0)]
        fixed = ok_ref[0, 0] > 0.5

        @pl.when(j == 0)
        def _():
            m_scr[...] = jnp.full_like(m_scr, MASK_VALUE)
            acc[...] = jnp.zeros_like(acc)

        def fixed_step(diagonal):
            for h in range(N_HEADS):
                wl = slice(h * lanes, (h + 1) * lanes)
                wv = slice(h * LANES, (h + 1) * LANES)
                qh = q_ref[:, wl]
                row = (c_ref[:, h:h + 1] - b_ref[:, h:h + 1]) if decay else -b_ref[:, h:h + 1]
                o_hi = jnp.zeros((tq, LANES), F32)
                o_lo = jnp.zeros((tq, LANES), F32)
                for cc in range(tq // ch):
                    cols = slice(cc * ch, (cc + 1) * ch)
                    s = _dot(qh, k_ref[cols, wl], NT)
                    s = s + ((row - ct_ref[h:h + 1, cols]) if decay else row)
                    if diagonal:
                        keep = (lax.broadcasted_iota(jnp.int32, (tq, ch), 0)
                                >= lax.broadcasted_iota(jnp.int32, (tq, ch), 1) + cc * ch)
                        s = jnp.where(keep, s, MASK_VALUE)
                    p = jnp.exp(s)
                    p_b = p.astype(BF16)
                    o_hi = o_hi + _dot(p_b, v_ref[cols, wv])
                    if decay:
                        o_lo = o_lo + _dot((p - p_b.astype(F32)).astype(BF16), v_ref[cols, wv])
                acc[0, :, wv] += o_hi
                if decay:
                    acc[1, :, wv] += o_lo

        def step(diagonal):
            if diagonal:
                keep = lax.broadcasted_iota(jnp.int32, (tq, tq), 0) >= lax.broadcasted_iota(jnp.int32, (tq, tq), 1)
            for h in range(N_HEADS):
                s = _dot(q_ref[:, h * lanes:(h + 1) * lanes], k_ref[:, h * lanes:(h + 1) * lanes], NT)
                if decay:
                    s = s + (c_ref[:, h:h + 1] - ct_ref[h:h + 1, :])
                if diagonal:
                    s = jnp.where(keep, s, MASK_VALUE)
                m_prev = m_scr[h]
                m_new = jnp.maximum(m_prev, jnp.max(s, axis=1, keepdims=True))
                alpha = jnp.exp(m_prev - m_new)
                p = jnp.exp(s - m_new)
                wv = slice(h * LANES, (h + 1) * LANES)
                p_b = p.astype(BF16)
                acc[0, :, wv] = alpha * acc[0, :, wv] + _dot(p_b, v_ref[:, wv])
                if decay:
                    acc[1, :, wv] = alpha * acc[1, :, wv] + _dot((p - p_b.astype(F32)).astype(BF16), v_ref[:, wv])
                m_scr[h] = m_new

        for diagonal, here in ((False, j < i), (True, j == i)):
            @pl.when(here & fixed)
            def _():
                fixed_step(diagonal)

            @pl.when(here & jnp.logical_not(fixed))
            def _():
                step(diagonal)

        @pl.when(j == i)
        def _():
            for h in range(N_HEADS):
                w = slice(h * hd, (h + 1) * hd)
                val, one = slice(h * LANES, h * LANES + hd), slice(h * LANES + hd, h * LANES + hd + 1)
                l = (acc[0, :, one] + acc[1, :, one]) if decay else acc[0, :, one]
                o_ref[:, w] = (acc[0, :, val] / l).astype(BF16)
                o32_ref[:, w] = ((acc[0, :, val] + acc[1, :, val]) if decay else acc[0, :, val]) / l
                lse_ref[:, h:h + 1] = jnp.where(fixed, b_ref[:, h:h + 1], m_scr[h]) + jnp.log(l)

    qspec = lambda n: pl.BlockSpec((tq, n), lambda s, qi, kj: (qi[s], 0))
    kspec = lambda n: pl.BlockSpec((tq, n), lambda s, qi, kj: (kj[s], 0))
    in_specs = [qspec(N_HEADS * lanes), kspec(N_HEADS * lanes), kspec(N_HEADS * LANES), qspec(N_HEADS),
                pl.BlockSpec(memory_space=pltpu.SMEM)]
    args = [q, k, _with_ones_column(v), bound, fixed_ok]
    if decay:
        in_specs += [qspec(N_HEADS), pl.BlockSpec((N_HEADS, tq), lambda s, qi, kj: (0, kj[s]))]
        args += [c, c_t]
    pairs = _causal_pairs(n_q, query_major=True)
    return _gridded_call(
        body, name=name, grid=(pairs[0].shape[0],), in_specs=in_specs, out_specs=[qspec(512), qspec(512), qspec(N_HEADS)],
        out_shape=[_sds((t, 512), BF16), _sds((t, 512), F32), _sds((t, N_HEADS), F32)],
        scratch_shapes=[pltpu.VMEM((N_HEADS, tq, 1), F32), pltpu.VMEM((2, tq, N_HEADS * LANES), F32)],
        args=args, exchange=exchange, prefetch=pairs)


def _attn_bwd(q, k, v, do, delta, lse, c, c_t, *, lanes, name, exchange=None):
    t = q.shape[0]
    tq = min(ATT_TILE, t)
    n_q = t // tq
    hd = HEAD_DIM
    decay = c is not None

    pairs = _causal_pairs(n_q, query_major=False)
    n_steps = pairs[0].shape[0]

    def body(qi_ref, kj_ref, *refs):
        if decay:
            q_ref, k_ref, v_ref, do_ref, delta_ref, lse_ref, c_ref, ct_ref, dq_hbm, dk_ref, dv_ref, dct_ref, dq_ref = refs
        else:
            q_ref, k_ref, v_ref, do_ref, delta_ref, lse_ref, dq_hbm, dk_ref, dv_ref, dq_ref = refs
        step_id = pl.program_id(0)
        i, j = qi_ref[step_id], kj_ref[step_id]

        @pl.when(step_id == 0)
        def _():
            dq_ref[...] = jnp.zeros_like(dq_ref)

        @pl.when(i == j)
        def _():
            dk_ref[...] = jnp.zeros_like(dk_ref)
            dv_ref[...] = jnp.zeros_like(dv_ref)
            if decay:
                dct_ref[...] = jnp.zeros_like(dct_ref)

        def step(diagonal):
            if diagonal:
                keep = lax.broadcasted_iota(jnp.int32, (tq, tq), 0) >= lax.broadcasted_iota(jnp.int32, (tq, tq), 1)
            rows = pl.ds(pl.multiple_of(i * tq, tq), tq)
            for h in range(N_HEADS):
                wl = slice(h * lanes, (h + 1) * lanes)
                w = slice(h * hd, (h + 1) * hd)
                qh, kh = q_ref[:, wl], k_ref[:, wl]
                s = _dot(qh, kh, NT)
                if decay:
                    s = s + (c_ref[:, h:h + 1] - ct_ref[h:h + 1, :])
                if diagonal:
                    s = jnp.where(keep, s, MASK_VALUE)
                p = jnp.exp(s - lse_ref[:, h:h + 1])
                doh = do_ref[:, w]
                dv_ref[:, w] += _dot(p.astype(BF16), doh, TN)
                dp = _dot(doh, v_ref[:, w], NT)
                ds = p * (dp - delta_ref[:, h:h + 1])
                if decay:
                    dct_ref[h:h + 1, :] -= _colsum(ds)
                ds_b = ds.astype(BF16)
                dk_ref[:, wl] += _dot(ds_b, qh, TN)
                dq_ref[rows, wl] += _dot(ds_b, kh)

        @pl.when(i > j)
        def _():
            step(False)

        @pl.when(i == j)
        def _():
            step(True)

        @pl.when(step_id == n_steps - 1)
        def _():
            pltpu.sync_copy(dq_ref, dq_hbm)

    qspec = lambda n: pl.BlockSpec((tq, n), lambda s, qi, kj: (qi[s], 0))
    kspec = lambda n: pl.BlockSpec((tq, n), lambda s, qi, kj: (kj[s], 0))
    in_specs = [qspec(N_HEADS * lanes), kspec(N_HEADS * lanes), kspec(512), qspec(512), qspec(N_HEADS), qspec(N_HEADS)]
    out_specs = [pl.BlockSpec(memory_space=pl.ANY), kspec(N_HEADS * lanes), kspec(512)]
    out_shape = [_sds((t, N_HEADS * lanes), F32), _sds((t, N_HEADS * lanes), F32), _sds((t, 512), F32)]
    args = [q, k, v, do, delta, lse]
    if decay:
        ctspec = pl.BlockSpec((N_HEADS, tq), lambda s, qi, kj: (0, kj[s]))
        in_specs += [qspec(N_HEADS), ctspec]
        out_specs.append(ctspec)
        out_shape.append(_sds((N_HEADS, t), F32))
        args += [c, c_t]
    return _gridded_call(body, name=name, grid=(n_steps,), in_specs=in_specs, out_specs=out_specs, out_shape=out_shape,
                         scratch_shapes=[pltpu.VMEM((t, N_HEADS * lanes), F32)], args=args, exchange=exchange, prefetch=pairs)


def _mix_fwd(x, y_mla, y_fox, gates, b_gate, wbm_t, wbf_t, wo):
    t = x.shape[0]
    tm = min(PROJ_TILE, t)

    def body(x_ref, ym_ref, yf_ref, gt_ref, bg_ref, wbm_ref, wbf_ref, wo_ref, out_ref):
        um = _dot(ym_ref[...], wbm_ref[...], NT)
        uf = _dot(yf_ref[...], wbf_ref[...], NT)
        sm = jax.nn.sigmoid(gt_ref[:, 0:D_MODEL] + bg_ref[0:1, :])
        sf = jax.nn.sigmoid(gt_ref[:, D_MODEL:2 * D_MODEL] + bg_ref[1:2, :])
        mixed = sm * um + sf * uf
        out_ref[...] = x_ref[...] + _dot(mixed.astype(BF16), wo_ref[...])

    tok = _tok(tm, D_MODEL)
    return pl.pallas_call(
        body, name="mix_fwd", grid=(t // tm,),
        in_specs=[tok, _tok(tm, 512), _tok(tm, 512), _tok(tm, 2 * D_MODEL), _full((2, D_MODEL)), _full(wbm_t.shape),
                  _full(wbf_t.shape), _full(wo.shape)],
        out_specs=tok, out_shape=_sds((t, D_MODEL), F32),
        compiler_params=pltpu.CompilerParams(dimension_semantics=("arbitrary",)),
    )(x, y_mla, y_fox, gates, b_gate, wbm_t, wbf_t, wo)


def _mix_bwd(dx, y_mla, y_fox, y_mla32, y_fox32, gates, b_gate, wbm_t, wbf_t, wo, exchange=None):
    t = dx.shape[0]
    tm = min(PROJ_TILE, t)

    def body(dx_ref, ym_ref, yf_ref, ym32_ref, yf32_ref, gt_ref, bg_ref, wbm_ref, wbf_ref, wo_ref,
             dym_ref, dyf_ref, dlm_ref, dlf_ref, dgt_ref, mixed_ref, dum_ref, duf_ref, dxb_ref, dbg_ref, prod):
        i = pl.program_id(0)
        dxb = dx_ref[...].astype(BF16)
        dxb_ref[...] = dxb
        dmixed = _dot(dxb, wo_ref[...], NT)
        um = _dot(ym_ref[...], wbm_ref[...], NT)
        uf = _dot(yf_ref[...], wbf_ref[...], NT)
        sm = jax.nn.sigmoid(gt_ref[:, 0:D_MODEL] + bg_ref[0:1, :])
        sf = jax.nn.sigmoid(gt_ref[:, D_MODEL:2 * D_MODEL] + bg_ref[1:2, :])
        mixed_ref[...] = (sm * um + sf * uf).astype(BF16)
        dum = (dmixed * sm).astype(BF16)
        duf = (dmixed * sf).astype(BF16)
        dum_ref[...] = dum
        duf_ref[...] = duf
        dgm = dmixed * um * (sm * (1.0 - sm))
        dgf = dmixed * uf * (sf * (1.0 - sf))
        dgt_ref[:, 0:D_MODEL] = dgm.astype(BF16)
        dgt_ref[:, D_MODEL:2 * D_MODEL] = dgf.astype(BF16)
        for du, wb_ref, y32_ref, dy_ref, dl_ref in ((dum, wbm_ref, ym32_ref, dym_ref, dlm_ref),
                                                    (duf, wbf_ref, yf32_ref, dyf_ref, dlf_ref)):
            dy = _dot(du, wb_ref[...])
            dy_ref[...] = dy.astype(BF16)
            prod[...] = dy.astype(BF16).astype(F32) * y32_ref[...]
            for h in range(N_HEADS):
                dl_ref[:, h:h + 1] = jnp.sum(prod[:, h * HEAD_DIM:(h + 1) * HEAD_DIM], axis=1, keepdims=True)

        @pl.when(i == 0)
        def _():
            dbg_ref[...] = jnp.zeros_like(dbg_ref)

        dbg_ref[0:1, :] += _colsum(dgm)
        dbg_ref[1:2, :] += _colsum(dgf)

    tok = _tok(tm, D_MODEL)
    tokb = _sds((t, D_MODEL), BF16)
    t512, t8 = _tok(tm, 512), _tok(tm, N_HEADS)
    return _gridded_call(
        body, name="mix_bwd", grid=(t // tm,),
        in_specs=[tok, t512, t512, t512, t512, _tok(tm, 2 * D_MODEL), _full((2, D_MODEL)), _full(wbm_t.shape),
                  _full(wbf_t.shape), _full(wo.shape)],
        out_specs=[t512, t512, t8, t8, _tok(tm, 2 * D_MODEL), tok, tok, tok, tok, _full((2, D_MODEL))],
        out_shape=[_sds((t, 512), BF16), _sds((t, 512), BF16), _sds((t, N_HEADS), F32), _sds((t, N_HEADS), F32),
                   _sds((t, 2 * D_MODEL), BF16), tokb, tokb, tokb, tokb, _sds((2, D_MODEL), F32)],
        scratch_shapes=[pltpu.VMEM((tm, 512), F32)],
        args=[dx, y_mla, y_fox, y_mla32, y_fox32, gates, b_gate, wbm_t, wbf_t, wo], exchange=exchange)


def _my_position():
    x, y, c = lax.axis_index("x"), lax.axis_index("y"), lax.axis_index("c")
    return x, y, c, 4 * x + 2 * y + c


def _peer(x, y, c, mask):
    px = 1 - x if mask & 4 else x
    py = 1 - y if mask & 2 else y
    pc = 1 - c if mask & 1 else c
    return (px, py, pc), 4 * px + 2 * py + pc


def _chip_peer(x, y, km):
    px = 1 - x if km & 2 else x
    py = 1 - y if km & 1 else y
    return px, py, 2 * px + py


_HBM = pl.BlockSpec(memory_space=pl.ANY)


def _wait_all(copies):
    for cp in copies:
        cp.wait()


class _ChipExchange:
    def __init__(self, gather, arrays):
        self.gather, self.arrays, self.aliased = gather, list(arrays), False
        n = len(self.arrays)
        self.out_shape = [_sds((N_DEV * a.shape[0],) + a.shape[1:], a.dtype) if gather else _sds(a.shape, a.dtype)
                          for a in self.arrays]
        self.scratch_shapes = [pltpu.SemaphoreType.DMA((n, N_CHIP)), pltpu.SemaphoreType.DMA((n, N_CHIP)),
                               pltpu.SemaphoreType.DMA((n,))]

    def copies(self, srcs, dsts, send_sems, recv_sems, local_sems):
        x, y, c, me = _my_position()
        q_me = 2 * x + y
        out = []
        for a in range(len(self.arrays)):
            if self.gather:
                r = srcs[a].shape[0]
                local_src, dst = srcs[a], dsts[a].at[pl.ds(me * r, r)]
            else:
                local_src, dst = srcs[a].at[q_me], dsts[a].at[q_me]
            out.append(pltpu.make_async_copy(local_src, dst, local_sems.at[a]))
            for km in range(1, N_CHIP):
                px, py, q_peer = _chip_peer(x, y, km)
                out.append(pltpu.make_async_remote_copy(
                    src_ref=srcs[a] if self.gather else srcs[a].at[q_peer], dst_ref=dst, send_sem=send_sems.at[a, km],
                    recv_sem=recv_sems.at[a, km], device_id=(px, py, c), device_id_type=MESH))
        return out

    def standalone(self, name):
        n = len(self.arrays)

        def body(*refs):
            copies = self.copies(refs[:n], refs[n:2 * n], *refs[2 * n:])
            for cp in copies:
                cp.start()
            _wait_all(copies)

        return pl.pallas_call(body, name=name, in_specs=[_HBM] * n, out_specs=[_HBM] * n, out_shape=self.out_shape,
                              scratch_shapes=self.scratch_shapes)(*self.arrays)


def _gridded_call(body, *, name, grid, in_specs, out_specs, out_shape, scratch_shapes, args, exchange=None, prefetch=()):
    params = pltpu.CompilerParams(dimension_semantics=("arbitrary",) * len(grid))
    n_pre, n_in, n_out, n_scr = len(prefetch), len(in_specs), len(out_specs), len(scratch_shapes)
    n_x = 0 if exchange is None else len(exchange.arrays)

    def carrier(*refs):
        pre, refs = refs[:n_pre], refs[n_pre:]
        ins, x_src, refs = refs[:n_in], refs[n_in:n_in + n_x], refs[n_in + n_x:]
        outs, x_dst, refs = refs[:n_out], refs[n_out:n_out + n_x], refs[n_out + n_x:]
        copies = exchange.copies(x_src, x_dst, *refs[n_scr:])
        pids = [pl.program_id(d) for d in range(len(grid))]
        first = functools.reduce(jnp.logical_and, [p == 0 for p in pids])
        last = functools.reduce(jnp.logical_and, [p == g - 1 for p, g in zip(pids, grid)])

        @pl.when(first)
        def _():
            for cp in copies:
                cp.start()

        body(*pre, *ins, *outs, *refs[:n_scr])

        @pl.when(last)
        def _():
            _wait_all(copies)

    x_shapes, x_scratch, x_args = ([], [], []) if exchange is None else (exchange.out_shape, exchange.scratch_shapes, exchange.arrays)
    spec = pltpu.PrefetchScalarGridSpec(
        num_scalar_prefetch=n_pre, grid=grid, in_specs=list(in_specs) + [_HBM] * n_x, out_specs=list(out_specs) + [_HBM] * n_x,
        scratch_shapes=list(scratch_shapes) + x_scratch)
    in_place = {n_pre + n_in + k: n_out + k for k in range(n_x)} if n_x and exchange.aliased else {}
    res = pl.pallas_call(body if exchange is None else carrier, name=name, grid_spec=spec, out_shape=list(out_shape) + x_shapes,
                         input_output_aliases=in_place, compiler_params=params)(*prefetch, *args, *x_args)
    return res[:n_out], (None if exchange is None else res[n_out:])


class _CoreExchange:
    def __init__(self, gather, arrays):
        self.gather, self.arrays, self.aliased = gather, list(arrays), gather
        n = len(self.arrays)
        self.out_shape = [_sds(a.shape, a.dtype) if gather else _sds((N_CHIP, a.shape[0] // N_DEV) + a.shape[1:], a.dtype)
                          for a in self.arrays]
        self.scratch_shapes = [pltpu.SemaphoreType.DMA((n, N_CHIP)), pltpu.SemaphoreType.DMA((n, N_CHIP))]

    def copies(self, srcs, dsts, send_sems, recv_sems):
        x, y, c, _ = _my_position()
        out = []
        for a in range(len(self.arrays)):
            r = srcs[a].shape[0] // N_DEV
            for q in range(N_CHIP):
                if self.gather:
                    rows = pl.ds((2 * q + c) * r, r)
                    src, dst = srcs[a].at[rows], dsts[a].at[rows]
                else:
                    src, dst = srcs[a].at[pl.ds((2 * q + 1 - c) * r, r)], dsts[a].at[q]
                out.append(pltpu.make_async_remote_copy(src_ref=src, dst_ref=dst, send_sem=send_sems.at[a, q],
                                                        recv_sem=recv_sems.at[a, q], device_id=(x, y, 1 - c), device_id_type=MESH))
        return out

    def standalone(self, name):
        n = len(self.arrays)

        def body(*refs):
            copies = self.copies(refs[:n], refs[n:2 * n], *refs[2 * n:])
            for cp in copies:
                cp.start()
            _wait_all(copies)

        return pl.pallas_call(body, name=name, in_specs=[_HBM] * n, out_specs=[_HBM] * n, out_shape=self.out_shape,
                              input_output_aliases={a: a for a in range(n)} if self.aliased else {},
                              scratch_shapes=self.scratch_shapes)(*self.arrays)


def _gather_over_cores(arrays, name):
    return _CoreExchange(True, arrays).standalone(name)


def _grads_to_sibling(grads, name):
    return _CoreExchange(False, grads).standalone(name)


def _pair_sum(grad, from_sibling, name):
    r, n = from_sibling.shape[1:]

    def body(g_ref, s_ref, o_ref):
        c = lax.axis_index("c")
        o_ref[...] = (g_ref[c].astype(F32) + s_ref[...].astype(F32)).astype(BF16)

    return pl.pallas_call(
        body, name=name, grid=(N_CHIP,),
        in_specs=[pl.BlockSpec((None, 2, r, n), lambda q: (q, 0, 0, 0)), pl.BlockSpec((None, r, n), lambda q: (q, 0, 0))],
        out_specs=pl.BlockSpec((None, r, n), lambda q: (q, 0, 0)), out_shape=_sds((N_CHIP, r, n), BF16),
    )(grad.reshape(N_CHIP, 2, r, n), from_sibling)


def _all_reduce_small(vec):
    r = vec.shape[0]

    def body(v_ref, o_ref, buf, send_sems, recv_sems):
        x, y, c, me = _my_position()
        buf[me] = v_ref[...]
        copies = []
        for mask in range(1, N_DEV):
            peer, _ = _peer(x, y, c, mask)
            cp = pltpu.make_async_remote_copy(src_ref=v_ref, dst_ref=buf.at[me], send_sem=send_sems.at[mask],
                                              recv_sem=recv_sems.at[mask], device_id=peer, device_id_type=MESH)
            cp.start()
            copies.append(cp)
        for cp in copies:
            cp.wait()
        total = buf[0]
        for s in range(1, N_DEV):
            total = total + buf[s]
        o_ref[...] = total

    vm = pl.BlockSpec(memory_space=pltpu.VMEM)
    return pl.pallas_call(
        body, name="all_reduce_small", in_specs=[vm], out_specs=vm, out_shape=_sds(vec.shape, F32),
        scratch_shapes=[pltpu.VMEM((N_DEV, r, 128), F32), pltpu.SemaphoreType.DMA((N_DEV,)), pltpu.SemaphoreType.DMA((N_DEV,))],
    )(vec)


def _adamw_math(w, g, m, v):
    m = ADAM_B1 * m + (1.0 - ADAM_B1) * g
    v = ADAM_B2 * v + (1.0 - ADAM_B2) * (g * g)
    m_hat = m / (1.0 - ADAM_B1 ** ADAM_STEP)
    v_hat = v / (1.0 - ADAM_B2 ** ADAM_STEP)
    delta = -ADAM_LR * (m_hat / (jnp.sqrt(v_hat) + ADAM_EPS) + ADAM_WD * w)
    return delta, m, v


def _reduce_adamw(slots, w, m, v, *, transpose, name):
    r, n = slots.shape[1:]
    padded = not transpose and w.shape[0] != r

    def body(s_ref, w_ref, m_ref, v_ref, g_ref, d_ref, nm_ref, nv_ref, *scr):
        g = s_ref[0].astype(F32)
        for s in range(1, slots.shape[0]):
            g = g + s_ref[s].astype(F32)
        if transpose:
            scr[0][...] = g.T
            g = scr[0][:, 0:w_ref.shape[1]]
        elif padded:
            scr[0][...] = g
            g = scr[0][0:w_ref.shape[0], :]
        g_ref[...] = g
        d_ref[...], nm_ref[...], nv_ref[...] = _adamw_math(w_ref[...], g, m_ref[...], v_ref[...])

    out = _sds(w.shape, F32)
    if not transpose and not padded and r % (2 * BF16_ROW_TILE) == 0:
        half = pl.BlockSpec((r // 2, n), lambda i: (i, 0))
        return pl.pallas_call(
            body, name=name, grid=(2,), in_specs=[pl.BlockSpec((slots.shape[0], r // 2, n), lambda i: (0, i, 0)), half, half, half],
            out_specs=[half] * 4, out_shape=[out, out, out, out])(slots, w, m, v)
    return pl.pallas_call(
        body, name=name, out_shape=[out, out, out, out],
        scratch_shapes=[pltpu.VMEM((n, r) if transpose else (r, n), F32)] if transpose or padded else [],
    )(slots, w, m, v)


def _adamw(g, w, m, v, name):
    def body(g_ref, w_ref, m_ref, v_ref, d_ref, nm_ref, nv_ref):
        d_ref[...], nm_ref[...], nv_ref[...] = _adamw_math(w_ref[...], g_ref[...], m_ref[...], v_ref[...])

    out = _sds(w.shape, F32)
    return pl.pallas_call(body, name=name, out_shape=[out, out, out])(g, w, m, v)


_SMALL = ["ffn1_norm", "mix_norm", "ffn2_norm", "mla_q_lat_norm", "mla_kv_lat_norm", "mla_q_nope_gain", "mla_q_rope_gain",
          "mla_k_nope_gain", "mla_k_rope_gain", "fox_q_gain", "fox_k_gain", "fox_b_f"]
_WEIGHTS = ["ffn1_norm", "ffn1_w_gate", "ffn1_w_up", "ffn1_w_down", "mix_norm", "w_in", "mla_q_lat_norm", "mla_w_qb",
            "mla_kv_lat_norm", "mla_w_kvb", "mla_q_nope_gain", "mla_q_rope_gain", "mla_k_nope_gain", "mla_k_rope_gain",
            "fox_q_gain", "fox_k_gain", "fox_b_f", "w_branch_mla", "w_branch_fox", "b_gate", "w_o", "ffn2_norm",
            "ffn2_w_gate", "ffn2_w_up", "ffn2_w_down"]
_IN_Q, _IN_KV, _IN_KR, _IN_FOX, _IN_F, _IN_GATES = (0, 192), (192, 128), (320, 32), (352, 1536), (1888, 8), (1896, 2048)


def _rows(a, seg):
    return a[seg[0]:seg[0] + seg[1]]


def _split_w_in(win_t):
    z = lambda n: jnp.zeros((n, D_MODEL), win_t.dtype)
    lat = jnp.concatenate([_rows(win_t, _IN_Q), z(LAT_KV - Q_LORA), _rows(win_t, _IN_KV), _rows(win_t, _IN_KR),
                           _rows(win_t, _IN_F), z(LAT_W - LAT_F - N_HEADS)], axis=0)
    return _rows(win_t, _IN_GATES), _rows(win_t, _IN_FOX), lat


def _join_w_in(d_gates, d_fox, d_lat):
    return jnp.concatenate([d_lat[LAT_Q:LAT_Q + Q_LORA], d_lat[LAT_KV:LAT_KV + KV_LORA], d_lat[LAT_KR:LAT_KR + ROPE_DIM],
                            d_fox, d_lat[LAT_F:LAT_F + N_HEADS], d_gates], axis=0)


def kernel(x, positions, ffn1_norm, ffn1_w_gate, ffn1_w_up, ffn1_w_down, mix_norm, w_in, mla_q_lat_norm, mla_w_qb, mla_kv_lat_norm, mla_w_kvb, mla_q_nope_gain, mla_q_rope_gain, mla_k_nope_gain, mla_k_rope_gain, fox_q_gain, fox_k_gain, fox_b_f, w_branch_mla, w_branch_fox, b_gate, w_o, ffn2_norm, ffn2_w_gate, ffn2_w_up, ffn2_w_down, loss_target, m_ffn1_norm, m_ffn1_w_gate, m_ffn1_w_up, m_ffn1_w_down, m_mix_norm, m_w_in, m_mla_q_lat_norm, m_mla_w_qb, m_mla_kv_lat_norm, m_mla_w_kvb, m_mla_q_nope_gain, m_mla_q_rope_gain, m_mla_k_nope_gain, m_mla_k_rope_gain, m_fox_q_gain, m_fox_k_gain, m_fox_b_f, m_w_branch_mla, m_w_branch_fox, m_b_gate, m_w_o, m_ffn2_norm, m_ffn2_w_gate, m_ffn2_w_up, m_ffn2_w_down, v_ffn1_norm, v_ffn1_w_gate, v_ffn1_w_up, v_ffn1_w_down, v_mix_norm, v_w_in, v_mla_q_lat_norm, v_mla_w_qb, v_mla_kv_lat_norm, v_mla_w_kvb, v_mla_q_nope_gain, v_mla_q_rope_gain, v_mla_k_nope_gain, v_mla_k_rope_gain, v_fox_q_gain, v_fox_k_gain, v_fox_b_f, v_w_branch_mla, v_w_branch_fox, v_b_gate, v_w_o, v_ffn2_norm, v_ffn2_w_gate, v_ffn2_w_up, v_ffn2_w_down):
    env = dict(locals())
    strip = lambda n, a: a if n in _SMALL else a[0]
    W = {n: strip(n, env[n]) for n in _WEIGHTS}
    M = {n: strip(n, env["m_" + n]) for n in _WEIGHTS}
    V = {n: strip(n, env["v_" + n]) for n in _WEIGHTS}
    xs = x[0]
    t = xs.shape[0]

    col_split = ["ffn1_w_gate", "ffn1_w_up", "ffn2_w_gate", "ffn2_w_up", "mla_w_qb", "mla_w_kvb", "w_branch_mla", "w_branch_fox"]
    row_split = ["ffn1_w_down", "ffn2_w_down", "w_o"]
    pieces = {n: W[n].T.astype(BF16) for n in col_split}
    pieces.update({n: W[n].astype(BF16) for n in row_split})
    pieces["w_in"] = jnp.pad(W["w_in"].T.astype(BF16), ((0, W_IN_PIECE_PAD - W_IN_PIECE), (0, 0)))
    pieces["b_gate"] = W["b_gate"].T
    group_a = ["ffn1_w_gate", "ffn1_w_up", "ffn1_w_down"]
    group_b = ["w_in", "mla_w_qb", "mla_w_kvb", "w_branch_mla", "w_branch_fox", "w_o", "b_gate"]
    group_c = ["ffn2_w_gate", "ffn2_w_up", "ffn2_w_down"]
    gather = lambda group: _ChipExchange(True, [pieces[n] for n in group])
    G = dict(zip(group_a, _gather_over_cores(gather(group_a).standalone("gather_ici_a"), "gather_d2d_a")))

    inv_freq = ROPE_THETA ** (-jnp.arange(ROPE_HALF, dtype=F32) / ROPE_HALF)
    ang = positions[0].astype(F32)[:, None] * inv_freq
    tables, mats = _prep_tables(jnp.cos(ang), jnp.sin(ang)), _group_matrices()
    gains = _head_gains(*[W[n] for n in ["mla_q_nope_gain", "mla_q_rope_gain", "mla_k_nope_gain", "mla_k_rope_gain",
                                         "fox_q_gain", "fox_k_gain"]])

    (x1, a1, b1), got_b = _ffn_fwd(xs, W["ffn1_norm"], G["ffn1_w_gate"], G["ffn1_w_up"], G["ffn1_w_down"],
                                   exchange=gather(group_b))
    G.update(zip(group_b, _gather_over_cores(got_b, "gather_d2d_b")))
    win_t = G["w_in"].reshape(N_DEV, W_IN_PIECE_PAD, D_MODEL)[:, :W_IN_PIECE].reshape(N_DEV * W_IN_PIECE, D_MODEL)
    wgate_t, wfox_t, wlat_t = _split_w_in(win_t)
    bg = G["b_gate"].T
    gates, fox, lat = _proj_fwd(x1, W["mix_norm"], wgate_t, wfox_t, wlat_t)
    prep_args = (fox, lat, tables, mats, *_interleave_weights(G["mla_w_qb"], G["mla_w_kvb"]), W["mla_q_lat_norm"],
                 W["mla_kv_lat_norm"], gains, W["fox_b_f"])
    fq, fk, fv, c, mq, mk, mv, fq_sq, fk_sq_max, mq_sq, mk_sq_max = _prep_fwd(*prep_args)
    c_t = c.T
    b_fox, ok_fox = _logit_bound(fq_sq, fk_sq_max)
    b_mla, ok_mla = _logit_bound(mq_sq, mk_sq_max)
    (y_fox, y_fox32, lse_fox), got_c = _attn_fwd(fq, fk, fv, b_fox, ok_fox, c, c_t, lanes=HEAD_DIM, name="fox_fwd",
                                                 exchange=gather(group_c))
    (y_mla, y_mla32, lse_mla), got_c = _attn_fwd(mq, mk, mv, b_mla, ok_mla, None, None, lanes=MLA_QK_LANES, name="mla_fwd",
                                                 exchange=_CoreExchange(True, got_c))
    G.update(zip(group_c, got_c))
    x2 = _mix_fwd(x1, y_mla, y_fox, gates, bg, G["w_branch_mla"], G["w_branch_fox"], G["w_o"])
    (dx3, a2, b2, loss_vec), _ = _ffn_fwd(x2, W["ffn2_norm"], G["ffn2_w_gate"], G["ffn2_w_up"], G["ffn2_w_down"],
                                          target=loss_target[0])

    def chip_sums(group, tag, from_sibling=None):
        if from_sibling is None:
            from_sibling = _grads_to_sibling([grads[n] for n in group], "grads_d2d_" + tag)
        return _ChipExchange(False, [_pair_sum(grads[n], s, "pair_sum_" + n) for n, s in zip(group, from_sibling)])

    (dx2, dg_ffn2, da2, db2, h2, n2, dyh2), _ = _ffn_bwd(dx3, x2, W["ffn2_norm"], a2, b2, G["ffn2_w_gate"], G["ffn2_w_up"],
                                                        G["ffn2_w_down"], "ffn2_bwd")
    grads = {"ffn2_w_gate": _tn_matmul(da2, n2, "ffn2_dgate"), "ffn2_w_up": _tn_matmul(db2, n2, "ffn2_dup"),
             "ffn2_w_down": _tn_matmul(h2, dyh2, "ffn2_ddown")}
    (dy_mla, dy_fox, delta_mla, delta_fox, dgates, mixed, dum, duf, dx2b, dbg), from_sibling_c = _mix_bwd(
        dx2, y_mla, y_fox, y_mla32, y_fox32, gates, bg, G["w_branch_mla"], G["w_branch_fox"], G["w_o"],
        exchange=_CoreExchange(False, [grads[n] for n in group_c]))
    grads["w_o"] = _tn_matmul(mixed, dx2b, "d_w_o")
    grads["w_branch_mla"] = _tn_matmul(dum, y_mla, "d_w_branch_mla")
    grads["w_branch_fox"] = _tn_matmul(duf, y_fox, "d_w_branch_fox")
    (dfq, dfk, dfv, dc_t), slots_c = _attn_bwd(fq, fk, fv, dy_fox, delta_fox, lse_fox, c, c_t, lanes=HEAD_DIM,
                                               name="fox_bwd", exchange=chip_sums(group_c, "c", from_sibling_c))
    slots = dict(zip(group_c, slots_c))
    (dmq, dmk, dmv), _ = _attn_bwd(mq, mk, mv, dy_mla, delta_mla, lse_mla, None, None, lanes=MLA_QK_LANES, name="mla_bwd")
    dfox, dlat, dwq, dwk, dwv, d_gql, d_gkvl, d_gq, d_gk, d_gkr, d_gfq, d_gfk, d_bf = _prep_bwd(
        *prep_args, dfq, dfk, dfv, dc_t.T, dmq, dmk, dmv)
    grads["mla_w_qb"], grads["mla_w_kvb"] = _deinterleave_grads(dwq, dwk, dwv)
    fold = lambda a, width: a.reshape(N_HEADS, width).sum(axis=0)[None]
    d_gq, d_gk = fold(d_gq, LANES), fold(d_gk, LANES)
    d_prep_small = [d_gql, d_gkvl, d_gq[:, :HEAD_DIM], d_gq[:, HEAD_DIM:HEAD_DIM + ROPE_DIM], d_gk[:, :HEAD_DIM],
                    d_gkr[:, HEAD_DIM:HEAD_DIM + ROPE_DIM], fold(d_gfq, HEAD_DIM), fold(d_gfk, HEAD_DIM), d_bf]
    dx1, dg_mix, nmix = _proj_bwd(dgates, dfox, dlat, wgate_t, wfox_t, wlat_t, x1, W["mix_norm"], dx2)
    dwin_t = _join_w_in(_tn_matmul(dgates, nmix, "d_w_in_gates"), _tn_matmul(dfox, nmix, "d_w_in_fox"),
                        _tn_matmul(dlat, nmix, "d_w_in_lat"))
    grads["w_in"] = jnp.pad(dwin_t.reshape(N_DEV, W_IN_PIECE, D_MODEL), ((0, 0), (0, W_IN_PIECE_PAD - W_IN_PIECE), (0, 0))
                            ).reshape(N_DEV * W_IN_PIECE_PAD, D_MODEL)
    grad_group_b = [n for n in group_b if n != "b_gate"]
    (dx0, dg_ffn1, da1, db1, h1, n1, dyh1), slots_b = _ffn_bwd(dx1, xs, W["ffn1_norm"], a1, b1, G["ffn1_w_gate"], G["ffn1_w_up"],
                                                              G["ffn1_w_down"], "ffn1_bwd", exchange=chip_sums(grad_group_b, "b"))
    slots.update(zip(grad_group_b, slots_b))
    grads["ffn1_w_gate"] = _tn_matmul(da1, n1, "ffn1_dgate")
    grads["ffn1_w_up"], got = _tn_matmul(db1, n1, "ffn1_dup", exchange=chip_sums(["ffn1_w_gate"], "a_gate"))
    slots["ffn1_w_gate"] = got[0]
    grads["ffn1_w_down"], got = _tn_matmul(h1, dyh1, "ffn1_ddown", exchange=chip_sums(["ffn1_w_up"], "a_up"))
    slots["ffn1_w_up"] = got[0]
    slots["ffn1_w_down"] = chip_sums(["ffn1_w_down"], "a_down").standalone("grads_ici_a_down")[0]

    small_parts = [dg_ffn1, dg_mix, dg_ffn2] + list(d_prep_small) + [dbg.reshape(1, 2 * D_MODEL), loss_vec]
    flat = jnp.concatenate([p.reshape(-1) for p in small_parts])
    n_flat = flat.shape[0]
    rows = -(-n_flat // (8 * 128)) * 8
    total = _all_reduce_small(jnp.pad(flat, (0, rows * 128 - n_flat)).reshape(rows, 128)).reshape(-1)
    offs, small_g = 0, {}
    for n in _SMALL:
        small_g[n] = total[offs:offs + W[n].shape[1]].reshape(W[n].shape)
        offs += W[n].shape[1]
    bg_full = total[offs:offs + 2 * D_MODEL].reshape(2, D_MODEL)
    offs += 2 * D_MODEL
    loss = (0.5 / D_MODEL) * jnp.sum(total[offs:offs + D_MODEL])
    _, _, _, me = _my_position()
    small_g["b_gate"] = lax.dynamic_slice_in_dim(bg_full, me * (D_MODEL // N_DEV), D_MODEL // N_DEV, axis=1)

    transposed_in_memory = ["ffn1_w_gate", "ffn1_w_up", "ffn2_w_gate", "ffn2_w_up", "w_in"]
    res = {}
    for n in _WEIGHTS:
        if n in small_g:
            res[n] = (small_g[n],) + tuple(_adamw(small_g[n], W[n], M[n], V[n], "adamw_" + n))
        elif n in transposed_in_memory:
            res[n] = tuple(o.T for o in _reduce_adamw(slots[n], W[n].T, M[n].T, V[n].T, transpose=False, name="adamw_" + n))
        else:
            res[n] = tuple(_reduce_adamw(slots[n], W[n], M[n], V[n], transpose=n in col_split, name="adamw_" + n))
    outs = [loss, dx0[None]]
    for k in range(4):
        outs += [res[n][k] if n in _SMALL else res[n][k][None] for n in _WEIGHTS]
    return tuple(outs)
```

```python
import functools

import jax
import jax.numpy as jnp
from jax import lax
from jax.experimental import pallas as pl
from jax.experimental.pallas import tpu as pltpu

F32 = jnp.float32
BF16 = jnp.bfloat16

D_MODEL = 1024
FFN_HIDDEN = 2816
N_HEADS = 8
HEAD_DIM = 64
ROPE_DIM = 32
ROPE_HALF = 16
Q_LORA = 192
KV_LORA = 128
ROPE_THETA = 10000.0
RMS_EPS = 1e-6
MLA_SCALE = (HEAD_DIM + ROPE_DIM) ** -0.5
FOX_SCALE = HEAD_DIM ** -0.5
MLA_QK_LANES = 128
ADAM_LR, ADAM_B1, ADAM_B2, ADAM_EPS, ADAM_WD, ADAM_STEP = 0.001, 0.9, 0.999, 1e-08, 0.01, 10
N_DEV = 8
N_CHIP = 4
W_IN_PIECE = 493
BF16_ROW_TILE = 16
W_IN_PIECE_PAD = 496
LAT_W = 512
LAT_Q, LAT_KV, LAT_KR, LAT_F = 0, 256, 384, 416
MASK_VALUE = -1e30
FIXED_SHIFT_MAX_BOUND = 30.0

TOK_TILE = 512
DW_TOK_TILE = 1024
PROJ_TILE = 512
PREP_TILE = 256
ATT_TILE = 512
ATT_COL_CHUNK = 256
FFN_HID_TILE = 1408
FFN_HID_SPLIT = ((0, 768), (768, 1408))

NT = (((1,), (1,)), ((), ()))
TN = (((0,), (0,)), ((), ()))
NN = (((1,), (0,)), ((), ()))
MESH = pl.DeviceIdType.MESH


def _dot(a, b, dims=NN):
    return lax.dot_general(a, b, dims, preferred_element_type=F32)


def _sds(shape, dtype):
    return jax.ShapeDtypeStruct(shape, dtype)


def _rms_fwd(x, g):
    r = lax.rsqrt(jnp.mean(x * x, axis=-1, keepdims=True) + RMS_EPS)
    return x * r * g, r


def _rms_bwd(dy, x, g, r):
    xn = x * r
    dyg = dy * g
    dx = r * (dyg - xn * jnp.mean(dyg * xn, axis=-1, keepdims=True))
    return dx, dy * xn


def _colsum(x):
    return jnp.sum(x, axis=0, keepdims=True)


def _full(shape):
    return pl.BlockSpec(shape, lambda *_: (0,) * len(shape))


def _tok(tm, n):
    return pl.BlockSpec((tm, n), lambda i, *_: (i, 0))


def _ffn_fwd(x, gain, wg_t, wu_t, wd, target=None, exchange=None):
    t = x.shape[0]
    tm = min(TOK_TILE, t)
    tf = FFN_HID_TILE
    n_t, n_f = t // tm, FFN_HIDDEN // tf
    with_loss = target is not None

    def body(*refs):
        if with_loss:
            x_ref, g_ref, wg_ref, wu_ref, wd_ref, t_ref, out_ref, a_ref, b_ref, lvec_ref, n_scr, acc = refs
        else:
            x_ref, g_ref, wg_ref, wu_ref, wd_ref, out_ref, a_ref, b_ref, n_scr, acc = refs
        i, j = pl.program_id(0), pl.program_id(1)

        @pl.when(j == 0)
        def _():
            xn, _ = _rms_fwd(x_ref[...], g_ref[...])
            n_scr[...] = xn.astype(BF16)
            acc[...] = jnp.zeros_like(acc)

        n = n_scr[...]
        a = _dot(n, wg_ref[...], NT)
        b = _dot(n, wu_ref[...], NT)
        a_ref[...] = a.astype(BF16)
        b_ref[...] = b.astype(BF16)
        h = (a * jax.nn.sigmoid(a)) * b
        acc[...] += _dot(h.astype(BF16), wd_ref[...])

        @pl.when(j == n_f - 1)
        def _():
            y = x_ref[...] + 0.5 * acc[...]
            if with_loss:
                diff = y - t_ref[...]
                out_ref[...] = diff * (1.0 / D_MODEL)
                sq = _colsum(diff * diff)

                @pl.when(i == 0)
                def _():
                    lvec_ref[...] = sq

                @pl.when(i > 0)
                def _():
                    lvec_ref[...] += sq
            else:
                out_ref[...] = y

    wspec = pl.BlockSpec((tf, D_MODEL), lambda i, j: (j, 0))
    hspec = pl.BlockSpec((tm, tf), lambda i, j: (i, j))
    in_specs = [_tok(tm, D_MODEL), _full((1, D_MODEL)), wspec, wspec, wspec]
    out_specs = [_tok(tm, D_MODEL), hspec, hspec]
    out_shape = [_sds((t, D_MODEL), F32), _sds((t, FFN_HIDDEN), BF16), _sds((t, FFN_HIDDEN), BF16)]
    args = [x, gain, wg_t, wu_t, wd]
    if with_loss:
        in_specs.append(_tok(tm, D_MODEL))
        out_specs.append(_full((1, D_MODEL)))
        out_shape.append(_sds((1, D_MODEL), F32))
        args.append(target)
    return _gridded_call(
        body, name="ffn_fwd_loss" if with_loss else "ffn_fwd", grid=(n_t, n_f), in_specs=in_specs, out_specs=out_specs,
        out_shape=out_shape, scratch_shapes=[pltpu.VMEM((tm, D_MODEL), BF16), pltpu.VMEM((tm, D_MODEL), F32)],
        args=args, exchange=exchange)


def _ffn_bwd(dy, x, gain, a, b, wg_t, wu_t, wd, name, exchange=None):
    t = x.shape[0]
    tm = min(TOK_TILE, t)
    tf = FFN_HID_TILE
    n_t, n_f = t // tm, FFN_HIDDEN // tf

    def body(dy_ref, x_ref, g_ref, a_ref, b_ref, wg_ref, wu_ref, wd_ref,
             dx_ref, dg_ref, da_ref, db_ref, h_ref, n_ref, dyh_ref, acc):
        i, j = pl.program_id(0), pl.program_id(1)

        @pl.when(j == 0)
        def _():
            xn, _ = _rms_fwd(x_ref[...], g_ref[...])
            n_ref[...] = xn.astype(BF16)
            dyh_ref[...] = (0.5 * dy_ref[...]).astype(BF16)
            acc[...] = jnp.zeros_like(acc)

        dyh = dyh_ref[...]
        for lo, hi in FFN_HID_SPLIT:
            dh = _dot(dyh, wd_ref[lo:hi, :], NT)
            av = a_ref[:, lo:hi].astype(F32)
            bv = b_ref[:, lo:hi].astype(F32)
            s = jax.nn.sigmoid(av)
            silu = av * s
            da = (dh * bv * (s * (1.0 + av * (1.0 - s)))).astype(BF16)
            db = (dh * silu).astype(BF16)
            da_ref[:, lo:hi] = da
            db_ref[:, lo:hi] = db
            h_ref[:, lo:hi] = (silu * bv).astype(BF16)
            acc[...] += _dot(da, wg_ref[lo:hi, :]) + _dot(db, wu_ref[lo:hi, :])

        @pl.when(j == n_f - 1)
        def _():
            xv, g = x_ref[...], g_ref[...]
            r = lax.rsqrt(jnp.mean(xv * xv, axis=-1, keepdims=True) + RMS_EPS)
            dx, dg_rows = _rms_bwd(acc[...], xv, g, r)
            dx_ref[...] = dy_ref[...] + dx
            dg = _colsum(dg_rows)

            @pl.when(i == 0)
            def _():
                dg_ref[...] = dg

            @pl.when(i > 0)
            def _():
                dg_ref[...] += dg

    wspec = pl.BlockSpec((tf, D_MODEL), lambda i, j: (j, 0))
    hspec = pl.BlockSpec((tm, tf), lambda i, j: (i, j))
    tok = _tok(tm, D_MODEL)
    hid = _sds((t, FFN_HIDDEN), BF16)
    return _gridded_call(
        body, name=name, grid=(n_t, n_f),
        in_specs=[tok, tok, _full((1, D_MODEL)), hspec, hspec, wspec, wspec, wspec],
        out_specs=[tok, _full((1, D_MODEL)), hspec, hspec, hspec, tok, tok],
        out_shape=[_sds((t, D_MODEL), F32), _sds((1, D_MODEL), F32), hid, hid, hid,
                   _sds((t, D_MODEL), BF16), _sds((t, D_MODEL), BF16)],
        scratch_shapes=[pltpu.VMEM((tm, D_MODEL), F32)], args=[dy, x, gain, a, b, wg_t, wu_t, wd], exchange=exchange)


def _tn_matmul(a, b, name, exchange=None):
    t, m = a.shape
    n = b.shape[1]
    tk = min(DW_TOK_TILE, t)
    n_k = t // tk

    def body(a_ref, b_ref, o_ref, acc):
        k = pl.program_id(0)
        p = _dot(a_ref[...], b_ref[...], TN)

        @pl.when(k == 0)
        def _():
            acc[...] = p

        @pl.when(k > 0)
        def _():
            acc[...] += p

        @pl.when(k == n_k - 1)
        def _():
            o_ref[...] = acc[...].astype(BF16)

    (out,), got = _gridded_call(
        body, name=name, grid=(n_k,), in_specs=[_tok(tk, m), _tok(tk, n)], out_specs=[_full((m, n))],
        out_shape=[_sds((m, n), BF16)], scratch_shapes=[pltpu.VMEM((m, n), F32)], args=[a, b], exchange=exchange)
    return out if exchange is None else (out, got)


def _proj_fwd(x, gain, wgate_t, wfox_t, wlat_t):
    t = x.shape[0]
    tm = min(PROJ_TILE, t)

    def body(x_ref, g_ref, wg_ref, wf_ref, wl_ref, og_ref, of_ref, ol_ref):
        xn, _ = _rms_fwd(x_ref[...], g_ref[...])
        n = xn.astype(BF16)
        og_ref[...] = _dot(n, wg_ref[...], NT)
        of_ref[...] = _dot(n, wf_ref[...], NT)
        ol_ref[...] = _dot(n, wl_ref[...], NT)

    return pl.pallas_call(
        body, name="proj_fwd", grid=(t // tm,),
        in_specs=[_tok(tm, D_MODEL), _full((1, D_MODEL)), _full(wgate_t.shape), _full(wfox_t.shape), _full(wlat_t.shape)],
        out_specs=[_tok(tm, 2 * D_MODEL), _tok(tm, 3 * 512), _tok(tm, LAT_W)],
        out_shape=[_sds((t, 2 * D_MODEL), F32), _sds((t, 3 * 512), F32), _sds((t, LAT_W), F32)],
        compiler_params=pltpu.CompilerParams(dimension_semantics=("arbitrary",)),
    )(x, gain, wgate_t, wfox_t, wlat_t)


def _proj_bwd(dgates, dfox, dlat, wgate_t, wfox_t, wlat_t, x, gain, dres):
    t = x.shape[0]
    tm = min(PROJ_TILE, t)

    def body(dg_ref, df_ref, dl_ref, wg_ref, wf_ref, wl_ref, x_ref, g_ref, dres_ref, dx_ref, dgain_ref, n_ref):
        i = pl.program_id(0)
        dn = _dot(dg_ref[...], wg_ref[...]) + _dot(df_ref[...], wf_ref[...]) + _dot(dl_ref[...], wl_ref[...])
        xv, g = x_ref[...], g_ref[...]
        xn, r = _rms_fwd(xv, g)
        n_ref[...] = xn.astype(BF16)
        dx, dg_rows = _rms_bwd(dn, xv, g, r)
        dx_ref[...] = dres_ref[...] + dx
        dgn = _colsum(dg_rows)

        @pl.when(i == 0)
        def _():
            dgain_ref[...] = dgn

        @pl.when(i > 0)
        def _():
            dgain_ref[...] += dgn

    tok = _tok(tm, D_MODEL)
    return pl.pallas_call(
        body, name="proj_bwd", grid=(t // tm,),
        in_specs=[_tok(tm, 2 * D_MODEL), _tok(tm, 3 * 512), _tok(tm, LAT_W), _full(wgate_t.shape), _full(wfox_t.shape),
                  _full(wlat_t.shape), tok, _full((1, D_MODEL)), tok],
        out_specs=[tok, _full((1, D_MODEL)), tok],
        out_shape=[_sds((t, D_MODEL), F32), _sds((1, D_MODEL), F32), _sds((t, D_MODEL), BF16)],
        compiler_params=pltpu.CompilerParams(dimension_semantics=("arbitrary",)),
    )(dgates, dfox, dlat, wgate_t, wfox_t, wlat_t, x, gain, dres)


def _tri(n, lower):
    r = lax.broadcasted_iota(jnp.int32, (n, n), 0)
    c = lax.broadcasted_iota(jnp.int32, (n, n), 1)
    return ((c <= r) if lower else (c >= r)).astype(F32)


def _log_sigmoid(z):
    return jnp.minimum(z, 0.0) - jnp.log1p(jnp.exp(-jnp.abs(z)))


N_GROUPS = 16
LANES = MLA_QK_LANES


def _split_dot(a, g):
    hi = a.astype(BF16)
    lo = (a - hi.astype(F32)).astype(BF16)
    return _dot(hi, g) + _dot(lo, g)


def _rope_fwd(y, c, s1, s2):
    return y * c + pltpu.roll(y, LANES - ROPE_HALF, 1) * s1 + pltpu.roll(y, ROPE_HALF, 1) * s2


def _rope_bwd(do, c, s1, s2):
    return do * c + pltpu.roll(do * s1, ROPE_HALF, 1) + pltpu.roll(do * s2, LANES - ROPE_HALF, 1)


def _prep_tables(cos, sin):
    t = cos.shape[0]
    z = lambda n: jnp.zeros((t, n), F32)
    c = jnp.concatenate([jnp.ones((t, HEAD_DIM), F32), cos, cos, z(LANES - HEAD_DIM - ROPE_DIM)], axis=1)
    s1 = jnp.concatenate([z(HEAD_DIM), -sin, z(LANES - HEAD_DIM - ROPE_HALF)], axis=1)
    s2 = jnp.concatenate([z(HEAD_DIM + ROPE_HALF), sin, z(LANES - HEAD_DIM - ROPE_DIM)], axis=1)
    lane = jnp.arange(LANES)
    rope_mask = ((lane >= HEAD_DIM) & (lane < HEAD_DIM + ROPE_DIM)).astype(F32)[None, :]
    return c, s1, s2, rope_mask


def _group_matrices():
    lane = jnp.arange(N_HEADS * LANES)
    head, d = (lane // LANES)[:, None], (lane % LANES)[:, None]
    col = jnp.arange(N_GROUPS)[None, :]
    g_mla = ((col == head) & (d < HEAD_DIM)) | ((col == N_HEADS + head) & (d >= HEAD_DIM) & (d < HEAD_DIM + ROPE_DIM))
    g_fox = col == (jnp.arange(N_HEADS * HEAD_DIM) // HEAD_DIM)[:, None]
    inv_mla = jnp.concatenate([jnp.full((1, N_HEADS), 1.0 / HEAD_DIM, F32), jnp.full((1, N_HEADS), 1.0 / ROPE_DIM, F32)], axis=1)
    inv_fox = jnp.full((1, N_GROUPS), 1.0 / HEAD_DIM, F32)
    return g_mla.astype(BF16), g_mla.T.astype(BF16), inv_mla, g_fox.astype(BF16), g_fox.T.astype(BF16), inv_fox


def _interleave_weights(wqb_t, wkvb_t):
    wq = jnp.pad(wqb_t.reshape(N_HEADS, HEAD_DIM + ROPE_DIM, Q_LORA), ((0, 0), (0, LANES - HEAD_DIM - ROPE_DIM), (0, 0)))
    kv = wkvb_t.reshape(N_HEADS, 2, HEAD_DIM, KV_LORA)
    wk = jnp.pad(kv[:, 0], ((0, 0), (0, LANES - HEAD_DIM), (0, 0)))
    return wq.reshape(N_HEADS * LANES, Q_LORA), wk.reshape(N_HEADS * LANES, KV_LORA), kv[:, 1].reshape(N_HEADS * HEAD_DIM, KV_LORA)


def _deinterleave_grads(dwq, dwk, dwv):
    dq = dwq.reshape(N_HEADS, LANES, Q_LORA)[:, :HEAD_DIM + ROPE_DIM].reshape(N_HEADS * (HEAD_DIM + ROPE_DIM), Q_LORA)
    dk = dwk.reshape(N_HEADS, LANES, KV_LORA)[:, :HEAD_DIM]
    dkv = jnp.stack([dk, dwv.reshape(N_HEADS, HEAD_DIM, KV_LORA)], axis=1)
    return dq, dkv.reshape(N_HEADS * 2 * HEAD_DIM, KV_LORA)


def _head_gains(g_qn, g_qr, g_kn, g_kr, g_fq, g_fk):
    z = lambda n: jnp.zeros((1, n), F32)
    gq = jnp.concatenate([g_qn, g_qr, z(LANES - HEAD_DIM - ROPE_DIM)], axis=1)
    gk = jnp.concatenate([g_kn, z(LANES - HEAD_DIM)], axis=1)
    gkr = jnp.concatenate([z(HEAD_DIM), g_kr, z(LANES - HEAD_DIM - ROPE_DIM)], axis=1)
    return jnp.tile(gq, (1, N_HEADS)), jnp.tile(gk, (1, N_HEADS)), gkr, jnp.tile(g_fq, (1, N_HEADS)), jnp.tile(g_fk, (1, N_HEADS))


def _group_rms(x, g, g_t, inv):
    r = lax.rsqrt(_split_dot(x * x, g) * inv + RMS_EPS)
    return _split_dot(r, g_t)


def _value_layout(wv):
    lane = jnp.arange(N_HEADS * LANES)
    src = (lane // LANES) * HEAD_DIM + lane % LANES
    place = (jnp.arange(N_HEADS * HEAD_DIM)[:, None] == src[None, :]) & (lane % LANES < HEAD_DIM)[None, :]
    ones = (lane % LANES == HEAD_DIM).astype(F32)[None, :]
    wv_il = jnp.pad(wv.reshape(N_HEADS, HEAD_DIM, KV_LORA), ((0, 0), (0, LANES - HEAD_DIM), (0, 0)))
    return place.astype(BF16), ones, wv_il.reshape(N_HEADS * LANES, KV_LORA)


def _prep_fwd(fox, lat, tables, mats, wq, wk, wv, g_qlat, g_kvlat, gains, b_f, v_place, v_ones, wv_il):
    t = fox.shape[0]
    tm = min(PREP_TILE, t)
    c_tab, s1_tab, s2_tab, rope_mask = tables

    def body(fox_ref, lat_ref, c_ref, s1_ref, s2_ref, rm_ref, gm_ref, gmt_ref, im_ref, gf_ref, gft_ref, if_ref,
             wq_ref, wk_ref, wv_ref, gql_ref, gkvl_ref, gq_ref, gk_ref, gkr_ref, gfq_ref, gfk_ref, bf_ref,
             vp_ref, vo_ref, wvil_ref,
             fq_ref, fk_ref, fv_ref, cc_ref, mq_ref, mk_ref, mv_ref, fqn_ref, fkmax_ref, mqn_ref, mkmax_ref,
             fv1_ref, mv1_ref, carry):
        i = pl.program_id(0)
        ct, s1, s2 = c_ref[...], s1_ref[...], s2_ref[...]

        @pl.when(i == 0)
        def _():
            carry[...] = jnp.zeros_like(carry)
            fkmax_ref[...] = jnp.zeros_like(fkmax_ref)
            mkmax_ref[...] = jnp.zeros_like(mkmax_ref)

        def square_sums(ref, g):
            v = ref[...].astype(F32)
            return _split_dot(v * v, g)

        xq = fox_ref[:, 0:512]
        fq_ref[...] = (xq * _group_rms(xq, gf_ref[...], gft_ref[...], if_ref[...]) * (gfq_ref[...] * FOX_SCALE)).astype(BF16)
        xk = fox_ref[:, 512:1024]
        fk_ref[...] = (xk * _group_rms(xk, gf_ref[...], gft_ref[...], if_ref[...]) * gfk_ref[...]).astype(BF16)
        fv_b = fox_ref[:, 1024:1536].astype(BF16)
        fv_ref[...] = fv_b
        fv1_ref[...] = (_dot(fv_b, vp_ref[...]) + vo_ref[...]).astype(BF16)
        fqn_ref[...] = square_sums(fq_ref, gf_ref[...])
        fkmax_ref[...] = jnp.maximum(fkmax_ref[...], jnp.max(square_sums(fk_ref, gf_ref[...]), axis=0, keepdims=True))


        logf = _log_sigmoid(lat_ref[:, LAT_F:LAT_F + N_HEADS] + bf_ref[...])
        cc_ref[...] = jnp.dot(_tri(tm, True), logf, precision=lax.Precision.HIGHEST, preferred_element_type=F32) + carry[...]
        carry[...] += _colsum(logf)

        qlat_n, _ = _rms_fwd(lat_ref[:, LAT_Q:LAT_Q + Q_LORA], gql_ref[...])
        p = _dot(qlat_n.astype(BF16), wq_ref[...], NT)
        y = p * _group_rms(p, gm_ref[...], gmt_ref[...], im_ref[...]) * (gq_ref[...] * MLA_SCALE)
        for h in range(N_HEADS):
            w = slice(h * LANES, (h + 1) * LANES)
            mq_ref[:, w] = _rope_fwd(y[:, w], ct, s1, s2).astype(BF16)

        kv_n, _ = _rms_fwd(lat_ref[:, LAT_KV:LAT_KV + KV_LORA], gkvl_ref[...])
        kv_b = kv_n.astype(BF16)
        pk = _dot(kv_b, wk_ref[...], NT)
        kn = pk * _group_rms(pk, gm_ref[...], gmt_ref[...], im_ref[...]) * gk_ref[...]
        rm = rm_ref[...]
        kr = pltpu.roll(lat_ref[:, LAT_KR:LAT_KR + LANES], HEAD_DIM, 1) * rm
        rr = lax.rsqrt(jnp.sum(kr * kr, axis=1, keepdims=True) * (1.0 / ROPE_DIM) + RMS_EPS)
        okr = _rope_fwd(kr * rr * gkr_ref[...], ct * rm, s1, s2)
        for h in range(N_HEADS):
            w = slice(h * LANES, (h + 1) * LANES)
            mk_ref[:, w] = (kn[:, w] + okr).astype(BF16)
        mv_ref[...] = _dot(kv_b, wv_ref[...], NT).astype(BF16)
        mv1_ref[...] = (_dot(kv_b, wvil_ref[...], NT).astype(BF16).astype(F32) + vo_ref[...]).astype(BF16)
        mqn_ref[...] = square_sums(mq_ref, gm_ref[...])
        mkmax_ref[...] = jnp.maximum(mkmax_ref[...], jnp.max(square_sums(mk_ref, gm_ref[...]), axis=0, keepdims=True))

    consts = [rope_mask, *mats, wq, wk, wv, g_qlat, g_kvlat, *gains, b_f, v_place, v_ones, wv_il]
    t512, tl, tg, g1 = _tok(tm, 512), _tok(tm, LANES), _tok(tm, N_GROUPS), _full((1, N_GROUPS))
    t1024 = _tok(tm, N_HEADS * LANES)
    return pl.pallas_call(
        body, name="prep_fwd", grid=(t // tm,),
        in_specs=[_tok(tm, 1536), _tok(tm, LAT_W), tl, tl, tl] + [_full(a.shape) for a in consts],
        out_specs=[t512, t512, t512, _tok(tm, N_HEADS), t1024, t1024, t512, tg, g1, tg, g1, t1024, t1024],
        out_shape=[_sds((t, 512), BF16), _sds((t, 512), BF16), _sds((t, 512), BF16), _sds((t, N_HEADS), F32),
                   _sds((t, N_HEADS * LANES), BF16), _sds((t, N_HEADS * LANES), BF16), _sds((t, 512), BF16),
                   _sds((t, N_GROUPS), F32), _sds((1, N_GROUPS), F32), _sds((t, N_GROUPS), F32), _sds((1, N_GROUPS), F32)]
        + [_sds((t, N_HEADS * LANES), BF16)] * 2,
        scratch_shapes=[pltpu.VMEM((1, N_HEADS), F32)],
        compiler_params=pltpu.CompilerParams(dimension_semantics=("arbitrary",)),
    )(fox, lat, c_tab, s1_tab, s2_tab, *consts)


def _prep_bwd(fox, lat, tables, mats, wq, wk, wv, g_qlat, g_kvlat, gains, b_f, dfq, dfk, dfv, dc, dmq, dmk, dmv):
    t = fox.shape[0]
    tm = min(PREP_TILE, t)
    n_t = t // tm
    c_tab, s1_tab, s2_tab, rope_mask = tables

    def body(fox_ref, lat_ref, c_ref, s1_ref, s2_ref, rm_ref, gm_ref, gmt_ref, im_ref, gf_ref, gft_ref, if_ref,
             wq_ref, wk_ref, wv_ref, gql_ref, gkvl_ref, gq_ref, gk_ref, gkr_ref, gfq_ref, gfk_ref, bf_ref,
             dfq_ref, dfk_ref, dfv_ref, dc_ref, dmq_ref, dmk_ref, dmv_ref,
             dfox_ref, dlat_ref, dwq_ref, dwk_ref, dwv_ref, o_gql, o_gkvl, o_gq, o_gk, o_gkr, o_gfq, o_gfk, o_bf,
             carry, lscr, wq_acc, wk_acc, wv_acc, dyscr):
        i = pl.program_id(0)
        ct, s1, s2, rm = c_ref[...], s1_ref[...], s2_ref[...], rm_ref[...]

        @pl.when(i == 0)
        def _():
            for ref in [carry, wq_acc, wk_acc, wv_acc, o_gql, o_gkvl, o_gq, o_gk, o_gkr, o_gfq, o_gfk, o_bf]:
                ref[...] = jnp.zeros_like(ref)

        def group_rms_bwd(dy, x, r, gain, g, g_t, inv):
            xn = x * r
            dyg = dy * gain
            mu = _split_dot(_split_dot(dyg * xn, g) * inv, g_t)
            return r * (dyg - xn * mu), dy * xn

        gf, gft, invf = gf_ref[...], gft_ref[...], if_ref[...]
        xq = fox_ref[:, 0:512]
        dxq, rows = group_rms_bwd(dfq_ref[...] * FOX_SCALE, xq, _group_rms(xq, gf, gft, invf), gfq_ref[...], gf, gft, invf)
        dfox_ref[:, 0:512] = dxq.astype(BF16)
        o_gfq[...] += _colsum(rows)
        xk = fox_ref[:, 512:1024]
        dxk, rows = group_rms_bwd(dfk_ref[...], xk, _group_rms(xk, gf, gft, invf), gfk_ref[...], gf, gft, invf)
        dfox_ref[:, 512:1024] = dxk.astype(BF16)
        o_gfk[...] += _colsum(rows)
        dfox_ref[:, 1024:1536] = dfv_ref[...].astype(BF16)

        lscr[...] = jnp.zeros_like(lscr)

        xql = lat_ref[:, LAT_Q:LAT_Q + Q_LORA]
        qlat_n, r_ql = _rms_fwd(xql, gql_ref[...])
        qlat_b = qlat_n.astype(BF16)
        gm, gmt, invm = gm_ref[...], gmt_ref[...], im_ref[...]
        p = _dot(qlat_b, wq_ref[...], NT)
        for h in range(N_HEADS):
            w = slice(h * LANES, (h + 1) * LANES)
            dyscr[:, w] = _rope_bwd(dmq_ref[:, w], ct, s1, s2) * MLA_SCALE
        dp, rows = group_rms_bwd(dyscr[...], p, _group_rms(p, gm, gmt, invm), gq_ref[...], gm, gmt, invm)
        o_gq[...] += _colsum(rows)
        dp_b = dp.astype(BF16)
        wq_acc[...] += _dot(dp_b, qlat_b, TN)
        dxql, rows = _rms_bwd(_dot(dp_b, wq_ref[...]), xql, gql_ref[...], r_ql)
        lscr[:, LAT_Q:LAT_Q + Q_LORA] = dxql
        o_gql[...] += _colsum(rows)

        xkv = lat_ref[:, LAT_KV:LAT_KV + KV_LORA]
        kv_n, r_kv = _rms_fwd(xkv, gkvl_ref[...])
        kv_b = kv_n.astype(BF16)
        pk = _dot(kv_b, wk_ref[...], NT)
        dpk, rows = group_rms_bwd(dmk_ref[...], pk, _group_rms(pk, gm, gmt, invm), gk_ref[...], gm, gmt, invm)
        o_gk[...] += _colsum(rows)
        dpk_b = dpk.astype(BF16)
        dv_b = dmv_ref[...].astype(BF16)
        wk_acc[...] += _dot(dpk_b, kv_b, TN)
        wv_acc[...] += _dot(dv_b, kv_b, TN)
        dxkv, rows = _rms_bwd(_dot(dpk_b, wk_ref[...]) + _dot(dv_b, wv_ref[...]), xkv, gkvl_ref[...], r_kv)
        lscr[:, LAT_KV:LAT_KV + KV_LORA] = dxkv
        o_gkvl[...] += _colsum(rows)

        dokr = dmk_ref[:, 0:LANES]
        for h in range(1, N_HEADS):
            dokr = dokr + dmk_ref[:, h * LANES:(h + 1) * LANES]
        dykr = _rope_bwd(dokr * rm, ct * rm, s1, s2)
        kr = pltpu.roll(lat_ref[:, LAT_KR:LAT_KR + LANES], HEAD_DIM, 1) * rm
        rr = lax.rsqrt(jnp.sum(kr * kr, axis=1, keepdims=True) * (1.0 / ROPE_DIM) + RMS_EPS)
        krn = kr * rr
        dyg = dykr * gkr_ref[...]
        dkr = rr * (dyg - krn * (jnp.sum(dyg * krn, axis=1, keepdims=True) * (1.0 / ROPE_DIM)))
        o_gkr[...] += _colsum(dykr * krn)
        lscr[:, LAT_KR:LAT_KR + LANES] = pltpu.roll(dkr, LANES - HEAD_DIM, 1)

        dcv = dc_ref[...]
        dlogf = jnp.dot(_tri(tm, False), dcv, precision=lax.Precision.HIGHEST, preferred_element_type=F32) + carry[...]
        carry[...] += _colsum(dcv)
        dz = dlogf * jax.nn.sigmoid(-(lat_ref[:, LAT_F:LAT_F + N_HEADS] + bf_ref[...]))
        lscr[:, LAT_F:LAT_F + N_HEADS] = dz
        o_bf[...] += _colsum(dz)

        dlat_ref[...] = lscr[...].astype(BF16)

        @pl.when(i == n_t - 1)
        def _():
            dwq_ref[...] = wq_acc[...].astype(BF16)
            dwk_ref[...] = wk_acc[...].astype(BF16)
            dwv_ref[...] = wv_acc[...].astype(BF16)

    consts = [rope_mask, *mats, wq, wk, wv, g_qlat, g_kvlat, *gains, b_f]

    def rtok(n):
        return pl.BlockSpec((tm, n), lambda i: (n_t - 1 - i, 0))

    sums = [(1, Q_LORA), (1, KV_LORA), (1, N_HEADS * LANES), (1, N_HEADS * LANES), (1, LANES), (1, 512), (1, 512), (1, N_HEADS)]
    return pl.pallas_call(
        body, name="prep_bwd", grid=(n_t,),
        in_specs=[rtok(1536), rtok(LAT_W), rtok(LANES), rtok(LANES), rtok(LANES)] + [_full(a.shape) for a in consts]
        + [rtok(512), rtok(512), rtok(512), rtok(N_HEADS), rtok(N_HEADS * LANES), rtok(N_HEADS * LANES), rtok(512)],
        out_specs=[rtok(1536), rtok(LAT_W), _full(wq.shape), _full(wk.shape), _full(wv.shape)] + [_full(s) for s in sums],
        out_shape=[_sds((t, 1536), BF16), _sds((t, LAT_W), BF16), _sds(wq.shape, BF16), _sds(wk.shape, BF16), _sds(wv.shape, BF16)]
        + [_sds(s, F32) for s in sums],
        scratch_shapes=[pltpu.VMEM((1, N_HEADS), F32), pltpu.VMEM((tm, LAT_W), F32), pltpu.VMEM(wq.shape, F32),
                        pltpu.VMEM(wk.shape, F32), pltpu.VMEM(wv.shape, F32), pltpu.VMEM((tm, N_HEADS * LANES), F32)],
        compiler_params=pltpu.CompilerParams(dimension_semantics=("arbitrary",)),
    )(fox, lat, c_tab, s1_tab, s2_tab, *consts, dfq, dfk, dfv, dc, dmq, dmk, dmv)


def _causal_pairs(n, query_major):
    pairs = [(i, j) for i in range(n) for j in range(i + 1)] if query_major else [(i, j) for j in range(n) for i in range(j, n)]
    return jnp.asarray([p[0] for p in pairs], jnp.int32), jnp.asarray([p[1] for p in pairs], jnp.int32)


def _logit_bound(q_sq, k_sq_max):
    qn = q_sq[:, :N_HEADS] + q_sq[:, N_HEADS:]
    kmax = k_sq_max[:, :N_HEADS] + k_sq_max[:, N_HEADS:]
    bound = jnp.sqrt(qn * kmax) * (1.0 + 2.0 ** -10) + 2.0 ** -10
    flag = (jnp.max(bound) <= FIXED_SHIFT_MAX_BOUND).astype(F32).reshape(1, 1)
    return bound, flag


def _attn_fwd(q, k, v, bound, fixed_ok, c, c_t, *, lanes, name, exchange=None):
    t = q.shape[0]
    tq = min(ATT_TILE, t)
    n_q = t // tq
    hd = HEAD_DIM
    ch = min(ATT_COL_CHUNK, tq)
    decay = c is not None

    def body(qi_ref, kj_ref, *refs):
        if decay:
            q_ref, k_ref, v_ref, b_ref, ok_ref, c_ref, ct_ref, o_ref, o32_ref, lse_ref, m_scr, acc = refs
        else:
            q_ref, k_ref, v_ref, b_ref, ok_ref, o_ref, o32_ref, lse_ref, m_scr, acc = refs
        i, j = qi_ref[pl.program_id(0)], kj_ref[pl.program_id(0)]
        fixed = ok_ref[0, 0] > 0.5

        @pl.when(j == 0)
        def _():
            m_scr[...] = jnp.full_like(m_scr, MASK_VALUE)
            acc[...] = jnp.zeros_like(acc)

        def fixed_step(diagonal):
            for h in range(N_HEADS):
                wl = slice(h * lanes, (h + 1) * lanes)
                wv = slice(h * LANES, (h + 1) * LANES)
                qh = q_ref[:, wl]
                row = (c_ref[:, h:h + 1] - b_ref[:, h:h + 1]) if decay else -b_ref[:, h:h + 1]
                o_hi = jnp.zeros((tq, LANES), F32)
                o_lo = jnp.zeros((tq, LANES), F32)
                for cc in range(tq // ch):
                    cols = slice(cc * ch, (cc + 1) * ch)
                    s = _dot(qh, k_ref[cols, wl], NT)
                    s = s + ((row - ct_ref[h:h + 1, cols]) if decay else row)
                    if diagonal:
                        keep = (lax.broadcasted_iota(jnp.int32, (tq, ch), 0)
                                >= lax.broadcasted_iota(jnp.int32, (tq, ch), 1) + cc * ch)
                        s = jnp.where(keep, s, MASK_VALUE)
                    p = jnp.exp(s)
                    p_b = p.astype(BF16)
                    o_hi = o_hi + _dot(p_b, v_ref[cols, wv])
                    if decay:
                        o_lo = o_lo + _dot((p - p_b.astype(F32)).astype(BF16), v_ref[cols, wv])
                acc[0, :, wv] += o_hi
                if decay:
                    acc[1, :, wv] += o_lo

        def step(diagonal):
            if diagonal:
                keep = lax.broadcasted_iota(jnp.int32, (tq, tq), 0) >= lax.broadcasted_iota(jnp.int32, (tq, tq), 1)
            for h in range(N_HEADS):
                s = _dot(q_ref[:, h * lanes:(h + 1) * lanes], k_ref[:, h * lanes:(h + 1) * lanes], NT)
                if decay:
                    s = s + (c_ref[:, h:h + 1] - ct_ref[h:h + 1, :])
                if diagonal:
                    s = jnp.where(keep, s, MASK_VALUE)
                m_prev = m_scr[h]
                m_new = jnp.maximum(m_prev, jnp.max(s, axis=1, keepdims=True))
                alpha = jnp.exp(m_prev - m_new)
                p = jnp.exp(s - m_new)
                wv = slice(h * LANES, (h + 1) * LANES)
                p_b = p.astype(BF16)
                acc[0, :, wv] = alpha * acc[0, :, wv] + _dot(p_b, v_ref[:, wv])
                if decay:
                    acc[1, :, wv] = alpha * acc[1, :, wv] + _dot((p - p_b.astype(F32)).astype(BF16), v_ref[:, wv])
                m_scr[h] = m_new

        for diagonal, here in ((False, j < i), (True, j == i)):
            @pl.when(here & fixed)
            def _():
                fixed_step(diagonal)

            @pl.when(here & jnp.logical_not(fixed))
            def _():
                step(diagonal)

        @pl.when(j == i)
        def _():
            for h in range(N_HEADS):
                w = slice(h * hd, (h + 1) * hd)
                val, one = slice(h * LANES, h * LANES + hd), slice(h * LANES + hd, h * LANES + hd + 1)
                l = (acc[0, :, one] + acc[1, :, one]) if decay else acc[0, :, one]
                o_ref[:, w] = (acc[0, :, val] / l).astype(BF16)
                o32_ref[:, w] = ((acc[0, :, val] + acc[1, :, val]) if decay else acc[0, :, val]) / l
                lse_ref[:, h:h + 1] = jnp.where(fixed, b_ref[:, h:h + 1], m_scr[h]) + jnp.log(l)

    qspec = lambda n: pl.BlockSpec((tq, n), lambda s, qi, kj: (qi[s], 0))
    kspec = lambda n: pl.BlockSpec((tq, n), lambda s, qi, kj: (kj[s], 0))
    in_specs = [qspec(N_HEADS * lanes), kspec(N_HEADS * lanes), kspec(N_HEADS * LANES), qspec(N_HEADS),
                pl.BlockSpec(memory_space=pltpu.SMEM)]
    args = [q, k, v, bound, fixed_ok]
    if decay:
        in_specs += [qspec(N_HEADS), pl.BlockSpec((N_HEADS, tq), lambda s, qi, kj: (0, kj[s]))]
        args += [c, c_t]
    pairs = _causal_pairs(n_q, query_major=True)
    return _gridded_call(
        body, name=name, grid=(pairs[0].shape[0],), in_specs=in_specs, out_specs=[qspec(512), qspec(512), qspec(N_HEADS)],
        out_shape=[_sds((t, 512), BF16), _sds((t, 512), F32), _sds((t, N_HEADS), F32)],
        scratch_shapes=[pltpu.VMEM((N_HEADS, tq, 1), F32), pltpu.VMEM((2, tq, N_HEADS * LANES), F32)],
        args=args, exchange=exchange, prefetch=pairs)


def _attn_bwd(q, k, v, do, delta, lse, c, c_t, *, lanes, name, exchange=None):
    t = q.shape[0]
    tq = min(ATT_TILE, t)
    n_q = t // tq
    hd = HEAD_DIM
    decay = c is not None

    pairs = _causal_pairs(n_q, query_major=False)
    n_steps = pairs[0].shape[0]

    def body(qi_ref, kj_ref, *refs):
        if decay:
            q_ref, k_ref, v_ref, do_ref, delta_ref, lse_ref, c_ref, ct_ref, dq_hbm, dk_ref, dv_ref, dct_ref, dq_ref = refs
        else:
            q_ref, k_ref, v_ref, do_ref, delta_ref, lse_ref, dq_hbm, dk_ref, dv_ref, dq_ref = refs
        step_id = pl.program_id(0)
        i, j = qi_ref[step_id], kj_ref[step_id]

        @pl.when(step_id == 0)
        def _():
            dq_ref[...] = jnp.zeros_like(dq_ref)

        @pl.when(i == j)
        def _():
            dk_ref[...] = jnp.zeros_like(dk_ref)
            dv_ref[...] = jnp.zeros_like(dv_ref)
            if decay:
                dct_ref[...] = jnp.zeros_like(dct_ref)

        def step(diagonal):
            if diagonal:
                keep = lax.broadcasted_iota(jnp.int32, (tq, tq), 0) >= lax.broadcasted_iota(jnp.int32, (tq, tq), 1)
            rows = pl.ds(pl.multiple_of(i * tq, tq), tq)
            for h in range(N_HEADS):
                wl = slice(h * lanes, (h + 1) * lanes)
                w = slice(h * hd, (h + 1) * hd)
                qh, kh = q_ref[:, wl], k_ref[:, wl]
                s = _dot(qh, kh, NT)
                if decay:
                    s = s + (c_ref[:, h:h + 1] - ct_ref[h:h + 1, :])
                if diagonal:
                    s = jnp.where(keep, s, MASK_VALUE)
                p = jnp.exp(s - lse_ref[:, h:h + 1])
                doh = do_ref[:, w]
                dv_ref[:, w] += _dot(p.astype(BF16), doh, TN)
                dp = _dot(doh, v_ref[:, w], NT)
                ds = p * (dp - delta_ref[:, h:h + 1])
                if decay:
                    dct_ref[h:h + 1, :] -= _colsum(ds)
                ds_b = ds.astype(BF16)
                dk_ref[:, wl] += _dot(ds_b, qh, TN)
                dq_ref[rows, wl] += _dot(ds_b, kh)

        @pl.when(i > j)
        def _():
            step(False)

        @pl.when(i == j)
        def _():
            step(True)

        @pl.when(step_id == n_steps - 1)
        def _():
            pltpu.sync_copy(dq_ref, dq_hbm)

    qspec = lambda n: pl.BlockSpec((tq, n), lambda s, qi, kj: (qi[s], 0))
    kspec = lambda n: pl.BlockSpec((tq, n), lambda s, qi, kj: (kj[s], 0))
    in_specs = [qspec(N_HEADS * lanes), kspec(N_HEADS * lanes), kspec(512), qspec(512), qspec(N_HEADS), qspec(N_HEADS)]
    out_specs = [pl.BlockSpec(memory_space=pl.ANY), kspec(N_HEADS * lanes), kspec(512)]
    out_shape = [_sds((t, N_HEADS * lanes), F32), _sds((t, N_HEADS * lanes), F32), _sds((t, 512), F32)]
    args = [q, k, v, do, delta, lse]
    if decay:
        ctspec = pl.BlockSpec((N_HEADS, tq), lambda s, qi, kj: (0, kj[s]))
        in_specs += [qspec(N_HEADS), ctspec]
        out_specs.append(ctspec)
        out_shape.append(_sds((N_HEADS, t), F32))
        args += [c, c_t]
    return _gridded_call(body, name=name, grid=(n_steps,), in_specs=in_specs, out_specs=out_specs, out_shape=out_shape,
                         scratch_shapes=[pltpu.VMEM((t, N_HEADS * lanes), F32)], args=args, exchange=exchange, prefetch=pairs)


def _mix_fwd(x, y_mla, y_fox, gates, b_gate, wbm_t, wbf_t, wo):
    t = x.shape[0]
    tm = min(PROJ_TILE, t)

    def body(x_ref, ym_ref, yf_ref, gt_ref, bg_ref, wbm_ref, wbf_ref, wo_ref, out_ref):
        um = _dot(ym_ref[...], wbm_ref[...], NT)
        uf = _dot(yf_ref[...], wbf_ref[...], NT)
        sm = jax.nn.sigmoid(gt_ref[:, 0:D_MODEL] + bg_ref[0:1, :])
        sf = jax.nn.sigmoid(gt_ref[:, D_MODEL:2 * D_MODEL] + bg_ref[1:2, :])
        mixed = sm * um + sf * uf
        out_ref[...] = x_ref[...] + _dot(mixed.astype(BF16), wo_ref[...])

    tok = _tok(tm, D_MODEL)
    return pl.pallas_call(
        body, name="mix_fwd", grid=(t // tm,),
        in_specs=[tok, _tok(tm, 512), _tok(tm, 512), _tok(tm, 2 * D_MODEL), _full((2, D_MODEL)), _full(wbm_t.shape),
                  _full(wbf_t.shape), _full(wo.shape)],
        out_specs=tok, out_shape=_sds((t, D_MODEL), F32),
        compiler_params=pltpu.CompilerParams(dimension_semantics=("arbitrary",)),
    )(x, y_mla, y_fox, gates, b_gate, wbm_t, wbf_t, wo)


def _mix_bwd(dx, y_mla, y_fox, y_mla32, y_fox32, gates, b_gate, wbm_t, wbf_t, wo, exchange=None):
    t = dx.shape[0]
    tm = min(PROJ_TILE, t)

    def body(dx_ref, ym_ref, yf_ref, ym32_ref, yf32_ref, gt_ref, bg_ref, wbm_ref, wbf_ref, wo_ref,
             dym_ref, dyf_ref, dlm_ref, dlf_ref, dgt_ref, mixed_ref, dum_ref, duf_ref, dxb_ref, dbg_ref, prod):
        i = pl.program_id(0)
        dxb = dx_ref[...].astype(BF16)
        dxb_ref[...] = dxb
        dmixed = _dot(dxb, wo_ref[...], NT)
        um = _dot(ym_ref[...], wbm_ref[...], NT)
        uf = _dot(yf_ref[...], wbf_ref[...], NT)
        sm = jax.nn.sigmoid(gt_ref[:, 0:D_MODEL] + bg_ref[0:1, :])
        sf = jax.nn.sigmoid(gt_ref[:, D_MODEL:2 * D_MODEL] + bg_ref[1:2, :])
        mixed_ref[...] = (sm * um + sf * uf).astype(BF16)
        dum = (dmixed * sm).astype(BF16)
        duf = (dmixed * sf).astype(BF16)
        dum_ref[...] = dum
        duf_ref[...] = duf
        dgm = dmixed * um * (sm * (1.0 - sm))
        dgf = dmixed * uf * (sf * (1.0 - sf))
        dgt_ref[:, 0:D_MODEL] = dgm.astype(BF16)
        dgt_ref[:, D_MODEL:2 * D_MODEL] = dgf.astype(BF16)
        for du, wb_ref, y32_ref, dy_ref, dl_ref in ((dum, wbm_ref, ym32_ref, dym_ref, dlm_ref),
                                                    (duf, wbf_ref, yf32_ref, dyf_ref, dlf_ref)):
            dy = _dot(du, wb_ref[...])
            dy_ref[...] = dy.astype(BF16)
            prod[...] = dy.astype(BF16).astype(F32) * y32_ref[...]
            for h in range(N_HEADS):
                dl_ref[:, h:h + 1] = jnp.sum(prod[:, h * HEAD_DIM:(h + 1) * HEAD_DIM], axis=1, keepdims=True)

        @pl.when(i == 0)
        def _():
            dbg_ref[...] = jnp.zeros_like(dbg_ref)

        dbg_ref[0:1, :] += _colsum(dgm)
        dbg_ref[1:2, :] += _colsum(dgf)

    tok = _tok(tm, D_MODEL)
    tokb = _sds((t, D_MODEL), BF16)
    t512, t8 = _tok(tm, 512), _tok(tm, N_HEADS)
    return _gridded_call(
        body, name="mix_bwd", grid=(t // tm,),
        in_specs=[tok, t512, t512, t512, t512, _tok(tm, 2 * D_MODEL), _full((2, D_MODEL)), _full(wbm_t.shape),
                  _full(wbf_t.shape), _full(wo.shape)],
        out_specs=[t512, t512, t8, t8, _tok(tm, 2 * D_MODEL), tok, tok, tok, tok, _full((2, D_MODEL))],
        out_shape=[_sds((t, 512), BF16), _sds((t, 512), BF16), _sds((t, N_HEADS), F32), _sds((t, N_HEADS), F32),
                   _sds((t, 2 * D_MODEL), BF16), tokb, tokb, tokb, tokb, _sds((2, D_MODEL), F32)],
        scratch_shapes=[pltpu.VMEM((tm, 512), F32)],
        args=[dx, y_mla, y_fox, y_mla32, y_fox32, gates, b_gate, wbm_t, wbf_t, wo], exchange=exchange)


def _my_position():
    x, y, c = lax.axis_index("x"), lax.axis_index("y"), lax.axis_index("c")
    return x, y, c, 4 * x + 2 * y + c


def _peer(x, y, c, mask):
    px = 1 - x if mask & 4 else x
    py = 1 - y if mask & 2 else y
    pc = 1 - c if mask & 1 else c
    return (px, py, pc), 4 * px + 2 * py + pc


def _chip_peer(x, y, km):
    px = 1 - x if km & 2 else x
    py = 1 - y if km & 1 else y
    return px, py, 2 * px + py


_HBM = pl.BlockSpec(memory_space=pl.ANY)


def _wait_all(copies):
    for cp in copies:
        cp.wait()


class _ChipExchange:
    def __init__(self, gather, arrays):
        self.gather, self.arrays, self.aliased = gather, list(arrays), False
        n = len(self.arrays)
        self.out_shape = [_sds((N_DEV * a.shape[0],) + a.shape[1:], a.dtype) if gather else _sds(a.shape, a.dtype)
                          for a in self.arrays]
        self.scratch_shapes = [pltpu.SemaphoreType.DMA((n, N_CHIP)), pltpu.SemaphoreType.DMA((n, N_CHIP)),
                               pltpu.SemaphoreType.DMA((n,))]

    def copies(self, srcs, dsts, send_sems, recv_sems, local_sems):
        x, y, c, me = _my_position()
        q_me = 2 * x + y
        out = []
        for a in range(len(self.arrays)):
            if self.gather:
                r = srcs[a].shape[0]
                local_src, dst = srcs[a], dsts[a].at[pl.ds(me * r, r)]
            else:
                local_src, dst = srcs[a].at[q_me], dsts[a].at[q_me]
            out.append(pltpu.make_async_copy(local_src, dst, local_sems.at[a]))
            for km in range(1, N_CHIP):
                px, py, q_peer = _chip_peer(x, y, km)
                out.append(pltpu.make_async_remote_copy(
                    src_ref=srcs[a] if self.gather else srcs[a].at[q_peer], dst_ref=dst, send_sem=send_sems.at[a, km],
                    recv_sem=recv_sems.at[a, km], device_id=(px, py, c), device_id_type=MESH))
        return out

    def standalone(self, name):
        n = len(self.arrays)

        def body(*refs):
            copies = self.copies(refs[:n], refs[n:2 * n], *refs[2 * n:])
            for cp in copies:
                cp.start()
            _wait_all(copies)

        return pl.pallas_call(body, name=name, in_specs=[_HBM] * n, out_specs=[_HBM] * n, out_shape=self.out_shape,
                              scratch_shapes=self.scratch_shapes)(*self.arrays)


def _gridded_call(body, *, name, grid, in_specs, out_specs, out_shape, scratch_shapes, args, exchange=None, prefetch=()):
    params = pltpu.CompilerParams(dimension_semantics=("arbitrary",) * len(grid))
    n_pre, n_in, n_out, n_scr = len(prefetch), len(in_specs), len(out_specs), len(scratch_shapes)
    n_x = 0 if exchange is None else len(exchange.arrays)

    def carrier(*refs):
        pre, refs = refs[:n_pre], refs[n_pre:]
        ins, x_src, refs = refs[:n_in], refs[n_in:n_in + n_x], refs[n_in + n_x:]
        outs, x_dst, refs = refs[:n_out], refs[n_out:n_out + n_x], refs[n_out + n_x:]
        copies = exchange.copies(x_src, x_dst, *refs[n_scr:])
        pids = [pl.program_id(d) for d in range(len(grid))]
        first = functools.reduce(jnp.logical_and, [p == 0 for p in pids])
        last = functools.reduce(jnp.logical_and, [p == g - 1 for p, g in zip(pids, grid)])

        @pl.when(first)
        def _():
            for cp in copies:
                cp.start()

        body(*pre, *ins, *outs, *refs[:n_scr])

        @pl.when(last)
        def _():
            _wait_all(copies)

    x_shapes, x_scratch, x_args = ([], [], []) if exchange is None else (exchange.out_shape, exchange.scratch_shapes, exchange.arrays)
    spec = pltpu.PrefetchScalarGridSpec(
        num_scalar_prefetch=n_pre, grid=grid, in_specs=list(in_specs) + [_HBM] * n_x, out_specs=list(out_specs) + [_HBM] * n_x,
        scratch_shapes=list(scratch_shapes) + x_scratch)
    in_place = {n_pre + n_in + k: n_out + k for k in range(n_x)} if n_x and exchange.aliased else {}
    res = pl.pallas_call(body if exchange is None else carrier, name=name, grid_spec=spec, out_shape=list(out_shape) + x_shapes,
                         input_output_aliases=in_place, compiler_params=params)(*prefetch, *args, *x_args)
    return res[:n_out], (None if exchange is None else res[n_out:])


class _CoreExchange:
    def __init__(self, gather, arrays):
        self.gather, self.arrays, self.aliased = gather, list(arrays), gather
        n = len(self.arrays)
        self.out_shape = [_sds(a.shape, a.dtype) if gather else _sds((N_CHIP, a.shape[0] // N_DEV) + a.shape[1:], a.dtype)
                          for a in self.arrays]
        self.scratch_shapes = [pltpu.SemaphoreType.DMA((n, N_CHIP)), pltpu.SemaphoreType.DMA((n, N_CHIP))]

    def copies(self, srcs, dsts, send_sems, recv_sems):
        x, y, c, _ = _my_position()
        out = []
        for a in range(len(self.arrays)):
            r = srcs[a].shape[0] // N_DEV
            for q in range(N_CHIP):
                if self.gather:
                    rows = pl.ds((2 * q + c) * r, r)
                    src, dst = srcs[a].at[rows], dsts[a].at[rows]
                else:
                    src, dst = srcs[a].at[pl.ds((2 * q + 1 - c) * r, r)], dsts[a].at[q]
                out.append(pltpu.make_async_remote_copy(src_ref=src, dst_ref=dst, send_sem=send_sems.at[a, q],
                                                        recv_sem=recv_sems.at[a, q], device_id=(x, y, 1 - c), device_id_type=MESH))
        return out

    def standalone(self, name):
        n = len(self.arrays)

        def body(*refs):
            copies = self.copies(refs[:n], refs[n:2 * n], *refs[2 * n:])
            for cp in copies:
                cp.start()
            _wait_all(copies)

        return pl.pallas_call(body, name=name, in_specs=[_HBM] * n, out_specs=[_HBM] * n, out_shape=self.out_shape,
                              input_output_aliases={a: a for a in range(n)} if self.aliased else {},
                              scratch_shapes=self.scratch_shapes)(*self.arrays)


def _gather_over_cores(arrays, name):
    return _CoreExchange(True, arrays).standalone(name)


def _grads_to_sibling(grads, name):
    return _CoreExchange(False, grads).standalone(name)


def _pair_sum(grad, from_sibling, name):
    r, n = from_sibling.shape[1:]

    def body(g_ref, s_ref, o_ref):
        c = lax.axis_index("c")
        o_ref[...] = (g_ref[c].astype(F32) + s_ref[...].astype(F32)).astype(BF16)

    return pl.pallas_call(
        body, name=name, grid=(N_CHIP,),
        in_specs=[pl.BlockSpec((None, 2, r, n), lambda q: (q, 0, 0, 0)), pl.BlockSpec((None, r, n), lambda q: (q, 0, 0))],
        out_specs=pl.BlockSpec((None, r, n), lambda q: (q, 0, 0)), out_shape=_sds((N_CHIP, r, n), BF16),
    )(grad.reshape(N_CHIP, 2, r, n), from_sibling)


def _all_reduce_small(vec):
    r = vec.shape[0]

    def body(v_ref, o_ref, buf, send_sems, recv_sems):
        x, y, c, me = _my_position()
        buf[me] = v_ref[...]
        copies = []
        for mask in range(1, N_DEV):
            peer, _ = _peer(x, y, c, mask)
            cp = pltpu.make_async_remote_copy(src_ref=v_ref, dst_ref=buf.at[me], send_sem=send_sems.at[mask],
                                              recv_sem=recv_sems.at[mask], device_id=peer, device_id_type=MESH)
            cp.start()
            copies.append(cp)
        for cp in copies:
            cp.wait()
        total = buf[0]
        for s in range(1, N_DEV):
            total = total + buf[s]
        o_ref[...] = total

    vm = pl.BlockSpec(memory_space=pltpu.VMEM)
    return pl.pallas_call(
        body, name="all_reduce_small", in_specs=[vm], out_specs=vm, out_shape=_sds(vec.shape, F32),
        scratch_shapes=[pltpu.VMEM((N_DEV, r, 128), F32), pltpu.SemaphoreType.DMA((N_DEV,)), pltpu.SemaphoreType.DMA((N_DEV,))],
    )(vec)


def _adamw_math(w, g, m, v):
    m = ADAM_B1 * m + (1.0 - ADAM_B1) * g
    v = ADAM_B2 * v + (1.0 - ADAM_B2) * (g * g)
    m_hat = m / (1.0 - ADAM_B1 ** ADAM_STEP)
    v_hat = v / (1.0 - ADAM_B2 ** ADAM_STEP)
    delta = -ADAM_LR * (m_hat / (jnp.sqrt(v_hat) + ADAM_EPS) + ADAM_WD * w)
    return delta, m, v


def _reduce_adamw(slots, w, m, v, *, transpose, name):
    r, n = slots.shape[1:]
    padded = not transpose and w.shape[0] != r

    def body(s_ref, w_ref, m_ref, v_ref, g_ref, d_ref, nm_ref, nv_ref, *scr):
        g = s_ref[0].astype(F32)
        for s in range(1, slots.shape[0]):
            g = g + s_ref[s].astype(F32)
        if transpose:
            scr[0][...] = g.T
            g = scr[0][:, 0:w_ref.shape[1]]
        elif padded:
            scr[0][...] = g
            g = scr[0][0:w_ref.shape[0], :]
        g_ref[...] = g
        d_ref[...], nm_ref[...], nv_ref[...] = _adamw_math(w_ref[...], g, m_ref[...], v_ref[...])

    out = _sds(w.shape, F32)
    if not transpose and not padded and r % (2 * BF16_ROW_TILE) == 0:
        half = pl.BlockSpec((r // 2, n), lambda i: (i, 0))
        return pl.pallas_call(
            body, name=name, grid=(2,), in_specs=[pl.BlockSpec((slots.shape[0], r // 2, n), lambda i: (0, i, 0)), half, half, half],
            out_specs=[half] * 4, out_shape=[out, out, out, out])(slots, w, m, v)
    return pl.pallas_call(
        body, name=name, out_shape=[out, out, out, out],
        scratch_shapes=[pltpu.VMEM((n, r) if transpose else (r, n), F32)] if transpose or padded else [],
    )(slots, w, m, v)


def _adamw(g, w, m, v, name):
    def body(g_ref, w_ref, m_ref, v_ref, d_ref, nm_ref, nv_ref):
        d_ref[...], nm_ref[...], nv_ref[...] = _adamw_math(w_ref[...], g_ref[...], m_ref[...], v_ref[...])

    out = _sds(w.shape, F32)
    return pl.pallas_call(body, name=name, out_shape=[out, out, out])(g, w, m, v)


_SMALL = ["ffn1_norm", "mix_norm", "ffn2_norm", "mla_q_lat_norm", "mla_kv_lat_norm", "mla_q_nope_gain", "mla_q_rope_gain",
          "mla_k_nope_gain", "mla_k_rope_gain", "fox_q_gain", "fox_k_gain", "fox_b_f"]
_WEIGHTS = ["ffn1_norm", "ffn1_w_gate", "ffn1_w_up", "ffn1_w_down", "mix_norm", "w_in", "mla_q_lat_norm", "mla_w_qb",
            "mla_kv_lat_norm", "mla_w_kvb", "mla_q_nope_gain", "mla_q_rope_gain", "mla_k_nope_gain", "mla_k_rope_gain",
            "fox_q_gain", "fox_k_gain", "fox_b_f", "w_branch_mla", "w_branch_fox", "b_gate", "w_o", "ffn2_norm",
            "ffn2_w_gate", "ffn2_w_up", "ffn2_w_down"]
_IN_Q, _IN_KV, _IN_KR, _IN_FOX, _IN_F, _IN_GATES = (0, 192), (192, 128), (320, 32), (352, 1536), (1888, 8), (1896, 2048)


def _rows(a, seg):
    return a[seg[0]:seg[0] + seg[1]]


def _split_w_in(win_t):
    z = lambda n: jnp.zeros((n, D_MODEL), win_t.dtype)
    lat = jnp.concatenate([_rows(win_t, _IN_Q), z(LAT_KV - Q_LORA), _rows(win_t, _IN_KV), _rows(win_t, _IN_KR),
                           _rows(win_t, _IN_F), z(LAT_W - LAT_F - N_HEADS)], axis=0)
    return _rows(win_t, _IN_GATES), _rows(win_t, _IN_FOX), lat


def _join_w_in(d_gates, d_fox, d_lat):
    return jnp.concatenate([d_lat[LAT_Q:LAT_Q + Q_LORA], d_lat[LAT_KV:LAT_KV + KV_LORA], d_lat[LAT_KR:LAT_KR + ROPE_DIM],
                            d_fox, d_lat[LAT_F:LAT_F + N_HEADS], d_gates], axis=0)


def kernel(x, positions, ffn1_norm, ffn1_w_gate, ffn1_w_up, ffn1_w_down, mix_norm, w_in, mla_q_lat_norm, mla_w_qb, mla_kv_lat_norm, mla_w_kvb, mla_q_nope_gain, mla_q_rope_gain, mla_k_nope_gain, mla_k_rope_gain, fox_q_gain, fox_k_gain, fox_b_f, w_branch_mla, w_branch_fox, b_gate, w_o, ffn2_norm, ffn2_w_gate, ffn2_w_up, ffn2_w_down, loss_target, m_ffn1_norm, m_ffn1_w_gate, m_ffn1_w_up, m_ffn1_w_down, m_mix_norm, m_w_in, m_mla_q_lat_norm, m_mla_w_qb, m_mla_kv_lat_norm, m_mla_w_kvb, m_mla_q_nope_gain, m_mla_q_rope_gain, m_mla_k_nope_gain, m_mla_k_rope_gain, m_fox_q_gain, m_fox_k_gain, m_fox_b_f, m_w_branch_mla, m_w_branch_fox, m_b_gate, m_w_o, m_ffn2_norm, m_ffn2_w_gate, m_ffn2_w_up, m_ffn2_w_down, v_ffn1_norm, v_ffn1_w_gate, v_ffn1_w_up, v_ffn1_w_down, v_mix_norm, v_w_in, v_mla_q_lat_norm, v_mla_w_qb, v_mla_kv_lat_norm, v_mla_w_kvb, v_mla_q_nope_gain, v_mla_q_rope_gain, v_mla_k_nope_gain, v_mla_k_rope_gain, v_fox_q_gain, v_fox_k_gain, v_fox_b_f, v_w_branch_mla, v_w_branch_fox, v_b_gate, v_w_o, v_ffn2_norm, v_ffn2_w_gate, v_ffn2_w_up, v_ffn2_w_down):
    env = dict(locals())
    strip = lambda n, a: a if n in _SMALL else a[0]
    W = {n: strip(n, env[n]) for n in _WEIGHTS}
    M = {n: strip(n, env["m_" + n]) for n in _WEIGHTS}
    V = {n: strip(n, env["v_" + n]) for n in _WEIGHTS}
    xs = x[0]
    t = xs.shape[0]

    col_split = ["ffn1_w_gate", "ffn1_w_up", "ffn2_w_gate", "ffn2_w_up", "mla_w_qb", "mla_w_kvb", "w_branch_mla", "w_branch_fox"]
    row_split = ["ffn1_w_down", "ffn2_w_down", "w_o"]
    pieces = {n: W[n].T.astype(BF16) for n in col_split}
    pieces.update({n: W[n].astype(BF16) for n in row_split})
    pieces["w_in"] = jnp.pad(W["w_in"].T.astype(BF16), ((0, W_IN_PIECE_PAD - W_IN_PIECE), (0, 0)))
    pieces["b_gate"] = W["b_gate"].T
    group_a = ["ffn1_w_gate", "ffn1_w_up", "ffn1_w_down"]
    group_b = ["w_in", "mla_w_qb", "mla_w_kvb", "w_branch_mla", "w_branch_fox", "w_o", "b_gate"]
    group_c = ["ffn2_w_gate", "ffn2_w_up", "ffn2_w_down"]
    gather = lambda group: _ChipExchange(True, [pieces[n] for n in group])
    G = dict(zip(group_a, _gather_over_cores(gather(group_a).standalone("gather_ici_a"), "gather_d2d_a")))

    inv_freq = ROPE_THETA ** (-jnp.arange(ROPE_HALF, dtype=F32) / ROPE_HALF)
    ang = positions[0].astype(F32)[:, None] * inv_freq
    tables, mats = _prep_tables(jnp.cos(ang), jnp.sin(ang)), _group_matrices()
    gains = _head_gains(*[W[n] for n in ["mla_q_nope_gain", "mla_q_rope_gain", "mla_k_nope_gain", "mla_k_rope_gain",
                                         "fox_q_gain", "fox_k_gain"]])

    (x1, a1, b1), got_b = _ffn_fwd(xs, W["ffn1_norm"], G["ffn1_w_gate"], G["ffn1_w_up"], G["ffn1_w_down"],
                                   exchange=gather(group_b))
    G.update(zip(group_b, _gather_over_cores(got_b, "gather_d2d_b")))
    win_t = G["w_in"].reshape(N_DEV, W_IN_PIECE_PAD, D_MODEL)[:, :W_IN_PIECE].reshape(N_DEV * W_IN_PIECE, D_MODEL)
    wgate_t, wfox_t, wlat_t = _split_w_in(win_t)
    bg = G["b_gate"].T
    gates, fox, lat = _proj_fwd(x1, W["mix_norm"], wgate_t, wfox_t, wlat_t)
    prep_args = (fox, lat, tables, mats, *_interleave_weights(G["mla_w_qb"], G["mla_w_kvb"]), W["mla_q_lat_norm"],
                 W["mla_kv_lat_norm"], gains, W["fox_b_f"])
    fq, fk, fv, c, mq, mk, mv, fq_sq, fk_sq_max, mq_sq, mk_sq_max, fv1, mv1 = _prep_fwd(*prep_args, *_value_layout(prep_args[6]))
    c_t = c.T
    b_fox, ok_fox = _logit_bound(fq_sq, fk_sq_max)
    b_mla, ok_mla = _logit_bound(mq_sq, mk_sq_max)
    (y_fox, y_fox32, lse_fox), got_c = _attn_fwd(fq, fk, fv1, b_fox, ok_fox, c, c_t, lanes=HEAD_DIM, name="fox_fwd",
                                                 exchange=gather(group_c))
    (y_mla, y_mla32, lse_mla), got_c = _attn_fwd(mq, mk, mv1, b_mla, ok_mla, None, None, lanes=MLA_QK_LANES, name="mla_fwd",
                                                 exchange=_CoreExchange(True, got_c))
    G.update(zip(group_c, got_c))
    x2 = _mix_fwd(x1, y_mla, y_fox, gates, bg, G["w_branch_mla"], G["w_branch_fox"], G["w_o"])
    (dx3, a2, b2, loss_vec), _ = _ffn_fwd(x2, W["ffn2_norm"], G["ffn2_w_gate"], G["ffn2_w_up"], G["ffn2_w_down"],
                                          target=loss_target[0])

    def chip_sums(group, tag, from_sibling=None):
        if from_sibling is None:
            from_sibling = _grads_to_sibling([grads[n] for n in group], "grads_d2d_" + tag)
        return _ChipExchange(False, [_pair_sum(grads[n], s, "pair_sum_" + n) for n, s in zip(group, from_sibling)])

    (dx2, dg_ffn2, da2, db2, h2, n2, dyh2), _ = _ffn_bwd(dx3, x2, W["ffn2_norm"], a2, b2, G["ffn2_w_gate"], G["ffn2_w_up"],
                                                        G["ffn2_w_down"], "ffn2_bwd")
    grads = {"ffn2_w_gate": _tn_matmul(da2, n2, "ffn2_dgate"), "ffn2_w_up": _tn_matmul(db2, n2, "ffn2_dup"),
             "ffn2_w_down": _tn_matmul(h2, dyh2, "ffn2_ddown")}
    (dy_mla, dy_fox, delta_mla, delta_fox, dgates, mixed, dum, duf, dx2b, dbg), from_sibling_c = _mix_bwd(
        dx2, y_mla, y_fox, y_mla32, y_fox32, gates, bg, G["w_branch_mla"], G["w_branch_fox"], G["w_o"],
        exchange=_CoreExchange(False, [grads[n] for n in group_c]))
    grads["w_o"] = _tn_matmul(mixed, dx2b, "d_w_o")
    grads["w_branch_mla"] = _tn_matmul(dum, y_mla, "d_w_branch_mla")
    grads["w_branch_fox"] = _tn_matmul(duf, y_fox, "d_w_branch_fox")
    (dfq, dfk, dfv, dc_t), slots_c = _attn_bwd(fq, fk, fv, dy_fox, delta_fox, lse_fox, c, c_t, lanes=HEAD_DIM,
                                               name="fox_bwd", exchange=chip_sums(group_c, "c", from_sibling_c))
    slots = dict(zip(group_c, slots_c))
    (dmq, dmk, dmv), _ = _attn_bwd(mq, mk, mv, dy_mla, delta_mla, lse_mla, None, None, lanes=MLA_QK_LANES, name="mla_bwd")
    dfox, dlat, dwq, dwk, dwv, d_gql, d_gkvl, d_gq, d_gk, d_gkr, d_gfq, d_gfk, d_bf = _prep_bwd(
        *prep_args, dfq, dfk, dfv, dc_t.T, dmq, dmk, dmv)
    grads["mla_w_qb"], grads["mla_w_kvb"] = _deinterleave_grads(dwq, dwk, dwv)
    fold = lambda a, width: a.reshape(N_HEADS, width).sum(axis=0)[None]
    d_gq, d_gk = fold(d_gq, LANES), fold(d_gk, LANES)
    d_prep_small = [d_gql, d_gkvl, d_gq[:, :HEAD_DIM], d_gq[:, HEAD_DIM:HEAD_DIM + ROPE_DIM], d_gk[:, :HEAD_DIM],
                    d_gkr[:, HEAD_DIM:HEAD_DIM + ROPE_DIM], fold(d_gfq, HEAD_DIM), fold(d_gfk, HEAD_DIM), d_bf]
    dx1, dg_mix, nmix = _proj_bwd(dgates, dfox, dlat, wgate_t, wfox_t, wlat_t, x1, W["mix_norm"], dx2)
    dwin_t = _join_w_in(_tn_matmul(dgates, nmix, "d_w_in_gates"), _tn_matmul(dfox, nmix, "d_w_in_fox"),
                        _tn_matmul(dlat, nmix, "d_w_in_lat"))
    grads["w_in"] = jnp.pad(dwin_t.reshape(N_DEV, W_IN_PIECE, D_MODEL), ((0, 0), (0, W_IN_PIECE_PAD - W_IN_PIECE), (0, 0))
                            ).reshape(N_DEV * W_IN_PIECE_PAD, D_MODEL)
    grad_group_b = [n for n in group_b if n != "b_gate"]
    (dx0, dg_ffn1, da1, db1, h1, n1, dyh1), slots_b = _ffn_bwd(dx1, xs, W["ffn1_norm"], a1, b1, G["ffn1_w_gate"], G["ffn1_w_up"],
                                                              G["ffn1_w_down"], "ffn1_bwd", exchange=chip_sums(grad_group_b, "b"))
    slots.update(zip(grad_group_b, slots_b))
    grads["ffn1_w_gate"] = _tn_matmul(da1, n1, "ffn1_dgate")
    grads["ffn1_w_up"], got = _tn_matmul(db1, n1, "ffn1_dup", exchange=chip_sums(["ffn1_w_gate"], "a_gate"))
    slots["ffn1_w_gate"] = got[0]
    grads["ffn1_w_down"], got = _tn_matmul(h1, dyh1, "ffn1_ddown", exchange=chip_sums(["ffn1_w_up"], "a_up"))
    slots["ffn1_w_up"] = got[0]
    slots["ffn1_w_down"] = chip_sums(["ffn1_w_down"], "a_down").standalone("grads_ici_a_down")[0]

    small_parts = [dg_ffn1, dg_mix, dg_ffn2] + list(d_prep_small) + [dbg.reshape(1, 2 * D_MODEL), loss_vec]
    flat = jnp.concatenate([p.reshape(-1) for p in small_parts])
    n_flat = flat.shape[0]
    rows = -(-n_flat // (8 * 128)) * 8
    total = _all_reduce_small(jnp.pad(flat, (0, rows * 128 - n_flat)).reshape(rows, 128)).reshape(-1)
    offs, small_g = 0, {}
    for n in _SMALL:
        small_g[n] = total[offs:offs + W[n].shape[1]].reshape(W[n].shape)
        offs += W[n].shape[1]
    bg_full = total[offs:offs + 2 * D_MODEL].reshape(2, D_MODEL)
    offs += 2 * D_MODEL
    loss = (0.5 / D_MODEL) * jnp.sum(total[offs:offs + D_MODEL])
    _, _, _, me = _my_position()
    small_g["b_gate"] = lax.dynamic_slice_in_dim(bg_full, me * (D_MODEL // N_DEV), D_MODEL // N_DEV, axis=1)

    transposed_in_memory = ["ffn1_w_gate", "ffn1_w_up", "ffn2_w_gate", "ffn2_w_up", "w_in"]
    res = {}
    for n in _WEIGHTS:
        if n in small_g:
            res[n] = (small_g[n],) + tuple(_adamw(small_g[n], W[n], M[n], V[n], "adamw_" + n))
        elif n in transposed_in_memory:
            res[n] = tuple(o.T for o in _reduce_adamw(slots[n], W[n].T, M[n].T, V[n].T, transpose=False, name="adamw_" + n))
        else:
            res[n] = tuple(_reduce_adamw(slots[n], W[n], M[n], V[n], transpose=n in col_split, name="adamw_" + n))
    outs = [loss, dx0[None]]
    for k in range(4):
        outs += [res[n][k] if n in _SMALL else res[n][k][None] for n in _WEIGHTS]
    return tuple(outs)
```

```python
import functools

import jax
import jax.numpy as jnp
from jax import lax
from jax.experimental import pallas as pl
from jax.experimental.pallas import tpu as pltpu

F32 = jnp.float32
BF16 = jnp.bfloat16

D_MODEL = 1024
FFN_HIDDEN = 2816
N_HEADS = 8
HEAD_DIM = 64
ROPE_DIM = 32
ROPE_HALF = 16
Q_LORA = 192
KV_LORA = 128
ROPE_THETA = 10000.0
RMS_EPS = 1e-6
MLA_SCALE = (HEAD_DIM + ROPE_DIM) ** -0.5
FOX_SCALE = HEAD_DIM ** -0.5
MLA_QK_LANES = 128
ADAM_LR, ADAM_B1, ADAM_B2, ADAM_EPS, ADAM_WD, ADAM_STEP = 0.001, 0.9, 0.999, 1e-08, 0.01, 10
N_DEV = 8
N_CHIP = 4
W_IN_PIECE = 493
BF16_ROW_TILE = 16
W_IN_PIECE_PAD = 496
LAT_W = 512
LAT_Q, LAT_KV, LAT_KR, LAT_F = 0, 256, 384, 416
MASK_VALUE = -1e30
FIXED_SHIFT_MAX_BOUND = 30.0

TOK_TILE = 512
DW_TOK_TILE = 1024
PROJ_TILE = 512
PREP_TILE = 256
ATT_TILE = 512
ATT_COL_CHUNK = 256
FFN_HID_TILE = 1408
FFN_HID_SPLIT = ((0, 768), (768, 1408))

NT = (((1,), (1,)), ((), ()))
TN = (((0,), (0,)), ((), ()))
NN = (((1,), (0,)), ((), ()))
MESH = pl.DeviceIdType.MESH


def _dot(a, b, dims=NN):
    return lax.dot_general(a, b, dims, preferred_element_type=F32)


def _sds(shape, dtype):
    return jax.ShapeDtypeStruct(shape, dtype)


def _rms_fwd(x, g):
    r = lax.rsqrt(jnp.mean(x * x, axis=-1, keepdims=True) + RMS_EPS)
    return x * r * g, r


def _rms_bwd(dy, x, g, r):
    xn = x * r
    dyg = dy * g
    dx = r * (dyg - xn * jnp.mean(dyg * xn, axis=-1, keepdims=True))
    return dx, dy * xn


def _colsum(x):
    return jnp.sum(x, axis=0, keepdims=True)


def _full(shape):
    return pl.BlockSpec(shape, lambda *_: (0,) * len(shape))


def _tok(tm, n):
    return pl.BlockSpec((tm, n), lambda i, *_: (i, 0))


def _ffn_fwd(x, gain, wg_t, wu_t, wd, target=None, exchange=None):
    t = x.shape[0]
    tm = min(TOK_TILE, t)
    tf = FFN_HID_TILE
    n_t, n_f = t // tm, FFN_HIDDEN // tf
    with_loss = target is not None

    def body(*refs):
        if with_loss:
            x_ref, g_ref, wg_ref, wu_ref, wd_ref, t_ref, out_ref, a_ref, b_ref, lvec_ref, n_scr, acc = refs
        else:
            x_ref, g_ref, wg_ref, wu_ref, wd_ref, out_ref, a_ref, b_ref, n_scr, acc = refs
        i, j = pl.program_id(0), pl.program_id(1)

        @pl.when(j == 0)
        def _():
            xn, _ = _rms_fwd(x_ref[...], g_ref[...])
            n_scr[...] = xn.astype(BF16)
            acc[...] = jnp.zeros_like(acc)

        n = n_scr[...]
        a = _dot(n, wg_ref[...], NT)
        b = _dot(n, wu_ref[...], NT)
        a_ref[...] = a.astype(BF16)
        b_ref[...] = b.astype(BF16)
        h = (a * jax.nn.sigmoid(a)) * b
        acc[...] += _dot(h.astype(BF16), wd_ref[...])

        @pl.when(j == n_f - 1)
        def _():
            y = x_ref[...] + 0.5 * acc[...]
            if with_loss:
                diff = y - t_ref[...]
                out_ref[...] = diff * (1.0 / D_MODEL)
                sq = _colsum(diff * diff)

                @pl.when(i == 0)
                def _():
                    lvec_ref[...] = sq

                @pl.when(i > 0)
                def _():
                    lvec_ref[...] += sq
            else:
                out_ref[...] = y

    wspec = pl.BlockSpec((tf, D_MODEL), lambda i, j: (j, 0))
    hspec = pl.BlockSpec((tm, tf), lambda i, j: (i, j))
    in_specs = [_tok(tm, D_MODEL), _full((1, D_MODEL)), wspec, wspec, wspec]
    out_specs = [_tok(tm, D_MODEL), hspec, hspec]
    out_shape = [_sds((t, D_MODEL), F32), _sds((t, FFN_HIDDEN), BF16), _sds((t, FFN_HIDDEN), BF16)]
    args = [x, gain, wg_t, wu_t, wd]
    if with_loss:
        in_specs.append(_tok(tm, D_MODEL))
        out_specs.append(_full((1, D_MODEL)))
        out_shape.append(_sds((1, D_MODEL), F32))
        args.append(target)
    return _gridded_call(
        body, name="ffn_fwd_loss" if with_loss else "ffn_fwd", grid=(n_t, n_f), in_specs=in_specs, out_specs=out_specs,
        out_shape=out_shape, scratch_shapes=[pltpu.VMEM((tm, D_MODEL), BF16), pltpu.VMEM((tm, D_MODEL), F32)],
        args=args, exchange=exchange)


def _ffn_bwd(dy, x, gain, a, b, wg_t, wu_t, wd, name, exchange=None):
    t = x.shape[0]
    tm = min(TOK_TILE, t)
    tf = FFN_HID_TILE
    n_t, n_f = t // tm, FFN_HIDDEN // tf

    def body(dy_ref, x_ref, g_ref, a_ref, b_ref, wg_ref, wu_ref, wd_ref,
             dx_ref, dg_ref, da_ref, db_ref, h_ref, n_ref, dyh_ref, acc):
        i, j = pl.program_id(0), pl.program_id(1)

        @pl.when(j == 0)
        def _():
            xn, _ = _rms_fwd(x_ref[...], g_ref[...])
            n_ref[...] = xn.astype(BF16)
            dyh_ref[...] = (0.5 * dy_ref[...]).astype(BF16)
            acc[...] = jnp.zeros_like(acc)

        dyh = dyh_ref[...]
        for lo, hi in FFN_HID_SPLIT:
            dh = _dot(dyh, wd_ref[lo:hi, :], NT)
            av = a_ref[:, lo:hi].astype(F32)
            bv = b_ref[:, lo:hi].astype(F32)
            s = jax.nn.sigmoid(av)
            silu = av * s
            da = (dh * bv * (s * (1.0 + av * (1.0 - s)))).astype(BF16)
            db = (dh * silu).astype(BF16)
            da_ref[:, lo:hi] = da
            db_ref[:, lo:hi] = db
            h_ref[:, lo:hi] = (silu * bv).astype(BF16)
            acc[...] += _dot(da, wg_ref[lo:hi, :]) + _dot(db, wu_ref[lo:hi, :])

        @pl.when(j == n_f - 1)
        def _():
            xv, g = x_ref[...], g_ref[...]
            r = lax.rsqrt(jnp.mean(xv * xv, axis=-1, keepdims=True) + RMS_EPS)
            dx, dg_rows = _rms_bwd(acc[...], xv, g, r)
            dx_ref[...] = dy_ref[...] + dx
            dg = _colsum(dg_rows)

            @pl.when(i == 0)
            def _():
                dg_ref[...] = dg

            @pl.when(i > 0)
            def _():
                dg_ref[...] += dg

    wspec = pl.BlockSpec((tf, D_MODEL), lambda i, j: (j, 0))
    hspec = pl.BlockSpec((tm, tf), lambda i, j: (i, j))
    tok = _tok(tm, D_MODEL)
    hid = _sds((t, FFN_HIDDEN), BF16)
    return _gridded_call(
        body, name=name, grid=(n_t, n_f),
        in_specs=[tok, tok, _full((1, D_MODEL)), hspec, hspec, wspec, wspec, wspec],
        out_specs=[tok, _full((1, D_MODEL)), hspec, hspec, hspec, tok, tok],
        out_shape=[_sds((t, D_MODEL), F32), _sds((1, D_MODEL), F32), hid, hid, hid,
                   _sds((t, D_MODEL), BF16), _sds((t, D_MODEL), BF16)],
        scratch_shapes=[pltpu.VMEM((tm, D_MODEL), F32)], args=[dy, x, gain, a, b, wg_t, wu_t, wd], exchange=exchange)


def _tn_matmul(a, b, name, exchange=None):
    t, m = a.shape
    n = b.shape[1]
    tk = min(DW_TOK_TILE, t)
    n_k = t // tk

    def body(a_ref, b_ref, o_ref, acc):
        k = pl.program_id(0)
        p = _dot(a_ref[...], b_ref[...], TN)

        @pl.when(k == 0)
        def _():
            acc[...] = p

        @pl.when(k > 0)
        def _():
            acc[...] += p

        @pl.when(k == n_k - 1)
        def _():
            o_ref[...] = acc[...].astype(BF16)

    (out,), got = _gridded_call(
        body, name=name, grid=(n_k,), in_specs=[_tok(tk, m), _tok(tk, n)], out_specs=[_full((m, n))],
        out_shape=[_sds((m, n), BF16)], scratch_shapes=[pltpu.VMEM((m, n), F32)], args=[a, b], exchange=exchange)
    return out if exchange is None else (out, got)


def _proj_fwd(x, gain, wgate_t, wfox_t, wlat_t):
    t = x.shape[0]
    tm = min(PROJ_TILE, t)

    def body(x_ref, g_ref, wg_ref, wf_ref, wl_ref, og_ref, of_ref, ol_ref):
        xn, _ = _rms_fwd(x_ref[...], g_ref[...])
        n = xn.astype(BF16)
        og_ref[...] = _dot(n, wg_ref[...], NT)
        of_ref[...] = _dot(n, wf_ref[...], NT)
        ol_ref[...] = _dot(n, wl_ref[...], NT)

    return pl.pallas_call(
        body, name="proj_fwd", grid=(t // tm,),
        in_specs=[_tok(tm, D_MODEL), _full((1, D_MODEL)), _full(wgate_t.shape), _full(wfox_t.shape), _full(wlat_t.shape)],
        out_specs=[_tok(tm, 2 * D_MODEL), _tok(tm, 3 * 512), _tok(tm, LAT_W)],
        out_shape=[_sds((t, 2 * D_MODEL), F32), _sds((t, 3 * 512), F32), _sds((t, LAT_W), F32)],
        compiler_params=pltpu.CompilerParams(dimension_semantics=("arbitrary",)),
    )(x, gain, wgate_t, wfox_t, wlat_t)


def _proj_bwd(dgates, dfox, dlat, wgate_t, wfox_t, wlat_t, x, gain, dres):
    t = x.shape[0]
    tm = min(PROJ_TILE, t)

    def body(dg_ref, df_ref, dl_ref, wg_ref, wf_ref, wl_ref, x_ref, g_ref, dres_ref, dx_ref, dgain_ref, n_ref):
        i = pl.program_id(0)
        dn = _dot(dg_ref[...], wg_ref[...]) + _dot(df_ref[...], wf_ref[...]) + _dot(dl_ref[...], wl_ref[...])
        xv, g = x_ref[...], g_ref[...]
        xn, r = _rms_fwd(xv, g)
        n_ref[...] = xn.astype(BF16)
        dx, dg_rows = _rms_bwd(dn, xv, g, r)
        dx_ref[...] = dres_ref[...] + dx
        dgn = _colsum(dg_rows)

        @pl.when(i == 0)
        def _():
            dgain_ref[...] = dgn

        @pl.when(i > 0)
        def _():
            dgain_ref[...] += dgn

    tok = _tok(tm, D_MODEL)
    return pl.pallas_call(
        body, name="proj_bwd", grid=(t // tm,),
        in_specs=[_tok(tm, 2 * D_MODEL), _tok(tm, 3 * 512), _tok(tm, LAT_W), _full(wgate_t.shape), _full(wfox_t.shape),
                  _full(wlat_t.shape), tok, _full((1, D_MODEL)), tok],
        out_specs=[tok, _full((1, D_MODEL)), tok],
        out_shape=[_sds((t, D_MODEL), F32), _sds((1, D_MODEL), F32), _sds((t, D_MODEL), BF16)],
        compiler_params=pltpu.CompilerParams(dimension_semantics=("arbitrary",)),
    )(dgates, dfox, dlat, wgate_t, wfox_t, wlat_t, x, gain, dres)


def _tri(n, lower):
    r = lax.broadcasted_iota(jnp.int32, (n, n), 0)
    c = lax.broadcasted_iota(jnp.int32, (n, n), 1)
    return ((c <= r) if lower else (c >= r)).astype(F32)


def _log_sigmoid(z):
    return jnp.minimum(z, 0.0) - jnp.log1p(jnp.exp(-jnp.abs(z)))


N_GROUPS = 16
LANES = MLA_QK_LANES


def _split_dot(a, g):
    hi = a.astype(BF16)
    lo = (a - hi.astype(F32)).astype(BF16)
    return _dot(hi, g) + _dot(lo, g)


def _rope_fwd(y, c, s1, s2):
    return y * c + pltpu.roll(y, LANES - ROPE_HALF, 1) * s1 + pltpu.roll(y, ROPE_HALF, 1) * s2


def _rope_bwd(do, c, s1, s2):
    return do * c + pltpu.roll(do * s1, ROPE_HALF, 1) + pltpu.roll(do * s2, LANES - ROPE_HALF, 1)


def _prep_tables(cos, sin):
    t = cos.shape[0]
    z = lambda n: jnp.zeros((t, n), F32)
    c = jnp.concatenate([jnp.ones((t, HEAD_DIM), F32), cos, cos, z(LANES - HEAD_DIM - ROPE_DIM)], axis=1)
    s1 = jnp.concatenate([z(HEAD_DIM), -sin, z(LANES - HEAD_DIM - ROPE_HALF)], axis=1)
    s2 = jnp.concatenate([z(HEAD_DIM + ROPE_HALF), sin, z(LANES - HEAD_DIM - ROPE_DIM)], axis=1)
    lane = jnp.arange(LANES)
    rope_mask = ((lane >= HEAD_DIM) & (lane < HEAD_DIM + ROPE_DIM)).astype(F32)[None, :]
    return c, s1, s2, rope_mask


def _group_matrices():
    lane = jnp.arange(N_HEADS * LANES)
    head, d = (lane // LANES)[:, None], (lane % LANES)[:, None]
    col = jnp.arange(N_GROUPS)[None, :]
    g_mla = ((col == head) & (d < HEAD_DIM)) | ((col == N_HEADS + head) & (d >= HEAD_DIM) & (d < HEAD_DIM + ROPE_DIM))
    g_fox = col == (jnp.arange(N_HEADS * HEAD_DIM) // HEAD_DIM)[:, None]
    inv_mla = jnp.concatenate([jnp.full((1, N_HEADS), 1.0 / HEAD_DIM, F32), jnp.full((1, N_HEADS), 1.0 / ROPE_DIM, F32)], axis=1)
    inv_fox = jnp.full((1, N_GROUPS), 1.0 / HEAD_DIM, F32)
    return g_mla.astype(BF16), g_mla.T.astype(BF16), inv_mla, g_fox.astype(BF16), g_fox.T.astype(BF16), inv_fox


def _interleave_weights(wqb_t, wkvb_t):
    wq = jnp.pad(wqb_t.reshape(N_HEADS, HEAD_DIM + ROPE_DIM, Q_LORA), ((0, 0), (0, LANES - HEAD_DIM - ROPE_DIM), (0, 0)))
    kv = wkvb_t.reshape(N_HEADS, 2, HEAD_DIM, KV_LORA)
    wk = jnp.pad(kv[:, 0], ((0, 0), (0, LANES - HEAD_DIM), (0, 0)))
    return wq.reshape(N_HEADS * LANES, Q_LORA), wk.reshape(N_HEADS * LANES, KV_LORA), kv[:, 1].reshape(N_HEADS * HEAD_DIM, KV_LORA)


def _deinterleave_grads(dwq, dwk, dwv):
    dq = dwq.reshape(N_HEADS, LANES, Q_LORA)[:, :HEAD_DIM + ROPE_DIM].reshape(N_HEADS * (HEAD_DIM + ROPE_DIM), Q_LORA)
    dk = dwk.reshape(N_HEADS, LANES, KV_LORA)[:, :HEAD_DIM]
    dkv = jnp.stack([dk, dwv.reshape(N_HEADS, HEAD_DIM, KV_LORA)], axis=1)
    return dq, dkv.reshape(N_HEADS * 2 * HEAD_DIM, KV_LORA)


def _head_gains(g_qn, g_qr, g_kn, g_kr, g_fq, g_fk):
    z = lambda n: jnp.zeros((1, n), F32)
    gq = jnp.concatenate([g_qn, g_qr, z(LANES - HEAD_DIM - ROPE_DIM)], axis=1)
    gk = jnp.concatenate([g_kn, z(LANES - HEAD_DIM)], axis=1)
    gkr = jnp.concatenate([z(HEAD_DIM), g_kr, z(LANES - HEAD_DIM - ROPE_DIM)], axis=1)
    return jnp.tile(gq, (1, N_HEADS)), jnp.tile(gk, (1, N_HEADS)), gkr, jnp.tile(g_fq, (1, N_HEADS)), jnp.tile(g_fk, (1, N_HEADS))


def _group_rms(x, g, g_t, inv):
    r = lax.rsqrt(_split_dot(x * x, g) * inv + RMS_EPS)
    return _split_dot(r, g_t)


def _value_layout(wv):
    lane = jnp.arange(N_HEADS * LANES)
    src = (lane // LANES) * HEAD_DIM + lane % LANES
    place = (jnp.arange(N_HEADS * HEAD_DIM)[:, None] == src[None, :]) & (lane % LANES < HEAD_DIM)[None, :]
    ones = (lane % LANES == HEAD_DIM).astype(F32)[None, :]
    wv_il = jnp.pad(wv.reshape(N_HEADS, HEAD_DIM, KV_LORA), ((0, 0), (0, LANES - HEAD_DIM), (0, 0)))
    return place.astype(BF16), ones, wv_il.reshape(N_HEADS * LANES, KV_LORA)


def _prep_fwd(fox, lat, tables, mats, wq, wk, wv, g_qlat, g_kvlat, gains, b_f, v_place, v_ones, wv_il):
    t = fox.shape[0]
    tm = min(PREP_TILE, t)
    c_tab, s1_tab, s2_tab, rope_mask = tables

    def body(fox_ref, lat_ref, c_ref, s1_ref, s2_ref, rm_ref, gm_ref, gmt_ref, im_ref, gf_ref, gft_ref, if_ref,
             wq_ref, wk_ref, wv_ref, gql_ref, gkvl_ref, gq_ref, gk_ref, gkr_ref, gfq_ref, gfk_ref, bf_ref,
             vp_ref, vo_ref, wvil_ref,
             fq_ref, fk_ref, fv_ref, cc_ref, mq_ref, mk_ref, mv_ref, fqn_ref, fkmax_ref, mqn_ref, mkmax_ref,
             fv1_ref, mv1_ref, carry):
        i = pl.program_id(0)
        ct, s1, s2 = c_ref[...], s1_ref[...], s2_ref[...]

        @pl.when(i == 0)
        def _():
            carry[...] = jnp.zeros_like(carry)
            fkmax_ref[...] = jnp.zeros_like(fkmax_ref)
            mkmax_ref[...] = jnp.zeros_like(mkmax_ref)

        def square_sums(ref, g):
            v = ref[...].astype(F32)
            return _split_dot(v * v, g)

        xq = fox_ref[:, 0:512]
        fq_ref[...] = (xq * _group_rms(xq, gf_ref[...], gft_ref[...], if_ref[...]) * (gfq_ref[...] * FOX_SCALE)).astype(BF16)
        xk = fox_ref[:, 512:1024]
        fk_ref[...] = (xk * _group_rms(xk, gf_ref[...], gft_ref[...], if_ref[...]) * gfk_ref[...]).astype(BF16)
        fv_b = fox_ref[:, 1024:1536].astype(BF16)
        fv_ref[...] = fv_b
        fv1_ref[...] = (_dot(fv_b, vp_ref[...]) + vo_ref[...]).astype(BF16)
        fqn_ref[...] = square_sums(fq_ref, gf_ref[...])
        fkmax_ref[...] = jnp.maximum(fkmax_ref[...], jnp.max(square_sums(fk_ref, gf_ref[...]), axis=0, keepdims=True))


        logf = _log_sigmoid(lat_ref[:, LAT_F:LAT_F + N_HEADS] + bf_ref[...])
        cc_ref[...] = jnp.dot(_tri(tm, True), logf, precision=lax.Precision.HIGHEST, preferred_element_type=F32) + carry[...]
        carry[...] += _colsum(logf)

        qlat_n, _ = _rms_fwd(lat_ref[:, LAT_Q:LAT_Q + Q_LORA], gql_ref[...])
        p = _dot(qlat_n.astype(BF16), wq_ref[...], NT)
        y = p * _group_rms(p, gm_ref[...], gmt_ref[...], im_ref[...]) * (gq_ref[...] * MLA_SCALE)
        for h in range(N_HEADS):
            w = slice(h * LANES, (h + 1) * LANES)
            mq_ref[:, w] = _rope_fwd(y[:, w], ct, s1, s2).astype(BF16)

        kv_n, _ = _rms_fwd(lat_ref[:, LAT_KV:LAT_KV + KV_LORA], gkvl_ref[...])
        kv_b = kv_n.astype(BF16)
        pk = _dot(kv_b, wk_ref[...], NT)
        kn = pk * _group_rms(pk, gm_ref[...], gmt_ref[...], im_ref[...]) * gk_ref[...]
        rm = rm_ref[...]
        kr = pltpu.roll(lat_ref[:, LAT_KR:LAT_KR + LANES], HEAD_DIM, 1) * rm
        rr = lax.rsqrt(jnp.sum(kr * kr, axis=1, keepdims=True) * (1.0 / ROPE_DIM) + RMS_EPS)
        okr = _rope_fwd(kr * rr * gkr_ref[...], ct * rm, s1, s2)
        for h in range(N_HEADS):
            w = slice(h * LANES, (h + 1) * LANES)
            mk_ref[:, w] = (kn[:, w] + okr).astype(BF16)
        mv_ref[...] = _dot(kv_b, wv_ref[...], NT).astype(BF16)
        mv1_ref[...] = (_dot(kv_b, wvil_ref[...], NT).astype(BF16).astype(F32) + vo_ref[...]).astype(BF16)
        mqn_ref[...] = square_sums(mq_ref, gm_ref[...])
        mkmax_ref[...] = jnp.maximum(mkmax_ref[...], jnp.max(square_sums(mk_ref, gm_ref[...]), axis=0, keepdims=True))

    consts = [rope_mask, *mats, wq, wk, wv, g_qlat, g_kvlat, *gains, b_f, v_place, v_ones, wv_il]
    t512, tl, tg, g1 = _tok(tm, 512), _tok(tm, LANES), _tok(tm, N_GROUPS), _full((1, N_GROUPS))
    t1024 = _tok(tm, N_HEADS * LANES)
    return pl.pallas_call(
        body, name="prep_fwd", grid=(t // tm,),
        in_specs=[_tok(tm, 1536), _tok(tm, LAT_W), tl, tl, tl] + [_full(a.shape) for a in consts],
        out_specs=[t512, t512, t512, _tok(tm, N_HEADS), t1024, t1024, t512, tg, g1, tg, g1, t1024, t1024],
        out_shape=[_sds((t, 512), BF16), _sds((t, 512), BF16), _sds((t, 512), BF16), _sds((t, N_HEADS), F32),
                   _sds((t, N_HEADS * LANES), BF16), _sds((t, N_HEADS * LANES), BF16), _sds((t, 512), BF16),
                   _sds((t, N_GROUPS), F32), _sds((1, N_GROUPS), F32), _sds((t, N_GROUPS), F32), _sds((1, N_GROUPS), F32)]
        + [_sds((t, N_HEADS * LANES), BF16)] * 2,
        scratch_shapes=[pltpu.VMEM((1, N_HEADS), F32)],
        compiler_params=pltpu.CompilerParams(dimension_semantics=("arbitrary",)),
    )(fox, lat, c_tab, s1_tab, s2_tab, *consts)


def _prep_bwd(fox, lat, tables, mats, wq, wk, wv, g_qlat, g_kvlat, gains, b_f, dfq, dfk, dfv, dc, dmq, dmk, dmv):
    t = fox.shape[0]
    tm = min(PREP_TILE, t)
    n_t = t // tm
    c_tab, s1_tab, s2_tab, rope_mask = tables

    def body(fox_ref, lat_ref, c_ref, s1_ref, s2_ref, rm_ref, gm_ref, gmt_ref, im_ref, gf_ref, gft_ref, if_ref,
             wq_ref, wk_ref, wv_ref, gql_ref, gkvl_ref, gq_ref, gk_ref, gkr_ref, gfq_ref, gfk_ref, bf_ref,
             dfq_ref, dfk_ref, dfv_ref, dc_ref, dmq_ref, dmk_ref, dmv_ref,
             dfox_ref, dlat_ref, dwq_ref, dwk_ref, dwv_ref, o_gql, o_gkvl, o_gq, o_gk, o_gkr, o_gfq, o_gfk, o_bf,
             carry, lscr, wq_acc, wk_acc, wv_acc, dyscr):
        i = pl.program_id(0)
        ct, s1, s2, rm = c_ref[...], s1_ref[...], s2_ref[...], rm_ref[...]

        @pl.when(i == 0)
        def _():
            for ref in [carry, wq_acc, wk_acc, wv_acc, o_gql, o_gkvl, o_gq, o_gk, o_gkr, o_gfq, o_gfk, o_bf]:
                ref[...] = jnp.zeros_like(ref)

        def group_rms_bwd(dy, x, r, gain, g, g_t, inv):
            xn = x * r
            dyg = dy * gain
            mu = _split_dot(_split_dot(dyg * xn, g) * inv, g_t)
            return r * (dyg - xn * mu), dy * xn

        gf, gft, invf = gf_ref[...], gft_ref[...], if_ref[...]
        xq = fox_ref[:, 0:512]
        dxq, rows = group_rms_bwd(dfq_ref[...] * FOX_SCALE, xq, _group_rms(xq, gf, gft, invf), gfq_ref[...], gf, gft, invf)
        dfox_ref[:, 0:512] = dxq.astype(BF16)
        o_gfq[...] += _colsum(rows)
        xk = fox_ref[:, 512:1024]
        dxk, rows = group_rms_bwd(dfk_ref[...], xk, _group_rms(xk, gf, gft, invf), gfk_ref[...], gf, gft, invf)
        dfox_ref[:, 512:1024] = dxk.astype(BF16)
        o_gfk[...] += _colsum(rows)
        dfox_ref[:, 1024:1536] = dfv_ref[...].astype(BF16)

        lscr[...] = jnp.zeros_like(lscr)

        xql = lat_ref[:, LAT_Q:LAT_Q + Q_LORA]
        qlat_n, r_ql = _rms_fwd(xql, gql_ref[...])
        qlat_b = qlat_n.astype(BF16)
        gm, gmt, invm = gm_ref[...], gmt_ref[...], im_ref[...]
        p = _dot(qlat_b, wq_ref[...], NT)
        for h in range(N_HEADS):
            w = slice(h * LANES, (h + 1) * LANES)
            dyscr[:, w] = _rope_bwd(dmq_ref[:, w], ct, s1, s2) * MLA_SCALE
        dp, rows = group_rms_bwd(dyscr[...], p, _group_rms(p, gm, gmt, invm), gq_ref[...], gm, gmt, invm)
        o_gq[...] += _colsum(rows)
        dp_b = dp.astype(BF16)
        wq_acc[...] += _dot(dp_b, qlat_b, TN)
        dxql, rows = _rms_bwd(_dot(dp_b, wq_ref[...]), xql, gql_ref[...], r_ql)
        lscr[:, LAT_Q:LAT_Q + Q_LORA] = dxql
        o_gql[...] += _colsum(rows)

        xkv = lat_ref[:, LAT_KV:LAT_KV + KV_LORA]
        kv_n, r_kv = _rms_fwd(xkv, gkvl_ref[...])
        kv_b = kv_n.astype(BF16)
        pk = _dot(kv_b, wk_ref[...], NT)
        dpk, rows = group_rms_bwd(dmk_ref[...], pk, _group_rms(pk, gm, gmt, invm), gk_ref[...], gm, gmt, invm)
        o_gk[...] += _colsum(rows)
        dpk_b = dpk.astype(BF16)
        dv_b = dmv_ref[...].astype(BF16)
        wk_acc[...] += _dot(dpk_b, kv_b, TN)
        wv_acc[...] += _dot(dv_b, kv_b, TN)
        dxkv, rows = _rms_bwd(_dot(dpk_b, wk_ref[...]) + _dot(dv_b, wv_ref[...]), xkv, gkvl_ref[...], r_kv)
        lscr[:, LAT_KV:LAT_KV + KV_LORA] = dxkv
        o_gkvl[...] += _colsum(rows)

        dokr = dmk_ref[:, 0:LANES]
        for h in range(1, N_HEADS):
            dokr = dokr + dmk_ref[:, h * LANES:(h + 1) * LANES]
        dykr = _rope_bwd(dokr * rm, ct * rm, s1, s2)
        kr = pltpu.roll(lat_ref[:, LAT_KR:LAT_KR + LANES], HEAD_DIM, 1) * rm
        rr = lax.rsqrt(jnp.sum(kr * kr, axis=1, keepdims=True) * (1.0 / ROPE_DIM) + RMS_EPS)
        krn = kr * rr
        dyg = dykr * gkr_ref[...]
        dkr = rr * (dyg - krn * (jnp.sum(dyg * krn, axis=1, keepdims=True) * (1.0 / ROPE_DIM)))
        o_gkr[...] += _colsum(dykr * krn)
        lscr[:, LAT_KR:LAT_KR + LANES] = pltpu.roll(dkr, LANES - HEAD_DIM, 1)

        dcv = dc_ref[...]
        dlogf = jnp.dot(_tri(tm, False), dcv, precision=lax.Precision.HIGHEST, preferred_element_type=F32) + carry[...]
        carry[...] += _colsum(dcv)
        dz = dlogf * jax.nn.sigmoid(-(lat_ref[:, LAT_F:LAT_F + N_HEADS] + bf_ref[...]))
        lscr[:, LAT_F:LAT_F + N_HEADS] = dz
        o_bf[...] += _colsum(dz)

        dlat_ref[...] = lscr[...].astype(BF16)

        @pl.when(i == n_t - 1)
        def _():
            dwq_ref[...] = wq_acc[...].astype(BF16)
            dwk_ref[...] = wk_acc[...].astype(BF16)
            dwv_ref[...] = wv_acc[...].astype(BF16)

    consts = [rope_mask, *mats, wq, wk, wv, g_qlat, g_kvlat, *gains, b_f]

    def rtok(n):
        return pl.BlockSpec((tm, n), lambda i: (n_t - 1 - i, 0))

    sums = [(1, Q_LORA), (1, KV_LORA), (1, N_HEADS * LANES), (1, N_HEADS * LANES), (1, LANES), (1, 512), (1, 512), (1, N_HEADS)]
    return pl.pallas_call(
        body, name="prep_bwd", grid=(n_t,),
        in_specs=[rtok(1536), rtok(LAT_W), rtok(LANES), rtok(LANES), rtok(LANES)] + [_full(a.shape) for a in consts]
        + [rtok(512), rtok(512), rtok(512), rtok(N_HEADS), rtok(N_HEADS * LANES), rtok(N_HEADS * LANES), rtok(512)],
        out_specs=[rtok(1536), rtok(LAT_W), _full(wq.shape), _full(wk.shape), _full(wv.shape)] + [_full(s) for s in sums],
        out_shape=[_sds((t, 1536), BF16), _sds((t, LAT_W), BF16), _sds(wq.shape, BF16), _sds(wk.shape, BF16), _sds(wv.shape, BF16)]
        + [_sds(s, F32) for s in sums],
        scratch_shapes=[pltpu.VMEM((1, N_HEADS), F32), pltpu.VMEM((tm, LAT_W), F32), pltpu.VMEM(wq.shape, F32),
                        pltpu.VMEM(wk.shape, F32), pltpu.VMEM(wv.shape, F32), pltpu.VMEM((tm, N_HEADS * LANES), F32)],
        compiler_params=pltpu.CompilerParams(dimension_semantics=("arbitrary",)),
    )(fox, lat, c_tab, s1_tab, s2_tab, *consts, dfq, dfk, dfv, dc, dmq, dmk, dmv)


def _causal_pairs(n, query_major):
    pairs = [(i, j) for i in range(n) for j in range(i + 1)] if query_major else [(i, j) for j in range(n) for i in range(j, n)]
    return jnp.asarray([p[0] for p in pairs], jnp.int32), jnp.asarray([p[1] for p in pairs], jnp.int32)


def _logit_bound(q_sq, k_sq_max):
    qn = q_sq[:, :N_HEADS] + q_sq[:, N_HEADS:]
    kmax = k_sq_max[:, :N_HEADS] + k_sq_max[:, N_HEADS:]
    bound = jnp.sqrt(qn * kmax) * (1.0 + 2.0 ** -10) + 2.0 ** -10
    flag = (jnp.max(bound) <= FIXED_SHIFT_MAX_BOUND).astype(F32).reshape(1, 1)
    return bound, flag


def _attn_fwd(q, k, v, bound, fixed_ok, c, c_t, *, lanes, name, exchange=None):
    t = q.shape[0]
    tq = min(ATT_TILE, t)
    n_q = t // tq
    hd = HEAD_DIM
    ch = min(ATT_COL_CHUNK, tq)
    decay = c is not None

    def body(qi_ref, kj_ref, *refs):
        if decay:
            q_ref, k_ref, v_ref, b_ref, ok_ref, c_ref, ct_ref, o_ref, o32_ref, lse_ref, m_scr, acc = refs
        else:
            q_ref, k_ref, v_ref, b_ref, ok_ref, o_ref, o32_ref, lse_ref, m_scr, acc = refs
        i, j = qi_ref[pl.program_id(0)], kj_ref[pl.program_id(0)]
        fixed = ok_ref[0, 0] > 0.5

        @pl.when(j == 0)
        def _():
            m_scr[...] = jnp.full_like(m_scr, MASK_VALUE)
            acc[...] = jnp.zeros_like(acc)

        def fixed_step(diagonal):
            for h in range(N_HEADS):
                wl = slice(h * lanes, (h + 1) * lanes)
                wv = slice(h * LANES, (h + 1) * LANES)
                qh = q_ref[:, wl]
                row = (c_ref[:, h:h + 1] - b_ref[:, h:h + 1]) if decay else -b_ref[:, h:h + 1]
                o_hi = jnp.zeros((tq, LANES), F32)
                o_lo = jnp.zeros((tq, LANES), F32)
                for cc in range(tq // ch):
                    cols = slice(cc * ch, (cc + 1) * ch)
                    s = _dot(qh, k_ref[cols, wl], NT)
                    s = s + ((row - ct_ref[h:h + 1, cols]) if decay else row)
                    if diagonal:
                        keep = (lax.broadcasted_iota(jnp.int32, (tq, ch), 0)
                                >= lax.broadcasted_iota(jnp.int32, (tq, ch), 1) + cc * ch)
                        s = jnp.where(keep, s, MASK_VALUE)
                    p = jnp.exp(s)
                    p_b = p.astype(BF16)
                    o_hi = o_hi + _dot(p_b, v_ref[cols, wv])
                    if decay:
                        o_lo = o_lo + _dot((p - p_b.astype(F32)).astype(BF16), v_ref[cols, wv])
                acc[0, :, wv] += o_hi
                if decay:
                    acc[1, :, wv] += o_lo

        def step(diagonal):
            if diagonal:
                keep = lax.broadcasted_iota(jnp.int32, (tq, tq), 0) >= lax.broadcasted_iota(jnp.int32, (tq, tq), 1)
            for h in range(N_HEADS):
                s = _dot(q_ref[:, h * lanes:(h + 1) * lanes], k_ref[:, h * lanes:(h + 1) * lanes], NT)
                if decay:
                    s = s + (c_ref[:, h:h + 1] - ct_ref[h:h + 1, :])
                if diagonal:
                    s = jnp.where(keep, s, MASK_VALUE)
                m_prev = m_scr[h]
                m_new = jnp.maximum(m_prev, jnp.max(s, axis=1, keepdims=True))
                alpha = jnp.exp(m_prev - m_new)
                p = jnp.exp(s - m_new)
                wv = slice(h * LANES, (h + 1) * LANES)
                p_b = p.astype(BF16)
                acc[0, :, wv] = alpha * acc[0, :, wv] + _dot(p_b, v_ref[:, wv])
                if decay:
                    acc[1, :, wv] = alpha * acc[1, :, wv] + _dot((p - p_b.astype(F32)).astype(BF16), v_ref[:, wv])
                m_scr[h] = m_new

        for diagonal, here in ((False, j < i), (True, j == i)):
            @pl.when(here & fixed)
            def _():
                fixed_step(diagonal)

            @pl.when(here & jnp.logical_not(fixed))
            def _():
                step(diagonal)

        @pl.when(j == i)
        def _():
            for h in range(N_HEADS):
                w = slice(h * hd, (h + 1) * hd)
                val, one = slice(h * LANES, h * LANES + hd), slice(h * LANES + hd, h * LANES + hd + 1)
                l = (acc[0, :, one] + acc[1, :, one]) if decay else acc[0, :, one]
                o_ref[:, w] = (acc[0, :, val] / l).astype(BF16)
                o32_ref[:, w] = ((acc[0, :, val] + acc[1, :, val]) if decay else acc[0, :, val]) / l
                lse_ref[:, h:h + 1] = jnp.where(fixed, b_ref[:, h:h + 1], m_scr[h]) + jnp.log(l)

    qspec = lambda n: pl.BlockSpec((tq, n), lambda s, qi, kj: (qi[s], 0))
    kspec = lambda n: pl.BlockSpec((tq, n), lambda s, qi, kj: (kj[s], 0))
    in_specs = [qspec(N_HEADS * lanes), kspec(N_HEADS * lanes), kspec(N_HEADS * LANES), qspec(N_HEADS),
                pl.BlockSpec(memory_space=pltpu.SMEM)]
    args = [q, k, v, bound, fixed_ok]
    if decay:
        in_specs += [qspec(N_HEADS), pl.BlockSpec((N_HEADS, tq), lambda s, qi, kj: (0, kj[s]))]
        args += [c, c_t]
    pairs = _causal_pairs(n_q, query_major=True)
    return _gridded_call(
        body, name=name, grid=(pairs[0].shape[0],), in_specs=in_specs, out_specs=[qspec(512), qspec(512), qspec(N_HEADS)],
        out_shape=[_sds((t, 512), BF16), _sds((t, 512), F32), _sds((t, N_HEADS), F32)],
        scratch_shapes=[pltpu.VMEM((N_HEADS, tq, 1), F32), pltpu.VMEM((2, tq, N_HEADS * LANES), F32)],
        args=args, exchange=exchange, prefetch=pairs)


def _attn_bwd(q, k, v, do, delta, lse, c, c_t, *, lanes, name, exchange=None):
    t = q.shape[0]
    tq = min(ATT_TILE, t)
    n_q = t // tq
    hd = HEAD_DIM
    decay = c is not None

    pairs = _causal_pairs(n_q, query_major=False)
    n_steps = pairs[0].shape[0]

    def body(qi_ref, kj_ref, *refs):
        if decay:
            q_ref, k_ref, v_ref, do_ref, delta_ref, lse_ref, c_ref, ct_ref, dq_hbm, dk_ref, dv_ref, dct_ref, dq_ref = refs
        else:
            q_ref, k_ref, v_ref, do_ref, delta_ref, lse_ref, dq_hbm, dk_ref, dv_ref, dq_ref = refs
        step_id = pl.program_id(0)
        i, j = qi_ref[step_id], kj_ref[step_id]

        @pl.when(step_id == 0)
        def _():
            dq_ref[...] = jnp.zeros_like(dq_ref)

        @pl.when(i == j)
        def _():
            dk_ref[...] = jnp.zeros_like(dk_ref)
            dv_ref[...] = jnp.zeros_like(dv_ref)
            if decay:
                dct_ref[...] = jnp.zeros_like(dct_ref)

        def step(diagonal):
            if diagonal:
                keep = lax.broadcasted_iota(jnp.int32, (tq, tq), 0) >= lax.broadcasted_iota(jnp.int32, (tq, tq), 1)
            rows = pl.ds(pl.multiple_of(i * tq, tq), tq)
            for h in range(N_HEADS):
                wl = slice(h * lanes, (h + 1) * lanes)
                w = slice(h * hd, (h + 1) * hd)
                qh, kh = q_ref[:, wl], k_ref[:, wl]
                s = _dot(qh, kh, NT)
                if decay:
                    s = s + (c_ref[:, h:h + 1] - ct_ref[h:h + 1, :])
                if diagonal:
                    s = jnp.where(keep, s, MASK_VALUE)
                p = jnp.exp(s - lse_ref[:, h:h + 1])
                doh = do_ref[:, w]
                dv_ref[w, :] += _dot(doh, p.astype(BF16), TN)
                dp = _dot(doh, v_ref[:, w], NT)
                ds = p * (dp - delta_ref[:, h:h + 1])
                if decay:
                    dct_ref[h:h + 1, :] -= _colsum(ds)
                ds_b = ds.astype(BF16)
                dk_ref[wl, :] += _dot(qh, ds_b, TN)
                dq_ref[rows, wl] += _dot(ds_b, kh)

        @pl.when(i > j)
        def _():
            step(False)

        @pl.when(i == j)
        def _():
            step(True)

        @pl.when(step_id == n_steps - 1)
        def _():
            pltpu.sync_copy(dq_ref, dq_hbm)

    qspec = lambda n: pl.BlockSpec((tq, n), lambda s, qi, kj: (qi[s], 0))
    kspec = lambda n: pl.BlockSpec((tq, n), lambda s, qi, kj: (kj[s], 0))
    in_specs = [qspec(N_HEADS * lanes), kspec(N_HEADS * lanes), kspec(512), qspec(512), qspec(N_HEADS), qspec(N_HEADS)]
    kspec_t = lambda n: pl.BlockSpec((n, tq), lambda s, qi, kj: (0, kj[s]))
    out_specs = [pl.BlockSpec(memory_space=pl.ANY), kspec_t(N_HEADS * lanes), kspec_t(512)]
    out_shape = [_sds((t, N_HEADS * lanes), F32), _sds((N_HEADS * lanes, t), F32), _sds((512, t), F32)]
    args = [q, k, v, do, delta, lse]
    if decay:
        ctspec = pl.BlockSpec((N_HEADS, tq), lambda s, qi, kj: (0, kj[s]))
        in_specs += [qspec(N_HEADS), ctspec]
        out_specs.append(ctspec)
        out_shape.append(_sds((N_HEADS, t), F32))
        args += [c, c_t]
    outs, got = _gridded_call(body, name=name, grid=(n_steps,), in_specs=in_specs, out_specs=out_specs, out_shape=out_shape,
                              scratch_shapes=[pltpu.VMEM((t, N_HEADS * lanes), F32)], args=args, exchange=exchange, prefetch=pairs)
    return [outs[0], outs[1].T, outs[2].T] + list(outs[3:]), got


def _mix_fwd(x, y_mla, y_fox, gates, b_gate, wbm_t, wbf_t, wo):
    t = x.shape[0]
    tm = min(PROJ_TILE, t)

    def body(x_ref, ym_ref, yf_ref, gt_ref, bg_ref, wbm_ref, wbf_ref, wo_ref, out_ref):
        um = _dot(ym_ref[...], wbm_ref[...], NT)
        uf = _dot(yf_ref[...], wbf_ref[...], NT)
        sm = jax.nn.sigmoid(gt_ref[:, 0:D_MODEL] + bg_ref[0:1, :])
        sf = jax.nn.sigmoid(gt_ref[:, D_MODEL:2 * D_MODEL] + bg_ref[1:2, :])
        mixed = sm * um + sf * uf
        out_ref[...] = x_ref[...] + _dot(mixed.astype(BF16), wo_ref[...])

    tok = _tok(tm, D_MODEL)
    return pl.pallas_call(
        body, name="mix_fwd", grid=(t // tm,),
        in_specs=[tok, _tok(tm, 512), _tok(tm, 512), _tok(tm, 2 * D_MODEL), _full((2, D_MODEL)), _full(wbm_t.shape),
                  _full(wbf_t.shape), _full(wo.shape)],
        out_specs=tok, out_shape=_sds((t, D_MODEL), F32),
        compiler_params=pltpu.CompilerParams(dimension_semantics=("arbitrary",)),
    )(x, y_mla, y_fox, gates, b_gate, wbm_t, wbf_t, wo)


def _mix_bwd(dx, y_mla, y_fox, y_mla32, y_fox32, gates, b_gate, wbm_t, wbf_t, wo, exchange=None):
    t = dx.shape[0]
    tm = min(PROJ_TILE, t)

    def body(dx_ref, ym_ref, yf_ref, ym32_ref, yf32_ref, gt_ref, bg_ref, wbm_ref, wbf_ref, wo_ref,
             dym_ref, dyf_ref, dlm_ref, dlf_ref, dgt_ref, mixed_ref, dum_ref, duf_ref, dxb_ref, dbg_ref, prod):
        i = pl.program_id(0)
        dxb = dx_ref[...].astype(BF16)
        dxb_ref[...] = dxb
        dmixed = _dot(dxb, wo_ref[...], NT)
        um = _dot(ym_ref[...], wbm_ref[...], NT)
        uf = _dot(yf_ref[...], wbf_ref[...], NT)
        sm = jax.nn.sigmoid(gt_ref[:, 0:D_MODEL] + bg_ref[0:1, :])
        sf = jax.nn.sigmoid(gt_ref[:, D_MODEL:2 * D_MODEL] + bg_ref[1:2, :])
        mixed_ref[...] = (sm * um + sf * uf).astype(BF16)
        dum = (dmixed * sm).astype(BF16)
        duf = (dmixed * sf).astype(BF16)
        dum_ref[...] = dum
        duf_ref[...] = duf
        dgm = dmixed * um * (sm * (1.0 - sm))
        dgf = dmixed * uf * (sf * (1.0 - sf))
        dgt_ref[:, 0:D_MODEL] = dgm.astype(BF16)
        dgt_ref[:, D_MODEL:2 * D_MODEL] = dgf.astype(BF16)
        for du, wb_ref, y32_ref, dy_ref, dl_ref in ((dum, wbm_ref, ym32_ref, dym_ref, dlm_ref),
                                                    (duf, wbf_ref, yf32_ref, dyf_ref, dlf_ref)):
            dy = _dot(du, wb_ref[...])
            dy_ref[...] = dy.astype(BF16)
            prod[...] = dy.astype(BF16).astype(F32) * y32_ref[...]
            for h in range(N_HEADS):
                dl_ref[:, h:h + 1] = jnp.sum(prod[:, h * HEAD_DIM:(h + 1) * HEAD_DIM], axis=1, keepdims=True)

        @pl.when(i == 0)
        def _():
            dbg_ref[...] = jnp.zeros_like(dbg_ref)

        dbg_ref[0:1, :] += _colsum(dgm)
        dbg_ref[1:2, :] += _colsum(dgf)

    tok = _tok(tm, D_MODEL)
    tokb = _sds((t, D_MODEL), BF16)
    t512, t8 = _tok(tm, 512), _tok(tm, N_HEADS)
    return _gridded_call(
        body, name="mix_bwd", grid=(t // tm,),
        in_specs=[tok, t512, t512, t512, t512, _tok(tm, 2 * D_MODEL), _full((2, D_MODEL)), _full(wbm_t.shape),
                  _full(wbf_t.shape), _full(wo.shape)],
        out_specs=[t512, t512, t8, t8, _tok(tm, 2 * D_MODEL), tok, tok, tok, tok, _full((2, D_MODEL))],
        out_shape=[_sds((t, 512), BF16), _sds((t, 512), BF16), _sds((t, N_HEADS), F32), _sds((t, N_HEADS), F32),
                   _sds((t, 2 * D_MODEL), BF16), tokb, tokb, tokb, tokb, _sds((2, D_MODEL), F32)],
        scratch_shapes=[pltpu.VMEM((tm, 512), F32)],
        args=[dx, y_mla, y_fox, y_mla32, y_fox32, gates, b_gate, wbm_t, wbf_t, wo], exchange=exchange)


def _my_position():
    x, y, c = lax.axis_index("x"), lax.axis_index("y"), lax.axis_index("c")
    return x, y, c, 4 * x + 2 * y + c


def _peer(x, y, c, mask):
    px = 1 - x if mask & 4 else x
    py = 1 - y if mask & 2 else y
    pc = 1 - c if mask & 1 else c
    return (px, py, pc), 4 * px + 2 * py + pc


def _chip_peer(x, y, km):
    px = 1 - x if km & 2 else x
    py = 1 - y if km & 1 else y
    return px, py, 2 * px + py


_HBM = pl.BlockSpec(memory_space=pl.ANY)


def _wait_all(copies):
    for cp in copies:
        cp.wait()


class _ChipExchange:
    def __init__(self, gather, arrays):
        self.gather, self.arrays, self.aliased = gather, list(arrays), False
        n = len(self.arrays)
        self.out_shape = [_sds((N_DEV * a.shape[0],) + a.shape[1:], a.dtype) if gather else _sds(a.shape, a.dtype)
                          for a in self.arrays]
        self.scratch_shapes = [pltpu.SemaphoreType.DMA((n, N_CHIP)), pltpu.SemaphoreType.DMA((n, N_CHIP)),
                               pltpu.SemaphoreType.DMA((n,))]

    def copies(self, srcs, dsts, send_sems, recv_sems, local_sems):
        x, y, c, me = _my_position()
        q_me = 2 * x + y
        out = []
        for a in range(len(self.arrays)):
            if self.gather:
                r = srcs[a].shape[0]
                local_src, dst = srcs[a], dsts[a].at[pl.ds(me * r, r)]
            else:
                local_src, dst = srcs[a].at[q_me], dsts[a].at[q_me]
            out.append(pltpu.make_async_copy(local_src, dst, local_sems.at[a]))
            for km in range(1, N_CHIP):
                px, py, q_peer = _chip_peer(x, y, km)
                out.append(pltpu.make_async_remote_copy(
                    src_ref=srcs[a] if self.gather else srcs[a].at[q_peer], dst_ref=dst, send_sem=send_sems.at[a, km],
                    recv_sem=recv_sems.at[a, km], device_id=(px, py, c), device_id_type=MESH))
        return out

    def standalone(self, name):
        n = len(self.arrays)

        def body(*refs):
            copies = self.copies(refs[:n], refs[n:2 * n], *refs[2 * n:])
            for cp in copies:
                cp.start()
            _wait_all(copies)

        return pl.pallas_call(body, name=name, in_specs=[_HBM] * n, out_specs=[_HBM] * n, out_shape=self.out_shape,
                              scratch_shapes=self.scratch_shapes)(*self.arrays)


def _gridded_call(body, *, name, grid, in_specs, out_specs, out_shape, scratch_shapes, args, exchange=None, prefetch=()):
    params = pltpu.CompilerParams(dimension_semantics=("arbitrary",) * len(grid))
    n_pre, n_in, n_out, n_scr = len(prefetch), len(in_specs), len(out_specs), len(scratch_shapes)
    n_x = 0 if exchange is None else len(exchange.arrays)

    def carrier(*refs):
        pre, refs = refs[:n_pre], refs[n_pre:]
        ins, x_src, refs = refs[:n_in], refs[n_in:n_in + n_x], refs[n_in + n_x:]
        outs, x_dst, refs = refs[:n_out], refs[n_out:n_out + n_x], refs[n_out + n_x:]
        copies = exchange.copies(x_src, x_dst, *refs[n_scr:])
        pids = [pl.program_id(d) for d in range(len(grid))]
        first = functools.reduce(jnp.logical_and, [p == 0 for p in pids])
        last = functools.reduce(jnp.logical_and, [p == g - 1 for p, g in zip(pids, grid)])

        @pl.when(first)
        def _():
            for cp in copies:
                cp.start()

        body(*pre, *ins, *outs, *refs[:n_scr])

        @pl.when(last)
        def _():
            _wait_all(copies)

    x_shapes, x_scratch, x_args = ([], [], []) if exchange is None else (exchange.out_shape, exchange.scratch_shapes, exchange.arrays)
    spec = pltpu.PrefetchScalarGridSpec(
        num_scalar_prefetch=n_pre, grid=grid, in_specs=list(in_specs) + [_HBM] * n_x, out_specs=list(out_specs) + [_HBM] * n_x,
        scratch_shapes=list(scratch_shapes) + x_scratch)
    in_place = {n_pre + n_in + k: n_out + k for k in range(n_x)} if n_x and exchange.aliased else {}
    res = pl.pallas_call(body if exchange is None else carrier, name=name, grid_spec=spec, out_shape=list(out_shape) + x_shapes,
                         input_output_aliases=in_place, compiler_params=params)(*prefetch, *args, *x_args)
    return res[:n_out], (None if exchange is None else res[n_out:])


class _CoreExchange:
    def __init__(self, gather, arrays):
        self.gather, self.arrays, self.aliased = gather, list(arrays), gather
        n = len(self.arrays)
        self.out_shape = [_sds(a.shape, a.dtype) if gather else _sds((N_CHIP, a.shape[0] // N_DEV) + a.shape[1:], a.dtype)
                          for a in self.arrays]
        self.scratch_shapes = [pltpu.SemaphoreType.DMA((n, N_CHIP)), pltpu.SemaphoreType.DMA((n, N_CHIP))]

    def copies(self, srcs, dsts, send_sems, recv_sems):
        x, y, c, _ = _my_position()
        out = []
        for a in range(len(self.arrays)):
            r = srcs[a].shape[0] // N_DEV
            for q in range(N_CHIP):
                if self.gather:
                    rows = pl.ds((2 * q + c) * r, r)
                    src, dst = srcs[a].at[rows], dsts[a].at[rows]
                else:
                    src, dst = srcs[a].at[pl.ds((2 * q + 1 - c) * r, r)], dsts[a].at[q]
                out.append(pltpu.make_async_remote_copy(src_ref=src, dst_ref=dst, send_sem=send_sems.at[a, q],
                                                        recv_sem=recv_sems.at[a, q], device_id=(x, y, 1 - c), device_id_type=MESH))
        return out

    def standalone(self, name):
        n = len(self.arrays)

        def body(*refs):
            copies = self.copies(refs[:n], refs[n:2 * n], *refs[2 * n:])
            for cp in copies:
                cp.start()
            _wait_all(copies)

        return pl.pallas_call(body, name=name, in_specs=[_HBM] * n, out_specs=[_HBM] * n, out_shape=self.out_shape,
                              input_output_aliases={a: a for a in range(n)} if self.aliased else {},
                              scratch_shapes=self.scratch_shapes)(*self.arrays)


def _gather_over_cores(arrays, name):
    return _CoreExchange(True, arrays).standalone(name)


def _grads_to_sibling(grads, name):
    return _CoreExchange(False, grads).standalone(name)


def _pair_sum(grad, from_sibling, name):
    r, n = from_sibling.shape[1:]

    def body(g_ref, s_ref, o_ref):
        c = lax.axis_index("c")
        o_ref[...] = (g_ref[c].astype(F32) + s_ref[...].astype(F32)).astype(BF16)

    return pl.pallas_call(
        body, name=name, grid=(N_CHIP,),
        in_specs=[pl.BlockSpec((None, 2, r, n), lambda q: (q, 0, 0, 0)), pl.BlockSpec((None, r, n), lambda q: (q, 0, 0))],
        out_specs=pl.BlockSpec((None, r, n), lambda q: (q, 0, 0)), out_shape=_sds((N_CHIP, r, n), BF16),
    )(grad.reshape(N_CHIP, 2, r, n), from_sibling)


def _all_reduce_small(vec):
    r = vec.shape[0]

    def body(v_ref, o_ref, buf, send_sems, recv_sems):
        x, y, c, me = _my_position()
        buf[me] = v_ref[...]
        copies = []
        for mask in range(1, N_DEV):
            peer, _ = _peer(x, y, c, mask)
            cp = pltpu.make_async_remote_copy(src_ref=v_ref, dst_ref=buf.at[me], send_sem=send_sems.at[mask],
                                              recv_sem=recv_sems.at[mask], device_id=peer, device_id_type=MESH)
            cp.start()
            copies.append(cp)
        for cp in copies:
            cp.wait()
        total = buf[0]
        for s in range(1, N_DEV):
            total = total + buf[s]
        o_ref[...] = total

    vm = pl.BlockSpec(memory_space=pltpu.VMEM)
    return pl.pallas_call(
        body, name="all_reduce_small", in_specs=[vm], out_specs=vm, out_shape=_sds(vec.shape, F32),
        scratch_shapes=[pltpu.VMEM((N_DEV, r, 128), F32), pltpu.SemaphoreType.DMA((N_DEV,)), pltpu.SemaphoreType.DMA((N_DEV,))],
    )(vec)


def _adamw_math(w, g, m, v):
    m = ADAM_B1 * m + (1.0 - ADAM_B1) * g
    v = ADAM_B2 * v + (1.0 - ADAM_B2) * (g * g)
    m_hat = m / (1.0 - ADAM_B1 ** ADAM_STEP)
    v_hat = v / (1.0 - ADAM_B2 ** ADAM_STEP)
    delta = -ADAM_LR * (m_hat / (jnp.sqrt(v_hat) + ADAM_EPS) + ADAM_WD * w)
    return delta, m, v


def _reduce_adamw(slots, w, m, v, *, transpose, name):
    r, n = slots.shape[1:]
    padded = not transpose and w.shape[0] != r

    def body(s_ref, w_ref, m_ref, v_ref, g_ref, d_ref, nm_ref, nv_ref, *scr):
        g = s_ref[0].astype(F32)
        for s in range(1, slots.shape[0]):
            g = g + s_ref[s].astype(F32)
        if transpose:
            scr[0][...] = g.T
            g = scr[0][:, 0:w_ref.shape[1]]
        elif padded:
            scr[0][...] = g
            g = scr[0][0:w_ref.shape[0], :]
        g_ref[...] = g
        d_ref[...], nm_ref[...], nv_ref[...] = _adamw_math(w_ref[...], g, m_ref[...], v_ref[...])

    out = _sds(w.shape, F32)
    if not transpose and not padded and r % (2 * BF16_ROW_TILE) == 0:
        half = pl.BlockSpec((r // 2, n), lambda i: (i, 0))
        return pl.pallas_call(
            body, name=name, grid=(2,), in_specs=[pl.BlockSpec((slots.shape[0], r // 2, n), lambda i: (0, i, 0)), half, half, half],
            out_specs=[half] * 4, out_shape=[out, out, out, out])(slots, w, m, v)
    return pl.pallas_call(
        body, name=name, out_shape=[out, out, out, out],
        scratch_shapes=[pltpu.VMEM((n, r) if transpose else (r, n), F32)] if transpose or padded else [],
    )(slots, w, m, v)


def _adamw(g, w, m, v, name):
    def body(g_ref, w_ref, m_ref, v_ref, d_ref, nm_ref, nv_ref):
        d_ref[...], nm_ref[...], nv_ref[...] = _adamw_math(w_ref[...], g_ref[...], m_ref[...], v_ref[...])

    out = _sds(w.shape, F32)
    return pl.pallas_call(body, name=name, out_shape=[out, out, out])(g, w, m, v)


_SMALL = ["ffn1_norm", "mix_norm", "ffn2_norm", "mla_q_lat_norm", "mla_kv_lat_norm", "mla_q_nope_gain", "mla_q_rope_gain",
          "mla_k_nope_gain", "mla_k_rope_gain", "fox_q_gain", "fox_k_gain", "fox_b_f"]
_WEIGHTS = ["ffn1_norm", "ffn1_w_gate", "ffn1_w_up", "ffn1_w_down", "mix_norm", "w_in", "mla_q_lat_norm", "mla_w_qb",
            "mla_kv_lat_norm", "mla_w_kvb", "mla_q_nope_gain", "mla_q_rope_gain", "mla_k_nope_gain", "mla_k_rope_gain",
            "fox_q_gain", "fox_k_gain", "fox_b_f", "w_branch_mla", "w_branch_fox", "b_gate", "w_o", "ffn2_norm",
            "ffn2_w_gate", "ffn2_w_up", "ffn2_w_down"]
_IN_Q, _IN_KV, _IN_KR, _IN_FOX, _IN_F, _IN_GATES = (0, 192), (192, 128), (320, 32), (352, 1536), (1888, 8), (1896, 2048)


def _rows(a, seg):
    return a[seg[0]:seg[0] + seg[1]]


def _split_w_in(win_t):
    z = lambda n: jnp.zeros((n, D_MODEL), win_t.dtype)
    lat = jnp.concatenate([_rows(win_t, _IN_Q), z(LAT_KV - Q_LORA), _rows(win_t, _IN_KV), _rows(win_t, _IN_KR),
                           _rows(win_t, _IN_F), z(LAT_W - LAT_F - N_HEADS)], axis=0)
    return _rows(win_t, _IN_GATES), _rows(win_t, _IN_FOX), lat


def _join_w_in(d_gates, d_fox, d_lat):
    return jnp.concatenate([d_lat[LAT_Q:LAT_Q + Q_LORA], d_lat[LAT_KV:LAT_KV + KV_LORA], d_lat[LAT_KR:LAT_KR + ROPE_DIM],
                            d_fox, d_lat[LAT_F:LAT_F + N_HEADS], d_gates], axis=0)


def kernel(x, positions, ffn1_norm, ffn1_w_gate, ffn1_w_up, ffn1_w_down, mix_norm, w_in, mla_q_lat_norm, mla_w_qb, mla_kv_lat_norm, mla_w_kvb, mla_q_nope_gain, mla_q_rope_gain, mla_k_nope_gain, mla_k_rope_gain, fox_q_gain, fox_k_gain, fox_b_f, w_branch_mla, w_branch_fox, b_gate, w_o, ffn2_norm, ffn2_w_gate, ffn2_w_up, ffn2_w_down, loss_target, m_ffn1_norm, m_ffn1_w_gate, m_ffn1_w_up, m_ffn1_w_down, m_mix_norm, m_w_in, m_mla_q_lat_norm, m_mla_w_qb, m_mla_kv_lat_norm, m_mla_w_kvb, m_mla_q_nope_gain, m_mla_q_rope_gain, m_mla_k_nope_gain, m_mla_k_rope_gain, m_fox_q_gain, m_fox_k_gain, m_fox_b_f, m_w_branch_mla, m_w_branch_fox, m_b_gate, m_w_o, m_ffn2_norm, m_ffn2_w_gate, m_ffn2_w_up, m_ffn2_w_down, v_ffn1_norm, v_ffn1_w_gate, v_ffn1_w_up, v_ffn1_w_down, v_mix_norm, v_w_in, v_mla_q_lat_norm, v_mla_w_qb, v_mla_kv_lat_norm, v_mla_w_kvb, v_mla_q_nope_gain, v_mla_q_rope_gain, v_mla_k_nope_gain, v_mla_k_rope_gain, v_fox_q_gain, v_fox_k_gain, v_fox_b_f, v_w_branch_mla, v_w_branch_fox, v_b_gate, v_w_o, v_ffn2_norm, v_ffn2_w_gate, v_ffn2_w_up, v_ffn2_w_down):
    env = dict(locals())
    strip = lambda n, a: a if n in _SMALL else a[0]
    W = {n: strip(n, env[n]) for n in _WEIGHTS}
    M = {n: strip(n, env["m_" + n]) for n in _WEIGHTS}
    V = {n: strip(n, env["v_" + n]) for n in _WEIGHTS}
    xs = x[0]
    t = xs.shape[0]

    col_split = ["ffn1_w_gate", "ffn1_w_up", "ffn2_w_gate", "ffn2_w_up", "mla_w_qb", "mla_w_kvb", "w_branch_mla", "w_branch_fox"]
    row_split = ["ffn1_w_down", "ffn2_w_down", "w_o"]
    pieces = {n: W[n].T.astype(BF16) for n in col_split}
    pieces.update({n: W[n].astype(BF16) for n in row_split})
    pieces["w_in"] = jnp.pad(W["w_in"].T.astype(BF16), ((0, W_IN_PIECE_PAD - W_IN_PIECE), (0, 0)))
    pieces["b_gate"] = W["b_gate"].T
    group_a = ["ffn1_w_gate", "ffn1_w_up", "ffn1_w_down"]
    group_b = ["w_in", "mla_w_qb", "mla_w_kvb", "w_branch_mla", "w_branch_fox", "w_o", "b_gate"]
    group_c = ["ffn2_w_gate", "ffn2_w_up", "ffn2_w_down"]
    gather = lambda group: _ChipExchange(True, [pieces[n] for n in group])
    G = dict(zip(group_a, _gather_over_cores(gather(group_a).standalone("gather_ici_a"), "gather_d2d_a")))

    inv_freq = ROPE_THETA ** (-jnp.arange(ROPE_HALF, dtype=F32) / ROPE_HALF)
    ang = positions[0].astype(F32)[:, None] * inv_freq
    tables, mats = _prep_tables(jnp.cos(ang), jnp.sin(ang)), _group_matrices()
    gains = _head_gains(*[W[n] for n in ["mla_q_nope_gain", "mla_q_rope_gain", "mla_k_nope_gain", "mla_k_rope_gain",
                                         "fox_q_gain", "fox_k_gain"]])

    (x1, a1, b1), got_b = _ffn_fwd(xs, W["ffn1_norm"], G["ffn1_w_gate"], G["ffn1_w_up"], G["ffn1_w_down"],
                                   exchange=gather(group_b))
    G.update(zip(group_b, _gather_over_cores(got_b, "gather_d2d_b")))
    win_t = G["w_in"].reshape(N_DEV, W_IN_PIECE_PAD, D_MODEL)[:, :W_IN_PIECE].reshape(N_DEV * W_IN_PIECE, D_MODEL)
    wgate_t, wfox_t, wlat_t = _split_w_in(win_t)
    bg = G["b_gate"].T
    gates, fox, lat = _proj_fwd(x1, W["mix_norm"], wgate_t, wfox_t, wlat_t)
    prep_args = (fox, lat, tables, mats, *_interleave_weights(G["mla_w_qb"], G["mla_w_kvb"]), W["mla_q_lat_norm"],
                 W["mla_kv_lat_norm"], gains, W["fox_b_f"])
    fq, fk, fv, c, mq, mk, mv, fq_sq, fk_sq_max, mq_sq, mk_sq_max, fv1, mv1 = _prep_fwd(*prep_args, *_value_layout(prep_args[6]))
    c_t = c.T
    b_fox, ok_fox = _logit_bound(fq_sq, fk_sq_max)
    b_mla, ok_mla = _logit_bound(mq_sq, mk_sq_max)
    (y_fox, y_fox32, lse_fox), got_c = _attn_fwd(fq, fk, fv1, b_fox, ok_fox, c, c_t, lanes=HEAD_DIM, name="fox_fwd",
                                                 exchange=gather(group_c))
    (y_mla, y_mla32, lse_mla), got_c = _attn_fwd(mq, mk, mv1, b_mla, ok_mla, None, None, lanes=MLA_QK_LANES, name="mla_fwd",
                                                 exchange=_CoreExchange(True, got_c))
    G.update(zip(group_c, got_c))
    x2 = _mix_fwd(x1, y_mla, y_fox, gates, bg, G["w_branch_mla"], G["w_branch_fox"], G["w_o"])
    (dx3, a2, b2, loss_vec), _ = _ffn_fwd(x2, W["ffn2_norm"], G["ffn2_w_gate"], G["ffn2_w_up"], G["ffn2_w_down"],
                                          target=loss_target[0])

    def chip_sums(group, tag, from_sibling=None):
        if from_sibling is None:
            from_sibling = _grads_to_sibling([grads[n] for n in group], "grads_d2d_" + tag)
        return _ChipExchange(False, [_pair_sum(grads[n], s, "pair_sum_" + n) for n, s in zip(group, from_sibling)])

    (dx2, dg_ffn2, da2, db2, h2, n2, dyh2), _ = _ffn_bwd(dx3, x2, W["ffn2_norm"], a2, b2, G["ffn2_w_gate"], G["ffn2_w_up"],
                                                        G["ffn2_w_down"], "ffn2_bwd")
    grads = {"ffn2_w_gate": _tn_matmul(da2, n2, "ffn2_dgate"), "ffn2_w_up": _tn_matmul(db2, n2, "ffn2_dup"),
             "ffn2_w_down": _tn_matmul(h2, dyh2, "ffn2_ddown")}
    (dy_mla, dy_fox, delta_mla, delta_fox, dgates, mixed, dum, duf, dx2b, dbg), from_sibling_c = _mix_bwd(
        dx2, y_mla, y_fox, y_mla32, y_fox32, gates, bg, G["w_branch_mla"], G["w_branch_fox"], G["w_o"],
        exchange=_CoreExchange(False, [grads[n] for n in group_c]))
    grads["w_o"] = _tn_matmul(mixed, dx2b, "d_w_o")
    grads["w_branch_mla"] = _tn_matmul(dum, y_mla, "d_w_branch_mla")
    grads["w_branch_fox"] = _tn_matmul(duf, y_fox, "d_w_branch_fox")
    (dfq, dfk, dfv, dc_t), slots_c = _attn_bwd(fq, fk, fv, dy_fox, delta_fox, lse_fox, c, c_t, lanes=HEAD_DIM,
                                               name="fox_bwd", exchange=chip_sums(group_c, "c", from_sibling_c))
    slots = dict(zip(group_c, slots_c))
    (dmq, dmk, dmv), _ = _attn_bwd(mq, mk, mv, dy_mla, delta_mla, lse_mla, None, None, lanes=MLA_QK_LANES, name="mla_bwd")
    dfox, dlat, dwq, dwk, dwv, d_gql, d_gkvl, d_gq, d_gk, d_gkr, d_gfq, d_gfk, d_bf = _prep_bwd(
        *prep_args, dfq, dfk, dfv, dc_t.T, dmq, dmk, dmv)
    grads["mla_w_qb"], grads["mla_w_kvb"] = _deinterleave_grads(dwq, dwk, dwv)
    fold = lambda a, width: a.reshape(N_HEADS, width).sum(axis=0)[None]
    d_gq, d_gk = fold(d_gq, LANES), fold(d_gk, LANES)
    d_prep_small = [d_gql, d_gkvl, d_gq[:, :HEAD_DIM], d_gq[:, HEAD_DIM:HEAD_DIM + ROPE_DIM], d_gk[:, :HEAD_DIM],
                    d_gkr[:, HEAD_DIM:HEAD_DIM + ROPE_DIM], fold(d_gfq, HEAD_DIM), fold(d_gfk, HEAD_DIM), d_bf]
    dx1, dg_mix, nmix = _proj_bwd(dgates, dfox, dlat, wgate_t, wfox_t, wlat_t, x1, W["mix_norm"], dx2)
    dwin_t = _join_w_in(_tn_matmul(dgates, nmix, "d_w_in_gates"), _tn_matmul(dfox, nmix, "d_w_in_fox"),
                        _tn_matmul(dlat, nmix, "d_w_in_lat"))
    grads["w_in"] = jnp.pad(dwin_t.reshape(N_DEV, W_IN_PIECE, D_MODEL), ((0, 0), (0, W_IN_PIECE_PAD - W_IN_PIECE), (0, 0))
                            ).reshape(N_DEV * W_IN_PIECE_PAD, D_MODEL)
    grad_group_b = [n for n in group_b if n != "b_gate"]
    (dx0, dg_ffn1, da1, db1, h1, n1, dyh1), slots_b = _ffn_bwd(dx1, xs, W["ffn1_norm"], a1, b1, G["ffn1_w_gate"], G["ffn1_w_up"],
                                                              G["ffn1_w_down"], "ffn1_bwd", exchange=chip_sums(grad_group_b, "b"))
    slots.update(zip(grad_group_b, slots_b))
    grads["ffn1_w_gate"] = _tn_matmul(da1, n1, "ffn1_dgate")
    grads["ffn1_w_up"], got = _tn_matmul(db1, n1, "ffn1_dup", exchange=chip_sums(["ffn1_w_gate"], "a_gate"))
    slots["ffn1_w_gate"] = got[0]
    grads["ffn1_w_down"], got = _tn_matmul(h1, dyh1, "ffn1_ddown", exchange=chip_sums(["ffn1_w_up"], "a_up"))
    slots["ffn1_w_up"] = got[0]
    slots["ffn1_w_down"] = chip_sums(["ffn1_w_down"], "a_down").standalone("grads_ici_a_down")[0]

    small_parts = [dg_ffn1, dg_mix, dg_ffn2] + list(d_prep_small) + [dbg.reshape(1, 2 * D_MODEL), loss_vec]
    flat = jnp.concatenate([p.reshape(-1) for p in small_parts])
    n_flat = flat.shape[0]
    rows = -(-n_flat // (8 * 128)) * 8
    total = _all_reduce_small(jnp.pad(flat, (0, rows * 128 - n_flat)).reshape(rows, 128)).reshape(-1)
    offs, small_g = 0, {}
    for n in _SMALL:
        small_g[n] = total[offs:offs + W[n].shape[1]].reshape(W[n].shape)
        offs += W[n].shape[1]
    bg_full = total[offs:offs + 2 * D_MODEL].reshape(2, D_MODEL)
    offs += 2 * D_MODEL
    loss = (0.5 / D_MODEL) * jnp.sum(total[offs:offs + D_MODEL])
    _, _, _, me = _my_position()
    small_g["b_gate"] = lax.dynamic_slice_in_dim(bg_full, me * (D_MODEL // N_DEV), D_MODEL // N_DEV, axis=1)

    transposed_in_memory = ["ffn1_w_gate", "ffn1_w_up", "ffn2_w_gate", "ffn2_w_up", "w_in"]
    res = {}
    for n in _WEIGHTS:
        if n in small_g:
            res[n] = (small_g[n],) + tuple(_adamw(small_g[n], W[n], M[n], V[n], "adamw_" + n))
        elif n in transposed_in_memory:
            res[n] = tuple(o.T for o in _reduce_adamw(slots[n], W[n].T, M[n].T, V[n].T, transpose=False, name="adamw_" + n))
        else:
            res[n] = tuple(_reduce_adamw(slots[n], W[n], M[n], V[n], transpose=n in col_split, name="adamw_" + n))
    outs = [loss, dx0[None]]
    for k in range(4):
        outs += [res[n][k] if n in _SMALL else res[n][k][None] for n in _WEIGHTS]
    return tuple(outs)
```

```python
import functools

import jax
import jax.numpy as jnp
from jax import lax
from jax.experimental import pallas as pl
from jax.experimental.pallas import tpu as pltpu

F32 = jnp.float32
BF16 = jnp.bfloat16

D_MODEL = 1024
FFN_HIDDEN = 2816
N_HEADS = 8
HEAD_DIM = 64
ROPE_DIM = 32
ROPE_HALF = 16
Q_LORA = 192
KV_LORA = 128
ROPE_THETA = 10000.0
RMS_EPS = 1e-6
MLA_SCALE = (HEAD_DIM + ROPE_DIM) ** -0.5
FOX_SCALE = HEAD_DIM ** -0.5
MLA_QK_LANES = 128
ADAM_LR, ADAM_B1, ADAM_B2, ADAM_EPS, ADAM_WD, ADAM_STEP = 0.001, 0.9, 0.999, 1e-08, 0.01, 10
N_DEV = 8
N_CHIP = 4
W_IN_PIECE = 493
BF16_ROW_TILE = 16
W_IN_PIECE_PAD = 496
LAT_W = 512
LAT_Q, LAT_KV, LAT_KR, LAT_F = 0, 256, 384, 416
MASK_VALUE = -1e30
FIXED_SHIFT_MAX_BOUND = 30.0

TOK_TILE = 512
DW_TOK_TILE = 1024
PROJ_TILE = 512
PREP_TILE = 256
ATT_TILE = 512
ATT_COL_CHUNK = 256
FFN_HID_TILE = 1408
FFN_HID_SPLIT = ((0, 768), (768, 1408))

NT = (((1,), (1,)), ((), ()))
TN = (((0,), (0,)), ((), ()))
NN = (((1,), (0,)), ((), ()))
MESH = pl.DeviceIdType.MESH


def _dot(a, b, dims=NN):
    return lax.dot_general(a, b, dims, preferred_element_type=F32)


def _sds(shape, dtype):
    return jax.ShapeDtypeStruct(shape, dtype)


def _rms_fwd(x, g):
    r = lax.rsqrt(jnp.mean(x * x, axis=-1, keepdims=True) + RMS_EPS)
    return x * r * g, r


def _rms_bwd(dy, x, g, r):
    xn = x * r
    dyg = dy * g
    dx = r * (dyg - xn * jnp.mean(dyg * xn, axis=-1, keepdims=True))
    return dx, dy * xn


def _colsum(x):
    return jnp.sum(x, axis=0, keepdims=True)


def _full(shape):
    return pl.BlockSpec(shape, lambda *_: (0,) * len(shape))


def _tok(tm, n):
    return pl.BlockSpec((tm, n), lambda i, *_: (i, 0))


def _ffn_fwd(x, gain, wg_t, wu_t, wd, target=None, exchange=None):
    t = x.shape[0]
    tm = min(TOK_TILE, t)
    tf = FFN_HID_TILE
    n_t, n_f = t // tm, FFN_HIDDEN // tf
    with_loss = target is not None

    def body(*refs):
        if with_loss:
            x_ref, g_ref, wg_ref, wu_ref, wd_ref, t_ref, out_ref, a_ref, b_ref, lvec_ref, n_scr, acc = refs
        else:
            x_ref, g_ref, wg_ref, wu_ref, wd_ref, out_ref, a_ref, b_ref, n_scr, acc = refs
        i, j = pl.program_id(0), pl.program_id(1)

        @pl.when(j == 0)
        def _():
            xn, _ = _rms_fwd(x_ref[...], g_ref[...])
            n_scr[...] = xn.astype(BF16)
            acc[...] = jnp.zeros_like(acc)

        n = n_scr[...]
        a = _dot(n, wg_ref[...], NT)
        b = _dot(n, wu_ref[...], NT)
        a_ref[...] = a.astype(BF16)
        b_ref[...] = b.astype(BF16)
        h = (a * jax.nn.sigmoid(a)) * b
        acc[...] += _dot(h.astype(BF16), wd_ref[...])

        @pl.when(j == n_f - 1)
        def _():
            y = x_ref[...] + 0.5 * acc[...]
            if with_loss:
                diff = y - t_ref[...]
                out_ref[...] = diff * (1.0 / D_MODEL)
                sq = _colsum(diff * diff)

                @pl.when(i == 0)
                def _():
                    lvec_ref[...] = sq

                @pl.when(i > 0)
                def _():
                    lvec_ref[...] += sq
            else:
                out_ref[...] = y

    wspec = pl.BlockSpec((tf, D_MODEL), lambda i, j: (j, 0))
    hspec = pl.BlockSpec((tm, tf), lambda i, j: (i, j))
    in_specs = [_tok(tm, D_MODEL), _full((1, D_MODEL)), wspec, wspec, wspec]
    out_specs = [_tok(tm, D_MODEL), hspec, hspec]
    out_shape = [_sds((t, D_MODEL), F32), _sds((t, FFN_HIDDEN), BF16), _sds((t, FFN_HIDDEN), BF16)]
    args = [x, gain, wg_t, wu_t, wd]
    if with_loss:
        in_specs.append(_tok(tm, D_MODEL))
        out_specs.append(_full((1, D_MODEL)))
        out_shape.append(_sds((1, D_MODEL), F32))
        args.append(target)
    return _gridded_call(
        body, name="ffn_fwd_loss" if with_loss else "ffn_fwd", grid=(n_t, n_f), in_specs=in_specs, out_specs=out_specs,
        out_shape=out_shape, scratch_shapes=[pltpu.VMEM((tm, D_MODEL), BF16), pltpu.VMEM((tm, D_MODEL), F32)],
        args=args, exchange=exchange)


def _ffn_bwd(dy, x, gain, a, b, wg_t, wu_t, wd, name, exchange=None):
    t = x.shape[0]
    tm = min(TOK_TILE, t)
    tf = FFN_HID_TILE
    n_t, n_f = t // tm, FFN_HIDDEN // tf

    def body(dy_ref, x_ref, g_ref, a_ref, b_ref, wg_ref, wu_ref, wd_ref,
             dx_ref, dg_ref, da_ref, db_ref, h_ref, n_ref, dyh_ref, acc):
        i, j = pl.program_id(0), pl.program_id(1)

        @pl.when(j == 0)
        def _():
            xn, _ = _rms_fwd(x_ref[...], g_ref[...])
            n_ref[...] = xn.astype(BF16)
            dyh_ref[...] = (0.5 * dy_ref[...]).astype(BF16)
            acc[...] = jnp.zeros_like(acc)

        dyh = dyh_ref[...]
        for lo, hi in FFN_HID_SPLIT:
            dh = _dot(dyh, wd_ref[lo:hi, :], NT)
            av = a_ref[:, lo:hi].astype(F32)
            bv = b_ref[:, lo:hi].astype(F32)
            s = jax.nn.sigmoid(av)
            silu = av * s
            da = (dh * bv * (s * (1.0 + av * (1.0 - s)))).astype(BF16)
            db = (dh * silu).astype(BF16)
            da_ref[:, lo:hi] = da
            db_ref[:, lo:hi] = db
            h_ref[:, lo:hi] = (silu * bv).astype(BF16)
            acc[...] += _dot(da, wg_ref[lo:hi, :]) + _dot(db, wu_ref[lo:hi, :])

        @pl.when(j == n_f - 1)
        def _():
            xv, g = x_ref[...], g_ref[...]
            r = lax.rsqrt(jnp.mean(xv * xv, axis=-1, keepdims=True) + RMS_EPS)
            dx, dg_rows = _rms_bwd(acc[...], xv, g, r)
            dx_ref[...] = dy_ref[...] + dx
            dg = _colsum(dg_rows)

            @pl.when(i == 0)
            def _():
                dg_ref[...] = dg

            @pl.when(i > 0)
            def _():
                dg_ref[...] += dg

    wspec = pl.BlockSpec((tf, D_MODEL), lambda i, j: (j, 0))
    hspec = pl.BlockSpec((tm, tf), lambda i, j: (i, j))
    tok = _tok(tm, D_MODEL)
    hid = _sds((t, FFN_HIDDEN), BF16)
    return _gridded_call(
        body, name=name, grid=(n_t, n_f),
        in_specs=[tok, tok, _full((1, D_MODEL)), hspec, hspec, wspec, wspec, wspec],
        out_specs=[tok, _full((1, D_MODEL)), hspec, hspec, hspec, tok, tok],
        out_shape=[_sds((t, D_MODEL), F32), _sds((1, D_MODEL), F32), hid, hid, hid,
                   _sds((t, D_MODEL), BF16), _sds((t, D_MODEL), BF16)],
        scratch_shapes=[pltpu.VMEM((tm, D_MODEL), F32)], args=[dy, x, gain, a, b, wg_t, wu_t, wd], exchange=exchange)


def _tn_matmul(a, b, name, exchange=None):
    t, m = a.shape
    n = b.shape[1]
    tk = min(DW_TOK_TILE, t)
    n_k = t // tk

    def body(a_ref, b_ref, o_ref, acc):
        k = pl.program_id(0)
        p = _dot(a_ref[...], b_ref[...], TN)

        @pl.when(k == 0)
        def _():
            acc[...] = p

        @pl.when(k > 0)
        def _():
            acc[...] += p

        @pl.when(k == n_k - 1)
        def _():
            o_ref[...] = acc[...].astype(BF16)

    (out,), got = _gridded_call(
        body, name=name, grid=(n_k,), in_specs=[_tok(tk, m), _tok(tk, n)], out_specs=[_full((m, n))],
        out_shape=[_sds((m, n), BF16)], scratch_shapes=[pltpu.VMEM((m, n), F32)], args=[a, b], exchange=exchange)
    return out if exchange is None else (out, got)


def _proj_fwd(x, gain, wgate_t, wfox_t, wlat_t):
    t = x.shape[0]
    tm = min(PROJ_TILE, t)

    def body(x_ref, g_ref, wg_ref, wf_ref, wl_ref, og_ref, of_ref, ol_ref):
        xn, _ = _rms_fwd(x_ref[...], g_ref[...])
        n = xn.astype(BF16)
        og_ref[...] = _dot(n, wg_ref[...], NT)
        of_ref[...] = _dot(n, wf_ref[...], NT)
        ol_ref[...] = _dot(n, wl_ref[...], NT)

    return pl.pallas_call(
        body, name="proj_fwd", grid=(t // tm,),
        in_specs=[_tok(tm, D_MODEL), _full((1, D_MODEL)), _full(wgate_t.shape), _full(wfox_t.shape), _full(wlat_t.shape)],
        out_specs=[_tok(tm, 2 * D_MODEL), _tok(tm, 3 * 512), _tok(tm, LAT_W)],
        out_shape=[_sds((t, 2 * D_MODEL), F32), _sds((t, 3 * 512), F32), _sds((t, LAT_W), F32)],
        compiler_params=pltpu.CompilerParams(dimension_semantics=("arbitrary",)),
    )(x, gain, wgate_t, wfox_t, wlat_t)


def _proj_bwd(dgates, dfox, dlat, wgate_t, wfox_t, wlat_t, x, gain, dres):
    t = x.shape[0]
    tm = min(PROJ_TILE, t)

    def body(dg_ref, df_ref, dl_ref, wg_ref, wf_ref, wl_ref, x_ref, g_ref, dres_ref, dx_ref, dgain_ref, n_ref):
        i = pl.program_id(0)
        dn = _dot(dg_ref[...], wg_ref[...]) + _dot(df_ref[...], wf_ref[...]) + _dot(dl_ref[...], wl_ref[...])
        xv, g = x_ref[...], g_ref[...]
        xn, r = _rms_fwd(xv, g)
        n_ref[...] = xn.astype(BF16)
        dx, dg_rows = _rms_bwd(dn, xv, g, r)
        dx_ref[...] = dres_ref[...] + dx
        dgn = _colsum(dg_rows)

        @pl.when(i == 0)
        def _():
            dgain_ref[...] = dgn

        @pl.when(i > 0)
        def _():
            dgain_ref[...] += dgn

    tok = _tok(tm, D_MODEL)
    return pl.pallas_call(
        body, name="proj_bwd", grid=(t // tm,),
        in_specs=[_tok(tm, 2 * D_MODEL), _tok(tm, 3 * 512), _tok(tm, LAT_W), _full(wgate_t.shape), _full(wfox_t.shape),
                  _full(wlat_t.shape), tok, _full((1, D_MODEL)), tok],
        out_specs=[tok, _full((1, D_MODEL)), tok],
        out_shape=[_sds((t, D_MODEL), F32), _sds((1, D_MODEL), F32), _sds((t, D_MODEL), BF16)],
        compiler_params=pltpu.CompilerParams(dimension_semantics=("arbitrary",)),
    )(dgates, dfox, dlat, wgate_t, wfox_t, wlat_t, x, gain, dres)


def _tri(n, lower):
    r = lax.broadcasted_iota(jnp.int32, (n, n), 0)
    c = lax.broadcasted_iota(jnp.int32, (n, n), 1)
    return ((c <= r) if lower else (c >= r)).astype(F32)


def _log_sigmoid(z):
    return jnp.minimum(z, 0.0) - jnp.log1p(jnp.exp(-jnp.abs(z)))


N_GROUPS = 16
LANES = MLA_QK_LANES


def _split_dot(a, g):
    hi = a.astype(BF16)
    lo = (a - hi.astype(F32)).astype(BF16)
    return _dot(hi, g) + _dot(lo, g)


def _rope_fwd(y, c, s1, s2):
    return y * c + pltpu.roll(y, LANES - ROPE_HALF, 1) * s1 + pltpu.roll(y, ROPE_HALF, 1) * s2


def _rope_bwd(do, c, s1, s2):
    return do * c + pltpu.roll(do * s1, ROPE_HALF, 1) + pltpu.roll(do * s2, LANES - ROPE_HALF, 1)


def _prep_tables(cos, sin):
    t = cos.shape[0]
    z = lambda n: jnp.zeros((t, n), F32)
    c = jnp.concatenate([jnp.ones((t, HEAD_DIM), F32), cos, cos, z(LANES - HEAD_DIM - ROPE_DIM)], axis=1)
    s1 = jnp.concatenate([z(HEAD_DIM), -sin, z(LANES - HEAD_DIM - ROPE_HALF)], axis=1)
    s2 = jnp.concatenate([z(HEAD_DIM + ROPE_HALF), sin, z(LANES - HEAD_DIM - ROPE_DIM)], axis=1)
    lane = jnp.arange(LANES)
    rope_mask = ((lane >= HEAD_DIM) & (lane < HEAD_DIM + ROPE_DIM)).astype(F32)[None, :]
    return c, s1, s2, rope_mask


def _group_matrices():
    lane = jnp.arange(N_HEADS * LANES)
    head, d = (lane // LANES)[:, None], (lane % LANES)[:, None]
    col = jnp.arange(N_GROUPS)[None, :]
    g_mla = ((col == head) & (d < HEAD_DIM)) | ((col == N_HEADS + head) & (d >= HEAD_DIM) & (d < HEAD_DIM + ROPE_DIM))
    g_fox = col == (jnp.arange(N_HEADS * HEAD_DIM) // HEAD_DIM)[:, None]
    inv_mla = jnp.concatenate([jnp.full((1, N_HEADS), 1.0 / HEAD_DIM, F32), jnp.full((1, N_HEADS), 1.0 / ROPE_DIM, F32)], axis=1)
    inv_fox = jnp.full((1, N_GROUPS), 1.0 / HEAD_DIM, F32)
    return g_mla.astype(BF16), g_mla.T.astype(BF16), inv_mla, g_fox.astype(BF16), g_fox.T.astype(BF16), inv_fox


def _interleave_weights(wqb_t, wkvb_t):
    wq = jnp.pad(wqb_t.reshape(N_HEADS, HEAD_DIM + ROPE_DIM, Q_LORA), ((0, 0), (0, LANES - HEAD_DIM - ROPE_DIM), (0, 0)))
    kv = wkvb_t.reshape(N_HEADS, 2, HEAD_DIM, KV_LORA)
    wk = jnp.pad(kv[:, 0], ((0, 0), (0, LANES - HEAD_DIM), (0, 0)))
    return wq.reshape(N_HEADS * LANES, Q_LORA), wk.reshape(N_HEADS * LANES, KV_LORA), kv[:, 1].reshape(N_HEADS * HEAD_DIM, KV_LORA)


def _deinterleave_grads(dwq, dwk, dwv):
    dq = dwq.reshape(N_HEADS, LANES, Q_LORA)[:, :HEAD_DIM + ROPE_DIM].reshape(N_HEADS * (HEAD_DIM + ROPE_DIM), Q_LORA)
    dk = dwk.reshape(N_HEADS, LANES, KV_LORA)[:, :HEAD_DIM]
    dkv = jnp.stack([dk, dwv.reshape(N_HEADS, HEAD_DIM, KV_LORA)], axis=1)
    return dq, dkv.reshape(N_HEADS * 2 * HEAD_DIM, KV_LORA)


def _head_gains(g_qn, g_qr, g_kn, g_kr, g_fq, g_fk):
    z = lambda n: jnp.zeros((1, n), F32)
    gq = jnp.concatenate([g_qn, g_qr, z(LANES - HEAD_DIM - ROPE_DIM)], axis=1)
    gk = jnp.concatenate([g_kn, z(LANES - HEAD_DIM)], axis=1)
    gkr = jnp.concatenate([z(HEAD_DIM), g_kr, z(LANES - HEAD_DIM - ROPE_DIM)], axis=1)
    return jnp.tile(gq, (1, N_HEADS)), jnp.tile(gk, (1, N_HEADS)), gkr, jnp.tile(g_fq, (1, N_HEADS)), jnp.tile(g_fk, (1, N_HEADS))


def _group_rms(x, g, g_t, inv):
    r = lax.rsqrt(_split_dot(x * x, g) * inv + RMS_EPS)
    return _split_dot(r, g_t)


def _value_layout(wv):
    lane = jnp.arange(N_HEADS * LANES)
    src = (lane // LANES) * HEAD_DIM + lane % LANES
    place = (jnp.arange(N_HEADS * HEAD_DIM)[:, None] == src[None, :]) & (lane % LANES < HEAD_DIM)[None, :]
    ones = (lane % LANES == HEAD_DIM).astype(F32)[None, :]
    wv_il = jnp.pad(wv.reshape(N_HEADS, HEAD_DIM, KV_LORA), ((0, 0), (0, LANES - HEAD_DIM), (0, 0)))
    return place.astype(BF16), ones, wv_il.reshape(N_HEADS * LANES, KV_LORA)


def _prep_fwd(fox, lat, tables, mats, wq, wk, wv, g_qlat, g_kvlat, gains, b_f, v_place, v_ones, wv_il):
    t = fox.shape[0]
    tm = min(PREP_TILE, t)
    c_tab, s1_tab, s2_tab, rope_mask = tables

    def body(fox_ref, lat_ref, c_ref, s1_ref, s2_ref, rm_ref, gm_ref, gmt_ref, im_ref, gf_ref, gft_ref, if_ref,
             wq_ref, wk_ref, wv_ref, gql_ref, gkvl_ref, gq_ref, gk_ref, gkr_ref, gfq_ref, gfk_ref, bf_ref,
             vp_ref, vo_ref, wvil_ref,
             fq_ref, fk_ref, fv_ref, cc_ref, mq_ref, mk_ref, mv_ref, fqn_ref, fkmax_ref, mqn_ref, mkmax_ref,
             fv1_ref, mv1_ref, carry):
        i = pl.program_id(0)
        ct, s1, s2 = c_ref[...], s1_ref[...], s2_ref[...]

        @pl.when(i == 0)
        def _():
            carry[...] = jnp.zeros_like(carry)
            fkmax_ref[...] = jnp.zeros_like(fkmax_ref)
            mkmax_ref[...] = jnp.zeros_like(mkmax_ref)

        def square_sums(ref, g):
            v = ref[...].astype(F32)
            return _split_dot(v * v, g)

        xq = fox_ref[:, 0:512]
        fq_ref[...] = (xq * _group_rms(xq, gf_ref[...], gft_ref[...], if_ref[...]) * (gfq_ref[...] * FOX_SCALE)).astype(BF16)
        xk = fox_ref[:, 512:1024]
        fk_ref[...] = (xk * _group_rms(xk, gf_ref[...], gft_ref[...], if_ref[...]) * gfk_ref[...]).astype(BF16)
        fv_b = fox_ref[:, 1024:1536].astype(BF16)
        fv_ref[...] = fv_b
        fv1_ref[...] = (_dot(fv_b, vp_ref[...]) + vo_ref[...]).astype(BF16)
        fqn_ref[...] = square_sums(fq_ref, gf_ref[...])
        fkmax_ref[...] = jnp.maximum(fkmax_ref[...], jnp.max(square_sums(fk_ref, gf_ref[...]), axis=0, keepdims=True))


        logf = _log_sigmoid(lat_ref[:, LAT_F:LAT_F + N_HEADS] + bf_ref[...])
        cc_ref[...] = jnp.dot(_tri(tm, True), logf, precision=lax.Precision.HIGHEST, preferred_element_type=F32) + carry[...]
        carry[...] += _colsum(logf)

        qlat_n, _ = _rms_fwd(lat_ref[:, LAT_Q:LAT_Q + Q_LORA], gql_ref[...])
        p = _dot(qlat_n.astype(BF16), wq_ref[...], NT)
        y = p * _group_rms(p, gm_ref[...], gmt_ref[...], im_ref[...]) * (gq_ref[...] * MLA_SCALE)
        for h in range(N_HEADS):
            w = slice(h * LANES, (h + 1) * LANES)
            mq_ref[:, w] = _rope_fwd(y[:, w], ct, s1, s2).astype(BF16)

        kv_n, _ = _rms_fwd(lat_ref[:, LAT_KV:LAT_KV + KV_LORA], gkvl_ref[...])
        kv_b = kv_n.astype(BF16)
        pk = _dot(kv_b, wk_ref[...], NT)
        kn = pk * _group_rms(pk, gm_ref[...], gmt_ref[...], im_ref[...]) * gk_ref[...]
        rm = rm_ref[...]
        kr = pltpu.roll(lat_ref[:, LAT_KR:LAT_KR + LANES], HEAD_DIM, 1) * rm
        rr = lax.rsqrt(jnp.sum(kr * kr, axis=1, keepdims=True) * (1.0 / ROPE_DIM) + RMS_EPS)
        okr = _rope_fwd(kr * rr * gkr_ref[...], ct * rm, s1, s2)
        for h in range(N_HEADS):
            w = slice(h * LANES, (h + 1) * LANES)
            mk_ref[:, w] = (kn[:, w] + okr).astype(BF16)
        mv_ref[...] = _dot(kv_b, wv_ref[...], NT).astype(BF16)
        mv1_ref[...] = (_dot(kv_b, wvil_ref[...], NT).astype(BF16).astype(F32) + vo_ref[...]).astype(BF16)
        mqn_ref[...] = square_sums(mq_ref, gm_ref[...])
        mkmax_ref[...] = jnp.maximum(mkmax_ref[...], jnp.max(square_sums(mk_ref, gm_ref[...]), axis=0, keepdims=True))

    consts = [rope_mask, *mats, wq, wk, wv, g_qlat, g_kvlat, *gains, b_f, v_place, v_ones, wv_il]
    t512, tl, tg, g1 = _tok(tm, 512), _tok(tm, LANES), _tok(tm, N_GROUPS), _full((1, N_GROUPS))
    t1024 = _tok(tm, N_HEADS * LANES)
    return pl.pallas_call(
        body, name="prep_fwd", grid=(t // tm,),
        in_specs=[_tok(tm, 1536), _tok(tm, LAT_W), tl, tl, tl] + [_full(a.shape) for a in consts],
        out_specs=[t512, t512, t512, _tok(tm, N_HEADS), t1024, t1024, t512, tg, g1, tg, g1, t1024, t1024],
        out_shape=[_sds((t, 512), BF16), _sds((t, 512), BF16), _sds((t, 512), BF16), _sds((t, N_HEADS), F32),
                   _sds((t, N_HEADS * LANES), BF16), _sds((t, N_HEADS * LANES), BF16), _sds((t, 512), BF16),
                   _sds((t, N_GROUPS), F32), _sds((1, N_GROUPS), F32), _sds((t, N_GROUPS), F32), _sds((1, N_GROUPS), F32)]
        + [_sds((t, N_HEADS * LANES), BF16)] * 2,
        scratch_shapes=[pltpu.VMEM((1, N_HEADS), F32)],
        compiler_params=pltpu.CompilerParams(dimension_semantics=("arbitrary",)),
    )(fox, lat, c_tab, s1_tab, s2_tab, *consts)


def _prep_bwd(fox, lat, tables, mats, wq, wk, wv, g_qlat, g_kvlat, gains, b_f, dfq, dfk, dfv, dc, dmq, dmk, dmv):
    t = fox.shape[0]
    tm = min(PREP_TILE, t)
    n_t = t // tm
    c_tab, s1_tab, s2_tab, rope_mask = tables

    def body(fox_ref, lat_ref, c_ref, s1_ref, s2_ref, rm_ref, gm_ref, gmt_ref, im_ref, gf_ref, gft_ref, if_ref,
             wq_ref, wk_ref, wv_ref, gql_ref, gkvl_ref, gq_ref, gk_ref, gkr_ref, gfq_ref, gfk_ref, bf_ref,
             dfq_ref, dfk_ref, dfv_ref, dc_ref, dmq_ref, dmk_ref, dmv_ref,
             dfox_ref, dlat_ref, dwq_ref, dwk_ref, dwv_ref, o_gql, o_gkvl, o_gq, o_gk, o_gkr, o_gfq, o_gfk, o_bf,
             carry, lscr, wq_acc, wk_acc, wv_acc, dyscr):
        i = pl.program_id(0)
        ct, s1, s2, rm = c_ref[...], s1_ref[...], s2_ref[...], rm_ref[...]

        @pl.when(i == 0)
        def _():
            for ref in [carry, wq_acc, wk_acc, wv_acc, o_gql, o_gkvl, o_gq, o_gk, o_gkr, o_gfq, o_gfk, o_bf]:
                ref[...] = jnp.zeros_like(ref)

        def group_rms_bwd(dy, x, r, gain, g, g_t, inv):
            xn = x * r
            dyg = dy * gain
            mu = _split_dot(_split_dot(dyg * xn, g) * inv, g_t)
            return r * (dyg - xn * mu), dy * xn

        gf, gft, invf = gf_ref[...], gft_ref[...], if_ref[...]
        xq = fox_ref[:, 0:512]
        dxq, rows = group_rms_bwd(dfq_ref[...] * FOX_SCALE, xq, _group_rms(xq, gf, gft, invf), gfq_ref[...], gf, gft, invf)
        dfox_ref[:, 0:512] = dxq.astype(BF16)
        o_gfq[...] += _colsum(rows)
        xk = fox_ref[:, 512:1024]
        dxk, rows = group_rms_bwd(dfk_ref[...].T, xk, _group_rms(xk, gf, gft, invf), gfk_ref[...], gf, gft, invf)
        dfox_ref[:, 512:1024] = dxk.astype(BF16)
        o_gfk[...] += _colsum(rows)
        dfox_ref[:, 1024:1536] = dfv_ref[...].T.astype(BF16)

        lscr[...] = jnp.zeros_like(lscr)

        xql = lat_ref[:, LAT_Q:LAT_Q + Q_LORA]
        qlat_n, r_ql = _rms_fwd(xql, gql_ref[...])
        qlat_b = qlat_n.astype(BF16)
        gm, gmt, invm = gm_ref[...], gmt_ref[...], im_ref[...]
        p = _dot(qlat_b, wq_ref[...], NT)
        for h in range(N_HEADS):
            w = slice(h * LANES, (h + 1) * LANES)
            dyscr[:, w] = _rope_bwd(dmq_ref[:, w], ct, s1, s2) * MLA_SCALE
        dp, rows = group_rms_bwd(dyscr[...], p, _group_rms(p, gm, gmt, invm), gq_ref[...], gm, gmt, invm)
        o_gq[...] += _colsum(rows)
        dp_b = dp.astype(BF16)
        wq_acc[...] += _dot(dp_b, qlat_b, TN)
        dxql, rows = _rms_bwd(_dot(dp_b, wq_ref[...]), xql, gql_ref[...], r_ql)
        lscr[:, LAT_Q:LAT_Q + Q_LORA] = dxql
        o_gql[...] += _colsum(rows)

        xkv = lat_ref[:, LAT_KV:LAT_KV + KV_LORA]
        kv_n, r_kv = _rms_fwd(xkv, gkvl_ref[...])
        kv_b = kv_n.astype(BF16)
        pk = _dot(kv_b, wk_ref[...], NT)
        dmk = dmk_ref[...].T
        dpk, rows = group_rms_bwd(dmk, pk, _group_rms(pk, gm, gmt, invm), gk_ref[...], gm, gmt, invm)
        o_gk[...] += _colsum(rows)
        dpk_b = dpk.astype(BF16)
        dv_b = dmv_ref[...].T.astype(BF16)
        wk_acc[...] += _dot(dpk_b, kv_b, TN)
        wv_acc[...] += _dot(dv_b, kv_b, TN)
        dxkv, rows = _rms_bwd(_dot(dpk_b, wk_ref[...]) + _dot(dv_b, wv_ref[...]), xkv, gkvl_ref[...], r_kv)
        lscr[:, LAT_KV:LAT_KV + KV_LORA] = dxkv
        o_gkvl[...] += _colsum(rows)

        dokr = dmk[:, 0:LANES]
        for h in range(1, N_HEADS):
            dokr = dokr + dmk[:, h * LANES:(h + 1) * LANES]
        dykr = _rope_bwd(dokr * rm, ct * rm, s1, s2)
        kr = pltpu.roll(lat_ref[:, LAT_KR:LAT_KR + LANES], HEAD_DIM, 1) * rm
        rr = lax.rsqrt(jnp.sum(kr * kr, axis=1, keepdims=True) * (1.0 / ROPE_DIM) + RMS_EPS)
        krn = kr * rr
        dyg = dykr * gkr_ref[...]
        dkr = rr * (dyg - krn * (jnp.sum(dyg * krn, axis=1, keepdims=True) * (1.0 / ROPE_DIM)))
        o_gkr[...] += _colsum(dykr * krn)
        lscr[:, LAT_KR:LAT_KR + LANES] = pltpu.roll(dkr, LANES - HEAD_DIM, 1)

        dcv = dc_ref[...]
        dlogf = jnp.dot(_tri(tm, False), dcv, precision=lax.Precision.HIGHEST, preferred_element_type=F32) + carry[...]
        carry[...] += _colsum(dcv)
        dz = dlogf * jax.nn.sigmoid(-(lat_ref[:, LAT_F:LAT_F + N_HEADS] + bf_ref[...]))
        lscr[:, LAT_F:LAT_F + N_HEADS] = dz
        o_bf[...] += _colsum(dz)

        dlat_ref[...] = lscr[...].astype(BF16)

        @pl.when(i == n_t - 1)
        def _():
            dwq_ref[...] = wq_acc[...].astype(BF16)
            dwk_ref[...] = wk_acc[...].astype(BF16)
            dwv_ref[...] = wv_acc[...].astype(BF16)

    consts = [rope_mask, *mats, wq, wk, wv, g_qlat, g_kvlat, *gains, b_f]

    def rtok(n):
        return pl.BlockSpec((tm, n), lambda i: (n_t - 1 - i, 0))

    def rtok_t(n):
        return pl.BlockSpec((n, tm), lambda i: (0, n_t - 1 - i))

    sums =[(1, Q_LORA), (1, KV_LORA), (1, N_HEADS * LANES), (1, N_HEADS * LANES), (1, LANES), (1, 512), (1, 512), (1, N_HEADS)]
    return pl.pallas_call(
        body, name="prep_bwd", grid=(n_t,),
        in_specs=[rtok(1536), rtok(LAT_W), rtok(LANES), rtok(LANES), rtok(LANES)] + [_full(a.shape) for a in consts]
        + [rtok(512), rtok_t(512), rtok_t(512), rtok(N_HEADS), rtok(N_HEADS * LANES), rtok_t(N_HEADS * LANES), rtok_t(512)],
        out_specs=[rtok(1536), rtok(LAT_W), _full(wq.shape), _full(wk.shape), _full(wv.shape)] + [_full(s) for s in sums],
        out_shape=[_sds((t, 1536), BF16), _sds((t, LAT_W), BF16), _sds(wq.shape, BF16), _sds(wk.shape, BF16), _sds(wv.shape, BF16)]
        + [_sds(s, F32) for s in sums],
        scratch_shapes=[pltpu.VMEM((1, N_HEADS), F32), pltpu.VMEM((tm, LAT_W), F32), pltpu.VMEM(wq.shape, F32),
                        pltpu.VMEM(wk.shape, F32), pltpu.VMEM(wv.shape, F32), pltpu.VMEM((tm, N_HEADS * LANES), F32)],
        compiler_params=pltpu.CompilerParams(dimension_semantics=("arbitrary",)),
    )(fox, lat, c_tab, s1_tab, s2_tab, *consts, dfq, dfk, dfv, dc, dmq, dmk, dmv)


def _causal_pairs(n, query_major):
    pairs = [(i, j) for i in range(n) for j in range(i + 1)] if query_major else [(i, j) for j in range(n) for i in range(j, n)]
    return jnp.asarray([p[0] for p in pairs], jnp.int32), jnp.asarray([p[1] for p in pairs], jnp.int32)


def _logit_bound(q_sq, k_sq_max):
    qn = q_sq[:, :N_HEADS] + q_sq[:, N_HEADS:]
    kmax = k_sq_max[:, :N_HEADS] + k_sq_max[:, N_HEADS:]
    bound = jnp.sqrt(qn * kmax) * (1.0 + 2.0 ** -10) + 2.0 ** -10
    flag = (jnp.max(bound) <= FIXED_SHIFT_MAX_BOUND).astype(F32).reshape(1, 1)
    return bound, flag


def _attn_fwd(q, k, v, bound, fixed_ok, c, c_t, *, lanes, name, exchange=None):
    t = q.shape[0]
    tq = min(ATT_TILE, t)
    n_q = t // tq
    hd = HEAD_DIM
    ch = min(ATT_COL_CHUNK, tq)
    decay = c is not None

    def body(qi_ref, kj_ref, *refs):
        if decay:
            q_ref, k_ref, v_ref, b_ref, ok_ref, c_ref, ct_ref, o_ref, o32_ref, lse_ref, m_scr, acc = refs
        else:
            q_ref, k_ref, v_ref, b_ref, ok_ref, o_ref, o32_ref, lse_ref, m_scr, acc = refs
        i, j = qi_ref[pl.program_id(0)], kj_ref[pl.program_id(0)]
        fixed = ok_ref[0, 0] > 0.5

        @pl.when(j == 0)
        def _():
            m_scr[...] = jnp.full_like(m_scr, MASK_VALUE)
            acc[...] = jnp.zeros_like(acc)

        def fixed_step(diagonal):
            for h in range(N_HEADS):
                wl = slice(h * lanes, (h + 1) * lanes)
                wv = slice(h * LANES, (h + 1) * LANES)
                qh = q_ref[:, wl]
                row = (c_ref[:, h:h + 1] - b_ref[:, h:h + 1]) if decay else -b_ref[:, h:h + 1]
                o_hi = jnp.zeros((tq, LANES), F32)
                o_lo = jnp.zeros((tq, LANES), F32)
                for cc in range(tq // ch):
                    cols = slice(cc * ch, (cc + 1) * ch)
                    s = _dot(qh, k_ref[cols, wl], NT)
                    s = s + ((row - ct_ref[h:h + 1, cols]) if decay else row)
                    if diagonal:
                        keep = (lax.broadcasted_iota(jnp.int32, (tq, ch), 0)
                                >= lax.broadcasted_iota(jnp.int32, (tq, ch), 1) + cc * ch)
                        s = jnp.where(keep, s, MASK_VALUE)
                    p = jnp.exp(s)
                    p_b = p.astype(BF16)
                    o_hi = o_hi + _dot(p_b, v_ref[cols, wv])
                    if decay:
                        o_lo = o_lo + _dot((p - p_b.astype(F32)).astype(BF16), v_ref[cols, wv])
                acc[0, :, wv] += o_hi
                if decay:
                    acc[1, :, wv] += o_lo

        def step(diagonal):
            if diagonal:
                keep = lax.broadcasted_iota(jnp.int32, (tq, tq), 0) >= lax.broadcasted_iota(jnp.int32, (tq, tq), 1)
            for h in range(N_HEADS):
                s = _dot(q_ref[:, h * lanes:(h + 1) * lanes], k_ref[:, h * lanes:(h + 1) * lanes], NT)
                if decay:
                    s = s + (c_ref[:, h:h + 1] - ct_ref[h:h + 1, :])
                if diagonal:
                    s = jnp.where(keep, s, MASK_VALUE)
                m_prev = m_scr[h]
                m_new = jnp.maximum(m_prev, jnp.max(s, axis=1, keepdims=True))
                alpha = jnp.exp(m_prev - m_new)
                p = jnp.exp(s - m_new)
                wv = slice(h * LANES, (h + 1) * LANES)
                p_b = p.astype(BF16)
                acc[0, :, wv] = alpha * acc[0, :, wv] + _dot(p_b, v_ref[:, wv])
                if decay:
                    acc[1, :, wv] = alpha * acc[1, :, wv] + _dot((p - p_b.astype(F32)).astype(BF16), v_ref[:, wv])
                m_scr[h] = m_new

        for diagonal, here in ((False, j < i), (True, j == i)):
            @pl.when(here & fixed)
            def _():
                fixed_step(diagonal)

            @pl.when(here & jnp.logical_not(fixed))
            def _():
                step(diagonal)

        @pl.when(j == i)
        def _():
            for h in range(N_HEADS):
                w = slice(h * hd, (h + 1) * hd)
                val, one = slice(h * LANES, h * LANES + hd), slice(h * LANES + hd, h * LANES + hd + 1)
                l = (acc[0, :, one] + acc[1, :, one]) if decay else acc[0, :, one]
                o_ref[:, w] = (acc[0, :, val] / l).astype(BF16)
                o32_ref[:, w] = ((acc[0, :, val] + acc[1, :, val]) if decay else acc[0, :, val]) / l
                lse_ref[:, h:h + 1] = jnp.where(fixed, b_ref[:, h:h + 1], m_scr[h]) + jnp.log(l)

    qspec = lambda n: pl.BlockSpec((tq, n), lambda s, qi, kj: (qi[s], 0))
    kspec = lambda n: pl.BlockSpec((tq, n), lambda s, qi, kj: (kj[s], 0))
    in_specs = [qspec(N_HEADS * lanes), kspec(N_HEADS * lanes), kspec(N_HEADS * LANES), qspec(N_HEADS),
                pl.BlockSpec(memory_space=pltpu.SMEM)]
    args = [q, k, v, bound, fixed_ok]
    if decay:
        in_specs += [qspec(N_HEADS), pl.BlockSpec((N_HEADS, tq), lambda s, qi, kj: (0, kj[s]))]
        args += [c, c_t]
    pairs = _causal_pairs(n_q, query_major=True)
    return _gridded_call(
        body, name=name, grid=(pairs[0].shape[0],), in_specs=in_specs, out_specs=[qspec(512), qspec(512), qspec(N_HEADS)],
        out_shape=[_sds((t, 512), BF16), _sds((t, 512), F32), _sds((t, N_HEADS), F32)],
        scratch_shapes=[pltpu.VMEM((N_HEADS, tq, 1), F32), pltpu.VMEM((2, tq, N_HEADS * LANES), F32)],
        args=args, exchange=exchange, prefetch=pairs)


def _attn_bwd(q, k, v, do, delta, lse, c, c_t, *, lanes, name, exchange=None):
    t = q.shape[0]
    tq = min(ATT_TILE, t)
    n_q = t // tq
    hd = HEAD_DIM
    decay = c is not None

    pairs = _causal_pairs(n_q, query_major=False)
    n_steps = pairs[0].shape[0]

    def body(qi_ref, kj_ref, *refs):
        if decay:
            q_ref, k_ref, v_ref, do_ref, delta_ref, lse_ref, c_ref, ct_ref, dq_hbm, dk_ref, dv_ref, dct_ref, dq_ref = refs
        else:
            q_ref, k_ref, v_ref, do_ref, delta_ref, lse_ref, dq_hbm, dk_ref, dv_ref, dq_ref = refs
        step_id = pl.program_id(0)
        i, j = qi_ref[step_id], kj_ref[step_id]

        @pl.when(step_id == 0)
        def _():
            dq_ref[...] = jnp.zeros_like(dq_ref)

        @pl.when(i == j)
        def _():
            dk_ref[...] = jnp.zeros_like(dk_ref)
            dv_ref[...] = jnp.zeros_like(dv_ref)
            if decay:
                dct_ref[...] = jnp.zeros_like(dct_ref)

        def step(diagonal):
            if diagonal:
                keep = lax.broadcasted_iota(jnp.int32, (tq, tq), 0) >= lax.broadcasted_iota(jnp.int32, (tq, tq), 1)
            rows = pl.ds(pl.multiple_of(i * tq, tq), tq)
            for h in range(N_HEADS):
                wl = slice(h * lanes, (h + 1) * lanes)
                w = slice(h * hd, (h + 1) * hd)
                qh, kh = q_ref[:, wl], k_ref[:, wl]
                s = _dot(qh, kh, NT)
                if decay:
                    s = s + (c_ref[:, h:h + 1] - ct_ref[h:h + 1, :])
                if diagonal:
                    s = jnp.where(keep, s, MASK_VALUE)
                p = jnp.exp(s - lse_ref[:, h:h + 1])
                doh = do_ref[:, w]
                dv_ref[w, :] += _dot(doh, p.astype(BF16), TN)
                dp = _dot(doh, v_ref[:, w], NT)
                ds = p * (dp - delta_ref[:, h:h + 1])
                if decay:
                    dct_ref[h:h + 1, :] -= _colsum(ds)
                ds_b = ds.astype(BF16)
                dk_ref[wl, :] += _dot(qh, ds_b, TN)
                dq_ref[rows, wl] += _dot(ds_b, kh)

        @pl.when(i > j)
        def _():
            step(False)

        @pl.when(i == j)
        def _():
            step(True)

        @pl.when(step_id == n_steps - 1)
        def _():
            pltpu.sync_copy(dq_ref, dq_hbm)

    qspec = lambda n: pl.BlockSpec((tq, n), lambda s, qi, kj: (qi[s], 0))
    kspec = lambda n: pl.BlockSpec((tq, n), lambda s, qi, kj: (kj[s], 0))
    in_specs = [qspec(N_HEADS * lanes), kspec(N_HEADS * lanes), kspec(512), qspec(512), qspec(N_HEADS), qspec(N_HEADS)]
    kspec_t = lambda n: pl.BlockSpec((n, tq), lambda s, qi, kj: (0, kj[s]))
    out_specs = [pl.BlockSpec(memory_space=pl.ANY), kspec_t(N_HEADS * lanes), kspec_t(512)]
    out_shape = [_sds((t, N_HEADS * lanes), F32), _sds((N_HEADS * lanes, t), F32), _sds((512, t), F32)]
    args = [q, k, v, do, delta, lse]
    if decay:
        ctspec = pl.BlockSpec((N_HEADS, tq), lambda s, qi, kj: (0, kj[s]))
        in_specs += [qspec(N_HEADS), ctspec]
        out_specs.append(ctspec)
        out_shape.append(_sds((N_HEADS, t), F32))
        args += [c, c_t]
    return _gridded_call(body, name=name, grid=(n_steps,), in_specs=in_specs, out_specs=out_specs, out_shape=out_shape,
                         scratch_shapes=[pltpu.VMEM((t, N_HEADS * lanes), F32)], args=args, exchange=exchange, prefetch=pairs)


def _mix_fwd(x, y_mla, y_fox, gates, b_gate, wbm_t, wbf_t, wo):
    t = x.shape[0]
    tm = min(PROJ_TILE, t)

    def body(x_ref, ym_ref, yf_ref, gt_ref, bg_ref, wbm_ref, wbf_ref, wo_ref, out_ref):
        um = _dot(ym_ref[...], wbm_ref[...], NT)
        uf = _dot(yf_ref[...], wbf_ref[...], NT)
        sm = jax.nn.sigmoid(gt_ref[:, 0:D_MODEL] + bg_ref[0:1, :])
        sf = jax.nn.sigmoid(gt_ref[:, D_MODEL:2 * D_MODEL] + bg_ref[1:2, :])
        mixed = sm * um + sf * uf
        out_ref[...] = x_ref[...] + _dot(mixed.astype(BF16), wo_ref[...])

    tok = _tok(tm, D_MODEL)
    return pl.pallas_call(
        body, name="mix_fwd", grid=(t // tm,),
        in_specs=[tok, _tok(tm, 512), _tok(tm, 512), _tok(tm, 2 * D_MODEL), _full((2, D_MODEL)), _full(wbm_t.shape),
                  _full(wbf_t.shape), _full(wo.shape)],
        out_specs=tok, out_shape=_sds((t, D_MODEL), F32),
        compiler_params=pltpu.CompilerParams(dimension_semantics=("arbitrary",)),
    )(x, y_mla, y_fox, gates, b_gate, wbm_t, wbf_t, wo)


def _mix_bwd(dx, y_mla, y_fox, y_mla32, y_fox32, gates, b_gate, wbm_t, wbf_t, wo, exchange=None):
    t = dx.shape[0]
    tm = min(PROJ_TILE, t)

    def body(dx_ref, ym_ref, yf_ref, ym32_ref, yf32_ref, gt_ref, bg_ref, wbm_ref, wbf_ref, wo_ref,
             dym_ref, dyf_ref, dlm_ref, dlf_ref, dgt_ref, mixed_ref, dum_ref, duf_ref, dxb_ref, dbg_ref, prod):
        i = pl.program_id(0)
        dxb = dx_ref[...].astype(BF16)
        dxb_ref[...] = dxb
        dmixed = _dot(dxb, wo_ref[...], NT)
        um = _dot(ym_ref[...], wbm_ref[...], NT)
        uf = _dot(yf_ref[...], wbf_ref[...], NT)
        sm = jax.nn.sigmoid(gt_ref[:, 0:D_MODEL] + bg_ref[0:1, :])
        sf = jax.nn.sigmoid(gt_ref[:, D_MODEL:2 * D_MODEL] + bg_ref[1:2, :])
        mixed_ref[...] = (sm * um + sf * uf).astype(BF16)
        dum = (dmixed * sm).astype(BF16)
        duf = (dmixed * sf).astype(BF16)
        dum_ref[...] = dum
        duf_ref[...] = duf
        dgm = dmixed * um * (sm * (1.0 - sm))
        dgf = dmixed * uf * (sf * (1.0 - sf))
        dgt_ref[:, 0:D_MODEL] = dgm.astype(BF16)
        dgt_ref[:, D_MODEL:2 * D_MODEL] = dgf.astype(BF16)
        for du, wb_ref, y32_ref, dy_ref, dl_ref in ((dum, wbm_ref, ym32_ref, dym_ref, dlm_ref),
                                                    (duf, wbf_ref, yf32_ref, dyf_ref, dlf_ref)):
            dy = _dot(du, wb_ref[...])
            dy_ref[...] = dy.astype(BF16)
            prod[...] = dy.astype(BF16).astype(F32) * y32_ref[...]
            for h in range(N_HEADS):
                dl_ref[:, h:h + 1] = jnp.sum(prod[:, h * HEAD_DIM:(h + 1) * HEAD_DIM], axis=1, keepdims=True)

        @pl.when(i == 0)
        def _():
            dbg_ref[...] = jnp.zeros_like(dbg_ref)

        dbg_ref[0:1, :] += _colsum(dgm)
        dbg_ref[1:2, :] += _colsum(dgf)

    tok = _tok(tm, D_MODEL)
    tokb = _sds((t, D_MODEL), BF16)
    t512, t8 = _tok(tm, 512), _tok(tm, N_HEADS)
    return _gridded_call(
        body, name="mix_bwd", grid=(t // tm,),
        in_specs=[tok, t512, t512, t512, t512, _tok(tm, 2 * D_MODEL), _full((2, D_MODEL)), _full(wbm_t.shape),
                  _full(wbf_t.shape), _full(wo.shape)],
        out_specs=[t512, t512, t8, t8, _tok(tm, 2 * D_MODEL), tok, tok, tok, tok, _full((2, D_MODEL))],
        out_shape=[_sds((t, 512), BF16), _sds((t, 512), BF16), _sds((t, N_HEADS), F32), _sds((t, N_HEADS), F32),
                   _sds((t, 2 * D_MODEL), BF16), tokb, tokb, tokb, tokb, _sds((2, D_MODEL), F32)],
        scratch_shapes=[pltpu.VMEM((tm, 512), F32)],
        args=[dx, y_mla, y_fox, y_mla32, y_fox32, gates, b_gate, wbm_t, wbf_t, wo], exchange=exchange)


def _my_position():
    x, y, c = lax.axis_index("x"), lax.axis_index("y"), lax.axis_index("c")
    return x, y, c, 4 * x + 2 * y + c


def _peer(x, y, c, mask):
    px = 1 - x if mask & 4 else x
    py = 1 - y if mask & 2 else y
    pc = 1 - c if mask & 1 else c
    return (px, py, pc), 4 * px + 2 * py + pc


def _chip_peer(x, y, km):
    px = 1 - x if km & 2 else x
    py = 1 - y if km & 1 else y
    return px, py, 2 * px + py


_HBM = pl.BlockSpec(memory_space=pl.ANY)


def _wait_all(copies):
    for cp in copies:
        cp.wait()


class _ChipExchange:
    def __init__(self, gather, arrays):
        self.gather, self.arrays, self.aliased = gather, list(arrays), False
        n = len(self.arrays)
        self.out_shape = [_sds((N_DEV * a.shape[0],) + a.shape[1:], a.dtype) if gather else _sds(a.shape, a.dtype)
                          for a in self.arrays]
        self.scratch_shapes = [pltpu.SemaphoreType.DMA((n, N_CHIP)), pltpu.SemaphoreType.DMA((n, N_CHIP)),
                               pltpu.SemaphoreType.DMA((n,))]

    def copies(self, srcs, dsts, send_sems, recv_sems, local_sems):
        x, y, c, me = _my_position()
        q_me = 2 * x + y
        out = []
        for a in range(len(self.arrays)):
            if self.gather:
                r = srcs[a].shape[0]
                local_src, dst = srcs[a], dsts[a].at[pl.ds(me * r, r)]
            else:
                local_src, dst = srcs[a].at[q_me], dsts[a].at[q_me]
            out.append(pltpu.make_async_copy(local_src, dst, local_sems.at[a]))
            for km in range(1, N_CHIP):
                px, py, q_peer = _chip_peer(x, y, km)
                out.append(pltpu.make_async_remote_copy(
                    src_ref=srcs[a] if self.gather else srcs[a].at[q_peer], dst_ref=dst, send_sem=send_sems.at[a, km],
                    recv_sem=recv_sems.at[a, km], device_id=(px, py, c), device_id_type=MESH))
        return out

    def standalone(self, name):
        n = len(self.arrays)

        def body(*refs):
            copies = self.copies(refs[:n], refs[n:2 * n], *refs[2 * n:])
            for cp in copies:
                cp.start()
            _wait_all(copies)

        return pl.pallas_call(body, name=name, in_specs=[_HBM] * n, out_specs=[_HBM] * n, out_shape=self.out_shape,
                              scratch_shapes=self.scratch_shapes)(*self.arrays)


def _gridded_call(body, *, name, grid, in_specs, out_specs, out_shape, scratch_shapes, args, exchange=None, prefetch=()):
    params = pltpu.CompilerParams(dimension_semantics=("arbitrary",) * len(grid))
    n_pre, n_in, n_out, n_scr = len(prefetch), len(in_specs), len(out_specs), len(scratch_shapes)
    n_x = 0 if exchange is None else len(exchange.arrays)

    def carrier(*refs):
        pre, refs = refs[:n_pre], refs[n_pre:]
        ins, x_src, refs = refs[:n_in], refs[n_in:n_in + n_x], refs[n_in + n_x:]
        outs, x_dst, refs = refs[:n_out], refs[n_out:n_out + n_x], refs[n_out + n_x:]
        copies = exchange.copies(x_src, x_dst, *refs[n_scr:])
        pids = [pl.program_id(d) for d in range(len(grid))]
        first = functools.reduce(jnp.logical_and, [p == 0 for p in pids])
        last = functools.reduce(jnp.logical_and, [p == g - 1 for p, g in zip(pids, grid)])

        @pl.when(first)
        def _():
            for cp in copies:
                cp.start()

        body(*pre, *ins, *outs, *refs[:n_scr])

        @pl.when(last)
        def _():
            _wait_all(copies)

    x_shapes, x_scratch, x_args = ([], [], []) if exchange is None else (exchange.out_shape, exchange.scratch_shapes, exchange.arrays)
    spec = pltpu.PrefetchScalarGridSpec(
        num_scalar_prefetch=n_pre, grid=grid, in_specs=list(in_specs) + [_HBM] * n_x, out_specs=list(out_specs) + [_HBM] * n_x,
        scratch_shapes=list(scratch_shapes) + x_scratch)
    in_place = {n_pre + n_in + k: n_out + k for k in range(n_x)} if n_x and exchange.aliased else {}
    res = pl.pallas_call(body if exchange is None else carrier, name=name, grid_spec=spec, out_shape=list(out_shape) + x_shapes,
                         input_output_aliases=in_place, compiler_params=params)(*prefetch, *args, *x_args)
    return res[:n_out], (None if exchange is None else res[n_out:])


class _CoreExchange:
    def __init__(self, gather, arrays):
        self.gather, self.arrays, self.aliased = gather, list(arrays), gather
        n = len(self.arrays)
        self.out_shape = [_sds(a.shape, a.dtype) if gather else _sds((N_CHIP, a.shape[0] // N_DEV) + a.shape[1:], a.dtype)
                          for a in self.arrays]
        self.scratch_shapes = [pltpu.SemaphoreType.DMA((n, N_CHIP)), pltpu.SemaphoreType.DMA((n, N_CHIP))]

    def copies(self, srcs, dsts, send_sems, recv_sems):
        x, y, c, _ = _my_position()
        out = []
        for a in range(len(self.arrays)):
            r = srcs[a].shape[0] // N_DEV
            for q in range(N_CHIP):
                if self.gather:
                    rows = pl.ds((2 * q + c) * r, r)
                    src, dst = srcs[a].at[rows], dsts[a].at[rows]
                else:
                    src, dst = srcs[a].at[pl.ds((2 * q + 1 - c) * r, r)], dsts[a].at[q]
                out.append(pltpu.make_async_remote_copy(src_ref=src, dst_ref=dst, send_sem=send_sems.at[a, q],
                                                        recv_sem=recv_sems.at[a, q], device_id=(x, y, 1 - c), device_id_type=MESH))
        return out

    def standalone(self, name):
        n = len(self.arrays)

        def body(*refs):
            copies = self.copies(refs[:n], refs[n:2 * n], *refs[2 * n:])
            for cp in copies:
                cp.start()
            _wait_all(copies)

        return pl.pallas_call(body, name=name, in_specs=[_HBM] * n, out_specs=[_HBM] * n, out_shape=self.out_shape,
                              input_output_aliases={a: a for a in range(n)} if self.aliased else {},
                              scratch_shapes=self.scratch_shapes)(*self.arrays)


def _gather_over_cores(arrays, name):
    return _CoreExchange(True, arrays).standalone(name)


def _grads_to_sibling(grads, name):
    return _CoreExchange(False, grads).standalone(name)


def _pair_sum(grad, from_sibling, name):
    r, n = from_sibling.shape[1:]

    def body(g_ref, s_ref, o_ref):
        c = lax.axis_index("c")
        o_ref[...] = (g_ref[c].astype(F32) + s_ref[...].astype(F32)).astype(BF16)

    return pl.pallas_call(
        body, name=name, grid=(N_CHIP,),
        in_specs=[pl.BlockSpec((None, 2, r, n), lambda q: (q, 0, 0, 0)), pl.BlockSpec((None, r, n), lambda q: (q, 0, 0))],
        out_specs=pl.BlockSpec((None, r, n), lambda q: (q, 0, 0)), out_shape=_sds((N_CHIP, r, n), BF16),
    )(grad.reshape(N_CHIP, 2, r, n), from_sibling)


def _all_reduce_small(vec):
    r = vec.shape[0]

    def body(v_ref, o_ref, buf, send_sems, recv_sems):
        x, y, c, me = _my_position()
        buf[me] = v_ref[...]
        copies = []
        for mask in range(1, N_DEV):
            peer, _ = _peer(x, y, c, mask)
            cp = pltpu.make_async_remote_copy(src_ref=v_ref, dst_ref=buf.at[me], send_sem=send_sems.at[mask],
                                              recv_sem=recv_sems.at[mask], device_id=peer, device_id_type=MESH)
            cp.start()
            copies.append(cp)
        for cp in copies:
            cp.wait()
        total = buf[0]
        for s in range(1, N_DEV):
            total = total + buf[s]
        o_ref[...] = total

    vm = pl.BlockSpec(memory_space=pltpu.VMEM)
    return pl.pallas_call(
        body, name="all_reduce_small", in_specs=[vm], out_specs=vm, out_shape=_sds(vec.shape, F32),
        scratch_shapes=[pltpu.VMEM((N_DEV, r, 128), F32), pltpu.SemaphoreType.DMA((N_DEV,)), pltpu.SemaphoreType.DMA((N_DEV,))],
    )(vec)


def _adamw_math(w, g, m, v):
    m = ADAM_B1 * m + (1.0 - ADAM_B1) * g
    v = ADAM_B2 * v + (1.0 - ADAM_B2) * (g * g)
    m_hat = m / (1.0 - ADAM_B1 ** ADAM_STEP)
    v_hat = v / (1.0 - ADAM_B2 ** ADAM_STEP)
    delta = -ADAM_LR * (m_hat / (jnp.sqrt(v_hat) + ADAM_EPS) + ADAM_WD * w)
    return delta, m, v


def _reduce_adamw(slots, w, m, v, *, transpose, name):
    r, n = slots.shape[1:]
    padded = not transpose and w.shape[0] != r

    def body(s_ref, w_ref, m_ref, v_ref, g_ref, d_ref, nm_ref, nv_ref, *scr):
        g = s_ref[0].astype(F32)
        for s in range(1, slots.shape[0]):
            g = g + s_ref[s].astype(F32)
        if transpose:
            scr[0][...] = g.T
            g = scr[0][:, 0:w_ref.shape[1]]
        elif padded:
            scr[0][...] = g
            g = scr[0][0:w_ref.shape[0], :]
        g_ref[...] = g
        d_ref[...], nm_ref[...], nv_ref[...] = _adamw_math(w_ref[...], g, m_ref[...], v_ref[...])

    out = _sds(w.shape, F32)
    if not transpose and not padded and r % (2 * BF16_ROW_TILE) == 0:
        half = pl.BlockSpec((r // 2, n), lambda i: (i, 0))
        return pl.pallas_call(
            body, name=name, grid=(2,), in_specs=[pl.BlockSpec((slots.shape[0], r // 2, n), lambda i: (0, i, 0)), half, half, half],
            out_specs=[half] * 4, out_shape=[out, out, out, out])(slots, w, m, v)
    return pl.pallas_call(
        body, name=name, out_shape=[out, out, out, out],
        scratch_shapes=[pltpu.VMEM((n, r) if transpose else (r, n), F32)] if transpose or padded else [],
    )(slots, w, m, v)


def _adamw(g, w, m, v, name):
    def body(g_ref, w_ref, m_ref, v_ref, d_ref, nm_ref, nv_ref):
        d_ref[...], nm_ref[...], nv_ref[...] = _adamw_math(w_ref[...], g_ref[...], m_ref[...], v_ref[...])

    out = _sds(w.shape, F32)
    return pl.pallas_call(body, name=name, out_shape=[out, out, out])(g, w, m, v)


_SMALL = ["ffn1_norm", "mix_norm", "ffn2_norm", "mla_q_lat_norm", "mla_kv_lat_norm", "mla_q_nope_gain", "mla_q_rope_gain",
          "mla_k_nope_gain", "mla_k_rope_gain", "fox_q_gain", "fox_k_gain", "fox_b_f"]
_WEIGHTS = ["ffn1_norm", "ffn1_w_gate", "ffn1_w_up", "ffn1_w_down", "mix_norm", "w_in", "mla_q_lat_norm", "mla_w_qb",
            "mla_kv_lat_norm", "mla_w_kvb", "mla_q_nope_gain", "mla_q_rope_gain", "mla_k_nope_gain", "mla_k_rope_gain",
            "fox_q_gain", "fox_k_gain", "fox_b_f", "w_branch_mla", "w_branch_fox", "b_gate", "w_o", "ffn2_norm",
            "ffn2_w_gate", "ffn2_w_up", "ffn2_w_down"]
_IN_Q, _IN_KV, _IN_KR, _IN_FOX, _IN_F, _IN_GATES = (0, 192), (192, 128), (320, 32), (352, 1536), (1888, 8), (1896, 2048)


def _rows(a, seg):
    return a[seg[0]:seg[0] + seg[1]]


def _split_w_in(win_t):
    z = lambda n: jnp.zeros((n, D_MODEL), win_t.dtype)
    lat = jnp.concatenate([_rows(win_t, _IN_Q), z(LAT_KV - Q_LORA), _rows(win_t, _IN_KV), _rows(win_t, _IN_KR),
                           _rows(win_t, _IN_F), z(LAT_W - LAT_F - N_HEADS)], axis=0)
    return _rows(win_t, _IN_GATES), _rows(win_t, _IN_FOX), lat


def _join_w_in(d_gates, d_fox, d_lat):
    return jnp.concatenate([d_lat[LAT_Q:LAT_Q + Q_LORA], d_lat[LAT_KV:LAT_KV + KV_LORA], d_lat[LAT_KR:LAT_KR + ROPE_DIM],
                            d_fox, d_lat[LAT_F:LAT_F + N_HEADS], d_gates], axis=0)


def kernel(x, positions, ffn1_norm, ffn1_w_gate, ffn1_w_up, ffn1_w_down, mix_norm, w_in, mla_q_lat_norm, mla_w_qb, mla_kv_lat_norm, mla_w_kvb, mla_q_nope_gain, mla_q_rope_gain, mla_k_nope_gain, mla_k_rope_gain, fox_q_gain, fox_k_gain, fox_b_f, w_branch_mla, w_branch_fox, b_gate, w_o, ffn2_norm, ffn2_w_gate, ffn2_w_up, ffn2_w_down, loss_target, m_ffn1_norm, m_ffn1_w_gate, m_ffn1_w_up, m_ffn1_w_down, m_mix_norm, m_w_in, m_mla_q_lat_norm, m_mla_w_qb, m_mla_kv_lat_norm, m_mla_w_kvb, m_mla_q_nope_gain, m_mla_q_rope_gain, m_mla_k_nope_gain, m_mla_k_rope_gain, m_fox_q_gain, m_fox_k_gain, m_fox_b_f, m_w_branch_mla, m_w_branch_fox, m_b_gate, m_w_o, m_ffn2_norm, m_ffn2_w_gate, m_ffn2_w_up, m_ffn2_w_down, v_ffn1_norm, v_ffn1_w_gate, v_ffn1_w_up, v_ffn1_w_down, v_mix_norm, v_w_in, v_mla_q_lat_norm, v_mla_w_qb, v_mla_kv_lat_norm, v_mla_w_kvb, v_mla_q_nope_gain, v_mla_q_rope_gain, v_mla_k_nope_gain, v_mla_k_rope_gain, v_fox_q_gain, v_fox_k_gain, v_fox_b_f, v_w_branch_mla, v_w_branch_fox, v_b_gate, v_w_o, v_ffn2_norm, v_ffn2_w_gate, v_ffn2_w_up, v_ffn2_w_down):
    env = dict(locals())
    strip = lambda n, a: a if n in _SMALL else a[0]
    W = {n: strip(n, env[n]) for n in _WEIGHTS}
    M = {n: strip(n, env["m_" + n]) for n in _WEIGHTS}
    V = {n: strip(n, env["v_" + n]) for n in _WEIGHTS}
    xs = x[0]
    t = xs.shape[0]

    col_split = ["ffn1_w_gate", "ffn1_w_up", "ffn2_w_gate", "ffn2_w_up", "mla_w_qb", "mla_w_kvb", "w_branch_mla", "w_branch_fox"]
    row_split = ["ffn1_w_down", "ffn2_w_down", "w_o"]
    pieces = {n: W[n].T.astype(BF16) for n in col_split}
    pieces.update({n: W[n].astype(BF16) for n in row_split})
    pieces["w_in"] = jnp.pad(W["w_in"].T.astype(BF16), ((0, W_IN_PIECE_PAD - W_IN_PIECE), (0, 0)))
    pieces["b_gate"] = W["b_gate"].T
    group_a = ["ffn1_w_gate", "ffn1_w_up", "ffn1_w_down"]
    group_b = ["w_in", "mla_w_qb", "mla_w_kvb", "w_branch_mla", "w_branch_fox", "w_o", "b_gate"]
    group_c = ["ffn2_w_gate", "ffn2_w_up", "ffn2_w_down"]
    gather = lambda group: _ChipExchange(True, [pieces[n] for n in group])
    G = dict(zip(group_a, _gather_over_cores(gather(group_a).standalone("gather_ici_a"), "gather_d2d_a")))

    inv_freq = ROPE_THETA ** (-jnp.arange(ROPE_HALF, dtype=F32) / ROPE_HALF)
    ang = positions[0].astype(F32)[:, None] * inv_freq
    tables, mats = _prep_tables(jnp.cos(ang), jnp.sin(ang)), _group_matrices()
    gains = _head_gains(*[W[n] for n in ["mla_q_nope_gain", "mla_q_rope_gain", "mla_k_nope_gain", "mla_k_rope_gain",
                                         "fox_q_gain", "fox_k_gain"]])

    (x1, a1, b1), got_b = _ffn_fwd(xs, W["ffn1_norm"], G["ffn1_w_gate"], G["ffn1_w_up"], G["ffn1_w_down"],
                                   exchange=gather(group_b))
    G.update(zip(group_b, _gather_over_cores(got_b, "gather_d2d_b")))
    win_t = G["w_in"].reshape(N_DEV, W_IN_PIECE_PAD, D_MODEL)[:, :W_IN_PIECE].reshape(N_DEV * W_IN_PIECE, D_MODEL)
    wgate_t, wfox_t, wlat_t = _split_w_in(win_t)
    bg = G["b_gate"].T
    gates, fox, lat = _proj_fwd(x1, W["mix_norm"], wgate_t, wfox_t, wlat_t)
    prep_args = (fox, lat, tables, mats, *_interleave_weights(G["mla_w_qb"], G["mla_w_kvb"]), W["mla_q_lat_norm"],
                 W["mla_kv_lat_norm"], gains, W["fox_b_f"])
    fq, fk, fv, c, mq, mk, mv, fq_sq, fk_sq_max, mq_sq, mk_sq_max, fv1, mv1 = _prep_fwd(*prep_args, *_value_layout(prep_args[6]))
    c_t = c.T
    b_fox, ok_fox = _logit_bound(fq_sq, fk_sq_max)
    b_mla, ok_mla = _logit_bound(mq_sq, mk_sq_max)
    (y_fox, y_fox32, lse_fox), got_c = _attn_fwd(fq, fk, fv1, b_fox, ok_fox, c, c_t, lanes=HEAD_DIM, name="fox_fwd",
                                                 exchange=gather(group_c))
    (y_mla, y_mla32, lse_mla), got_c = _attn_fwd(mq, mk, mv1, b_mla, ok_mla, None, None, lanes=MLA_QK_LANES, name="mla_fwd",
                                                 exchange=_CoreExchange(True, got_c))
    G.update(zip(group_c, got_c))
    x2 = _mix_fwd(x1, y_mla, y_fox, gates, bg, G["w_branch_mla"], G["w_branch_fox"], G["w_o"])
    (dx3, a2, b2, loss_vec), _ = _ffn_fwd(x2, W["ffn2_norm"], G["ffn2_w_gate"], G["ffn2_w_up"], G["ffn2_w_down"],
                                          target=loss_target[0])

    def chip_sums(group, tag, from_sibling=None):
        if from_sibling is None:
            from_sibling = _grads_to_sibling([grads[n] for n in group], "grads_d2d_" + tag)
        return _ChipExchange(False, [_pair_sum(grads[n], s, "pair_sum_" + n) for n, s in zip(group, from_sibling)])

    (dx2, dg_ffn2, da2, db2, h2, n2, dyh2), _ = _ffn_bwd(dx3, x2, W["ffn2_norm"], a2, b2, G["ffn2_w_gate"], G["ffn2_w_up"],
                                                        G["ffn2_w_down"], "ffn2_bwd")
    grads = {"ffn2_w_gate": _tn_matmul(da2, n2, "ffn2_dgate"), "ffn2_w_up": _tn_matmul(db2, n2, "ffn2_dup"),
             "ffn2_w_down": _tn_matmul(h2, dyh2, "ffn2_ddown")}
    (dy_mla, dy_fox, delta_mla, delta_fox, dgates, mixed, dum, duf, dx2b, dbg), from_sibling_c = _mix_bwd(
        dx2, y_mla, y_fox, y_mla32, y_fox32, gates, bg, G["w_branch_mla"], G["w_branch_fox"], G["w_o"],
        exchange=_CoreExchange(False, [grads[n] for n in group_c]))
    grads["w_o"] = _tn_matmul(mixed, dx2b, "d_w_o")
    grads["w_branch_mla"] = _tn_matmul(dum, y_mla, "d_w_branch_mla")
    grads["w_branch_fox"] = _tn_matmul(duf, y_fox, "d_w_branch_fox")
    (dfq, dfk, dfv, dc_t), slots_c = _attn_bwd(fq, fk, fv, dy_fox, delta_fox, lse_fox, c, c_t, lanes=HEAD_DIM,
                                               name="fox_bwd", exchange=chip_sums(group_c, "c", from_sibling_c))
    slots = dict(zip(group_c, slots_c))
    (dmq, dmk, dmv), _ = _attn_bwd(mq, mk, mv, dy_mla, delta_mla, lse_mla, None, None, lanes=MLA_QK_LANES, name="mla_bwd")
    dfox, dlat, dwq, dwk, dwv, d_gql, d_gkvl, d_gq, d_gk, d_gkr, d_gfq, d_gfk, d_bf = _prep_bwd(
        *prep_args, dfq, dfk, dfv, dc_t.T, dmq, dmk, dmv)
    grads["mla_w_qb"], grads["mla_w_kvb"] = _deinterleave_grads(dwq, dwk, dwv)
    fold = lambda a, width: a.reshape(N_HEADS, width).sum(axis=0)[None]
    d_gq, d_gk = fold(d_gq, LANES), fold(d_gk, LANES)
    d_prep_small = [d_gql, d_gkvl, d_gq[:, :HEAD_DIM], d_gq[:, HEAD_DIM:HEAD_DIM + ROPE_DIM], d_gk[:, :HEAD_DIM],
                    d_gkr[:, HEAD_DIM:HEAD_DIM + ROPE_DIM], fold(d_gfq, HEAD_DIM), fold(d_gfk, HEAD_DIM), d_bf]
    dx1, dg_mix, nmix = _proj_bwd(dgates, dfox, dlat, wgate_t, wfox_t, wlat_t, x1, W["mix_norm"], dx2)
    dwin_t = _join_w_in(_tn_matmul(dgates, nmix, "d_w_in_gates"), _tn_matmul(dfox, nmix, "d_w_in_fox"),
                        _tn_matmul(dlat, nmix, "d_w_in_lat"))
    grads["w_in"] = jnp.pad(dwin_t.reshape(N_DEV, W_IN_PIECE, D_MODEL), ((0, 0), (0, W_IN_PIECE_PAD - W_IN_PIECE), (0, 0))
                            ).reshape(N_DEV * W_IN_PIECE_PAD, D_MODEL)
    grad_group_b = [n for n in group_b if n != "b_gate"]
    (dx0, dg_ffn1, da1, db1, h1, n1, dyh1), slots_b = _ffn_bwd(dx1, xs, W["ffn1_norm"], a1, b1, G["ffn1_w_gate"], G["ffn1_w_up"],
                                                              G["ffn1_w_down"], "ffn1_bwd", exchange=chip_sums(grad_group_b, "b"))
    slots.update(zip(grad_group_b, slots_b))
    grads["ffn1_w_gate"] = _tn_matmul(da1, n1, "ffn1_dgate")
    grads["ffn1_w_up"], got = _tn_matmul(db1, n1, "ffn1_dup", exchange=chip_sums(["ffn1_w_gate"], "a_gate"))
    slots["ffn1_w_gate"] = got[0]
    grads["ffn1_w_down"], got = _tn_matmul(h1, dyh1, "ffn1_ddown", exchange=chip_sums(["ffn1_w_up"], "a_up"))
    slots["ffn1_w_up"] = got[0]
    slots["ffn1_w_down"] = chip_sums(["ffn1_w_down"], "a_down").standalone("grads_ici_a_down")[0]

    small_parts = [dg_ffn1, dg_mix, dg_ffn2] + list(d_prep_small) + [dbg.reshape(1, 2 * D_MODEL), loss_vec]
    flat = jnp.concatenate([p.reshape(-1) for p in small_parts])
    n_flat = flat.shape[0]
    rows = -(-n_flat // (8 * 128)) * 8
    total = _all_reduce_small(jnp.pad(flat, (0, rows * 128 - n_flat)).reshape(rows, 128)).reshape(-1)
    offs, small_g = 0, {}
    for n in _SMALL:
        small_g[n] = total[offs:offs + W[n].shape[1]].reshape(W[n].shape)
        offs += W[n].shape[1]
    bg_full = total[offs:offs + 2 * D_MODEL].reshape(2, D_MODEL)
    offs += 2 * D_MODEL
    loss = (0.5 / D_MODEL) * jnp.sum(total[offs:offs + D_MODEL])
    _, _, _, me = _my_position()
    small_g["b_gate"] = lax.dynamic_slice_in_dim(bg_full, me * (D_MODEL // N_DEV), D_MODEL // N_DEV, axis=1)

    transposed_in_memory = ["ffn1_w_gate", "ffn1_w_up", "ffn2_w_gate", "ffn2_w_up", "w_in"]
    res = {}
    for n in _WEIGHTS:
        if n in small_g:
            res[n] = (small_g[n],) + tuple(_adamw(small_g[n], W[n], M[n], V[n], "adamw_" + n))
        elif n in transposed_in_memory:
            res[n] = tuple(o.T for o in _reduce_adamw(slots[n], W[n].T, M[n].T, V[n].T, transpose=False, name="adamw_" + n))
        else:
            res[n] = tuple(_reduce_adamw(slots[n], W[n], M[n], V[n], transpose=n in col_split, name="adamw_" + n))
    outs = [loss, dx0[None]]
    for k in range(4):
        outs += [res[n][k] if n in _SMALL else res[n][k][None] for n in _WEIGHTS]
    return tuple(outs)
```

```python
import functools

import jax
import jax.numpy as jnp
from jax import lax
from jax.experimental import pallas as pl
from jax.experimental.pallas import tpu as pltpu

F32 = jnp.float32
BF16 = jnp.bfloat16

D_MODEL = 1024
FFN_HIDDEN = 2816
N_HEADS = 8
HEAD_DIM = 64
ROPE_DIM = 32
ROPE_HALF = 16
Q_LORA = 192
KV_LORA = 128
ROPE_THETA = 10000.0
RMS_EPS = 1e-6
MLA_SCALE = (HEAD_DIM + ROPE_DIM) ** -0.5
FOX_SCALE = HEAD_DIM ** -0.5
MLA_QK_LANES = 128
ADAM_LR, ADAM_B1, ADAM_B2, ADAM_EPS, ADAM_WD, ADAM_STEP = 0.001, 0.9, 0.999, 1e-08, 0.01, 10
N_DEV = 8
N_CHIP = 4
W_IN_PIECE = 493
BF16_ROW_TILE = 16
W_IN_PIECE_PAD = 496
LAT_W = 512
LAT_Q, LAT_KV, LAT_KR, LAT_F = 0, 256, 384, 416
MASK_VALUE = -1e30
FIXED_SHIFT_MAX_BOUND = 30.0

TOK_TILE = 512
DW_TOK_TILE = 1024
PROJ_TILE = 512
PREP_TILE = 256
ATT_TILE = 512
ATT_COL_CHUNK = 256
FFN_HID_TILE = 1408
FFN_HID_SPLIT = ((0, 768), (768, 1408))

NT = (((1,), (1,)), ((), ()))
TN = (((0,), (0,)), ((), ()))
NN = (((1,), (0,)), ((), ()))
MESH = pl.DeviceIdType.MESH


def _dot(a, b, dims=NN):
    return lax.dot_general(a, b, dims, preferred_element_type=F32)


def _sds(shape, dtype):
    return jax.ShapeDtypeStruct(shape, dtype)


def _rms_fwd(x, g):
    r = lax.rsqrt(jnp.mean(x * x, axis=-1, keepdims=True) + RMS_EPS)
    return x * r * g, r


def _rms_bwd(dy, x, g, r):
    xn = x * r
    dyg = dy * g
    dx = r * (dyg - xn * jnp.mean(dyg * xn, axis=-1, keepdims=True))
    return dx, dy * xn


def _colsum(x):
    return jnp.sum(x, axis=0, keepdims=True)


def _full(shape):
    return pl.BlockSpec(shape, lambda *_: (0,) * len(shape))


def _tok(tm, n):
    return pl.BlockSpec((tm, n), lambda i, *_: (i, 0))


def _ffn_fwd(x, gain, wg_t, wu_t, wd, target=None, exchange=None):
    t = x.shape[0]
    tm = min(TOK_TILE, t)
    tf = FFN_HID_TILE
    n_t, n_f = t // tm, FFN_HIDDEN // tf
    with_loss = target is not None

    def body(*refs):
        if with_loss:
            x_ref, g_ref, wg_ref, wu_ref, wd_ref, t_ref, out_ref, a_ref, b_ref, lvec_ref, n_scr, acc = refs
        else:
            x_ref, g_ref, wg_ref, wu_ref, wd_ref, out_ref, a_ref, b_ref, n_scr, acc = refs
        i, j = pl.program_id(0), pl.program_id(1)

        @pl.when(j == 0)
        def _():
            xn, _ = _rms_fwd(x_ref[...], g_ref[...])
            n_scr[...] = xn.astype(BF16)
            acc[...] = jnp.zeros_like(acc)

        n = n_scr[...]
        a = _dot(n, wg_ref[...], NT)
        b = _dot(n, wu_ref[...], NT)
        a_ref[...] = a.astype(BF16)
        b_ref[...] = b.astype(BF16)
        h = (a * jax.nn.sigmoid(a)) * b
        acc[...] += _dot(h.astype(BF16), wd_ref[...])

        @pl.when(j == n_f - 1)
        def _():
            y = x_ref[...] + 0.5 * acc[...]
            if with_loss:
                diff = y - t_ref[...]
                out_ref[...] = diff * (1.0 / D_MODEL)
                sq = _colsum(diff * diff)

                @pl.when(i == 0)
                def _():
                    lvec_ref[...] = sq

                @pl.when(i > 0)
                def _():
                    lvec_ref[...] += sq
            else:
                out_ref[...] = y

    wspec = pl.BlockSpec((tf, D_MODEL), lambda i, j: (j, 0))
    hspec = pl.BlockSpec((tm, tf), lambda i, j: (i, j))
    in_specs = [_tok(tm, D_MODEL), _full((1, D_MODEL)), wspec, wspec, wspec]
    out_specs = [_tok(tm, D_MODEL), hspec, hspec]
    out_shape = [_sds((t, D_MODEL), F32), _sds((t, FFN_HIDDEN), BF16), _sds((t, FFN_HIDDEN), BF16)]
    args = [x, gain, wg_t, wu_t, wd]
    if with_loss:
        in_specs.append(_tok(tm, D_MODEL))
        out_specs.append(_full((1, D_MODEL)))
        out_shape.append(_sds((1, D_MODEL), F32))
        args.append(target)
    return _gridded_call(
        body, name="ffn_fwd_loss" if with_loss else "ffn_fwd", grid=(n_t, n_f), in_specs=in_specs, out_specs=out_specs,
        out_shape=out_shape, scratch_shapes=[pltpu.VMEM((tm, D_MODEL), BF16), pltpu.VMEM((tm, D_MODEL), F32)],
        args=args, exchange=exchange)


def _ffn_bwd(dy, x, gain, a, b, wg_t, wu_t, wd, name, exchange=None):
    t = x.shape[0]
    tm = min(TOK_TILE, t)
    tf = FFN_HID_TILE
    n_t, n_f = t // tm, FFN_HIDDEN // tf

    def body(dy_ref, x_ref, g_ref, a_ref, b_ref, wg_ref, wu_ref, wd_ref,
             dx_ref, dg_ref, da_ref, db_ref, h_ref, n_ref, dyh_ref, acc):
        i, j = pl.program_id(0), pl.program_id(1)

        @pl.when(j == 0)
        def _():
            xn, _ = _rms_fwd(x_ref[...], g_ref[...])
            n_ref[...] = xn.astype(BF16)
            dyh_ref[...] = (0.5 * dy_ref[...]).astype(BF16)
            acc[...] = jnp.zeros_like(acc)

        dyh = dyh_ref[...]
        for lo, hi in FFN_HID_SPLIT:
            dh = _dot(dyh, wd_ref[lo:hi, :], NT)
            av = a_ref[:, lo:hi].astype(F32)
            bv = b_ref[:, lo:hi].astype(F32)
            s = jax.nn.sigmoid(av)
            silu = av * s
            da = (dh * bv * (s * (1.0 + av * (1.0 - s)))).astype(BF16)
            db = (dh * silu).astype(BF16)
            da_ref[:, lo:hi] = da
            db_ref[:, lo:hi] = db
            h_ref[:, lo:hi] = (silu * bv).astype(BF16)
            acc[...] += _dot(da, wg_ref[lo:hi, :]) + _dot(db, wu_ref[lo:hi, :])

        @pl.when(j == n_f - 1)
        def _():
            xv, g = x_ref[...], g_ref[...]
            r = lax.rsqrt(jnp.mean(xv * xv, axis=-1, keepdims=True) + RMS_EPS)
            dx, dg_rows = _rms_bwd(acc[...], xv, g, r)
            dx_ref[...] = dy_ref[...] + dx
            dg = _colsum(dg_rows)

            @pl.when(i == 0)
            def _():
                dg_ref[...] = dg

            @pl.when(i > 0)
            def _():
                dg_ref[...] += dg

    wspec = pl.BlockSpec((tf, D_MODEL), lambda i, j: (j, 0))
    hspec = pl.BlockSpec((tm, tf), lambda i, j: (i, j))
    tok = _tok(tm, D_MODEL)
    hid = _sds((t, FFN_HIDDEN), BF16)
    return _gridded_call(
        body, name=name, grid=(n_t, n_f),
        in_specs=[tok, tok, _full((1, D_MODEL)), hspec, hspec, wspec, wspec, wspec],
        out_specs=[tok, _full((1, D_MODEL)), hspec, hspec, hspec, tok, tok],
        out_shape=[_sds((t, D_MODEL), F32), _sds((1, D_MODEL), F32), hid, hid, hid,
                   _sds((t, D_MODEL), BF16), _sds((t, D_MODEL), BF16)],
        scratch_shapes=[pltpu.VMEM((tm, D_MODEL), F32)], args=[dy, x, gain, a, b, wg_t, wu_t, wd], exchange=exchange)


def _tn_matmul(a, b, name, exchange=None):
    t, m = a.shape
    n = b.shape[1]
    tk = min(DW_TOK_TILE, t)
    n_k = t // tk

    def body(a_ref, b_ref, o_ref, acc):
        k = pl.program_id(0)
        p = _dot(a_ref[...], b_ref[...], TN)

        @pl.when(k == 0)
        def _():
            acc[...] = p

        @pl.when(k > 0)
        def _():
            acc[...] += p

        @pl.when(k == n_k - 1)
        def _():
            o_ref[...] = acc[...].astype(BF16)

    (out,), got = _gridded_call(
        body, name=name, grid=(n_k,), in_specs=[_tok(tk, m), _tok(tk, n)], out_specs=[_full((m, n))],
        out_shape=[_sds((m, n), BF16)], scratch_shapes=[pltpu.VMEM((m, n), F32)], args=[a, b], exchange=exchange)
    return out if exchange is None else (out, got)


def _proj_fwd(x, gain, wgate_t, wfox_t, wlat_t):
    t = x.shape[0]
    tm = min(PROJ_TILE, t)

    def body(x_ref, g_ref, wg_ref, wf_ref, wl_ref, og_ref, of_ref, ol_ref):
        xn, _ = _rms_fwd(x_ref[...], g_ref[...])
        n = xn.astype(BF16)
        og_ref[...] = _dot(n, wg_ref[...], NT)
        of_ref[...] = _dot(n, wf_ref[...], NT)
        ol_ref[...] = _dot(n, wl_ref[...], NT)

    return pl.pallas_call(
        body, name="proj_fwd", grid=(t // tm,),
        in_specs=[_tok(tm, D_MODEL), _full((1, D_MODEL)), _full(wgate_t.shape), _full(wfox_t.shape), _full(wlat_t.shape)],
        out_specs=[_tok(tm, 2 * D_MODEL), _tok(tm, 3 * 512), _tok(tm, LAT_W)],
        out_shape=[_sds((t, 2 * D_MODEL), F32), _sds((t, 3 * 512), F32), _sds((t, LAT_W), F32)],
        compiler_params=pltpu.CompilerParams(dimension_semantics=("arbitrary",)),
    )(x, gain, wgate_t, wfox_t, wlat_t)


def _proj_bwd(dgates, dfox, dlat, wgate_t, wfox_t, wlat_t, x, gain, dres):
    t = x.shape[0]
    tm = min(PROJ_TILE, t)

    def body(dg_ref, df_ref, dl_ref, wg_ref, wf_ref, wl_ref, x_ref, g_ref, dres_ref, dx_ref, dgain_ref, n_ref):
        i = pl.program_id(0)
        dn = _dot(dg_ref[...], wg_ref[...]) + _dot(df_ref[...], wf_ref[...]) + _dot(dl_ref[...], wl_ref[...])
        xv, g = x_ref[...], g_ref[...]
        xn, r = _rms_fwd(xv, g)
        n_ref[...] = xn.astype(BF16)
        dx, dg_rows = _rms_bwd(dn, xv, g, r)
        dx_ref[...] = dres_ref[...] + dx
        dgn = _colsum(dg_rows)

        @pl.when(i == 0)
        def _():
            dgain_ref[...] = dgn

        @pl.when(i > 0)
        def _():
            dgain_ref[...] += dgn

    tok = _tok(tm, D_MODEL)
    return pl.pallas_call(
        body, name="proj_bwd", grid=(t // tm,),
        in_specs=[_tok(tm, 2 * D_MODEL), _tok(tm, 3 * 512), _tok(tm, LAT_W), _full(wgate_t.shape), _full(wfox_t.shape),
                  _full(wlat_t.shape), tok, _full((1, D_MODEL)), tok],
        out_specs=[tok, _full((1, D_MODEL)), tok],
        out_shape=[_sds((t, D_MODEL), F32), _sds((1, D_MODEL), F32), _sds((t, D_MODEL), BF16)],
        compiler_params=pltpu.CompilerParams(dimension_semantics=("arbitrary",)),
    )(dgates, dfox, dlat, wgate_t, wfox_t, wlat_t, x, gain, dres)


def _tri(n, lower):
    r = lax.broadcasted_iota(jnp.int32, (n, n), 0)
    c = lax.broadcasted_iota(jnp.int32, (n, n), 1)
    return ((c <= r) if lower else (c >= r)).astype(F32)


def _log_sigmoid(z):
    return jnp.minimum(z, 0.0) - jnp.log1p(jnp.exp(-jnp.abs(z)))


N_GROUPS = 16
LANES = MLA_QK_LANES


def _split_dot(a, g):
    hi = a.astype(BF16)
    lo = (a - hi.astype(F32)).astype(BF16)
    return _dot(hi, g) + _dot(lo, g)


def _rope_fwd(y, c, s1, s2):
    return y * c + pltpu.roll(y, LANES - ROPE_HALF, 1) * s1 + pltpu.roll(y, ROPE_HALF, 1) * s2


def _rope_bwd(do, c, s1, s2):
    return do * c + pltpu.roll(do * s1, ROPE_HALF, 1) + pltpu.roll(do * s2, LANES - ROPE_HALF, 1)


def _prep_tables(cos, sin):
    t = cos.shape[0]
    z = lambda n: jnp.zeros((t, n), F32)
    c = jnp.concatenate([jnp.ones((t, HEAD_DIM), F32), cos, cos, z(LANES - HEAD_DIM - ROPE_DIM)], axis=1)
    s1 = jnp.concatenate([z(HEAD_DIM), -sin, z(LANES - HEAD_DIM - ROPE_HALF)], axis=1)
    s2 = jnp.concatenate([z(HEAD_DIM + ROPE_HALF), sin, z(LANES - HEAD_DIM - ROPE_DIM)], axis=1)
    lane = jnp.arange(LANES)
    rope_mask = ((lane >= HEAD_DIM) & (lane < HEAD_DIM + ROPE_DIM)).astype(F32)[None, :]
    return c, s1, s2, rope_mask


def _group_matrices():
    lane = jnp.arange(N_HEADS * LANES)
    head, d = (lane // LANES)[:, None], (lane % LANES)[:, None]
    col = jnp.arange(N_GROUPS)[None, :]
    g_mla = ((col == head) & (d < HEAD_DIM)) | ((col == N_HEADS + head) & (d >= HEAD_DIM) & (d < HEAD_DIM + ROPE_DIM))
    g_fox = col == (jnp.arange(N_HEADS * HEAD_DIM) // HEAD_DIM)[:, None]
    inv_mla = jnp.concatenate([jnp.full((1, N_HEADS), 1.0 / HEAD_DIM, F32), jnp.full((1, N_HEADS), 1.0 / ROPE_DIM, F32)], axis=1)
    inv_fox = jnp.full((1, N_GROUPS), 1.0 / HEAD_DIM, F32)
    return g_mla.astype(BF16), g_mla.T.astype(BF16), inv_mla, g_fox.astype(BF16), g_fox.T.astype(BF16), inv_fox


def _interleave_weights(wqb_t, wkvb_t):
    wq = jnp.pad(wqb_t.reshape(N_HEADS, HEAD_DIM + ROPE_DIM, Q_LORA), ((0, 0), (0, LANES - HEAD_DIM - ROPE_DIM), (0, 0)))
    kv = wkvb_t.reshape(N_HEADS, 2, HEAD_DIM, KV_LORA)
    wk = jnp.pad(kv[:, 0], ((0, 0), (0, LANES - HEAD_DIM), (0, 0)))
    return wq.reshape(N_HEADS * LANES, Q_LORA), wk.reshape(N_HEADS * LANES, KV_LORA), kv[:, 1].reshape(N_HEADS * HEAD_DIM, KV_LORA)


def _deinterleave_grads(dwq, dwk, dwv):
    dq = dwq.reshape(N_HEADS, LANES, Q_LORA)[:, :HEAD_DIM + ROPE_DIM].reshape(N_HEADS * (HEAD_DIM + ROPE_DIM), Q_LORA)
    dk = dwk.reshape(N_HEADS, LANES, KV_LORA)[:, :HEAD_DIM]
    dkv = jnp.stack([dk, dwv.reshape(N_HEADS, HEAD_DIM, KV_LORA)], axis=1)
    return dq, dkv.reshape(N_HEADS * 2 * HEAD_DIM, KV_LORA)


def _head_gains(g_qn, g_qr, g_kn, g_kr, g_fq, g_fk):
    z = lambda n: jnp.zeros((1, n), F32)
    gq = jnp.concatenate([g_qn, g_qr, z(LANES - HEAD_DIM - ROPE_DIM)], axis=1)
    gk = jnp.concatenate([g_kn, z(LANES - HEAD_DIM)], axis=1)
    gkr = jnp.concatenate([z(HEAD_DIM), g_kr, z(LANES - HEAD_DIM - ROPE_DIM)], axis=1)
    return jnp.tile(gq, (1, N_HEADS)), jnp.tile(gk, (1, N_HEADS)), gkr, jnp.tile(g_fq, (1, N_HEADS)), jnp.tile(g_fk, (1, N_HEADS))


def _group_rms(x, g, g_t, inv):
    r = lax.rsqrt(_split_dot(x * x, g) * inv + RMS_EPS)
    return _split_dot(r, g_t)


def _value_layout(wv):
    lane = jnp.arange(N_HEADS * LANES)
    src = (lane // LANES) * HEAD_DIM + lane % LANES
    place = (jnp.arange(N_HEADS * HEAD_DIM)[:, None] == src[None, :]) & (lane % LANES < HEAD_DIM)[None, :]
    ones = (lane % LANES == HEAD_DIM).astype(F32)[None, :]
    wv_il = jnp.pad(wv.reshape(N_HEADS, HEAD_DIM, KV_LORA), ((0, 0), (0, LANES - HEAD_DIM), (0, 0)))
    return place.astype(BF16), ones, wv_il.reshape(N_HEADS * LANES, KV_LORA)


def _prep_fwd(fox, lat, tables, mats, wq, wk, wv, g_qlat, g_kvlat, gains, b_f, v_place, v_ones, wv_il):
    t = fox.shape[0]
    tm = min(PREP_TILE, t)
    c_tab, s1_tab, s2_tab, rope_mask = tables

    def body(fox_ref, lat_ref, c_ref, s1_ref, s2_ref, rm_ref, gm_ref, gmt_ref, im_ref, gf_ref, gft_ref, if_ref,
             wq_ref, wk_ref, wv_ref, gql_ref, gkvl_ref, gq_ref, gk_ref, gkr_ref, gfq_ref, gfk_ref, bf_ref,
             vp_ref, vo_ref, wvil_ref,
             fq_ref, fk_ref, fv_ref, cc_ref, mq_ref, mk_ref, mv_ref, fqn_ref, fkmax_ref, mqn_ref, mkmax_ref,
             fv1_ref, mv1_ref, carry):
        i = pl.program_id(0)
        ct, s1, s2 = c_ref[...], s1_ref[...], s2_ref[...]

        @pl.when(i == 0)
        def _():
            carry[...] = jnp.zeros_like(carry)
            fkmax_ref[...] = jnp.zeros_like(fkmax_ref)
            mkmax_ref[...] = jnp.zeros_like(mkmax_ref)

        def square_sums(ref, g):
            v = ref[...].astype(F32)
            return _split_dot(v * v, g)

        xq = fox_ref[:, 0:512]
        fq_ref[...] = (xq * _group_rms(xq, gf_ref[...], gft_ref[...], if_ref[...]) * (gfq_ref[...] * FOX_SCALE)).astype(BF16)
        xk = fox_ref[:, 512:1024]
        fk_ref[...] = (xk * _group_rms(xk, gf_ref[...], gft_ref[...], if_ref[...]) * gfk_ref[...]).astype(BF16)
        fv_b = fox_ref[:, 1024:1536].astype(BF16)
        fv_ref[...] = fv_b
        fv1_ref[...] = (_dot(fv_b, vp_ref[...]) + vo_ref[...]).astype(BF16)
        fqn_ref[...] = square_sums(fq_ref, gf_ref[...])
        fkmax_ref[...] = jnp.maximum(fkmax_ref[...], jnp.max(square_sums(fk_ref, gf_ref[...]), axis=0, keepdims=True))


        logf = _log_sigmoid(lat_ref[:, LAT_F:LAT_F + N_HEADS] + bf_ref[...])
        cc_ref[...] = jnp.dot(_tri(tm, True), logf, precision=lax.Precision.HIGHEST, preferred_element_type=F32) + carry[...]
        carry[...] += _colsum(logf)

        qlat_n, _ = _rms_fwd(lat_ref[:, LAT_Q:LAT_Q + Q_LORA], gql_ref[...])
        p = _dot(qlat_n.astype(BF16), wq_ref[...], NT)
        y = p * _group_rms(p, gm_ref[...], gmt_ref[...], im_ref[...]) * (gq_ref[...] * MLA_SCALE)
        for h in range(N_HEADS):
            w = slice(h * LANES, (h + 1) * LANES)
            mq_ref[:, w] = _rope_fwd(y[:, w], ct, s1, s2).astype(BF16)

        kv_n, _ = _rms_fwd(lat_ref[:, LAT_KV:LAT_KV + KV_LORA], gkvl_ref[...])
        kv_b = kv_n.astype(BF16)
        pk = _dot(kv_b, wk_ref[...], NT)
        kn = pk * _group_rms(pk, gm_ref[...], gmt_ref[...], im_ref[...]) * gk_ref[...]
        rm = rm_ref[...]
        kr = pltpu.roll(lat_ref[:, LAT_KR:LAT_KR + LANES], HEAD_DIM, 1) * rm
        rr = lax.rsqrt(jnp.sum(kr * kr, axis=1, keepdims=True) * (1.0 / ROPE_DIM) + RMS_EPS)
        okr = _rope_fwd(kr * rr * gkr_ref[...], ct * rm, s1, s2)
        for h in range(N_HEADS):
            w = slice(h * LANES, (h + 1) * LANES)
            mk_ref[:, w] = (kn[:, w] + okr).astype(BF16)
        mv_ref[...] = _dot(kv_b, wv_ref[...], NT).astype(BF16)
        mv1_ref[...] = (_dot(kv_b, wvil_ref[...], NT).astype(BF16).astype(F32) + vo_ref[...]).astype(BF16)
        mqn_ref[...] = square_sums(mq_ref, gm_ref[...])
        mkmax_ref[...] = jnp.maximum(mkmax_ref[...], jnp.max(square_sums(mk_ref, gm_ref[...]), axis=0, keepdims=True))

    consts = [rope_mask, *mats, wq, wk, wv, g_qlat, g_kvlat, *gains, b_f, v_place, v_ones, wv_il]
    t512, tl, tg, g1 = _tok(tm, 512), _tok(tm, LANES), _tok(tm, N_GROUPS), _full((1, N_GROUPS))
    t1024 = _tok(tm, N_HEADS * LANES)
    return pl.pallas_call(
        body, name="prep_fwd", grid=(t // tm,),
        in_specs=[_tok(tm, 1536), _tok(tm, LAT_W), tl, tl, tl] + [_full(a.shape) for a in consts],
        out_specs=[t512, t512, t512, _tok(tm, N_HEADS), t1024, t1024, t512, tg, g1, tg, g1, t1024, t1024],
        out_shape=[_sds((t, 512), BF16), _sds((t, 512), BF16), _sds((t, 512), BF16), _sds((t, N_HEADS), F32),
                   _sds((t, N_HEADS * LANES), BF16), _sds((t, N_HEADS * LANES), BF16), _sds((t, 512), BF16),
                   _sds((t, N_GROUPS), F32), _sds((1, N_GROUPS), F32), _sds((t, N_GROUPS), F32), _sds((1, N_GROUPS), F32)]
        + [_sds((t, N_HEADS * LANES), BF16)] * 2,
        scratch_shapes=[pltpu.VMEM((1, N_HEADS), F32)],
        compiler_params=pltpu.CompilerParams(dimension_semantics=("arbitrary",)),
    )(fox, lat, c_tab, s1_tab, s2_tab, *consts)


def _prep_bwd(fox, lat, tables, mats, wq, wk, wv, g_qlat, g_kvlat, gains, b_f, dfq, dfk, dfv, dc, dmq, dmk, dmv):
    t = fox.shape[0]
    tm = min(PREP_TILE, t)
    n_t = t // tm
    c_tab, s1_tab, s2_tab, rope_mask = tables

    def body(fox_ref, lat_ref, c_ref, s1_ref, s2_ref, rm_ref, gm_ref, gmt_ref, im_ref, gf_ref, gft_ref, if_ref,
             wq_ref, wk_ref, wv_ref, gql_ref, gkvl_ref, gq_ref, gk_ref, gkr_ref, gfq_ref, gfk_ref, bf_ref,
             dfq_ref, dfk_ref, dfv_ref, dc_ref, dmq_ref, dmk_ref, dmv_ref,
             dfox_ref, dlat_ref, dwq_ref, dwk_ref, dwv_ref, o_gql, o_gkvl, o_gq, o_gk, o_gkr, o_gfq, o_gfk, o_bf,
             carry, lscr, wq_acc, wk_acc, wv_acc, dyscr):
        i = pl.program_id(0)
        ct, s1, s2, rm = c_ref[...], s1_ref[...], s2_ref[...], rm_ref[...]

        @pl.when(i == 0)
        def _():
            for ref in [carry, wq_acc, wk_acc, wv_acc, o_gql, o_gkvl, o_gq, o_gk, o_gkr, o_gfq, o_gfk, o_bf]:
                ref[...] = jnp.zeros_like(ref)

        def group_rms_bwd(dy, x, r, gain, g, g_t, inv):
            xn = x * r
            dyg = dy * gain
            mu = _split_dot(_split_dot(dyg * xn, g) * inv, g_t)
            return r * (dyg - xn * mu), dy * xn

        gf, gft, invf = gf_ref[...], gft_ref[...], if_ref[...]
        xq = fox_ref[:, 0:512]
        dxq, rows = group_rms_bwd(dfq_ref[...] * FOX_SCALE, xq, _group_rms(xq, gf, gft, invf), gfq_ref[...], gf, gft, invf)
        dfox_ref[:, 0:512] = dxq.astype(BF16)
        o_gfq[...] += _colsum(rows)
        xk = fox_ref[:, 512:1024]
        dxk, rows = group_rms_bwd(dfk_ref[...].T, xk, _group_rms(xk, gf, gft, invf), gfk_ref[...], gf, gft, invf)
        dfox_ref[:, 512:1024] = dxk.astype(BF16)
        o_gfk[...] += _colsum(rows)
        dfox_ref[:, 1024:1536] = dfv_ref[...].T.astype(BF16)

        lscr[...] = jnp.zeros_like(lscr)

        xql = lat_ref[:, LAT_Q:LAT_Q + Q_LORA]
        qlat_n, r_ql = _rms_fwd(xql, gql_ref[...])
        qlat_b = qlat_n.astype(BF16)
        gm, gmt, invm = gm_ref[...], gmt_ref[...], im_ref[...]
        p = _dot(qlat_b, wq_ref[...], NT)
        for h in range(N_HEADS):
            w = slice(h * LANES, (h + 1) * LANES)
            dyscr[:, w] = _rope_bwd(dmq_ref[:, w], ct, s1, s2) * MLA_SCALE
        dp, rows = group_rms_bwd(dyscr[...], p, _group_rms(p, gm, gmt, invm), gq_ref[...], gm, gmt, invm)
        o_gq[...] += _colsum(rows)
        dp_b = dp.astype(BF16)
        wq_acc[...] += _dot(dp_b, qlat_b, TN)
        dxql, rows = _rms_bwd(_dot(dp_b, wq_ref[...]), xql, gql_ref[...], r_ql)
        lscr[:, LAT_Q:LAT_Q + Q_LORA] = dxql
        o_gql[...] += _colsum(rows)

        xkv = lat_ref[:, LAT_KV:LAT_KV + KV_LORA]
        kv_n, r_kv = _rms_fwd(xkv, gkvl_ref[...])
        kv_b = kv_n.astype(BF16)
        pk = _dot(kv_b, wk_ref[...], NT)
        dmk = dmk_ref[...].T
        dpk, rows = group_rms_bwd(dmk, pk, _group_rms(pk, gm, gmt, invm), gk_ref[...], gm, gmt, invm)
        o_gk[...] += _colsum(rows)
        dpk_b = dpk.astype(BF16)
        dv_b = dmv_ref[...].T.astype(BF16)
        wk_acc[...] += _dot(dpk_b, kv_b, TN)
        wv_acc[...] += _dot(dv_b, kv_b, TN)
        dxkv, rows = _rms_bwd(_dot(dpk_b, wk_ref[...]) + _dot(dv_b, wv_ref[...]), xkv, gkvl_ref[...], r_kv)
        lscr[:, LAT_KV:LAT_KV + KV_LORA] = dxkv
        o_gkvl[...] += _colsum(rows)

        dokr = dmk[:, 0:LANES]
        for h in range(1, N_HEADS):
            dokr = dokr + dmk[:, h * LANES:(h + 1) * LANES]
        dykr = _rope_bwd(dokr * rm, ct * rm, s1, s2)
        kr = pltpu.roll(lat_ref[:, LAT_KR:LAT_KR + LANES], HEAD_DIM, 1) * rm
        rr = lax.rsqrt(jnp.sum(kr * kr, axis=1, keepdims=True) * (1.0 / ROPE_DIM) + RMS_EPS)
        krn = kr * rr
        dyg = dykr * gkr_ref[...]
        dkr = rr * (dyg - krn * (jnp.sum(dyg * krn, axis=1, keepdims=True) * (1.0 / ROPE_DIM)))
        o_gkr[...] += _colsum(dykr * krn)
        lscr[:, LAT_KR:LAT_KR + LANES] = pltpu.roll(dkr, LANES - HEAD_DIM, 1)

        dcv = dc_ref[...]
        dlogf = jnp.dot(_tri(tm, False), dcv, precision=lax.Precision.HIGHEST, preferred_element_type=F32) + carry[...]
        carry[...] += _colsum(dcv)
        dz = dlogf * jax.nn.sigmoid(-(lat_ref[:, LAT_F:LAT_F + N_HEADS] + bf_ref[...]))
        lscr[:, LAT_F:LAT_F + N_HEADS] = dz
        o_bf[...] += _colsum(dz)

        dlat_ref[...] = lscr[...].astype(BF16)

        @pl.when(i == n_t - 1)
        def _():
            dwq_ref[...] = wq_acc[...].astype(BF16)
            dwk_ref[...] = wk_acc[...].astype(BF16)
            dwv_ref[...] = wv_acc[...].astype(BF16)

    consts = [rope_mask, *mats, wq, wk, wv, g_qlat, g_kvlat, *gains, b_f]

    def rtok(n):
        return pl.BlockSpec((tm, n), lambda i: (n_t - 1 - i, 0))

    def rtok_t(n):
        return pl.BlockSpec((n, tm), lambda i: (0, n_t - 1 - i))

    sums =[(1, Q_LORA), (1, KV_LORA), (1, N_HEADS * LANES), (1, N_HEADS * LANES), (1, LANES), (1, 512), (1, 512), (1, N_HEADS)]
    return pl.pallas_call(
        body, name="prep_bwd", grid=(n_t,),
        in_specs=[rtok(1536), rtok(LAT_W), rtok(LANES), rtok(LANES), rtok(LANES)] + [_full(a.shape) for a in consts]
        + [rtok(512), rtok_t(512), rtok_t(512), rtok(N_HEADS), rtok(N_HEADS * LANES), rtok_t(N_HEADS * LANES), rtok_t(512)],
        out_specs=[rtok(1536), rtok(LAT_W), _full(wq.shape), _full(wk.shape), _full(wv.shape)] + [_full(s) for s in sums],
        out_shape=[_sds((t, 1536), BF16), _sds((t, LAT_W), BF16), _sds(wq.shape, BF16), _sds(wk.shape, BF16), _sds(wv.shape, BF16)]
        + [_sds(s, F32) for s in sums],
        scratch_shapes=[pltpu.VMEM((1, N_HEADS), F32), pltpu.VMEM((tm, LAT_W), F32), pltpu.VMEM(wq.shape, F32),
                        pltpu.VMEM(wk.shape, F32), pltpu.VMEM(wv.shape, F32), pltpu.VMEM((tm, N_HEADS * LANES), F32)],
        compiler_params=pltpu.CompilerParams(dimension_semantics=("arbitrary",)),
    )(fox, lat, c_tab, s1_tab, s2_tab, *consts, dfq, dfk, dfv, dc, dmq, dmk, dmv)


def _causal_pairs(n, query_major):
    pairs = [(i, j) for i in range(n) for j in range(i + 1)] if query_major else [(i, j) for j in range(n) for i in range(j, n)]
    return jnp.asarray([p[0] for p in pairs], jnp.int32), jnp.asarray([p[1] for p in pairs], jnp.int32)


def _logit_bound(q_sq, k_sq_max):
    qn = q_sq[:, :N_HEADS] + q_sq[:, N_HEADS:]
    kmax = k_sq_max[:, :N_HEADS] + k_sq_max[:, N_HEADS:]
    bound = jnp.sqrt(qn * kmax) * (1.0 + 2.0 ** -10) + 2.0 ** -10
    flag = (jnp.max(bound) <= FIXED_SHIFT_MAX_BOUND).astype(F32).reshape(1, 1)
    return bound, flag


def _attn_fwd(q, k, v, bound, fixed_ok, c, c_t, *, lanes, name, exchange=None):
    t = q.shape[0]
    tq = min(ATT_TILE, t)
    n_q = t // tq
    hd = HEAD_DIM
    ch = min(ATT_COL_CHUNK, tq)
    decay = c is not None

    def body(qi_ref, kj_ref, *refs):
        if decay:
            q_ref, k_ref, v_ref, b_ref, ok_ref, c_ref, ct_ref, o_ref, o32_ref, lse_ref, m_scr, acc = refs
        else:
            q_ref, k_ref, v_ref, b_ref, ok_ref, o_ref, o32_ref, lse_ref, m_scr, acc = refs
        i, j = qi_ref[pl.program_id(0)], kj_ref[pl.program_id(0)]
        fixed = ok_ref[0, 0] > 0.5

        @pl.when(j == 0)
        def _():
            m_scr[...] = jnp.full_like(m_scr, MASK_VALUE)
            acc[...] = jnp.zeros_like(acc)

        def fixed_step(diagonal):
            for h in range(N_HEADS):
                wl = slice(h * lanes, (h + 1) * lanes)
                wv = slice(h * LANES, (h + 1) * LANES)
                qh = q_ref[:, wl]
                row = (c_ref[:, h:h + 1] - b_ref[:, h:h + 1]) if decay else -b_ref[:, h:h + 1]
                o_hi = jnp.zeros((tq, LANES), F32)
                o_lo = jnp.zeros((tq, LANES), F32)
                for cc in range(tq // ch):
                    cols = slice(cc * ch, (cc + 1) * ch)
                    s = _dot(qh, k_ref[cols, wl], NT)
                    s = s + ((row - ct_ref[h:h + 1, cols]) if decay else row)
                    if diagonal:
                        keep = (lax.broadcasted_iota(jnp.int32, (tq, ch), 0)
                                >= lax.broadcasted_iota(jnp.int32, (tq, ch), 1) + cc * ch)
                        s = jnp.where(keep, s, MASK_VALUE)
                    p = jnp.exp(s)
                    p_b = p.astype(BF16)
                    o_hi = o_hi + _dot(p_b, v_ref[cols, wv])
                    if decay:
                        o_lo = o_lo + _dot((p - p_b.astype(F32)).astype(BF16), v_ref[cols, wv])
                acc[0, :, wv] += o_hi
                if decay:
                    acc[1, :, wv] += o_lo

        def step(diagonal):
            if diagonal:
                keep = lax.broadcasted_iota(jnp.int32, (tq, tq), 0) >= lax.broadcasted_iota(jnp.int32, (tq, tq), 1)
            for h in range(N_HEADS):
                s = _dot(q_ref[:, h * lanes:(h + 1) * lanes], k_ref[:, h * lanes:(h + 1) * lanes], NT)
                if decay:
                    s = s + (c_ref[:, h:h + 1] - ct_ref[h:h + 1, :])
                if diagonal:
                    s = jnp.where(keep, s, MASK_VALUE)
                m_prev = m_scr[h]
                m_new = jnp.maximum(m_prev, jnp.max(s, axis=1, keepdims=True))
                alpha = jnp.exp(m_prev - m_new)
                p = jnp.exp(s - m_new)
                wv = slice(h * LANES, (h + 1) * LANES)
                p_b = p.astype(BF16)
                acc[0, :, wv] = alpha * acc[0, :, wv] + _dot(p_b, v_ref[:, wv])
                if decay:
                    acc[1, :, wv] = alpha * acc[1, :, wv] + _dot((p - p_b.astype(F32)).astype(BF16), v_ref[:, wv])
                m_scr[h] = m_new

        for diagonal, here in ((False, j < i), (True, j == i)):
            @pl.when(here & fixed)
            def _():
                fixed_step(diagonal)

            @pl.when(here & jnp.logical_not(fixed))
            def _():
                step(diagonal)

        @pl.when(j == i)
        def _():
            for h in range(N_HEADS):
                w = slice(h * hd, (h + 1) * hd)
                val, one = slice(h * LANES, h * LANES + hd), slice(h * LANES + hd, h * LANES + hd + 1)
                l = (acc[0, :, one] + acc[1, :, one]) if decay else acc[0, :, one]
                o_ref[:, w] = (acc[0, :, val] / l).astype(BF16)
                o32_ref[:, w] = ((acc[0, :, val] + acc[1, :, val]) if decay else acc[0, :, val]) / l
                lse_ref[:, h:h + 1] = jnp.where(fixed, b_ref[:, h:h + 1], m_scr[h]) + jnp.log(l)

    qspec = lambda n: pl.BlockSpec((tq, n), lambda s, qi, kj: (qi[s], 0))
    kspec = lambda n: pl.BlockSpec((tq, n), lambda s, qi, kj: (kj[s], 0))
    in_specs = [qspec(N_HEADS * lanes), kspec(N_HEADS * lanes), kspec(N_HEADS * LANES), qspec(N_HEADS),
                pl.BlockSpec(memory_space=pltpu.SMEM)]
    args = [q, k, v, bound, fixed_ok]
    if decay:
        in_specs += [qspec(N_HEADS), pl.BlockSpec((N_HEADS, tq), lambda s, qi, kj: (0, kj[s]))]
        args += [c, c_t]
    pairs = _causal_pairs(n_q, query_major=True)
    return _gridded_call(
        body, name=name, grid=(pairs[0].shape[0],), in_specs=in_specs, out_specs=[qspec(512), qspec(512), qspec(N_HEADS)],
        out_shape=[_sds((t, 512), BF16), _sds((t, 512), F32), _sds((t, N_HEADS), F32)],
        scratch_shapes=[pltpu.VMEM((N_HEADS, tq, 1), F32), pltpu.VMEM((2, tq, N_HEADS * LANES), F32)],
        args=args, exchange=exchange, prefetch=pairs)


def _attn_bwd(q, k, v, do, delta, lse, c, c_t, *, lanes, name, exchange=None):
    t = q.shape[0]
    tq = min(ATT_TILE, t)
    n_q = t // tq
    hd = HEAD_DIM
    decay = c is not None

    pairs = _causal_pairs(n_q, query_major=False)
    n_steps = pairs[0].shape[0]

    def body(qi_ref, kj_ref, *refs):
        if decay:
            q_ref, k_ref, v_ref, do_ref, delta_ref, lse_ref, c_ref, ct_ref, dq_hbm, dk_ref, dv_ref, dct_ref, dq_ref = refs
        else:
            q_ref, k_ref, v_ref, do_ref, delta_ref, lse_ref, dq_hbm, dk_ref, dv_ref, dq_ref = refs
        step_id = pl.program_id(0)
        i, j = qi_ref[step_id], kj_ref[step_id]

        @pl.when(step_id == 0)
        def _():
            dq_ref[...] = jnp.zeros_like(dq_ref)

        @pl.when(i == j)
        def _():
            dk_ref[...] = jnp.zeros_like(dk_ref)
            dv_ref[...] = jnp.zeros_like(dv_ref)
            if decay:
                dct_ref[...] = jnp.zeros_like(dct_ref)

        def step(diagonal):
            if diagonal:
                keep = lax.broadcasted_iota(jnp.int32, (tq, tq), 0) >= lax.broadcasted_iota(jnp.int32, (tq, tq), 1)
            rows = pl.ds(pl.multiple_of(i * tq, tq), tq)
            for h in range(N_HEADS):
                wl = slice(h * lanes, (h + 1) * lanes)
                w = slice(h * hd, (h + 1) * hd)
                qh, kh = q_ref[:, wl], k_ref[:, wl]
                s = _dot(qh, kh, NT)
                if decay:
                    s = s + (c_ref[:, h:h + 1] - ct_ref[h:h + 1, :])
                if diagonal:
                    s = jnp.where(keep, s, MASK_VALUE)
                p = jnp.exp(s - lse_ref[:, h:h + 1])
                doh = do_ref[:, w]
                dv_ref[w, :] += _dot(doh, p.astype(BF16), TN)
                dp = _dot(doh, v_ref[:, w], NT)
                ds = p * (dp - delta_ref[:, h:h + 1])
                if decay:
                    dct_ref[h:h + 1, :] -= _colsum(ds)
                ds_b = ds.astype(BF16)
                dk_ref[wl, :] += _dot(qh, ds_b, TN)
                dq_ref[rows, wl] += _dot(ds_b, kh)

        @pl.when(i > j)
        def _():
            step(False)

        @pl.when(i == j)
        def _():
            step(True)

        @pl.when(step_id == n_steps - 1)
        def _():
            pltpu.sync_copy(dq_ref, dq_hbm)

    qspec = lambda n: pl.BlockSpec((tq, n), lambda s, qi, kj: (qi[s], 0))
    kspec = lambda n: pl.BlockSpec((tq, n), lambda s, qi, kj: (kj[s], 0))
    in_specs = [qspec(N_HEADS * lanes), kspec(N_HEADS * lanes), kspec(512), qspec(512), qspec(N_HEADS), qspec(N_HEADS)]
    kspec_t = lambda n: pl.BlockSpec((n, tq), lambda s, qi, kj: (0, kj[s]))
    out_specs = [pl.BlockSpec(memory_space=pl.ANY), kspec_t(N_HEADS * lanes), kspec_t(512)]
    out_shape = [_sds((t, N_HEADS * lanes), F32), _sds((N_HEADS * lanes, t), F32), _sds((512, t), F32)]
    args = [q, k, v, do, delta, lse]
    if decay:
        ctspec = pl.BlockSpec((N_HEADS, tq), lambda s, qi, kj: (0, kj[s]))
        in_specs += [qspec(N_HEADS), ctspec]
        out_specs.append(ctspec)
        out_shape.append(_sds((N_HEADS, t), F32))
        args += [c, c_t]
    return _gridded_call(body, name=name, grid=(n_steps,), in_specs=in_specs, out_specs=out_specs, out_shape=out_shape,
                         scratch_shapes=[pltpu.VMEM((t, N_HEADS * lanes), F32)], args=args, exchange=exchange, prefetch=pairs)


def _mix_fwd(x, y_mla, y_fox, gates, b_gate, wbm_t, wbf_t, wo):
    t = x.shape[0]
    tm = min(PROJ_TILE, t)

    def body(x_ref, ym_ref, yf_ref, gt_ref, bg_ref, wbm_ref, wbf_ref, wo_ref, out_ref):
        um = _dot(ym_ref[...], wbm_ref[...], NT)
        uf = _dot(yf_ref[...], wbf_ref[...], NT)
        sm = jax.nn.sigmoid(gt_ref[:, 0:D_MODEL] + bg_ref[0:1, :])
        sf = jax.nn.sigmoid(gt_ref[:, D_MODEL:2 * D_MODEL] + bg_ref[1:2, :])
        mixed = sm * um + sf * uf
        out_ref[...] = x_ref[...] + _dot(mixed.astype(BF16), wo_ref[...])

    tok = _tok(tm, D_MODEL)
    return pl.pallas_call(
        body, name="mix_fwd", grid=(t // tm,),
        in_specs=[tok, _tok(tm, 512), _tok(tm, 512), _tok(tm, 2 * D_MODEL), _full((2, D_MODEL)), _full(wbm_t.shape),
                  _full(wbf_t.shape), _full(wo.shape)],
        out_specs=tok, out_shape=_sds((t, D_MODEL), F32),
        compiler_params=pltpu.CompilerParams(dimension_semantics=("arbitrary",)),
    )(x, y_mla, y_fox, gates, b_gate, wbm_t, wbf_t, wo)


def _mix_bwd(dx, y_mla, y_fox, y_mla32, y_fox32, gates, b_gate, wbm_t, wbf_t, wo, head_groups, exchange=None):
    t = dx.shape[0]
    tm = min(PROJ_TILE, t)

    def body(dx_ref, ym_ref, yf_ref, ym32_ref, yf32_ref, gt_ref, bg_ref, wbm_ref, wbf_ref, wo_ref, gf_ref,
             dym_ref, dyf_ref, dlm_ref, dlf_ref, dgt_ref, mixed_ref, dum_ref, duf_ref, dxb_ref, dbg_ref):
        i = pl.program_id(0)
        dxb = dx_ref[...].astype(BF16)
        dxb_ref[...] = dxb
        dmixed = _dot(dxb, wo_ref[...], NT)
        um = _dot(ym_ref[...], wbm_ref[...], NT)
        uf = _dot(yf_ref[...], wbf_ref[...], NT)
        sm = jax.nn.sigmoid(gt_ref[:, 0:D_MODEL] + bg_ref[0:1, :])
        sf = jax.nn.sigmoid(gt_ref[:, D_MODEL:2 * D_MODEL] + bg_ref[1:2, :])
        mixed_ref[...] = (sm * um + sf * uf).astype(BF16)
        dum = (dmixed * sm).astype(BF16)
        duf = (dmixed * sf).astype(BF16)
        dum_ref[...] = dum
        duf_ref[...] = duf
        dgm = dmixed * um * (sm * (1.0 - sm))
        dgf = dmixed * uf * (sf * (1.0 - sf))
        dgt_ref[:, 0:D_MODEL] = dgm.astype(BF16)
        dgt_ref[:, D_MODEL:2 * D_MODEL] = dgf.astype(BF16)
        for du, wb_ref, y32_ref, dy_ref, dl_ref in ((dum, wbm_ref, ym32_ref, dym_ref, dlm_ref),
                                                    (duf, wbf_ref, yf32_ref, dyf_ref, dlf_ref)):
            dy = _dot(du, wb_ref[...])
            dy_ref[...] = dy.astype(BF16)
            prod = dy.astype(BF16).astype(F32) * y32_ref[...]
            hi = prod.astype(BF16)
            rest = prod - hi.astype(F32)
            mid = rest.astype(BF16)
            dl_ref[...] = _dot(hi, gf_ref[...]) + _dot(mid, gf_ref[...]) + _dot((rest - mid.astype(F32)).astype(BF16), gf_ref[...])

        @pl.when(i == 0)
        def _():
            dbg_ref[...] = jnp.zeros_like(dbg_ref)

        dbg_ref[0:1, :] += _colsum(dgm)
        dbg_ref[1:2, :] += _colsum(dgf)

    tok = _tok(tm, D_MODEL)
    tokb = _sds((t, D_MODEL), BF16)
    t512, tg = _tok(tm, 512), _tok(tm, N_GROUPS)
    return _gridded_call(
        body, name="mix_bwd", grid=(t // tm,),
        in_specs=[tok, t512, t512, t512, t512, _tok(tm, 2 * D_MODEL), _full((2, D_MODEL)), _full(wbm_t.shape),
                  _full(wbf_t.shape), _full(wo.shape), _full(head_groups.shape)],
        out_specs=[t512, t512, tg, tg, _tok(tm, 2 * D_MODEL), tok, tok, tok, tok, _full((2, D_MODEL))],
        out_shape=[_sds((t, 512), BF16), _sds((t, 512), BF16), _sds((t, N_GROUPS), F32), _sds((t, N_GROUPS), F32),
                   _sds((t, 2 * D_MODEL), BF16), tokb, tokb, tokb, tokb, _sds((2, D_MODEL), F32)],
        scratch_shapes=[], args=[dx, y_mla, y_fox, y_mla32, y_fox32, gates, b_gate, wbm_t, wbf_t, wo, head_groups],
        exchange=exchange)


def _my_position():
    x, y, c = lax.axis_index("x"), lax.axis_index("y"), lax.axis_index("c")
    return x, y, c, 4 * x + 2 * y + c


def _peer(x, y, c, mask):
    px = 1 - x if mask & 4 else x
    py = 1 - y if mask & 2 else y
    pc = 1 - c if mask & 1 else c
    return (px, py, pc), 4 * px + 2 * py + pc


def _chip_peer(x, y, km):
    px = 1 - x if km & 2 else x
    py = 1 - y if km & 1 else y
    return px, py, 2 * px + py


_HBM = pl.BlockSpec(memory_space=pl.ANY)


def _wait_all(copies):
    for cp in copies:
        cp.wait()


class _ChipExchange:
    def __init__(self, gather, arrays):
        self.gather, self.arrays, self.aliased = gather, list(arrays), False
        n = len(self.arrays)
        self.out_shape = [_sds((N_DEV * a.shape[0],) + a.shape[1:], a.dtype) if gather else _sds(a.shape, a.dtype)
                          for a in self.arrays]
        self.scratch_shapes = [pltpu.SemaphoreType.DMA((n, N_CHIP)), pltpu.SemaphoreType.DMA((n, N_CHIP)),
                               pltpu.SemaphoreType.DMA((n,))]

    def copies(self, srcs, dsts, send_sems, recv_sems, local_sems):
        x, y, c, me = _my_position()
        q_me = 2 * x + y
        out = []
        for a in range(len(self.arrays)):
            if self.gather:
                r = srcs[a].shape[0]
                local_src, dst = srcs[a], dsts[a].at[pl.ds(me * r, r)]
            else:
                local_src, dst = srcs[a].at[q_me], dsts[a].at[q_me]
            out.append(pltpu.make_async_copy(local_src, dst, local_sems.at[a]))
            for km in range(1, N_CHIP):
                px, py, q_peer = _chip_peer(x, y, km)
                out.append(pltpu.make_async_remote_copy(
                    src_ref=srcs[a] if self.gather else srcs[a].at[q_peer], dst_ref=dst, send_sem=send_sems.at[a, km],
                    recv_sem=recv_sems.at[a, km], device_id=(px, py, c), device_id_type=MESH))
        return out

    def standalone(self, name):
        n = len(self.arrays)

        def body(*refs):
            copies = self.copies(refs[:n], refs[n:2 * n], *refs[2 * n:])
            for cp in copies:
                cp.start()
            _wait_all(copies)

        return pl.pallas_call(body, name=name, in_specs=[_HBM] * n, out_specs=[_HBM] * n, out_shape=self.out_shape,
                              scratch_shapes=self.scratch_shapes)(*self.arrays)


def _gridded_call(body, *, name, grid, in_specs, out_specs, out_shape, scratch_shapes, args, exchange=None, prefetch=()):
    params = pltpu.CompilerParams(dimension_semantics=("arbitrary",) * len(grid))
    n_pre, n_in, n_out, n_scr = len(prefetch), len(in_specs), len(out_specs), len(scratch_shapes)
    n_x = 0 if exchange is None else len(exchange.arrays)

    def carrier(*refs):
        pre, refs = refs[:n_pre], refs[n_pre:]
        ins, x_src, refs = refs[:n_in], refs[n_in:n_in + n_x], refs[n_in + n_x:]
        outs, x_dst, refs = refs[:n_out], refs[n_out:n_out + n_x], refs[n_out + n_x:]
        copies = exchange.copies(x_src, x_dst, *refs[n_scr:])
        pids = [pl.program_id(d) for d in range(len(grid))]
        first = functools.reduce(jnp.logical_and, [p == 0 for p in pids])
        last = functools.reduce(jnp.logical_and, [p == g - 1 for p, g in zip(pids, grid)])

        @pl.when(first)
        def _():
            for cp in copies:
                cp.start()

        body(*pre, *ins, *outs, *refs[:n_scr])

        @pl.when(last)
        def _():
            _wait_all(copies)

    x_shapes, x_scratch, x_args = ([], [], []) if exchange is None else (exchange.out_shape, exchange.scratch_shapes, exchange.arrays)
    spec = pltpu.PrefetchScalarGridSpec(
        num_scalar_prefetch=n_pre, grid=grid, in_specs=list(in_specs) + [_HBM] * n_x, out_specs=list(out_specs) + [_HBM] * n_x,
        scratch_shapes=list(scratch_shapes) + x_scratch)
    in_place = {n_pre + n_in + k: n_out + k for k in range(n_x)} if n_x and exchange.aliased else {}
    res = pl.pallas_call(body if exchange is None else carrier, name=name, grid_spec=spec, out_shape=list(out_shape) + x_shapes,
                         input_output_aliases=in_place, compiler_params=params)(*prefetch, *args, *x_args)
    return res[:n_out], (None if exchange is None else res[n_out:])


class _CoreExchange:
    def __init__(self, gather, arrays):
        self.gather, self.arrays, self.aliased = gather, list(arrays), gather
        n = len(self.arrays)
        self.out_shape = [_sds(a.shape, a.dtype) if gather else _sds((N_CHIP, a.shape[0] // N_DEV) + a.shape[1:], a.dtype)
                          for a in self.arrays]
        self.scratch_shapes = [pltpu.SemaphoreType.DMA((n, N_CHIP)), pltpu.SemaphoreType.DMA((n, N_CHIP))]

    def copies(self, srcs, dsts, send_sems, recv_sems):
        x, y, c, _ = _my_position()
        out = []
        for a in range(len(self.arrays)):
            r = srcs[a].shape[0] // N_DEV
            for q in range(N_CHIP):
                if self.gather:
                    rows = pl.ds((2 * q + c) * r, r)
                    src, dst = srcs[a].at[rows], dsts[a].at[rows]
                else:
                    src, dst = srcs[a].at[pl.ds((2 * q + 1 - c) * r, r)], dsts[a].at[q]
                out.append(pltpu.make_async_remote_copy(src_ref=src, dst_ref=dst, send_sem=send_sems.at[a, q],
                                                        recv_sem=recv_sems.at[a, q], device_id=(x, y, 1 - c), device_id_type=MESH))
        return out

    def standalone(self, name):
        n = len(self.arrays)

        def body(*refs):
            copies = self.copies(refs[:n], refs[n:2 * n], *refs[2 * n:])
            for cp in copies:
                cp.start()
            _wait_all(copies)

        return pl.pallas_call(body, name=name, in_specs=[_HBM] * n, out_specs=[_HBM] * n, out_shape=self.out_shape,
                              input_output_aliases={a: a for a in range(n)} if self.aliased else {},
                              scratch_shapes=self.scratch_shapes)(*self.arrays)


def _gather_over_cores(arrays, name):
    return _CoreExchange(True, arrays).standalone(name)


def _grads_to_sibling(grads, name):
    return _CoreExchange(False, grads).standalone(name)


def _pair_sum(grad, from_sibling, name):
    r, n = from_sibling.shape[1:]

    def body(g_ref, s_ref, o_ref):
        c = lax.axis_index("c")
        o_ref[...] = (g_ref[c].astype(F32) + s_ref[...].astype(F32)).astype(BF16)

    return pl.pallas_call(
        body, name=name, grid=(N_CHIP,),
        in_specs=[pl.BlockSpec((None, 2, r, n), lambda q: (q, 0, 0, 0)), pl.BlockSpec((None, r, n), lambda q: (q, 0, 0))],
        out_specs=pl.BlockSpec((None, r, n), lambda q: (q, 0, 0)), out_shape=_sds((N_CHIP, r, n), BF16),
    )(grad.reshape(N_CHIP, 2, r, n), from_sibling)


def _all_reduce_small(vec):
    r = vec.shape[0]

    def body(v_ref, o_ref, buf, send_sems, recv_sems):
        x, y, c, me = _my_position()
        buf[me] = v_ref[...]
        copies = []
        for mask in range(1, N_DEV):
            peer, _ = _peer(x, y, c, mask)
            cp = pltpu.make_async_remote_copy(src_ref=v_ref, dst_ref=buf.at[me], send_sem=send_sems.at[mask],
                                              recv_sem=recv_sems.at[mask], device_id=peer, device_id_type=MESH)
            cp.start()
            copies.append(cp)
        for cp in copies:
            cp.wait()
        total = buf[0]
        for s in range(1, N_DEV):
            total = total + buf[s]
        o_ref[...] = total

    vm = pl.BlockSpec(memory_space=pltpu.VMEM)
    return pl.pallas_call(
        body, name="all_reduce_small", in_specs=[vm], out_specs=vm, out_shape=_sds(vec.shape, F32),
        scratch_shapes=[pltpu.VMEM((N_DEV, r, 128), F32), pltpu.SemaphoreType.DMA((N_DEV,)), pltpu.SemaphoreType.DMA((N_DEV,))],
    )(vec)


def _adamw_math(w, g, m, v):
    m = ADAM_B1 * m + (1.0 - ADAM_B1) * g
    v = ADAM_B2 * v + (1.0 - ADAM_B2) * (g * g)
    m_hat = m / (1.0 - ADAM_B1 ** ADAM_STEP)
    v_hat = v / (1.0 - ADAM_B2 ** ADAM_STEP)
    delta = -ADAM_LR * (m_hat / (jnp.sqrt(v_hat) + ADAM_EPS) + ADAM_WD * w)
    return delta, m, v


def _reduce_adamw(slots, w, m, v, *, transpose, name):
    r, n = slots.shape[1:]
    padded = not transpose and w.shape[0] != r

    def body(s_ref, w_ref, m_ref, v_ref, g_ref, d_ref, nm_ref, nv_ref, *scr):
        g = s_ref[0].astype(F32)
        for s in range(1, slots.shape[0]):
            g = g + s_ref[s].astype(F32)
        if transpose:
            scr[0][...] = g.T
            g = scr[0][:, 0:w_ref.shape[1]]
        elif padded:
            scr[0][...] = g
            g = scr[0][0:w_ref.shape[0], :]
        g_ref[...] = g
        d_ref[...], nm_ref[...], nv_ref[...] = _adamw_math(w_ref[...], g, m_ref[...], v_ref[...])

    out = _sds(w.shape, F32)
    if not transpose and not padded and r % (2 * BF16_ROW_TILE) == 0:
        half = pl.BlockSpec((r // 2, n), lambda i: (i, 0))
        return pl.pallas_call(
            body, name=name, grid=(2,), in_specs=[pl.BlockSpec((slots.shape[0], r // 2, n), lambda i: (0, i, 0)), half, half, half],
            out_specs=[half] * 4, out_shape=[out, out, out, out])(slots, w, m, v)
    return pl.pallas_call(
        body, name=name, out_shape=[out, out, out, out],
        scratch_shapes=[pltpu.VMEM((n, r) if transpose else (r, n), F32)] if transpose or padded else [],
    )(slots, w, m, v)


def _adamw(g, w, m, v, name):
    def body(g_ref, w_ref, m_ref, v_ref, d_ref, nm_ref, nv_ref):
        d_ref[...], nm_ref[...], nv_ref[...] = _adamw_math(w_ref[...], g_ref[...], m_ref[...], v_ref[...])

    out = _sds(w.shape, F32)
    return pl.pallas_call(body, name=name, out_shape=[out, out, out])(g, w, m, v)


_SMALL = ["ffn1_norm", "mix_norm", "ffn2_norm", "mla_q_lat_norm", "mla_kv_lat_norm", "mla_q_nope_gain", "mla_q_rope_gain",
          "mla_k_nope_gain", "mla_k_rope_gain", "fox_q_gain", "fox_k_gain", "fox_b_f"]
_WEIGHTS = ["ffn1_norm", "ffn1_w_gate", "ffn1_w_up", "ffn1_w_down", "mix_norm", "w_in", "mla_q_lat_norm", "mla_w_qb",
            "mla_kv_lat_norm", "mla_w_kvb", "mla_q_nope_gain", "mla_q_rope_gain", "mla_k_nope_gain", "mla_k_rope_gain",
            "fox_q_gain", "fox_k_gain", "fox_b_f", "w_branch_mla", "w_branch_fox", "b_gate", "w_o", "ffn2_norm",
            "ffn2_w_gate", "ffn2_w_up", "ffn2_w_down"]
_IN_Q, _IN_KV, _IN_KR, _IN_FOX, _IN_F, _IN_GATES = (0, 192), (192, 128), (320, 32), (352, 1536), (1888, 8), (1896, 2048)


def _rows(a, seg):
    return a[seg[0]:seg[0] + seg[1]]


def _split_w_in(win_t):
    z = lambda n: jnp.zeros((n, D_MODEL), win_t.dtype)
    lat = jnp.concatenate([_rows(win_t, _IN_Q), z(LAT_KV - Q_LORA), _rows(win_t, _IN_KV), _rows(win_t, _IN_KR),
                           _rows(win_t, _IN_F), z(LAT_W - LAT_F - N_HEADS)], axis=0)
    return _rows(win_t, _IN_GATES), _rows(win_t, _IN_FOX), lat


def _join_w_in(d_gates, d_fox, d_lat):
    return jnp.concatenate([d_lat[LAT_Q:LAT_Q + Q_LORA], d_lat[LAT_KV:LAT_KV + KV_LORA], d_lat[LAT_KR:LAT_KR + ROPE_DIM],
                            d_fox, d_lat[LAT_F:LAT_F + N_HEADS], d_gates], axis=0)


def kernel(x, positions, ffn1_norm, ffn1_w_gate, ffn1_w_up, ffn1_w_down, mix_norm, w_in, mla_q_lat_norm, mla_w_qb, mla_kv_lat_norm, mla_w_kvb, mla_q_nope_gain, mla_q_rope_gain, mla_k_nope_gain, mla_k_rope_gain, fox_q_gain, fox_k_gain, fox_b_f, w_branch_mla, w_branch_fox, b_gate, w_o, ffn2_norm, ffn2_w_gate, ffn2_w_up, ffn2_w_down, loss_target, m_ffn1_norm, m_ffn1_w_gate, m_ffn1_w_up, m_ffn1_w_down, m_mix_norm, m_w_in, m_mla_q_lat_norm, m_mla_w_qb, m_mla_kv_lat_norm, m_mla_w_kvb, m_mla_q_nope_gain, m_mla_q_rope_gain, m_mla_k_nope_gain, m_mla_k_rope_gain, m_fox_q_gain, m_fox_k_gain, m_fox_b_f, m_w_branch_mla, m_w_branch_fox, m_b_gate, m_w_o, m_ffn2_norm, m_ffn2_w_gate, m_ffn2_w_up, m_ffn2_w_down, v_ffn1_norm, v_ffn1_w_gate, v_ffn1_w_up, v_ffn1_w_down, v_mix_norm, v_w_in, v_mla_q_lat_norm, v_mla_w_qb, v_mla_kv_lat_norm, v_mla_w_kvb, v_mla_q_nope_gain, v_mla_q_rope_gain, v_mla_k_nope_gain, v_mla_k_rope_gain, v_fox_q_gain, v_fox_k_gain, v_fox_b_f, v_w_branch_mla, v_w_branch_fox, v_b_gate, v_w_o, v_ffn2_norm, v_ffn2_w_gate, v_ffn2_w_up, v_ffn2_w_down):
    env = dict(locals())
    strip = lambda n, a: a if n in _SMALL else a[0]
    W = {n: strip(n, env[n]) for n in _WEIGHTS}
    M = {n: strip(n, env["m_" + n]) for n in _WEIGHTS}
    V = {n: strip(n, env["v_" + n]) for n in _WEIGHTS}
    xs = x[0]
    t = xs.shape[0]

    col_split = ["ffn1_w_gate", "ffn1_w_up", "ffn2_w_gate", "ffn2_w_up", "mla_w_qb", "mla_w_kvb", "w_branch_mla", "w_branch_fox"]
    row_split = ["ffn1_w_down", "ffn2_w_down", "w_o"]
    pieces = {n: W[n].T.astype(BF16) for n in col_split}
    pieces.update({n: W[n].astype(BF16) for n in row_split})
    pieces["w_in"] = jnp.pad(W["w_in"].T.astype(BF16), ((0, W_IN_PIECE_PAD - W_IN_PIECE), (0, 0)))
    pieces["b_gate"] = W["b_gate"].T
    group_a = ["ffn1_w_gate", "ffn1_w_up", "ffn1_w_down"]
    group_b = ["w_in", "mla_w_qb", "mla_w_kvb", "w_branch_mla", "w_branch_fox", "w_o", "b_gate"]
    group_c = ["ffn2_w_gate", "ffn2_w_up", "ffn2_w_down"]
    gather = lambda group: _ChipExchange(True, [pieces[n] for n in group])
    G = dict(zip(group_a, _gather_over_cores(gather(group_a).standalone("gather_ici_a"), "gather_d2d_a")))

    inv_freq = ROPE_THETA ** (-jnp.arange(ROPE_HALF, dtype=F32) / ROPE_HALF)
    ang = positions[0].astype(F32)[:, None] * inv_freq
    tables, mats = _prep_tables(jnp.cos(ang), jnp.sin(ang)), _group_matrices()
    gains = _head_gains(*[W[n] for n in ["mla_q_nope_gain", "mla_q_rope_gain", "mla_k_nope_gain", "mla_k_rope_gain",
                                         "fox_q_gain", "fox_k_gain"]])

    (x1, a1, b1), got_b = _ffn_fwd(xs, W["ffn1_norm"], G["ffn1_w_gate"], G["ffn1_w_up"], G["ffn1_w_down"],
                                   exchange=gather(group_b))
    G.update(zip(group_b, _gather_over_cores(got_b, "gather_d2d_b")))
    win_t = G["w_in"].reshape(N_DEV, W_IN_PIECE_PAD, D_MODEL)[:, :W_IN_PIECE].reshape(N_DEV * W_IN_PIECE, D_MODEL)
    wgate_t, wfox_t, wlat_t = _split_w_in(win_t)
    bg = G["b_gate"].T
    gates, fox, lat = _proj_fwd(x1, W["mix_norm"], wgate_t, wfox_t, wlat_t)
    prep_args = (fox, lat, tables, mats, *_interleave_weights(G["mla_w_qb"], G["mla_w_kvb"]), W["mla_q_lat_norm"],
                 W["mla_kv_lat_norm"], gains, W["fox_b_f"])
    fq, fk, fv, c, mq, mk, mv, fq_sq, fk_sq_max, mq_sq, mk_sq_max, fv1, mv1 = _prep_fwd(*prep_args, *_value_layout(prep_args[6]))
    c_t = c.T
    b_fox, ok_fox = _logit_bound(fq_sq, fk_sq_max)
    b_mla, ok_mla = _logit_bound(mq_sq, mk_sq_max)
    (y_fox, y_fox32, lse_fox), got_c = _attn_fwd(fq, fk, fv1, b_fox, ok_fox, c, c_t, lanes=HEAD_DIM, name="fox_fwd",
                                                 exchange=gather(group_c))
    (y_mla, y_mla32, lse_mla), got_c = _attn_fwd(mq, mk, mv1, b_mla, ok_mla, None, None, lanes=MLA_QK_LANES, name="mla_fwd",
                                                 exchange=_CoreExchange(True, got_c))
    G.update(zip(group_c, got_c))
    x2 = _mix_fwd(x1, y_mla, y_fox, gates, bg, G["w_branch_mla"], G["w_branch_fox"], G["w_o"])
    (dx3, a2, b2, loss_vec), _ = _ffn_fwd(x2, W["ffn2_norm"], G["ffn2_w_gate"], G["ffn2_w_up"], G["ffn2_w_down"],
                                          target=loss_target[0])

    def chip_sums(group, tag, from_sibling=None):
        if from_sibling is None:
            from_sibling = _grads_to_sibling([grads[n] for n in group], "grads_d2d_" + tag)
        return _ChipExchange(False, [_pair_sum(grads[n], s, "pair_sum_" + n) for n, s in zip(group, from_sibling)])

    (dx2, dg_ffn2, da2, db2, h2, n2, dyh2), _ = _ffn_bwd(dx3, x2, W["ffn2_norm"], a2, b2, G["ffn2_w_gate"], G["ffn2_w_up"],
                                                        G["ffn2_w_down"], "ffn2_bwd")
    grads = {"ffn2_w_gate": _tn_matmul(da2, n2, "ffn2_dgate"), "ffn2_w_up": _tn_matmul(db2, n2, "ffn2_dup"),
             "ffn2_w_down": _tn_matmul(h2, dyh2, "ffn2_ddown")}
    (dy_mla, dy_fox, delta_mla, delta_fox, dgates, mixed, dum, duf, dx2b, dbg), from_sibling_c = _mix_bwd(
        dx2, y_mla, y_fox, y_mla32, y_fox32, gates, bg, G["w_branch_mla"], G["w_branch_fox"], G["w_o"], mats[3],
        exchange=_CoreExchange(False, [grads[n] for n in group_c]))
    delta_mla, delta_fox = delta_mla[:, :N_HEADS], delta_fox[:, :N_HEADS]
    grads["w_o"] = _tn_matmul(mixed, dx2b, "d_w_o")
    grads["w_branch_mla"] = _tn_matmul(dum, y_mla, "d_w_branch_mla")
    grads["w_branch_fox"] = _tn_matmul(duf, y_fox, "d_w_branch_fox")
    (dfq, dfk, dfv, dc_t), slots_c = _attn_bwd(fq, fk, fv, dy_fox, delta_fox, lse_fox, c, c_t, lanes=HEAD_DIM,
                                               name="fox_bwd", exchange=chip_sums(group_c, "c", from_sibling_c))
    slots = dict(zip(group_c, slots_c))
    (dmq, dmk, dmv), _ = _attn_bwd(mq, mk, mv, dy_mla, delta_mla, lse_mla, None, None, lanes=MLA_QK_LANES, name="mla_bwd")
    dfox, dlat, dwq, dwk, dwv, d_gql, d_gkvl, d_gq, d_gk, d_gkr, d_gfq, d_gfk, d_bf = _prep_bwd(
        *prep_args, dfq, dfk, dfv, dc_t.T, dmq, dmk, dmv)
    grads["mla_w_qb"], grads["mla_w_kvb"] = _deinterleave_grads(dwq, dwk, dwv)
    fold = lambda a, width: a.reshape(N_HEADS, width).sum(axis=0)[None]
    d_gq, d_gk = fold(d_gq, LANES), fold(d_gk, LANES)
    d_prep_small = [d_gql, d_gkvl, d_gq[:, :HEAD_DIM], d_gq[:, HEAD_DIM:HEAD_DIM + ROPE_DIM], d_gk[:, :HEAD_DIM],
                    d_gkr[:, HEAD_DIM:HEAD_DIM + ROPE_DIM], fold(d_gfq, HEAD_DIM), fold(d_gfk, HEAD_DIM), d_bf]
    dx1, dg_mix, nmix = _proj_bwd(dgates, dfox, dlat, wgate_t, wfox_t, wlat_t, x1, W["mix_norm"], dx2)
    dwin_t = _join_w_in(_tn_matmul(dgates, nmix, "d_w_in_gates"), _tn_matmul(dfox, nmix, "d_w_in_fox"),
                        _tn_matmul(dlat, nmix, "d_w_in_lat"))
    grads["w_in"] = jnp.pad(dwin_t.reshape(N_DEV, W_IN_PIECE, D_MODEL), ((0, 0), (0, W_IN_PIECE_PAD - W_IN_PIECE), (0, 0))
                            ).reshape(N_DEV * W_IN_PIECE_PAD, D_MODEL)
    grad_group_b = [n for n in group_b if n != "b_gate"]
    (dx0, dg_ffn1, da1, db1, h1, n1, dyh1), slots_b = _ffn_bwd(dx1, xs, W["ffn1_norm"], a1, b1, G["ffn1_w_gate"], G["ffn1_w_up"],
                                                              G["ffn1_w_down"], "ffn1_bwd", exchange=chip_sums(grad_group_b, "b"))
    slots.update(zip(grad_group_b, slots_b))
    grads["ffn1_w_gate"] = _tn_matmul(da1, n1, "ffn1_dgate")
    grads["ffn1_w_up"], got = _tn_matmul(db1, n1, "ffn1_dup", exchange=chip_sums(["ffn1_w_gate"], "a_gate"))
    slots["ffn1_w_gate"] = got[0]
    grads["ffn1_w_down"], got = _tn_matmul(h1, dyh1, "ffn1_ddown", exchange=chip_sums(["ffn1_w_up"], "a_up"))
    slots["ffn1_w_up"] = got[0]
    slots["ffn1_w_down"] = chip_sums(["ffn1_w_down"], "a_down").standalone("grads_ici_a_down")[0]

    small_parts = [dg_ffn1, dg_mix, dg_ffn2] + list(d_prep_small) + [dbg.reshape(1, 2 * D_MODEL), loss_vec]
    flat = jnp.concatenate([p.reshape(-1) for p in small_parts])
    n_flat = flat.shape[0]
    rows = -(-n_flat // (8 * 128)) * 8
    total = _all_reduce_small(jnp.pad(flat, (0, rows * 128 - n_flat)).reshape(rows, 128)).reshape(-1)
    offs, small_g = 0, {}
    for n in _SMALL:
        small_g[n] = total[offs:offs + W[n].shape[1]].reshape(W[n].shape)
        offs += W[n].shape[1]
    bg_full = total[offs:offs + 2 * D_MODEL].reshape(2, D_MODEL)
    offs += 2 * D_MODEL
    loss = (0.5 / D_MODEL) * jnp.sum(total[offs:offs + D_MODEL])
    _, _, _, me = _my_position()
    small_g["b_gate"] = lax.dynamic_slice_in_dim(bg_full, me * (D_MODEL // N_DEV), D_MODEL // N_DEV, axis=1)

    transposed_in_memory = ["ffn1_w_gate", "ffn1_w_up", "ffn2_w_gate", "ffn2_w_up", "w_in"]
    res = {}
    for n in _WEIGHTS:
        if n in small_g:
            res[n] = (small_g[n],) + tuple(_adamw(small_g[n], W[n], M[n], V[n], "adamw_" + n))
        elif n in transposed_in_memory:
            res[n] = tuple(o.T for o in _reduce_adamw(slots[n], W[n].T, M[n].T, V[n].T, transpose=False, name="adamw_" + n))
        else:
            res[n] = tuple(_reduce_adamw(slots[n], W[n], M[n], V[n], transpose=n in col_split, name="adamw_" + n))
    outs = [loss, dx0[None]]
    for k in range(4):
        outs += [res[n][k] if n in _SMALL else res[n][k][None] for n in _WEIGHTS]
    return tuple(outs)
```

```python
import functools

import jax
import jax.numpy as jnp
from jax import lax
from jax.experimental import pallas as pl
from jax.experimental.pallas import tpu as pltpu

F32 = jnp.float32
BF16 = jnp.bfloat16

D_MODEL = 1024
FFN_HIDDEN = 2816
N_HEADS = 8
HEAD_DIM = 64
ROPE_DIM = 32
ROPE_HALF = 16
Q_LORA = 192
KV_LORA = 128
ROPE_THETA = 10000.0
RMS_EPS = 1e-6
MLA_SCALE = (HEAD_DIM + ROPE_DIM) ** -0.5
FOX_SCALE = HEAD_DIM ** -0.5
MLA_QK_LANES = 128
ADAM_LR, ADAM_B1, ADAM_B2, ADAM_EPS, ADAM_WD, ADAM_STEP = 0.001, 0.9, 0.999, 1e-08, 0.01, 10
N_DEV = 8
N_CHIP = 4
W_IN_PIECE = 493
BF16_ROW_TILE = 16
W_IN_PIECE_PAD = 496
LAT_W = 512
LAT_Q, LAT_KV, LAT_KR, LAT_F = 0, 256, 384, 416
MASK_VALUE = -1e30
FIXED_SHIFT_MAX_BOUND = 30.0

TOK_TILE = 512
DW_TOK_TILE = 1024
PROJ_TILE = 512
PREP_TILE = 256
ATT_TILE = 512
ATT_COL_CHUNK = 256
FFN_HID_TILE = 1408
FFN_HID_SPLIT = ((0, 768), (768, 1408))

NT = (((1,), (1,)), ((), ()))
TN = (((0,), (0,)), ((), ()))
NN = (((1,), (0,)), ((), ()))
MESH = pl.DeviceIdType.MESH


def _dot(a, b, dims=NN):
    return lax.dot_general(a, b, dims, preferred_element_type=F32)


def _sds(shape, dtype):
    return jax.ShapeDtypeStruct(shape, dtype)


def _rms_fwd(x, g):
    r = lax.rsqrt(jnp.mean(x * x, axis=-1, keepdims=True) + RMS_EPS)
    return x * r * g, r


def _rms_bwd(dy, x, g, r):
    xn = x * r
    dyg = dy * g
    dx = r * (dyg - xn * jnp.mean(dyg * xn, axis=-1, keepdims=True))
    return dx, dy * xn


def _colsum(x):
    return jnp.sum(x, axis=0, keepdims=True)


def _full(shape):
    return pl.BlockSpec(shape, lambda *_: (0,) * len(shape))


def _tok(tm, n):
    return pl.BlockSpec((tm, n), lambda i, *_: (i, 0))


def _ffn_fwd(x, gain, wg_t, wu_t, wd, target=None, exchange=None):
    t = x.shape[0]
    tm = min(TOK_TILE, t)
    tf = FFN_HID_TILE
    n_t, n_f = t // tm, FFN_HIDDEN // tf
    with_loss = target is not None

    def body(*refs):
        if with_loss:
            x_ref, g_ref, wg_ref, wu_ref, wd_ref, t_ref, out_ref, a_ref, b_ref, lvec_ref, n_scr, acc = refs
        else:
            x_ref, g_ref, wg_ref, wu_ref, wd_ref, out_ref, a_ref, b_ref, n_scr, acc = refs
        i, j = pl.program_id(0), pl.program_id(1)

        @pl.when(j == 0)
        def _():
            xn, _ = _rms_fwd(x_ref[...], g_ref[...])
            n_scr[...] = xn.astype(BF16)
            acc[...] = jnp.zeros_like(acc)

        n = n_scr[...]
        a = _dot(n, wg_ref[...], NT)
        b = _dot(n, wu_ref[...], NT)
        a_ref[...] = a.astype(BF16)
        b_ref[...] = b.astype(BF16)
        h = (a * jax.nn.sigmoid(a)) * b
        acc[...] += _dot(h.astype(BF16), wd_ref[...])

        @pl.when(j == n_f - 1)
        def _():
            y = x_ref[...] + 0.5 * acc[...]
            if with_loss:
                diff = y - t_ref[...]
                out_ref[...] = diff * (1.0 / D_MODEL)
                sq = _colsum(diff * diff)

                @pl.when(i == 0)
                def _():
                    lvec_ref[...] = sq

                @pl.when(i > 0)
                def _():
                    lvec_ref[...] += sq
            else:
                out_ref[...] = y

    wspec = pl.BlockSpec((tf, D_MODEL), lambda i, j: (j, 0))
    hspec = pl.BlockSpec((tm, tf), lambda i, j: (i, j))
    in_specs = [_tok(tm, D_MODEL), _full((1, D_MODEL)), wspec, wspec, wspec]
    out_specs = [_tok(tm, D_MODEL), hspec, hspec]
    out_shape = [_sds((t, D_MODEL), F32), _sds((t, FFN_HIDDEN), BF16), _sds((t, FFN_HIDDEN), BF16)]
    args = [x, gain, wg_t, wu_t, wd]
    if with_loss:
        in_specs.append(_tok(tm, D_MODEL))
        out_specs.append(_full((1, D_MODEL)))
        out_shape.append(_sds((1, D_MODEL), F32))
        args.append(target)
    return _gridded_call(
        body, name="ffn_fwd_loss" if with_loss else "ffn_fwd", grid=(n_t, n_f), in_specs=in_specs, out_specs=out_specs,
        out_shape=out_shape, scratch_shapes=[pltpu.VMEM((tm, D_MODEL), BF16), pltpu.VMEM((tm, D_MODEL), F32)],
        args=args, exchange=exchange)


def _ffn_bwd(dy, x, gain, a, b, wg_t, wu_t, wd, name, exchange=None):
    t = x.shape[0]
    tm = min(TOK_TILE, t)
    tf = FFN_HID_TILE
    n_t, n_f = t // tm, FFN_HIDDEN // tf

    def body(dy_ref, x_ref, g_ref, a_ref, b_ref, wg_ref, wu_ref, wd_ref,
             dx_ref, dg_ref, da_ref, db_ref, h_ref, n_ref, dyh_ref, acc):
        i, j = pl.program_id(0), pl.program_id(1)

        @pl.when(j == 0)
        def _():
            xn, _ = _rms_fwd(x_ref[...], g_ref[...])
            n_ref[...] = xn.astype(BF16)
            dyh_ref[...] = (0.5 * dy_ref[...]).astype(BF16)
            acc[...] = jnp.zeros_like(acc)

        dyh = dyh_ref[...]
        for lo, hi in FFN_HID_SPLIT:
            dh = _dot(dyh, wd_ref[lo:hi, :], NT)
            av = a_ref[:, lo:hi].astype(F32)
            bv = b_ref[:, lo:hi].astype(F32)
            s = jax.nn.sigmoid(av)
            silu = av * s
            da = (dh * bv * (s * (1.0 + av * (1.0 - s)))).astype(BF16)
            db = (dh * silu).astype(BF16)
            da_ref[:, lo:hi] = da
            db_ref[:, lo:hi] = db
            h_ref[:, lo:hi] = (silu * bv).astype(BF16)
            acc[...] += _dot(da, wg_ref[lo:hi, :]) + _dot(db, wu_ref[lo:hi, :])

        @pl.when(j == n_f - 1)
        def _():
            xv, g = x_ref[...], g_ref[...]
            r = lax.rsqrt(jnp.mean(xv * xv, axis=-1, keepdims=True) + RMS_EPS)
            dx, dg_rows = _rms_bwd(acc[...], xv, g, r)
            dx_ref[...] = dy_ref[...] + dx
            dg = _colsum(dg_rows)

            @pl.when(i == 0)
            def _():
                dg_ref[...] = dg

            @pl.when(i > 0)
            def _():
                dg_ref[...] += dg

    wspec = pl.BlockSpec((tf, D_MODEL), lambda i, j: (j, 0))
    hspec = pl.BlockSpec((tm, tf), lambda i, j: (i, j))
    tok = _tok(tm, D_MODEL)
    hid = _sds((t, FFN_HIDDEN), BF16)
    return _gridded_call(
        body, name=name, grid=(n_t, n_f),
        in_specs=[tok, tok, _full((1, D_MODEL)), hspec, hspec, wspec, wspec, wspec],
        out_specs=[tok, _full((1, D_MODEL)), hspec, hspec, hspec, tok, tok],
        out_shape=[_sds((t, D_MODEL), F32), _sds((1, D_MODEL), F32), hid, hid, hid,
                   _sds((t, D_MODEL), BF16), _sds((t, D_MODEL), BF16)],
        scratch_shapes=[pltpu.VMEM((tm, D_MODEL), F32)], args=[dy, x, gain, a, b, wg_t, wu_t, wd], exchange=exchange)


def _tn_matmul(a, b, name, exchange=None):
    t, m = a.shape
    n = b.shape[1]
    tk = min(DW_TOK_TILE, t)
    n_k = t // tk

    def body(a_ref, b_ref, o_ref, acc):
        k = pl.program_id(0)
        p = _dot(a_ref[...], b_ref[...], TN)

        @pl.when(k == 0)
        def _():
            acc[...] = p

        @pl.when(k > 0)
        def _():
            acc[...] += p

        @pl.when(k == n_k - 1)
        def _():
            o_ref[...] = acc[...].astype(BF16)

    (out,), got = _gridded_call(
        body, name=name, grid=(n_k,), in_specs=[_tok(tk, m), _tok(tk, n)], out_specs=[_full((m, n))],
        out_shape=[_sds((m, n), BF16)], scratch_shapes=[pltpu.VMEM((m, n), F32)], args=[a, b], exchange=exchange)
    return out if exchange is None else (out, got)


def _proj_fwd(x, gain, wgate_t, wfox_t, wlat_t):
    t = x.shape[0]
    tm = min(PROJ_TILE, t)

    def body(x_ref, g_ref, wg_ref, wf_ref, wl_ref, og_ref, of_ref, ol_ref):
        xn, _ = _rms_fwd(x_ref[...], g_ref[...])
        n = xn.astype(BF16)
        og_ref[...] = _dot(n, wg_ref[...], NT)
        of_ref[...] = _dot(n, wf_ref[...], NT)
        ol_ref[...] = _dot(n, wl_ref[...], NT)

    return pl.pallas_call(
        body, name="proj_fwd", grid=(t // tm,),
        in_specs=[_tok(tm, D_MODEL), _full((1, D_MODEL)), _full(wgate_t.shape), _full(wfox_t.shape), _full(wlat_t.shape)],
        out_specs=[_tok(tm, 2 * D_MODEL), _tok(tm, 3 * 512), _tok(tm, LAT_W)],
        out_shape=[_sds((t, 2 * D_MODEL), F32), _sds((t, 3 * 512), F32), _sds((t, LAT_W), F32)],
        compiler_params=pltpu.CompilerParams(dimension_semantics=("arbitrary",)),
    )(x, gain, wgate_t, wfox_t, wlat_t)


def _proj_bwd(dgates, dfox, dlat, wgate_t, wfox_t, wlat_t, x, gain, dres):
    t = x.shape[0]
    tm = min(PROJ_TILE, t)

    def body(dg_ref, df_ref, dl_ref, wg_ref, wf_ref, wl_ref, x_ref, g_ref, dres_ref, dx_ref, dgain_ref, n_ref):
        i = pl.program_id(0)
        dn = _dot(dg_ref[...], wg_ref[...]) + _dot(df_ref[...], wf_ref[...]) + _dot(dl_ref[...], wl_ref[...])
        xv, g = x_ref[...], g_ref[...]
        xn, r = _rms_fwd(xv, g)
        n_ref[...] = xn.astype(BF16)
        dx, dg_rows = _rms_bwd(dn, xv, g, r)
        dx_ref[...] = dres_ref[...] + dx
        dgn = _colsum(dg_rows)

        @pl.when(i == 0)
        def _():
            dgain_ref[...] = dgn

        @pl.when(i > 0)
        def _():
            dgain_ref[...] += dgn

    tok = _tok(tm, D_MODEL)
    return pl.pallas_call(
        body, name="proj_bwd", grid=(t // tm,),
        in_specs=[_tok(tm, 2 * D_MODEL), _tok(tm, 3 * 512), _tok(tm, LAT_W), _full(wgate_t.shape), _full(wfox_t.shape),
                  _full(wlat_t.shape), tok, _full((1, D_MODEL)), tok],
        out_specs=[tok, _full((1, D_MODEL)), tok],
        out_shape=[_sds((t, D_MODEL), F32), _sds((1, D_MODEL), F32), _sds((t, D_MODEL), BF16)],
        compiler_params=pltpu.CompilerParams(dimension_semantics=("arbitrary",)),
    )(dgates, dfox, dlat, wgate_t, wfox_t, wlat_t, x, gain, dres)


def _tri(n, lower):
    r = lax.broadcasted_iota(jnp.int32, (n, n), 0)
    c = lax.broadcasted_iota(jnp.int32, (n, n), 1)
    return ((c <= r) if lower else (c >= r)).astype(F32)


def _log_sigmoid(z):
    return jnp.minimum(z, 0.0) - jnp.log1p(jnp.exp(-jnp.abs(z)))


N_GROUPS = 16
LANES = MLA_QK_LANES


def _split_dot(a, g):
    hi = a.astype(BF16)
    lo = (a - hi.astype(F32)).astype(BF16)
    return _dot(hi, g) + _dot(lo, g)


def _rope_fwd(y, c, s1, s2):
    return y * c + pltpu.roll(y, LANES - ROPE_HALF, 1) * s1 + pltpu.roll(y, ROPE_HALF, 1) * s2


def _rope_bwd(do, c, s1, s2):
    return do * c + pltpu.roll(do * s1, ROPE_HALF, 1) + pltpu.roll(do * s2, LANES - ROPE_HALF, 1)


def _prep_tables(cos, sin):
    t = cos.shape[0]
    z = lambda n: jnp.zeros((t, n), F32)
    c = jnp.concatenate([jnp.ones((t, HEAD_DIM), F32), cos, cos, z(LANES - HEAD_DIM - ROPE_DIM)], axis=1)
    s1 = jnp.concatenate([z(HEAD_DIM), -sin, z(LANES - HEAD_DIM - ROPE_HALF)], axis=1)
    s2 = jnp.concatenate([z(HEAD_DIM + ROPE_HALF), sin, z(LANES - HEAD_DIM - ROPE_DIM)], axis=1)
    lane = jnp.arange(LANES)
    rope_mask = ((lane >= HEAD_DIM) & (lane < HEAD_DIM + ROPE_DIM)).astype(F32)[None, :]
    return c, s1, s2, rope_mask


def _group_matrices():
    lane = jnp.arange(N_HEADS * LANES)
    head, d = (lane // LANES)[:, None], (lane % LANES)[:, None]
    col = jnp.arange(N_GROUPS)[None, :]
    g_mla = ((col == head) & (d < HEAD_DIM)) | ((col == N_HEADS + head) & (d >= HEAD_DIM) & (d < HEAD_DIM + ROPE_DIM))
    g_fox = col == (jnp.arange(N_HEADS * HEAD_DIM) // HEAD_DIM)[:, None]
    inv_mla = jnp.concatenate([jnp.full((1, N_HEADS), 1.0 / HEAD_DIM, F32), jnp.full((1, N_HEADS), 1.0 / ROPE_DIM, F32)], axis=1)
    inv_fox = jnp.full((1, N_GROUPS), 1.0 / HEAD_DIM, F32)
    return g_mla.astype(BF16), g_mla.T.astype(BF16), inv_mla, g_fox.astype(BF16), g_fox.T.astype(BF16), inv_fox


def _interleave_weights(wqb_t, wkvb_t):
    wq = jnp.pad(wqb_t.reshape(N_HEADS, HEAD_DIM + ROPE_DIM, Q_LORA), ((0, 0), (0, LANES - HEAD_DIM - ROPE_DIM), (0, 0)))
    kv = wkvb_t.reshape(N_HEADS, 2, HEAD_DIM, KV_LORA)
    wk = jnp.pad(kv[:, 0], ((0, 0), (0, LANES - HEAD_DIM), (0, 0)))
    return wq.reshape(N_HEADS * LANES, Q_LORA), wk.reshape(N_HEADS * LANES, KV_LORA), kv[:, 1].reshape(N_HEADS * HEAD_DIM, KV_LORA)


def _deinterleave_grads(dwq, dwk, dwv):
    dq = dwq.reshape(N_HEADS, LANES, Q_LORA)[:, :HEAD_DIM + ROPE_DIM].reshape(N_HEADS * (HEAD_DIM + ROPE_DIM), Q_LORA)
    dk = dwk.reshape(N_HEADS, LANES, KV_LORA)[:, :HEAD_DIM]
    dkv = jnp.stack([dk, dwv.reshape(N_HEADS, HEAD_DIM, KV_LORA)], axis=1)
    return dq, dkv.reshape(N_HEADS * 2 * HEAD_DIM, KV_LORA)


def _head_gains(g_qn, g_qr, g_kn, g_kr, g_fq, g_fk):
    z = lambda n: jnp.zeros((1, n), F32)
    gq = jnp.concatenate([g_qn, g_qr, z(LANES - HEAD_DIM - ROPE_DIM)], axis=1)
    gk = jnp.concatenate([g_kn, z(LANES - HEAD_DIM)], axis=1)
    gkr = jnp.concatenate([z(HEAD_DIM), g_kr, z(LANES - HEAD_DIM - ROPE_DIM)], axis=1)
    return jnp.tile(gq, (1, N_HEADS)), jnp.tile(gk, (1, N_HEADS)), gkr, jnp.tile(g_fq, (1, N_HEADS)), jnp.tile(g_fk, (1, N_HEADS))


def _group_rms(x, g, g_t, inv):
    r = lax.rsqrt(_split_dot(x * x, g) * inv + RMS_EPS)
    return _split_dot(r, g_t)


def _value_layout(wv):
    lane = jnp.arange(N_HEADS * LANES)
    src = (lane // LANES) * HEAD_DIM + lane % LANES
    place = (jnp.arange(N_HEADS * HEAD_DIM)[:, None] == src[None, :]) & (lane % LANES < HEAD_DIM)[None, :]
    ones = (lane % LANES == HEAD_DIM).astype(F32)[None, :]
    wv_il = jnp.pad(wv.reshape(N_HEADS, HEAD_DIM, KV_LORA), ((0, 0), (0, LANES - HEAD_DIM), (0, 0)))
    return place.astype(BF16), ones, wv_il.reshape(N_HEADS * LANES, KV_LORA)


def _prep_fwd(fox, lat, tables, mats, wq, wk, wv, g_qlat, g_kvlat, gains, b_f, v_place, v_ones, wv_il):
    t = fox.shape[0]
    tm = min(PREP_TILE, t)
    c_tab, s1_tab, s2_tab, rope_mask = tables

    def body(fox_ref, lat_ref, c_ref, s1_ref, s2_ref, rm_ref, gm_ref, gmt_ref, im_ref, gf_ref, gft_ref, if_ref,
             wq_ref, wk_ref, wv_ref, gql_ref, gkvl_ref, gq_ref, gk_ref, gkr_ref, gfq_ref, gfk_ref, bf_ref,
             vp_ref, vo_ref, wvil_ref,
             fq_ref, fk_ref, fv_ref, cc_ref, mq_ref, mk_ref, mv_ref, fqn_ref, fkmax_ref, mqn_ref, mkmax_ref,
             fv1_ref, mv1_ref, carry):
        i = pl.program_id(0)
        ct, s1, s2 = c_ref[...], s1_ref[...], s2_ref[...]

        @pl.when(i == 0)
        def _():
            carry[...] = jnp.zeros_like(carry)
            fkmax_ref[...] = jnp.zeros_like(fkmax_ref)
            mkmax_ref[...] = jnp.zeros_like(mkmax_ref)

        def square_sums(ref, g):
            v = ref[...].astype(F32)
            return _split_dot(v * v, g)

        xq = fox_ref[:, 0:512]
        fq_ref[...] = (xq * _group_rms(xq, gf_ref[...], gft_ref[...], if_ref[...]) * (gfq_ref[...] * FOX_SCALE)).astype(BF16)
        xk = fox_ref[:, 512:1024]
        fk_ref[...] = (xk * _group_rms(xk, gf_ref[...], gft_ref[...], if_ref[...]) * gfk_ref[...]).astype(BF16)
        fv_b = fox_ref[:, 1024:1536].astype(BF16)
        fv_ref[...] = fv_b
        fv1_ref[...] = (_dot(fv_b, vp_ref[...]) + vo_ref[...]).astype(BF16)
        fqn_ref[...] = square_sums(fq_ref, gf_ref[...])
        fkmax_ref[...] = jnp.maximum(fkmax_ref[...], jnp.max(square_sums(fk_ref, gf_ref[...]), axis=0, keepdims=True))


        logf = _log_sigmoid(lat_ref[:, LAT_F:LAT_F + N_HEADS] + bf_ref[...])
        cc_ref[...] = jnp.dot(_tri(tm, True), logf, precision=lax.Precision.HIGHEST, preferred_element_type=F32) + carry[...]
        carry[...] += _colsum(logf)

        qlat_n, _ = _rms_fwd(lat_ref[:, LAT_Q:LAT_Q + Q_LORA], gql_ref[...])
        p = _dot(qlat_n.astype(BF16), wq_ref[...], NT)
        y = p * _group_rms(p, gm_ref[...], gmt_ref[...], im_ref[...]) * (gq_ref[...] * MLA_SCALE)
        for h in range(N_HEADS):
            w = slice(h * LANES, (h + 1) * LANES)
            mq_ref[:, w] = _rope_fwd(y[:, w], ct, s1, s2).astype(BF16)

        kv_n, _ = _rms_fwd(lat_ref[:, LAT_KV:LAT_KV + KV_LORA], gkvl_ref[...])
        kv_b = kv_n.astype(BF16)
        pk = _dot(kv_b, wk_ref[...], NT)
        kn = pk * _group_rms(pk, gm_ref[...], gmt_ref[...], im_ref[...]) * gk_ref[...]
        rm = rm_ref[...]
        kr = pltpu.roll(lat_ref[:, LAT_KR:LAT_KR + LANES], HEAD_DIM, 1) * rm
        rr = lax.rsqrt(jnp.sum(kr * kr, axis=1, keepdims=True) * (1.0 / ROPE_DIM) + RMS_EPS)
        okr = _rope_fwd(kr * rr * gkr_ref[...], ct * rm, s1, s2)
        for h in range(N_HEADS):
            w = slice(h * LANES, (h + 1) * LANES)
            mk_ref[:, w] = (kn[:, w] + okr).astype(BF16)
        mv_ref[...] = _dot(kv_b, wv_ref[...], NT).astype(BF16)
        mv1_ref[...] = (_dot(kv_b, wvil_ref[...], NT).astype(BF16).astype(F32) + vo_ref[...]).astype(BF16)
        mqn_ref[...] = square_sums(mq_ref, gm_ref[...])
        mkmax_ref[...] = jnp.maximum(mkmax_ref[...], jnp.max(square_sums(mk_ref, gm_ref[...]), axis=0, keepdims=True))

    consts = [rope_mask, *mats, wq, wk, wv, g_qlat, g_kvlat, *gains, b_f, v_place, v_ones, wv_il]
    t512, tl, tg, g1 = _tok(tm, 512), _tok(tm, LANES), _tok(tm, N_GROUPS), _full((1, N_GROUPS))
    t1024 = _tok(tm, N_HEADS * LANES)
    return pl.pallas_call(
        body, name="prep_fwd", grid=(t // tm,),
        in_specs=[_tok(tm, 1536), _tok(tm, LAT_W), tl, tl, tl] + [_full(a.shape) for a in consts],
        out_specs=[t512, t512, t512, _tok(tm, N_HEADS), t1024, t1024, t512, tg, g1, tg, g1, t1024, t1024],
        out_shape=[_sds((t, 512), BF16), _sds((t, 512), BF16), _sds((t, 512), BF16), _sds((t, N_HEADS), F32),
                   _sds((t, N_HEADS * LANES), BF16), _sds((t, N_HEADS * LANES), BF16), _sds((t, 512), BF16),
                   _sds((t, N_GROUPS), F32), _sds((1, N_GROUPS), F32), _sds((t, N_GROUPS), F32), _sds((1, N_GROUPS), F32)]
        + [_sds((t, N_HEADS * LANES), BF16)] * 2,
        scratch_shapes=[pltpu.VMEM((1, N_HEADS), F32)],
        compiler_params=pltpu.CompilerParams(dimension_semantics=("arbitrary",)),
    )(fox, lat, c_tab, s1_tab, s2_tab, *consts)


def _prep_bwd(fox, lat, tables, mats, wq, wk, wv, g_qlat, g_kvlat, gains, b_f, dfq, dfk, dfv, dc, dmq, dmk, dmv):
    t = fox.shape[0]
    tm = min(PREP_TILE, t)
    n_t = t // tm
    c_tab, s1_tab, s2_tab, rope_mask = tables

    def body(fox_ref, lat_ref, c_ref, s1_ref, s2_ref, rm_ref, gm_ref, gmt_ref, im_ref, gf_ref, gft_ref, if_ref,
             wq_ref, wk_ref, wv_ref, gql_ref, gkvl_ref, gq_ref, gk_ref, gkr_ref, gfq_ref, gfk_ref, bf_ref,
             dfq_ref, dfk_ref, dfv_ref, dc_ref, dmq_ref, dmk_ref, dmv_ref,
             dfox_ref, dlat_ref, dwq_ref, dwk_ref, dwv_ref, o_gql, o_gkvl, o_gq, o_gk, o_gkr, o_gfq, o_gfk, o_bf,
             carry, lscr, wq_acc, wk_acc, wv_acc, dyscr):
        i = pl.program_id(0)
        ct, s1, s2, rm = c_ref[...], s1_ref[...], s2_ref[...], rm_ref[...]

        @pl.when(i == 0)
        def _():
            for ref in [carry, wq_acc, wk_acc, wv_acc, o_gql, o_gkvl, o_gq, o_gk, o_gkr, o_gfq, o_gfk, o_bf]:
                ref[...] = jnp.zeros_like(ref)

        def group_rms_bwd(dy, x, r, gain, g, g_t, inv):
            xn = x * r
            dyg = dy * gain
            mu = _split_dot(_split_dot(dyg * xn, g) * inv, g_t)
            return r * (dyg - xn * mu), dy * xn

        gf, gft, invf = gf_ref[...], gft_ref[...], if_ref[...]
        xq = fox_ref[:, 0:512]
        dxq, rows = group_rms_bwd(dfq_ref[...] * FOX_SCALE, xq, _group_rms(xq, gf, gft, invf), gfq_ref[...], gf, gft, invf)
        dfox_ref[:, 0:512] = dxq.astype(BF16)
        o_gfq[...] += _colsum(rows)
        xk = fox_ref[:, 512:1024]
        dxk, rows = group_rms_bwd(dfk_ref[...].T, xk, _group_rms(xk, gf, gft, invf), gfk_ref[...], gf, gft, invf)
        dfox_ref[:, 512:1024] = dxk.astype(BF16)
        o_gfk[...] += _colsum(rows)
        dfox_ref[:, 1024:1536] = dfv_ref[...].T.astype(BF16)

        lscr[...] = jnp.zeros_like(lscr)

        xql = lat_ref[:, LAT_Q:LAT_Q + Q_LORA]
        qlat_n, r_ql = _rms_fwd(xql, gql_ref[...])
        qlat_b = qlat_n.astype(BF16)
        gm, gmt, invm = gm_ref[...], gmt_ref[...], im_ref[...]
        p = _dot(qlat_b, wq_ref[...], NT)
        for h in range(N_HEADS):
            w = slice(h * LANES, (h + 1) * LANES)
            dyscr[:, w] = _rope_bwd(dmq_ref[:, w], ct, s1, s2) * MLA_SCALE
        dp, rows = group_rms_bwd(dyscr[...], p, _group_rms(p, gm, gmt, invm), gq_ref[...], gm, gmt, invm)
        o_gq[...] += _colsum(rows)
        dp_b = dp.astype(BF16)
        wq_acc[...] += _dot(dp_b, qlat_b, TN)
        dxql, rows = _rms_bwd(_dot(dp_b, wq_ref[...]), xql, gql_ref[...], r_ql)
        lscr[:, LAT_Q:LAT_Q + Q_LORA] = dxql
        o_gql[...] += _colsum(rows)

        xkv = lat_ref[:, LAT_KV:LAT_KV + KV_LORA]
        kv_n, r_kv = _rms_fwd(xkv, gkvl_ref[...])
        kv_b = kv_n.astype(BF16)
        pk = _dot(kv_b, wk_ref[...], NT)
        dmk = dmk_ref[...].T
        dpk, rows = group_rms_bwd(dmk, pk, _group_rms(pk, gm, gmt, invm), gk_ref[...], gm, gmt, invm)
        o_gk[...] += _colsum(rows)
        dpk_b = dpk.astype(BF16)
        dv_b = dmv_ref[...].T.astype(BF16)
        wk_acc[...] += _dot(dpk_b, kv_b, TN)
        wv_acc[...] += _dot(dv_b, kv_b, TN)
        dxkv, rows = _rms_bwd(_dot(dpk_b, wk_ref[...]) + _dot(dv_b, wv_ref[...]), xkv, gkvl_ref[...], r_kv)
        lscr[:, LAT_KV:LAT_KV + KV_LORA] = dxkv
        o_gkvl[...] += _colsum(rows)

        dokr = dmk[:, 0:LANES]
        for h in range(1, N_HEADS):
            dokr = dokr + dmk[:, h * LANES:(h + 1) * LANES]
        dykr = _rope_bwd(dokr * rm, ct * rm, s1, s2)
        kr = pltpu.roll(lat_ref[:, LAT_KR:LAT_KR + LANES], HEAD_DIM, 1) * rm
        rr = lax.rsqrt(jnp.sum(kr * kr, axis=1, keepdims=True) * (1.0 / ROPE_DIM) + RMS_EPS)
        krn = kr * rr
        dyg = dykr * gkr_ref[...]
        dkr = rr * (dyg - krn * (jnp.sum(dyg * krn, axis=1, keepdims=True) * (1.0 / ROPE_DIM)))
        o_gkr[...] += _colsum(dykr * krn)
        lscr[:, LAT_KR:LAT_KR + LANES] = pltpu.roll(dkr, LANES - HEAD_DIM, 1)

        dcv = dc_ref[...]
        dlogf = jnp.dot(_tri(tm, False), dcv, precision=lax.Precision.HIGHEST, preferred_element_type=F32) + carry[...]
        carry[...] += _colsum(dcv)
        dz = dlogf * jax.nn.sigmoid(-(lat_ref[:, LAT_F:LAT_F + N_HEADS] + bf_ref[...]))
        lscr[:, LAT_F:LAT_F + N_HEADS] = dz
        o_bf[...] += _colsum(dz)

        dlat_ref[...] = lscr[...].astype(BF16)

        @pl.when(i == n_t - 1)
        def _():
            dwq_ref[...] = wq_acc[...].astype(BF16)
            dwk_ref[...] = wk_acc[...].astype(BF16)
            dwv_ref[...] = wv_acc[...].astype(BF16)

    consts = [rope_mask, *mats, wq, wk, wv, g_qlat, g_kvlat, *gains, b_f]

    def rtok(n):
        return pl.BlockSpec((tm, n), lambda i: (n_t - 1 - i, 0))

    def rtok_t(n):
        return pl.BlockSpec((n, tm), lambda i: (0, n_t - 1 - i))

    sums =[(1, Q_LORA), (1, KV_LORA), (1, N_HEADS * LANES), (1, N_HEADS * LANES), (1, LANES), (1, 512), (1, 512), (1, N_HEADS)]
    return pl.pallas_call(
        body, name="prep_bwd", grid=(n_t,),
        in_specs=[rtok(1536), rtok(LAT_W), rtok(LANES), rtok(LANES), rtok(LANES)] + [_full(a.shape) for a in consts]
        + [rtok(512), rtok_t(512), rtok_t(512), rtok(N_HEADS), rtok(N_HEADS * LANES), rtok_t(N_HEADS * LANES), rtok_t(512)],
        out_specs=[rtok(1536), rtok(LAT_W), _full(wq.shape), _full(wk.shape), _full(wv.shape)] + [_full(s) for s in sums],
        out_shape=[_sds((t, 1536), BF16), _sds((t, LAT_W), BF16), _sds(wq.shape, BF16), _sds(wk.shape, BF16), _sds(wv.shape, BF16)]
        + [_sds(s, F32) for s in sums],
        scratch_shapes=[pltpu.VMEM((1, N_HEADS), F32), pltpu.VMEM((tm, LAT_W), F32), pltpu.VMEM(wq.shape, F32),
                        pltpu.VMEM(wk.shape, F32), pltpu.VMEM(wv.shape, F32), pltpu.VMEM((tm, N_HEADS * LANES), F32)],
        compiler_params=pltpu.CompilerParams(dimension_semantics=("arbitrary",)),
    )(fox, lat, c_tab, s1_tab, s2_tab, *consts, dfq, dfk, dfv, dc, dmq, dmk, dmv)


def _causal_pairs(n, query_major):
    pairs = [(i, j) for i in range(n) for j in range(i + 1)] if query_major else [(i, j) for j in range(n) for i in range(j, n)]
    return jnp.asarray([p[0] for p in pairs], jnp.int32), jnp.asarray([p[1] for p in pairs], jnp.int32)


def _logit_bound(q_sq, k_sq_max):
    qn = q_sq[:, :N_HEADS] + q_sq[:, N_HEADS:]
    kmax = k_sq_max[:, :N_HEADS] + k_sq_max[:, N_HEADS:]
    bound = jnp.sqrt(qn * kmax) * (1.0 + 2.0 ** -10) + 2.0 ** -10
    flag = (jnp.max(bound) <= FIXED_SHIFT_MAX_BOUND).astype(F32).reshape(1, 1)
    return bound, flag


def _attn_fwd(q, k, v, bound, fixed_ok, c, c_t, *, lanes, name, exchange=None):
    t = q.shape[0]
    tq = min(ATT_TILE, t)
    n_q = t // tq
    hd = HEAD_DIM
    ch = min(ATT_COL_CHUNK, tq)
    decay = c is not None

    def body(qi_ref, kj_ref, *refs):
        if decay:
            q_ref, k_ref, v_ref, b_ref, ok_ref, c_ref, ct_ref, o_ref, o32_ref, lse_ref, m_scr, acc = refs
        else:
            q_ref, k_ref, v_ref, b_ref, ok_ref, o_ref, o32_ref, lse_ref, m_scr, acc = refs
        i, j = qi_ref[pl.program_id(0)], kj_ref[pl.program_id(0)]
        fixed = ok_ref[0, 0] > 0.5

        @pl.when(j == 0)
        def _():
            m_scr[...] = jnp.full_like(m_scr, MASK_VALUE)
            acc[...] = jnp.zeros_like(acc)

        def fixed_step(diagonal):
            for h in range(N_HEADS):
                wl = slice(h * lanes, (h + 1) * lanes)
                wv = slice(h * LANES, (h + 1) * LANES)
                qh = q_ref[:, wl]
                row = (c_ref[:, h:h + 1] - b_ref[:, h:h + 1]) if decay else -b_ref[:, h:h + 1]
                o_hi = jnp.zeros((tq, LANES), F32)
                o_lo = jnp.zeros((tq, LANES), F32)
                for cc in range(tq // ch):
                    cols = slice(cc * ch, (cc + 1) * ch)
                    s = _dot(qh, k_ref[cols, wl], NT)
                    s = s + ((row - ct_ref[h:h + 1, cols]) if decay else row)
                    if diagonal:
                        keep = (lax.broadcasted_iota(jnp.int32, (tq, ch), 0)
                                >= lax.broadcasted_iota(jnp.int32, (tq, ch), 1) + cc * ch)
                        s = jnp.where(keep, s, MASK_VALUE)
                    p = jnp.exp(s)
                    p_b = p.astype(BF16)
                    o_hi = o_hi + _dot(p_b, v_ref[cols, wv])
                    if decay:
                        o_lo = o_lo + _dot((p - p_b.astype(F32)).astype(BF16), v_ref[cols, wv])
                acc[0, :, wv] += o_hi
                if decay:
                    acc[1, :, wv] += o_lo

        def step(diagonal):
            if diagonal:
                keep = lax.broadcasted_iota(jnp.int32, (tq, tq), 0) >= lax.broadcasted_iota(jnp.int32, (tq, tq), 1)
            for h in range(N_HEADS):
                s = _dot(q_ref[:, h * lanes:(h + 1) * lanes], k_ref[:, h * lanes:(h + 1) * lanes], NT)
                if decay:
                    s = s + (c_ref[:, h:h + 1] - ct_ref[h:h + 1, :])
                if diagonal:
                    s = jnp.where(keep, s, MASK_VALUE)
                m_prev = m_scr[h]
                m_new = jnp.maximum(m_prev, jnp.max(s, axis=1, keepdims=True))
                alpha = jnp.exp(m_prev - m_new)
                p = jnp.exp(s - m_new)
                wv = slice(h * LANES, (h + 1) * LANES)
                p_b = p.astype(BF16)
                acc[0, :, wv] = alpha * acc[0, :, wv] + _dot(p_b, v_ref[:, wv])
                if decay:
                    acc[1, :, wv] = alpha * acc[1, :, wv] + _dot((p - p_b.astype(F32)).astype(BF16), v_ref[:, wv])
                m_scr[h] = m_new

        for diagonal, here in ((False, j < i), (True, j == i)):
            @pl.when(here & fixed)
            def _():
                fixed_step(diagonal)

            @pl.when(here & jnp.logical_not(fixed))
            def _():
                step(diagonal)

        @pl.when(j == i)
        def _():
            for h in range(N_HEADS):
                w = slice(h * hd, (h + 1) * hd)
                val, one = slice(h * LANES, h * LANES + hd), slice(h * LANES + hd, h * LANES + hd + 1)
                l = (acc[0, :, one] + acc[1, :, one]) if decay else acc[0, :, one]
                o_ref[:, w] = (acc[0, :, val] / l).astype(BF16)
                o32_ref[:, w] = ((acc[0, :, val] + acc[1, :, val]) if decay else acc[0, :, val]) / l
                lse_ref[:, h:h + 1] = jnp.where(fixed, b_ref[:, h:h + 1], m_scr[h]) + jnp.log(l)

    qspec = lambda n: pl.BlockSpec((tq, n), lambda s, qi, kj: (qi[s], 0))
    kspec = lambda n: pl.BlockSpec((tq, n), lambda s, qi, kj: (kj[s], 0))
    in_specs = [qspec(N_HEADS * lanes), kspec(N_HEADS * lanes), kspec(N_HEADS * LANES), qspec(N_HEADS),
                pl.BlockSpec(memory_space=pltpu.SMEM)]
    args = [q, k, v, bound, fixed_ok]
    if decay:
        in_specs += [qspec(N_HEADS), pl.BlockSpec((N_HEADS, tq), lambda s, qi, kj: (0, kj[s]))]
        args += [c, c_t]
    pairs = _causal_pairs(n_q, query_major=True)
    return _gridded_call(
        body, name=name, grid=(pairs[0].shape[0],), in_specs=in_specs, out_specs=[qspec(512), qspec(512), qspec(N_HEADS)],
        out_shape=[_sds((t, 512), BF16), _sds((t, 512), F32), _sds((t, N_HEADS), F32)],
        scratch_shapes=[pltpu.VMEM((N_HEADS, tq, 1), F32), pltpu.VMEM((2, tq, N_HEADS * LANES), F32)],
        args=args, exchange=exchange, prefetch=pairs)


def _attn_bwd(q, k, v, do, delta, lse, c, c_t, *, lanes, name, exchange=None):
    t = q.shape[0]
    tq = min(ATT_TILE, t)
    n_q = t // tq
    hd = HEAD_DIM
    decay = c is not None

    pairs = _causal_pairs(n_q, query_major=False)
    n_steps = pairs[0].shape[0]

    def body(qi_ref, kj_ref, *refs):
        if decay:
            q_ref, k_ref, v_ref, do_ref, delta_ref, lse_ref, c_ref, ct_ref, dq_hbm, dk_ref, dv_ref, dct_ref, dq_ref = refs
        else:
            q_ref, k_ref, v_ref, do_ref, delta_ref, lse_ref, dq_hbm, dk_ref, dv_ref, dq_ref = refs
        step_id = pl.program_id(0)
        i, j = qi_ref[step_id], kj_ref[step_id]

        @pl.when(step_id == 0)
        def _():
            dq_ref[...] = jnp.zeros_like(dq_ref)

        @pl.when(i == j)
        def _():
            dk_ref[...] = jnp.zeros_like(dk_ref)
            dv_ref[...] = jnp.zeros_like(dv_ref)
            if decay:
                dct_ref[...] = jnp.zeros_like(dct_ref)

        def step(diagonal):
            if diagonal:
                keep = lax.broadcasted_iota(jnp.int32, (tq, tq), 0) >= lax.broadcasted_iota(jnp.int32, (tq, tq), 1)
            rows = pl.ds(pl.multiple_of(i * tq, tq), tq)
            for h in range(N_HEADS):
                wl = slice(h * lanes, (h + 1) * lanes)
                w = slice(h * hd, (h + 1) * hd)
                qh, kh = q_ref[:, wl], k_ref[:, wl]
                s = _dot(qh, kh, NT)
                if decay:
                    s = s + (c_ref[:, h:h + 1] - ct_ref[h:h + 1, :])
                if diagonal:
                    s = jnp.where(keep, s, MASK_VALUE)
                p = jnp.exp(s - lse_ref[:, h:h + 1])
                doh = do_ref[:, w]
                dv_ref[w, :] += _dot(doh, p.astype(BF16), TN)
                dp = _dot(doh, v_ref[:, w], NT)
                ds = p * (dp - delta_ref[:, h:h + 1])
                if decay:
                    dct_ref[h:h + 1, :] -= _colsum(ds)
                ds_b = ds.astype(BF16)
                dk_ref[wl, :] += _dot(qh, ds_b, TN)
                dq_ref[rows, wl] += _dot(ds_b, kh)

        @pl.when(i > j)
        def _():
            step(False)

        @pl.when(i == j)
        def _():
            step(True)

        @pl.when(step_id == n_steps - 1)
        def _():
            pltpu.sync_copy(dq_ref, dq_hbm)

    qspec = lambda n: pl.BlockSpec((tq, n), lambda s, qi, kj: (qi[s], 0))
    kspec = lambda n: pl.BlockSpec((tq, n), lambda s, qi, kj: (kj[s], 0))
    in_specs = [qspec(N_HEADS * lanes), kspec(N_HEADS * lanes), kspec(512), qspec(512), qspec(N_HEADS), qspec(N_HEADS)]
    kspec_t = lambda n: pl.BlockSpec((n, tq), lambda s, qi, kj: (0, kj[s]))
    out_specs = [pl.BlockSpec(memory_space=pl.ANY), kspec_t(N_HEADS * lanes), kspec_t(512)]
    out_shape = [_sds((t, N_HEADS * lanes), F32), _sds((N_HEADS * lanes, t), F32), _sds((512, t), F32)]
    args = [q, k, v, do, delta, lse]
    if decay:
        ctspec = pl.BlockSpec((N_HEADS, tq), lambda s, qi, kj: (0, kj[s]))
        in_specs += [qspec(N_HEADS), ctspec]
        out_specs.append(ctspec)
        out_shape.append(_sds((N_HEADS, t), F32))
        args += [c, c_t]
    return _gridded_call(body, name=name, grid=(n_steps,), in_specs=in_specs, out_specs=out_specs, out_shape=out_shape,
                         scratch_shapes=[pltpu.VMEM((t, N_HEADS * lanes), F32)], args=args, exchange=exchange, prefetch=pairs)


def _mix_fwd(x, y_mla, y_fox, gates, b_gate, wbm_t, wbf_t, wo):
    t = x.shape[0]
    tm = min(PROJ_TILE, t)

    def body(x_ref, ym_ref, yf_ref, gt_ref, bg_ref, wbm_ref, wbf_ref, wo_ref, out_ref):
        um = _dot(ym_ref[...], wbm_ref[...], NT)
        uf = _dot(yf_ref[...], wbf_ref[...], NT)
        sm = jax.nn.sigmoid(gt_ref[:, 0:D_MODEL] + bg_ref[0:1, :])
        sf = jax.nn.sigmoid(gt_ref[:, D_MODEL:2 * D_MODEL] + bg_ref[1:2, :])
        mixed = sm * um + sf * uf
        out_ref[...] = x_ref[...] + _dot(mixed.astype(BF16), wo_ref[...])

    tok = _tok(tm, D_MODEL)
    return pl.pallas_call(
        body, name="mix_fwd", grid=(t // tm,),
        in_specs=[tok, _tok(tm, 512), _tok(tm, 512), _tok(tm, 2 * D_MODEL), _full((2, D_MODEL)), _full(wbm_t.shape),
                  _full(wbf_t.shape), _full(wo.shape)],
        out_specs=tok, out_shape=_sds((t, D_MODEL), F32),
        compiler_params=pltpu.CompilerParams(dimension_semantics=("arbitrary",)),
    )(x, y_mla, y_fox, gates, b_gate, wbm_t, wbf_t, wo)


def _mix_bwd(dx, y_mla, y_fox, y_mla32, y_fox32, gates, b_gate, wbm_t, wbf_t, wo, head_groups, exchange=None):
    t = dx.shape[0]
    tm = min(PROJ_TILE, t)

    def body(dx_ref, ym_ref, yf_ref, ym32_ref, yf32_ref, gt_ref, bg_ref, wbm_ref, wbf_ref, wo_ref, gf_ref,
             dym_ref, dyf_ref, dlm_ref, dlf_ref, dgt_ref, mixed_ref, dum_ref, duf_ref, dxb_ref, dbg_ref):
        i = pl.program_id(0)
        dxb = dx_ref[...].astype(BF16)
        dxb_ref[...] = dxb
        dmixed = _dot(dxb, wo_ref[...], NT)
        um = _dot(ym_ref[...], wbm_ref[...], NT)
        uf = _dot(yf_ref[...], wbf_ref[...], NT)
        sm = jax.nn.sigmoid(gt_ref[:, 0:D_MODEL] + bg_ref[0:1, :])
        sf = jax.nn.sigmoid(gt_ref[:, D_MODEL:2 * D_MODEL] + bg_ref[1:2, :])
        mixed_ref[...] = (sm * um + sf * uf).astype(BF16)
        dum = (dmixed * sm).astype(BF16)
        duf = (dmixed * sf).astype(BF16)
        dum_ref[...] = dum
        duf_ref[...] = duf
        dgm = dmixed * um * (sm * (1.0 - sm))
        dgf = dmixed * uf * (sf * (1.0 - sf))
        dgt_ref[:, 0:D_MODEL] = dgm.astype(BF16)
        dgt_ref[:, D_MODEL:2 * D_MODEL] = dgf.astype(BF16)
        for du, wb_ref, y32_ref, dy_ref, dl_ref in ((dum, wbm_ref, ym32_ref, dym_ref, dlm_ref),
                                                    (duf, wbf_ref, yf32_ref, dyf_ref, dlf_ref)):
            dy = _dot(du, wb_ref[...])
            dy_ref[...] = dy.astype(BF16)
            prod = dy.astype(BF16).astype(F32) * y32_ref[...]
            hi = prod.astype(BF16)
            rest = prod - hi.astype(F32)
            mid = rest.astype(BF16)
            dl_ref[...] = _dot(hi, gf_ref[...]) + _dot(mid, gf_ref[...]) + _dot((rest - mid.astype(F32)).astype(BF16), gf_ref[...])

        @pl.when(i == 0)
        def _():
            dbg_ref[...] = jnp.zeros_like(dbg_ref)

        dbg_ref[0:1, :] += _colsum(dgm)
        dbg_ref[1:2, :] += _colsum(dgf)

    tok = _tok(tm, D_MODEL)
    tokb = _sds((t, D_MODEL), BF16)
    t512, tg = _tok(tm, 512), _tok(tm, N_GROUPS)
    return _gridded_call(
        body, name="mix_bwd", grid=(t // tm,),
        in_specs=[tok, t512, t512, t512, t512, _tok(tm, 2 * D_MODEL), _full((2, D_MODEL)), _full(wbm_t.shape),
                  _full(wbf_t.shape), _full(wo.shape), _full(head_groups.shape)],
        out_specs=[t512, t512, tg, tg, _tok(tm, 2 * D_MODEL), tok, tok, tok, tok, _full((2, D_MODEL))],
        out_shape=[_sds((t, 512), BF16), _sds((t, 512), BF16), _sds((t, N_GROUPS), F32), _sds((t, N_GROUPS), F32),
                   _sds((t, 2 * D_MODEL), BF16), tokb, tokb, tokb, tokb, _sds((2, D_MODEL), F32)],
        scratch_shapes=[], args=[dx, y_mla, y_fox, y_mla32, y_fox32, gates, b_gate, wbm_t, wbf_t, wo, head_groups],
        exchange=exchange)


def _my_position():
    x, y, c = lax.axis_index("x"), lax.axis_index("y"), lax.axis_index("c")
    return x, y, c, 4 * x + 2 * y + c


def _peer(x, y, c, mask):
    px = 1 - x if mask & 4 else x
    py = 1 - y if mask & 2 else y
    pc = 1 - c if mask & 1 else c
    return (px, py, pc), 4 * px + 2 * py + pc


def _chip_peer(x, y, km):
    px = 1 - x if km & 2 else x
    py = 1 - y if km & 1 else y
    return px, py, 2 * px + py


_HBM = pl.BlockSpec(memory_space=pl.ANY)


def _wait_all(copies):
    for cp in copies:
        cp.wait()


class _ChipExchange:
    def __init__(self, gather, arrays):
        self.gather, self.arrays, self.aliased = gather, list(arrays), False
        n = len(self.arrays)
        self.out_shape = [_sds((N_DEV * a.shape[0],) + a.shape[1:], a.dtype) if gather else _sds(a.shape, a.dtype)
                          for a in self.arrays]
        self.scratch_shapes = [pltpu.SemaphoreType.DMA((n, N_CHIP)), pltpu.SemaphoreType.DMA((n, N_CHIP)),
                               pltpu.SemaphoreType.DMA((n,))]

    def copies(self, srcs, dsts, send_sems, recv_sems, local_sems):
        x, y, c, me = _my_position()
        q_me = 2 * x + y
        out = []
        for a in range(len(self.arrays)):
            if self.gather:
                r = srcs[a].shape[0]
                local_src, dst = srcs[a], dsts[a].at[pl.ds(me * r, r)]
            else:
                local_src, dst = srcs[a].at[q_me], dsts[a].at[q_me]
            out.append(pltpu.make_async_copy(local_src, dst, local_sems.at[a]))
            for km in range(1, N_CHIP):
                px, py, q_peer = _chip_peer(x, y, km)
                out.append(pltpu.make_async_remote_copy(
                    src_ref=srcs[a] if self.gather else srcs[a].at[q_peer], dst_ref=dst, send_sem=send_sems.at[a, km],
                    recv_sem=recv_sems.at[a, km], device_id=(px, py, c), device_id_type=MESH))
        return out

    def standalone(self, name):
        n = len(self.arrays)

        def body(*refs):
            copies = self.copies(refs[:n], refs[n:2 * n], *refs[2 * n:])
            for cp in copies:
                cp.start()
            _wait_all(copies)

        return pl.pallas_call(body, name=name, in_specs=[_HBM] * n, out_specs=[_HBM] * n, out_shape=self.out_shape,
                              scratch_shapes=self.scratch_shapes)(*self.arrays)


def _gridded_call(body, *, name, grid, in_specs, out_specs, out_shape, scratch_shapes, args, exchange=None, prefetch=()):
    params = pltpu.CompilerParams(dimension_semantics=("arbitrary",) * len(grid))
    n_pre, n_in, n_out, n_scr = len(prefetch), len(in_specs), len(out_specs), len(scratch_shapes)
    n_x = 0 if exchange is None else len(exchange.arrays)

    def carrier(*refs):
        pre, refs = refs[:n_pre], refs[n_pre:]
        ins, x_src, refs = refs[:n_in], refs[n_in:n_in + n_x], refs[n_in + n_x:]
        outs, x_dst, refs = refs[:n_out], refs[n_out:n_out + n_x], refs[n_out + n_x:]
        copies = exchange.copies(x_src, x_dst, *refs[n_scr:])
        pids = [pl.program_id(d) for d in range(len(grid))]
        first = functools.reduce(jnp.logical_and, [p == 0 for p in pids])
        last = functools.reduce(jnp.logical_and, [p == g - 1 for p, g in zip(pids, grid)])

        @pl.when(first)
        def _():
            for cp in copies:
                cp.start()

        body(*pre, *ins, *outs, *refs[:n_scr])

        @pl.when(last)
        def _():
            _wait_all(copies)

    x_shapes, x_scratch, x_args = ([], [], []) if exchange is None else (exchange.out_shape, exchange.scratch_shapes, exchange.arrays)
    spec = pltpu.PrefetchScalarGridSpec(
        num_scalar_prefetch=n_pre, grid=grid, in_specs=list(in_specs) + [_HBM] * n_x, out_specs=list(out_specs) + [_HBM] * n_x,
        scratch_shapes=list(scratch_shapes) + x_scratch)
    in_place = {n_pre + n_in + k: n_out + k for k in range(n_x)} if n_x and exchange.aliased else {}
    res = pl.pallas_call(body if exchange is None else carrier, name=name, grid_spec=spec, out_shape=list(out_shape) + x_shapes,
                         input_output_aliases=in_place, compiler_params=params)(*prefetch, *args, *x_args)
    return res[:n_out], (None if exchange is None else res[n_out:])


class _CoreExchange:
    def __init__(self, gather, arrays):
        self.gather, self.arrays, self.aliased = gather, list(arrays), gather
        n = len(self.arrays)
        self.out_shape = [_sds(a.shape, a.dtype) if gather else _sds((N_CHIP, a.shape[0] // N_DEV) + a.shape[1:], a.dtype)
                          for a in self.arrays]
        self.scratch_shapes = [pltpu.SemaphoreType.DMA((n, N_CHIP)), pltpu.SemaphoreType.DMA((n, N_CHIP))]

    def copies(self, srcs, dsts, send_sems, recv_sems):
        x, y, c, _ = _my_position()
        out = []
        for a in range(len(self.arrays)):
            r = srcs[a].shape[0] // N_DEV
            for q in range(N_CHIP):
                if self.gather:
                    rows = pl.ds((2 * q + c) * r, r)
                    src, dst = srcs[a].at[rows], dsts[a].at[rows]
                else:
                    src, dst = srcs[a].at[pl.ds((2 * q + 1 - c) * r, r)], dsts[a].at[q]
                out.append(pltpu.make_async_remote_copy(src_ref=src, dst_ref=dst, send_sem=send_sems.at[a, q],
                                                        recv_sem=recv_sems.at[a, q], device_id=(x, y, 1 - c), device_id_type=MESH))
        return out

    def standalone(self, name):
        n = len(self.arrays)

        def body(*refs):
            copies = self.copies(refs[:n], refs[n:2 * n], *refs[2 * n:])
            for cp in copies:
                cp.start()
            _wait_all(copies)

        return pl.pallas_call(body, name=name, in_specs=[_HBM] * n, out_specs=[_HBM] * n, out_shape=self.out_shape,
                              input_output_aliases={a: a for a in range(n)} if self.aliased else {},
                              scratch_shapes=self.scratch_shapes)(*self.arrays)


def _gather_over_cores(arrays, name):
    return _CoreExchange(True, arrays).standalone(name)


def _gather_both_stages(pieces, name):
    n_arr = len(pieces)

    def body(*refs):
        srcs, dsts = refs[:n_arr], refs[n_arr:2 * n_arr]
        send_sems, recv_sems, local_sems = refs[2 * n_arr:]
        x, y, c, me = _my_position()
        sibling = (x, y, 1 - c)

        def copy(a, k, piece, to, from_src=False):
            r = srcs[a].shape[0]
            rows = dsts[a].at[pl.ds(piece * r, r)]
            return pltpu.make_async_remote_copy(src_ref=srcs[a] if from_src else rows, dst_ref=rows, send_sem=send_sems.at[a, k],
                                                recv_sem=recv_sems.at[a, k], device_id=to, device_id_type=MESH)

        local, sent = [], []
        for a in range(n_arr):
            r = srcs[a].shape[0]
            local.append(pltpu.make_async_copy(srcs[a], dsts[a].at[pl.ds(me * r, r)], local_sems.at[a]))
            sent.append(copy(a, 0, me, sibling, from_src=True))
            for km in range(1, N_CHIP):
                px, py, _ = _chip_peer(x, y, km)
                sent.append(copy(a, km, me, (px, py, c), from_src=True))
        for cp in local + sent:
            cp.start()
        for a in range(n_arr):
            for km in range(1, N_CHIP):
                px, py, q = _chip_peer(x, y, km)
                copy(a, km, 2 * q + c, (px, py, c)).wait_recv()
                passed = copy(a, N_CHIP - 1 + km, 2 * q + c, sibling)
                passed.start()
                sent.append(passed)
        for a in range(n_arr):
            copy(a, 0, 4 * x + 2 * y + 1 - c, sibling).wait_recv()
            for km in range(1, N_CHIP):
                _, _, q = _chip_peer(x, y, km)
                copy(a, N_CHIP - 1 + km, 2 * q + 1 - c, sibling).wait_recv()
        for cp in sent:
            cp.wait_send()
        for cp in local:
            cp.wait()

    n_sem = 2 * N_CHIP - 1
    return pl.pallas_call(
        body, name=name, in_specs=[_HBM] * n_arr, out_specs=[_HBM] * n_arr,
        out_shape=[_sds((N_DEV * p.shape[0],) + p.shape[1:], p.dtype) for p in pieces],
        scratch_shapes=[pltpu.SemaphoreType.DMA((n_arr, n_sem)), pltpu.SemaphoreType.DMA((n_arr, n_sem)),
                        pltpu.SemaphoreType.DMA((n_arr,))],
    )(*pieces)


def _grads_to_sibling(grads, name):
    return _CoreExchange(False, grads).standalone(name)


def _pair_sum(grad, from_sibling, name):
    r, n = from_sibling.shape[1:]

    def body(g_ref, s_ref, o_ref):
        c = lax.axis_index("c")
        o_ref[...] = (g_ref[c].astype(F32) + s_ref[...].astype(F32)).astype(BF16)

    return pl.pallas_call(
        body, name=name, grid=(N_CHIP,),
        in_specs=[pl.BlockSpec((None, 2, r, n), lambda q: (q, 0, 0, 0)), pl.BlockSpec((None, r, n), lambda q: (q, 0, 0))],
        out_specs=pl.BlockSpec((None, r, n), lambda q: (q, 0, 0)), out_shape=_sds((N_CHIP, r, n), BF16),
    )(grad.reshape(N_CHIP, 2, r, n), from_sibling)


def _all_reduce_small(vec):
    r = vec.shape[0]

    def body(v_ref, o_ref, buf, send_sems, recv_sems):
        x, y, c, me = _my_position()
        buf[me] = v_ref[...]
        copies = []
        for mask in range(1, N_DEV):
            peer, _ = _peer(x, y, c, mask)
            cp = pltpu.make_async_remote_copy(src_ref=v_ref, dst_ref=buf.at[me], send_sem=send_sems.at[mask],
                                              recv_sem=recv_sems.at[mask], device_id=peer, device_id_type=MESH)
            cp.start()
            copies.append(cp)
        for cp in copies:
            cp.wait()
        total = buf[0]
        for s in range(1, N_DEV):
            total = total + buf[s]
        o_ref[...] = total

    vm = pl.BlockSpec(memory_space=pltpu.VMEM)
    return pl.pallas_call(
        body, name="all_reduce_small", in_specs=[vm], out_specs=vm, out_shape=_sds(vec.shape, F32),
        scratch_shapes=[pltpu.VMEM((N_DEV, r, 128), F32), pltpu.SemaphoreType.DMA((N_DEV,)), pltpu.SemaphoreType.DMA((N_DEV,))],
    )(vec)


def _adamw_math(w, g, m, v):
    m = ADAM_B1 * m + (1.0 - ADAM_B1) * g
    v = ADAM_B2 * v + (1.0 - ADAM_B2) * (g * g)
    m_hat = m / (1.0 - ADAM_B1 ** ADAM_STEP)
    v_hat = v / (1.0 - ADAM_B2 ** ADAM_STEP)
    delta = -ADAM_LR * (m_hat / (jnp.sqrt(v_hat) + ADAM_EPS) + ADAM_WD * w)
    return delta, m, v


def _reduce_adamw(slots, w, m, v, *, transpose, name):
    r, n = slots.shape[1:]
    padded = not transpose and w.shape[0] != r

    def body(s_ref, w_ref, m_ref, v_ref, g_ref, d_ref, nm_ref, nv_ref, *scr):
        g = s_ref[0].astype(F32)
        for s in range(1, slots.shape[0]):
            g = g + s_ref[s].astype(F32)
        if transpose:
            scr[0][...] = g.T
            g = scr[0][:, 0:w_ref.shape[1]]
        elif padded:
            scr[0][...] = g
            g = scr[0][0:w_ref.shape[0], :]
        g_ref[...] = g
        d_ref[...], nm_ref[...], nv_ref[...] = _adamw_math(w_ref[...], g, m_ref[...], v_ref[...])

    out = _sds(w.shape, F32)
    if not transpose and not padded and r % (2 * BF16_ROW_TILE) == 0:
        half = pl.BlockSpec((r // 2, n), lambda i: (i, 0))
        return pl.pallas_call(
            body, name=name, grid=(2,), in_specs=[pl.BlockSpec((slots.shape[0], r // 2, n), lambda i: (0, i, 0)), half, half, half],
            out_specs=[half] * 4, out_shape=[out, out, out, out])(slots, w, m, v)
    return pl.pallas_call(
        body, name=name, out_shape=[out, out, out, out],
        scratch_shapes=[pltpu.VMEM((n, r) if transpose else (r, n), F32)] if transpose or padded else [],
    )(slots, w, m, v)


def _adamw(g, w, m, v, name):
    def body(g_ref, w_ref, m_ref, v_ref, d_ref, nm_ref, nv_ref):
        d_ref[...], nm_ref[...], nv_ref[...] = _adamw_math(w_ref[...], g_ref[...], m_ref[...], v_ref[...])

    out = _sds(w.shape, F32)
    return pl.pallas_call(body, name=name, out_shape=[out, out, out])(g, w, m, v)


_SMALL = ["ffn1_norm", "mix_norm", "ffn2_norm", "mla_q_lat_norm", "mla_kv_lat_norm", "mla_q_nope_gain", "mla_q_rope_gain",
          "mla_k_nope_gain", "mla_k_rope_gain", "fox_q_gain", "fox_k_gain", "fox_b_f"]
_WEIGHTS = ["ffn1_norm", "ffn1_w_gate", "ffn1_w_up", "ffn1_w_down", "mix_norm", "w_in", "mla_q_lat_norm", "mla_w_qb",
            "mla_kv_lat_norm", "mla_w_kvb", "mla_q_nope_gain", "mla_q_rope_gain", "mla_k_nope_gain", "mla_k_rope_gain",
            "fox_q_gain", "fox_k_gain", "fox_b_f", "w_branch_mla", "w_branch_fox", "b_gate", "w_o", "ffn2_norm",
            "ffn2_w_gate", "ffn2_w_up", "ffn2_w_down"]
_IN_Q, _IN_KV, _IN_KR, _IN_FOX, _IN_F, _IN_GATES = (0, 192), (192, 128), (320, 32), (352, 1536), (1888, 8), (1896, 2048)


def _rows(a, seg):
    return a[seg[0]:seg[0] + seg[1]]


def _split_w_in(win_t):
    z = lambda n: jnp.zeros((n, D_MODEL), win_t.dtype)
    lat = jnp.concatenate([_rows(win_t, _IN_Q), z(LAT_KV - Q_LORA), _rows(win_t, _IN_KV), _rows(win_t, _IN_KR),
                           _rows(win_t, _IN_F), z(LAT_W - LAT_F - N_HEADS)], axis=0)
    return _rows(win_t, _IN_GATES), _rows(win_t, _IN_FOX), lat


def _join_w_in(d_gates, d_fox, d_lat):
    return jnp.concatenate([d_lat[LAT_Q:LAT_Q + Q_LORA], d_lat[LAT_KV:LAT_KV + KV_LORA], d_lat[LAT_KR:LAT_KR + ROPE_DIM],
                            d_fox, d_lat[LAT_F:LAT_F + N_HEADS], d_gates], axis=0)


def kernel(x, positions, ffn1_norm, ffn1_w_gate, ffn1_w_up, ffn1_w_down, mix_norm, w_in, mla_q_lat_norm, mla_w_qb, mla_kv_lat_norm, mla_w_kvb, mla_q_nope_gain, mla_q_rope_gain, mla_k_nope_gain, mla_k_rope_gain, fox_q_gain, fox_k_gain, fox_b_f, w_branch_mla, w_branch_fox, b_gate, w_o, ffn2_norm, ffn2_w_gate, ffn2_w_up, ffn2_w_down, loss_target, m_ffn1_norm, m_ffn1_w_gate, m_ffn1_w_up, m_ffn1_w_down, m_mix_norm, m_w_in, m_mla_q_lat_norm, m_mla_w_qb, m_mla_kv_lat_norm, m_mla_w_kvb, m_mla_q_nope_gain, m_mla_q_rope_gain, m_mla_k_nope_gain, m_mla_k_rope_gain, m_fox_q_gain, m_fox_k_gain, m_fox_b_f, m_w_branch_mla, m_w_branch_fox, m_b_gate, m_w_o, m_ffn2_norm, m_ffn2_w_gate, m_ffn2_w_up, m_ffn2_w_down, v_ffn1_norm, v_ffn1_w_gate, v_ffn1_w_up, v_ffn1_w_down, v_mix_norm, v_w_in, v_mla_q_lat_norm, v_mla_w_qb, v_mla_kv_lat_norm, v_mla_w_kvb, v_mla_q_nope_gain, v_mla_q_rope_gain, v_mla_k_nope_gain, v_mla_k_rope_gain, v_fox_q_gain, v_fox_k_gain, v_fox_b_f, v_w_branch_mla, v_w_branch_fox, v_b_gate, v_w_o, v_ffn2_norm, v_ffn2_w_gate, v_ffn2_w_up, v_ffn2_w_down):
    env = dict(locals())
    strip = lambda n, a: a if n in _SMALL else a[0]
    W = {n: strip(n, env[n]) for n in _WEIGHTS}
    M = {n: strip(n, env["m_" + n]) for n in _WEIGHTS}
    V = {n: strip(n, env["v_" + n]) for n in _WEIGHTS}
    xs = x[0]
    t = xs.shape[0]

    col_split = ["ffn1_w_gate", "ffn1_w_up", "ffn2_w_gate", "ffn2_w_up", "mla_w_qb", "mla_w_kvb", "w_branch_mla", "w_branch_fox"]
    row_split = ["ffn1_w_down", "ffn2_w_down", "w_o"]
    pieces = {n: W[n].T.astype(BF16) for n in col_split}
    pieces.update({n: W[n].astype(BF16) for n in row_split})
    pieces["w_in"] = jnp.pad(W["w_in"].T.astype(BF16), ((0, W_IN_PIECE_PAD - W_IN_PIECE), (0, 0)))
    pieces["b_gate"] = W["b_gate"].T
    group_a = ["ffn1_w_gate", "ffn1_w_up", "ffn1_w_down"]
    group_b = ["w_in", "mla_w_qb", "mla_w_kvb", "w_branch_mla", "w_branch_fox", "w_o", "b_gate"]
    group_c = ["ffn2_w_gate", "ffn2_w_up", "ffn2_w_down"]
    gather = lambda group: _ChipExchange(True, [pieces[n] for n in group])
    G = dict(zip(group_a, _gather_both_stages([pieces[n] for n in group_a], "gather_a")))

    inv_freq = ROPE_THETA ** (-jnp.arange(ROPE_HALF, dtype=F32) / ROPE_HALF)
    ang = positions[0].astype(F32)[:, None] * inv_freq
    tables, mats = _prep_tables(jnp.cos(ang), jnp.sin(ang)), _group_matrices()
    gains = _head_gains(*[W[n] for n in ["mla_q_nope_gain", "mla_q_rope_gain", "mla_k_nope_gain", "mla_k_rope_gain",
                                         "fox_q_gain", "fox_k_gain"]])

    (x1, a1, b1), got_b = _ffn_fwd(xs, W["ffn1_norm"], G["ffn1_w_gate"], G["ffn1_w_up"], G["ffn1_w_down"],
                                   exchange=gather(group_b))
    G.update(zip(group_b, _gather_over_cores(got_b, "gather_d2d_b")))
    win_t = G["w_in"].reshape(N_DEV, W_IN_PIECE_PAD, D_MODEL)[:, :W_IN_PIECE].reshape(N_DEV * W_IN_PIECE, D_MODEL)
    wgate_t, wfox_t, wlat_t = _split_w_in(win_t)
    bg = G["b_gate"].T
    gates, fox, lat = _proj_fwd(x1, W["mix_norm"], wgate_t, wfox_t, wlat_t)
    prep_args = (fox, lat, tables, mats, *_interleave_weights(G["mla_w_qb"], G["mla_w_kvb"]), W["mla_q_lat_norm"],
                 W["mla_kv_lat_norm"], gains, W["fox_b_f"])
    fq, fk, fv, c, mq, mk, mv, fq_sq, fk_sq_max, mq_sq, mk_sq_max, fv1, mv1 = _prep_fwd(*prep_args, *_value_layout(prep_args[6]))
    c_t = c.T
    b_fox, ok_fox = _logit_bound(fq_sq, fk_sq_max)
    b_mla, ok_mla = _logit_bound(mq_sq, mk_sq_max)
    (y_fox, y_fox32, lse_fox), got_c = _attn_fwd(fq, fk, fv1, b_fox, ok_fox, c, c_t, lanes=HEAD_DIM, name="fox_fwd",
                                                 exchange=gather(group_c))
    (y_mla, y_mla32, lse_mla), got_c = _attn_fwd(mq, mk, mv1, b_mla, ok_mla, None, None, lanes=MLA_QK_LANES, name="mla_fwd",
                                                 exchange=_CoreExchange(True, got_c))
    G.update(zip(group_c, got_c))
    x2 = _mix_fwd(x1, y_mla, y_fox, gates, bg, G["w_branch_mla"], G["w_branch_fox"], G["w_o"])
    (dx3, a2, b2, loss_vec), _ = _ffn_fwd(x2, W["ffn2_norm"], G["ffn2_w_gate"], G["ffn2_w_up"], G["ffn2_w_down"],
                                          target=loss_target[0])

    def chip_sums(group, tag, from_sibling=None):
        if from_sibling is None:
            from_sibling = _grads_to_sibling([grads[n] for n in group], "grads_d2d_" + tag)
        return _ChipExchange(False, [_pair_sum(grads[n], s, "pair_sum_" + n) for n, s in zip(group, from_sibling)])

    (dx2, dg_ffn2, da2, db2, h2, n2, dyh2), _ = _ffn_bwd(dx3, x2, W["ffn2_norm"], a2, b2, G["ffn2_w_gate"], G["ffn2_w_up"],
                                                        G["ffn2_w_down"], "ffn2_bwd")
    grads = {"ffn2_w_gate": _tn_matmul(da2, n2, "ffn2_dgate"), "ffn2_w_up": _tn_matmul(db2, n2, "ffn2_dup"),
             "ffn2_w_down": _tn_matmul(h2, dyh2, "ffn2_ddown")}
    (dy_mla, dy_fox, delta_mla, delta_fox, dgates, mixed, dum, duf, dx2b, dbg), from_sibling_c = _mix_bwd(
        dx2, y_mla, y_fox, y_mla32, y_fox32, gates, bg, G["w_branch_mla"], G["w_branch_fox"], G["w_o"], mats[3],
        exchange=_CoreExchange(False, [grads[n] for n in group_c]))
    delta_mla, delta_fox = delta_mla[:, :N_HEADS], delta_fox[:, :N_HEADS]
    grads["w_o"] = _tn_matmul(mixed, dx2b, "d_w_o")
    grads["w_branch_mla"] = _tn_matmul(dum, y_mla, "d_w_branch_mla")
    grads["w_branch_fox"] = _tn_matmul(duf, y_fox, "d_w_branch_fox")
    (dfq, dfk, dfv, dc_t), slots_c = _attn_bwd(fq, fk, fv, dy_fox, delta_fox, lse_fox, c, c_t, lanes=HEAD_DIM,
                                               name="fox_bwd", exchange=chip_sums(group_c, "c", from_sibling_c))
    slots = dict(zip(group_c, slots_c))
    (dmq, dmk, dmv), _ = _attn_bwd(mq, mk, mv, dy_mla, delta_mla, lse_mla, None, None, lanes=MLA_QK_LANES, name="mla_bwd")
    dfox, dlat, dwq, dwk, dwv, d_gql, d_gkvl, d_gq, d_gk, d_gkr, d_gfq, d_gfk, d_bf = _prep_bwd(
        *prep_args, dfq, dfk, dfv, dc_t.T, dmq, dmk, dmv)
    grads["mla_w_qb"], grads["mla_w_kvb"] = _deinterleave_grads(dwq, dwk, dwv)
    fold = lambda a, width: a.reshape(N_HEADS, width).sum(axis=0)[None]
    d_gq, d_gk = fold(d_gq, LANES), fold(d_gk, LANES)
    d_prep_small = [d_gql, d_gkvl, d_gq[:, :HEAD_DIM], d_gq[:, HEAD_DIM:HEAD_DIM + ROPE_DIM], d_gk[:, :HEAD_DIM],
                    d_gkr[:, HEAD_DIM:HEAD_DIM + ROPE_DIM], fold(d_gfq, HEAD_DIM), fold(d_gfk, HEAD_DIM), d_bf]
    dx1, dg_mix, nmix = _proj_bwd(dgates, dfox, dlat, wgate_t, wfox_t, wlat_t, x1, W["mix_norm"], dx2)
    dwin_t = _join_w_in(_tn_matmul(dgates, nmix, "d_w_in_gates"), _tn_matmul(dfox, nmix, "d_w_in_fox"),
                        _tn_matmul(dlat, nmix, "d_w_in_lat"))
    grads["w_in"] = jnp.pad(dwin_t.reshape(N_DEV, W_IN_PIECE, D_MODEL), ((0, 0), (0, W_IN_PIECE_PAD - W_IN_PIECE), (0, 0))
                            ).reshape(N_DEV * W_IN_PIECE_PAD, D_MODEL)
    grad_group_b = [n for n in group_b if n != "b_gate"]
    (dx0, dg_ffn1, da1, db1, h1, n1, dyh1), slots_b = _ffn_bwd(dx1, xs, W["ffn1_norm"], a1, b1, G["ffn1_w_gate"], G["ffn1_w_up"],
                                                              G["ffn1_w_down"], "ffn1_bwd", exchange=chip_sums(grad_group_b, "b"))
    slots.update(zip(grad_group_b, slots_b))
    grads["ffn1_w_gate"] = _tn_matmul(da1, n1, "ffn1_dgate")
    grads["ffn1_w_up"], got = _tn_matmul(db1, n1, "ffn1_dup", exchange=chip_sums(["ffn1_w_gate"], "a_gate"))
    slots["ffn1_w_gate"] = got[0]
    grads["ffn1_w_down"], got = _tn_matmul(h1, dyh1, "ffn1_ddown", exchange=chip_sums(["ffn1_w_up"], "a_up"))
    slots["ffn1_w_up"] = got[0]
    slots["ffn1_w_down"] = chip_sums(["ffn1_w_down"], "a_down").standalone("grads_ici_a_down")[0]

    small_parts = [dg_ffn1, dg_mix, dg_ffn2] + list(d_prep_small) + [dbg.reshape(1, 2 * D_MODEL), loss_vec]
    flat = jnp.concatenate([p.reshape(-1) for p in small_parts])
    n_flat = flat.shape[0]
    rows = -(-n_flat // (8 * 128)) * 8
    total = _all_reduce_small(jnp.pad(flat, (0, rows * 128 - n_flat)).reshape(rows, 128)).reshape(-1)
    offs, small_g = 0, {}
    for n in _SMALL:
        small_g[n] = total[offs:offs + W[n].shape[1]].reshape(W[n].shape)
        offs += W[n].shape[1]
    bg_full = total[offs:offs + 2 * D_MODEL].reshape(2, D_MODEL)
    offs += 2 * D_MODEL
    loss = (0.5 / D_MODEL) * jnp.sum(total[offs:offs + D_MODEL])
    _, _, _, me = _my_position()
    small_g["b_gate"] = lax.dynamic_slice_in_dim(bg_full, me * (D_MODEL // N_DEV), D_MODEL // N_DEV, axis=1)

    transposed_in_memory = ["ffn1_w_gate", "ffn1_w_up", "ffn2_w_gate", "ffn2_w_up", "w_in"]
    res = {}
    for n in _WEIGHTS:
        if n in small_g:
            res[n] = (small_g[n],) + tuple(_adamw(small_g[n], W[n], M[n], V[n], "adamw_" + n))
        elif n in transposed_in_memory:
            res[n] = tuple(o.T for o in _reduce_adamw(slots[n], W[n].T, M[n].T, V[n].T, transpose=False, name="adamw_" + n))
        else:
            res[n] = tuple(_reduce_adamw(slots[n], W[n], M[n], V[n], transpose=n in col_split, name="adamw_" + n))
    outs = [loss, dx0[None]]
    for k in range(4):
        outs += [res[n][k] if n in _SMALL else res[n][k][None] for n in _WEIGHTS]
    return tuple(outs)
```

```python
import functools

import jax
import jax.numpy as jnp
from jax import lax
from jax.experimental import pallas as pl
from jax.experimental.pallas import tpu as pltpu

F32 = jnp.float32
BF16 = jnp.bfloat16

D_MODEL = 1024
FFN_HIDDEN = 2816
N_HEADS = 8
HEAD_DIM = 64
ROPE_DIM = 32
ROPE_HALF = 16
Q_LORA = 192
KV_LORA = 128
ROPE_THETA = 10000.0
RMS_EPS = 1e-6
MLA_SCALE = (HEAD_DIM + ROPE_DIM) ** -0.5
FOX_SCALE = HEAD_DIM ** -0.5
MLA_QK_LANES = 128
ADAM_LR, ADAM_B1, ADAM_B2, ADAM_EPS, ADAM_WD, ADAM_STEP = 0.001, 0.9, 0.999, 1e-08, 0.01, 10
N_DEV = 8
N_CHIP = 4
W_IN_PIECE = 493
BF16_ROW_TILE = 16
W_IN_PIECE_PAD = 496
LAT_W = 512
LAT_Q, LAT_KV, LAT_KR, LAT_F = 0, 256, 384, 416
MASK_VALUE = -1e30
FIXED_SHIFT_MAX_BOUND = 30.0

TOK_TILE = 512
DW_TOK_TILE = 1024
DW_SPLIT_ROWS = 2048
PROJ_TILE = 512
PREP_TILE = 256
ATT_TILE = 512
ATT_COL_CHUNK = 256
FFN_HID_TILE = 1408
FFN_HID_SPLIT = ((0, 768), (768, 1408))

NT = (((1,), (1,)), ((), ()))
TN = (((0,), (0,)), ((), ()))
NN = (((1,), (0,)), ((), ()))
MESH = pl.DeviceIdType.MESH


def _dot(a, b, dims=NN):
    return lax.dot_general(a, b, dims, preferred_element_type=F32)


def _sds(shape, dtype):
    return jax.ShapeDtypeStruct(shape, dtype)


def _rms_fwd(x, g):
    r = lax.rsqrt(jnp.mean(x * x, axis=-1, keepdims=True) + RMS_EPS)
    return x * r * g, r


def _rms_bwd(dy, x, g, r):
    xn = x * r
    dyg = dy * g
    dx = r * (dyg - xn * jnp.mean(dyg * xn, axis=-1, keepdims=True))
    return dx, dy * xn


def _colsum(x):
    return jnp.sum(x, axis=0, keepdims=True)


def _full(shape):
    return pl.BlockSpec(shape, lambda *_: (0,) * len(shape))


def _tok(tm, n):
    return pl.BlockSpec((tm, n), lambda i, *_: (i, 0))


def _ffn_fwd(x, gain, wg_t, wu_t, wd, target=None, exchange=None):
    t = x.shape[0]
    tm = min(TOK_TILE, t)
    tf = FFN_HID_TILE
    n_t, n_f = t // tm, FFN_HIDDEN // tf
    with_loss = target is not None

    def body(*refs):
        if with_loss:
            x_ref, g_ref, wg_ref, wu_ref, wd_ref, t_ref, out_ref, a_ref, b_ref, lvec_ref, n_scr, acc = refs
        else:
            x_ref, g_ref, wg_ref, wu_ref, wd_ref, out_ref, a_ref, b_ref, n_scr, acc = refs
        i, j = pl.program_id(0), pl.program_id(1)

        @pl.when(j == 0)
        def _():
            xn, _ = _rms_fwd(x_ref[...], g_ref[...])
            n_scr[...] = xn.astype(BF16)
            acc[...] = jnp.zeros_like(acc)

        n = n_scr[...]
        a = _dot(n, wg_ref[...], NT)
        b = _dot(n, wu_ref[...], NT)
        a_ref[...] = a.astype(BF16)
        b_ref[...] = b.astype(BF16)
        h = (a * jax.nn.sigmoid(a)) * b
        acc[...] += _dot(h.astype(BF16), wd_ref[...])

        @pl.when(j == n_f - 1)
        def _():
            y = x_ref[...] + 0.5 * acc[...]
            if with_loss:
                diff = y - t_ref[...]
                out_ref[...] = diff * (1.0 / D_MODEL)
                sq = _colsum(diff * diff)

                @pl.when(i == 0)
                def _():
                    lvec_ref[...] = sq

                @pl.when(i > 0)
                def _():
                    lvec_ref[...] += sq
            else:
                out_ref[...] = y

    wspec = pl.BlockSpec((tf, D_MODEL), lambda i, j: (j, 0))
    hspec = pl.BlockSpec((tm, tf), lambda i, j: (i, j))
    in_specs = [_tok(tm, D_MODEL), _full((1, D_MODEL)), wspec, wspec, wspec]
    out_specs = [_tok(tm, D_MODEL), hspec, hspec]
    out_shape = [_sds((t, D_MODEL), F32), _sds((t, FFN_HIDDEN), BF16), _sds((t, FFN_HIDDEN), BF16)]
    args = [x, gain, wg_t, wu_t, wd]
    if with_loss:
        in_specs.append(_tok(tm, D_MODEL))
        out_specs.append(_full((1, D_MODEL)))
        out_shape.append(_sds((1, D_MODEL), F32))
        args.append(target)
    return _gridded_call(
        body, name="ffn_fwd_loss" if with_loss else "ffn_fwd", grid=(n_t, n_f), in_specs=in_specs, out_specs=out_specs,
        out_shape=out_shape, scratch_shapes=[pltpu.VMEM((tm, D_MODEL), BF16), pltpu.VMEM((tm, D_MODEL), F32)],
        args=args, exchange=exchange)


def _ffn_bwd(dy, x, gain, a, b, wg_t, wu_t, wd, name, exchange=None):
    t = x.shape[0]
    tm = min(TOK_TILE, t)
    tf = FFN_HID_TILE
    n_t, n_f = t // tm, FFN_HIDDEN // tf

    def body(dy_ref, x_ref, g_ref, a_ref, b_ref, wg_ref, wu_ref, wd_ref,
             dx_ref, dg_ref, da_ref, db_ref, h_ref, n_ref, dyh_ref, acc):
        i, j = pl.program_id(0), pl.program_id(1)

        @pl.when(j == 0)
        def _():
            xn, _ = _rms_fwd(x_ref[...], g_ref[...])
            n_ref[...] = xn.astype(BF16)
            dyh_ref[...] = (0.5 * dy_ref[...]).astype(BF16)
            acc[...] = jnp.zeros_like(acc)

        dyh = dyh_ref[...]
        for lo, hi in FFN_HID_SPLIT:
            dh = _dot(dyh, wd_ref[lo:hi, :], NT)
            av = a_ref[:, lo:hi].astype(F32)
            bv = b_ref[:, lo:hi].astype(F32)
            s = jax.nn.sigmoid(av)
            silu = av * s
            da = (dh * bv * (s * (1.0 + av * (1.0 - s)))).astype(BF16)
            db = (dh * silu).astype(BF16)
            da_ref[:, lo:hi] = da
            db_ref[:, lo:hi] = db
            h_ref[:, lo:hi] = (silu * bv).astype(BF16)
            acc[...] += _dot(da, wg_ref[lo:hi, :]) + _dot(db, wu_ref[lo:hi, :])

        @pl.when(j == n_f - 1)
        def _():
            xv, g = x_ref[...], g_ref[...]
            r = lax.rsqrt(jnp.mean(xv * xv, axis=-1, keepdims=True) + RMS_EPS)
            dx, dg_rows = _rms_bwd(acc[...], xv, g, r)
            dx_ref[...] = dy_ref[...] + dx
            dg = _colsum(dg_rows)

            @pl.when(i == 0)
            def _():
                dg_ref[...] = dg

            @pl.when(i > 0)
            def _():
                dg_ref[...] += dg

    wspec = pl.BlockSpec((tf, D_MODEL), lambda i, j: (j, 0))
    hspec = pl.BlockSpec((tm, tf), lambda i, j: (i, j))
    tok = _tok(tm, D_MODEL)
    hid = _sds((t, FFN_HIDDEN), BF16)
    return _gridded_call(
        body, name=name, grid=(n_t, n_f),
        in_specs=[tok, tok, _full((1, D_MODEL)), hspec, hspec, wspec, wspec, wspec],
        out_specs=[tok, _full((1, D_MODEL)), hspec, hspec, hspec, tok, tok],
        out_shape=[_sds((t, D_MODEL), F32), _sds((1, D_MODEL), F32), hid, hid, hid,
                   _sds((t, D_MODEL), BF16), _sds((t, D_MODEL), BF16)],
        scratch_shapes=[pltpu.VMEM((tm, D_MODEL), F32)], args=[dy, x, gain, a, b, wg_t, wu_t, wd], exchange=exchange)


def _tn_matmul(a, b, name, exchange=None):
    t, m = a.shape
    n = b.shape[1]
    n_m = 2 if m >= DW_SPLIT_ROWS else 1
    tk = min(DW_TOK_TILE * n_m, t)
    n_k, tm = t // tk, m // n_m

    def body(a_ref, b_ref, o_ref, acc):
        k = pl.program_id(1)
        p = _dot(a_ref[...], b_ref[...], TN)

        @pl.when(k == 0)
        def _():
            acc[...] = p

        @pl.when(k > 0)
        def _():
            acc[...] += p

        @pl.when(k == n_k - 1)
        def _():
            o_ref[...] = acc[...].astype(BF16)

    (out,), got = _gridded_call(
        body, name=name, grid=(n_m, n_k),
        in_specs=[pl.BlockSpec((tk, tm), lambda i, k: (k, i)), pl.BlockSpec((tk, n), lambda i, k: (k, 0))],
        out_specs=[pl.BlockSpec((tm, n), lambda i, k: (i, 0))], out_shape=[_sds((m, n), BF16)],
        scratch_shapes=[pltpu.VMEM((tm, n), F32)], args=[a, b], exchange=exchange)
    return out if exchange is None else (out, got)


def _proj_fwd(x, gain, wgate_t, wfox_t, wlat_t):
    t = x.shape[0]
    tm = min(PROJ_TILE, t)

    def body(x_ref, g_ref, wg_ref, wf_ref, wl_ref, og_ref, of_ref, ol_ref):
        xn, _ = _rms_fwd(x_ref[...], g_ref[...])
        n = xn.astype(BF16)
        og_ref[...] = _dot(n, wg_ref[...], NT)
        of_ref[...] = _dot(n, wf_ref[...], NT)
        ol_ref[...] = _dot(n, wl_ref[...], NT)

    return pl.pallas_call(
        body, name="proj_fwd", grid=(t // tm,),
        in_specs=[_tok(tm, D_MODEL), _full((1, D_MODEL)), _full(wgate_t.shape), _full(wfox_t.shape), _full(wlat_t.shape)],
        out_specs=[_tok(tm, 2 * D_MODEL), _tok(tm, 3 * 512), _tok(tm, LAT_W)],
        out_shape=[_sds((t, 2 * D_MODEL), F32), _sds((t, 3 * 512), F32), _sds((t, LAT_W), F32)],
        compiler_params=pltpu.CompilerParams(dimension_semantics=("arbitrary",)),
    )(x, gain, wgate_t, wfox_t, wlat_t)


def _proj_bwd(dgates, dfox, dlat, wgate_t, wfox_t, wlat_t, x, gain, dres):
    t = x.shape[0]
    tm = min(PROJ_TILE, t)

    def body(dg_ref, df_ref, dl_ref, wg_ref, wf_ref, wl_ref, x_ref, g_ref, dres_ref, dx_ref, dgain_ref, n_ref):
        i = pl.program_id(0)
        dn = _dot(dg_ref[...], wg_ref[...]) + _dot(df_ref[...], wf_ref[...]) + _dot(dl_ref[...], wl_ref[...])
        xv, g = x_ref[...], g_ref[...]
        xn, r = _rms_fwd(xv, g)
        n_ref[...] = xn.astype(BF16)
        dx, dg_rows = _rms_bwd(dn, xv, g, r)
        dx_ref[...] = dres_ref[...] + dx
        dgn = _colsum(dg_rows)

        @pl.when(i == 0)
        def _():
            dgain_ref[...] = dgn

        @pl.when(i > 0)
        def _():
            dgain_ref[...] += dgn

    tok = _tok(tm, D_MODEL)
    return pl.pallas_call(
        body, name="proj_bwd", grid=(t // tm,),
        in_specs=[_tok(tm, 2 * D_MODEL), _tok(tm, 3 * 512), _tok(tm, LAT_W), _full(wgate_t.shape), _full(wfox_t.shape),
                  _full(wlat_t.shape), tok, _full((1, D_MODEL)), tok],
        out_specs=[tok, _full((1, D_MODEL)), tok],
        out_shape=[_sds((t, D_MODEL), F32), _sds((1, D_MODEL), F32), _sds((t, D_MODEL), BF16)],
        compiler_params=pltpu.CompilerParams(dimension_semantics=("arbitrary",)),
    )(dgates, dfox, dlat, wgate_t, wfox_t, wlat_t, x, gain, dres)


def _tri(n, lower):
    r = lax.broadcasted_iota(jnp.int32, (n, n), 0)
    c = lax.broadcasted_iota(jnp.int32, (n, n), 1)
    return ((c <= r) if lower else (c >= r)).astype(F32)


def _log_sigmoid(z):
    return jnp.minimum(z, 0.0) - jnp.log1p(jnp.exp(-jnp.abs(z)))


N_GROUPS = 16
LANES = MLA_QK_LANES


def _split_dot(a, g):
    hi = a.astype(BF16)
    lo = (a - hi.astype(F32)).astype(BF16)
    return _dot(hi, g) + _dot(lo, g)


def _rope_fwd(y, c, s1, s2):
    return y * c + pltpu.roll(y, LANES - ROPE_HALF, 1) * s1 + pltpu.roll(y, ROPE_HALF, 1) * s2


def _rope_bwd(do, c, s1, s2):
    return do * c + pltpu.roll(do * s1, ROPE_HALF, 1) + pltpu.roll(do * s2, LANES - ROPE_HALF, 1)


def _prep_tables(cos, sin):
    t = cos.shape[0]
    z = lambda n: jnp.zeros((t, n), F32)
    c = jnp.concatenate([jnp.ones((t, HEAD_DIM), F32), cos, cos, z(LANES - HEAD_DIM - ROPE_DIM)], axis=1)
    s1 = jnp.concatenate([z(HEAD_DIM), -sin, z(LANES - HEAD_DIM - ROPE_HALF)], axis=1)
    s2 = jnp.concatenate([z(HEAD_DIM + ROPE_HALF), sin, z(LANES - HEAD_DIM - ROPE_DIM)], axis=1)
    lane = jnp.arange(LANES)
    rope_mask = ((lane >= HEAD_DIM) & (lane < HEAD_DIM + ROPE_DIM)).astype(F32)[None, :]
    return c, s1, s2, rope_mask


def _group_matrices():
    lane = jnp.arange(N_HEADS * LANES)
    head, d = (lane // LANES)[:, None], (lane % LANES)[:, None]
    col = jnp.arange(N_GROUPS)[None, :]
    g_mla = ((col == head) & (d < HEAD_DIM)) | ((col == N_HEADS + head) & (d >= HEAD_DIM) & (d < HEAD_DIM + ROPE_DIM))
    g_fox = col == (jnp.arange(N_HEADS * HEAD_DIM) // HEAD_DIM)[:, None]
    inv_mla = jnp.concatenate([jnp.full((1, N_HEADS), 1.0 / HEAD_DIM, F32), jnp.full((1, N_HEADS), 1.0 / ROPE_DIM, F32)], axis=1)
    inv_fox = jnp.full((1, N_GROUPS), 1.0 / HEAD_DIM, F32)
    return g_mla.astype(BF16), g_mla.T.astype(BF16), inv_mla, g_fox.astype(BF16), g_fox.T.astype(BF16), inv_fox


def _interleave_weights(wqb_t, wkvb_t):
    wq = jnp.pad(wqb_t.reshape(N_HEADS, HEAD_DIM + ROPE_DIM, Q_LORA), ((0, 0), (0, LANES - HEAD_DIM - ROPE_DIM), (0, 0)))
    kv = wkvb_t.reshape(N_HEADS, 2, HEAD_DIM, KV_LORA)
    wk = jnp.pad(kv[:, 0], ((0, 0), (0, LANES - HEAD_DIM), (0, 0)))
    return wq.reshape(N_HEADS * LANES, Q_LORA), wk.reshape(N_HEADS * LANES, KV_LORA), kv[:, 1].reshape(N_HEADS * HEAD_DIM, KV_LORA)


def _deinterleave_grads(dwq, dwk, dwv):
    dq = dwq.reshape(N_HEADS, LANES, Q_LORA)[:, :HEAD_DIM + ROPE_DIM].reshape(N_HEADS * (HEAD_DIM + ROPE_DIM), Q_LORA)
    dk = dwk.reshape(N_HEADS, LANES, KV_LORA)[:, :HEAD_DIM]
    dkv = jnp.stack([dk, dwv.reshape(N_HEADS, HEAD_DIM, KV_LORA)], axis=1)
    return dq, dkv.reshape(N_HEADS * 2 * HEAD_DIM, KV_LORA)


def _head_gains(g_qn, g_qr, g_kn, g_kr, g_fq, g_fk):
    z = lambda n: jnp.zeros((1, n), F32)
    gq = jnp.concatenate([g_qn, g_qr, z(LANES - HEAD_DIM - ROPE_DIM)], axis=1)
    gk = jnp.concatenate([g_kn, z(LANES - HEAD_DIM)], axis=1)
    gkr = jnp.concatenate([z(HEAD_DIM), g_kr, z(LANES - HEAD_DIM - ROPE_DIM)], axis=1)
    return jnp.tile(gq, (1, N_HEADS)), jnp.tile(gk, (1, N_HEADS)), gkr, jnp.tile(g_fq, (1, N_HEADS)), jnp.tile(g_fk, (1, N_HEADS))


def _group_rms(x, g, g_t, inv):
    r = lax.rsqrt(_split_dot(x * x, g) * inv + RMS_EPS)
    return _split_dot(r, g_t)


def _value_layout(wv):
    lane = jnp.arange(N_HEADS * LANES)
    src = (lane // LANES) * HEAD_DIM + lane % LANES
    place = (jnp.arange(N_HEADS * HEAD_DIM)[:, None] == src[None, :]) & (lane % LANES < HEAD_DIM)[None, :]
    ones = (lane % LANES == HEAD_DIM).astype(F32)[None, :]
    wv_il = jnp.pad(wv.reshape(N_HEADS, HEAD_DIM, KV_LORA), ((0, 0), (0, LANES - HEAD_DIM), (0, 0)))
    return place.astype(BF16), ones, wv_il.reshape(N_HEADS * LANES, KV_LORA)


def _prep_fwd(fox, lat, tables, mats, wq, wk, wv, g_qlat, g_kvlat, gains, b_f, v_place, v_ones, wv_il):
    t = fox.shape[0]
    tm = min(PREP_TILE, t)
    c_tab, s1_tab, s2_tab, rope_mask = tables

    def body(fox_ref, lat_ref, c_ref, s1_ref, s2_ref, rm_ref, gm_ref, gmt_ref, im_ref, gf_ref, gft_ref, if_ref,
             wq_ref, wk_ref, wv_ref, gql_ref, gkvl_ref, gq_ref, gk_ref, gkr_ref, gfq_ref, gfk_ref, bf_ref,
             vp_ref, vo_ref, wvil_ref,
             fq_ref, fk_ref, fv_ref, cc_ref, mq_ref, mk_ref, mv_ref, fqn_ref, fkmax_ref, mqn_ref, mkmax_ref,
             fv1_ref, mv1_ref, carry):
        i = pl.program_id(0)
        ct, s1, s2 = c_ref[...], s1_ref[...], s2_ref[...]

        @pl.when(i == 0)
        def _():
            carry[...] = jnp.zeros_like(carry)
            fkmax_ref[...] = jnp.zeros_like(fkmax_ref)
            mkmax_ref[...] = jnp.zeros_like(mkmax_ref)

        def square_sums(ref, g):
            v = ref[...].astype(F32)
            return _split_dot(v * v, g)

        xq = fox_ref[:, 0:512]
        fq_ref[...] = (xq * _group_rms(xq, gf_ref[...], gft_ref[...], if_ref[...]) * (gfq_ref[...] * FOX_SCALE)).astype(BF16)
        xk = fox_ref[:, 512:1024]
        fk_ref[...] = (xk * _group_rms(xk, gf_ref[...], gft_ref[...], if_ref[...]) * gfk_ref[...]).astype(BF16)
        fv_b = fox_ref[:, 1024:1536].astype(BF16)
        fv_ref[...] = fv_b
        fv1_ref[...] = (_dot(fv_b, vp_ref[...]) + vo_ref[...]).astype(BF16)
        fqn_ref[...] = square_sums(fq_ref, gf_ref[...])
        fkmax_ref[...] = jnp.maximum(fkmax_ref[...], jnp.max(square_sums(fk_ref, gf_ref[...]), axis=0, keepdims=True))


        logf = _log_sigmoid(lat_ref[:, LAT_F:LAT_F + N_HEADS] + bf_ref[...])
        cc_ref[...] = jnp.dot(_tri(tm, True), logf, precision=lax.Precision.HIGHEST, preferred_element_type=F32) + carry[...]
        carry[...] += _colsum(logf)

        qlat_n, _ = _rms_fwd(lat_ref[:, LAT_Q:LAT_Q + Q_LORA], gql_ref[...])
        p = _dot(qlat_n.astype(BF16), wq_ref[...], NT)
        y = p * _group_rms(p, gm_ref[...], gmt_ref[...], im_ref[...]) * (gq_ref[...] * MLA_SCALE)
        for h in range(N_HEADS):
            w = slice(h * LANES, (h + 1) * LANES)
            mq_ref[:, w] = _rope_fwd(y[:, w], ct, s1, s2).astype(BF16)

        kv_n, _ = _rms_fwd(lat_ref[:, LAT_KV:LAT_KV + KV_LORA], gkvl_ref[...])
        kv_b = kv_n.astype(BF16)
        pk = _dot(kv_b, wk_ref[...], NT)
        kn = pk * _group_rms(pk, gm_ref[...], gmt_ref[...], im_ref[...]) * gk_ref[...]
        rm = rm_ref[...]
        kr = pltpu.roll(lat_ref[:, LAT_KR:LAT_KR + LANES], HEAD_DIM, 1) * rm
        rr = lax.rsqrt(jnp.sum(kr * kr, axis=1, keepdims=True) * (1.0 / ROPE_DIM) + RMS_EPS)
        okr = _rope_fwd(kr * rr * gkr_ref[...], ct * rm, s1, s2)
        for h in range(N_HEADS):
            w = slice(h * LANES, (h + 1) * LANES)
            mk_ref[:, w] = (kn[:, w] + okr).astype(BF16)
        mv_ref[...] = _dot(kv_b, wv_ref[...], NT).astype(BF16)
        mv1_ref[...] = (_dot(kv_b, wvil_ref[...], NT).astype(BF16).astype(F32) + vo_ref[...]).astype(BF16)
        mqn_ref[...] = square_sums(mq_ref, gm_ref[...])
        mkmax_ref[...] = jnp.maximum(mkmax_ref[...], jnp.max(square_sums(mk_ref, gm_ref[...]), axis=0, keepdims=True))

    consts = [rope_mask, *mats, wq, wk, wv, g_qlat, g_kvlat, *gains, b_f, v_place, v_ones, wv_il]
    t512, tl, tg, g1 = _tok(tm, 512), _tok(tm, LANES), _tok(tm, N_GROUPS), _full((1, N_GROUPS))
    t1024 = _tok(tm, N_HEADS * LANES)
    return pl.pallas_call(
        body, name="prep_fwd", grid=(t // tm,),
        in_specs=[_tok(tm, 1536), _tok(tm, LAT_W), tl, tl, tl] + [_full(a.shape) for a in consts],
        out_specs=[t512, t512, t512, _tok(tm, N_HEADS), t1024, t1024, t512, tg, g1, tg, g1, t1024, t1024],
        out_shape=[_sds((t, 512), BF16), _sds((t, 512), BF16), _sds((t, 512), BF16), _sds((t, N_HEADS), F32),
                   _sds((t, N_HEADS * LANES), BF16), _sds((t, N_HEADS * LANES), BF16), _sds((t, 512), BF16),
                   _sds((t, N_GROUPS), F32), _sds((1, N_GROUPS), F32), _sds((t, N_GROUPS), F32), _sds((1, N_GROUPS), F32)]
        + [_sds((t, N_HEADS * LANES), BF16)] * 2,
        scratch_shapes=[pltpu.VMEM((1, N_HEADS), F32)],
        compiler_params=pltpu.CompilerParams(dimension_semantics=("arbitrary",)),
    )(fox, lat, c_tab, s1_tab, s2_tab, *consts)


def _prep_bwd(fox, lat, tables, mats, wq, wk, wv, g_qlat, g_kvlat, gains, b_f, dfq, dfk, dfv, dc, dmq, dmk, dmv):
    t = fox.shape[0]
    tm = min(PREP_TILE, t)
    n_t = t // tm
    c_tab, s1_tab, s2_tab, rope_mask = tables

    def body(fox_ref, lat_ref, c_ref, s1_ref, s2_ref, rm_ref, gm_ref, gmt_ref, im_ref, gf_ref, gft_ref, if_ref,
             wq_ref, wk_ref, wv_ref, gql_ref, gkvl_ref, gq_ref, gk_ref, gkr_ref, gfq_ref, gfk_ref, bf_ref,
             dfq_ref, dfk_ref, dfv_ref, dc_ref, dmq_ref, dmk_ref, dmv_ref,
             dfox_ref, dlat_ref, dwq_ref, dwk_ref, dwv_ref, o_gql, o_gkvl, o_gq, o_gk, o_gkr, o_gfq, o_gfk, o_bf,
             carry, lscr, wq_acc, wk_acc, wv_acc, dyscr):
        i = pl.program_id(0)
        ct, s1, s2, rm = c_ref[...], s1_ref[...], s2_ref[...], rm_ref[...]

        @pl.when(i == 0)
        def _():
            for ref in [carry, wq_acc, wk_acc, wv_acc, o_gql, o_gkvl, o_gq, o_gk, o_gkr, o_gfq, o_gfk, o_bf]:
                ref[...] = jnp.zeros_like(ref)

        def group_rms_bwd(dy, x, r, gain, g, g_t, inv):
            xn = x * r
            dyg = dy * gain
            mu = _split_dot(_split_dot(dyg * xn, g) * inv, g_t)
            return r * (dyg - xn * mu), dy * xn

        gf, gft, invf = gf_ref[...], gft_ref[...], if_ref[...]
        xq = fox_ref[:, 0:512]
        dxq, rows = group_rms_bwd(dfq_ref[...] * FOX_SCALE, xq, _group_rms(xq, gf, gft, invf), gfq_ref[...], gf, gft, invf)
        dfox_ref[:, 0:512] = dxq.astype(BF16)
        o_gfq[...] += _colsum(rows)
        xk = fox_ref[:, 512:1024]
        dxk, rows = group_rms_bwd(dfk_ref[...].T, xk, _group_rms(xk, gf, gft, invf), gfk_ref[...], gf, gft, invf)
        dfox_ref[:, 512:1024] = dxk.astype(BF16)
        o_gfk[...] += _colsum(rows)
        dfox_ref[:, 1024:1536] = dfv_ref[...].T.astype(BF16)

        lscr[...] = jnp.zeros_like(lscr)

        xql = lat_ref[:, LAT_Q:LAT_Q + Q_LORA]
        qlat_n, r_ql = _rms_fwd(xql, gql_ref[...])
        qlat_b = qlat_n.astype(BF16)
        gm, gmt, invm = gm_ref[...], gmt_ref[...], im_ref[...]
        p = _dot(qlat_b, wq_ref[...], NT)
        for h in range(N_HEADS):
            w = slice(h * LANES, (h + 1) * LANES)
            dyscr[:, w] = _rope_bwd(dmq_ref[:, w], ct, s1, s2) * MLA_SCALE
        dp, rows = group_rms_bwd(dyscr[...], p, _group_rms(p, gm, gmt, invm), gq_ref[...], gm, gmt, invm)
        o_gq[...] += _colsum(rows)
        dp_b = dp.astype(BF16)
        wq_acc[...] += _dot(dp_b, qlat_b, TN)
        dxql, rows = _rms_bwd(_dot(dp_b, wq_ref[...]), xql, gql_ref[...], r_ql)
        lscr[:, LAT_Q:LAT_Q + Q_LORA] = dxql
        o_gql[...] += _colsum(rows)

        xkv = lat_ref[:, LAT_KV:LAT_KV + KV_LORA]
        kv_n, r_kv = _rms_fwd(xkv, gkvl_ref[...])
        kv_b = kv_n.astype(BF16)
        pk = _dot(kv_b, wk_ref[...], NT)
        dmk = dmk_ref[...].T
        dpk, rows = group_rms_bwd(dmk, pk, _group_rms(pk, gm, gmt, invm), gk_ref[...], gm, gmt, invm)
        o_gk[...] += _colsum(rows)
        dpk_b = dpk.astype(BF16)
        dv_b = dmv_ref[...].T.astype(BF16)
        wk_acc[...] += _dot(dpk_b, kv_b, TN)
        wv_acc[...] += _dot(dv_b, kv_b, TN)
        dxkv, rows = _rms_bwd(_dot(dpk_b, wk_ref[...]) + _dot(dv_b, wv_ref[...]), xkv, gkvl_ref[...], r_kv)
        lscr[:, LAT_KV:LAT_KV + KV_LORA] = dxkv
        o_gkvl[...] += _colsum(rows)

        dokr = dmk[:, 0:LANES]
        for h in range(1, N_HEADS):
            dokr = dokr + dmk[:, h * LANES:(h + 1) * LANES]
        dykr = _rope_bwd(dokr * rm, ct * rm, s1, s2)
        kr = pltpu.roll(lat_ref[:, LAT_KR:LAT_KR + LANES], HEAD_DIM, 1) * rm
        rr = lax.rsqrt(jnp.sum(kr * kr, axis=1, keepdims=True) * (1.0 / ROPE_DIM) + RMS_EPS)
        krn = kr * rr
        dyg = dykr * gkr_ref[...]
        dkr = rr * (dyg - krn * (jnp.sum(dyg * krn, axis=1, keepdims=True) * (1.0 / ROPE_DIM)))
        o_gkr[...] += _colsum(dykr * krn)
        lscr[:, LAT_KR:LAT_KR + LANES] = pltpu.roll(dkr, LANES - HEAD_DIM, 1)

        dcv = dc_ref[...]
        dlogf = jnp.dot(_tri(tm, False), dcv, precision=lax.Precision.HIGHEST, preferred_element_type=F32) + carry[...]
        carry[...] += _colsum(dcv)
        dz = dlogf * jax.nn.sigmoid(-(lat_ref[:, LAT_F:LAT_F + N_HEADS] + bf_ref[...]))
        lscr[:, LAT_F:LAT_F + N_HEADS] = dz
        o_bf[...] += _colsum(dz)

        dlat_ref[...] = lscr[...].astype(BF16)

        @pl.when(i == n_t - 1)
        def _():
            dwq_ref[...] = wq_acc[...].astype(BF16)
            dwk_ref[...] = wk_acc[...].astype(BF16)
            dwv_ref[...] = wv_acc[...].astype(BF16)

    consts = [rope_mask, *mats, wq, wk, wv, g_qlat, g_kvlat, *gains, b_f]

    def rtok(n):
        return pl.BlockSpec((tm, n), lambda i: (n_t - 1 - i, 0))

    def rtok_t(n):
        return pl.BlockSpec((n, tm), lambda i: (0, n_t - 1 - i))

    sums =[(1, Q_LORA), (1, KV_LORA), (1, N_HEADS * LANES), (1, N_HEADS * LANES), (1, LANES), (1, 512), (1, 512), (1, N_HEADS)]
    return pl.pallas_call(
        body, name="prep_bwd", grid=(n_t,),
        in_specs=[rtok(1536), rtok(LAT_W), rtok(LANES), rtok(LANES), rtok(LANES)] + [_full(a.shape) for a in consts]
        + [rtok(512), rtok_t(512), rtok_t(512), rtok(N_HEADS), rtok(N_HEADS * LANES), rtok_t(N_HEADS * LANES), rtok_t(512)],
        out_specs=[rtok(1536), rtok(LAT_W), _full(wq.shape), _full(wk.shape), _full(wv.shape)] + [_full(s) for s in sums],
        out_shape=[_sds((t, 1536), BF16), _sds((t, LAT_W), BF16), _sds(wq.shape, BF16), _sds(wk.shape, BF16), _sds(wv.shape, BF16)]
        + [_sds(s, F32) for s in sums],
        scratch_shapes=[pltpu.VMEM((1, N_HEADS), F32), pltpu.VMEM((tm, LAT_W), F32), pltpu.VMEM(wq.shape, F32),
                        pltpu.VMEM(wk.shape, F32), pltpu.VMEM(wv.shape, F32), pltpu.VMEM((tm, N_HEADS * LANES), F32)],
        compiler_params=pltpu.CompilerParams(dimension_semantics=("arbitrary",)),
    )(fox, lat, c_tab, s1_tab, s2_tab, *consts, dfq, dfk, dfv, dc, dmq, dmk, dmv)


def _causal_pairs(n, query_major):
    pairs = [(i, j) for i in range(n) for j in range(i + 1)] if query_major else [(i, j) for j in range(n) for i in range(j, n)]
    return jnp.asarray([p[0] for p in pairs], jnp.int32), jnp.asarray([p[1] for p in pairs], jnp.int32)


def _logit_bound(q_sq, k_sq_max):
    qn = q_sq[:, :N_HEADS] + q_sq[:, N_HEADS:]
    kmax = k_sq_max[:, :N_HEADS] + k_sq_max[:, N_HEADS:]
    bound = jnp.sqrt(qn * kmax) * (1.0 + 2.0 ** -10) + 2.0 ** -10
    flag = (jnp.max(bound) <= FIXED_SHIFT_MAX_BOUND).astype(F32).reshape(1, 1)
    return bound, flag


def _attn_fwd(q, k, v, bound, fixed_ok, c, c_t, *, lanes, name, exchange=None):
    t = q.shape[0]
    tq = min(ATT_TILE, t)
    n_q = t // tq
    hd = HEAD_DIM
    ch = min(ATT_COL_CHUNK, tq)
    decay = c is not None

    def body(qi_ref, kj_ref, *refs):
        if decay:
            q_ref, k_ref, v_ref, b_ref, ok_ref, c_ref, ct_ref, o_ref, o32_ref, lse_ref, m_scr, acc = refs
        else:
            q_ref, k_ref, v_ref, b_ref, ok_ref, o_ref, o32_ref, lse_ref, m_scr, acc = refs
        i, j = qi_ref[pl.program_id(0)], kj_ref[pl.program_id(0)]
        fixed = ok_ref[0, 0] > 0.5

        @pl.when(j == 0)
        def _():
            m_scr[...] = jnp.full_like(m_scr, MASK_VALUE)
            acc[...] = jnp.zeros_like(acc)

        def fixed_step(diagonal):
            for h in range(N_HEADS):
                wl = slice(h * lanes, (h + 1) * lanes)
                wv = slice(h * LANES, (h + 1) * LANES)
                qh = q_ref[:, wl]
                row = (c_ref[:, h:h + 1] - b_ref[:, h:h + 1]) if decay else -b_ref[:, h:h + 1]
                o_hi = jnp.zeros((tq, LANES), F32)
                o_lo = jnp.zeros((tq, LANES), F32)
                for cc in range(tq // ch):
                    cols = slice(cc * ch, (cc + 1) * ch)
                    s = _dot(qh, k_ref[cols, wl], NT)
                    s = s + ((row - ct_ref[h:h + 1, cols]) if decay else row)
                    if diagonal:
                        keep = (lax.broadcasted_iota(jnp.int32, (tq, ch), 0)
                                >= lax.broadcasted_iota(jnp.int32, (tq, ch), 1) + cc * ch)
                        s = jnp.where(keep, s, MASK_VALUE)
                    p = jnp.exp(s)
                    p_b = p.astype(BF16)
                    o_hi = o_hi + _dot(p_b, v_ref[cols, wv])
                    if decay:
                        o_lo = o_lo + _dot((p - p_b.astype(F32)).astype(BF16), v_ref[cols, wv])
                acc[0, :, wv] += o_hi
                if decay:
                    acc[1, :, wv] += o_lo

        def step(diagonal):
            if diagonal:
                keep = lax.broadcasted_iota(jnp.int32, (tq, tq), 0) >= lax.broadcasted_iota(jnp.int32, (tq, tq), 1)
            for h in range(N_HEADS):
                s = _dot(q_ref[:, h * lanes:(h + 1) * lanes], k_ref[:, h * lanes:(h + 1) * lanes], NT)
                if decay:
                    s = s + (c_ref[:, h:h + 1] - ct_ref[h:h + 1, :])
                if diagonal:
                    s = jnp.where(keep, s, MASK_VALUE)
                m_prev = m_scr[h]
                m_new = jnp.maximum(m_prev, jnp.max(s, axis=1, keepdims=True))
                alpha = jnp.exp(m_prev - m_new)
                p = jnp.exp(s - m_new)
                wv = slice(h * LANES, (h + 1) * LANES)
                p_b = p.astype(BF16)
                acc[0, :, wv] = alpha * acc[0, :, wv] + _dot(p_b, v_ref[:, wv])
                if decay:
                    acc[1, :, wv] = alpha * acc[1, :, wv] + _dot((p - p_b.astype(F32)).astype(BF16), v_ref[:, wv])
                m_scr[h] = m_new

        for diagonal, here in ((False, j < i), (True, j == i)):
            @pl.when(here & fixed)
            def _():
                fixed_step(diagonal)

            @pl.when(here & jnp.logical_not(fixed))
            def _():
                step(diagonal)

        @pl.when(j == i)
        def _():
            for h in range(N_HEADS):
                w = slice(h * hd, (h + 1) * hd)
                val, one = slice(h * LANES, h * LANES + hd), slice(h * LANES + hd, h * LANES + hd + 1)
                l = (acc[0, :, one] + acc[1, :, one]) if decay else acc[0, :, one]
                o_ref[:, w] = (acc[0, :, val] / l).astype(BF16)
                o32_ref[:, w] = ((acc[0, :, val] + acc[1, :, val]) if decay else acc[0, :, val]) / l
                lse_ref[:, h:h + 1] = jnp.where(fixed, b_ref[:, h:h + 1], m_scr[h]) + jnp.log(l)

    qspec = lambda n: pl.BlockSpec((tq, n), lambda s, qi, kj: (qi[s], 0))
    kspec = lambda n: pl.BlockSpec((tq, n), lambda s, qi, kj: (kj[s], 0))
    in_specs = [qspec(N_HEADS * lanes), kspec(N_HEADS * lanes), kspec(N_HEADS * LANES), qspec(N_HEADS),
                pl.BlockSpec(memory_space=pltpu.SMEM)]
    args = [q, k, v, bound, fixed_ok]
    if decay:
        in_specs += [qspec(N_HEADS), pl.BlockSpec((N_HEADS, tq), lambda s, qi, kj: (0, kj[s]))]
        args += [c, c_t]
    pairs = _causal_pairs(n_q, query_major=True)
    return _gridded_call(
        body, name=name, grid=(pairs[0].shape[0],), in_specs=in_specs, out_specs=[qspec(512), qspec(512), qspec(N_HEADS)],
        out_shape=[_sds((t, 512), BF16), _sds((t, 512), F32), _sds((t, N_HEADS), F32)],
        scratch_shapes=[pltpu.VMEM((N_HEADS, tq, 1), F32), pltpu.VMEM((2, tq, N_HEADS * LANES), F32)],
        args=args, exchange=exchange, prefetch=pairs)


def _attn_bwd(q, k, v, do, delta, lse, c, c_t, *, lanes, name, exchange=None):
    t = q.shape[0]
    tq = min(ATT_TILE, t)
    n_q = t // tq
    hd = HEAD_DIM
    decay = c is not None

    pairs = _causal_pairs(n_q, query_major=False)
    n_steps = pairs[0].shape[0]

    def body(qi_ref, kj_ref, *refs):
        if decay:
            q_ref, k_ref, v_ref, do_ref, delta_ref, lse_ref, c_ref, ct_ref, dq_hbm, dk_ref, dv_ref, dct_ref, dq_ref = refs
        else:
            q_ref, k_ref, v_ref, do_ref, delta_ref, lse_ref, dq_hbm, dk_ref, dv_ref, dq_ref = refs
        step_id = pl.program_id(0)
        i, j = qi_ref[step_id], kj_ref[step_id]

        @pl.when(step_id == 0)
        def _():
            dq_ref[...] = jnp.zeros_like(dq_ref)

        @pl.when(i == j)
        def _():
            dk_ref[...] = jnp.zeros_like(dk_ref)
            dv_ref[...] = jnp.zeros_like(dv_ref)
            if decay:
                dct_ref[...] = jnp.zeros_like(dct_ref)

        def step(diagonal):
            if diagonal:
                keep = lax.broadcasted_iota(jnp.int32, (tq, tq), 0) >= lax.broadcasted_iota(jnp.int32, (tq, tq), 1)
            rows = pl.ds(pl.multiple_of(i * tq, tq), tq)
            for h in range(N_HEADS):
                wl = slice(h * lanes, (h + 1) * lanes)
                w = slice(h * hd, (h + 1) * hd)
                qh, kh = q_ref[:, wl], k_ref[:, wl]
                s = _dot(qh, kh, NT)
                if decay:
                    s = s + (c_ref[:, h:h + 1] - ct_ref[h:h + 1, :])
                if diagonal:
                    s = jnp.where(keep, s, MASK_VALUE)
                p = jnp.exp(s - lse_ref[:, h:h + 1])
                doh = do_ref[:, w]
                dv_ref[w, :] += _dot(doh, p.astype(BF16), TN)
                dp = _dot(doh, v_ref[:, w], NT)
                ds = p * (dp - delta_ref[:, h:h + 1])
                if decay:
                    dct_ref[h:h + 1, :] -= _colsum(ds)
                ds_b = ds.astype(BF16)
                dk_ref[wl, :] += _dot(qh, ds_b, TN)
                dq_ref[rows, wl] += _dot(ds_b, kh)

        @pl.when(i > j)
        def _():
            step(False)

        @pl.when(i == j)
        def _():
            step(True)

        @pl.when(step_id == n_steps - 1)
        def _():
            pltpu.sync_copy(dq_ref, dq_hbm)

    qspec = lambda n: pl.BlockSpec((tq, n), lambda s, qi, kj: (qi[s], 0))
    kspec = lambda n: pl.BlockSpec((tq, n), lambda s, qi, kj: (kj[s], 0))
    in_specs = [qspec(N_HEADS * lanes), kspec(N_HEADS * lanes), kspec(512), qspec(512), qspec(N_HEADS), qspec(N_HEADS)]
    kspec_t = lambda n: pl.BlockSpec((n, tq), lambda s, qi, kj: (0, kj[s]))
    out_specs = [pl.BlockSpec(memory_space=pl.ANY), kspec_t(N_HEADS * lanes), kspec_t(512)]
    out_shape = [_sds((t, N_HEADS * lanes), F32), _sds((N_HEADS * lanes, t), F32), _sds((512, t), F32)]
    args = [q, k, v, do, delta, lse]
    if decay:
        ctspec = pl.BlockSpec((N_HEADS, tq), lambda s, qi, kj: (0, kj[s]))
        in_specs += [qspec(N_HEADS), ctspec]
        out_specs.append(ctspec)
        out_shape.append(_sds((N_HEADS, t), F32))
        args += [c, c_t]
    return _gridded_call(body, name=name, grid=(n_steps,), in_specs=in_specs, out_specs=out_specs, out_shape=out_shape,
                         scratch_shapes=[pltpu.VMEM((t, N_HEADS * lanes), F32)], args=args, exchange=exchange, prefetch=pairs)


def _mix_fwd(x, y_mla, y_fox, gates, b_gate, wbm_t, wbf_t, wo):
    t = x.shape[0]
    tm = min(PROJ_TILE, t)

    def body(x_ref, ym_ref, yf_ref, gt_ref, bg_ref, wbm_ref, wbf_ref, wo_ref, out_ref):
        um = _dot(ym_ref[...], wbm_ref[...], NT)
        uf = _dot(yf_ref[...], wbf_ref[...], NT)
        sm = jax.nn.sigmoid(gt_ref[:, 0:D_MODEL] + bg_ref[0:1, :])
        sf = jax.nn.sigmoid(gt_ref[:, D_MODEL:2 * D_MODEL] + bg_ref[1:2, :])
        mixed = sm * um + sf * uf
        out_ref[...] = x_ref[...] + _dot(mixed.astype(BF16), wo_ref[...])

    tok = _tok(tm, D_MODEL)
    return pl.pallas_call(
        body, name="mix_fwd", grid=(t // tm,),
        in_specs=[tok, _tok(tm, 512), _tok(tm, 512), _tok(tm, 2 * D_MODEL), _full((2, D_MODEL)), _full(wbm_t.shape),
                  _full(wbf_t.shape), _full(wo.shape)],
        out_specs=tok, out_shape=_sds((t, D_MODEL), F32),
        compiler_params=pltpu.CompilerParams(dimension_semantics=("arbitrary",)),
    )(x, y_mla, y_fox, gates, b_gate, wbm_t, wbf_t, wo)


def _mix_bwd(dx, y_mla, y_fox, y_mla32, y_fox32, gates, b_gate, wbm_t, wbf_t, wo, head_groups, exchange=None):
    t = dx.shape[0]
    tm = min(PROJ_TILE, t)

    def body(dx_ref, ym_ref, yf_ref, ym32_ref, yf32_ref, gt_ref, bg_ref, wbm_ref, wbf_ref, wo_ref, gf_ref,
             dym_ref, dyf_ref, dlm_ref, dlf_ref, dgt_ref, mixed_ref, dum_ref, duf_ref, dxb_ref, dbg_ref):
        i = pl.program_id(0)
        dxb = dx_ref[...].astype(BF16)
        dxb_ref[...] = dxb
        dmixed = _dot(dxb, wo_ref[...], NT)
        um = _dot(ym_ref[...], wbm_ref[...], NT)
        uf = _dot(yf_ref[...], wbf_ref[...], NT)
        sm = jax.nn.sigmoid(gt_ref[:, 0:D_MODEL] + bg_ref[0:1, :])
        sf = jax.nn.sigmoid(gt_ref[:, D_MODEL:2 * D_MODEL] + bg_ref[1:2, :])
        mixed_ref[...] = (sm * um + sf * uf).astype(BF16)
        dum = (dmixed * sm).astype(BF16)
        duf = (dmixed * sf).astype(BF16)
        dum_ref[...] = dum
        duf_ref[...] = duf
        dgm = dmixed * um * (sm * (1.0 - sm))
        dgf = dmixed * uf * (sf * (1.0 - sf))
        dgt_ref[:, 0:D_MODEL] = dgm.astype(BF16)
        dgt_ref[:, D_MODEL:2 * D_MODEL] = dgf.astype(BF16)
        for du, wb_ref, y32_ref, dy_ref, dl_ref in ((dum, wbm_ref, ym32_ref, dym_ref, dlm_ref),
                                                    (duf, wbf_ref, yf32_ref, dyf_ref, dlf_ref)):
            dy = _dot(du, wb_ref[...])
            dy_ref[...] = dy.astype(BF16)
            prod = dy.astype(BF16).astype(F32) * y32_ref[...]
            hi = prod.astype(BF16)
            rest = prod - hi.astype(F32)
            mid = rest.astype(BF16)
            dl_ref[...] = _dot(hi, gf_ref[...]) + _dot(mid, gf_ref[...]) + _dot((rest - mid.astype(F32)).astype(BF16), gf_ref[...])

        @pl.when(i == 0)
        def _():
            dbg_ref[...] = jnp.zeros_like(dbg_ref)

        dbg_ref[0:1, :] += _colsum(dgm)
        dbg_ref[1:2, :] += _colsum(dgf)

    tok = _tok(tm, D_MODEL)
    tokb = _sds((t, D_MODEL), BF16)
    t512, tg = _tok(tm, 512), _tok(tm, N_GROUPS)
    return _gridded_call(
        body, name="mix_bwd", grid=(t // tm,),
        in_specs=[tok, t512, t512, t512, t512, _tok(tm, 2 * D_MODEL), _full((2, D_MODEL)), _full(wbm_t.shape),
                  _full(wbf_t.shape), _full(wo.shape), _full(head_groups.shape)],
        out_specs=[t512, t512, tg, tg, _tok(tm, 2 * D_MODEL), tok, tok, tok, tok, _full((2, D_MODEL))],
        out_shape=[_sds((t, 512), BF16), _sds((t, 512), BF16), _sds((t, N_GROUPS), F32), _sds((t, N_GROUPS), F32),
                   _sds((t, 2 * D_MODEL), BF16), tokb, tokb, tokb, tokb, _sds((2, D_MODEL), F32)],
        scratch_shapes=[], args=[dx, y_mla, y_fox, y_mla32, y_fox32, gates, b_gate, wbm_t, wbf_t, wo, head_groups],
        exchange=exchange)


def _my_position():
    x, y, c = lax.axis_index("x"), lax.axis_index("y"), lax.axis_index("c")
    return x, y, c, 4 * x + 2 * y + c


def _peer(x, y, c, mask):
    px = 1 - x if mask & 4 else x
    py = 1 - y if mask & 2 else y
    pc = 1 - c if mask & 1 else c
    return (px, py, pc), 4 * px + 2 * py + pc


def _chip_peer(x, y, km):
    px = 1 - x if km & 2 else x
    py = 1 - y if km & 1 else y
    return px, py, 2 * px + py


_HBM = pl.BlockSpec(memory_space=pl.ANY)


def _wait_all(copies):
    for cp in copies:
        cp.wait()


class _ChipExchange:
    def __init__(self, gather, arrays):
        self.gather, self.arrays, self.aliased = gather, list(arrays), False
        n = len(self.arrays)
        self.out_shape = [_sds((N_DEV * a.shape[0],) + a.shape[1:], a.dtype) if gather else _sds(a.shape, a.dtype)
                          for a in self.arrays]
        self.scratch_shapes = [pltpu.SemaphoreType.DMA((n, N_CHIP)), pltpu.SemaphoreType.DMA((n, N_CHIP)),
                               pltpu.SemaphoreType.DMA((n,))]

    def copies(self, srcs, dsts, send_sems, recv_sems, local_sems):
        x, y, c, me = _my_position()
        q_me = 2 * x + y
        out = []
        for a in range(len(self.arrays)):
            if self.gather:
                r = srcs[a].shape[0]
                local_src, dst = srcs[a], dsts[a].at[pl.ds(me * r, r)]
            else:
                local_src, dst = srcs[a].at[q_me], dsts[a].at[q_me]
            out.append(pltpu.make_async_copy(local_src, dst, local_sems.at[a]))
            for km in range(1, N_CHIP):
                px, py, q_peer = _chip_peer(x, y, km)
                out.append(pltpu.make_async_remote_copy(
                    src_ref=srcs[a] if self.gather else srcs[a].at[q_peer], dst_ref=dst, send_sem=send_sems.at[a, km],
                    recv_sem=recv_sems.at[a, km], device_id=(px, py, c), device_id_type=MESH))
        return out

    def standalone(self, name):
        n = len(self.arrays)

        def body(*refs):
            copies = self.copies(refs[:n], refs[n:2 * n], *refs[2 * n:])
            for cp in copies:
                cp.start()
            _wait_all(copies)

        return pl.pallas_call(body, name=name, in_specs=[_HBM] * n, out_specs=[_HBM] * n, out_shape=self.out_shape,
                              scratch_shapes=self.scratch_shapes)(*self.arrays)


def _gridded_call(body, *, name, grid, in_specs, out_specs, out_shape, scratch_shapes, args, exchange=None, prefetch=()):
    params = pltpu.CompilerParams(dimension_semantics=("arbitrary",) * len(grid))
    n_pre, n_in, n_out, n_scr = len(prefetch), len(in_specs), len(out_specs), len(scratch_shapes)
    n_x = 0 if exchange is None else len(exchange.arrays)

    def carrier(*refs):
        pre, refs = refs[:n_pre], refs[n_pre:]
        ins, x_src, refs = refs[:n_in], refs[n_in:n_in + n_x], refs[n_in + n_x:]
        outs, x_dst, refs = refs[:n_out], refs[n_out:n_out + n_x], refs[n_out + n_x:]
        copies = exchange.copies(x_src, x_dst, *refs[n_scr:])
        pids = [pl.program_id(d) for d in range(len(grid))]
        first = functools.reduce(jnp.logical_and, [p == 0 for p in pids])
        last = functools.reduce(jnp.logical_and, [p == g - 1 for p, g in zip(pids, grid)])

        @pl.when(first)
        def _():
            for cp in copies:
                cp.start()

        body(*pre, *ins, *outs, *refs[:n_scr])

        @pl.when(last)
        def _():
            _wait_all(copies)

    x_shapes, x_scratch, x_args = ([], [], []) if exchange is None else (exchange.out_shape, exchange.scratch_shapes, exchange.arrays)
    spec = pltpu.PrefetchScalarGridSpec(
        num_scalar_prefetch=n_pre, grid=grid, in_specs=list(in_specs) + [_HBM] * n_x, out_specs=list(out_specs) + [_HBM] * n_x,
        scratch_shapes=list(scratch_shapes) + x_scratch)
    in_place = {n_pre + n_in + k: n_out + k for k in range(n_x)} if n_x and exchange.aliased else {}
    res = pl.pallas_call(body if exchange is None else carrier, name=name, grid_spec=spec, out_shape=list(out_shape) + x_shapes,
                         input_output_aliases=in_place, compiler_params=params)(*prefetch, *args, *x_args)
    return res[:n_out], (None if exchange is None else res[n_out:])


class _CoreExchange:
    def __init__(self, gather, arrays):
        self.gather, self.arrays, self.aliased = gather, list(arrays), gather
        n = len(self.arrays)
        self.out_shape = [_sds(a.shape, a.dtype) if gather else _sds((N_CHIP, a.shape[0] // N_DEV) + a.shape[1:], a.dtype)
                          for a in self.arrays]
        self.scratch_shapes = [pltpu.SemaphoreType.DMA((n, N_CHIP)), pltpu.SemaphoreType.DMA((n, N_CHIP))]

    def copies(self, srcs, dsts, send_sems, recv_sems):
        x, y, c, _ = _my_position()
        out = []
        for a in range(len(self.arrays)):
            r = srcs[a].shape[0] // N_DEV
            for q in range(N_CHIP):
                if self.gather:
                    rows = pl.ds((2 * q + c) * r, r)
                    src, dst = srcs[a].at[rows], dsts[a].at[rows]
                else:
                    src, dst = srcs[a].at[pl.ds((2 * q + 1 - c) * r, r)], dsts[a].at[q]
                out.append(pltpu.make_async_remote_copy(src_ref=src, dst_ref=dst, send_sem=send_sems.at[a, q],
                                                        recv_sem=recv_sems.at[a, q], device_id=(x, y, 1 - c), device_id_type=MESH))
        return out

    def standalone(self, name):
        n = len(self.arrays)

        def body(*refs):
            copies = self.copies(refs[:n], refs[n:2 * n], *refs[2 * n:])
            for cp in copies:
                cp.start()
            _wait_all(copies)

        return pl.pallas_call(body, name=name, in_specs=[_HBM] * n, out_specs=[_HBM] * n, out_shape=self.out_shape,
                              input_output_aliases={a: a for a in range(n)} if self.aliased else {},
                              scratch_shapes=self.scratch_shapes)(*self.arrays)


def _gather_over_cores(arrays, name):
    return _CoreExchange(True, arrays).standalone(name)


def _gather_both_stages(pieces, name):
    n_arr = len(pieces)

    def body(*refs):
        srcs, dsts = refs[:n_arr], refs[n_arr:2 * n_arr]
        send_sems, recv_sems, local_sems = refs[2 * n_arr:]
        x, y, c, me = _my_position()
        sibling = (x, y, 1 - c)

        def copy(a, k, piece, to, from_src=False):
            r = srcs[a].shape[0]
            rows = dsts[a].at[pl.ds(piece * r, r)]
            return pltpu.make_async_remote_copy(src_ref=srcs[a] if from_src else rows, dst_ref=rows, send_sem=send_sems.at[a, k],
                                                recv_sem=recv_sems.at[a, k], device_id=to, device_id_type=MESH)

        local, sent = [], []
        for a in range(n_arr):
            r = srcs[a].shape[0]
            local.append(pltpu.make_async_copy(srcs[a], dsts[a].at[pl.ds(me * r, r)], local_sems.at[a]))
            sent.append(copy(a, 0, me, sibling, from_src=True))
            for km in range(1, N_CHIP):
                px, py, _ = _chip_peer(x, y, km)
                sent.append(copy(a, km, me, (px, py, c), from_src=True))
        for cp in local + sent:
            cp.start()
        for a in range(n_arr):
            for km in range(1, N_CHIP):
                px, py, q = _chip_peer(x, y, km)
                copy(a, km, 2 * q + c, (px, py, c)).wait_recv()
                passed = copy(a, N_CHIP - 1 + km, 2 * q + c, sibling)
                passed.start()
                sent.append(passed)
        for a in range(n_arr):
            copy(a, 0, 4 * x + 2 * y + 1 - c, sibling).wait_recv()
            for km in range(1, N_CHIP):
                _, _, q = _chip_peer(x, y, km)
                copy(a, N_CHIP - 1 + km, 2 * q + 1 - c, sibling).wait_recv()
        for cp in sent:
            cp.wait_send()
        for cp in local:
            cp.wait()

    n_sem = 2 * N_CHIP - 1
    return pl.pallas_call(
        body, name=name, in_specs=[_HBM] * n_arr, out_specs=[_HBM] * n_arr,
        out_shape=[_sds((N_DEV * p.shape[0],) + p.shape[1:], p.dtype) for p in pieces],
        scratch_shapes=[pltpu.SemaphoreType.DMA((n_arr, n_sem)), pltpu.SemaphoreType.DMA((n_arr, n_sem)),
                        pltpu.SemaphoreType.DMA((n_arr,))],
    )(*pieces)


def _grads_to_sibling(grads, name):
    return _CoreExchange(False, grads).standalone(name)


def _pair_sum(grad, from_sibling, name):
    r, n = from_sibling.shape[1:]

    def body(g_ref, s_ref, o_ref):
        c = lax.axis_index("c")
        o_ref[...] = (g_ref[c].astype(F32) + s_ref[...].astype(F32)).astype(BF16)

    return pl.pallas_call(
        body, name=name, grid=(N_CHIP,),
        in_specs=[pl.BlockSpec((None, 2, r, n), lambda q: (q, 0, 0, 0)), pl.BlockSpec((None, r, n), lambda q: (q, 0, 0))],
        out_specs=pl.BlockSpec((None, r, n), lambda q: (q, 0, 0)), out_shape=_sds((N_CHIP, r, n), BF16),
    )(grad.reshape(N_CHIP, 2, r, n), from_sibling)


def _all_reduce_small(vec):
    r = vec.shape[0]

    def body(v_ref, o_ref, buf, send_sems, recv_sems):
        x, y, c, me = _my_position()
        buf[me] = v_ref[...]
        copies = []
        for mask in range(1, N_DEV):
            peer, _ = _peer(x, y, c, mask)
            cp = pltpu.make_async_remote_copy(src_ref=v_ref, dst_ref=buf.at[me], send_sem=send_sems.at[mask],
                                              recv_sem=recv_sems.at[mask], device_id=peer, device_id_type=MESH)
            cp.start()
            copies.append(cp)
        for cp in copies:
            cp.wait()
        total = buf[0]
        for s in range(1, N_DEV):
            total = total + buf[s]
        o_ref[...] = total

    vm = pl.BlockSpec(memory_space=pltpu.VMEM)
    return pl.pallas_call(
        body, name="all_reduce_small", in_specs=[vm], out_specs=vm, out_shape=_sds(vec.shape, F32),
        scratch_shapes=[pltpu.VMEM((N_DEV, r, 128), F32), pltpu.SemaphoreType.DMA((N_DEV,)), pltpu.SemaphoreType.DMA((N_DEV,))],
    )(vec)


def _adamw_math(w, g, m, v):
    m = ADAM_B1 * m + (1.0 - ADAM_B1) * g
    v = ADAM_B2 * v + (1.0 - ADAM_B2) * (g * g)
    m_hat = m / (1.0 - ADAM_B1 ** ADAM_STEP)
    v_hat = v / (1.0 - ADAM_B2 ** ADAM_STEP)
    delta = -ADAM_LR * (m_hat / (jnp.sqrt(v_hat) + ADAM_EPS) + ADAM_WD * w)
    return delta, m, v


def _reduce_adamw(slots, w, m, v, *, transpose, name):
    r, n = slots.shape[1:]
    padded = not transpose and w.shape[0] != r

    def body(s_ref, w_ref, m_ref, v_ref, g_ref, d_ref, nm_ref, nv_ref, *scr):
        g = s_ref[0].astype(F32)
        for s in range(1, slots.shape[0]):
            g = g + s_ref[s].astype(F32)
        if transpose:
            scr[0][...] = g.T
            g = scr[0][:, 0:w_ref.shape[1]]
        elif padded:
            scr[0][...] = g
            g = scr[0][0:w_ref.shape[0], :]
        g_ref[...] = g
        d_ref[...], nm_ref[...], nv_ref[...] = _adamw_math(w_ref[...], g, m_ref[...], v_ref[...])

    out = _sds(w.shape, F32)
    if not transpose and not padded and r % (2 * BF16_ROW_TILE) == 0:
        half = pl.BlockSpec((r // 2, n), lambda i: (i, 0))
        return pl.pallas_call(
            body, name=name, grid=(2,), in_specs=[pl.BlockSpec((slots.shape[0], r // 2, n), lambda i: (0, i, 0)), half, half, half],
            out_specs=[half] * 4, out_shape=[out, out, out, out])(slots, w, m, v)
    return pl.pallas_call(
        body, name=name, out_shape=[out, out, out, out],
        scratch_shapes=[pltpu.VMEM((n, r) if transpose else (r, n), F32)] if transpose or padded else [],
    )(slots, w, m, v)


def _adamw(g, w, m, v, name):
    def body(g_ref, w_ref, m_ref, v_ref, d_ref, nm_ref, nv_ref):
        d_ref[...], nm_ref[...], nv_ref[...] = _adamw_math(w_ref[...], g_ref[...], m_ref[...], v_ref[...])

    out = _sds(w.shape, F32)
    return pl.pallas_call(body, name=name, out_shape=[out, out, out])(g, w, m, v)


_SMALL = ["ffn1_norm", "mix_norm", "ffn2_norm", "mla_q_lat_norm", "mla_kv_lat_norm", "mla_q_nope_gain", "mla_q_rope_gain",
          "mla_k_nope_gain", "mla_k_rope_gain", "fox_q_gain", "fox_k_gain", "fox_b_f"]
_WEIGHTS = ["ffn1_norm", "ffn1_w_gate", "ffn1_w_up", "ffn1_w_down", "mix_norm", "w_in", "mla_q_lat_norm", "mla_w_qb",
            "mla_kv_lat_norm", "mla_w_kvb", "mla_q_nope_gain", "mla_q_rope_gain", "mla_k_nope_gain", "mla_k_rope_gain",
            "fox_q_gain", "fox_k_gain", "fox_b_f", "w_branch_mla", "w_branch_fox", "b_gate", "w_o", "ffn2_norm",
            "ffn2_w_gate", "ffn2_w_up", "ffn2_w_down"]
_IN_Q, _IN_KV, _IN_KR, _IN_FOX, _IN_F, _IN_GATES = (0, 192), (192, 128), (320, 32), (352, 1536), (1888, 8), (1896, 2048)


def _rows(a, seg):
    return a[seg[0]:seg[0] + seg[1]]


def _split_w_in(win_t):
    z = lambda n: jnp.zeros((n, D_MODEL), win_t.dtype)
    lat = jnp.concatenate([_rows(win_t, _IN_Q), z(LAT_KV - Q_LORA), _rows(win_t, _IN_KV), _rows(win_t, _IN_KR),
                           _rows(win_t, _IN_F), z(LAT_W - LAT_F - N_HEADS)], axis=0)
    return _rows(win_t, _IN_GATES), _rows(win_t, _IN_FOX), lat


def _join_w_in(d_gates, d_fox, d_lat):
    return jnp.concatenate([d_lat[LAT_Q:LAT_Q + Q_LORA], d_lat[LAT_KV:LAT_KV + KV_LORA], d_lat[LAT_KR:LAT_KR + ROPE_DIM],
                            d_fox, d_lat[LAT_F:LAT_F + N_HEADS], d_gates], axis=0)


def kernel(x, positions, ffn1_norm, ffn1_w_gate, ffn1_w_up, ffn1_w_down, mix_norm, w_in, mla_q_lat_norm, mla_w_qb, mla_kv_lat_norm, mla_w_kvb, mla_q_nope_gain, mla_q_rope_gain, mla_k_nope_gain, mla_k_rope_gain, fox_q_gain, fox_k_gain, fox_b_f, w_branch_mla, w_branch_fox, b_gate, w_o, ffn2_norm, ffn2_w_gate, ffn2_w_up, ffn2_w_down, loss_target, m_ffn1_norm, m_ffn1_w_gate, m_ffn1_w_up, m_ffn1_w_down, m_mix_norm, m_w_in, m_mla_q_lat_norm, m_mla_w_qb, m_mla_kv_lat_norm, m_mla_w_kvb, m_mla_q_nope_gain, m_mla_q_rope_gain, m_mla_k_nope_gain, m_mla_k_rope_gain, m_fox_q_gain, m_fox_k_gain, m_fox_b_f, m_w_branch_mla, m_w_branch_fox, m_b_gate, m_w_o, m_ffn2_norm, m_ffn2_w_gate, m_ffn2_w_up, m_ffn2_w_down, v_ffn1_norm, v_ffn1_w_gate, v_ffn1_w_up, v_ffn1_w_down, v_mix_norm, v_w_in, v_mla_q_lat_norm, v_mla_w_qb, v_mla_kv_lat_norm, v_mla_w_kvb, v_mla_q_nope_gain, v_mla_q_rope_gain, v_mla_k_nope_gain, v_mla_k_rope_gain, v_fox_q_gain, v_fox_k_gain, v_fox_b_f, v_w_branch_mla, v_w_branch_fox, v_b_gate, v_w_o, v_ffn2_norm, v_ffn2_w_gate, v_ffn2_w_up, v_ffn2_w_down):
    env = dict(locals())
    strip = lambda n, a: a if n in _SMALL else a[0]
    W = {n: strip(n, env[n]) for n in _WEIGHTS}
    M = {n: strip(n, env["m_" + n]) for n in _WEIGHTS}
    V = {n: strip(n, env["v_" + n]) for n in _WEIGHTS}
    xs = x[0]
    t = xs.shape[0]

    col_split = ["ffn1_w_gate", "ffn1_w_up", "ffn2_w_gate", "ffn2_w_up", "mla_w_qb", "mla_w_kvb", "w_branch_mla", "w_branch_fox"]
    row_split = ["ffn1_w_down", "ffn2_w_down", "w_o"]
    pieces = {n: W[n].T.astype(BF16) for n in col_split}
    pieces.update({n: W[n].astype(BF16) for n in row_split})
    pieces["w_in"] = jnp.pad(W["w_in"].T.astype(BF16), ((0, W_IN_PIECE_PAD - W_IN_PIECE), (0, 0)))
    pieces["b_gate"] = W["b_gate"].T
    group_a = ["ffn1_w_gate", "ffn1_w_up", "ffn1_w_down"]
    group_b = ["w_in", "mla_w_qb", "mla_w_kvb", "w_branch_mla", "w_branch_fox", "w_o", "b_gate"]
    group_c = ["ffn2_w_gate", "ffn2_w_up", "ffn2_w_down"]
    gather = lambda group: _ChipExchange(True, [pieces[n] for n in group])
    G = dict(zip(group_a, _gather_both_stages([pieces[n] for n in group_a], "gather_a")))

    inv_freq = ROPE_THETA ** (-jnp.arange(ROPE_HALF, dtype=F32) / ROPE_HALF)
    ang = positions[0].astype(F32)[:, None] * inv_freq
    tables, mats = _prep_tables(jnp.cos(ang), jnp.sin(ang)), _group_matrices()
    gains = _head_gains(*[W[n] for n in ["mla_q_nope_gain", "mla_q_rope_gain", "mla_k_nope_gain", "mla_k_rope_gain",
                                         "fox_q_gain", "fox_k_gain"]])

    (x1, a1, b1), got_b = _ffn_fwd(xs, W["ffn1_norm"], G["ffn1_w_gate"], G["ffn1_w_up"], G["ffn1_w_down"],
                                   exchange=gather(group_b))
    G.update(zip(group_b, _gather_over_cores(got_b, "gather_d2d_b")))
    win_t = G["w_in"].reshape(N_DEV, W_IN_PIECE_PAD, D_MODEL)[:, :W_IN_PIECE].reshape(N_DEV * W_IN_PIECE, D_MODEL)
    wgate_t, wfox_t, wlat_t = _split_w_in(win_t)
    bg = G["b_gate"].T
    gates, fox, lat = _proj_fwd(x1, W["mix_norm"], wgate_t, wfox_t, wlat_t)
    prep_args = (fox, lat, tables, mats, *_interleave_weights(G["mla_w_qb"], G["mla_w_kvb"]), W["mla_q_lat_norm"],
                 W["mla_kv_lat_norm"], gains, W["fox_b_f"])
    fq, fk, fv, c, mq, mk, mv, fq_sq, fk_sq_max, mq_sq, mk_sq_max, fv1, mv1 = _prep_fwd(*prep_args, *_value_layout(prep_args[6]))
    c_t = c.T
    b_fox, ok_fox = _logit_bound(fq_sq, fk_sq_max)
    b_mla, ok_mla = _logit_bound(mq_sq, mk_sq_max)
    (y_fox, y_fox32, lse_fox), got_c = _attn_fwd(fq, fk, fv1, b_fox, ok_fox, c, c_t, lanes=HEAD_DIM, name="fox_fwd",
                                                 exchange=gather(group_c))
    (y_mla, y_mla32, lse_mla), got_c = _attn_fwd(mq, mk, mv1, b_mla, ok_mla, None, None, lanes=MLA_QK_LANES, name="mla_fwd",
                                                 exchange=_CoreExchange(True, got_c))
    G.update(zip(group_c, got_c))
    x2 = _mix_fwd(x1, y_mla, y_fox, gates, bg, G["w_branch_mla"], G["w_branch_fox"], G["w_o"])
    (dx3, a2, b2, loss_vec), _ = _ffn_fwd(x2, W["ffn2_norm"], G["ffn2_w_gate"], G["ffn2_w_up"], G["ffn2_w_down"],
                                          target=loss_target[0])

    def chip_sums(group, tag, from_sibling=None):
        if from_sibling is None:
            from_sibling = _grads_to_sibling([grads[n] for n in group], "grads_d2d_" + tag)
        return _ChipExchange(False, [_pair_sum(grads[n], s, "pair_sum_" + n) for n, s in zip(group, from_sibling)])

    (dx2, dg_ffn2, da2, db2, h2, n2, dyh2), _ = _ffn_bwd(dx3, x2, W["ffn2_norm"], a2, b2, G["ffn2_w_gate"], G["ffn2_w_up"],
                                                        G["ffn2_w_down"], "ffn2_bwd")
    grads = {"ffn2_w_gate": _tn_matmul(da2, n2, "ffn2_dgate"), "ffn2_w_up": _tn_matmul(db2, n2, "ffn2_dup"),
             "ffn2_w_down": _tn_matmul(h2, dyh2, "ffn2_ddown")}
    (dy_mla, dy_fox, delta_mla, delta_fox, dgates, mixed, dum, duf, dx2b, dbg), from_sibling_c = _mix_bwd(
        dx2, y_mla, y_fox, y_mla32, y_fox32, gates, bg, G["w_branch_mla"], G["w_branch_fox"], G["w_o"], mats[3],
        exchange=_CoreExchange(False, [grads[n] for n in group_c]))
    delta_mla, delta_fox = delta_mla[:, :N_HEADS], delta_fox[:, :N_HEADS]
    grads["w_o"] = _tn_matmul(mixed, dx2b, "d_w_o")
    grads["w_branch_mla"] = _tn_matmul(dum, y_mla, "d_w_branch_mla")
    grads["w_branch_fox"] = _tn_matmul(duf, y_fox, "d_w_branch_fox")
    (dfq, dfk, dfv, dc_t), slots_c = _attn_bwd(fq, fk, fv, dy_fox, delta_fox, lse_fox, c, c_t, lanes=HEAD_DIM,
                                               name="fox_bwd", exchange=chip_sums(group_c, "c", from_sibling_c))
    slots = dict(zip(group_c, slots_c))
    (dmq, dmk, dmv), _ = _attn_bwd(mq, mk, mv, dy_mla, delta_mla, lse_mla, None, None, lanes=MLA_QK_LANES, name="mla_bwd")
    dfox, dlat, dwq, dwk, dwv, d_gql, d_gkvl, d_gq, d_gk, d_gkr, d_gfq, d_gfk, d_bf = _prep_bwd(
        *prep_args, dfq, dfk, dfv, dc_t.T, dmq, dmk, dmv)
    grads["mla_w_qb"], grads["mla_w_kvb"] = _deinterleave_grads(dwq, dwk, dwv)
    fold = lambda a, width: a.reshape(N_HEADS, width).sum(axis=0)[None]
    d_gq, d_gk = fold(d_gq, LANES), fold(d_gk, LANES)
    d_prep_small = [d_gql, d_gkvl, d_gq[:, :HEAD_DIM], d_gq[:, HEAD_DIM:HEAD_DIM + ROPE_DIM], d_gk[:, :HEAD_DIM],
                    d_gkr[:, HEAD_DIM:HEAD_DIM + ROPE_DIM], fold(d_gfq, HEAD_DIM), fold(d_gfk, HEAD_DIM), d_bf]
    dx1, dg_mix, nmix = _proj_bwd(dgates, dfox, dlat, wgate_t, wfox_t, wlat_t, x1, W["mix_norm"], dx2)
    dwin_t = _join_w_in(_tn_matmul(dgates, nmix, "d_w_in_gates"), _tn_matmul(dfox, nmix, "d_w_in_fox"),
                        _tn_matmul(dlat, nmix, "d_w_in_lat"))
    grads["w_in"] = jnp.pad(dwin_t.reshape(N_DEV, W_IN_PIECE, D_MODEL), ((0, 0), (0, W_IN_PIECE_PAD - W_IN_PIECE), (0, 0))
                            ).reshape(N_DEV * W_IN_PIECE_PAD, D_MODEL)
    grad_group_b = [n for n in group_b if n != "b_gate"]
    (dx0, dg_ffn1, da1, db1, h1, n1, dyh1), slots_b = _ffn_bwd(dx1, xs, W["ffn1_norm"], a1, b1, G["ffn1_w_gate"], G["ffn1_w_up"],
                                                              G["ffn1_w_down"], "ffn1_bwd", exchange=chip_sums(grad_group_b, "b"))
    slots.update(zip(grad_group_b, slots_b))
    grads["ffn1_w_gate"] = _tn_matmul(da1, n1, "ffn1_dgate")
    grads["ffn1_w_up"], got = _tn_matmul(db1, n1, "ffn1_dup", exchange=chip_sums(["ffn1_w_gate"], "a_gate"))
    slots["ffn1_w_gate"] = got[0]
    grads["ffn1_w_down"], got = _tn_matmul(h1, dyh1, "ffn1_ddown", exchange=chip_sums(["ffn1_w_up"], "a_up"))
    slots["ffn1_w_up"] = got[0]
    slots["ffn1_w_down"] = chip_sums(["ffn1_w_down"], "a_down").standalone("grads_ici_a_down")[0]

    small_parts = [dg_ffn1, dg_mix, dg_ffn2] + list(d_prep_small) + [dbg.reshape(1, 2 * D_MODEL), loss_vec]
    flat = jnp.concatenate([p.reshape(-1) for p in small_parts])
    n_flat = flat.shape[0]
    rows = -(-n_flat // (8 * 128)) * 8
    total = _all_reduce_small(jnp.pad(flat, (0, rows * 128 - n_flat)).reshape(rows, 128)).reshape(-1)
    offs, small_g = 0, {}
    for n in _SMALL:
        small_g[n] = total[offs:offs + W[n].shape[1]].reshape(W[n].shape)
        offs += W[n].shape[1]
    bg_full = total[offs:offs + 2 * D_MODEL].reshape(2, D_MODEL)
    offs += 2 * D_MODEL
    loss = (0.5 / D_MODEL) * jnp.sum(total[offs:offs + D_MODEL])
    _, _, _, me = _my_position()
    small_g["b_gate"] = lax.dynamic_slice_in_dim(bg_full, me * (D_MODEL // N_DEV), D_MODEL // N_DEV, axis=1)

    transposed_in_memory = ["ffn1_w_gate", "ffn1_w_up", "ffn2_w_gate", "ffn2_w_up", "w_in"]
    res = {}
    for n in _WEIGHTS:
        if n in small_g:
            res[n] = (small_g[n],) + tuple(_adamw(small_g[n], W[n], M[n], V[n], "adamw_" + n))
        elif n in transposed_in_memory:
            res[n] = tuple(o.T for o in _reduce_adamw(slots[n], W[n].T, M[n].T, V[n].T, transpose=False, name="adamw_" + n))
        else:
            res[n] = tuple(_reduce_adamw(slots[n], W[n], M[n], V[n], transpose=n in col_split, name="adamw_" + n))
    outs = [loss, dx0[None]]
    for k in range(4):
        outs += [res[n][k] if n in _SMALL else res[n][k][None] for n in _WEIGHTS]
    return tuple(outs)
```

```python
import functools

import jax
import jax.numpy as jnp
from jax import lax
from jax.experimental import pallas as pl
from jax.experimental.pallas import tpu as pltpu

F32 = jnp.float32
BF16 = jnp.bfloat16

D_MODEL = 1024
FFN_HIDDEN = 2816
N_HEADS = 8
HEAD_DIM = 64
ROPE_DIM = 32
ROPE_HALF = 16
Q_LORA = 192
KV_LORA = 128
ROPE_THETA = 10000.0
RMS_EPS = 1e-6
MLA_SCALE = (HEAD_DIM + ROPE_DIM) ** -0.5
FOX_SCALE = HEAD_DIM ** -0.5
MLA_QK_LANES = 128
ADAM_LR, ADAM_B1, ADAM_B2, ADAM_EPS, ADAM_WD, ADAM_STEP = 0.001, 0.9, 0.999, 1e-08, 0.01, 10
N_DEV = 8
N_CHIP = 4
W_IN_PIECE = 493
BF16_ROW_TILE = 16
W_IN_PIECE_PAD = 496
LAT_W = 512
LAT_Q, LAT_KV, LAT_KR, LAT_F = 0, 256, 384, 416
MASK_VALUE = -1e30
FIXED_SHIFT_MAX_BOUND = 30.0

TOK_TILE = 512
DW_TOK_TILE = 1024
DW_SPLIT_ROWS = 2048
PROJ_TILE = 512
PREP_TILE = 256
ATT_TILE = 512
ATT_COL_CHUNK = 256
FFN_HID_TILE = 1408
FFN_HID_SPLIT = ((0, 768), (768, 1408))

NT = (((1,), (1,)), ((), ()))
TN = (((0,), (0,)), ((), ()))
NN = (((1,), (0,)), ((), ()))
MESH = pl.DeviceIdType.MESH


def _dot(a, b, dims=NN):
    return lax.dot_general(a, b, dims, preferred_element_type=F32)


def _sds(shape, dtype):
    return jax.ShapeDtypeStruct(shape, dtype)


def _rms_fwd(x, g):
    r = lax.rsqrt(jnp.mean(x * x, axis=-1, keepdims=True) + RMS_EPS)
    return x * r * g, r


def _rms_bwd(dy, x, g, r):
    xn = x * r
    dyg = dy * g
    dx = r * (dyg - xn * jnp.mean(dyg * xn, axis=-1, keepdims=True))
    return dx, dy * xn


def _colsum(x):
    return jnp.sum(x, axis=0, keepdims=True)


def _full(shape):
    return pl.BlockSpec(shape, lambda *_: (0,) * len(shape))


def _tok(tm, n):
    return pl.BlockSpec((tm, n), lambda i, *_: (i, 0))


def _ffn_fwd(x, gain, wg_t, wu_t, wd, target=None, exchange=None):
    t = x.shape[0]
    tm = min(TOK_TILE, t)
    tf = FFN_HID_TILE
    n_t, n_f = t // tm, FFN_HIDDEN // tf
    with_loss = target is not None

    def body(*refs):
        if with_loss:
            x_ref, g_ref, wg_ref, wu_ref, wd_ref, t_ref, out_ref, a_ref, b_ref, lvec_ref, n_scr, acc = refs
        else:
            x_ref, g_ref, wg_ref, wu_ref, wd_ref, out_ref, a_ref, b_ref, n_scr, acc = refs
        i, j = pl.program_id(0), pl.program_id(1)

        @pl.when(j == 0)
        def _():
            xn, _ = _rms_fwd(x_ref[...], g_ref[...])
            n_scr[...] = xn.astype(BF16)
            acc[...] = jnp.zeros_like(acc)

        n = n_scr[...]
        a = _dot(n, wg_ref[...], NT)
        b = _dot(n, wu_ref[...], NT)
        a_ref[...] = a.astype(BF16)
        b_ref[...] = b.astype(BF16)
        h = (a * jax.nn.sigmoid(a)) * b
        acc[...] += _dot(h.astype(BF16), wd_ref[...])

        @pl.when(j == n_f - 1)
        def _():
            y = x_ref[...] + 0.5 * acc[...]
            if with_loss:
                diff = y - t_ref[...]
                out_ref[...] = diff * (1.0 / D_MODEL)
                sq = _colsum(diff * diff)

                @pl.when(i == 0)
                def _():
                    lvec_ref[...] = sq

                @pl.when(i > 0)
                def _():
                    lvec_ref[...] += sq
            else:
                out_ref[...] = y

    wspec = pl.BlockSpec((tf, D_MODEL), lambda i, j: (j, 0))
    hspec = pl.BlockSpec((tm, tf), lambda i, j: (i, j))
    in_specs = [_tok(tm, D_MODEL), _full((1, D_MODEL)), wspec, wspec, wspec]
    out_specs = [_tok(tm, D_MODEL), hspec, hspec]
    out_shape = [_sds((t, D_MODEL), F32), _sds((t, FFN_HIDDEN), BF16), _sds((t, FFN_HIDDEN), BF16)]
    args = [x, gain, wg_t, wu_t, wd]
    if with_loss:
        in_specs.append(_tok(tm, D_MODEL))
        out_specs.append(_full((1, D_MODEL)))
        out_shape.append(_sds((1, D_MODEL), F32))
        args.append(target)
    return _gridded_call(
        body, name="ffn_fwd_loss" if with_loss else "ffn_fwd", grid=(n_t, n_f), in_specs=in_specs, out_specs=out_specs,
        out_shape=out_shape, scratch_shapes=[pltpu.VMEM((tm, D_MODEL), BF16), pltpu.VMEM((tm, D_MODEL), F32)],
        args=args, exchange=exchange)


def _ffn_bwd(dy, x, gain, a, b, wg_t, wu_t, wd, name, exchange=None):
    t = x.shape[0]
    tm = min(TOK_TILE, t)
    tf = FFN_HID_TILE
    n_t, n_f = t // tm, FFN_HIDDEN // tf

    def body(dy_ref, x_ref, g_ref, a_ref, b_ref, wg_ref, wu_ref, wd_ref,
             dx_ref, dg_ref, da_ref, db_ref, h_ref, n_ref, dyh_ref, acc):
        i, j = pl.program_id(0), pl.program_id(1)

        @pl.when(j == 0)
        def _():
            xn, _ = _rms_fwd(x_ref[...], g_ref[...])
            n_ref[...] = xn.astype(BF16)
            dyh_ref[...] = (0.5 * dy_ref[...]).astype(BF16)
            acc[...] = jnp.zeros_like(acc)

        dyh = dyh_ref[...]
        for lo, hi in FFN_HID_SPLIT:
            dh = _dot(dyh, wd_ref[lo:hi, :], NT)
            av = a_ref[:, lo:hi].astype(F32)
            bv = b_ref[:, lo:hi].astype(F32)
            s = jax.nn.sigmoid(av)
            silu = av * s
            da = (dh * bv * (s * (1.0 + av * (1.0 - s)))).astype(BF16)
            db = (dh * silu).astype(BF16)
            da_ref[:, lo:hi] = da
            db_ref[:, lo:hi] = db
            h_ref[:, lo:hi] = (silu * bv).astype(BF16)
            acc[...] += _dot(da, wg_ref[lo:hi, :]) + _dot(db, wu_ref[lo:hi, :])

        @pl.when(j == n_f - 1)
        def _():
            xv, g = x_ref[...], g_ref[...]
            r = lax.rsqrt(jnp.mean(xv * xv, axis=-1, keepdims=True) + RMS_EPS)
            dx, dg_rows = _rms_bwd(acc[...], xv, g, r)
            dx_ref[...] = dy_ref[...] + dx
            dg = _colsum(dg_rows)

            @pl.when(i == 0)
            def _():
                dg_ref[...] = dg

            @pl.when(i > 0)
            def _():
                dg_ref[...] += dg

    wspec = pl.BlockSpec((tf, D_MODEL), lambda i, j: (j, 0))
    hspec = pl.BlockSpec((tm, tf), lambda i, j: (i, j))
    tok = _tok(tm, D_MODEL)
    hid = _sds((t, FFN_HIDDEN), BF16)
    return _gridded_call(
        body, name=name, grid=(n_t, n_f),
        in_specs=[tok, tok, _full((1, D_MODEL)), hspec, hspec, wspec, wspec, wspec],
        out_specs=[tok, _full((1, D_MODEL)), hspec, hspec, hspec, tok, tok],
        out_shape=[_sds((t, D_MODEL), F32), _sds((1, D_MODEL), F32), hid, hid, hid,
                   _sds((t, D_MODEL), BF16), _sds((t, D_MODEL), BF16)],
        scratch_shapes=[pltpu.VMEM((tm, D_MODEL), F32)], args=[dy, x, gain, a, b, wg_t, wu_t, wd], exchange=exchange)


def _tn_matmul(a, b, name, exchange=None):
    t, m = a.shape
    n = b.shape[1]
    n_m = 2 if m >= DW_SPLIT_ROWS else 1
    tk = min(DW_TOK_TILE * n_m, t)
    n_k, tm = t // tk, m // n_m

    def body(a_ref, b_ref, o_ref, acc):
        k = pl.program_id(1)
        p = _dot(a_ref[...], b_ref[...], TN)

        @pl.when(k == 0)
        def _():
            acc[...] = p

        @pl.when(k > 0)
        def _():
            acc[...] += p

        @pl.when(k == n_k - 1)
        def _():
            o_ref[...] = acc[...].astype(BF16)

    (out,), got = _gridded_call(
        body, name=name, grid=(n_m, n_k),
        in_specs=[pl.BlockSpec((tk, tm), lambda i, k: (k, i)), pl.BlockSpec((tk, n), lambda i, k: (k, 0))],
        out_specs=[pl.BlockSpec((tm, n), lambda i, k: (i, 0))], out_shape=[_sds((m, n), BF16)],
        scratch_shapes=[pltpu.VMEM((tm, n), F32)], args=[a, b], exchange=exchange)
    return out if exchange is None else (out, got)


def _proj_fwd(x, gain, wgate_t, wfox_t, wlat_t):
    t = x.shape[0]
    tm = min(PROJ_TILE, t)

    def body(x_ref, g_ref, wg_ref, wf_ref, wl_ref, og_ref, of_ref, ol_ref):
        xn, _ = _rms_fwd(x_ref[...], g_ref[...])
        n = xn.astype(BF16)
        og_ref[...] = _dot(n, wg_ref[...], NT)
        of_ref[...] = _dot(n, wf_ref[...], NT)
        ol_ref[...] = _dot(n, wl_ref[...], NT)

    return pl.pallas_call(
        body, name="proj_fwd", grid=(t // tm,),
        in_specs=[_tok(tm, D_MODEL), _full((1, D_MODEL)), _full(wgate_t.shape), _full(wfox_t.shape), _full(wlat_t.shape)],
        out_specs=[_tok(tm, 2 * D_MODEL), _tok(tm, 3 * 512), _tok(tm, LAT_W)],
        out_shape=[_sds((t, 2 * D_MODEL), F32), _sds((t, 3 * 512), F32), _sds((t, LAT_W), F32)],
        compiler_params=pltpu.CompilerParams(dimension_semantics=("arbitrary",)),
    )(x, gain, wgate_t, wfox_t, wlat_t)


def _proj_bwd(dgates, dfox, dlat, wgate_t, wfox_t, wlat_t, x, gain, dres):
    t = x.shape[0]
    tm = min(PROJ_TILE, t)

    def body(dg_ref, df_ref, dl_ref, wg_ref, wf_ref, wl_ref, x_ref, g_ref, dres_ref, dx_ref, dgain_ref, n_ref):
        i = pl.program_id(0)
        dn = _dot(dg_ref[...], wg_ref[...]) + _dot(df_ref[...], wf_ref[...]) + _dot(dl_ref[...], wl_ref[...])
        xv, g = x_ref[...], g_ref[...]
        xn, r = _rms_fwd(xv, g)
        n_ref[...] = xn.astype(BF16)
        dx, dg_rows = _rms_bwd(dn, xv, g, r)
        dx_ref[...] = dres_ref[...] + dx
        dgn = _colsum(dg_rows)

        @pl.when(i == 0)
        def _():
            dgain_ref[...] = dgn

        @pl.when(i > 0)
        def _():
            dgain_ref[...] += dgn

    tok = _tok(tm, D_MODEL)
    return pl.pallas_call(
        body, name="proj_bwd", grid=(t // tm,),
        in_specs=[_tok(tm, 2 * D_MODEL), _tok(tm, 3 * 512), _tok(tm, LAT_W), _full(wgate_t.shape), _full(wfox_t.shape),
                  _full(wlat_t.shape), tok, _full((1, D_MODEL)), tok],
        out_specs=[tok, _full((1, D_MODEL)), tok],
        out_shape=[_sds((t, D_MODEL), F32), _sds((1, D_MODEL), F32), _sds((t, D_MODEL), BF16)],
        compiler_params=pltpu.CompilerParams(dimension_semantics=("arbitrary",)),
    )(dgates, dfox, dlat, wgate_t, wfox_t, wlat_t, x, gain, dres)


def _tri(n, lower):
    r = lax.broadcasted_iota(jnp.int32, (n, n), 0)
    c = lax.broadcasted_iota(jnp.int32, (n, n), 1)
    return ((c <= r) if lower else (c >= r)).astype(F32)


def _log_sigmoid(z):
    return jnp.minimum(z, 0.0) - jnp.log1p(jnp.exp(-jnp.abs(z)))


N_GROUPS = 16
LANES = MLA_QK_LANES


def _split_dot(a, g):
    hi = a.astype(BF16)
    lo = (a - hi.astype(F32)).astype(BF16)
    return _dot(hi, g) + _dot(lo, g)


def _rope_fwd(y, c, s1, s2):
    return y * c + pltpu.roll(y, LANES - ROPE_HALF, 1) * s1 + pltpu.roll(y, ROPE_HALF, 1) * s2


def _rope_bwd(do, c, s1, s2):
    return do * c + pltpu.roll(do * s1, ROPE_HALF, 1) + pltpu.roll(do * s2, LANES - ROPE_HALF, 1)


def _prep_tables(cos, sin):
    t = cos.shape[0]
    z = lambda n: jnp.zeros((t, n), F32)
    c = jnp.concatenate([jnp.ones((t, HEAD_DIM), F32), cos, cos, z(LANES - HEAD_DIM - ROPE_DIM)], axis=1)
    s1 = jnp.concatenate([z(HEAD_DIM), -sin, z(LANES - HEAD_DIM - ROPE_HALF)], axis=1)
    s2 = jnp.concatenate([z(HEAD_DIM + ROPE_HALF), sin, z(LANES - HEAD_DIM - ROPE_DIM)], axis=1)
    lane = jnp.arange(LANES)
    rope_mask = ((lane >= HEAD_DIM) & (lane < HEAD_DIM + ROPE_DIM)).astype(F32)[None, :]
    return c, s1, s2, rope_mask


def _group_matrices():
    lane = jnp.arange(N_HEADS * LANES)
    head, d = (lane // LANES)[:, None], (lane % LANES)[:, None]
    col = jnp.arange(N_GROUPS)[None, :]
    g_mla = ((col == head) & (d < HEAD_DIM)) | ((col == N_HEADS + head) & (d >= HEAD_DIM) & (d < HEAD_DIM + ROPE_DIM))
    g_fox = col == (jnp.arange(N_HEADS * HEAD_DIM) // HEAD_DIM)[:, None]
    inv_mla = jnp.concatenate([jnp.full((1, N_HEADS), 1.0 / HEAD_DIM, F32), jnp.full((1, N_HEADS), 1.0 / ROPE_DIM, F32)], axis=1)
    inv_fox = jnp.full((1, N_GROUPS), 1.0 / HEAD_DIM, F32)
    return g_mla.astype(BF16), g_mla.T.astype(BF16), inv_mla, g_fox.astype(BF16), g_fox.T.astype(BF16), inv_fox


def _interleave_weights(wqb_t, wkvb_t):
    wq = jnp.pad(wqb_t.reshape(N_HEADS, HEAD_DIM + ROPE_DIM, Q_LORA), ((0, 0), (0, LANES - HEAD_DIM - ROPE_DIM), (0, 0)))
    kv = wkvb_t.reshape(N_HEADS, 2, HEAD_DIM, KV_LORA)
    wk = jnp.pad(kv[:, 0], ((0, 0), (0, LANES - HEAD_DIM), (0, 0)))
    return wq.reshape(N_HEADS * LANES, Q_LORA), wk.reshape(N_HEADS * LANES, KV_LORA), kv[:, 1].reshape(N_HEADS * HEAD_DIM, KV_LORA)


def _deinterleave_grads(dwq, dwk, dwv):
    dq = dwq.reshape(N_HEADS, LANES, Q_LORA)[:, :HEAD_DIM + ROPE_DIM].reshape(N_HEADS * (HEAD_DIM + ROPE_DIM), Q_LORA)
    dk = dwk.reshape(N_HEADS, LANES, KV_LORA)[:, :HEAD_DIM]
    dkv = jnp.stack([dk, dwv.reshape(N_HEADS, HEAD_DIM, KV_LORA)], axis=1)
    return dq, dkv.reshape(N_HEADS * 2 * HEAD_DIM, KV_LORA)


def _head_gains(g_qn, g_qr, g_kn, g_kr, g_fq, g_fk):
    z = lambda n: jnp.zeros((1, n), F32)
    gq = jnp.concatenate([g_qn, g_qr, z(LANES - HEAD_DIM - ROPE_DIM)], axis=1)
    gk = jnp.concatenate([g_kn, z(LANES - HEAD_DIM)], axis=1)
    gkr = jnp.concatenate([z(HEAD_DIM), g_kr, z(LANES - HEAD_DIM - ROPE_DIM)], axis=1)
    return jnp.tile(gq, (1, N_HEADS)), jnp.tile(gk, (1, N_HEADS)), gkr, jnp.tile(g_fq, (1, N_HEADS)), jnp.tile(g_fk, (1, N_HEADS))


def _group_rms(x, g, g_t, inv):
    r = lax.rsqrt(_split_dot(x * x, g) * inv + RMS_EPS)
    return _split_dot(r, g_t)


def _value_layout(wv):
    lane = jnp.arange(N_HEADS * LANES)
    src = (lane // LANES) * HEAD_DIM + lane % LANES
    place = (jnp.arange(N_HEADS * HEAD_DIM)[:, None] == src[None, :]) & (lane % LANES < HEAD_DIM)[None, :]
    ones = (lane % LANES == HEAD_DIM).astype(F32)[None, :]
    wv_il = jnp.pad(wv.reshape(N_HEADS, HEAD_DIM, KV_LORA), ((0, 0), (0, LANES - HEAD_DIM), (0, 0)))
    return place.astype(BF16), ones, wv_il.reshape(N_HEADS * LANES, KV_LORA)


def _prep_fwd(fox, lat, tables, mats, wq, wk, wv, g_qlat, g_kvlat, gains, b_f, v_place, v_ones, wv_il):
    t = fox.shape[0]
    tm = min(PREP_TILE, t)
    c_tab, s1_tab, s2_tab, rope_mask = tables

    def body(fox_ref, lat_ref, c_ref, s1_ref, s2_ref, rm_ref, gm_ref, gmt_ref, im_ref, gf_ref, gft_ref, if_ref,
             wq_ref, wk_ref, wv_ref, gql_ref, gkvl_ref, gq_ref, gk_ref, gkr_ref, gfq_ref, gfk_ref, bf_ref,
             vp_ref, vo_ref, wvil_ref,
             fq_ref, fk_ref, fv_ref, cc_ref, mq_ref, mk_ref, mv_ref, fqn_ref, fkmax_ref, mqn_ref, mkmax_ref,
             fv1_ref, mv1_ref, carry):
        i = pl.program_id(0)
        ct, s1, s2 = c_ref[...], s1_ref[...], s2_ref[...]

        @pl.when(i == 0)
        def _():
            carry[...] = jnp.zeros_like(carry)
            fkmax_ref[...] = jnp.zeros_like(fkmax_ref)
            mkmax_ref[...] = jnp.zeros_like(mkmax_ref)

        def square_sums(ref, g):
            v = ref[...].astype(F32)
            return _split_dot(v * v, g)

        xq = fox_ref[:, 0:512]
        fq_ref[...] = (xq * _group_rms(xq, gf_ref[...], gft_ref[...], if_ref[...]) * (gfq_ref[...] * FOX_SCALE)).astype(BF16)
        xk = fox_ref[:, 512:1024]
        fk_ref[...] = (xk * _group_rms(xk, gf_ref[...], gft_ref[...], if_ref[...]) * gfk_ref[...]).astype(BF16)
        fv_b = fox_ref[:, 1024:1536].astype(BF16)
        fv_ref[...] = fv_b
        fv1_ref[...] = (_dot(fv_b, vp_ref[...]) + vo_ref[...]).astype(BF16)
        fqn_ref[...] = square_sums(fq_ref, gf_ref[...])
        fkmax_ref[...] = jnp.maximum(fkmax_ref[...], jnp.max(square_sums(fk_ref, gf_ref[...]), axis=0, keepdims=True))


        logf = _log_sigmoid(lat_ref[:, LAT_F:LAT_F + N_HEADS] + bf_ref[...])
        cc_ref[...] = jnp.dot(_tri(tm, True), logf, precision=lax.Precision.HIGHEST, preferred_element_type=F32) + carry[...]
        carry[...] += _colsum(logf)

        qlat_n, _ = _rms_fwd(lat_ref[:, LAT_Q:LAT_Q + Q_LORA], gql_ref[...])
        p = _dot(qlat_n.astype(BF16), wq_ref[...], NT)
        y = p * _group_rms(p, gm_ref[...], gmt_ref[...], im_ref[...]) * (gq_ref[...] * MLA_SCALE)
        for h in range(N_HEADS):
            w = slice(h * LANES, (h + 1) * LANES)
            mq_ref[:, w] = _rope_fwd(y[:, w], ct, s1, s2).astype(BF16)

        kv_n, _ = _rms_fwd(lat_ref[:, LAT_KV:LAT_KV + KV_LORA], gkvl_ref[...])
        kv_b = kv_n.astype(BF16)
        pk = _dot(kv_b, wk_ref[...], NT)
        kn = pk * _group_rms(pk, gm_ref[...], gmt_ref[...], im_ref[...]) * gk_ref[...]
        rm = rm_ref[...]
        kr = pltpu.roll(lat_ref[:, LAT_KR:LAT_KR + LANES], HEAD_DIM, 1) * rm
        rr = lax.rsqrt(jnp.sum(kr * kr, axis=1, keepdims=True) * (1.0 / ROPE_DIM) + RMS_EPS)
        okr = _rope_fwd(kr * rr * gkr_ref[...], ct * rm, s1, s2)
        for h in range(N_HEADS):
            w = slice(h * LANES, (h + 1) * LANES)
            mk_ref[:, w] = (kn[:, w] + okr).astype(BF16)
        mv_ref[...] = _dot(kv_b, wv_ref[...], NT).astype(BF16)
        mv1_ref[...] = (_dot(kv_b, wvil_ref[...], NT).astype(BF16).astype(F32) + vo_ref[...]).astype(BF16)
        mqn_ref[...] = square_sums(mq_ref, gm_ref[...])
        mkmax_ref[...] = jnp.maximum(mkmax_ref[...], jnp.max(square_sums(mk_ref, gm_ref[...]), axis=0, keepdims=True))

    consts = [rope_mask, *mats, wq, wk, wv, g_qlat, g_kvlat, *gains, b_f, v_place, v_ones, wv_il]
    t512, tl, tg, g1 = _tok(tm, 512), _tok(tm, LANES), _tok(tm, N_GROUPS), _full((1, N_GROUPS))
    t1024 = _tok(tm, N_HEADS * LANES)
    return pl.pallas_call(
        body, name="prep_fwd", grid=(t // tm,),
        in_specs=[_tok(tm, 1536), _tok(tm, LAT_W), tl, tl, tl] + [_full(a.shape) for a in consts],
        out_specs=[t512, t512, t512, _tok(tm, N_HEADS), t1024, t1024, t512, tg, g1, tg, g1, t1024, t1024],
        out_shape=[_sds((t, 512), BF16), _sds((t, 512), BF16), _sds((t, 512), BF16), _sds((t, N_HEADS), F32),
                   _sds((t, N_HEADS * LANES), BF16), _sds((t, N_HEADS * LANES), BF16), _sds((t, 512), BF16),
                   _sds((t, N_GROUPS), F32), _sds((1, N_GROUPS), F32), _sds((t, N_GROUPS), F32), _sds((1, N_GROUPS), F32)]
        + [_sds((t, N_HEADS * LANES), BF16)] * 2,
        scratch_shapes=[pltpu.VMEM((1, N_HEADS), F32)],
        compiler_params=pltpu.CompilerParams(dimension_semantics=("arbitrary",)),
    )(fox, lat, c_tab, s1_tab, s2_tab, *consts)


def _prep_bwd(fox, lat, tables, mats, wq, wk, wv, g_qlat, g_kvlat, gains, b_f, dfq, dfk, dfv, dc, dmq, dmk, dmv):
    t = fox.shape[0]
    tm = min(PREP_TILE, t)
    n_t = t // tm
    c_tab, s1_tab, s2_tab, rope_mask = tables

    def body(fox_ref, lat_ref, c_ref, s1_ref, s2_ref, rm_ref, gm_ref, gmt_ref, im_ref, gf_ref, gft_ref, if_ref,
             wq_ref, wk_ref, wv_ref, gql_ref, gkvl_ref, gq_ref, gk_ref, gkr_ref, gfq_ref, gfk_ref, bf_ref,
             dfq_ref, dfk_ref, dfv_ref, dc_ref, dmq_ref, dmk_ref, dmv_ref,
             dfox_ref, dlat_ref, dwq_ref, dwk_ref, dwv_ref, o_gql, o_gkvl, o_gq, o_gk, o_gkr, o_gfq, o_gfk, o_bf,
             carry, lscr, wq_acc, wk_acc, wv_acc, dyscr):
        i = pl.program_id(0)
        ct, s1, s2, rm = c_ref[...], s1_ref[...], s2_ref[...], rm_ref[...]

        @pl.when(i == 0)
        def _():
            for ref in [carry, wq_acc, wk_acc, wv_acc, o_gql, o_gkvl, o_gq, o_gk, o_gkr, o_gfq, o_gfk, o_bf]:
                ref[...] = jnp.zeros_like(ref)

        def group_rms_bwd(dy, x, r, gain, g, g_t, inv):
            xn = x * r
            dyg = dy * gain
            mu = _split_dot(_split_dot(dyg * xn, g) * inv, g_t)
            return r * (dyg - xn * mu), dy * xn

        gf, gft, invf = gf_ref[...], gft_ref[...], if_ref[...]
        xq = fox_ref[:, 0:512]
        dxq, rows = group_rms_bwd(dfq_ref[...] * FOX_SCALE, xq, _group_rms(xq, gf, gft, invf), gfq_ref[...], gf, gft, invf)
        dfox_ref[:, 0:512] = dxq.astype(BF16)
        o_gfq[...] += _colsum(rows)
        xk = fox_ref[:, 512:1024]
        dxk, rows = group_rms_bwd(dfk_ref[...].T, xk, _group_rms(xk, gf, gft, invf), gfk_ref[...], gf, gft, invf)
        dfox_ref[:, 512:1024] = dxk.astype(BF16)
        o_gfk[...] += _colsum(rows)
        dfox_ref[:, 1024:1536] = dfv_ref[...].T.astype(BF16)

        lscr[...] = jnp.zeros_like(lscr)

        xql = lat_ref[:, LAT_Q:LAT_Q + Q_LORA]
        qlat_n, r_ql = _rms_fwd(xql, gql_ref[...])
        qlat_b = qlat_n.astype(BF16)
        gm, gmt, invm = gm_ref[...], gmt_ref[...], im_ref[...]
        p = _dot(qlat_b, wq_ref[...], NT)
        for h in range(N_HEADS):
            w = slice(h * LANES, (h + 1) * LANES)
            dyscr[:, w] = _rope_bwd(dmq_ref[:, w], ct, s1, s2) * MLA_SCALE
        dp, rows = group_rms_bwd(dyscr[...], p, _group_rms(p, gm, gmt, invm), gq_ref[...], gm, gmt, invm)
        o_gq[...] += _colsum(rows)
        dp_b = dp.astype(BF16)
        wq_acc[...] += _dot(dp_b, qlat_b, TN)
        dxql, rows = _rms_bwd(_dot(dp_b, wq_ref[...]), xql, gql_ref[...], r_ql)
        lscr[:, LAT_Q:LAT_Q + Q_LORA] = dxql
        o_gql[...] += _colsum(rows)

        xkv = lat_ref[:, LAT_KV:LAT_KV + KV_LORA]
        kv_n, r_kv = _rms_fwd(xkv, gkvl_ref[...])
        kv_b = kv_n.astype(BF16)
        pk = _dot(kv_b, wk_ref[...], NT)
        dmk = dmk_ref[...].T
        dpk, rows = group_rms_bwd(dmk, pk, _group_rms(pk, gm, gmt, invm), gk_ref[...], gm, gmt, invm)
        o_gk[...] += _colsum(rows)
        dpk_b = dpk.astype(BF16)
        dv_b = dmv_ref[...].T.astype(BF16)
        wk_acc[...] += _dot(dpk_b, kv_b, TN)
        wv_acc[...] += _dot(dv_b, kv_b, TN)
        dxkv, rows = _rms_bwd(_dot(dpk_b, wk_ref[...]) + _dot(dv_b, wv_ref[...]), xkv, gkvl_ref[...], r_kv)
        lscr[:, LAT_KV:LAT_KV + KV_LORA] = dxkv
        o_gkvl[...] += _colsum(rows)

        dokr = dmk[:, 0:LANES]
        for h in range(1, N_HEADS):
            dokr = dokr + dmk[:, h * LANES:(h + 1) * LANES]
        dykr = _rope_bwd(dokr * rm, ct * rm, s1, s2)
        kr = pltpu.roll(lat_ref[:, LAT_KR:LAT_KR + LANES], HEAD_DIM, 1) * rm
        rr = lax.rsqrt(jnp.sum(kr * kr, axis=1, keepdims=True) * (1.0 / ROPE_DIM) + RMS_EPS)
        krn = kr * rr
        dyg = dykr * gkr_ref[...]
        dkr = rr * (dyg - krn * (jnp.sum(dyg * krn, axis=1, keepdims=True) * (1.0 / ROPE_DIM)))
        o_gkr[...] += _colsum(dykr * krn)
        lscr[:, LAT_KR:LAT_KR + LANES] = pltpu.roll(dkr, LANES - HEAD_DIM, 1)

        dcv = dc_ref[...]
        dlogf = jnp.dot(_tri(tm, False), dcv, precision=lax.Precision.HIGHEST, preferred_element_type=F32) + carry[...]
        carry[...] += _colsum(dcv)
        dz = dlogf * jax.nn.sigmoid(-(lat_ref[:, LAT_F:LAT_F + N_HEADS] + bf_ref[...]))
        lscr[:, LAT_F:LAT_F + N_HEADS] = dz
        o_bf[...] += _colsum(dz)

        dlat_ref[...] = lscr[...].astype(BF16)

        @pl.when(i == n_t - 1)
        def _():
            dwq_ref[...] = wq_acc[...].astype(BF16)
            dwk_ref[...] = wk_acc[...].astype(BF16)
            dwv_ref[...] = wv_acc[...].astype(BF16)

    consts = [rope_mask, *mats, wq, wk, wv, g_qlat, g_kvlat, *gains, b_f]

    def rtok(n):
        return pl.BlockSpec((tm, n), lambda i: (n_t - 1 - i, 0))

    def rtok_t(n):
        return pl.BlockSpec((n, tm), lambda i: (0, n_t - 1 - i))

    sums =[(1, Q_LORA), (1, KV_LORA), (1, N_HEADS * LANES), (1, N_HEADS * LANES), (1, LANES), (1, 512), (1, 512), (1, N_HEADS)]
    return pl.pallas_call(
        body, name="prep_bwd", grid=(n_t,),
        in_specs=[rtok(1536), rtok(LAT_W), rtok(LANES), rtok(LANES), rtok(LANES)] + [_full(a.shape) for a in consts]
        + [rtok(512), rtok_t(512), rtok_t(512), rtok(N_HEADS), rtok(N_HEADS * LANES), rtok_t(N_HEADS * LANES), rtok_t(512)],
        out_specs=[rtok(1536), rtok(LAT_W), _full(wq.shape), _full(wk.shape), _full(wv.shape)] + [_full(s) for s in sums],
        out_shape=[_sds((t, 1536), BF16), _sds((t, LAT_W), BF16), _sds(wq.shape, BF16), _sds(wk.shape, BF16), _sds(wv.shape, BF16)]
        + [_sds(s, F32) for s in sums],
        scratch_shapes=[pltpu.VMEM((1, N_HEADS), F32), pltpu.VMEM((tm, LAT_W), F32), pltpu.VMEM(wq.shape, F32),
                        pltpu.VMEM(wk.shape, F32), pltpu.VMEM(wv.shape, F32), pltpu.VMEM((tm, N_HEADS * LANES), F32)],
        compiler_params=pltpu.CompilerParams(dimension_semantics=("arbitrary",)),
    )(fox, lat, c_tab, s1_tab, s2_tab, *consts, dfq, dfk, dfv, dc, dmq, dmk, dmv)


def _causal_pairs(n, query_major):
    pairs = [(i, j) for i in range(n) for j in range(i + 1)] if query_major else [(i, j) for j in range(n) for i in range(j, n)]
    return jnp.asarray([p[0] for p in pairs], jnp.int32), jnp.asarray([p[1] for p in pairs], jnp.int32)


def _logit_bound(q_sq, k_sq_max):
    qn = q_sq[:, :N_HEADS] + q_sq[:, N_HEADS:]
    kmax = k_sq_max[:, :N_HEADS] + k_sq_max[:, N_HEADS:]
    bound = jnp.sqrt(qn * kmax) * (1.0 + 2.0 ** -10) + 2.0 ** -10
    flag = (jnp.max(bound) <= FIXED_SHIFT_MAX_BOUND).astype(F32).reshape(1, 1)
    return bound, flag


def _attn_fwd(q, k, v, bound, fixed_ok, c, c_t, *, lanes, name, exchange=None):
    t = q.shape[0]
    tq = min(ATT_TILE, t)
    n_q = t // tq
    hd = HEAD_DIM
    ch = min(ATT_COL_CHUNK, tq)
    decay = c is not None

    def body(qi_ref, kj_ref, *refs):
        if decay:
            q_ref, k_ref, v_ref, b_ref, ok_ref, c_ref, ct_ref, o_ref, o32_ref, lse_ref, m_scr, acc = refs
        else:
            q_ref, k_ref, v_ref, b_ref, ok_ref, o_ref, o32_ref, lse_ref, m_scr, acc = refs
        i, j = qi_ref[pl.program_id(0)], kj_ref[pl.program_id(0)]
        fixed = ok_ref[0, 0] > 0.5

        @pl.when(j == 0)
        def _():
            m_scr[...] = jnp.full_like(m_scr, MASK_VALUE)
            acc[...] = jnp.zeros_like(acc)

        def fixed_step(diagonal):
            for h in range(N_HEADS):
                wl = slice(h * lanes, (h + 1) * lanes)
                wv = slice(h * LANES, (h + 1) * LANES)
                qh = q_ref[:, wl]
                row = (c_ref[:, h:h + 1] - b_ref[:, h:h + 1]) if decay else -b_ref[:, h:h + 1]
                o_hi = jnp.zeros((tq, LANES), F32)
                o_lo = jnp.zeros((tq, LANES), F32)
                for cc in range(tq // ch):
                    cols = slice(cc * ch, (cc + 1) * ch)
                    s = _dot(qh, k_ref[cols, wl], NT)
                    s = s + ((row - ct_ref[h:h + 1, cols]) if decay else row)
                    if diagonal:
                        keep = (lax.broadcasted_iota(jnp.int32, (tq, ch), 0)
                                >= lax.broadcasted_iota(jnp.int32, (tq, ch), 1) + cc * ch)
                        s = jnp.where(keep, s, MASK_VALUE)
                    p = jnp.exp(s)
                    p_b = p.astype(BF16)
                    o_hi = o_hi + _dot(p_b, v_ref[cols, wv])
                    if decay:
                        o_lo = o_lo + _dot((p - p_b.astype(F32)).astype(BF16), v_ref[cols, wv])
                acc[0, :, wv] += o_hi
                if decay:
                    acc[1, :, wv] += o_lo

        def step(diagonal):
            if diagonal:
                keep = lax.broadcasted_iota(jnp.int32, (tq, tq), 0) >= lax.broadcasted_iota(jnp.int32, (tq, tq), 1)
            for h in range(N_HEADS):
                s = _dot(q_ref[:, h * lanes:(h + 1) * lanes], k_ref[:, h * lanes:(h + 1) * lanes], NT)
                if decay:
                    s = s + (c_ref[:, h:h + 1] - ct_ref[h:h + 1, :])
                if diagonal:
                    s = jnp.where(keep, s, MASK_VALUE)
                m_prev = m_scr[h]
                m_new = jnp.maximum(m_prev, jnp.max(s, axis=1, keepdims=True))
                alpha = jnp.exp(m_prev - m_new)
                p = jnp.exp(s - m_new)
                wv = slice(h * LANES, (h + 1) * LANES)
                p_b = p.astype(BF16)
                acc[0, :, wv] = alpha * acc[0, :, wv] + _dot(p_b, v_ref[:, wv])
                if decay:
                    acc[1, :, wv] = alpha * acc[1, :, wv] + _dot((p - p_b.astype(F32)).astype(BF16), v_ref[:, wv])
                m_scr[h] = m_new

        for diagonal, here in ((False, j < i), (True, j == i)):
            @pl.when(here & fixed)
            def _():
                fixed_step(diagonal)

            @pl.when(here & jnp.logical_not(fixed))
            def _():
                step(diagonal)

        @pl.when(j == i)
        def _():
            for h in range(N_HEADS):
                w = slice(h * hd, (h + 1) * hd)
                val, one = slice(h * LANES, h * LANES + hd), slice(h * LANES + hd, h * LANES + hd + 1)
                l = (acc[0, :, one] + acc[1, :, one]) if decay else acc[0, :, one]
                o_ref[:, w] = (acc[0, :, val] / l).astype(BF16)
                o32_ref[:, w] = ((acc[0, :, val] + acc[1, :, val]) if decay else acc[0, :, val]) / l
                lse_ref[:, h:h + 1] = jnp.where(fixed, b_ref[:, h:h + 1], m_scr[h]) + jnp.log(l)

    qspec = lambda n: pl.BlockSpec((tq, n), lambda s, qi, kj: (qi[s], 0))
    kspec = lambda n: pl.BlockSpec((tq, n), lambda s, qi, kj: (kj[s], 0))
    in_specs = [qspec(N_HEADS * lanes), kspec(N_HEADS * lanes), kspec(N_HEADS * LANES), qspec(N_HEADS),
                pl.BlockSpec(memory_space=pltpu.SMEM)]
    args = [q, k, v, bound, fixed_ok]
    if decay:
        in_specs += [qspec(N_HEADS), pl.BlockSpec((N_HEADS, tq), lambda s, qi, kj: (0, kj[s]))]
        args += [c, c_t]
    pairs = _causal_pairs(n_q, query_major=True)
    return _gridded_call(
        body, name=name, grid=(pairs[0].shape[0],), in_specs=in_specs, out_specs=[qspec(512), qspec(512), qspec(N_HEADS)],
        out_shape=[_sds((t, 512), BF16), _sds((t, 512), F32), _sds((t, N_HEADS), F32)],
        scratch_shapes=[pltpu.VMEM((N_HEADS, tq, 1), F32), pltpu.VMEM((2, tq, N_HEADS * LANES), F32)],
        args=args, exchange=exchange, prefetch=pairs)


def _attn_bwd(q, k, v, do, delta, lse, c, c_t, *, lanes, name, exchange=None):
    t = q.shape[0]
    tq = min(ATT_TILE, t)
    n_q = t // tq
    hd = HEAD_DIM
    decay = c is not None

    pairs = _causal_pairs(n_q, query_major=False)
    n_steps = pairs[0].shape[0]

    def body(qi_ref, kj_ref, *refs):
        if decay:
            q_ref, k_ref, v_ref, do_ref, delta_ref, lse_ref, c_ref, ct_ref, dq_hbm, dk_ref, dv_ref, dct_ref, dq_ref = refs
        else:
            q_ref, k_ref, v_ref, do_ref, delta_ref, lse_ref, dq_hbm, dk_ref, dv_ref, dq_ref = refs
        step_id = pl.program_id(0)
        i, j = qi_ref[step_id], kj_ref[step_id]

        @pl.when(step_id == 0)
        def _():
            dq_ref[...] = jnp.zeros_like(dq_ref)

        @pl.when(i == j)
        def _():
            dk_ref[...] = jnp.zeros_like(dk_ref)
            dv_ref[...] = jnp.zeros_like(dv_ref)
            if decay:
                dct_ref[...] = jnp.zeros_like(dct_ref)

        def step(diagonal):
            if diagonal:
                keep = lax.broadcasted_iota(jnp.int32, (tq, tq), 0) >= lax.broadcasted_iota(jnp.int32, (tq, tq), 1)
            rows = pl.ds(pl.multiple_of(i * tq, tq), tq)
            for h in range(N_HEADS):
                wl = slice(h * lanes, (h + 1) * lanes)
                w = slice(h * hd, (h + 1) * hd)
                qh, kh = q_ref[:, wl], k_ref[:, wl]
                s = _dot(qh, kh, NT)
                if decay:
                    s = s + (c_ref[:, h:h + 1] - ct_ref[h:h + 1, :])
                if diagonal:
                    s = jnp.where(keep, s, MASK_VALUE)
                p = jnp.exp(s - lse_ref[:, h:h + 1])
                doh = do_ref[:, w]
                dv_ref[w, :] += _dot(doh, p.astype(BF16), TN)
                dp = _dot(doh, v_ref[:, w], NT)
                ds = p * (dp - delta_ref[:, h:h + 1])
                if decay:
                    dct_ref[h:h + 1, :] -= _colsum(ds)
                ds_b = ds.astype(BF16)
                dk_ref[wl, :] += _dot(qh, ds_b, TN)
                dq_ref[rows, wl] += _dot(ds_b, kh)

        @pl.when(i > j)
        def _():
            step(False)

        @pl.when(i == j)
        def _():
            step(True)

        @pl.when(step_id == n_steps - 1)
        def _():
            pltpu.sync_copy(dq_ref, dq_hbm)

    qspec = lambda n: pl.BlockSpec((tq, n), lambda s, qi, kj: (qi[s], 0))
    kspec = lambda n: pl.BlockSpec((tq, n), lambda s, qi, kj: (kj[s], 0))
    in_specs = [qspec(N_HEADS * lanes), kspec(N_HEADS * lanes), kspec(512), qspec(512), qspec(N_HEADS), qspec(N_HEADS)]
    kspec_t = lambda n: pl.BlockSpec((n, tq), lambda s, qi, kj: (0, kj[s]))
    out_specs = [pl.BlockSpec(memory_space=pl.ANY), kspec_t(N_HEADS * lanes), kspec_t(512)]
    out_shape = [_sds((t, N_HEADS * lanes), F32), _sds((N_HEADS * lanes, t), F32), _sds((512, t), F32)]
    args = [q, k, v, do, delta, lse]
    if decay:
        ctspec = pl.BlockSpec((N_HEADS, tq), lambda s, qi, kj: (0, kj[s]))
        in_specs += [qspec(N_HEADS), ctspec]
        out_specs.append(ctspec)
        out_shape.append(_sds((N_HEADS, t), F32))
        args += [c, c_t]
    return _gridded_call(body, name=name, grid=(n_steps,), in_specs=in_specs, out_specs=out_specs, out_shape=out_shape,
                         scratch_shapes=[pltpu.VMEM((t, N_HEADS * lanes), F32)], args=args, exchange=exchange, prefetch=pairs)


def _mix_fwd(x, y_mla, y_fox, gates, b_gate, wbm_t, wbf_t, wo):
    t = x.shape[0]
    tm = min(PROJ_TILE, t)

    def body(x_ref, ym_ref, yf_ref, gt_ref, bg_ref, wbm_ref, wbf_ref, wo_ref, out_ref):
        um = _dot(ym_ref[...], wbm_ref[...], NT)
        uf = _dot(yf_ref[...], wbf_ref[...], NT)
        sm = jax.nn.sigmoid(gt_ref[:, 0:D_MODEL] + bg_ref[0:1, :])
        sf = jax.nn.sigmoid(gt_ref[:, D_MODEL:2 * D_MODEL] + bg_ref[1:2, :])
        mixed = sm * um + sf * uf
        out_ref[...] = x_ref[...] + _dot(mixed.astype(BF16), wo_ref[...])

    tok = _tok(tm, D_MODEL)
    return pl.pallas_call(
        body, name="mix_fwd", grid=(t // tm,),
        in_specs=[tok, _tok(tm, 512), _tok(tm, 512), _tok(tm, 2 * D_MODEL), _full((2, D_MODEL)), _full(wbm_t.shape),
                  _full(wbf_t.shape), _full(wo.shape)],
        out_specs=tok, out_shape=_sds((t, D_MODEL), F32),
        compiler_params=pltpu.CompilerParams(dimension_semantics=("arbitrary",)),
    )(x, y_mla, y_fox, gates, b_gate, wbm_t, wbf_t, wo)


def _mix_bwd(dx, y_mla, y_fox, y_mla32, y_fox32, gates, b_gate, wbm_t, wbf_t, wo, head_groups, exchange=None):
    t = dx.shape[0]
    tm = min(PROJ_TILE, t)

    def body(dx_ref, ym_ref, yf_ref, ym32_ref, yf32_ref, gt_ref, bg_ref, wbm_ref, wbf_ref, wo_ref, gf_ref,
             dym_ref, dyf_ref, dlm_ref, dlf_ref, dgt_ref, mixed_ref, dum_ref, duf_ref, dxb_ref, dbg_ref):
        i = pl.program_id(0)
        dxb = dx_ref[...].astype(BF16)
        dxb_ref[...] = dxb
        dmixed = _dot(dxb, wo_ref[...], NT)
        um = _dot(ym_ref[...], wbm_ref[...], NT)
        uf = _dot(yf_ref[...], wbf_ref[...], NT)
        sm = jax.nn.sigmoid(gt_ref[:, 0:D_MODEL] + bg_ref[0:1, :])
        sf = jax.nn.sigmoid(gt_ref[:, D_MODEL:2 * D_MODEL] + bg_ref[1:2, :])
        mixed_ref[...] = (sm * um + sf * uf).astype(BF16)
        dum = (dmixed * sm).astype(BF16)
        duf = (dmixed * sf).astype(BF16)
        dum_ref[...] = dum
        duf_ref[...] = duf
        dgm = dmixed * um * (sm * (1.0 - sm))
        dgf = dmixed * uf * (sf * (1.0 - sf))
        dgt_ref[:, 0:D_MODEL] = dgm.astype(BF16)
        dgt_ref[:, D_MODEL:2 * D_MODEL] = dgf.astype(BF16)
        for du, wb_ref, y32_ref, dy_ref, dl_ref in ((dum, wbm_ref, ym32_ref, dym_ref, dlm_ref),
                                                    (duf, wbf_ref, yf32_ref, dyf_ref, dlf_ref)):
            dy = _dot(du, wb_ref[...])
            dy_ref[...] = dy.astype(BF16)
            prod = dy.astype(BF16).astype(F32) * y32_ref[...]
            hi = prod.astype(BF16)
            rest = prod - hi.astype(F32)
            mid = rest.astype(BF16)
            dl_ref[...] = _dot(hi, gf_ref[...]) + _dot(mid, gf_ref[...]) + _dot((rest - mid.astype(F32)).astype(BF16), gf_ref[...])

        @pl.when(i == 0)
        def _():
            dbg_ref[...] = jnp.zeros_like(dbg_ref)

        dbg_ref[0:1, :] += _colsum(dgm)
        dbg_ref[1:2, :] += _colsum(dgf)

    tok = _tok(tm, D_MODEL)
    tokb = _sds((t, D_MODEL), BF16)
    t512, tg = _tok(tm, 512), _tok(tm, N_GROUPS)
    return _gridded_call(
        body, name="mix_bwd", grid=(t // tm,),
        in_specs=[tok, t512, t512, t512, t512, _tok(tm, 2 * D_MODEL), _full((2, D_MODEL)), _full(wbm_t.shape),
                  _full(wbf_t.shape), _full(wo.shape), _full(head_groups.shape)],
        out_specs=[t512, t512, tg, tg, _tok(tm, 2 * D_MODEL), tok, tok, tok, tok, _full((2, D_MODEL))],
        out_shape=[_sds((t, 512), BF16), _sds((t, 512), BF16), _sds((t, N_GROUPS), F32), _sds((t, N_GROUPS), F32),
                   _sds((t, 2 * D_MODEL), BF16), tokb, tokb, tokb, tokb, _sds((2, D_MODEL), F32)],
        scratch_shapes=[], args=[dx, y_mla, y_fox, y_mla32, y_fox32, gates, b_gate, wbm_t, wbf_t, wo, head_groups],
        exchange=exchange)


def _my_position():
    x, y, c = lax.axis_index("x"), lax.axis_index("y"), lax.axis_index("c")
    return x, y, c, 4 * x + 2 * y + c


def _peer(x, y, c, mask):
    px = 1 - x if mask & 4 else x
    py = 1 - y if mask & 2 else y
    pc = 1 - c if mask & 1 else c
    return (px, py, pc), 4 * px + 2 * py + pc


def _chip_peer(x, y, km):
    px = 1 - x if km & 2 else x
    py = 1 - y if km & 1 else y
    return px, py, 2 * px + py


_HBM = pl.BlockSpec(memory_space=pl.ANY)


def _wait_all(copies):
    for cp in copies:
        cp.wait()


class _ChipExchange:
    def __init__(self, gather, arrays):
        self.gather, self.arrays, self.aliased = gather, list(arrays), False
        n = len(self.arrays)
        self.out_shape = [_sds((N_DEV * a.shape[0],) + a.shape[1:], a.dtype) if gather else _sds(a.shape, a.dtype)
                          for a in self.arrays]
        self.scratch_shapes = [pltpu.SemaphoreType.DMA((n, N_CHIP)), pltpu.SemaphoreType.DMA((n, N_CHIP)),
                               pltpu.SemaphoreType.DMA((n,))]

    def copies(self, srcs, dsts, send_sems, recv_sems, local_sems):
        x, y, c, me = _my_position()
        q_me = 2 * x + y
        out = []
        for a in range(len(self.arrays)):
            if self.gather:
                r = srcs[a].shape[0]
                local_src, dst = srcs[a], dsts[a].at[pl.ds(me * r, r)]
            else:
                local_src, dst = srcs[a].at[q_me], dsts[a].at[q_me]
            out.append(pltpu.make_async_copy(local_src, dst, local_sems.at[a]))
            for km in range(1, N_CHIP):
                px, py, q_peer = _chip_peer(x, y, km)
                out.append(pltpu.make_async_remote_copy(
                    src_ref=srcs[a] if self.gather else srcs[a].at[q_peer], dst_ref=dst, send_sem=send_sems.at[a, km],
                    recv_sem=recv_sems.at[a, km], device_id=(px, py, c), device_id_type=MESH))
        return out

    def standalone(self, name):
        n = len(self.arrays)

        def body(*refs):
            copies = self.copies(refs[:n], refs[n:2 * n], *refs[2 * n:])
            for cp in copies:
                cp.start()
            _wait_all(copies)

        return pl.pallas_call(body, name=name, in_specs=[_HBM] * n, out_specs=[_HBM] * n, out_shape=self.out_shape,
                              scratch_shapes=self.scratch_shapes)(*self.arrays)


def _gridded_call(body, *, name, grid, in_specs, out_specs, out_shape, scratch_shapes, args, exchange=None, prefetch=()):
    params = pltpu.CompilerParams(dimension_semantics=("arbitrary",) * len(grid))
    n_pre, n_in, n_out, n_scr = len(prefetch), len(in_specs), len(out_specs), len(scratch_shapes)
    n_x = 0 if exchange is None else len(exchange.arrays)

    def carrier(*refs):
        pre, refs = refs[:n_pre], refs[n_pre:]
        ins, x_src, refs = refs[:n_in], refs[n_in:n_in + n_x], refs[n_in + n_x:]
        outs, x_dst, refs = refs[:n_out], refs[n_out:n_out + n_x], refs[n_out + n_x:]
        copies = exchange.copies(x_src, x_dst, *refs[n_scr:])
        pids = [pl.program_id(d) for d in range(len(grid))]
        first = functools.reduce(jnp.logical_and, [p == 0 for p in pids])
        last = functools.reduce(jnp.logical_and, [p == g - 1 for p, g in zip(pids, grid)])

        @pl.when(first)
        def _():
            for cp in copies:
                cp.start()

        body(*pre, *ins, *outs, *refs[:n_scr])

        @pl.when(last)
        def _():
            _wait_all(copies)

    x_shapes, x_scratch, x_args = ([], [], []) if exchange is None else (exchange.out_shape, exchange.scratch_shapes, exchange.arrays)
    spec = pltpu.PrefetchScalarGridSpec(
        num_scalar_prefetch=n_pre, grid=grid, in_specs=list(in_specs) + [_HBM] * n_x, out_specs=list(out_specs) + [_HBM] * n_x,
        scratch_shapes=list(scratch_shapes) + x_scratch)
    in_place = {n_pre + n_in + k: n_out + k for k in range(n_x)} if n_x and exchange.aliased else {}
    res = pl.pallas_call(body if exchange is None else carrier, name=name, grid_spec=spec, out_shape=list(out_shape) + x_shapes,
                         input_output_aliases=in_place, compiler_params=params)(*prefetch, *args, *x_args)
    return res[:n_out], (None if exchange is None else res[n_out:])


class _CoreExchange:
    def __init__(self, gather, arrays):
        self.gather, self.arrays, self.aliased = gather, list(arrays), gather
        n = len(self.arrays)
        self.out_shape = [_sds(a.shape, a.dtype) if gather else _sds((N_CHIP, a.shape[0] // N_DEV) + a.shape[1:], a.dtype)
                          for a in self.arrays]
        self.scratch_shapes = [pltpu.SemaphoreType.DMA((n, N_CHIP)), pltpu.SemaphoreType.DMA((n, N_CHIP))]

    def copies(self, srcs, dsts, send_sems, recv_sems):
        x, y, c, _ = _my_position()
        out = []
        for a in range(len(self.arrays)):
            r = srcs[a].shape[0] // N_DEV
            for q in range(N_CHIP):
                if self.gather:
                    rows = pl.ds((2 * q + c) * r, r)
                    src, dst = srcs[a].at[rows], dsts[a].at[rows]
                else:
                    src, dst = srcs[a].at[pl.ds((2 * q + 1 - c) * r, r)], dsts[a].at[q]
                out.append(pltpu.make_async_remote_copy(src_ref=src, dst_ref=dst, send_sem=send_sems.at[a, q],
                                                        recv_sem=recv_sems.at[a, q], device_id=(x, y, 1 - c), device_id_type=MESH))
        return out

    def standalone(self, name):
        n = len(self.arrays)

        def body(*refs):
            copies = self.copies(refs[:n], refs[n:2 * n], *refs[2 * n:])
            for cp in copies:
                cp.start()
            _wait_all(copies)

        return pl.pallas_call(body, name=name, in_specs=[_HBM] * n, out_specs=[_HBM] * n, out_shape=self.out_shape,
                              input_output_aliases={a: a for a in range(n)} if self.aliased else {},
                              scratch_shapes=self.scratch_shapes)(*self.arrays)


def _gather_over_cores(arrays, name):
    return _CoreExchange(True, arrays).standalone(name)


def _grads_to_sibling(grads, name):
    return _CoreExchange(False, grads).standalone(name)


def _pair_sum(grad, from_sibling, name):
    r, n = from_sibling.shape[1:]

    def body(g_ref, s_ref, o_ref):
        c = lax.axis_index("c")
        o_ref[...] = (g_ref[c].astype(F32) + s_ref[...].astype(F32)).astype(BF16)

    return pl.pallas_call(
        body, name=name, grid=(N_CHIP,),
        in_specs=[pl.BlockSpec((None, 2, r, n), lambda q: (q, 0, 0, 0)), pl.BlockSpec((None, r, n), lambda q: (q, 0, 0))],
        out_specs=pl.BlockSpec((None, r, n), lambda q: (q, 0, 0)), out_shape=_sds((N_CHIP, r, n), BF16),
    )(grad.reshape(N_CHIP, 2, r, n), from_sibling)


def _gather_by_axes(pieces, name):
    n_arr = len(pieces)

    def body(*refs):
        srcs, dsts = refs[:n_arr], refs[n_arr:2 * n_arr]
        send_sems, recv_sems, local_sems = refs[2 * n_arr:]
        x, y, c, me = _my_position()
        sibling, x_nbr, y_nbr = (x, y, 1 - c), (1 - x, y, c), (x, 1 - y, c)

        def ident(px, py, pc):
            return 4 * px + 2 * py + pc

        def half(a, piece, hf):
            h = srcs[a].shape[0] // 2
            return dsts[a].at[pl.ds(piece * 2 * h + hf * h, h)]

        def copy(a, k, src, dst, to):
            return pltpu.make_async_remote_copy(src_ref=src, dst_ref=dst, send_sem=send_sems.at[a, k], recv_sem=recv_sems.at[a, k],
                                                device_id=to, device_id_type=MESH)

        def arrival(a, k, piece, hf, frm):
            return copy(a, k, half(a, piece, hf), half(a, piece, hf), frm)

        def pass_on(a, k, piece, hf, to):
            return copy(a, k, half(a, piece, hf), half(a, piece, hf), to)

        local, sent = [], []
        for a in range(n_arr):
            r = srcs[a].shape[0]
            h = r // 2
            whole = dsts[a].at[pl.ds(me * r, r)]
            own = [srcs[a].at[pl.ds(0, h)], srcs[a].at[pl.ds(h, h)]]
            local.append(pltpu.make_async_copy(srcs[a], whole, local_sems.at[a]))
            sent += [copy(a, 0, srcs[a], whole, sibling),
                     copy(a, 1, own[0], half(a, me, 0), x_nbr), copy(a, 2, own[1], half(a, me, 1), x_nbr),
                     copy(a, 4, own[1], half(a, me, 1), y_nbr), copy(a, 5, own[0], half(a, me, 0), y_nbr)]
        for cp in local + sent:
            cp.start()

        def then(cps):
            for cp in cps:
                cp.start()
            sent.extend(cps)

        id_x, id_y, id_d = ident(1 - x, y, c), ident(x, 1 - y, c), ident(1 - x, 1 - y, c)
        for a in range(n_arr):
            arrival(a, 4, id_y, 1, y_nbr).wait_recv()
            then([pass_on(a, 3, id_y, 1, x_nbr), pass_on(a, 7, id_y, 1, sibling)])
            arrival(a, 1, id_x, 0, x_nbr).wait_recv()
            then([pass_on(a, 6, id_x, 0, y_nbr), pass_on(a, 8, id_x, 0, sibling)])
            arrival(a, 2, id_x, 1, x_nbr).wait_recv()
            then([pass_on(a, 9, id_x, 1, sibling)])
            arrival(a, 5, id_y, 0, y_nbr).wait_recv()
            then([pass_on(a, 10, id_y, 0, sibling)])
            arrival(a, 3, id_d, 1, x_nbr).wait_recv()
            then([pass_on(a, 11, id_d, 1, sibling)])
            arrival(a, 6, id_d, 0, y_nbr).wait_recv()
            then([pass_on(a, 12, id_d, 0, sibling)])
        sx, sy, sd = ident(1 - x, y, 1 - c), ident(x, 1 - y, 1 - c), ident(1 - x, 1 - y, 1 - c)
        for a in range(n_arr):
            r = srcs[a].shape[0]
            rows = dsts[a].at[pl.ds(ident(x, y, 1 - c) * r, r)]
            copy(a, 0, rows, rows, sibling).wait_recv()
            for k, piece, hf in ((7, sy, 1), (8, sx, 0), (9, sx, 1), (10, sy, 0), (11, sd, 1), (12, sd, 0)):
                arrival(a, k, piece, hf, sibling).wait_recv()
        for cp in sent:
            cp.wait_send()
        for cp in local:
            cp.wait()

    n_sem = 13
    return pl.pallas_call(
        body, name=name, in_specs=[_HBM] * n_arr, out_specs=[_HBM] * n_arr,
        out_shape=[_sds((N_DEV * p.shape[0],) + p.shape[1:], p.dtype) for p in pieces],
        scratch_shapes=[pltpu.SemaphoreType.DMA((n_arr, n_sem)), pltpu.SemaphoreType.DMA((n_arr, n_sem)),
                        pltpu.SemaphoreType.DMA((n_arr,))],
    )(*pieces)


def _all_reduce_small(vec):
    r = vec.shape[0]

    def body(v_ref, o_ref, buf, send_sems, recv_sems):
        x, y, c, me = _my_position()
        buf[me] = v_ref[...]
        copies = []
        for mask in range(1, N_DEV):
            peer, _ = _peer(x, y, c, mask)
            cp = pltpu.make_async_remote_copy(src_ref=v_ref, dst_ref=buf.at[me], send_sem=send_sems.at[mask],
                                              recv_sem=recv_sems.at[mask], device_id=peer, device_id_type=MESH)
            cp.start()
            copies.append(cp)
        for cp in copies:
            cp.wait()
        total = buf[0]
        for s in range(1, N_DEV):
            total = total + buf[s]
        o_ref[...] = total

    vm = pl.BlockSpec(memory_space=pltpu.VMEM)
    return pl.pallas_call(
        body, name="all_reduce_small", in_specs=[vm], out_specs=vm, out_shape=_sds(vec.shape, F32),
        scratch_shapes=[pltpu.VMEM((N_DEV, r, 128), F32), pltpu.SemaphoreType.DMA((N_DEV,)), pltpu.SemaphoreType.DMA((N_DEV,))],
    )(vec)


def _adamw_math(w, g, m, v):
    m = ADAM_B1 * m + (1.0 - ADAM_B1) * g
    v = ADAM_B2 * v + (1.0 - ADAM_B2) * (g * g)
    m_hat = m / (1.0 - ADAM_B1 ** ADAM_STEP)
    v_hat = v / (1.0 - ADAM_B2 ** ADAM_STEP)
    delta = -ADAM_LR * (m_hat / (jnp.sqrt(v_hat) + ADAM_EPS) + ADAM_WD * w)
    return delta, m, v


def _reduce_adamw(slots, w, m, v, *, transpose, name):
    r, n = slots.shape[1:]
    padded = not transpose and w.shape[0] != r

    def body(s_ref, w_ref, m_ref, v_ref, g_ref, d_ref, nm_ref, nv_ref, *scr):
        g = s_ref[0].astype(F32)
        for s in range(1, slots.shape[0]):
            g = g + s_ref[s].astype(F32)
        if transpose:
            scr[0][...] = g.T
            g = scr[0][:, 0:w_ref.shape[1]]
        elif padded:
            scr[0][...] = g
            g = scr[0][0:w_ref.shape[0], :]
        g_ref[...] = g
        d_ref[...], nm_ref[...], nv_ref[...] = _adamw_math(w_ref[...], g, m_ref[...], v_ref[...])

    out = _sds(w.shape, F32)
    if not transpose and not padded and r % (2 * BF16_ROW_TILE) == 0:
        half = pl.BlockSpec((r // 2, n), lambda i: (i, 0))
        return pl.pallas_call(
            body, name=name, grid=(2,), in_specs=[pl.BlockSpec((slots.shape[0], r // 2, n), lambda i: (0, i, 0)), half, half, half],
            out_specs=[half] * 4, out_shape=[out, out, out, out])(slots, w, m, v)
    return pl.pallas_call(
        body, name=name, out_shape=[out, out, out, out],
        scratch_shapes=[pltpu.VMEM((n, r) if transpose else (r, n), F32)] if transpose or padded else [],
    )(slots, w, m, v)


def _adamw(g, w, m, v, name):
    def body(g_ref, w_ref, m_ref, v_ref, d_ref, nm_ref, nv_ref):
        d_ref[...], nm_ref[...], nv_ref[...] = _adamw_math(w_ref[...], g_ref[...], m_ref[...], v_ref[...])

    out = _sds(w.shape, F32)
    return pl.pallas_call(body, name=name, out_shape=[out, out, out])(g, w, m, v)


_SMALL = ["ffn1_norm", "mix_norm", "ffn2_norm", "mla_q_lat_norm", "mla_kv_lat_norm", "mla_q_nope_gain", "mla_q_rope_gain",
          "mla_k_nope_gain", "mla_k_rope_gain", "fox_q_gain", "fox_k_gain", "fox_b_f"]
_WEIGHTS = ["ffn1_norm", "ffn1_w_gate", "ffn1_w_up", "ffn1_w_down", "mix_norm", "w_in", "mla_q_lat_norm", "mla_w_qb",
            "mla_kv_lat_norm", "mla_w_kvb", "mla_q_nope_gain", "mla_q_rope_gain", "mla_k_nope_gain", "mla_k_rope_gain",
            "fox_q_gain", "fox_k_gain", "fox_b_f", "w_branch_mla", "w_branch_fox", "b_gate", "w_o", "ffn2_norm",
            "ffn2_w_gate", "ffn2_w_up", "ffn2_w_down"]
_IN_Q, _IN_KV, _IN_KR, _IN_FOX, _IN_F, _IN_GATES = (0, 192), (192, 128), (320, 32), (352, 1536), (1888, 8), (1896, 2048)


def _rows(a, seg):
    return a[seg[0]:seg[0] + seg[1]]


def _split_w_in(win_t):
    z = lambda n: jnp.zeros((n, D_MODEL), win_t.dtype)
    lat = jnp.concatenate([_rows(win_t, _IN_Q), z(LAT_KV - Q_LORA), _rows(win_t, _IN_KV), _rows(win_t, _IN_KR),
                           _rows(win_t, _IN_F), z(LAT_W - LAT_F - N_HEADS)], axis=0)
    return _rows(win_t, _IN_GATES), _rows(win_t, _IN_FOX), lat


def _join_w_in(d_gates, d_fox, d_lat):
    return jnp.concatenate([d_lat[LAT_Q:LAT_Q + Q_LORA], d_lat[LAT_KV:LAT_KV + KV_LORA], d_lat[LAT_KR:LAT_KR + ROPE_DIM],
                            d_fox, d_lat[LAT_F:LAT_F + N_HEADS], d_gates], axis=0)


def kernel(x, positions, ffn1_norm, ffn1_w_gate, ffn1_w_up, ffn1_w_down, mix_norm, w_in, mla_q_lat_norm, mla_w_qb, mla_kv_lat_norm, mla_w_kvb, mla_q_nope_gain, mla_q_rope_gain, mla_k_nope_gain, mla_k_rope_gain, fox_q_gain, fox_k_gain, fox_b_f, w_branch_mla, w_branch_fox, b_gate, w_o, ffn2_norm, ffn2_w_gate, ffn2_w_up, ffn2_w_down, loss_target, m_ffn1_norm, m_ffn1_w_gate, m_ffn1_w_up, m_ffn1_w_down, m_mix_norm, m_w_in, m_mla_q_lat_norm, m_mla_w_qb, m_mla_kv_lat_norm, m_mla_w_kvb, m_mla_q_nope_gain, m_mla_q_rope_gain, m_mla_k_nope_gain, m_mla_k_rope_gain, m_fox_q_gain, m_fox_k_gain, m_fox_b_f, m_w_branch_mla, m_w_branch_fox, m_b_gate, m_w_o, m_ffn2_norm, m_ffn2_w_gate, m_ffn2_w_up, m_ffn2_w_down, v_ffn1_norm, v_ffn1_w_gate, v_ffn1_w_up, v_ffn1_w_down, v_mix_norm, v_w_in, v_mla_q_lat_norm, v_mla_w_qb, v_mla_kv_lat_norm, v_mla_w_kvb, v_mla_q_nope_gain, v_mla_q_rope_gain, v_mla_k_nope_gain, v_mla_k_rope_gain, v_fox_q_gain, v_fox_k_gain, v_fox_b_f, v_w_branch_mla, v_w_branch_fox, v_b_gate, v_w_o, v_ffn2_norm, v_ffn2_w_gate, v_ffn2_w_up, v_ffn2_w_down):
    env = dict(locals())
    strip = lambda n, a: a if n in _SMALL else a[0]
    W = {n: strip(n, env[n]) for n in _WEIGHTS}
    M = {n: strip(n, env["m_" + n]) for n in _WEIGHTS}
    V = {n: strip(n, env["v_" + n]) for n in _WEIGHTS}
    xs = x[0]
    t = xs.shape[0]

    col_split = ["ffn1_w_gate", "ffn1_w_up", "ffn2_w_gate", "ffn2_w_up", "mla_w_qb", "mla_w_kvb", "w_branch_mla", "w_branch_fox"]
    row_split = ["ffn1_w_down", "ffn2_w_down", "w_o"]
    pieces = {n: W[n].T.astype(BF16) for n in col_split}
    pieces.update({n: W[n].astype(BF16) for n in row_split})
    pieces["w_in"] = jnp.pad(W["w_in"].T.astype(BF16), ((0, W_IN_PIECE_PAD - W_IN_PIECE), (0, 0)))
    pieces["b_gate"] = W["b_gate"].T
    group_a = ["ffn1_w_gate", "ffn1_w_up", "ffn1_w_down"]
    group_b = ["w_in", "mla_w_qb", "mla_w_kvb", "w_branch_mla", "w_branch_fox", "w_o", "b_gate"]
    group_c = ["ffn2_w_gate", "ffn2_w_up", "ffn2_w_down"]
    gather = lambda group: _ChipExchange(True, [pieces[n] for n in group])
    G = dict(zip(group_a, _gather_by_axes([pieces[n] for n in group_a], "gather_a")))

    inv_freq = ROPE_THETA ** (-jnp.arange(ROPE_HALF, dtype=F32) / ROPE_HALF)
    ang = positions[0].astype(F32)[:, None] * inv_freq
    tables, mats = _prep_tables(jnp.cos(ang), jnp.sin(ang)), _group_matrices()
    gains = _head_gains(*[W[n] for n in ["mla_q_nope_gain", "mla_q_rope_gain", "mla_k_nope_gain", "mla_k_rope_gain",
                                         "fox_q_gain", "fox_k_gain"]])

    (x1, a1, b1), got_b = _ffn_fwd(xs, W["ffn1_norm"], G["ffn1_w_gate"], G["ffn1_w_up"], G["ffn1_w_down"],
                                   exchange=gather(group_b))
    G.update(zip(group_b, _gather_over_cores(got_b, "gather_d2d_b")))
    win_t = G["w_in"].reshape(N_DEV, W_IN_PIECE_PAD, D_MODEL)[:, :W_IN_PIECE].reshape(N_DEV * W_IN_PIECE, D_MODEL)
    wgate_t, wfox_t, wlat_t = _split_w_in(win_t)
    bg = G["b_gate"].T
    gates, fox, lat = _proj_fwd(x1, W["mix_norm"], wgate_t, wfox_t, wlat_t)
    prep_args = (fox, lat, tables, mats, *_interleave_weights(G["mla_w_qb"], G["mla_w_kvb"]), W["mla_q_lat_norm"],
                 W["mla_kv_lat_norm"], gains, W["fox_b_f"])
    fq, fk, fv, c, mq, mk, mv, fq_sq, fk_sq_max, mq_sq, mk_sq_max, fv1, mv1 = _prep_fwd(*prep_args, *_value_layout(prep_args[6]))
    c_t = c.T
    b_fox, ok_fox = _logit_bound(fq_sq, fk_sq_max)
    b_mla, ok_mla = _logit_bound(mq_sq, mk_sq_max)
    (y_fox, y_fox32, lse_fox), got_c = _attn_fwd(fq, fk, fv1, b_fox, ok_fox, c, c_t, lanes=HEAD_DIM, name="fox_fwd",
                                                 exchange=gather(group_c))
    (y_mla, y_mla32, lse_mla), got_c = _attn_fwd(mq, mk, mv1, b_mla, ok_mla, None, None, lanes=MLA_QK_LANES, name="mla_fwd",
                                                 exchange=_CoreExchange(True, got_c))
    G.update(zip(group_c, got_c))
    x2 = _mix_fwd(x1, y_mla, y_fox, gates, bg, G["w_branch_mla"], G["w_branch_fox"], G["w_o"])
    (dx3, a2, b2, loss_vec), _ = _ffn_fwd(x2, W["ffn2_norm"], G["ffn2_w_gate"], G["ffn2_w_up"], G["ffn2_w_down"],
                                          target=loss_target[0])

    def chip_sums(group, tag, from_sibling=None):
        if from_sibling is None:
            from_sibling = _grads_to_sibling([grads[n] for n in group], "grads_d2d_" + tag)
        return _ChipExchange(False, [_pair_sum(grads[n], s, "pair_sum_" + n) for n, s in zip(group, from_sibling)])

    (dx2, dg_ffn2, da2, db2, h2, n2, dyh2), _ = _ffn_bwd(dx3, x2, W["ffn2_norm"], a2, b2, G["ffn2_w_gate"], G["ffn2_w_up"],
                                                        G["ffn2_w_down"], "ffn2_bwd")
    grads = {"ffn2_w_gate": _tn_matmul(da2, n2, "ffn2_dgate"), "ffn2_w_up": _tn_matmul(db2, n2, "ffn2_dup"),
             "ffn2_w_down": _tn_matmul(h2, dyh2, "ffn2_ddown")}
    (dy_mla, dy_fox, delta_mla, delta_fox, dgates, mixed, dum, duf, dx2b, dbg), from_sibling_c = _mix_bwd(
        dx2, y_mla, y_fox, y_mla32, y_fox32, gates, bg, G["w_branch_mla"], G["w_branch_fox"], G["w_o"], mats[3],
        exchange=_CoreExchange(False, [grads[n] for n in group_c]))
    delta_mla, delta_fox = delta_mla[:, :N_HEADS], delta_fox[:, :N_HEADS]
    grads["w_o"] = _tn_matmul(mixed, dx2b, "d_w_o")
    grads["w_branch_mla"] = _tn_matmul(dum, y_mla, "d_w_branch_mla")
    grads["w_branch_fox"] = _tn_matmul(duf, y_fox, "d_w_branch_fox")
    (dfq, dfk, dfv, dc_t), slots_c = _attn_bwd(fq, fk, fv, dy_fox, delta_fox, lse_fox, c, c_t, lanes=HEAD_DIM,
                                               name="fox_bwd", exchange=chip_sums(group_c, "c", from_sibling_c))
    slots = dict(zip(group_c, slots_c))
    (dmq, dmk, dmv), _ = _attn_bwd(mq, mk, mv, dy_mla, delta_mla, lse_mla, None, None, lanes=MLA_QK_LANES, name="mla_bwd")
    dfox, dlat, dwq, dwk, dwv, d_gql, d_gkvl, d_gq, d_gk, d_gkr, d_gfq, d_gfk, d_bf = _prep_bwd(
        *prep_args, dfq, dfk, dfv, dc_t.T, dmq, dmk, dmv)
    grads["mla_w_qb"], grads["mla_w_kvb"] = _deinterleave_grads(dwq, dwk, dwv)
    fold = lambda a, width: a.reshape(N_HEADS, width).sum(axis=0)[None]
    d_gq, d_gk = fold(d_gq, LANES), fold(d_gk, LANES)
    d_prep_small = [d_gql, d_gkvl, d_gq[:, :HEAD_DIM], d_gq[:, HEAD_DIM:HEAD_DIM + ROPE_DIM], d_gk[:, :HEAD_DIM],
                    d_gkr[:, HEAD_DIM:HEAD_DIM + ROPE_DIM], fold(d_gfq, HEAD_DIM), fold(d_gfk, HEAD_DIM), d_bf]
    dx1, dg_mix, nmix = _proj_bwd(dgates, dfox, dlat, wgate_t, wfox_t, wlat_t, x1, W["mix_norm"], dx2)
    dwin_t = _join_w_in(_tn_matmul(dgates, nmix, "d_w_in_gates"), _tn_matmul(dfox, nmix, "d_w_in_fox"),
                        _tn_matmul(dlat, nmix, "d_w_in_lat"))
    grads["w_in"] = jnp.pad(dwin_t.reshape(N_DEV, W_IN_PIECE, D_MODEL), ((0, 0), (0, W_IN_PIECE_PAD - W_IN_PIECE), (0, 0))
                            ).reshape(N_DEV * W_IN_PIECE_PAD, D_MODEL)
    grad_group_b = [n for n in group_b if n != "b_gate"]
    (dx0, dg_ffn1, da1, db1, h1, n1, dyh1), slots_b = _ffn_bwd(dx1, xs, W["ffn1_norm"], a1, b1, G["ffn1_w_gate"], G["ffn1_w_up"],
                                                              G["ffn1_w_down"], "ffn1_bwd", exchange=chip_sums(grad_group_b, "b"))
    slots.update(zip(grad_group_b, slots_b))
    grads["ffn1_w_gate"] = _tn_matmul(da1, n1, "ffn1_dgate")
    grads["ffn1_w_up"], got = _tn_matmul(db1, n1, "ffn1_dup", exchange=chip_sums(["ffn1_w_gate"], "a_gate"))
    slots["ffn1_w_gate"] = got[0]
    grads["ffn1_w_down"], got = _tn_matmul(h1, dyh1, "ffn1_ddown", exchange=chip_sums(["ffn1_w_up"], "a_up"))
    slots["ffn1_w_up"] = got[0]
    slots["ffn1_w_down"] = chip_sums(["ffn1_w_down"], "a_down").standalone("grads_ici_a_down")[0]

    small_parts = [dg_ffn1, dg_mix, dg_ffn2] + list(d_prep_small) + [dbg.reshape(1, 2 * D_MODEL), loss_vec]
    flat = jnp.concatenate([p.reshape(-1) for p in small_parts])
    n_flat = flat.shape[0]
    rows = -(-n_flat // (8 * 128)) * 8
    total = _all_reduce_small(jnp.pad(flat, (0, rows * 128 - n_flat)).reshape(rows, 128)).reshape(-1)
    offs, small_g = 0, {}
    for n in _SMALL:
        small_g[n] = total[offs:offs + W[n].shape[1]].reshape(W[n].shape)
        offs += W[n].shape[1]
    bg_full = total[offs:offs + 2 * D_MODEL].reshape(2, D_MODEL)
    offs += 2 * D_MODEL
    loss = (0.5 / D_MODEL) * jnp.sum(total[offs:offs + D_MODEL])
    _, _, _, me = _my_position()
    small_g["b_gate"] = lax.dynamic_slice_in_dim(bg_full, me * (D_MODEL // N_DEV), D_MODEL // N_DEV, axis=1)

    transposed_in_memory = ["ffn1_w_gate", "ffn1_w_up", "ffn2_w_gate", "ffn2_w_up", "w_in"]
    res = {}
    for n in _WEIGHTS:
        if n in small_g:
            res[n] = (small_g[n],) + tuple(_adamw(small_g[n], W[n], M[n], V[n], "adamw_" + n))
        elif n in transposed_in_memory:
            res[n] = tuple(o.T for o in _reduce_adamw(slots[n], W[n].T, M[n].T, V[n].T, transpose=False, name="adamw_" + n))
        else:
            res[n] = tuple(_reduce_adamw(slots[n], W[n], M[n], V[n], transpose=n in col_split, name="adamw_" + n))
    outs = [loss, dx0[None]]
    for k in range(4):
        outs += [res[n][k] if n in _SMALL else res[n][k][None] for n in _WEIGHTS]
    return tuple(outs)
```

```python
import functools

import jax
import jax.numpy as jnp
from jax import lax
from jax.experimental import pallas as pl
from jax.experimental.pallas import tpu as pltpu

F32 = jnp.float32
BF16 = jnp.bfloat16

D_MODEL = 1024
FFN_HIDDEN = 2816
N_HEADS = 8
HEAD_DIM = 64
ROPE_DIM = 32
ROPE_HALF = 16
Q_LORA = 192
KV_LORA = 128
ROPE_THETA = 10000.0
RMS_EPS = 1e-6
MLA_SCALE = (HEAD_DIM + ROPE_DIM) ** -0.5
FOX_SCALE = HEAD_DIM ** -0.5
MLA_QK_LANES = 128
ADAM_LR, ADAM_B1, ADAM_B2, ADAM_EPS, ADAM_WD, ADAM_STEP = 0.001, 0.9, 0.999, 1e-08, 0.01, 10
N_DEV = 8
N_CHIP = 4
W_IN_PIECE = 493
BF16_ROW_TILE = 16
W_IN_PIECE_PAD = 496
LAT_W = 512
LAT_Q, LAT_KV, LAT_KR, LAT_F = 0, 256, 384, 416
MASK_VALUE = -1e30
FIXED_SHIFT_MAX_BOUND = 30.0

TOK_TILE = 512
DW_TOK_TILE = 1024
DW_SPLIT_ROWS = 2048
PROJ_TILE = 512
PREP_TILE = 256
ATT_TILE = 512
ATT_COL_CHUNK = 256
FFN_HID_TILE = 1408
FFN_HID_SPLIT = ((0, 768), (768, 1408))

NT = (((1,), (1,)), ((), ()))
TN = (((0,), (0,)), ((), ()))
NN = (((1,), (0,)), ((), ()))
MESH = pl.DeviceIdType.MESH


def _dot(a, b, dims=NN):
    return lax.dot_general(a, b, dims, preferred_element_type=F32)


def _sds(shape, dtype):
    return jax.ShapeDtypeStruct(shape, dtype)


def _rms_fwd(x, g):
    r = lax.rsqrt(jnp.mean(x * x, axis=-1, keepdims=True) + RMS_EPS)
    return x * r * g, r


def _rms_bwd(dy, x, g, r):
    xn = x * r
    dyg = dy * g
    dx = r * (dyg - xn * jnp.mean(dyg * xn, axis=-1, keepdims=True))
    return dx, dy * xn


def _colsum(x):
    return jnp.sum(x, axis=0, keepdims=True)


def _full(shape):
    return pl.BlockSpec(shape, lambda *_: (0,) * len(shape))


def _tok(tm, n):
    return pl.BlockSpec((tm, n), lambda i, *_: (i, 0))


def _ffn_fwd(x, gain, wg_t, wu_t, wd, target=None, exchange=None, then_to_sibling=False):
    t = x.shape[0]
    tm = min(TOK_TILE, t)
    tf = FFN_HID_TILE
    n_t, n_f = t // tm, FFN_HIDDEN // tf
    with_loss = target is not None

    def body(*refs):
        if with_loss:
            x_ref, g_ref, wg_ref, wu_ref, wd_ref, t_ref, out_ref, a_ref, b_ref, lvec_ref, n_scr, acc = refs
        else:
            x_ref, g_ref, wg_ref, wu_ref, wd_ref, out_ref, a_ref, b_ref, n_scr, acc = refs
        i, j = pl.program_id(0), pl.program_id(1)

        @pl.when(j == 0)
        def _():
            xn, _ = _rms_fwd(x_ref[...], g_ref[...])
            n_scr[...] = xn.astype(BF16)
            acc[...] = jnp.zeros_like(acc)

        n = n_scr[...]
        a = _dot(n, wg_ref[...], NT)
        b = _dot(n, wu_ref[...], NT)
        a_ref[...] = a.astype(BF16)
        b_ref[...] = b.astype(BF16)
        h = (a * jax.nn.sigmoid(a)) * b
        acc[...] += _dot(h.astype(BF16), wd_ref[...])

        @pl.when(j == n_f - 1)
        def _():
            y = x_ref[...] + 0.5 * acc[...]
            if with_loss:
                diff = y - t_ref[...]
                out_ref[...] = diff * (1.0 / D_MODEL)
                sq = _colsum(diff * diff)

                @pl.when(i == 0)
                def _():
                    lvec_ref[...] = sq

                @pl.when(i > 0)
                def _():
                    lvec_ref[...] += sq
            else:
                out_ref[...] = y

    wspec = pl.BlockSpec((tf, D_MODEL), lambda i, j: (j, 0))
    hspec = pl.BlockSpec((tm, tf), lambda i, j: (i, j))
    in_specs = [_tok(tm, D_MODEL), _full((1, D_MODEL)), wspec, wspec, wspec]
    out_specs = [_tok(tm, D_MODEL), hspec, hspec]
    out_shape = [_sds((t, D_MODEL), F32), _sds((t, FFN_HIDDEN), BF16), _sds((t, FFN_HIDDEN), BF16)]
    args = [x, gain, wg_t, wu_t, wd]
    if with_loss:
        in_specs.append(_tok(tm, D_MODEL))
        out_specs.append(_full((1, D_MODEL)))
        out_shape.append(_sds((1, D_MODEL), F32))
        args.append(target)
    return _gridded_call(
        body, name="ffn_fwd_loss" if with_loss else "ffn_fwd", grid=(n_t, n_f), in_specs=in_specs, out_specs=out_specs,
        out_shape=out_shape, scratch_shapes=[pltpu.VMEM((tm, D_MODEL), BF16), pltpu.VMEM((tm, D_MODEL), F32)],
        args=args, exchange=exchange, then_to_sibling=then_to_sibling)


def _ffn_bwd(dy, x, gain, a, b, wg_t, wu_t, wd, name, exchange=None):
    t = x.shape[0]
    tm = min(TOK_TILE, t)
    tf = FFN_HID_TILE
    n_t, n_f = t // tm, FFN_HIDDEN // tf

    def body(dy_ref, x_ref, g_ref, a_ref, b_ref, wg_ref, wu_ref, wd_ref,
             dx_ref, dg_ref, da_ref, db_ref, h_ref, n_ref, dyh_ref, acc):
        i, j = pl.program_id(0), pl.program_id(1)

        @pl.when(j == 0)
        def _():
            xn, _ = _rms_fwd(x_ref[...], g_ref[...])
            n_ref[...] = xn.astype(BF16)
            dyh_ref[...] = (0.5 * dy_ref[...]).astype(BF16)
            acc[...] = jnp.zeros_like(acc)

        dyh = dyh_ref[...]
        for lo, hi in FFN_HID_SPLIT:
            dh = _dot(dyh, wd_ref[lo:hi, :], NT)
            av = a_ref[:, lo:hi].astype(F32)
            bv = b_ref[:, lo:hi].astype(F32)
            s = jax.nn.sigmoid(av)
            silu = av * s
            da = (dh * bv * (s * (1.0 + av * (1.0 - s)))).astype(BF16)
            db = (dh * silu).astype(BF16)
            da_ref[:, lo:hi] = da
            db_ref[:, lo:hi] = db
            h_ref[:, lo:hi] = (silu * bv).astype(BF16)
            acc[...] += _dot(da, wg_ref[lo:hi, :]) + _dot(db, wu_ref[lo:hi, :])

        @pl.when(j == n_f - 1)
        def _():
            xv, g = x_ref[...], g_ref[...]
            r = lax.rsqrt(jnp.mean(xv * xv, axis=-1, keepdims=True) + RMS_EPS)
            dx, dg_rows = _rms_bwd(acc[...], xv, g, r)
            dx_ref[...] = dy_ref[...] + dx
            dg = _colsum(dg_rows)

            @pl.when(i == 0)
            def _():
                dg_ref[...] = dg

            @pl.when(i > 0)
            def _():
                dg_ref[...] += dg

    wspec = pl.BlockSpec((tf, D_MODEL), lambda i, j: (j, 0))
    hspec = pl.BlockSpec((tm, tf), lambda i, j: (i, j))
    tok = _tok(tm, D_MODEL)
    hid = _sds((t, FFN_HIDDEN), BF16)
    return _gridded_call(
        body, name=name, grid=(n_t, n_f),
        in_specs=[tok, tok, _full((1, D_MODEL)), hspec, hspec, wspec, wspec, wspec],
        out_specs=[tok, _full((1, D_MODEL)), hspec, hspec, hspec, tok, tok],
        out_shape=[_sds((t, D_MODEL), F32), _sds((1, D_MODEL), F32), hid, hid, hid,
                   _sds((t, D_MODEL), BF16), _sds((t, D_MODEL), BF16)],
        scratch_shapes=[pltpu.VMEM((tm, D_MODEL), F32)], args=[dy, x, gain, a, b, wg_t, wu_t, wd], exchange=exchange)


def _tn_matmul(a, b, name, exchange=None):
    t, m = a.shape
    n = b.shape[1]
    n_m = 2 if m >= DW_SPLIT_ROWS else 1
    tk = min(DW_TOK_TILE * n_m, t)
    n_k, tm = t // tk, m // n_m

    def body(a_ref, b_ref, o_ref, acc):
        k = pl.program_id(1)
        p = _dot(a_ref[...], b_ref[...], TN)

        @pl.when(k == 0)
        def _():
            acc[...] = p

        @pl.when(k > 0)
        def _():
            acc[...] += p

        @pl.when(k == n_k - 1)
        def _():
            o_ref[...] = acc[...].astype(BF16)

    (out,), got = _gridded_call(
        body, name=name, grid=(n_m, n_k),
        in_specs=[pl.BlockSpec((tk, tm), lambda i, k: (k, i)), pl.BlockSpec((tk, n), lambda i, k: (k, 0))],
        out_specs=[pl.BlockSpec((tm, n), lambda i, k: (i, 0))], out_shape=[_sds((m, n), BF16)],
        scratch_shapes=[pltpu.VMEM((tm, n), F32)], args=[a, b], exchange=exchange)
    return out if exchange is None else (out, got)


def _proj_fwd(x, gain, wgate_t, wfox_t, wlat_t):
    t = x.shape[0]
    tm = min(PROJ_TILE, t)

    def body(x_ref, g_ref, wg_ref, wf_ref, wl_ref, og_ref, of_ref, ol_ref):
        xn, _ = _rms_fwd(x_ref[...], g_ref[...])
        n = xn.astype(BF16)
        og_ref[...] = _dot(n, wg_ref[...], NT)
        of_ref[...] = _dot(n, wf_ref[...], NT)
        ol_ref[...] = _dot(n, wl_ref[...], NT)

    return pl.pallas_call(
        body, name="proj_fwd", grid=(t // tm,),
        in_specs=[_tok(tm, D_MODEL), _full((1, D_MODEL)), _full(wgate_t.shape), _full(wfox_t.shape), _full(wlat_t.shape)],
        out_specs=[_tok(tm, 2 * D_MODEL), _tok(tm, 3 * 512), _tok(tm, LAT_W)],
        out_shape=[_sds((t, 2 * D_MODEL), F32), _sds((t, 3 * 512), F32), _sds((t, LAT_W), F32)],
        compiler_params=pltpu.CompilerParams(dimension_semantics=("arbitrary",)),
    )(x, gain, wgate_t, wfox_t, wlat_t)


def _proj_bwd(dgates, dfox, dlat, wgate_t, wfox_t, wlat_t, x, gain, dres):
    t = x.shape[0]
    tm = min(PROJ_TILE, t)

    def body(dg_ref, df_ref, dl_ref, wg_ref, wf_ref, wl_ref, x_ref, g_ref, dres_ref, dx_ref, dgain_ref, n_ref):
        i = pl.program_id(0)
        dn = _dot(dg_ref[...], wg_ref[...]) + _dot(df_ref[...], wf_ref[...]) + _dot(dl_ref[...], wl_ref[...])
        xv, g = x_ref[...], g_ref[...]
        xn, r = _rms_fwd(xv, g)
        n_ref[...] = xn.astype(BF16)
        dx, dg_rows = _rms_bwd(dn, xv, g, r)
        dx_ref[...] = dres_ref[...] + dx
        dgn = _colsum(dg_rows)

        @pl.when(i == 0)
        def _():
            dgain_ref[...] = dgn

        @pl.when(i > 0)
        def _():
            dgain_ref[...] += dgn

    tok = _tok(tm, D_MODEL)
    return pl.pallas_call(
        body, name="proj_bwd", grid=(t // tm,),
        in_specs=[_tok(tm, 2 * D_MODEL), _tok(tm, 3 * 512), _tok(tm, LAT_W), _full(wgate_t.shape), _full(wfox_t.shape),
                  _full(wlat_t.shape), tok, _full((1, D_MODEL)), tok],
        out_specs=[tok, _full((1, D_MODEL)), tok],
        out_shape=[_sds((t, D_MODEL), F32), _sds((1, D_MODEL), F32), _sds((t, D_MODEL), BF16)],
        compiler_params=pltpu.CompilerParams(dimension_semantics=("arbitrary",)),
    )(dgates, dfox, dlat, wgate_t, wfox_t, wlat_t, x, gain, dres)


def _tri(n, lower):
    r = lax.broadcasted_iota(jnp.int32, (n, n), 0)
    c = lax.broadcasted_iota(jnp.int32, (n, n), 1)
    return ((c <= r) if lower else (c >= r)).astype(F32)


def _log_sigmoid(z):
    return jnp.minimum(z, 0.0) - jnp.log1p(jnp.exp(-jnp.abs(z)))


N_GROUPS = 16
LANES = MLA_QK_LANES


def _split_dot(a, g):
    hi = a.astype(BF16)
    lo = (a - hi.astype(F32)).astype(BF16)
    return _dot(hi, g) + _dot(lo, g)


def _rope_fwd(y, c, s1, s2):
    return y * c + pltpu.roll(y, LANES - ROPE_HALF, 1) * s1 + pltpu.roll(y, ROPE_HALF, 1) * s2


def _rope_bwd(do, c, s1, s2):
    return do * c + pltpu.roll(do * s1, ROPE_HALF, 1) + pltpu.roll(do * s2, LANES - ROPE_HALF, 1)


def _prep_tables(cos, sin):
    t = cos.shape[0]
    z = lambda n: jnp.zeros((t, n), F32)
    c = jnp.concatenate([jnp.ones((t, HEAD_DIM), F32), cos, cos, z(LANES - HEAD_DIM - ROPE_DIM)], axis=1)
    s1 = jnp.concatenate([z(HEAD_DIM), -sin, z(LANES - HEAD_DIM - ROPE_HALF)], axis=1)
    s2 = jnp.concatenate([z(HEAD_DIM + ROPE_HALF), sin, z(LANES - HEAD_DIM - ROPE_DIM)], axis=1)
    lane = jnp.arange(LANES)
    rope_mask = ((lane >= HEAD_DIM) & (lane < HEAD_DIM + ROPE_DIM)).astype(F32)[None, :]
    return c, s1, s2, rope_mask


def _group_matrices():
    lane = jnp.arange(N_HEADS * LANES)
    head, d = (lane // LANES)[:, None], (lane % LANES)[:, None]
    col = jnp.arange(N_GROUPS)[None, :]
    g_mla = ((col == head) & (d < HEAD_DIM)) | ((col == N_HEADS + head) & (d >= HEAD_DIM) & (d < HEAD_DIM + ROPE_DIM))
    g_fox = col == (jnp.arange(N_HEADS * HEAD_DIM) // HEAD_DIM)[:, None]
    inv_mla = jnp.concatenate([jnp.full((1, N_HEADS), 1.0 / HEAD_DIM, F32), jnp.full((1, N_HEADS), 1.0 / ROPE_DIM, F32)], axis=1)
    inv_fox = jnp.full((1, N_GROUPS), 1.0 / HEAD_DIM, F32)
    return g_mla.astype(BF16), g_mla.T.astype(BF16), inv_mla, g_fox.astype(BF16), g_fox.T.astype(BF16), inv_fox


def _interleave_weights(wqb_t, wkvb_t):
    wq = jnp.pad(wqb_t.reshape(N_HEADS, HEAD_DIM + ROPE_DIM, Q_LORA), ((0, 0), (0, LANES - HEAD_DIM - ROPE_DIM), (0, 0)))
    kv = wkvb_t.reshape(N_HEADS, 2, HEAD_DIM, KV_LORA)
    wk = jnp.pad(kv[:, 0], ((0, 0), (0, LANES - HEAD_DIM), (0, 0)))
    return wq.reshape(N_HEADS * LANES, Q_LORA), wk.reshape(N_HEADS * LANES, KV_LORA), kv[:, 1].reshape(N_HEADS * HEAD_DIM, KV_LORA)


def _deinterleave_grads(dwq, dwk, dwv):
    dq = dwq.reshape(N_HEADS, LANES, Q_LORA)[:, :HEAD_DIM + ROPE_DIM].reshape(N_HEADS * (HEAD_DIM + ROPE_DIM), Q_LORA)
    dk = dwk.reshape(N_HEADS, LANES, KV_LORA)[:, :HEAD_DIM]
    dkv = jnp.stack([dk, dwv.reshape(N_HEADS, HEAD_DIM, KV_LORA)], axis=1)
    return dq, dkv.reshape(N_HEADS * 2 * HEAD_DIM, KV_LORA)


def _head_gains(g_qn, g_qr, g_kn, g_kr, g_fq, g_fk):
    z = lambda n: jnp.zeros((1, n), F32)
    gq = jnp.concatenate([g_qn, g_qr, z(LANES - HEAD_DIM - ROPE_DIM)], axis=1)
    gk = jnp.concatenate([g_kn, z(LANES - HEAD_DIM)], axis=1)
    gkr = jnp.concatenate([z(HEAD_DIM), g_kr, z(LANES - HEAD_DIM - ROPE_DIM)], axis=1)
    return jnp.tile(gq, (1, N_HEADS)), jnp.tile(gk, (1, N_HEADS)), gkr, jnp.tile(g_fq, (1, N_HEADS)), jnp.tile(g_fk, (1, N_HEADS))


def _group_rms(x, g, g_t, inv):
    r = lax.rsqrt(_split_dot(x * x, g) * inv + RMS_EPS)
    return _split_dot(r, g_t)


def _value_layout(wv):
    lane = jnp.arange(N_HEADS * LANES)
    src = (lane // LANES) * HEAD_DIM + lane % LANES
    place = (jnp.arange(N_HEADS * HEAD_DIM)[:, None] == src[None, :]) & (lane % LANES < HEAD_DIM)[None, :]
    ones = (lane % LANES == HEAD_DIM).astype(F32)[None, :]
    wv_il = jnp.pad(wv.reshape(N_HEADS, HEAD_DIM, KV_LORA), ((0, 0), (0, LANES - HEAD_DIM), (0, 0)))
    return place.astype(BF16), ones, wv_il.reshape(N_HEADS * LANES, KV_LORA)


def _prep_fwd(fox, lat, tables, mats, wq, wk, wv, g_qlat, g_kvlat, gains, b_f, v_place, v_ones, wv_il):
    t = fox.shape[0]
    tm = min(PREP_TILE, t)
    c_tab, s1_tab, s2_tab, rope_mask = tables

    def body(fox_ref, lat_ref, c_ref, s1_ref, s2_ref, rm_ref, gm_ref, gmt_ref, im_ref, gf_ref, gft_ref, if_ref,
             wq_ref, wk_ref, wv_ref, gql_ref, gkvl_ref, gq_ref, gk_ref, gkr_ref, gfq_ref, gfk_ref, bf_ref,
             vp_ref, vo_ref, wvil_ref,
             fq_ref, fk_ref, fv_ref, cc_ref, mq_ref, mk_ref, mv_ref, fqn_ref, fkmax_ref, mqn_ref, mkmax_ref,
             fv1_ref, mv1_ref, carry):
        i = pl.program_id(0)
        ct, s1, s2 = c_ref[...], s1_ref[...], s2_ref[...]

        @pl.when(i == 0)
        def _():
            carry[...] = jnp.zeros_like(carry)
            fkmax_ref[...] = jnp.zeros_like(fkmax_ref)
            mkmax_ref[...] = jnp.zeros_like(mkmax_ref)

        def square_sums(ref, g):
            v = ref[...].astype(F32)
            return _split_dot(v * v, g)

        xq = fox_ref[:, 0:512]
        fq_ref[...] = (xq * _group_rms(xq, gf_ref[...], gft_ref[...], if_ref[...]) * (gfq_ref[...] * FOX_SCALE)).astype(BF16)
        xk = fox_ref[:, 512:1024]
        fk_ref[...] = (xk * _group_rms(xk, gf_ref[...], gft_ref[...], if_ref[...]) * gfk_ref[...]).astype(BF16)
        fv_b = fox_ref[:, 1024:1536].astype(BF16)
        fv_ref[...] = fv_b
        fv1_ref[...] = (_dot(fv_b, vp_ref[...]) + vo_ref[...]).astype(BF16)
        fqn_ref[...] = square_sums(fq_ref, gf_ref[...])
        fkmax_ref[...] = jnp.maximum(fkmax_ref[...], jnp.max(square_sums(fk_ref, gf_ref[...]), axis=0, keepdims=True))


        logf = _log_sigmoid(lat_ref[:, LAT_F:LAT_F + N_HEADS] + bf_ref[...])
        cc_ref[...] = jnp.dot(_tri(tm, True), logf, precision=lax.Precision.HIGHEST, preferred_element_type=F32) + carry[...]
        carry[...] += _colsum(logf)

        qlat_n, _ = _rms_fwd(lat_ref[:, LAT_Q:LAT_Q + Q_LORA], gql_ref[...])
        p = _dot(qlat_n.astype(BF16), wq_ref[...], NT)
        y = p * _group_rms(p, gm_ref[...], gmt_ref[...], im_ref[...]) * (gq_ref[...] * MLA_SCALE)
        for h in range(N_HEADS):
            w = slice(h * LANES, (h + 1) * LANES)
            mq_ref[:, w] = _rope_fwd(y[:, w], ct, s1, s2).astype(BF16)

        kv_n, _ = _rms_fwd(lat_ref[:, LAT_KV:LAT_KV + KV_LORA], gkvl_ref[...])
        kv_b = kv_n.astype(BF16)
        pk = _dot(kv_b, wk_ref[...], NT)
        kn = pk * _group_rms(pk, gm_ref[...], gmt_ref[...], im_ref[...]) * gk_ref[...]
        rm = rm_ref[...]
        kr = pltpu.roll(lat_ref[:, LAT_KR:LAT_KR + LANES], HEAD_DIM, 1) * rm
        rr = lax.rsqrt(jnp.sum(kr * kr, axis=1, keepdims=True) * (1.0 / ROPE_DIM) + RMS_EPS)
        okr = _rope_fwd(kr * rr * gkr_ref[...], ct * rm, s1, s2)
        for h in range(N_HEADS):
            w = slice(h * LANES, (h + 1) * LANES)
            mk_ref[:, w] = (kn[:, w] + okr).astype(BF16)
        mv_ref[...] = _dot(kv_b, wv_ref[...], NT).astype(BF16)
        mv1_ref[...] = (_dot(kv_b, wvil_ref[...], NT).astype(BF16).astype(F32) + vo_ref[...]).astype(BF16)
        mqn_ref[...] = square_sums(mq_ref, gm_ref[...])
        mkmax_ref[...] = jnp.maximum(mkmax_ref[...], jnp.max(square_sums(mk_ref, gm_ref[...]), axis=0, keepdims=True))

    consts = [rope_mask, *mats, wq, wk, wv, g_qlat, g_kvlat, *gains, b_f, v_place, v_ones, wv_il]
    t512, tl, tg, g1 = _tok(tm, 512), _tok(tm, LANES), _tok(tm, N_GROUPS), _full((1, N_GROUPS))
    t1024 = _tok(tm, N_HEADS * LANES)
    return pl.pallas_call(
        body, name="prep_fwd", grid=(t // tm,),
        in_specs=[_tok(tm, 1536), _tok(tm, LAT_W), tl, tl, tl] + [_full(a.shape) for a in consts],
        out_specs=[t512, t512, t512, _tok(tm, N_HEADS), t1024, t1024, t512, tg, g1, tg, g1, t1024, t1024],
        out_shape=[_sds((t, 512), BF16), _sds((t, 512), BF16), _sds((t, 512), BF16), _sds((t, N_HEADS), F32),
                   _sds((t, N_HEADS * LANES), BF16), _sds((t, N_HEADS * LANES), BF16), _sds((t, 512), BF16),
                   _sds((t, N_GROUPS), F32), _sds((1, N_GROUPS), F32), _sds((t, N_GROUPS), F32), _sds((1, N_GROUPS), F32)]
        + [_sds((t, N_HEADS * LANES), BF16)] * 2,
        scratch_shapes=[pltpu.VMEM((1, N_HEADS), F32)],
        compiler_params=pltpu.CompilerParams(dimension_semantics=("arbitrary",)),
    )(fox, lat, c_tab, s1_tab, s2_tab, *consts)


def _prep_bwd(fox, lat, tables, mats, wq, wk, wv, g_qlat, g_kvlat, gains, b_f, dfq, dfk, dfv, dc, dmq, dmk, dmv):
    t = fox.shape[0]
    tm = min(PREP_TILE, t)
    n_t = t // tm
    c_tab, s1_tab, s2_tab, rope_mask = tables

    def body(fox_ref, lat_ref, c_ref, s1_ref, s2_ref, rm_ref, gm_ref, gmt_ref, im_ref, gf_ref, gft_ref, if_ref,
             wq_ref, wk_ref, wv_ref, gql_ref, gkvl_ref, gq_ref, gk_ref, gkr_ref, gfq_ref, gfk_ref, bf_ref,
             dfq_ref, dfk_ref, dfv_ref, dc_ref, dmq_ref, dmk_ref, dmv_ref,
             dfox_ref, dlat_ref, dwq_ref, dwk_ref, dwv_ref, o_gql, o_gkvl, o_gq, o_gk, o_gkr, o_gfq, o_gfk, o_bf,
             carry, lscr, wq_acc, wk_acc, wv_acc, dyscr):
        i = pl.program_id(0)
        ct, s1, s2, rm = c_ref[...], s1_ref[...], s2_ref[...], rm_ref[...]

        @pl.when(i == 0)
        def _():
            for ref in [carry, wq_acc, wk_acc, wv_acc, o_gql, o_gkvl, o_gq, o_gk, o_gkr, o_gfq, o_gfk, o_bf]:
                ref[...] = jnp.zeros_like(ref)

        def group_rms_bwd(dy, x, r, gain, g, g_t, inv):
            xn = x * r
            dyg = dy * gain
            mu = _split_dot(_split_dot(dyg * xn, g) * inv, g_t)
            return r * (dyg - xn * mu), dy * xn

        gf, gft, invf = gf_ref[...], gft_ref[...], if_ref[...]
        xq = fox_ref[:, 0:512]
        dxq, rows = group_rms_bwd(dfq_ref[...] * FOX_SCALE, xq, _group_rms(xq, gf, gft, invf), gfq_ref[...], gf, gft, invf)
        dfox_ref[:, 0:512] = dxq.astype(BF16)
        o_gfq[...] += _colsum(rows)
        xk = fox_ref[:, 512:1024]
        dxk, rows = group_rms_bwd(dfk_ref[...].T, xk, _group_rms(xk, gf, gft, invf), gfk_ref[...], gf, gft, invf)
        dfox_ref[:, 512:1024] = dxk.astype(BF16)
        o_gfk[...] += _colsum(rows)
        dfox_ref[:, 1024:1536] = dfv_ref[...].T.astype(BF16)

        lscr[...] = jnp.zeros_like(lscr)

        xql = lat_ref[:, LAT_Q:LAT_Q + Q_LORA]
        qlat_n, r_ql = _rms_fwd(xql, gql_ref[...])
        qlat_b = qlat_n.astype(BF16)
        gm, gmt, invm = gm_ref[...], gmt_ref[...], im_ref[...]
        p = _dot(qlat_b, wq_ref[...], NT)
        for h in range(N_HEADS):
            w = slice(h * LANES, (h + 1) * LANES)
            dyscr[:, w] = _rope_bwd(dmq_ref[:, w], ct, s1, s2) * MLA_SCALE
        dp, rows = group_rms_bwd(dyscr[...], p, _group_rms(p, gm, gmt, invm), gq_ref[...], gm, gmt, invm)
        o_gq[...] += _colsum(rows)
        dp_b = dp.astype(BF16)
        wq_acc[...] += _dot(dp_b, qlat_b, TN)
        dxql, rows = _rms_bwd(_dot(dp_b, wq_ref[...]), xql, gql_ref[...], r_ql)
        lscr[:, LAT_Q:LAT_Q + Q_LORA] = dxql
        o_gql[...] += _colsum(rows)

        xkv = lat_ref[:, LAT_KV:LAT_KV + KV_LORA]
        kv_n, r_kv = _rms_fwd(xkv, gkvl_ref[...])
        kv_b = kv_n.astype(BF16)
        pk = _dot(kv_b, wk_ref[...], NT)
        dmk = dmk_ref[...].T
        dpk, rows = group_rms_bwd(dmk, pk, _group_rms(pk, gm, gmt, invm), gk_ref[...], gm, gmt, invm)
        o_gk[...] += _colsum(rows)
        dpk_b = dpk.astype(BF16)
        dv_b = dmv_ref[...].T.astype(BF16)
        wk_acc[...] += _dot(dpk_b, kv_b, TN)
        wv_acc[...] += _dot(dv_b, kv_b, TN)
        dxkv, rows = _rms_bwd(_dot(dpk_b, wk_ref[...]) + _dot(dv_b, wv_ref[...]), xkv, gkvl_ref[...], r_kv)
        lscr[:, LAT_KV:LAT_KV + KV_LORA] = dxkv
        o_gkvl[...] += _colsum(rows)

        dokr = dmk[:, 0:LANES]
        for h in range(1, N_HEADS):
            dokr = dokr + dmk[:, h * LANES:(h + 1) * LANES]
        dykr = _rope_bwd(dokr * rm, ct * rm, s1, s2)
        kr = pltpu.roll(lat_ref[:, LAT_KR:LAT_KR + LANES], HEAD_DIM, 1) * rm
        rr = lax.rsqrt(jnp.sum(kr * kr, axis=1, keepdims=True) * (1.0 / ROPE_DIM) + RMS_EPS)
        krn = kr * rr
        dyg = dykr * gkr_ref[...]
        dkr = rr * (dyg - krn * (jnp.sum(dyg * krn, axis=1, keepdims=True) * (1.0 / ROPE_DIM)))
        o_gkr[...] += _colsum(dykr * krn)
        lscr[:, LAT_KR:LAT_KR + LANES] = pltpu.roll(dkr, LANES - HEAD_DIM, 1)

        dcv = dc_ref[...]
        dlogf = jnp.dot(_tri(tm, False), dcv, precision=lax.Precision.HIGHEST, preferred_element_type=F32) + carry[...]
        carry[...] += _colsum(dcv)
        dz = dlogf * jax.nn.sigmoid(-(lat_ref[:, LAT_F:LAT_F + N_HEADS] + bf_ref[...]))
        lscr[:, LAT_F:LAT_F + N_HEADS] = dz
        o_bf[...] += _colsum(dz)

        dlat_ref[...] = lscr[...].astype(BF16)

        @pl.when(i == n_t - 1)
        def _():
            dwq_ref[...] = wq_acc[...].astype(BF16)
            dwk_ref[...] = wk_acc[...].astype(BF16)
            dwv_ref[...] = wv_acc[...].astype(BF16)

    consts = [rope_mask, *mats, wq, wk, wv, g_qlat, g_kvlat, *gains, b_f]

    def rtok(n):
        return pl.BlockSpec((tm, n), lambda i: (n_t - 1 - i, 0))

    def rtok_t(n):
        return pl.BlockSpec((n, tm), lambda i: (0, n_t - 1 - i))

    sums =[(1, Q_LORA), (1, KV_LORA), (1, N_HEADS * LANES), (1, N_HEADS * LANES), (1, LANES), (1, 512), (1, 512), (1, N_HEADS)]
    return pl.pallas_call(
        body, name="prep_bwd", grid=(n_t,),
        in_specs=[rtok(1536), rtok(LAT_W), rtok(LANES), rtok(LANES), rtok(LANES)] + [_full(a.shape) for a in consts]
        + [rtok(512), rtok_t(512), rtok_t(512), rtok(N_HEADS), rtok(N_HEADS * LANES), rtok_t(N_HEADS * LANES), rtok_t(512)],
        out_specs=[rtok(1536), rtok(LAT_W), _full(wq.shape), _full(wk.shape), _full(wv.shape)] + [_full(s) for s in sums],
        out_shape=[_sds((t, 1536), BF16), _sds((t, LAT_W), BF16), _sds(wq.shape, BF16), _sds(wk.shape, BF16), _sds(wv.shape, BF16)]
        + [_sds(s, F32) for s in sums],
        scratch_shapes=[pltpu.VMEM((1, N_HEADS), F32), pltpu.VMEM((tm, LAT_W), F32), pltpu.VMEM(wq.shape, F32),
                        pltpu.VMEM(wk.shape, F32), pltpu.VMEM(wv.shape, F32), pltpu.VMEM((tm, N_HEADS * LANES), F32)],
        compiler_params=pltpu.CompilerParams(dimension_semantics=("arbitrary",)),
    )(fox, lat, c_tab, s1_tab, s2_tab, *consts, dfq, dfk, dfv, dc, dmq, dmk, dmv)


def _causal_pairs(n, query_major):
    pairs = [(i, j) for i in range(n) for j in range(i + 1)] if query_major else [(i, j) for j in range(n) for i in range(j, n)]
    return jnp.asarray([p[0] for p in pairs], jnp.int32), jnp.asarray([p[1] for p in pairs], jnp.int32)


def _logit_bound(q_sq, k_sq_max):
    qn = q_sq[:, :N_HEADS] + q_sq[:, N_HEADS:]
    kmax = k_sq_max[:, :N_HEADS] + k_sq_max[:, N_HEADS:]
    bound = jnp.sqrt(qn * kmax) * (1.0 + 2.0 ** -10) + 2.0 ** -10
    flag = (jnp.max(bound) <= FIXED_SHIFT_MAX_BOUND).astype(F32).reshape(1, 1)
    return bound, flag


def _attn_fwd(q, k, v, bound, fixed_ok, c, c_t, *, lanes, name, exchange=None):
    t = q.shape[0]
    tq = min(ATT_TILE, t)
    n_q = t // tq
    hd = HEAD_DIM
    ch = min(ATT_COL_CHUNK, tq)
    decay = c is not None

    def body(qi_ref, kj_ref, *refs):
        if decay:
            q_ref, k_ref, v_ref, b_ref, ok_ref, c_ref, ct_ref, o_ref, o32_ref, lse_ref, m_scr, acc = refs
        else:
            q_ref, k_ref, v_ref, b_ref, ok_ref, o_ref, o32_ref, lse_ref, m_scr, acc = refs
        i, j = qi_ref[pl.program_id(0)], kj_ref[pl.program_id(0)]
        fixed = ok_ref[0, 0] > 0.5

        @pl.when(j == 0)
        def _():
            m_scr[...] = jnp.full_like(m_scr, MASK_VALUE)
            acc[...] = jnp.zeros_like(acc)

        def fixed_step(diagonal):
            for h in range(N_HEADS):
                wl = slice(h * lanes, (h + 1) * lanes)
                wv = slice(h * LANES, (h + 1) * LANES)
                qh = q_ref[:, wl]
                row = (c_ref[:, h:h + 1] - b_ref[:, h:h + 1]) if decay else -b_ref[:, h:h + 1]
                o_hi = jnp.zeros((tq, LANES), F32)
                o_lo = jnp.zeros((tq, LANES), F32)
                for cc in range(tq // ch):
                    cols = slice(cc * ch, (cc + 1) * ch)
                    s = _dot(qh, k_ref[cols, wl], NT)
                    s = s + ((row - ct_ref[h:h + 1, cols]) if decay else row)
                    if diagonal:
                        keep = (lax.broadcasted_iota(jnp.int32, (tq, ch), 0)
                                >= lax.broadcasted_iota(jnp.int32, (tq, ch), 1) + cc * ch)
                        s = jnp.where(keep, s, MASK_VALUE)
                    p = jnp.exp(s)
                    p_b = p.astype(BF16)
                    o_hi = o_hi + _dot(p_b, v_ref[cols, wv])
                    if decay:
                        o_lo = o_lo + _dot((p - p_b.astype(F32)).astype(BF16), v_ref[cols, wv])
                acc[0, :, wv] += o_hi
                if decay:
                    acc[1, :, wv] += o_lo

        def step(diagonal):
            if diagonal:
                keep = lax.broadcasted_iota(jnp.int32, (tq, tq), 0) >= lax.broadcasted_iota(jnp.int32, (tq, tq), 1)
            for h in range(N_HEADS):
                s = _dot(q_ref[:, h * lanes:(h + 1) * lanes], k_ref[:, h * lanes:(h + 1) * lanes], NT)
                if decay:
                    s = s + (c_ref[:, h:h + 1] - ct_ref[h:h + 1, :])
                if diagonal:
                    s = jnp.where(keep, s, MASK_VALUE)
                m_prev = m_scr[h]
                m_new = jnp.maximum(m_prev, jnp.max(s, axis=1, keepdims=True))
                alpha = jnp.exp(m_prev - m_new)
                p = jnp.exp(s - m_new)
                wv = slice(h * LANES, (h + 1) * LANES)
                p_b = p.astype(BF16)
                acc[0, :, wv] = alpha * acc[0, :, wv] + _dot(p_b, v_ref[:, wv])
                if decay:
                    acc[1, :, wv] = alpha * acc[1, :, wv] + _dot((p - p_b.astype(F32)).astype(BF16), v_ref[:, wv])
                m_scr[h] = m_new

        for diagonal, here in ((False, j < i), (True, j == i)):
            @pl.when(here & fixed)
            def _():
                fixed_step(diagonal)

            @pl.when(here & jnp.logical_not(fixed))
            def _():
                step(diagonal)

        @pl.when(j == i)
        def _():
            for h in range(N_HEADS):
                w = slice(h * hd, (h + 1) * hd)
                val, one = slice(h * LANES, h * LANES + hd), slice(h * LANES + hd, h * LANES + hd + 1)
                l = (acc[0, :, one] + acc[1, :, one]) if decay else acc[0, :, one]
                o_ref[:, w] = (acc[0, :, val] / l).astype(BF16)
                o32_ref[:, w] = ((acc[0, :, val] + acc[1, :, val]) if decay else acc[0, :, val]) / l
                lse_ref[:, h:h + 1] = jnp.where(fixed, b_ref[:, h:h + 1], m_scr[h]) + jnp.log(l)

    qspec = lambda n: pl.BlockSpec((tq, n), lambda s, qi, kj: (qi[s], 0))
    kspec = lambda n: pl.BlockSpec((tq, n), lambda s, qi, kj: (kj[s], 0))
    in_specs = [qspec(N_HEADS * lanes), kspec(N_HEADS * lanes), kspec(N_HEADS * LANES), qspec(N_HEADS),
                pl.BlockSpec(memory_space=pltpu.SMEM)]
    args = [q, k, v, bound, fixed_ok]
    if decay:
        in_specs += [qspec(N_HEADS), pl.BlockSpec((N_HEADS, tq), lambda s, qi, kj: (0, kj[s]))]
        args += [c, c_t]
    pairs = _causal_pairs(n_q, query_major=True)
    return _gridded_call(
        body, name=name, grid=(pairs[0].shape[0],), in_specs=in_specs, out_specs=[qspec(512), qspec(512), qspec(N_HEADS)],
        out_shape=[_sds((t, 512), BF16), _sds((t, 512), F32), _sds((t, N_HEADS), F32)],
        scratch_shapes=[pltpu.VMEM((N_HEADS, tq, 1), F32), pltpu.VMEM((2, tq, N_HEADS * LANES), F32)],
        args=args, exchange=exchange, prefetch=pairs)


def _attn_bwd(q, k, v, do, delta, lse, c, c_t, *, lanes, name, exchange=None):
    t = q.shape[0]
    tq = min(ATT_TILE, t)
    n_q = t // tq
    hd = HEAD_DIM
    decay = c is not None

    pairs = _causal_pairs(n_q, query_major=False)
    n_steps = pairs[0].shape[0]

    def body(qi_ref, kj_ref, *refs):
        if decay:
            q_ref, k_ref, v_ref, do_ref, delta_ref, lse_ref, c_ref, ct_ref, dq_hbm, dk_ref, dv_ref, dct_ref, dq_ref = refs
        else:
            q_ref, k_ref, v_ref, do_ref, delta_ref, lse_ref, dq_hbm, dk_ref, dv_ref, dq_ref = refs
        step_id = pl.program_id(0)
        i, j = qi_ref[step_id], kj_ref[step_id]

        @pl.when(step_id == 0)
        def _():
            dq_ref[...] = jnp.zeros_like(dq_ref)

        @pl.when(i == j)
        def _():
            dk_ref[...] = jnp.zeros_like(dk_ref)
            dv_ref[...] = jnp.zeros_like(dv_ref)
            if decay:
                dct_ref[...] = jnp.zeros_like(dct_ref)

        def step(diagonal):
            if diagonal:
                keep = lax.broadcasted_iota(jnp.int32, (tq, tq), 0) >= lax.broadcasted_iota(jnp.int32, (tq, tq), 1)
            rows = pl.ds(pl.multiple_of(i * tq, tq), tq)
            for h in range(N_HEADS):
                wl = slice(h * lanes, (h + 1) * lanes)
                w = slice(h * hd, (h + 1) * hd)
                qh, kh = q_ref[:, wl], k_ref[:, wl]
                s = _dot(qh, kh, NT)
                if decay:
                    s = s + (c_ref[:, h:h + 1] - ct_ref[h:h + 1, :])
                if diagonal:
                    s = jnp.where(keep, s, MASK_VALUE)
                p = jnp.exp(s - lse_ref[:, h:h + 1])
                doh = do_ref[:, w]
                dv_ref[w, :] += _dot(doh, p.astype(BF16), TN)
                dp = _dot(doh, v_ref[:, w], NT)
                ds = p * (dp - delta_ref[:, h:h + 1])
                if decay:
                    dct_ref[h:h + 1, :] -= _colsum(ds)
                ds_b = ds.astype(BF16)
                dk_ref[wl, :] += _dot(qh, ds_b, TN)
                dq_ref[rows, wl] += _dot(ds_b, kh)

        @pl.when(i > j)
        def _():
            step(False)

        @pl.when(i == j)
        def _():
            step(True)

        @pl.when(step_id == n_steps - 1)
        def _():
            pltpu.sync_copy(dq_ref, dq_hbm)

    qspec = lambda n: pl.BlockSpec((tq, n), lambda s, qi, kj: (qi[s], 0))
    kspec = lambda n: pl.BlockSpec((tq, n), lambda s, qi, kj: (kj[s], 0))
    in_specs = [qspec(N_HEADS * lanes), kspec(N_HEADS * lanes), kspec(512), qspec(512), qspec(N_HEADS), qspec(N_HEADS)]
    kspec_t = lambda n: pl.BlockSpec((n, tq), lambda s, qi, kj: (0, kj[s]))
    out_specs = [pl.BlockSpec(memory_space=pl.ANY), kspec_t(N_HEADS * lanes), kspec_t(512)]
    out_shape = [_sds((t, N_HEADS * lanes), F32), _sds((N_HEADS * lanes, t), F32), _sds((512, t), F32)]
    args = [q, k, v, do, delta, lse]
    if decay:
        ctspec = pl.BlockSpec((N_HEADS, tq), lambda s, qi, kj: (0, kj[s]))
        in_specs += [qspec(N_HEADS), ctspec]
        out_specs.append(ctspec)
        out_shape.append(_sds((N_HEADS, t), F32))
        args += [c, c_t]
    return _gridded_call(body, name=name, grid=(n_steps,), in_specs=in_specs, out_specs=out_specs, out_shape=out_shape,
                         scratch_shapes=[pltpu.VMEM((t, N_HEADS * lanes), F32)], args=args, exchange=exchange, prefetch=pairs)


def _mix_fwd(x, y_mla, y_fox, gates, b_gate, wbm_t, wbf_t, wo):
    t = x.shape[0]
    tm = min(PROJ_TILE, t)

    def body(x_ref, ym_ref, yf_ref, gt_ref, bg_ref, wbm_ref, wbf_ref, wo_ref, out_ref):
        um = _dot(ym_ref[...], wbm_ref[...], NT)
        uf = _dot(yf_ref[...], wbf_ref[...], NT)
        sm = jax.nn.sigmoid(gt_ref[:, 0:D_MODEL] + bg_ref[0:1, :])
        sf = jax.nn.sigmoid(gt_ref[:, D_MODEL:2 * D_MODEL] + bg_ref[1:2, :])
        mixed = sm * um + sf * uf
        out_ref[...] = x_ref[...] + _dot(mixed.astype(BF16), wo_ref[...])

    tok = _tok(tm, D_MODEL)
    return pl.pallas_call(
        body, name="mix_fwd", grid=(t // tm,),
        in_specs=[tok, _tok(tm, 512), _tok(tm, 512), _tok(tm, 2 * D_MODEL), _full((2, D_MODEL)), _full(wbm_t.shape),
                  _full(wbf_t.shape), _full(wo.shape)],
        out_specs=tok, out_shape=_sds((t, D_MODEL), F32),
        compiler_params=pltpu.CompilerParams(dimension_semantics=("arbitrary",)),
    )(x, y_mla, y_fox, gates, b_gate, wbm_t, wbf_t, wo)


def _mix_bwd(dx, y_mla, y_fox, y_mla32, y_fox32, gates, b_gate, wbm_t, wbf_t, wo, head_groups, exchange=None):
    t = dx.shape[0]
    tm = min(PROJ_TILE, t)

    def body(dx_ref, ym_ref, yf_ref, ym32_ref, yf32_ref, gt_ref, bg_ref, wbm_ref, wbf_ref, wo_ref, gf_ref,
             dym_ref, dyf_ref, dlm_ref, dlf_ref, dgt_ref, mixed_ref, dum_ref, duf_ref, dxb_ref, dbg_ref):
        i = pl.program_id(0)
        dxb = dx_ref[...].astype(BF16)
        dxb_ref[...] = dxb
        dmixed = _dot(dxb, wo_ref[...], NT)
        um = _dot(ym_ref[...], wbm_ref[...], NT)
        uf = _dot(yf_ref[...], wbf_ref[...], NT)
        sm = jax.nn.sigmoid(gt_ref[:, 0:D_MODEL] + bg_ref[0:1, :])
        sf = jax.nn.sigmoid(gt_ref[:, D_MODEL:2 * D_MODEL] + bg_ref[1:2, :])
        mixed_ref[...] = (sm * um + sf * uf).astype(BF16)
        dum = (dmixed * sm).astype(BF16)
        duf = (dmixed * sf).astype(BF16)
        dum_ref[...] = dum
        duf_ref[...] = duf
        dgm = dmixed * um * (sm * (1.0 - sm))
        dgf = dmixed * uf * (sf * (1.0 - sf))
        dgt_ref[:, 0:D_MODEL] = dgm.astype(BF16)
        dgt_ref[:, D_MODEL:2 * D_MODEL] = dgf.astype(BF16)
        for du, wb_ref, y32_ref, dy_ref, dl_ref in ((dum, wbm_ref, ym32_ref, dym_ref, dlm_ref),
                                                    (duf, wbf_ref, yf32_ref, dyf_ref, dlf_ref)):
            dy = _dot(du, wb_ref[...])
            dy_ref[...] = dy.astype(BF16)
            prod = dy.astype(BF16).astype(F32) * y32_ref[...]
            hi = prod.astype(BF16)
            rest = prod - hi.astype(F32)
            mid = rest.astype(BF16)
            dl_ref[...] = _dot(hi, gf_ref[...]) + _dot(mid, gf_ref[...]) + _dot((rest - mid.astype(F32)).astype(BF16), gf_ref[...])

        @pl.when(i == 0)
        def _():
            dbg_ref[...] = jnp.zeros_like(dbg_ref)

        dbg_ref[0:1, :] += _colsum(dgm)
        dbg_ref[1:2, :] += _colsum(dgf)

    tok = _tok(tm, D_MODEL)
    tokb = _sds((t, D_MODEL), BF16)
    t512, tg = _tok(tm, 512), _tok(tm, N_GROUPS)
    return _gridded_call(
        body, name="mix_bwd", grid=(t // tm,),
        in_specs=[tok, t512, t512, t512, t512, _tok(tm, 2 * D_MODEL), _full((2, D_MODEL)), _full(wbm_t.shape),
                  _full(wbf_t.shape), _full(wo.shape), _full(head_groups.shape)],
        out_specs=[t512, t512, tg, tg, _tok(tm, 2 * D_MODEL), tok, tok, tok, tok, _full((2, D_MODEL))],
        out_shape=[_sds((t, 512), BF16), _sds((t, 512), BF16), _sds((t, N_GROUPS), F32), _sds((t, N_GROUPS), F32),
                   _sds((t, 2 * D_MODEL), BF16), tokb, tokb, tokb, tokb, _sds((2, D_MODEL), F32)],
        scratch_shapes=[], args=[dx, y_mla, y_fox, y_mla32, y_fox32, gates, b_gate, wbm_t, wbf_t, wo, head_groups],
        exchange=exchange)


def _my_position():
    x, y, c = lax.axis_index("x"), lax.axis_index("y"), lax.axis_index("c")
    return x, y, c, 4 * x + 2 * y + c


def _peer(x, y, c, mask):
    px = 1 - x if mask & 4 else x
    py = 1 - y if mask & 2 else y
    pc = 1 - c if mask & 1 else c
    return (px, py, pc), 4 * px + 2 * py + pc


def _chip_peer(x, y, km):
    px = 1 - x if km & 2 else x
    py = 1 - y if km & 1 else y
    return px, py, 2 * px + py


_HBM = pl.BlockSpec(memory_space=pl.ANY)


def _wait_all(copies):
    for cp in copies:
        cp.wait()


class _ChipExchange:
    def __init__(self, gather, arrays):
        self.gather, self.arrays, self.aliased = gather, list(arrays), False
        n = len(self.arrays)
        self.out_shape = [_sds((N_DEV * a.shape[0],) + a.shape[1:], a.dtype) if gather else _sds(a.shape, a.dtype)
                          for a in self.arrays]
        self.scratch_shapes = [pltpu.SemaphoreType.DMA((n, N_CHIP)), pltpu.SemaphoreType.DMA((n, N_CHIP)),
                               pltpu.SemaphoreType.DMA((n,))]

    def copies(self, srcs, dsts, send_sems, recv_sems, local_sems):
        x, y, c, me = _my_position()
        q_me = 2 * x + y
        out = []
        for a in range(len(self.arrays)):
            if self.gather:
                r = srcs[a].shape[0]
                local_src, dst = srcs[a], dsts[a].at[pl.ds(me * r, r)]
            else:
                local_src, dst = srcs[a].at[q_me], dsts[a].at[q_me]
            out.append(pltpu.make_async_copy(local_src, dst, local_sems.at[a]))
            for km in range(1, N_CHIP):
                px, py, q_peer = _chip_peer(x, y, km)
                out.append(pltpu.make_async_remote_copy(
                    src_ref=srcs[a] if self.gather else srcs[a].at[q_peer], dst_ref=dst, send_sem=send_sems.at[a, km],
                    recv_sem=recv_sems.at[a, km], device_id=(px, py, c), device_id_type=MESH))
        return out

    def standalone(self, name):
        n = len(self.arrays)

        def body(*refs):
            copies = self.copies(refs[:n], refs[n:2 * n], *refs[2 * n:])
            for cp in copies:
                cp.start()
            _wait_all(copies)

        return pl.pallas_call(body, name=name, in_specs=[_HBM] * n, out_specs=[_HBM] * n, out_shape=self.out_shape,
                              scratch_shapes=self.scratch_shapes)(*self.arrays)


def _gridded_call(body, *, name, grid, in_specs, out_specs, out_shape, scratch_shapes, args, exchange=None, prefetch=(),
                  then_to_sibling=False):
    params = pltpu.CompilerParams(dimension_semantics=("arbitrary",) * len(grid))
    n_pre, n_in, n_out, n_scr = len(prefetch), len(in_specs), len(out_specs), len(scratch_shapes)
    n_x = 0 if exchange is None else len(exchange.arrays)
    n_xsem = 0 if exchange is None else len(exchange.scratch_shapes)
    second = _CoreExchange(True, exchange.out_shape) if then_to_sibling else None

    def carrier(*refs):
        pre, refs = refs[:n_pre], refs[n_pre:]
        ins, x_src, refs = refs[:n_in], refs[n_in:n_in + n_x], refs[n_in + n_x:]
        outs, x_dst, refs = refs[:n_out], refs[n_out:n_out + n_x], refs[n_out + n_x:]
        copies = exchange.copies(x_src, x_dst, *refs[n_scr:n_scr + n_xsem])
        pids = [pl.program_id(d) for d in range(len(grid))]
        first = functools.reduce(jnp.logical_and, [p == 0 for p in pids])
        last = functools.reduce(jnp.logical_and, [p == g - 1 for p, g in zip(pids, grid)])

        @pl.when(first)
        def _():
            for cp in copies:
                cp.start()

        body(*pre, *ins, *outs, *refs[:n_scr])

        @pl.when(last)
        def _():
            _wait_all(copies)
            if second is not None:
                onward = second.copies(x_dst, x_dst, *refs[n_scr + n_xsem:])
                for cp in onward:
                    cp.start()
                _wait_all(onward)

    x_shapes, x_scratch, x_args = ([], [], []) if exchange is None else (exchange.out_shape, exchange.scratch_shapes, exchange.arrays)
    if second is not None:
        x_scratch = x_scratch + second.scratch_shapes
    spec = pltpu.PrefetchScalarGridSpec(
        num_scalar_prefetch=n_pre, grid=grid, in_specs=list(in_specs) + [_HBM] * n_x, out_specs=list(out_specs) + [_HBM] * n_x,
        scratch_shapes=list(scratch_shapes) + x_scratch)
    in_place = {n_pre + n_in + k: n_out + k for k in range(n_x)} if n_x and exchange.aliased else {}
    res = pl.pallas_call(body if exchange is None else carrier, name=name, grid_spec=spec, out_shape=list(out_shape) + x_shapes,
                         input_output_aliases=in_place, compiler_params=params)(*prefetch, *args, *x_args)
    return res[:n_out], (None if exchange is None else res[n_out:])


class _CoreExchange:
    def __init__(self, gather, arrays):
        self.gather, self.arrays, self.aliased = gather, list(arrays), gather
        n = len(self.arrays)
        self.out_shape = [_sds(a.shape, a.dtype) if gather else _sds((N_CHIP, a.shape[0] // N_DEV) + a.shape[1:], a.dtype)
                          for a in self.arrays]
        self.scratch_shapes = [pltpu.SemaphoreType.DMA((n, N_CHIP)), pltpu.SemaphoreType.DMA((n, N_CHIP))]

    def copies(self, srcs, dsts, send_sems, recv_sems):
        x, y, c, _ = _my_position()
        out = []
        for a in range(len(self.arrays)):
            r = srcs[a].shape[0] // N_DEV
            for q in range(N_CHIP):
                if self.gather:
                    rows = pl.ds((2 * q + c) * r, r)
                    src, dst = srcs[a].at[rows], dsts[a].at[rows]
                else:
                    src, dst = srcs[a].at[pl.ds((2 * q + 1 - c) * r, r)], dsts[a].at[q]
                out.append(pltpu.make_async_remote_copy(src_ref=src, dst_ref=dst, send_sem=send_sems.at[a, q],
                                                        recv_sem=recv_sems.at[a, q], device_id=(x, y, 1 - c), device_id_type=MESH))
        return out

    def standalone(self, name):
        n = len(self.arrays)

        def body(*refs):
            copies = self.copies(refs[:n], refs[n:2 * n], *refs[2 * n:])
            for cp in copies:
                cp.start()
            _wait_all(copies)

        return pl.pallas_call(body, name=name, in_specs=[_HBM] * n, out_specs=[_HBM] * n, out_shape=self.out_shape,
                              input_output_aliases={a: a for a in range(n)} if self.aliased else {},
                              scratch_shapes=self.scratch_shapes)(*self.arrays)


def _grads_to_sibling(grads, name):
    return _CoreExchange(False, grads).standalone(name)


def _pair_sum(grad, from_sibling, name):
    r, n = from_sibling.shape[1:]

    def body(g_ref, s_ref, o_ref):
        c = lax.axis_index("c")
        o_ref[...] = (g_ref[c].astype(F32) + s_ref[...].astype(F32)).astype(BF16)

    return pl.pallas_call(
        body, name=name, grid=(N_CHIP,),
        in_specs=[pl.BlockSpec((None, 2, r, n), lambda q: (q, 0, 0, 0)), pl.BlockSpec((None, r, n), lambda q: (q, 0, 0))],
        out_specs=pl.BlockSpec((None, r, n), lambda q: (q, 0, 0)), out_shape=_sds((N_CHIP, r, n), BF16),
    )(grad.reshape(N_CHIP, 2, r, n), from_sibling)


def _gather_by_axes(pieces, name):
    n_arr = len(pieces)

    def body(*refs):
        srcs, dsts = refs[:n_arr], refs[n_arr:2 * n_arr]
        send_sems, recv_sems, local_sems = refs[2 * n_arr:]
        x, y, c, me = _my_position()
        sibling, x_nbr, y_nbr = (x, y, 1 - c), (1 - x, y, c), (x, 1 - y, c)

        def ident(px, py, pc):
            return 4 * px + 2 * py + pc

        def half(a, piece, hf):
            h = srcs[a].shape[0] // 2
            return dsts[a].at[pl.ds(piece * 2 * h + hf * h, h)]

        def copy(a, k, src, dst, to):
            return pltpu.make_async_remote_copy(src_ref=src, dst_ref=dst, send_sem=send_sems.at[a, k], recv_sem=recv_sems.at[a, k],
                                                device_id=to, device_id_type=MESH)

        def arrival(a, k, piece, hf, frm):
            return copy(a, k, half(a, piece, hf), half(a, piece, hf), frm)

        def pass_on(a, k, piece, hf, to):
            return copy(a, k, half(a, piece, hf), half(a, piece, hf), to)

        local, sent = [], []
        for a in range(n_arr):
            r = srcs[a].shape[0]
            h = r // 2
            whole = dsts[a].at[pl.ds(me * r, r)]
            own = [srcs[a].at[pl.ds(0, h)], srcs[a].at[pl.ds(h, h)]]
            local.append(pltpu.make_async_copy(srcs[a], whole, local_sems.at[a]))
            sent += [copy(a, 0, srcs[a], whole, sibling),
                     copy(a, 1, own[0], half(a, me, 0), x_nbr), copy(a, 2, own[1], half(a, me, 1), x_nbr),
                     copy(a, 4, own[1], half(a, me, 1), y_nbr), copy(a, 5, own[0], half(a, me, 0), y_nbr)]
        for cp in local + sent:
            cp.start()

        def then(cps):
            for cp in cps:
                cp.start()
            sent.extend(cps)

        id_x, id_y, id_d = ident(1 - x, y, c), ident(x, 1 - y, c), ident(1 - x, 1 - y, c)
        for a in range(n_arr):
            arrival(a, 4, id_y, 1, y_nbr).wait_recv()
            then([pass_on(a, 3, id_y, 1, x_nbr), pass_on(a, 7, id_y, 1, sibling)])
            arrival(a, 1, id_x, 0, x_nbr).wait_recv()
            then([pass_on(a, 6, id_x, 0, y_nbr), pass_on(a, 8, id_x, 0, sibling)])
            arrival(a, 2, id_x, 1, x_nbr).wait_recv()
            then([pass_on(a, 9, id_x, 1, sibling)])
            arrival(a, 5, id_y, 0, y_nbr).wait_recv()
            then([pass_on(a, 10, id_y, 0, sibling)])
            arrival(a, 3, id_d, 1, x_nbr).wait_recv()
            then([pass_on(a, 11, id_d, 1, sibling)])
            arrival(a, 6, id_d, 0, y_nbr).wait_recv()
            then([pass_on(a, 12, id_d, 0, sibling)])
        sx, sy, sd = ident(1 - x, y, 1 - c), ident(x, 1 - y, 1 - c), ident(1 - x, 1 - y, 1 - c)
        for a in range(n_arr):
            r = srcs[a].shape[0]
            rows = dsts[a].at[pl.ds(ident(x, y, 1 - c) * r, r)]
            copy(a, 0, rows, rows, sibling).wait_recv()
            for k, piece, hf in ((7, sy, 1), (8, sx, 0), (9, sx, 1), (10, sy, 0), (11, sd, 1), (12, sd, 0)):
                arrival(a, k, piece, hf, sibling).wait_recv()
        for cp in sent:
            cp.wait_send()
        for cp in local:
            cp.wait()

    n_sem = 13
    return pl.pallas_call(
        body, name=name, in_specs=[_HBM] * n_arr, out_specs=[_HBM] * n_arr,
        out_shape=[_sds((N_DEV * p.shape[0],) + p.shape[1:], p.dtype) for p in pieces],
        scratch_shapes=[pltpu.SemaphoreType.DMA((n_arr, n_sem)), pltpu.SemaphoreType.DMA((n_arr, n_sem)),
                        pltpu.SemaphoreType.DMA((n_arr,))],
    )(*pieces)


def _all_reduce_small(vec):
    r = vec.shape[0]

    def body(v_ref, o_ref, buf, send_sems, recv_sems):
        x, y, c, me = _my_position()
        buf[me] = v_ref[...]
        copies = []
        for mask in range(1, N_DEV):
            peer, _ = _peer(x, y, c, mask)
            cp = pltpu.make_async_remote_copy(src_ref=v_ref, dst_ref=buf.at[me], send_sem=send_sems.at[mask],
                                              recv_sem=recv_sems.at[mask], device_id=peer, device_id_type=MESH)
            cp.start()
            copies.append(cp)
        for cp in copies:
            cp.wait()
        total = buf[0]
        for s in range(1, N_DEV):
            total = total + buf[s]
        o_ref[...] = total

    vm = pl.BlockSpec(memory_space=pltpu.VMEM)
    return pl.pallas_call(
        body, name="all_reduce_small", in_specs=[vm], out_specs=vm, out_shape=_sds(vec.shape, F32),
        scratch_shapes=[pltpu.VMEM((N_DEV, r, 128), F32), pltpu.SemaphoreType.DMA((N_DEV,)), pltpu.SemaphoreType.DMA((N_DEV,))],
    )(vec)


def _adamw_math(w, g, m, v):
    m = ADAM_B1 * m + (1.0 - ADAM_B1) * g
    v = ADAM_B2 * v + (1.0 - ADAM_B2) * (g * g)
    m_hat = m / (1.0 - ADAM_B1 ** ADAM_STEP)
    v_hat = v / (1.0 - ADAM_B2 ** ADAM_STEP)
    delta = -ADAM_LR * (m_hat / (jnp.sqrt(v_hat) + ADAM_EPS) + ADAM_WD * w)
    return delta, m, v


def _reduce_adamw(slots, w, m, v, *, transpose, name):
    r, n = slots.shape[1:]
    padded = not transpose and w.shape[0] != r

    def body(s_ref, w_ref, m_ref, v_ref, g_ref, d_ref, nm_ref, nv_ref, *scr):
        g = s_ref[0].astype(F32)
        for s in range(1, slots.shape[0]):
            g = g + s_ref[s].astype(F32)
        if transpose:
            scr[0][...] = g.T
            g = scr[0][:, 0:w_ref.shape[1]]
        elif padded:
            scr[0][...] = g
            g = scr[0][0:w_ref.shape[0], :]
        g_ref[...] = g
        d_ref[...], nm_ref[...], nv_ref[...] = _adamw_math(w_ref[...], g, m_ref[...], v_ref[...])

    out = _sds(w.shape, F32)
    if not transpose and not padded and r % (2 * BF16_ROW_TILE) == 0:
        half = pl.BlockSpec((r // 2, n), lambda i: (i, 0))
        return pl.pallas_call(
            body, name=name, grid=(2,), in_specs=[pl.BlockSpec((slots.shape[0], r // 2, n), lambda i: (0, i, 0)), half, half, half],
            out_specs=[half] * 4, out_shape=[out, out, out, out])(slots, w, m, v)
    return pl.pallas_call(
        body, name=name, out_shape=[out, out, out, out],
        scratch_shapes=[pltpu.VMEM((n, r) if transpose else (r, n), F32)] if transpose or padded else [],
    )(slots, w, m, v)


def _adamw(g, w, m, v, name):
    def body(g_ref, w_ref, m_ref, v_ref, d_ref, nm_ref, nv_ref):
        d_ref[...], nm_ref[...], nv_ref[...] = _adamw_math(w_ref[...], g_ref[...], m_ref[...], v_ref[...])

    out = _sds(w.shape, F32)
    return pl.pallas_call(body, name=name, out_shape=[out, out, out])(g, w, m, v)


_SMALL = ["ffn1_norm", "mix_norm", "ffn2_norm", "mla_q_lat_norm", "mla_kv_lat_norm", "mla_q_nope_gain", "mla_q_rope_gain",
          "mla_k_nope_gain", "mla_k_rope_gain", "fox_q_gain", "fox_k_gain", "fox_b_f"]
_WEIGHTS = ["ffn1_norm", "ffn1_w_gate", "ffn1_w_up", "ffn1_w_down", "mix_norm", "w_in", "mla_q_lat_norm", "mla_w_qb",
            "mla_kv_lat_norm", "mla_w_kvb", "mla_q_nope_gain", "mla_q_rope_gain", "mla_k_nope_gain", "mla_k_rope_gain",
            "fox_q_gain", "fox_k_gain", "fox_b_f", "w_branch_mla", "w_branch_fox", "b_gate", "w_o", "ffn2_norm",
            "ffn2_w_gate", "ffn2_w_up", "ffn2_w_down"]
_IN_Q, _IN_KV, _IN_KR, _IN_FOX, _IN_F, _IN_GATES = (0, 192), (192, 128), (320, 32), (352, 1536), (1888, 8), (1896, 2048)


def _rows(a, seg):
    return a[seg[0]:seg[0] + seg[1]]


def _split_w_in(win_t):
    z = lambda n: jnp.zeros((n, D_MODEL), win_t.dtype)
    lat = jnp.concatenate([_rows(win_t, _IN_Q), z(LAT_KV - Q_LORA), _rows(win_t, _IN_KV), _rows(win_t, _IN_KR),
                           _rows(win_t, _IN_F), z(LAT_W - LAT_F - N_HEADS)], axis=0)
    return _rows(win_t, _IN_GATES), _rows(win_t, _IN_FOX), lat


def _join_w_in(d_gates, d_fox, d_lat):
    return jnp.concatenate([d_lat[LAT_Q:LAT_Q + Q_LORA], d_lat[LAT_KV:LAT_KV + KV_LORA], d_lat[LAT_KR:LAT_KR + ROPE_DIM],
                            d_fox, d_lat[LAT_F:LAT_F + N_HEADS], d_gates], axis=0)


def kernel(x, positions, ffn1_norm, ffn1_w_gate, ffn1_w_up, ffn1_w_down, mix_norm, w_in, mla_q_lat_norm, mla_w_qb, mla_kv_lat_norm, mla_w_kvb, mla_q_nope_gain, mla_q_rope_gain, mla_k_nope_gain, mla_k_rope_gain, fox_q_gain, fox_k_gain, fox_b_f, w_branch_mla, w_branch_fox, b_gate, w_o, ffn2_norm, ffn2_w_gate, ffn2_w_up, ffn2_w_down, loss_target, m_ffn1_norm, m_ffn1_w_gate, m_ffn1_w_up, m_ffn1_w_down, m_mix_norm, m_w_in, m_mla_q_lat_norm, m_mla_w_qb, m_mla_kv_lat_norm, m_mla_w_kvb, m_mla_q_nope_gain, m_mla_q_rope_gain, m_mla_k_nope_gain, m_mla_k_rope_gain, m_fox_q_gain, m_fox_k_gain, m_fox_b_f, m_w_branch_mla, m_w_branch_fox, m_b_gate, m_w_o, m_ffn2_norm, m_ffn2_w_gate, m_ffn2_w_up, m_ffn2_w_down, v_ffn1_norm, v_ffn1_w_gate, v_ffn1_w_up, v_ffn1_w_down, v_mix_norm, v_w_in, v_mla_q_lat_norm, v_mla_w_qb, v_mla_kv_lat_norm, v_mla_w_kvb, v_mla_q_nope_gain, v_mla_q_rope_gain, v_mla_k_nope_gain, v_mla_k_rope_gain, v_fox_q_gain, v_fox_k_gain, v_fox_b_f, v_w_branch_mla, v_w_branch_fox, v_b_gate, v_w_o, v_ffn2_norm, v_ffn2_w_gate, v_ffn2_w_up, v_ffn2_w_down):
    env = dict(locals())
    strip = lambda n, a: a if n in _SMALL else a[0]
    W = {n: strip(n, env[n]) for n in _WEIGHTS}
    M = {n: strip(n, env["m_" + n]) for n in _WEIGHTS}
    V = {n: strip(n, env["v_" + n]) for n in _WEIGHTS}
    xs = x[0]
    t = xs.shape[0]

    col_split = ["ffn1_w_gate", "ffn1_w_up", "ffn2_w_gate", "ffn2_w_up", "mla_w_qb", "mla_w_kvb", "w_branch_mla", "w_branch_fox"]
    row_split = ["ffn1_w_down", "ffn2_w_down", "w_o"]
    pieces = {n: W[n].T.astype(BF16) for n in col_split}
    pieces.update({n: W[n].astype(BF16) for n in row_split})
    pieces["w_in"] = jnp.pad(W["w_in"].T.astype(BF16), ((0, W_IN_PIECE_PAD - W_IN_PIECE), (0, 0)))
    pieces["b_gate"] = W["b_gate"].T
    group_a = ["ffn1_w_gate", "ffn1_w_up", "ffn1_w_down"]
    group_b = ["w_in", "mla_w_qb", "mla_w_kvb", "w_branch_mla", "w_branch_fox", "w_o", "b_gate"]
    group_c = ["ffn2_w_gate", "ffn2_w_up", "ffn2_w_down"]
    gather = lambda group: _ChipExchange(True, [pieces[n] for n in group])
    G = dict(zip(group_a, _gather_by_axes([pieces[n] for n in group_a], "gather_a")))

    inv_freq = ROPE_THETA ** (-jnp.arange(ROPE_HALF, dtype=F32) / ROPE_HALF)
    ang = positions[0].astype(F32)[:, None] * inv_freq
    tables, mats = _prep_tables(jnp.cos(ang), jnp.sin(ang)), _group_matrices()
    gains = _head_gains(*[W[n] for n in ["mla_q_nope_gain", "mla_q_rope_gain", "mla_k_nope_gain", "mla_k_rope_gain",
                                         "fox_q_gain", "fox_k_gain"]])

    (x1, a1, b1), got_b = _ffn_fwd(xs, W["ffn1_norm"], G["ffn1_w_gate"], G["ffn1_w_up"], G["ffn1_w_down"],
                                   exchange=gather(group_b), then_to_sibling=True)
    G.update(zip(group_b, got_b))
    win_t = G["w_in"].reshape(N_DEV, W_IN_PIECE_PAD, D_MODEL)[:, :W_IN_PIECE].reshape(N_DEV * W_IN_PIECE, D_MODEL)
    wgate_t, wfox_t, wlat_t = _split_w_in(win_t)
    bg = G["b_gate"].T
    gates, fox, lat = _proj_fwd(x1, W["mix_norm"], wgate_t, wfox_t, wlat_t)
    prep_args = (fox, lat, tables, mats, *_interleave_weights(G["mla_w_qb"], G["mla_w_kvb"]), W["mla_q_lat_norm"],
                 W["mla_kv_lat_norm"], gains, W["fox_b_f"])
    fq, fk, fv, c, mq, mk, mv, fq_sq, fk_sq_max, mq_sq, mk_sq_max, fv1, mv1 = _prep_fwd(*prep_args, *_value_layout(prep_args[6]))
    c_t = c.T
    b_fox, ok_fox = _logit_bound(fq_sq, fk_sq_max)
    b_mla, ok_mla = _logit_bound(mq_sq, mk_sq_max)
    (y_fox, y_fox32, lse_fox), got_c = _attn_fwd(fq, fk, fv1, b_fox, ok_fox, c, c_t, lanes=HEAD_DIM, name="fox_fwd",
                                                 exchange=gather(group_c))
    (y_mla, y_mla32, lse_mla), got_c = _attn_fwd(mq, mk, mv1, b_mla, ok_mla, None, None, lanes=MLA_QK_LANES, name="mla_fwd",
                                                 exchange=_CoreExchange(True, got_c))
    G.update(zip(group_c, got_c))
    x2 = _mix_fwd(x1, y_mla, y_fox, gates, bg, G["w_branch_mla"], G["w_branch_fox"], G["w_o"])
    (dx3, a2, b2, loss_vec), _ = _ffn_fwd(x2, W["ffn2_norm"], G["ffn2_w_gate"], G["ffn2_w_up"], G["ffn2_w_down"],
                                          target=loss_target[0])

    def chip_sums(group, tag, from_sibling=None):
        if from_sibling is None:
            from_sibling = _grads_to_sibling([grads[n] for n in group], "grads_d2d_" + tag)
        return _ChipExchange(False, [_pair_sum(grads[n], s, "pair_sum_" + n) for n, s in zip(group, from_sibling)])

    (dx2, dg_ffn2, da2, db2, h2, n2, dyh2), _ = _ffn_bwd(dx3, x2, W["ffn2_norm"], a2, b2, G["ffn2_w_gate"], G["ffn2_w_up"],
                                                        G["ffn2_w_down"], "ffn2_bwd")
    grads = {"ffn2_w_gate": _tn_matmul(da2, n2, "ffn2_dgate"), "ffn2_w_up": _tn_matmul(db2, n2, "ffn2_dup"),
             "ffn2_w_down": _tn_matmul(h2, dyh2, "ffn2_ddown")}
    (dy_mla, dy_fox, delta_mla, delta_fox, dgates, mixed, dum, duf, dx2b, dbg), from_sibling_c = _mix_bwd(
        dx2, y_mla, y_fox, y_mla32, y_fox32, gates, bg, G["w_branch_mla"], G["w_branch_fox"], G["w_o"], mats[3],
        exchange=_CoreExchange(False, [grads[n] for n in group_c]))
    delta_mla, delta_fox = delta_mla[:, :N_HEADS], delta_fox[:, :N_HEADS]
    grads["w_o"] = _tn_matmul(mixed, dx2b, "d_w_o")
    grads["w_branch_mla"] = _tn_matmul(dum, y_mla, "d_w_branch_mla")
    grads["w_branch_fox"] = _tn_matmul(duf, y_fox, "d_w_branch_fox")
    (dfq, dfk, dfv, dc_t), slots_c = _attn_bwd(fq, fk, fv, dy_fox, delta_fox, lse_fox, c, c_t, lanes=HEAD_DIM,
                                               name="fox_bwd", exchange=chip_sums(group_c, "c", from_sibling_c))
    slots = dict(zip(group_c, slots_c))
    (dmq, dmk, dmv), _ = _attn_bwd(mq, mk, mv, dy_mla, delta_mla, lse_mla, None, None, lanes=MLA_QK_LANES, name="mla_bwd")
    dfox, dlat, dwq, dwk, dwv, d_gql, d_gkvl, d_gq, d_gk, d_gkr, d_gfq, d_gfk, d_bf = _prep_bwd(
        *prep_args, dfq, dfk, dfv, dc_t.T, dmq, dmk, dmv)
    grads["mla_w_qb"], grads["mla_w_kvb"] = _deinterleave_grads(dwq, dwk, dwv)
    fold = lambda a, width: a.reshape(N_HEADS, width).sum(axis=0)[None]
    d_gq, d_gk = fold(d_gq, LANES), fold(d_gk, LANES)
    d_prep_small = [d_gql, d_gkvl, d_gq[:, :HEAD_DIM], d_gq[:, HEAD_DIM:HEAD_DIM + ROPE_DIM], d_gk[:, :HEAD_DIM],
                    d_gkr[:, HEAD_DIM:HEAD_DIM + ROPE_DIM], fold(d_gfq, HEAD_DIM), fold(d_gfk, HEAD_DIM), d_bf]
    dx1, dg_mix, nmix = _proj_bwd(dgates, dfox, dlat, wgate_t, wfox_t, wlat_t, x1, W["mix_norm"], dx2)
    dwin_t = _join_w_in(_tn_matmul(dgates, nmix, "d_w_in_gates"), _tn_matmul(dfox, nmix, "d_w_in_fox"),
                        _tn_matmul(dlat, nmix, "d_w_in_lat"))
    grads["w_in"] = jnp.pad(dwin_t.reshape(N_DEV, W_IN_PIECE, D_MODEL), ((0, 0), (0, W_IN_PIECE_PAD - W_IN_PIECE), (0, 0))
                            ).reshape(N_DEV * W_IN_PIECE_PAD, D_MODEL)
    grad_group_b = [n for n in group_b if n != "b_gate"]
    (dx0, dg_ffn1, da1, db1, h1, n1, dyh1), slots_b = _ffn_bwd(dx1, xs, W["ffn1_norm"], a1, b1, G["ffn1_w_gate"], G["ffn1_w_up"],
                                                              G["ffn1_w_down"], "ffn1_bwd", exchange=chip_sums(grad_group_b, "b"))
    slots.update(zip(grad_group_b, slots_b))
    grads["ffn1_w_gate"] = _tn_matmul(da1, n1, "ffn1_dgate")
    grads["ffn1_w_up"], got = _tn_matmul(db1, n1, "ffn1_dup", exchange=chip_sums(["ffn1_w_gate"], "a_gate"))
    slots["ffn1_w_gate"] = got[0]
    grads["ffn1_w_down"], got = _tn_matmul(h1, dyh1, "ffn1_ddown", exchange=chip_sums(["ffn1_w_up"], "a_up"))
    slots["ffn1_w_up"] = got[0]
    slots["ffn1_w_down"] = chip_sums(["ffn1_w_down"], "a_down").standalone("grads_ici_a_down")[0]

    small_parts = [dg_ffn1, dg_mix, dg_ffn2] + list(d_prep_small) + [dbg.reshape(1, 2 * D_MODEL), loss_vec]
    flat = jnp.concatenate([p.reshape(-1) for p in small_parts])
    n_flat = flat.shape[0]
    rows = -(-n_flat // (8 * 128)) * 8
    total = _all_reduce_small(jnp.pad(flat, (0, rows * 128 - n_flat)).reshape(rows, 128)).reshape(-1)
    offs, small_g = 0, {}
    for n in _SMALL:
        small_g[n] = total[offs:offs + W[n].shape[1]].reshape(W[n].shape)
        offs += W[n].shape[1]
    bg_full = total[offs:offs + 2 * D_MODEL].reshape(2, D_MODEL)
    offs += 2 * D_MODEL
    loss = (0.5 / D_MODEL) * jnp.sum(total[offs:offs + D_MODEL])
    _, _, _, me = _my_position()
    small_g["b_gate"] = lax.dynamic_slice_in_dim(bg_full, me * (D_MODEL // N_DEV), D_MODEL // N_DEV, axis=1)

    transposed_in_memory = ["ffn1_w_gate", "ffn1_w_up", "ffn2_w_gate", "ffn2_w_up", "w_in"]
    res = {}
    for n in _WEIGHTS:
        if n in small_g:
            res[n] = (small_g[n],) + tuple(_adamw(small_g[n], W[n], M[n], V[n], "adamw_" + n))
        elif n in transposed_in_memory:
            res[n] = tuple(o.T for o in _reduce_adamw(slots[n], W[n].T, M[n].T, V[n].T, transpose=False, name="adamw_" + n))
        else:
            res[n] = tuple(_reduce_adamw(slots[n], W[n], M[n], V[n], transpose=n in col_split, name="adamw_" + n))
    outs = [loss, dx0[None]]
    for k in range(4):
        outs += [res[n][k] if n in _SMALL else res[n][k][None] for n in _WEIGHTS]
    return tuple(outs)
```
